```python
import jax
import jax.numpy as jnp
from jax import lax
import numpy as np

D_MODEL = 1024
BATCH = 8
SEQ = 4096
DEPTH = 2
DEC_BATCH = 8
DEC_SEQ = 64
PAST_LEN = 4096

CHUNK = 64
N_META = 16
Q_BLOCK = 128
N_EVEN = (DEPTH + 1) // 2
N_ODD = DEPTH // 2
HG_HEADS = 4
HG_DK = 128
HG_DV = 128
HG_QK = HG_HEADS * HG_DK
HG_VO = HG_HEADS * HG_DV
FOX_HEADS = 4
FOX_DH = 128
FOX_W = FOX_HEADS * FOX_DH
FOX_FORGET_INIT = 2.0
EVEN_IN = 2 * HG_QK + 2 * HG_VO + 3 * FOX_W + FOX_HEADS
EVEN_OUT = HG_VO + FOX_W
MLA_HEADS = 8
MLA_Q_LORA = 512
MLA_KV_LORA = 256
MLA_NOPE = 128
MLA_ROPE = 64
MLA_V = 128
ODD_IN = MLA_Q_LORA + MLA_KV_LORA + MLA_ROPE
ROPE_BASE = 10000.0
N_EXPERTS = 16
N_GROUPS = 4
EXPERTS_PER_GROUP = N_EXPERTS // N_GROUPS
TOP_K = 2
D_EXPERT = 256
ALPHA = (2 * DEPTH) ** 0.25
BETA = (8 * DEPTH) ** -0.25
LN_EPS = 1e-5
RMS_EPS = 1e-6

kernel_name = 'hybrid_streaming_encoder_step'


def layer_norm(x, g, b):
    xf = x.astype(jnp.float32)
    mu = jnp.mean(xf, axis=-1, keepdims=True)
    var = jnp.mean(jnp.square(xf - mu), axis=-1, keepdims=True)
    return ((xf - mu) * lax.rsqrt(var + LN_EPS) * g + b).astype(x.dtype)


def rms_norm(x, g):
    xf = x.astype(jnp.float32)
    return (xf * lax.rsqrt(jnp.mean(xf * xf, axis=-1, keepdims=True) + RMS_EPS) * g).astype(x.dtype)


def rope(x, pos):
    half = MLA_ROPE // 2
    inv = ROPE_BASE ** (-jnp.arange(half, dtype=jnp.float32) / half)
    ang = pos.astype(jnp.float32)[:, None] * inv[None, :]
    shape = (pos.shape[0],) + (1,) * (x.ndim - 3) + (half,)
    cos = jnp.cos(ang).reshape(shape)
    sin = jnp.sin(ang).reshape(shape)
    xf = x.astype(jnp.float32)
    x1, x2 = xf[..., :half], xf[..., half:]
    return jnp.concatenate([x1 * cos - x2 * sin, x1 * sin + x2 * cos], axis=-1).astype(x.dtype)


def block_attention(q, k, v, grp_q, grp_k, scale, bias_q=None, bias_k=None):
    B, Tq, H, Dk = q.shape
    qb = min(Q_BLOCK, Tq)
    nb = -(-Tq // qb)
    pad = nb * qb - Tq
    has_bias = bias_q is not None
    q_blocks = jnp.swapaxes(jnp.pad(q, ((0, 0), (0, pad), (0, 0), (0, 0))).reshape(B, nb, qb, H, Dk), 0, 1)
    g_blocks = jnp.pad(grp_q, (0, pad), constant_values=jnp.iinfo(jnp.int32).max).reshape(nb, qb)
    if has_bias:
        b_blocks = jnp.swapaxes(jnp.pad(bias_q, ((0, 0), (0, pad), (0, 0))).reshape(B, nb, qb, H), 0, 1)
        bias_kt = jnp.swapaxes(bias_k, 1, 2)[:, :, None, :]
        xs = (q_blocks, g_blocks, b_blocks)
    else:
        xs = (q_blocks, g_blocks)

    def one(blk):
        qi, gi = blk[0], blk[1]
        s = jnp.einsum('bqhd,bkhd->bhqk', qi, k).astype(jnp.float32) * scale
        if has_bias:
            s = s + jnp.swapaxes(blk[2], 1, 2)[..., None] - bias_kt
        visible = (grp_k[None, :] <= gi[:, None])[None, None]
        p = jax.nn.softmax(jnp.where(visible, s, -jnp.inf), axis=-1).astype(v.dtype)
        return jnp.einsum('bhqk,bkhd->bqhd', p, v)

    o = lax.map(one, xs)
    return jnp.swapaxes(o, 0, 1).reshape(B, nb * qb, H, -1)[:, :Tq]


def hgrn2_scan(q, lf, k, v, s0, chunk):
    B, T, H, DK = q.shape
    n = T // chunk

    def blocks(a):
        return jnp.moveaxis(a.reshape(B, n, chunk, H, a.shape[-1]), 1, 0)

    tri = jnp.tril(jnp.ones((chunk, chunk), dtype=bool))[None, :, :, None, None]

    def step(S, blk):
        qc, lfc, kc, vc = blk
        vc = vc.astype(jnp.float32)
        b = jnp.cumsum(lfc, axis=1)
        decay = jnp.exp(jnp.where(tri, b[:, :, None] - b[:, None, :], -jnp.inf))
        A = jnp.einsum('btshd,bshd->bhts', decay * qc[:, :, None].astype(jnp.float32), kc)
        o = jnp.einsum('bhts,bshv->bthv', A, vc) + jnp.einsum('bthd,bhdv->bthv', qc * jnp.exp(b), S)
        b_last = b[:, -1]
        S_new = jnp.exp(b_last)[..., None] * S + jnp.einsum('bshd,bshv->bhdv', kc * jnp.exp(b_last[:, None] - b), vc)
        return S_new, o

    S, o = lax.scan(step, s0.astype(jnp.float32), (blocks(q), blocks(lf), blocks(k), blocks(v)))
    return jnp.moveaxis(o, 0, 1).reshape(B, T, H, -1).astype(q.dtype), S


def hgrn2_segments(q, lf, k, v, s0, segments):
    outs = []
    s = s0
    start = 0
    for length, chunk in segments:
        sl = slice(start, start + length)
        o, s = hgrn2_scan(q[:, sl], lf[:, sl], k[:, sl], v[:, sl], s, chunk)
        outs.append(o)
        start += length
    return jnp.concatenate(outs, axis=1), s


def even_mixer(x, hg_s0, fox_k_past, fox_v_past, fox_lf_past, pos_q, pos_k, segments,
               lb, w_in, hg_norm_g, fox_fb, w_out):
    B, T, _ = x.shape
    z = jnp.einsum('btd,de->bte', x, w_in)
    c0 = HG_QK
    c1 = c0 + HG_QK
    c2 = c1 + HG_VO
    c3 = c2 + HG_VO
    c4 = c3 + FOX_W
    c5 = c4 + FOX_W
    c6 = c5 + FOX_W
    hq, hf, hi, hg, fq, fk, fv, ffg = jnp.split(z, [c0, c1, c2, c3, c4, c5, c6], axis=-1)
    zf = hf.astype(jnp.float32)
    lf_hg = jnp.log(lb + (1.0 - lb) * jax.nn.sigmoid(zf))
    k_hg = (1.0 - lb) * jax.nn.sigmoid(-zf)
    hd = lambda a: a.reshape(B, T, HG_HEADS, -1)
    o_hg, s_new = hgrn2_segments(hd(hq), hd(lf_hg), hd(k_hg), hd(hi), hg_s0, segments)
    o_hg = rms_norm(o_hg, hg_norm_g).reshape(B, T, HG_VO) * jax.nn.sigmoid(hg)
    fh = lambda a: a.reshape(B, T, FOX_HEADS, FOX_DH)
    q_f, k_new, v_new = fh(fq), fh(fk), fh(fv)
    lf_new = jax.nn.log_sigmoid((ffg + fox_fb).astype(jnp.float32))
    if fox_k_past is None:
        k_all, v_all, lf_all = k_new, v_new, lf_new
    else:
        k_all = jnp.concatenate([fox_k_past.astype(k_new.dtype), k_new], axis=1)
        v_all = jnp.concatenate([fox_v_past.astype(v_new.dtype), v_new], axis=1)
        lf_all = jnp.concatenate([fox_lf_past.astype(jnp.float32), lf_new], axis=1)
    F = jnp.cumsum(lf_all, axis=1)
    o_fox = block_attention(q_f, k_all, v_all, pos_q, pos_k, FOX_DH ** -0.5, F[:, -T:], F)
    mixed = jnp.concatenate([o_hg, o_fox.reshape(B, T, FOX_W).astype(o_hg.dtype)], axis=-1)
    y = jnp.einsum('bte,ed->btd', mixed, w_out)
    return y, s_new, k_new, v_new, lf_new


def odd_mixer(x, ckv_past, kpe_past, pos_q, grp_q, grp_k,
              w_in, q_norm_g, kv_norm_g, w_uq, w_uk, w_uv, w_out):
    B, T, _ = x.shape
    z = jnp.einsum('btd,de->bte', x, w_in)
    cq, ckv, kpe = jnp.split(z, [MLA_Q_LORA, MLA_Q_LORA + MLA_KV_LORA], axis=-1)
    cq = rms_norm(cq, q_norm_g)
    ckv = rms_norm(ckv, kv_norm_g)
    kpe = rope(kpe, pos_q)
    q = jnp.einsum('btc,ce->bte', cq, w_uq).reshape(B, T, MLA_HEADS, MLA_NOPE + MLA_ROPE)
    q = jnp.concatenate([q[..., :MLA_NOPE], rope(q[..., MLA_NOPE:], pos_q)], axis=-1)
    if ckv_past is None:
        ckv_all, kpe_all = ckv, kpe
    else:
        ckv_all = jnp.concatenate([ckv_past.astype(ckv.dtype), ckv], axis=1)
        kpe_all = jnp.concatenate([kpe_past.astype(kpe.dtype), kpe], axis=1)
    k_nope = jnp.einsum('bsc,chd->bshd', ckv_all, w_uk)
    v = jnp.einsum('bsc,chd->bshd', ckv_all, w_uv)
    k_rot = jnp.broadcast_to(kpe_all[:, :, None, :], k_nope.shape[:3] + (MLA_ROPE,)).astype(k_nope.dtype)
    k = jnp.concatenate([k_nope, k_rot], axis=-1)
    o = block_attention(q.astype(k.dtype), k, v, grp_q, grp_k, (MLA_NOPE + MLA_ROPE) ** -0.5)
    y = jnp.einsum('bte,ed->btd', o.reshape(B, T, MLA_HEADS * MLA_V), w_out)
    return y, ckv, kpe


def moe(x, router_w, router_bias, w_gate, w_up, w_down):
    B, T, D = x.shape
    scores = jax.nn.sigmoid(jnp.einsum('btd,de->bte', x, router_w).astype(jnp.float32))
    grouped = (scores + router_bias.astype(jnp.float32)).reshape(B, T, N_GROUPS, EXPERTS_PER_GROUP)
    group_score = jnp.sum(lax.top_k(grouped, TOP_K)[0], axis=-1)
    best = jnp.argmax(group_score, axis=-1)
    in_group = (jnp.arange(N_GROUPS) == best[..., None])[..., None]
    masked = jnp.where(in_group, grouped, -jnp.inf).reshape(B, T, N_EXPERTS)
    _, idx = lax.top_k(masked, TOP_K)
    w = jnp.take_along_axis(scores, idx, axis=-1)
    w = w / jnp.sum(w, axis=-1, keepdims=True)
    combine = jnp.sum(jax.nn.one_hot(idx, N_EXPERTS, dtype=jnp.float32) * w[..., None], axis=-2).astype(x.dtype)
    y = jnp.zeros_like(x)
    for e in range(N_EXPERTS):
        h = jax.nn.silu(x @ w_gate[e]) * (x @ w_up[e])
        y = y + (h @ w_down[e]).astype(x.dtype) * combine[..., e:e + 1]
    return y


def trunk(x, hg_s0, fox_k_c, fox_v_c, fox_lf_c, mla_ckv_c, mla_kpe_c, pos_q, pos_k, grp_q, grp_k, segments, p):
    lb_table = jnp.cumsum(jax.nn.softmax(p['hg_lb_logits'].astype(jnp.float32), axis=0), axis=0)
    hg_new, fk_new, fv_new, flf_new, ckv_new, kpe_new = [], [], [], [], [], []
    for l in range(DEPTH):
        if l % 2 == 0:
            e = l // 2
            if fox_k_c is None:
                past = (None, None, None)
            else:
                past = (fox_k_c[e], fox_v_c[e], fox_lf_c[e])
            y, s, kr, vr, lr = even_mixer(x, hg_s0[e], past[0], past[1], past[2], pos_q, pos_k, segments,
                                          lb_table[l], p['even_w_in'][e], p['hg_norm_g'][e],
                                          p['fox_forget_bias'][e], p['even_w_out'][e])
            hg_new.append(s)
            fk_new.append(kr)
            fv_new.append(vr)
            flf_new.append(lr)
        else:
            o = l // 2
            if mla_ckv_c is None:
                past = (None, None)
            else:
                past = (mla_ckv_c[o], mla_kpe_c[o])
            y, c, kp = odd_mixer(x, past[0], past[1], pos_q, grp_q, grp_k,
                                 p['mla_w_in'][o], p['mla_q_norm_g'][o], p['mla_kv_norm_g'][o],
                                 p['mla_w_uq'][o], p['mla_w_uk'][o], p['mla_w_uv'][o], p['mla_w_out'][o])
            ckv_new.append(c)
            kpe_new.append(kp)
        x = layer_norm(ALPHA * x + y.astype(x.dtype), p['ln_mix_g'][l], p['ln_mix_b'][l])
        f = moe(x, p['router_w'], p['router_bias'], p['moe_w_gate'][l], p['moe_w_up'][l], p['moe_w_down'][l])
        x = layer_norm(ALPHA * x + f, p['ln_ffn_g'][l], p['ln_ffn_b'][l])
    return (x, jnp.stack(hg_new), jnp.stack(fk_new), jnp.stack(fv_new), jnp.stack(flf_new),
            jnp.stack(ckv_new), jnp.stack(kpe_new))


def setup_inputs(seed: int = 0) -> dict:
    key = jax.random.key(seed)
    ks = jax.random.split(key, 32)
    nrm = lambda k, shape, scale: jax.random.normal(k, shape, jnp.float32) * scale
    return {
        'x_prompt': nrm(ks[0], (BATCH, SEQ, D_MODEL), 1.0),
        'x_sample': nrm(ks[1], (DEC_BATCH, DEC_SEQ, D_MODEL), 1.0),
        'state_hgrn2': nrm(ks[2], (N_EVEN, DEC_BATCH, HG_HEADS, HG_DK, HG_DV), 0.5),
        'cache_fox_k': nrm(ks[3], (N_EVEN, DEC_BATCH, PAST_LEN, FOX_HEADS, FOX_DH), 1.0),
        'cache_fox_v': nrm(ks[4], (N_EVEN, DEC_BATCH, PAST_LEN, FOX_HEADS, FOX_DH), 1.0),
        'cache_fox_logf': jax.nn.log_sigmoid(FOX_FORGET_INIT + nrm(ks[5], (N_EVEN, DEC_BATCH, PAST_LEN, FOX_HEADS), 1.0)),
        'cache_mla_ckv': nrm(ks[6], (N_ODD, DEC_BATCH, PAST_LEN, MLA_KV_LORA), 1.0),
        'cache_mla_kpe': nrm(ks[7], (N_ODD, DEC_BATCH, PAST_LEN, MLA_ROPE), 1.0),
        'meta_tokens': nrm(ks[8], (N_META, D_MODEL), 1.0),
        'even_w_in': nrm(ks[9], (N_EVEN, D_MODEL, EVEN_IN), D_MODEL ** -0.5),
        'hg_lb_logits': nrm(ks[10], (DEPTH + 1, HG_QK), 0.5),
        'hg_norm_g': 1.0 + nrm(ks[11], (N_EVEN, HG_HEADS, HG_DV), 0.05),
        'fox_forget_bias': FOX_FORGET_INIT + nrm(ks[12], (N_EVEN, FOX_HEADS), 0.1),
        'even_w_out': nrm(ks[13], (N_EVEN, EVEN_OUT, D_MODEL), BETA * EVEN_OUT ** -0.5),
        'mla_w_in': nrm(ks[14], (N_ODD, D_MODEL, ODD_IN), D_MODEL ** -0.5),
        'mla_q_norm_g': 1.0 + nrm(ks[15], (N_ODD, MLA_Q_LORA), 0.05),
        'mla_kv_norm_g': 1.0 + nrm(ks[16], (N_ODD, MLA_KV_LORA), 0.05),
        'mla_w_uq': nrm(ks[17], (N_ODD, MLA_Q_LORA, MLA_HEADS * (MLA_NOPE + MLA_ROPE)), MLA_Q_LORA ** -0.5),
        'mla_w_uk': nrm(ks[18], (N_ODD, MLA_KV_LORA, MLA_HEADS, MLA_NOPE), MLA_KV_LORA ** -0.5),
        'mla_w_uv': nrm(ks[19], (N_ODD, MLA_KV_LORA, MLA_HEADS, MLA_V), MLA_KV_LORA ** -0.5),
        'mla_w_out': nrm(ks[20], (N_ODD, MLA_HEADS * MLA_V, D_MODEL), BETA * (MLA_HEADS * MLA_V) ** -0.5),
        'ln_mix_g': 1.0 + nrm(ks[21], (DEPTH, D_MODEL), 0.05),
        'ln_mix_b': nrm(ks[22], (DEPTH, D_MODEL), 0.02),
        'ln_ffn_g': 1.0 + nrm(ks[23], (DEPTH, D_MODEL), 0.05),
        'ln_ffn_b': nrm(ks[24], (DEPTH, D_MODEL), 0.02),
        'router_w': nrm(ks[25], (D_MODEL, N_EXPERTS), D_MODEL ** -0.5),
        'router_bias': nrm(ks[26], (N_EXPERTS,), 0.01),
        'moe_w_gate': nrm(ks[27], (DEPTH, N_EXPERTS, D_MODEL, D_EXPERT), D_MODEL ** -0.5),
        'moe_w_up': nrm(ks[28], (DEPTH, N_EXPERTS, D_MODEL, D_EXPERT), D_MODEL ** -0.5),
        'moe_w_down': nrm(ks[29], (DEPTH, N_EXPERTS, D_EXPERT, D_MODEL), BETA * D_EXPERT ** -0.5),
    }


def reference(x_prompt, x_sample, state_hgrn2, cache_fox_k, cache_fox_v, cache_fox_logf, cache_mla_ckv, cache_mla_kpe,
              meta_tokens, even_w_in, hg_lb_logits, hg_norm_g, fox_forget_bias, even_w_out,
              mla_w_in, mla_q_norm_g, mla_kv_norm_g, mla_w_uq, mla_w_uk, mla_w_uv, mla_w_out,
              ln_mix_g, ln_mix_b, ln_ffn_g, ln_ffn_b, router_w, router_bias, moe_w_gate, moe_w_up, moe_w_down):
    p = {
        'even_w_in': even_w_in, 'hg_lb_logits': hg_lb_logits, 'hg_norm_g': hg_norm_g,
        'fox_forget_bias': fox_forget_bias, 'even_w_out': even_w_out,
        'mla_w_in': mla_w_in, 'mla_q_norm_g': mla_q_norm_g, 'mla_kv_norm_g': mla_kv_norm_g,
        'mla_w_uq': mla_w_uq, 'mla_w_uk': mla_w_uk, 'mla_w_uv': mla_w_uv, 'mla_w_out': mla_w_out,
        'ln_mix_g': ln_mix_g, 'ln_mix_b': ln_mix_b, 'ln_ffn_g': ln_ffn_g, 'ln_ffn_b': ln_ffn_b,
        'router_w': router_w, 'router_bias': router_bias,
        'moe_w_gate': moe_w_gate, 'moe_w_up': moe_w_up, 'moe_w_down': moe_w_down,
    }
    B, T = x_prompt.shape[0], x_prompt.shape[1]
    meta = jnp.broadcast_to(meta_tokens[None].astype(x_prompt.dtype), (B, N_META, D_MODEL))
    xp = jnp.concatenate([meta, x_prompt], axis=1)
    pos_p = jnp.arange(N_META + T, dtype=jnp.int32)
    grp_p = jnp.where(pos_p < N_META, -1, (pos_p - N_META) // CHUNK).astype(jnp.int32)
    s0_p = jnp.zeros((N_EVEN, B, HG_HEADS, HG_DK, HG_DV), jnp.float32)
    hp, hg_p, fk_p, fv_p, flf_p, ckv_p, kpe_p = trunk(
        xp, s0_p, None, None, None, None, None, pos_p, pos_p, grp_p, grp_p,
        ((N_META, N_META), (T, CHUNK)), p)
    y_prompt = hp[:, N_META:]
    Tn = x_sample.shape[1]
    P = cache_fox_k.shape[2]
    pos_s = P + jnp.arange(Tn, dtype=jnp.int32)
    pos_sk = jnp.arange(P + Tn, dtype=jnp.int32)
    hs, hg_s, fk_s, fv_s, flf_s, ckv_s, kpe_s = trunk(
        x_sample, state_hgrn2, cache_fox_k, cache_fox_v, cache_fox_logf, cache_mla_ckv, cache_mla_kpe,
        pos_s, pos_sk, pos_s // CHUNK, pos_sk // CHUNK, ((Tn, Tn),), p)
    return (y_prompt, hs, hg_p, fk_p, fv_p, flf_p, ckv_p, kpe_p, hg_s, fk_s, fv_s, flf_s, ckv_s, kpe_s)
```

```python
import functools

import jax
import jax.numpy as jnp
from jax import lax
from jax.experimental import pallas as pl
from jax.experimental.pallas import tpu as pltpu

D_MODEL = 1024
CHUNK = 64
N_META = 16
HG_HEADS = 4
HG_DK = 128
HG_DV = 128
HG_W = HG_HEADS * HG_DK
FOX_HEADS = 4
FOX_DH = 128
FOX_W = FOX_HEADS * FOX_DH
MLA_HEADS = 8
MLA_Q_LORA = 512
MLA_KV_LORA = 256
MLA_NOPE = 128
MLA_ROPE = 64
MLA_V = 128
MLA_QPAD = 256
ROPE_BASE = 10000.0
N_EXPERTS = 16
N_GROUPS = 4
EXPERTS_PER_GROUP = 4
D_EXPERT = 256
DEPTH = 2
ALPHA = (2 * DEPTH) ** 0.25
LN_EPS = 1e-5
RMS_EPS = 1e-6

LANES = 128
HG_SUB = 8
NEG = -1e30
F32 = jnp.float32
BF16 = jnp.bfloat16
VMEM_LIMIT = 56 * 1024 * 1024


def _dot(a, b):
    return jnp.dot(a, b, preferred_element_type=F32)


def _dot_nt(a, b):
    return lax.dot_general(a, b, (((1,), (1,)), ((), ())), preferred_element_type=F32)


def _dot_tn(a, b):
    return lax.dot_general(a, b, (((0,), (0,)), ((), ())), preferred_element_type=F32)


def _split3(x):
    hi = x.astype(BF16)
    r = x - hi.astype(F32)
    mid = r.astype(BF16)
    lo = (r - mid.astype(F32)).astype(BF16)
    return hi, mid, lo


def _cumsum_rows(tri, x):
    hi, mid, lo = _split3(x)
    return _dot(tri, hi) + _dot(tri, mid) + _dot(tri, lo)


def _sigmoid(x):
    return 1.0 / (1.0 + jnp.exp(-x))


def _log_sigmoid(x):
    return jnp.minimum(x, 0.0) - jnp.log(1.0 + jnp.exp(-jnp.abs(x)))


def _layer_norm(x, g, b):
    mu = jnp.mean(x, axis=-1, keepdims=True)
    xc = x - mu
    var = jnp.mean(xc * xc, axis=-1, keepdims=True)
    return xc * lax.rsqrt(var + LN_EPS) * g + b


def _rms_norm(x, g):
    return x * lax.rsqrt(jnp.mean(x * x, axis=-1, keepdims=True) + RMS_EPS) * g


def _params(sem):
    return pltpu.CompilerParams(dimension_semantics=sem, vmem_limit_bytes=VMEM_LIMIT)


def _full_spec(a):
    nd = a.ndim
    return pl.BlockSpec(a.shape, lambda *_: (0,) * nd)


def _row_call(kernel, name, rows, tm, row_ins, full_ins, outs, scratch=()):
    assert rows % tm == 0
    in_specs = [pl.BlockSpec((tm, a.shape[1]), lambda i: (i, 0)) for a in row_ins]
    in_specs += [_full_spec(a) for a in full_ins]
    out_specs = [pl.BlockSpec((tm, c), lambda i: (i, 0)) for c, _ in outs]
    out_shape = [jax.ShapeDtypeStruct((rows, c), dt) for c, dt in outs]
    return pl.pallas_call(
        kernel, name=name, grid=(rows // tm,), in_specs=in_specs, out_specs=out_specs,
        out_shape=out_shape, scratch_shapes=list(scratch),
        compiler_params=_params(("parallel",)))(*row_ins, *full_ins)


def _even_proj_kernel(x_ref, w_ref, wf_ref, lbl_ref, fb_ref,
                      hq_ref, lf_ref, hk_ref, hv_ref, hgate_ref,
                      fq_ref, fk_ref, fv_ref, fk16_ref, fv16_ref, flf_ref, *, layer):
    xb = x_ref[...].astype(BF16)

    def blk(j):
        return _dot(xb, w_ref[:, j * HG_W:(j + 1) * HG_W])

    logits = lbl_ref[...]
    e = jnp.exp(logits - jnp.max(logits, axis=0, keepdims=True))
    lb = jnp.sum(e[:layer + 1], axis=0, keepdims=True) / jnp.sum(e, axis=0, keepdims=True)

    hq_ref[...] = blk(0).astype(BF16)
    zf = blk(1)
    lf_ref[...] = jnp.log(lb + (1.0 - lb) * _sigmoid(zf))
    hk_ref[...] = ((1.0 - lb) * _sigmoid(-zf)).astype(BF16)
    hv_ref[...] = blk(2).astype(BF16)
    hgate_ref[...] = _sigmoid(blk(3)).astype(BF16)
    fq_ref[...] = (blk(4) * (FOX_DH ** -0.5)).astype(BF16)
    fk = blk(5)
    fk_ref[...] = fk
    fk16_ref[...] = fk.astype(BF16)
    fv = blk(6)
    fv_ref[...] = fv
    fv16_ref[...] = fv.astype(BF16)
    flf_ref[...] = _log_sigmoid(_dot(xb, wf_ref[...]) + fb_ref[...])


def _even_proj(x, w_main, w_f, lb_logits, fb_pad, tm, layer):
    rows = x.shape[0]
    outs = [(HG_W, BF16), (HG_W, F32), (HG_W, BF16), (HG_W, BF16), (HG_W, BF16),
            (FOX_W, BF16), (FOX_W, F32), (FOX_W, F32), (FOX_W, BF16), (FOX_W, BF16), (LANES, F32)]
    return _row_call(functools.partial(_even_proj_kernel, layer=layer), "even_proj", rows, tm,
                     [x], [w_main, w_f, lb_logits, fb_pad], outs)


def _bcast_sub(x, j):
    n, c = x.shape
    x3 = x.reshape(n // HG_SUB, HG_SUB, c)
    return jnp.broadcast_to(x3[:, j:j + 1, :], x3.shape).reshape(n, c)


def _level_ref(b, w):
    n, c = b.shape
    parts = [jnp.broadcast_to(b[m * 2 * w + w - 1:m * 2 * w + w, :], (2 * w, c)) for m in range(n // (2 * w))]
    return parts[0] if len(parts) == 1 else jnp.concatenate(parts, axis=0)


def _hgrn2_kernel(q_ref, lf_ref, k_ref, v_ref, gate_ref, flf_ref, g_ref, e_ref, s0_ref, f0_ref,
                  o_ref, fcum_ref, sout_ref, st_scr, fc_scr, p_scr, *, n_chunks):
    i = pl.program_id(1)
    C = CHUNK

    @pl.when(i == 0)
    def _():
        for h in range(HG_HEADS):
            st_scr[h] = s0_ref[h].T
        fc_scr[...] = f0_ref[...]

    row = lax.broadcasted_iota(jnp.int32, (C, 1), 0)
    col = lax.broadcasted_iota(jnp.int32, (1, C), 1)
    tri = (col <= row).astype(BF16)
    same = lambda w: (row // w) == (col // w)
    levels = (32, 16, 8)

    for c in range(n_chunks):
        sl = slice(c * C, (c + 1) * C)
        b = _cumsum_rows(tri, lf_ref[sl, :])
        fcum = _cumsum_rows(tri, flf_ref[sl, :]) + fc_scr[...]
        fcum_ref[sl, :] = fcum
        fc_scr[...] = fcum[C - 1:C, :]

        q = q_ref[sl, :].astype(F32)
        k = k_ref[sl, :].astype(F32)
        v = v_ref[sl, :]
        qb = (q * jnp.exp(b)).astype(BF16)
        b_last = b[C - 1:C, :]
        kd = (k * jnp.exp(b_last - b)).astype(BF16)
        e_last = jnp.exp(b_last)

        lv = []
        for w in levels:
            upper = (row % (2 * w)) >= w
            ew = jnp.exp(-jnp.abs(b - _level_ref(b, w)))
            lv.append((jnp.where(upper, q * ew, 0.0).astype(BF16), jnp.where(upper, 0.0, k * ew).astype(BF16)))

        for j in range(HG_SUB):
            pj = jnp.exp(jnp.where((row % HG_SUB) >= j, b - _bcast_sub(b, j), NEG)) * q * _bcast_sub(k, j)
            pj = pj.astype(BF16)
            for h in range(HG_HEADS):
                p_scr[h * C:(h + 1) * C, j * HG_DK:(j + 1) * HG_DK] = pj[:, h * HG_DK:(h + 1) * HG_DK]
        diag = _dot(p_scr[...], e_ref[...])

        for h in range(HG_HEADS):
            hs = slice(h * HG_DK, (h + 1) * HG_DK)
            a = jnp.where(same(HG_SUB), diag[h * C:(h + 1) * C, :], 0.0)
            for w, (qw, kw) in zip(levels, lv):
                aw = _dot_nt(qw[:, hs], kw[:, hs])
                a = a + (aw if 2 * w == C else jnp.where(same(2 * w), aw, 0.0))
            st = st_scr[h]
            vh = v[:, hs]
            o = _dot(a.astype(BF16), vh) + _dot_nt(qb[:, hs], st.astype(BF16))
            st_scr[h] = st * e_last[:, hs] + _dot_tn(vh, kd[:, hs])
            o = _rms_norm(o, g_ref[:, hs])
            o_ref[sl, hs] = (o * gate_ref[sl, hs].astype(F32)).astype(BF16)

    @pl.when(i == pl.num_programs(1) - 1)
    def _():
        for h in range(HG_HEADS):
            sout_ref[h] = st_scr[h].T


def _hgrn2(q, lf, k, v, gate, flf, g, e_mat, s0, f0, n_seq, seq_len, row_off, tb):
    assert seq_len % tb == 0 and tb % CHUNK == 0 and row_off % tb == 0
    nb = seq_len // tb
    off = row_off // tb
    per_seq = s0.shape[0] > 1
    rmap = lambda s, i: (off + s * nb + i, 0)
    omap = lambda s, i: (s * nb + i, 0)
    smap = (lambda s, i: (s, 0, 0, 0)) if per_seq else (lambda s, i: (0, 0, 0, 0))
    fmap = (lambda s, i: (s, 0, 0)) if per_seq else (lambda s, i: (0, 0, 0))
    in_specs = [pl.BlockSpec((tb, HG_W), rmap) for _ in range(5)]
    in_specs += [pl.BlockSpec((tb, LANES), rmap), _full_spec(g), _full_spec(e_mat),
                 pl.BlockSpec((None, HG_HEADS, HG_DK, HG_DV), smap), pl.BlockSpec((None, 1, LANES), fmap)]
    out_specs = [pl.BlockSpec((tb, HG_W), omap), pl.BlockSpec((tb, LANES), omap),
                 pl.BlockSpec((None, HG_HEADS, HG_DK, HG_DV), lambda s, i: (s, 0, 0, 0))]
    out_shape = [jax.ShapeDtypeStruct((n_seq * seq_len, HG_W), BF16),
                 jax.ShapeDtypeStruct((n_seq * seq_len, LANES), F32),
                 jax.ShapeDtypeStruct((n_seq, HG_HEADS, HG_DK, HG_DV), F32)]
    scratch = [pltpu.VMEM((HG_HEADS, HG_DV, HG_DK), F32), pltpu.VMEM((1, LANES), F32),
               pltpu.VMEM((HG_HEADS * CHUNK, HG_SUB * HG_DK), BF16)]
    return pl.pallas_call(
        functools.partial(_hgrn2_kernel, n_chunks=tb // CHUNK), name="hgrn2",
        grid=(n_seq, nb), in_specs=in_specs, out_specs=out_specs, out_shape=out_shape,
        scratch_shapes=scratch, compiler_params=_params(("parallel", "arbitrary")))(
            q, lf, k, v, gate, flf, g, e_mat, s0, f0)


def _cumsum_kernel(x_ref, tri_ref, o_ref, carry):
    @pl.when(pl.program_id(1) == 0)
    def _():
        carry[...] = jnp.zeros_like(carry)

    out = _cumsum_rows(tri_ref[...], x_ref[...]) + carry[...]
    o_ref[...] = out
    carry[...] = out[out.shape[0] - 1:, :]


def _cumsum_cols(x, n_seq, seq_len, tb):
    c = x.shape[1]
    nb = seq_len // tb
    tri = (jnp.arange(tb)[None, :] <= jnp.arange(tb)[:, None]).astype(BF16)
    return pl.pallas_call(
        _cumsum_kernel, name="cumsum", grid=(n_seq, nb),
        in_specs=[pl.BlockSpec((tb, c), lambda s, i: (s * nb + i, 0)), _full_spec(tri)],
        out_specs=pl.BlockSpec((tb, c), lambda s, i: (s * nb + i, 0)),
        out_shape=jax.ShapeDtypeStruct(x.shape, F32),
        scratch_shapes=[pltpu.VMEM((1, c), F32)],
        compiler_params=_params(("parallel", "arbitrary")))(x, tri)


def _flash_kernel(*refs, n_past_blk, tkp, tq, has_bias, has_rope, mask_mode, has_past, single_q):
    it = iter(refs)
    q_ref = next(it)
    fq_ref = next(it) if has_bias else None
    if has_past:
        kp_ref, vp_ref = next(it), next(it)
        rp_ref = next(it) if has_rope else None
        fkp_ref = next(it) if has_bias else None
    kn_ref, vn_ref = next(it), next(it)
    rn_ref = next(it) if has_rope else None
    fkn_ref = next(it) if has_bias else None
    o_ref = next(it)
    m_scr, l_scr, acc_scr = next(it), next(it), next(it)

    qi = pl.program_id(2)
    q = q_ref[...]
    m_scr[...] = jnp.full(m_scr.shape, NEG, F32)
    l_scr[...] = jnp.zeros(l_scr.shape, F32)
    acc_scr[...] = jnp.zeros(acc_scr.shape, F32)

    def step(k, r, v, fk, mask):
        if has_rope:
            k = jnp.concatenate([k, r], axis=1)
        s = _dot_nt(q, k.astype(BF16))
        if has_bias:
            s = s + fq_ref[...] - fk
        if mask is not None:
            s = jnp.where(mask, s, NEG)
        m_prev = m_scr[...]
        m_new = jnp.maximum(m_prev, jnp.max(s, axis=1, keepdims=True))
        alpha = jnp.exp(m_prev - m_new)
        p = jnp.exp(s - m_new)
        l_scr[...] = alpha * l_scr[...] + jnp.sum(p, axis=1, keepdims=True)
        acc_scr[...] = alpha * acc_scr[...] + _dot(p.astype(BF16), v.astype(BF16))
        m_scr[...] = m_new

    if has_past:
        def past_body(j, carry):
            rs = pl.ds(pl.multiple_of(j * tkp, tkp), tkp)
            step(kp_ref[rs, :], rp_ref[rs, :] if has_rope else None, vp_ref[rs, :],
                 fkp_ref[:, rs] if has_bias else None, None)
            return carry
        if n_past_blk == 1:
            step(kp_ref[...], rp_ref[...] if has_rope else None, vp_ref[...],
                 fkp_ref[...] if has_bias else None, None)
        else:
            lax.fori_loop(0, n_past_blk, past_body, 0)

    def new_body(j, carry):
        rs = pl.ds(pl.multiple_of(j * tq, tq), tq)
        step(kn_ref[rs, :], rn_ref[rs, :] if has_rope else None, vn_ref[rs, :],
             fkn_ref[:, rs] if has_bias else None, None)
        return carry
    if single_q:
        rs = slice(None)
    else:
        lax.fori_loop(0, qi, new_body, 0)
        rs = pl.ds(pl.multiple_of(qi * tq, tq), tq)

    row = lax.broadcasted_iota(jnp.int32, (tq, 1), 0)
    col = lax.broadcasted_iota(jnp.int32, (1, tq), 1)
    if mask_mode == "causal":
        mask = col <= row
    elif mask_mode == "chunk":
        mask = (col // CHUNK) <= (row // CHUNK)
    else:
        mask = None
    step(kn_ref[rs, :], rn_ref[rs, :] if has_rope else None, vn_ref[rs, :],
         fkn_ref[:, rs] if has_bias else None, mask)
    o_ref[...] = (acc_scr[...] / l_scr[...]).astype(o_ref.dtype)


def _flash(q, kn, vn, *, n_seq, n_heads, seq_len, tq, dq, dk, dv, q_off, k_off, mask_mode,
           fq=None, fkn=None, rn=None, past=None):
    assert seq_len % tq == 0 and q_off % tq == 0 and k_off % seq_len == 0
    nq = seq_len // tq
    qo = q_off // tq
    ko = k_off // seq_len
    has_bias = fq is not None
    has_rope = rn is not None
    has_past = past is not None
    ins, specs = [q], [pl.BlockSpec((tq, dq), lambda b, h, i: (qo + b * nq + i, h))]
    if has_bias:
        ins.append(fq)
        specs.append(pl.BlockSpec((None, tq, 1), lambda b, h, i: (h, qo + b * nq + i, 0)))
    n_past_blk, tkp = 0, 0
    if has_past:
        tp = past["k"].shape[1]
        tkp = past["tk"]
        assert tp % tkp == 0
        n_past_blk = tp // tkp
        pb = (lambda b: b) if past["k"].shape[0] > 1 else (lambda b: 0)
        ins += [past["k"], past["v"]]
        specs += [pl.BlockSpec((None, tp, dk), lambda b, h, i: (pb(b), 0, h)),
                  pl.BlockSpec((None, tp, dv), lambda b, h, i: (pb(b), 0, h))]
        if has_rope:
            ins.append(past["r"])
            specs.append(pl.BlockSpec((None, tp, LANES), lambda b, h, i: (pb(b), 0, 0)))
        if has_bias:
            ins.append(past["fk"])
            specs.append(pl.BlockSpec((None, None, 1, tp), lambda b, h, i: (pb(b), h, 0, 0)))
    ins += [kn, vn]
    specs += [pl.BlockSpec((seq_len, dk), lambda b, h, i: (ko + b, h)),
              pl.BlockSpec((seq_len, dv), lambda b, h, i: (ko + b, h))]
    if has_rope:
        ins.append(rn)
        specs.append(pl.BlockSpec((seq_len, LANES), lambda b, h, i: (ko + b, 0)))
    if has_bias:
        ins.append(fkn)
        specs.append(pl.BlockSpec((None, None, 1, seq_len), lambda b, h, i: (h, ko + b, 0, 0)))
    kern = functools.partial(_flash_kernel, n_past_blk=n_past_blk, tkp=tkp, tq=tq, has_bias=has_bias,
                             has_rope=has_rope, mask_mode=mask_mode, has_past=has_past, single_q=nq == 1)
    return pl.pallas_call(
        kern, name="flash", grid=(n_seq, n_heads, nq), in_specs=specs,
        out_specs=pl.BlockSpec((tq, dv), lambda b, h, i: (b * nq + i, h)),
        out_shape=jax.ShapeDtypeStruct((n_seq * seq_len, n_heads * dv), BF16),
        scratch_shapes=[pltpu.VMEM((tq, 1), F32), pltpu.VMEM((tq, 1), F32), pltpu.VMEM((tq, dv), F32)],
        compiler_params=_params(("parallel", "parallel", "arbitrary")))(*ins)


def _route(sc, sb):
    def top2_sum(v):
        a, b, c, d = v
        a, b = jnp.maximum(a, b), jnp.minimum(a, b)
        c, d = jnp.maximum(c, d), jnp.minimum(c, d)
        hi, lo2 = jnp.maximum(a, c), jnp.minimum(a, c)
        return hi + jnp.maximum(lo2, jnp.maximum(b, d))

    gs = [top2_sum(sb[g * EXPERTS_PER_GROUP:(g + 1) * EXPERTS_PER_GROUP]) for g in range(N_GROUPS)]
    best_v, best_g = gs[0], jnp.zeros(gs[0].shape, jnp.int32)
    for g in range(1, N_GROUPS):
        upd = gs[g] > best_v
        best_v = jnp.where(upd, gs[g], best_v)
        best_g = jnp.where(upd, g, best_g)
    masked = [jnp.where(best_g == (e // EXPERTS_PER_GROUP), sb[e], -jnp.inf) for e in range(N_EXPERTS)]

    def argmax_first(vals, exclude=None):
        bv = jnp.full(vals[0].shape, -jnp.inf, F32)
        bi = jnp.full(vals[0].shape, -1, jnp.int32)
        for e, v in enumerate(vals):
            upd = v > bv
            if exclude is not None:
                upd = upd & (exclude != e)
            bv = jnp.where(upd, v, bv)
            bi = jnp.where(upd, e, bi)
        return bi

    i1 = argmax_first(masked)
    i2 = argmax_first(masked, exclude=i1)
    w1 = sum(jnp.where(i1 == e, sc[e], 0.0) for e in range(N_EXPERTS))
    w2 = sum(jnp.where(i2 == e, sc[e], 0.0) for e in range(N_EXPERTS))
    tot = w1 + w2
    return [jnp.where(i1 == e, w1 / tot, 0.0) + jnp.where(i2 == e, w2 / tot, 0.0) for e in range(N_EXPERTS)]


def _mix_kernel(*refs, n_act):
    x_ref = refs[0]
    a_refs = refs[1:1 + n_act]
    w_ref, g_ref, b_ref, rw_hi_ref, rw_lo_ref, rb_ref, x1_ref, comb_ref, ct_scr = refs[1 + n_act:]
    y = None
    k0 = 0
    for a_ref in a_refs:
        kw = a_ref.shape[1]
        part = _dot(a_ref[...], w_ref[k0:k0 + kw, :])
        y = part if y is None else y + part
        k0 += kw
    x1 = _layer_norm(ALPHA * x_ref[...] + y, g_ref[...], b_ref[...])
    x1_ref[...] = x1

    x_hi = x1.astype(BF16)
    x_lo = (x1 - x_hi.astype(F32)).astype(BF16)
    logits = _dot(x_hi, rw_hi_ref[...]) + _dot(x_lo, rw_hi_ref[...]) + _dot(x_hi, rw_lo_ref[...])
    scores_t = _sigmoid(logits).T
    sc = [scores_t[e:e + 1, :] for e in range(N_EXPERTS)]
    sb = [sc[e] + rb_ref[e:e + 1, :] for e in range(N_EXPERTS)]
    comb = _route(sc, sb)
    ct_scr[...] = jnp.zeros(ct_scr.shape, F32)
    for e in range(N_EXPERTS):
        ct_scr[e:e + 1, :] = comb[e]
    comb_ref[...] = ct_scr[...].T


def _mix(x, acts, w_out, ln_g, ln_b, rw_hi, rw_lo, rb, tm):
    rows = x.shape[0]
    return _row_call(functools.partial(_mix_kernel, n_act=len(acts)), "mix", rows, tm,
                     [x] + list(acts), [w_out, ln_g, ln_b, rw_hi, rw_lo, rb],
                     [(D_MODEL, F32), (LANES, F32)], scratch=[pltpu.VMEM((LANES, tm), F32)])


def _moe_kernel(x_ref, comb_ref, wg_ref, wu_ref, wd_ref, g_ref, b_ref, o_ref, xb_scr, acc_scr):
    e = pl.program_id(1)

    @pl.when(e == 0)
    def _():
        xb_scr[...] = x_ref[...].astype(BF16)
        acc_scr[...] = jnp.zeros(acc_scr.shape, F32)

    xb = xb_scr[...]
    lane = lax.broadcasted_iota(jnp.int32, (1, LANES), 1)
    c_e = jnp.sum(jnp.where(lane == e, comb_ref[...], 0.0), axis=1, keepdims=True)
    gate = _dot(xb, wg_ref[...])
    h = gate * _sigmoid(gate) * _dot(xb, wu_ref[...])
    acc_scr[...] += _dot((h * c_e).astype(BF16), wd_ref[...])

    @pl.when(e == N_EXPERTS - 1)
    def _():
        o_ref[...] = _layer_norm(ALPHA * x_ref[...] + acc_scr[...], g_ref[...], b_ref[...])


def _moe(x, comb, wg, wu, wd, ln_g, ln_b, tm):
    rows = x.shape[0]
    assert rows % tm == 0
    return pl.pallas_call(
        _moe_kernel, name="moe", grid=(rows // tm, N_EXPERTS),
        in_specs=[pl.BlockSpec((tm, D_MODEL), lambda i, e: (i, 0)),
                  pl.BlockSpec((tm, LANES), lambda i, e: (i, 0)),
                  pl.BlockSpec((None, D_MODEL, D_EXPERT), lambda i, e: (e, 0, 0)),
                  pl.BlockSpec((None, D_MODEL, D_EXPERT), lambda i, e: (e, 0, 0)),
                  pl.BlockSpec((None, D_EXPERT, D_MODEL), lambda i, e: (e, 0, 0)),
                  _full_spec(ln_g), _full_spec(ln_b)],
        out_specs=pl.BlockSpec((tm, D_MODEL), lambda i, e: (i, 0)),
        out_shape=jax.ShapeDtypeStruct((rows, D_MODEL), F32),
        scratch_shapes=[pltpu.VMEM((tm, D_MODEL), BF16), pltpu.VMEM((tm, D_MODEL), F32)],
        compiler_params=_params(("parallel", "arbitrary")))(x, comb, wg, wu, wd, ln_g, ln_b)


def _rope128(x, cos_t, sin_t):
    lane = lax.broadcasted_iota(jnp.int32, (1, LANES), 1)
    half = MLA_ROPE // 2
    swapped = jnp.where(lane < half, pltpu.roll(x, LANES - half, axis=1), pltpu.roll(x, half, axis=1))
    return x * cos_t + swapped * sin_t


def _odd_proj_kernel(x_ref, cos_ref, sin_ref, w_ref, gq_ref, gkv_ref, wuq_ref,
                     q_ref, ckv_ref, kpe_ref, kpe16_ref):
    z = _dot(x_ref[...].astype(BF16), w_ref[...])
    cq = _rms_norm(z[:, :MLA_Q_LORA], gq_ref[...])
    ckv_ref[...] = _rms_norm(z[:, MLA_Q_LORA:MLA_Q_LORA + MLA_KV_LORA], gkv_ref[...])
    cos_t, sin_t = cos_ref[...], sin_ref[...]
    kpe = _rope128(z[:, MLA_Q_LORA + MLA_KV_LORA:], cos_t, sin_t)
    kpe_ref[...] = kpe[:, :MLA_ROPE]
    kpe16_ref[...] = kpe.astype(BF16)
    qf = _dot(cq.astype(BF16), wuq_ref[...])
    scale = (MLA_NOPE + MLA_ROPE) ** -0.5
    for h in range(MLA_HEADS):
        c0 = h * MLA_QPAD
        q_ref[:, c0:c0 + MLA_NOPE] = (qf[:, c0:c0 + MLA_NOPE] * scale).astype(BF16)
        qr = _rope128(qf[:, c0 + MLA_NOPE:c0 + MLA_QPAD], cos_t, sin_t)
        q_ref[:, c0 + MLA_NOPE:c0 + MLA_QPAD] = (qr * scale).astype(BF16)


def _odd_proj(x, cos_t, sin_t, w_in, gq, gkv, wuq, tm):
    rows = x.shape[0]
    outs = [(MLA_HEADS * MLA_QPAD, BF16), (MLA_KV_LORA, F32), (MLA_ROPE, F32), (LANES, BF16)]
    return _row_call(_odd_proj_kernel, "odd_proj", rows, tm, [x, cos_t, sin_t], [w_in, gq, gkv, wuq], outs)


def _kv_expand_kernel(c_ref, w_ref, k_ref, v_ref):
    kv = _dot(c_ref[...].astype(BF16), w_ref[...])
    n = MLA_HEADS * MLA_NOPE
    k_ref[...] = kv[:, :n].astype(BF16)
    v_ref[...] = kv[:, n:].astype(BF16)


def _kv_expand(ckv, w_ukv, tm):
    rows = ckv.shape[0]
    return _row_call(_kv_expand_kernel, "kv_expand", rows, tm, [ckv], [w_ukv],
                     [(MLA_HEADS * MLA_NOPE, BF16), (MLA_HEADS * MLA_V, BF16)])


def _rope_tables(pos):
    half = MLA_ROPE // 2
    inv = ROPE_BASE ** (-jnp.arange(half, dtype=F32) / half)
    ang = pos.astype(F32)[:, None] * inv[None, :]
    cos, sin = jnp.cos(ang), jnp.sin(ang)
    z = jnp.zeros((pos.shape[0], LANES - MLA_ROPE), F32)
    return jnp.concatenate([cos, cos, z], axis=1), jnp.concatenate([-sin, sin, z], axis=1)


def _pad_rows(a, n):
    return jnp.pad(a, ((0, n - a.shape[0]),) + ((0, 0),) * (a.ndim - 1))


def kernel(x_prompt, x_sample, state_hgrn2, cache_fox_k, cache_fox_v, cache_fox_logf, cache_mla_ckv, cache_mla_kpe, meta_tokens, even_w_in, hg_lb_logits, hg_norm_g, fox_forget_bias, even_w_out, mla_w_in, mla_q_norm_g, mla_kv_norm_g, mla_w_uq, mla_w_uk, mla_w_uv, mla_w_out, ln_mix_g, ln_mix_b, ln_ffn_g, ln_ffn_b, router_w, router_bias, moe_w_gate, moe_w_up, moe_w_down):
    B, T, _ = x_prompt.shape
    Bs, Ts, _ = x_sample.shape
    P = cache_fox_k.shape[2]
    RM = B * T
    RS = Bs * Ts
    RSM = -(-(RS + N_META) // LANES) * LANES
    ME = slice(RS, RS + N_META)
    TM_MAIN, TM_MOE, TQ = 512, 1024, 512

    xm = x_prompt.reshape(RM, D_MODEL)
    xs = _pad_rows(jnp.concatenate([x_sample.reshape(RS, D_MODEL), meta_tokens.astype(F32)], axis=0), RSM)

    w_in0 = even_w_in[0]
    n_main = 7 * HG_W
    w_even = w_in0[:, :n_main].astype(BF16)
    w_even_f = jnp.pad(w_in0[:, n_main:], ((0, 0), (0, LANES - FOX_HEADS))).astype(BF16)
    fb_pad = jnp.pad(fox_forget_bias[0][None, :], ((0, 0), (0, LANES - FOX_HEADS)))
    g_hg = hg_norm_g[0].reshape(1, HG_W)
    w_out0 = even_w_out[0].astype(BF16)
    e_mat = ((jnp.arange(HG_SUB * HG_DK)[:, None] // HG_DK) == (jnp.arange(CHUNK)[None, :] % HG_SUB)).astype(BF16)

    w_odd = jnp.pad(mla_w_in[0], ((0, 0), (0, LANES - MLA_ROPE))).astype(BF16)
    gq = mla_q_norm_g[0][None, :]
    gkv = mla_kv_norm_g[0][None, :]
    wuq = mla_w_uq[0].reshape(MLA_Q_LORA, MLA_HEADS, MLA_NOPE + MLA_ROPE)
    wuq = jnp.pad(wuq, ((0, 0), (0, 0), (0, MLA_QPAD - MLA_NOPE - MLA_ROPE)))
    wuq = wuq.reshape(MLA_Q_LORA, MLA_HEADS * MLA_QPAD).astype(BF16)
    w_ukv = jnp.concatenate([mla_w_uk[0].reshape(MLA_KV_LORA, -1), mla_w_uv[0].reshape(MLA_KV_LORA, -1)],
                            axis=1).astype(BF16)
    w_out1 = mla_w_out[0].astype(BF16)

    rw = jnp.pad(router_w, ((0, 0), (0, LANES - N_EXPERTS)))
    rw_hi = rw.astype(BF16)
    rw_lo = (rw - rw_hi.astype(F32)).astype(BF16)
    rb = jnp.pad(router_bias.astype(F32)[:, None], ((0, LANES - N_EXPERTS), (0, 0)))
    wg = moe_w_gate.astype(BF16)
    wu = moe_w_up.astype(BF16)
    wd = moe_w_down.astype(BF16)
    row2 = lambda a: a[None, :]

    def ffn(x, acts, w_out, l, tm_mix, tm_moe):
        x1, comb = _mix(x, acts, w_out, row2(ln_mix_g[l]), row2(ln_mix_b[l]), rw_hi, rw_lo, rb, tm_mix)
        return _moe(x1, comb, wg[l], wu[l], wd[l], row2(ln_ffn_g[l]), row2(ln_ffn_b[l]), tm_moe)

    pm = _even_proj(xm, w_even, w_even_f, hg_lb_logits, fb_pad, TM_MAIN, 0)
    ps = _even_proj(xs, w_even, w_even_f, hg_lb_logits, fb_pad, RSM, 0)
    names = ("hq", "lf", "hk", "hv", "hgate", "fq", "fk", "fv", "fk16", "fv16", "flf")
    pm = dict(zip(names, pm))
    ps = dict(zip(names, ps))

    hg_keys = ("hq", "lf", "hk", "hv", "hgate", "flf")
    meta_in = [_pad_rows(ps[n][ME], CHUNK) for n in hg_keys]
    zero_s = jnp.zeros((1, HG_HEADS, HG_DK, HG_DV), F32)
    zero_f = jnp.zeros((1, 1, LANES), F32)
    o_hg_meta, fc_meta, s_meta = _hgrn2(*meta_in, g_hg, e_mat, zero_s, zero_f, 1, CHUNK, 0, CHUNK)
    o_hg_meta, fc_meta = o_hg_meta[:N_META], fc_meta[:N_META]
    f_meta_end = fc_meta[N_META - 1:N_META][None]

    o_hg_m, fc_m, s_main = _hgrn2(*[pm[n] for n in hg_keys], g_hg, e_mat, s_meta, f_meta_end, B, T, 0, 256)

    logf_c = jnp.pad(cache_fox_logf[0].reshape(Bs * P, FOX_HEADS), ((0, 0), (0, LANES - FOX_HEADS)))
    fpast = _cumsum_cols(logf_c, Bs, P, 512)[:, :FOX_HEADS].reshape(Bs, P, FOX_HEADS)
    f0_s = jnp.pad(fpast[:, P - 1:P, :], ((0, 0), (0, 0), (0, LANES - FOX_HEADS)))
    o_hg_s, fc_s, s_samp = _hgrn2(*[ps[n] for n in hg_keys], g_hg, e_mat, state_hgrn2[0], f0_s, Bs, Ts, 0, CHUNK)

    def bias_layouts(fc, n_seq, seq_len):
        f4 = fc[:, :FOX_HEADS].T
        return f4[:, :, None], f4.reshape(FOX_HEADS, n_seq, 1, seq_len)

    fq_m, fk_m = bias_layouts(fc_m, B, T)
    fq_s, fk_s = bias_layouts(fc_s, Bs, Ts)
    fq_t, fk_t = bias_layouts(fc_meta, 1, N_META)

    fox_kw = dict(n_heads=FOX_HEADS, dq=FOX_DH, dk=FOX_DH, dv=FOX_DH, mask_mode="causal")
    meta_past = dict(k=ps["fk16"][ME][None], v=ps["fv16"][ME][None],
                     fk=jnp.transpose(fk_t, (1, 0, 2, 3)), tk=N_META)
    o_fox_m = _flash(pm["fq"], pm["fk16"], pm["fv16"], n_seq=B, seq_len=T, tq=TQ, q_off=0, k_off=0,
                     fq=fq_m, fkn=fk_m, past=meta_past, **fox_kw)
    samp_past = dict(k=cache_fox_k[0].reshape(Bs, P, FOX_W), v=cache_fox_v[0].reshape(Bs, P, FOX_W),
                     fk=jnp.transpose(fpast, (0, 2, 1))[:, :, None, :], tk=1024)
    o_fox_s = _flash(ps["fq"], ps["fk16"], ps["fv16"], n_seq=Bs, seq_len=Ts, tq=Ts, q_off=0, k_off=0,
                     fq=fq_s, fkn=fk_s, past=samp_past, **fox_kw)
    o_fox_t = _flash(ps["fq"][ME], ps["fk16"][ME], ps["fv16"][ME], n_seq=1, seq_len=N_META, tq=N_META,
                     q_off=0, k_off=0, fq=fq_t, fkn=fk_t, **fox_kw)

    o_hg_small = _pad_rows(jnp.concatenate([o_hg_s, o_hg_meta], axis=0), RSM)
    o_fox_small = _pad_rows(jnp.concatenate([o_fox_s, o_fox_t], axis=0), RSM)
    xm = ffn(xm, [o_hg_m, o_fox_m], w_out0, 0, TM_MAIN, TM_MOE)
    xs = ffn(xs, [o_hg_small, o_fox_small], w_out0, 0, RSM, RSM)

    cos_m, sin_m = _rope_tables(N_META + jnp.arange(T, dtype=jnp.int32))
    pos_small = _pad_rows(jnp.concatenate([jnp.tile(P + jnp.arange(Ts, dtype=jnp.int32), Bs),
                                           jnp.arange(N_META, dtype=jnp.int32)]), RSM)
    cos_s, sin_s = _rope_tables(pos_small)
    qm, ckv_m, kpe_m, kpe16_m = _odd_proj(xm, jnp.tile(cos_m, (B, 1)), jnp.tile(sin_m, (B, 1)),
                                          w_odd, gq, gkv, wuq, TM_MAIN)
    qs, ckv_s, kpe_s, kpe16_s = _odd_proj(xs, cos_s, sin_s, w_odd, gq, gkv, wuq, RSM)
    kn_m, vn_m = _kv_expand(ckv_m, w_ukv, 1024)
    kn_s, vn_s = _kv_expand(ckv_s, w_ukv, RSM)
    kp_c, vp_c = _kv_expand(cache_mla_ckv[0].reshape(Bs * P, MLA_KV_LORA), w_ukv, 1024)
    rp_c = jnp.pad(cache_mla_kpe[0], ((0, 0), (0, 0), (0, LANES - MLA_ROPE))).astype(BF16)

    mla_kw = dict(n_heads=MLA_HEADS, dq=MLA_QPAD, dk=MLA_NOPE, dv=MLA_V)
    meta_past = dict(k=kn_s[ME][None], v=vn_s[ME][None], r=kpe16_s[ME][None], tk=N_META)
    o_m = _flash(qm, kn_m, vn_m, n_seq=B, seq_len=T, tq=TQ, q_off=0, k_off=0, rn=kpe16_m,
                 past=meta_past, mask_mode="chunk", **mla_kw)
    samp_past = dict(k=kp_c.reshape(Bs, P, -1), v=vp_c.reshape(Bs, P, -1), r=rp_c, tk=1024)
    o_s = _flash(qs, kn_s, vn_s, n_seq=Bs, seq_len=Ts, tq=Ts, q_off=0, k_off=0, rn=kpe16_s,
                 past=samp_past, mask_mode="full", **mla_kw)
    o_t = _flash(qs[ME], kn_s[ME], vn_s[ME], n_seq=1, seq_len=N_META, tq=N_META, q_off=0, k_off=0,
                 rn=kpe16_s[ME], mask_mode="full", **mla_kw)
    xm = ffn(xm, [o_m], w_out1, 1, TM_MAIN, TM_MOE)
    xs = ffn(xs, [_pad_rows(jnp.concatenate([o_s, o_t], axis=0), RSM)], w_out1, 1, RSM, RSM)

    def with_meta(main, small, width):
        meta = jnp.broadcast_to(small[ME][None], (B, N_META, width))
        return jnp.concatenate([meta, main.reshape(B, T, width)], axis=1)

    y_prompt = xm.reshape(B, T, D_MODEL)
    y_sample = xs[:RS].reshape(Bs, Ts, D_MODEL)
    hg_p = s_main[None]
    fk_p = with_meta(pm["fk"], ps["fk"], FOX_W).reshape(1, B, N_META + T, FOX_HEADS, FOX_DH)
    fv_p = with_meta(pm["fv"], ps["fv"], FOX_W).reshape(1, B, N_META + T, FOX_HEADS, FOX_DH)
    flf_p = with_meta(pm["flf"][:, :FOX_HEADS], ps["flf"][:, :FOX_HEADS], FOX_HEADS)[None]
    ckv_p = with_meta(ckv_m, ckv_s, MLA_KV_LORA)[None]
    kpe_p = with_meta(kpe_m, kpe_s, MLA_ROPE)[None]
    hg_s = s_samp[None]
    fk_s_out = ps["fk"][:RS].reshape(1, Bs, Ts, FOX_HEADS, FOX_DH)
    fv_s_out = ps["fv"][:RS].reshape(1, Bs, Ts, FOX_HEADS, FOX_DH)
    flf_s = ps["flf"][:RS, :FOX_HEADS].reshape(1, Bs, Ts, FOX_HEADS)
    ckv_so = ckv_s[:RS].reshape(1, Bs, Ts, MLA_KV_LORA)
    kpe_so = kpe_s[:RS].reshape(1, Bs, Ts, MLA_ROPE)
    return (y_prompt, y_sample, hg_p, fk_p, fv_p, flf_p, ckv_p, kpe_p,
            hg_s, fk_s_out, fv_s_out, flf_s, ckv_so, kpe_so)
```

```python
import functools

import jax
import jax.numpy as jnp
from jax import lax
from jax.experimental import pallas as pl
from jax.experimental.pallas import tpu as pltpu

D_MODEL = 1024
CHUNK = 64
N_META = 16
HG_HEADS = 4
HG_DK = 128
HG_DV = 128
HG_W = HG_HEADS * HG_DK
FOX_HEADS = 4
FOX_DH = 128
FOX_W = FOX_HEADS * FOX_DH
MLA_HEADS = 8
MLA_Q_LORA = 512
MLA_KV_LORA = 256
MLA_NOPE = 128
MLA_ROPE = 64
MLA_V = 128
MLA_QPAD = 256
ROPE_BASE = 10000.0
N_EXPERTS = 16
N_GROUPS = 4
EXPERTS_PER_GROUP = 4
D_EXPERT = 256
DEPTH = 2
ALPHA = (2 * DEPTH) ** 0.25
LN_EPS = 1e-5
RMS_EPS = 1e-6

LANES = 128
HG_SUB = 8
NEG = -1e30
LOG2E = 1.4426950408889634
F32 = jnp.float32
BF16 = jnp.bfloat16
VMEM_LIMIT = 56 * 1024 * 1024


def _dot(a, b):
    return jnp.dot(a, b, preferred_element_type=F32)


def _dot_nt(a, b):
    return lax.dot_general(a, b, (((1,), (1,)), ((), ())), preferred_element_type=F32)


def _dot_tn(a, b):
    return lax.dot_general(a, b, (((0,), (0,)), ((), ())), preferred_element_type=F32)


def _split3(x):
    hi = x.astype(BF16)
    r = x - hi.astype(F32)
    mid = r.astype(BF16)
    lo = (r - mid.astype(F32)).astype(BF16)
    return hi, mid, lo


def _cumsum_rows(tri, x):
    hi, mid, lo = _split3(x)
    return _dot(tri, hi) + _dot(tri, mid) + _dot(tri, lo)


def _sigmoid(x):
    return 1.0 / (1.0 + jnp.exp(-x))


def _log_sigmoid(x):
    return jnp.minimum(x, 0.0) - jnp.log(1.0 + jnp.exp(-jnp.abs(x)))


def _layer_norm(x, g, b):
    mu = jnp.mean(x, axis=-1, keepdims=True)
    xc = x - mu
    var = jnp.mean(xc * xc, axis=-1, keepdims=True)
    return xc * lax.rsqrt(var + LN_EPS) * g + b


def _rms_norm(x, g):
    return x * lax.rsqrt(jnp.mean(x * x, axis=-1, keepdims=True) + RMS_EPS) * g


def _params(sem):
    return pltpu.CompilerParams(dimension_semantics=sem, vmem_limit_bytes=VMEM_LIMIT)


def _full_spec(a):
    nd = a.ndim
    return pl.BlockSpec(a.shape, lambda *_: (0,) * nd)


def _row_call(kernel, name, rows, tm, row_ins, full_ins, outs, scratch=()):
    assert rows % tm == 0
    in_specs = [pl.BlockSpec((tm, a.shape[1]), lambda i: (i, 0)) for a in row_ins]
    in_specs += [_full_spec(a) for a in full_ins]
    out_specs = [pl.BlockSpec((tm, c), lambda i: (i, 0)) for c, _ in outs]
    out_shape = [jax.ShapeDtypeStruct((rows, c), dt) for c, dt in outs]
    return pl.pallas_call(
        kernel, name=name, grid=(rows // tm,), in_specs=in_specs, out_specs=out_specs,
        out_shape=out_shape, scratch_shapes=list(scratch),
        compiler_params=_params(("parallel",)))(*row_ins, *full_ins)


def _even_proj_kernel(x_ref, w_ref, wf_ref, lbl_ref, fb_ref,
                      hq_ref, lf_ref, hk_ref, hv_ref, hgate_ref,
                      fq_ref, fk_ref, fv_ref, fk16_ref, fv16_ref, flf_ref, *, layer):
    xb = x_ref[...].astype(BF16)

    def blk(j):
        return _dot(xb, w_ref[:, j * HG_W:(j + 1) * HG_W])

    logits = lbl_ref[...]
    e = jnp.exp(logits - jnp.max(logits, axis=0, keepdims=True))
    lb = jnp.sum(e[:layer + 1], axis=0, keepdims=True) / jnp.sum(e, axis=0, keepdims=True)

    hq_ref[...] = blk(0).astype(BF16)
    zf = blk(1)
    lf_ref[...] = jnp.log(lb + (1.0 - lb) * _sigmoid(zf))
    hk_ref[...] = ((1.0 - lb) * _sigmoid(-zf)).astype(BF16)
    hv_ref[...] = blk(2).astype(BF16)
    hgate_ref[...] = _sigmoid(blk(3)).astype(BF16)
    fq_ref[...] = (blk(4) * (FOX_DH ** -0.5 * LOG2E)).astype(BF16)
    fk = blk(5)
    fk_ref[...] = fk
    fk16_ref[...] = fk.astype(BF16)
    fv = blk(6)
    fv_ref[...] = fv
    fv16_ref[...] = fv.astype(BF16)
    flf_ref[...] = _log_sigmoid(_dot(xb, wf_ref[...]) + fb_ref[...])


def _even_proj(x, w_main, w_f, lb_logits, fb_pad, tm, layer):
    rows = x.shape[0]
    outs = [(HG_W, BF16), (HG_W, F32), (HG_W, BF16), (HG_W, BF16), (HG_W, BF16),
            (FOX_W, BF16), (FOX_W, F32), (FOX_W, F32), (FOX_W, BF16), (FOX_W, BF16), (LANES, F32)]
    return _row_call(functools.partial(_even_proj_kernel, layer=layer), "even_proj", rows, tm,
                     [x], [w_main, w_f, lb_logits, fb_pad], outs)


def _bcast_sub(x, j):
    n, c = x.shape
    x3 = x.reshape(n // HG_SUB, HG_SUB, c)
    return jnp.broadcast_to(x3[:, j:j + 1, :], x3.shape).reshape(n, c)


def _level_ref(b, w):
    n, c = b.shape
    parts = [jnp.broadcast_to(b[m * 2 * w + w - 1:m * 2 * w + w, :], (2 * w, c)) for m in range(n // (2 * w))]
    return parts[0] if len(parts) == 1 else jnp.concatenate(parts, axis=0)


def _hgrn2_kernel(q_ref, lf_ref, k_ref, v_ref, gate_ref, flf_ref, g_ref, e_ref, s0_ref, f0_ref,
                  o_ref, fcum_ref, sout_ref, st_scr, fc_scr, p_scr, *, n_chunks):
    i = pl.program_id(1)
    C = CHUNK

    @pl.when(i == 0)
    def _():
        for h in range(HG_HEADS):
            st_scr[h] = s0_ref[h].T
        fc_scr[...] = f0_ref[...]

    row = lax.broadcasted_iota(jnp.int32, (C, 1), 0)
    col = lax.broadcasted_iota(jnp.int32, (1, C), 1)
    tri = (col <= row).astype(BF16)
    same = lambda w: (row // w) == (col // w)
    levels = (32, 16, 8)

    for c in range(n_chunks):
        sl = slice(c * C, (c + 1) * C)
        b = _cumsum_rows(tri, lf_ref[sl, :])
        fcum = _cumsum_rows(tri, flf_ref[sl, :]) + fc_scr[...]
        fcum_ref[sl, :] = fcum
        fc_scr[...] = fcum[C - 1:C, :]

        q = q_ref[sl, :].astype(F32)
        k = k_ref[sl, :].astype(F32)
        v = v_ref[sl, :]
        qb = (q * jnp.exp(b)).astype(BF16)
        b_last = b[C - 1:C, :]
        kd = (k * jnp.exp(b_last - b)).astype(BF16)
        e_last = jnp.exp(b_last)

        lv = []
        for w in levels:
            upper = (row % (2 * w)) >= w
            ew = jnp.exp(-jnp.abs(b - _level_ref(b, w)))
            lv.append((jnp.where(upper, q * ew, 0.0).astype(BF16), jnp.where(upper, 0.0, k * ew).astype(BF16)))

        for j in range(HG_SUB):
            pj = jnp.exp(jnp.where((row % HG_SUB) >= j, b - _bcast_sub(b, j), NEG)) * q * _bcast_sub(k, j)
            pj = pj.astype(BF16)
            for h in range(HG_HEADS):
                p_scr[h * C:(h + 1) * C, j * HG_DK:(j + 1) * HG_DK] = pj[:, h * HG_DK:(h + 1) * HG_DK]
        diag = _dot(p_scr[...], e_ref[...])

        for h in range(HG_HEADS):
            hs = slice(h * HG_DK, (h + 1) * HG_DK)
            a = jnp.where(same(HG_SUB), diag[h * C:(h + 1) * C, :], 0.0)
            for w, (qw, kw) in zip(levels, lv):
                aw = _dot_nt(qw[:, hs], kw[:, hs])
                a = a + (aw if 2 * w == C else jnp.where(same(2 * w), aw, 0.0))
            st = st_scr[h]
            vh = v[:, hs]
            o = _dot(a.astype(BF16), vh) + _dot_nt(qb[:, hs], st.astype(BF16))
            st_scr[h] = st * e_last[:, hs] + _dot_tn(vh, kd[:, hs])
            o = _rms_norm(o, g_ref[:, hs])
            o_ref[sl, hs] = (o * gate_ref[sl, hs].astype(F32)).astype(BF16)

    @pl.when(i == pl.num_programs(1) - 1)
    def _():
        for h in range(HG_HEADS):
            sout_ref[h] = st_scr[h].T


def _hgrn2(q, lf, k, v, gate, flf, g, e_mat, s0, f0, n_seq, seq_len, row_off, tb):
    assert seq_len % tb == 0 and tb % CHUNK == 0 and row_off % tb == 0
    nb = seq_len // tb
    off = row_off // tb
    per_seq = s0.shape[0] > 1
    rmap = lambda s, i: (off + s * nb + i, 0)
    omap = lambda s, i: (s * nb + i, 0)
    smap = (lambda s, i: (s, 0, 0, 0)) if per_seq else (lambda s, i: (0, 0, 0, 0))
    fmap = (lambda s, i: (s, 0, 0)) if per_seq else (lambda s, i: (0, 0, 0))
    in_specs = [pl.BlockSpec((tb, HG_W), rmap) for _ in range(5)]
    in_specs += [pl.BlockSpec((tb, LANES), rmap), _full_spec(g), _full_spec(e_mat),
                 pl.BlockSpec((None, HG_HEADS, HG_DK, HG_DV), smap), pl.BlockSpec((None, 1, LANES), fmap)]
    out_specs = [pl.BlockSpec((tb, HG_W), omap), pl.BlockSpec((tb, LANES), omap),
                 pl.BlockSpec((None, HG_HEADS, HG_DK, HG_DV), lambda s, i: (s, 0, 0, 0))]
    out_shape = [jax.ShapeDtypeStruct((n_seq * seq_len, HG_W), BF16),
                 jax.ShapeDtypeStruct((n_seq * seq_len, LANES), F32),
                 jax.ShapeDtypeStruct((n_seq, HG_HEADS, HG_DK, HG_DV), F32)]
    scratch = [pltpu.VMEM((HG_HEADS, HG_DV, HG_DK), F32), pltpu.VMEM((1, LANES), F32),
               pltpu.VMEM((HG_HEADS * CHUNK, HG_SUB * HG_DK), BF16)]
    return pl.pallas_call(
        functools.partial(_hgrn2_kernel, n_chunks=tb // CHUNK), name="hgrn2",
        grid=(n_seq, nb), in_specs=in_specs, out_specs=out_specs, out_shape=out_shape,
        scratch_shapes=scratch, compiler_params=_params(("parallel", "arbitrary")))(
            q, lf, k, v, gate, flf, g, e_mat, s0, f0)


def _cumsum_kernel(x_ref, tri_ref, o_ref, carry):
    @pl.when(pl.program_id(1) == 0)
    def _():
        carry[...] = jnp.zeros_like(carry)

    out = _cumsum_rows(tri_ref[...], x_ref[...]) + carry[...]
    o_ref[...] = out
    carry[...] = out[out.shape[0] - 1:, :]


def _cumsum_cols(x, n_seq, seq_len, tb):
    c = x.shape[1]
    nb = seq_len // tb
    tri = (jnp.arange(tb)[None, :] <= jnp.arange(tb)[:, None]).astype(BF16)
    return pl.pallas_call(
        _cumsum_kernel, name="cumsum", grid=(n_seq, nb),
        in_specs=[pl.BlockSpec((tb, c), lambda s, i: (s * nb + i, 0)), _full_spec(tri)],
        out_specs=pl.BlockSpec((tb, c), lambda s, i: (s * nb + i, 0)),
        out_shape=jax.ShapeDtypeStruct(x.shape, F32),
        scratch_shapes=[pltpu.VMEM((1, c), F32)],
        compiler_params=_params(("parallel", "arbitrary")))(x, tri)


def _flash_kernel(*refs, n_past_blk, tkp, tq, has_bias, has_rope, mask_mode, has_past, single_q):
    it = iter(refs)
    q_ref = next(it)
    fq_ref = next(it) if has_bias else None
    if has_past:
        kp_ref, vp_ref = next(it), next(it)
        rp_ref = next(it) if has_rope else None
        fkp_ref = next(it) if has_bias else None
    kn_ref, vn_ref = next(it), next(it)
    rn_ref = next(it) if has_rope else None
    fkn_ref = next(it) if has_bias else None
    o_ref = next(it)
    m_scr, acc_scr, sa_scr, sb_scr = next(it), next(it), next(it), next(it)
    dv = o_ref.shape[1]

    qi = pl.program_id(2)
    q = q_ref[...]
    m_scr[...] = jnp.full(m_scr.shape, NEG, F32)
    acc_scr[...] = jnp.zeros(acc_scr.shape, F32)
    fq_b = jnp.broadcast_to(fq_ref[...] * LOG2E, (tq, LANES)) if has_bias else None

    def scores(k, r, fk):
        if has_rope:
            k = jnp.concatenate([k, r], axis=1)
        s = _dot_nt(q, k.astype(BF16))
        if has_bias:
            s = s + jnp.tile(fq_b, (1, s.shape[1] // LANES)) if s.shape[1] % LANES == 0 else s + fq_b[:, :1]
            s = s - fk * LOG2E
        return s

    def update(s, v, mask):
        if mask is not None:
            s = jnp.where(mask, s, NEG)
        m_prev = m_scr[...]
        m_new = jnp.maximum(m_prev, jnp.max(s, axis=1, keepdims=True))
        alpha = jnp.exp2(m_prev - m_new)
        if s.shape[1] % LANES == 0:
            p = jnp.exp2(s - jnp.tile(m_new, (1, s.shape[1] // LANES)))
        else:
            p = jnp.exp2(s - m_new[:, :1])
        v1 = jnp.concatenate([v.astype(BF16), jnp.ones((v.shape[0], LANES), BF16)], axis=1)
        acc_scr[...] = jnp.tile(alpha, (1, acc_scr.shape[1] // LANES)) * acc_scr[...] + _dot(p.astype(BF16), v1)
        m_scr[...] = m_new

    def past_block(rs):
        return (kp_ref[rs, :], rp_ref[rs, :] if has_rope else None, fkp_ref[:, rs] if has_bias else None)

    def new_block(rs):
        return (kn_ref[rs, :], rn_ref[rs, :] if has_rope else None, fkn_ref[:, rs] if has_bias else None)

    if has_past:
        if n_past_blk == 1:
            update(scores(*past_block(slice(None))), vp_ref[...], None)
        else:
            def past_body(j, carry):
                rs = pl.ds(pl.multiple_of(j * tkp, tkp), tkp)
                update(scores(*past_block(rs)), vp_ref[rs, :], None)
                return carry
            lax.fori_loop(0, n_past_blk, past_body, 0)

    row = lax.broadcasted_iota(jnp.int32, (tq, 1), 0)
    col = lax.broadcasted_iota(jnp.int32, (1, tq), 1)
    if mask_mode == "causal":
        mask = col <= row
    elif mask_mode == "chunk":
        mask = (col // CHUNK) <= (row // CHUNK)
    else:
        mask = None

    if single_q:
        update(scores(*new_block(slice(None))), vn_ref[...], mask)
    else:
        blk = lambda j: pl.ds(pl.multiple_of(j * tq, tq), tq)

        def fill(s_ref, j):
            s_ref[...] = scores(*new_block(blk(j)))

        def drain(s_ref, j, msk):
            update(s_ref[...], vn_ref[blk(j), :], msk)

        fill(sa_scr, 0)
        n_pairs = qi // 2

        def pair_body(i, carry):
            fill(sb_scr, 2 * i + 1)
            drain(sa_scr, 2 * i, None)
            fill(sa_scr, 2 * i + 2)
            drain(sb_scr, 2 * i + 1, None)
            return carry
        lax.fori_loop(0, n_pairs, pair_body, 0)

        @pl.when(qi % 2 == 0)
        def _():
            drain(sa_scr, qi, mask)

        @pl.when(qi % 2 == 1)
        def _():
            fill(sb_scr, qi)
            drain(sa_scr, qi - 1, None)
            drain(sb_scr, qi, mask)

    acc = acc_scr[...]
    o_ref[...] = (acc[:, :dv] / acc[:, dv:]).astype(o_ref.dtype)


def _flash(q, kn, vn, *, n_seq, n_heads, seq_len, tq, dq, dk, dv, q_off, k_off, mask_mode,
           fq=None, fkn=None, rn=None, past=None):
    assert seq_len % tq == 0 and q_off % tq == 0 and k_off % seq_len == 0
    nq = seq_len // tq
    qo = q_off // tq
    ko = k_off // seq_len
    has_bias = fq is not None
    has_rope = rn is not None
    has_past = past is not None
    ins, specs = [q], [pl.BlockSpec((tq, dq), lambda b, h, i: (qo + b * nq + i, h))]
    if has_bias:
        ins.append(fq)
        specs.append(pl.BlockSpec((None, tq, 1), lambda b, h, i: (h, qo + b * nq + i, 0)))
    n_past_blk, tkp = 0, 0
    if has_past:
        tp = past["k"].shape[1]
        tkp = past["tk"]
        assert tp % tkp == 0
        n_past_blk = tp // tkp
        pb = (lambda b: b) if past["k"].shape[0] > 1 else (lambda b: 0)
        ins += [past["k"], past["v"]]
        specs += [pl.BlockSpec((None, tp, dk), lambda b, h, i: (pb(b), 0, h)),
                  pl.BlockSpec((None, tp, dv), lambda b, h, i: (pb(b), 0, h))]
        if has_rope:
            ins.append(past["r"])
            specs.append(pl.BlockSpec((None, tp, LANES), lambda b, h, i: (pb(b), 0, 0)))
        if has_bias:
            ins.append(past["fk"])
            specs.append(pl.BlockSpec((None, None, 1, tp), lambda b, h, i: (pb(b), h, 0, 0)))
    ins += [kn, vn]
    specs += [pl.BlockSpec((seq_len, dk), lambda b, h, i: (ko + b, h)),
              pl.BlockSpec((seq_len, dv), lambda b, h, i: (ko + b, h))]
    if has_rope:
        ins.append(rn)
        specs.append(pl.BlockSpec((seq_len, LANES), lambda b, h, i: (ko + b, 0)))
    if has_bias:
        ins.append(fkn)
        specs.append(pl.BlockSpec((None, None, 1, seq_len), lambda b, h, i: (h, ko + b, 0, 0)))
    kern = functools.partial(_flash_kernel, n_past_blk=n_past_blk, tkp=tkp, tq=tq, has_bias=has_bias,
                             has_rope=has_rope, mask_mode=mask_mode, has_past=has_past, single_q=nq == 1)
    return pl.pallas_call(
        kern, name="flash", grid=(n_seq, n_heads, nq), in_specs=specs,
        out_specs=pl.BlockSpec((tq, dv), lambda b, h, i: (b * nq + i, h)),
        out_shape=jax.ShapeDtypeStruct((n_seq * seq_len, n_heads * dv), BF16),
        scratch_shapes=[pltpu.VMEM((tq, LANES), F32), pltpu.VMEM((tq, dv + LANES), F32),
                        pltpu.VMEM((tq, tq) if nq > 1 else (HG_SUB, LANES), F32),
                        pltpu.VMEM((tq, tq) if nq > 1 else (HG_SUB, LANES), F32)],
        compiler_params=_params(("parallel", "parallel", "arbitrary")))(*ins)


def _route(sc, sb):
    def top2_sum(v):
        a, b, c, d = v
        a, b = jnp.maximum(a, b), jnp.minimum(a, b)
        c, d = jnp.maximum(c, d), jnp.minimum(c, d)
        hi, lo2 = jnp.maximum(a, c), jnp.minimum(a, c)
        return hi + jnp.maximum(lo2, jnp.maximum(b, d))

    gs = [top2_sum(sb[g * EXPERTS_PER_GROUP:(g + 1) * EXPERTS_PER_GROUP]) for g in range(N_GROUPS)]
    best_v, best_g = gs[0], jnp.zeros(gs[0].shape, jnp.int32)
    for g in range(1, N_GROUPS):
        upd = gs[g] > best_v
        best_v = jnp.where(upd, gs[g], best_v)
        best_g = jnp.where(upd, g, best_g)
    masked = [jnp.where(best_g == (e // EXPERTS_PER_GROUP), sb[e], -jnp.inf) for e in range(N_EXPERTS)]

    def argmax_first(vals, exclude=None):
        bv = jnp.full(vals[0].shape, -jnp.inf, F32)
        bi = jnp.full(vals[0].shape, -1, jnp.int32)
        for e, v in enumerate(vals):
            upd = v > bv
            if exclude is not None:
                upd = upd & (exclude != e)
            bv = jnp.where(upd, v, bv)
            bi = jnp.where(upd, e, bi)
        return bi

    i1 = argmax_first(masked)
    i2 = argmax_first(masked, exclude=i1)
    w1 = sum(jnp.where(i1 == e, sc[e], 0.0) for e in range(N_EXPERTS))
    w2 = sum(jnp.where(i2 == e, sc[e], 0.0) for e in range(N_EXPERTS))
    tot = w1 + w2
    return [jnp.where(i1 == e, w1 / tot, 0.0) + jnp.where(i2 == e, w2 / tot, 0.0) for e in range(N_EXPERTS)]


def _mix_kernel(*refs, n_act):
    x_ref = refs[0]
    a_refs = refs[1:1 + n_act]
    w_ref, g_ref, b_ref, rw_hi_ref, rw_lo_ref, rb_ref, x1_ref, comb_ref, ct_scr = refs[1 + n_act:]
    y = None
    k0 = 0
    for a_ref in a_refs:
        kw = a_ref.shape[1]
        part = _dot(a_ref[...], w_ref[k0:k0 + kw, :])
        y = part if y is None else y + part
        k0 += kw
    x1 = _layer_norm(ALPHA * x_ref[...] + y, g_ref[...], b_ref[...])
    x1_ref[...] = x1

    x_hi = x1.astype(BF16)
    x_lo = (x1 - x_hi.astype(F32)).astype(BF16)
    logits = _dot(x_hi, rw_hi_ref[...]) + _dot(x_lo, rw_hi_ref[...]) + _dot(x_hi, rw_lo_ref[...])
    scores_t = _sigmoid(logits).T
    sc = [scores_t[e:e + 1, :] for e in range(N_EXPERTS)]
    sb = [sc[e] + rb_ref[e:e + 1, :] for e in range(N_EXPERTS)]
    comb = _route(sc, sb)
    ct_scr[...] = jnp.zeros(ct_scr.shape, F32)
    for e in range(N_EXPERTS):
        ct_scr[e:e + 1, :] = comb[e]
    comb_ref[...] = ct_scr[...].T


def _mix(x, acts, w_out, ln_g, ln_b, rw_hi, rw_lo, rb, tm):
    rows = x.shape[0]
    return _row_call(functools.partial(_mix_kernel, n_act=len(acts)), "mix", rows, tm,
                     [x] + list(acts), [w_out, ln_g, ln_b, rw_hi, rw_lo, rb],
                     [(D_MODEL, F32), (LANES, F32)], scratch=[pltpu.VMEM((LANES, tm), F32)])


def _moe_kernel(x_ref, comb_ref, wg_ref, wu_ref, wd_ref, g_ref, b_ref, o_ref, xb_scr, acc_scr):
    e = pl.program_id(1)

    @pl.when(e == 0)
    def _():
        xb_scr[...] = x_ref[...].astype(BF16)
        acc_scr[...] = jnp.zeros(acc_scr.shape, F32)

    xb = xb_scr[...]
    lane = lax.broadcasted_iota(jnp.int32, (1, LANES), 1)
    c_e = jnp.sum(jnp.where(lane == e, comb_ref[...], 0.0), axis=1, keepdims=True)
    gate = _dot(xb, wg_ref[...])
    h = gate * _sigmoid(gate) * _dot(xb, wu_ref[...])
    acc_scr[...] += _dot((h * c_e).astype(BF16), wd_ref[...])

    @pl.when(e == N_EXPERTS - 1)
    def _():
        o_ref[...] = _layer_norm(ALPHA * x_ref[...] + acc_scr[...], g_ref[...], b_ref[...])


def _moe(x, comb, wg, wu, wd, ln_g, ln_b, tm):
    rows = x.shape[0]
    assert rows % tm == 0
    return pl.pallas_call(
        _moe_kernel, name="moe", grid=(rows // tm, N_EXPERTS),
        in_specs=[pl.BlockSpec((tm, D_MODEL), lambda i, e: (i, 0)),
                  pl.BlockSpec((tm, LANES), lambda i, e: (i, 0)),
                  pl.BlockSpec((None, D_MODEL, D_EXPERT), lambda i, e: (e, 0, 0)),
                  pl.BlockSpec((None, D_MODEL, D_EXPERT), lambda i, e: (e, 0, 0)),
                  pl.BlockSpec((None, D_EXPERT, D_MODEL), lambda i, e: (e, 0, 0)),
                  _full_spec(ln_g), _full_spec(ln_b)],
        out_specs=pl.BlockSpec((tm, D_MODEL), lambda i, e: (i, 0)),
        out_shape=jax.ShapeDtypeStruct((rows, D_MODEL), F32),
        scratch_shapes=[pltpu.VMEM((tm, D_MODEL), BF16), pltpu.VMEM((tm, D_MODEL), F32)],
        compiler_params=_params(("parallel", "arbitrary")))(x, comb, wg, wu, wd, ln_g, ln_b)


def _rope128(x, cos_t, sin_t):
    lane = lax.broadcasted_iota(jnp.int32, (1, LANES), 1)
    half = MLA_ROPE // 2
    swapped = jnp.where(lane < half, pltpu.roll(x, LANES - half, axis=1), pltpu.roll(x, half, axis=1))
    return x * cos_t + swapped * sin_t


def _odd_proj_kernel(x_ref, cos_ref, sin_ref, w_ref, gq_ref, gkv_ref, wuq_ref,
                     q_ref, ckv_ref, kpe_ref, kpe16_ref):
    z = _dot(x_ref[...].astype(BF16), w_ref[...])
    cq = _rms_norm(z[:, :MLA_Q_LORA], gq_ref[...])
    ckv_ref[...] = _rms_norm(z[:, MLA_Q_LORA:MLA_Q_LORA + MLA_KV_LORA], gkv_ref[...])
    cos_t, sin_t = cos_ref[...], sin_ref[...]
    kpe = _rope128(z[:, MLA_Q_LORA + MLA_KV_LORA:], cos_t, sin_t)
    kpe_ref[...] = kpe[:, :MLA_ROPE]
    kpe16_ref[...] = kpe.astype(BF16)
    qf = _dot(cq.astype(BF16), wuq_ref[...])
    scale = (MLA_NOPE + MLA_ROPE) ** -0.5 * LOG2E
    for h in range(MLA_HEADS):
        c0 = h * MLA_QPAD
        q_ref[:, c0:c0 + MLA_NOPE] = (qf[:, c0:c0 + MLA_NOPE] * scale).astype(BF16)
        qr = _rope128(qf[:, c0 + MLA_NOPE:c0 + MLA_QPAD], cos_t, sin_t)
        q_ref[:, c0 + MLA_NOPE:c0 + MLA_QPAD] = (qr * scale).astype(BF16)


def _odd_proj(x, cos_t, sin_t, w_in, gq, gkv, wuq, tm):
    rows = x.shape[0]
    outs = [(MLA_HEADS * MLA_QPAD, BF16), (MLA_KV_LORA, F32), (MLA_ROPE, F32), (LANES, BF16)]
    return _row_call(_odd_proj_kernel, "odd_proj", rows, tm, [x, cos_t, sin_t], [w_in, gq, gkv, wuq], outs)


def _kv_expand_kernel(c_ref, w_ref, k_ref, v_ref):
    kv = _dot(c_ref[...].astype(BF16), w_ref[...])
    n = MLA_HEADS * MLA_NOPE
    k_ref[...] = kv[:, :n].astype(BF16)
    v_ref[...] = kv[:, n:].astype(BF16)


def _kv_expand(ckv, w_ukv, tm):
    rows = ckv.shape[0]
    return _row_call(_kv_expand_kernel, "kv_expand", rows, tm, [ckv], [w_ukv],
                     [(MLA_HEADS * MLA_NOPE, BF16), (MLA_HEADS * MLA_V, BF16)])


def _rope_tables(pos):
    half = MLA_ROPE // 2
    inv = ROPE_BASE ** (-jnp.arange(half, dtype=F32) / half)
    ang = pos.astype(F32)[:, None] * inv[None, :]
    cos, sin = jnp.cos(ang), jnp.sin(ang)
    z = jnp.zeros((pos.shape[0], LANES - MLA_ROPE), F32)
    return jnp.concatenate([cos, cos, z], axis=1), jnp.concatenate([-sin, sin, z], axis=1)


def _pad_rows(a, n):
    return jnp.pad(a, ((0, n - a.shape[0]),) + ((0, 0),) * (a.ndim - 1))


def kernel(x_prompt, x_sample, state_hgrn2, cache_fox_k, cache_fox_v, cache_fox_logf, cache_mla_ckv, cache_mla_kpe, meta_tokens, even_w_in, hg_lb_logits, hg_norm_g, fox_forget_bias, even_w_out, mla_w_in, mla_q_norm_g, mla_kv_norm_g, mla_w_uq, mla_w_uk, mla_w_uv, mla_w_out, ln_mix_g, ln_mix_b, ln_ffn_g, ln_ffn_b, router_w, router_bias, moe_w_gate, moe_w_up, moe_w_down):
    B, T, _ = x_prompt.shape
    Bs, Ts, _ = x_sample.shape
    P = cache_fox_k.shape[2]
    RM = B * T
    RS = Bs * Ts
    RSM = -(-(RS + N_META) // LANES) * LANES
    ME = slice(RS, RS + N_META)
    TM_MAIN, TM_MOE, TQ = 512, 1024, 512

    xm = x_prompt.reshape(RM, D_MODEL)
    xs = _pad_rows(jnp.concatenate([x_sample.reshape(RS, D_MODEL), meta_tokens.astype(F32)], axis=0), RSM)

    w_in0 = even_w_in[0]
    n_main = 7 * HG_W
    w_even = w_in0[:, :n_main].astype(BF16)
    w_even_f = jnp.pad(w_in0[:, n_main:], ((0, 0), (0, LANES - FOX_HEADS))).astype(BF16)
    fb_pad = jnp.pad(fox_forget_bias[0][None, :], ((0, 0), (0, LANES - FOX_HEADS)))
    g_hg = hg_norm_g[0].reshape(1, HG_W)
    w_out0 = even_w_out[0].astype(BF16)
    e_mat = ((jnp.arange(HG_SUB * HG_DK)[:, None] // HG_DK) == (jnp.arange(CHUNK)[None, :] % HG_SUB)).astype(BF16)

    w_odd = jnp.pad(mla_w_in[0], ((0, 0), (0, LANES - MLA_ROPE))).astype(BF16)
    gq = mla_q_norm_g[0][None, :]
    gkv = mla_kv_norm_g[0][None, :]
    wuq = mla_w_uq[0].reshape(MLA_Q_LORA, MLA_HEADS, MLA_NOPE + MLA_ROPE)
    wuq = jnp.pad(wuq, ((0, 0), (0, 0), (0, MLA_QPAD - MLA_NOPE - MLA_ROPE)))
    wuq = wuq.reshape(MLA_Q_LORA, MLA_HEADS * MLA_QPAD).astype(BF16)
    w_ukv = jnp.concatenate([mla_w_uk[0].reshape(MLA_KV_LORA, -1), mla_w_uv[0].reshape(MLA_KV_LORA, -1)],
                            axis=1).astype(BF16)
    w_out1 = mla_w_out[0].astype(BF16)

    rw = jnp.pad(router_w, ((0, 0), (0, LANES - N_EXPERTS)))
    rw_hi = rw.astype(BF16)
    rw_lo = (rw - rw_hi.astype(F32)).astype(BF16)
    rb = jnp.pad(router_bias.astype(F32)[:, None], ((0, LANES - N_EXPERTS), (0, 0)))
    wg = moe_w_gate.astype(BF16)
    wu = moe_w_up.astype(BF16)
    wd = moe_w_down.astype(BF16)
    row2 = lambda a: a[None, :]

    def ffn(x, acts, w_out, l, tm_mix, tm_moe):
        x1, comb = _mix(x, acts, w_out, row2(ln_mix_g[l]), row2(ln_mix_b[l]), rw_hi, rw_lo, rb, tm_mix)
        return _moe(x1, comb, wg[l], wu[l], wd[l], row2(ln_ffn_g[l]), row2(ln_ffn_b[l]), tm_moe)

    pm = _even_proj(xm, w_even, w_even_f, hg_lb_logits, fb_pad, TM_MAIN, 0)
    ps = _even_proj(xs, w_even, w_even_f, hg_lb_logits, fb_pad, RSM, 0)
    names = ("hq", "lf", "hk", "hv", "hgate", "fq", "fk", "fv", "fk16", "fv16", "flf")
    pm = dict(zip(names, pm))
    ps = dict(zip(names, ps))

    hg_keys = ("hq", "lf", "hk", "hv", "hgate", "flf")
    meta_in = [_pad_rows(ps[n][ME], CHUNK) for n in hg_keys]
    zero_s = jnp.zeros((1, HG_HEADS, HG_DK, HG_DV), F32)
    zero_f = jnp.zeros((1, 1, LANES), F32)
    o_hg_meta, fc_meta, s_meta = _hgrn2(*meta_in, g_hg, e_mat, zero_s, zero_f, 1, CHUNK, 0, CHUNK)
    o_hg_meta, fc_meta = o_hg_meta[:N_META], fc_meta[:N_META]
    f_meta_end = fc_meta[N_META - 1:N_META][None]

    o_hg_m, fc_m, s_main = _hgrn2(*[pm[n] for n in hg_keys], g_hg, e_mat, s_meta, f_meta_end, B, T, 0, 256)

    logf_c = jnp.pad(cache_fox_logf[0].reshape(Bs * P, FOX_HEADS), ((0, 0), (0, LANES - FOX_HEADS)))
    fpast = _cumsum_cols(logf_c, Bs, P, 512)[:, :FOX_HEADS].reshape(Bs, P, FOX_HEADS)
    f0_s = jnp.pad(fpast[:, P - 1:P, :], ((0, 0), (0, 0), (0, LANES - FOX_HEADS)))
    o_hg_s, fc_s, s_samp = _hgrn2(*[ps[n] for n in hg_keys], g_hg, e_mat, state_hgrn2[0], f0_s, Bs, Ts, 0, CHUNK)

    def bias_layouts(fc, n_seq, seq_len):
        f4 = fc[:, :FOX_HEADS].T
        return f4[:, :, None], f4.reshape(FOX_HEADS, n_seq, 1, seq_len)

    fq_m, fk_m = bias_layouts(fc_m, B, T)
    fq_s, fk_s = bias_layouts(fc_s, Bs, Ts)
    fq_t, fk_t = bias_layouts(fc_meta, 1, N_META)

    fox_kw = dict(n_heads=FOX_HEADS, dq=FOX_DH, dk=FOX_DH, dv=FOX_DH, mask_mode="causal")
    meta_past = dict(k=ps["fk16"][ME][None], v=ps["fv16"][ME][None],
                     fk=jnp.transpose(fk_t, (1, 0, 2, 3)), tk=N_META)
    o_fox_m = _flash(pm["fq"], pm["fk16"], pm["fv16"], n_seq=B, seq_len=T, tq=TQ, q_off=0, k_off=0,
                     fq=fq_m, fkn=fk_m, past=meta_past, **fox_kw)
    samp_past = dict(k=cache_fox_k[0].reshape(Bs, P, FOX_W), v=cache_fox_v[0].reshape(Bs, P, FOX_W),
                     fk=jnp.transpose(fpast, (0, 2, 1))[:, :, None, :], tk=1024)
    o_fox_s = _flash(ps["fq"], ps["fk16"], ps["fv16"], n_seq=Bs, seq_len=Ts, tq=Ts, q_off=0, k_off=0,
                     fq=fq_s, fkn=fk_s, past=samp_past, **fox_kw)
    o_fox_t = _flash(ps["fq"][ME], ps["fk16"][ME], ps["fv16"][ME], n_seq=1, seq_len=N_META, tq=N_META,
                     q_off=0, k_off=0, fq=fq_t, fkn=fk_t, **fox_kw)

    o_hg_small = _pad_rows(jnp.concatenate([o_hg_s, o_hg_meta], axis=0), RSM)
    o_fox_small = _pad_rows(jnp.concatenate([o_fox_s, o_fox_t], axis=0), RSM)
    xm = ffn(xm, [o_hg_m, o_fox_m], w_out0, 0, TM_MAIN, TM_MOE)
    xs = ffn(xs, [o_hg_small, o_fox_small], w_out0, 0, RSM, RSM)

    cos_m, sin_m = _rope_tables(N_META + jnp.arange(T, dtype=jnp.int32))
    pos_small = _pad_rows(jnp.concatenate([jnp.tile(P + jnp.arange(Ts, dtype=jnp.int32), Bs),
                                           jnp.arange(N_META, dtype=jnp.int32)]), RSM)
    cos_s, sin_s = _rope_tables(pos_small)
    qm, ckv_m, kpe_m, kpe16_m = _odd_proj(xm, jnp.tile(cos_m, (B, 1)), jnp.tile(sin_m, (B, 1)),
                                          w_odd, gq, gkv, wuq, TM_MAIN)
    qs, ckv_s, kpe_s, kpe16_s = _odd_proj(xs, cos_s, sin_s, w_odd, gq, gkv, wuq, RSM)
    kn_m, vn_m = _kv_expand(ckv_m, w_ukv, 1024)
    kn_s, vn_s = _kv_expand(ckv_s, w_ukv, RSM)
    kp_c, vp_c = _kv_expand(cache_mla_ckv[0].reshape(Bs * P, MLA_KV_LORA), w_ukv, 1024)
    rp_c = jnp.pad(cache_mla_kpe[0], ((0, 0), (0, 0), (0, LANES - MLA_ROPE))).astype(BF16)

    mla_kw = dict(n_heads=MLA_HEADS, dq=MLA_QPAD, dk=MLA_NOPE, dv=MLA_V)
    meta_past = dict(k=kn_s[ME][None], v=vn_s[ME][None], r=kpe16_s[ME][None], tk=N_META)
    o_m = _flash(qm, kn_m, vn_m, n_seq=B, seq_len=T, tq=TQ, q_off=0, k_off=0, rn=kpe16_m,
                 past=meta_past, mask_mode="chunk", **mla_kw)
    samp_past = dict(k=kp_c.reshape(Bs, P, -1), v=vp_c.reshape(Bs, P, -1), r=rp_c, tk=1024)
    o_s = _flash(qs, kn_s, vn_s, n_seq=Bs, seq_len=Ts, tq=Ts, q_off=0, k_off=0, rn=kpe16_s,
                 past=samp_past, mask_mode="full", **mla_kw)
    o_t = _flash(qs[ME], kn_s[ME], vn_s[ME], n_seq=1, seq_len=N_META, tq=N_META, q_off=0, k_off=0,
                 rn=kpe16_s[ME], mask_mode="full", **mla_kw)
    xm = ffn(xm, [o_m], w_out1, 1, TM_MAIN, TM_MOE)
    xs = ffn(xs, [_pad_rows(jnp.concatenate([o_s, o_t], axis=0), RSM)], w_out1, 1, RSM, RSM)

    def with_meta(main, small, width):
        meta = jnp.broadcast_to(small[ME][None], (B, N_META, width))
        return jnp.concatenate([meta, main.reshape(B, T, width)], axis=1)

    y_prompt = xm.reshape(B, T, D_MODEL)
    y_sample = xs[:RS].reshape(Bs, Ts, D_MODEL)
    hg_p = s_main[None]
    fk_p = with_meta(pm["fk"], ps["fk"], FOX_W).reshape(1, B, N_META + T, FOX_HEADS, FOX_DH)
    fv_p = with_meta(pm["fv"], ps["fv"], FOX_W).reshape(1, B, N_META + T, FOX_HEADS, FOX_DH)
    flf_p = with_meta(pm["flf"][:, :FOX_HEADS], ps["flf"][:, :FOX_HEADS], FOX_HEADS)[None]
    ckv_p = with_meta(ckv_m, ckv_s, MLA_KV_LORA)[None]
    kpe_p = with_meta(kpe_m, kpe_s, MLA_ROPE)[None]
    hg_s = s_samp[None]
    fk_s_out = ps["fk"][:RS].reshape(1, Bs, Ts, FOX_HEADS, FOX_DH)
    fv_s_out = ps["fv"][:RS].reshape(1, Bs, Ts, FOX_HEADS, FOX_DH)
    flf_s = ps["flf"][:RS, :FOX_HEADS].reshape(1, Bs, Ts, FOX_HEADS)
    ckv_so = ckv_s[:RS].reshape(1, Bs, Ts, MLA_KV_LORA)
    kpe_so = kpe_s[:RS].reshape(1, Bs, Ts, MLA_ROPE)
    return (y_prompt, y_sample, hg_p, fk_p, fv_p, flf_p, ckv_p, kpe_p,
            hg_s, fk_s_out, fv_s_out, flf_s, ckv_so, kpe_so)
```

```python
import functools

import jax
import jax.numpy as jnp
from jax import lax
from jax.experimental import pallas as pl
from jax.experimental.pallas import tpu as pltpu
from jax.experimental.pallas import tpu_sc as plsc

D_MODEL = 1024
CHUNK = 64
N_META = 16
HG_HEADS = 4
HG_DK = 128
HG_DV = 128
HG_W = HG_HEADS * HG_DK
FOX_HEADS = 4
FOX_DH = 128
FOX_W = FOX_HEADS * FOX_DH
MLA_HEADS = 8
MLA_Q_LORA = 512
MLA_KV_LORA = 256
MLA_NOPE = 128
MLA_ROPE = 64
MLA_V = 128
MLA_QPAD = 256
ROPE_BASE = 10000.0
N_EXPERTS = 16
N_GROUPS = 4
EXPERTS_PER_GROUP = 4
D_EXPERT = 256
DEPTH = 2
ALPHA = (2 * DEPTH) ** 0.25
LN_EPS = 1e-5
RMS_EPS = 1e-6

LANES = 128
HG_SUB = 8
HG_GROUP = 4
NEG = -1e30
LOG2E = 1.4426950408889634
F32 = jnp.float32
BF16 = jnp.bfloat16
VMEM_LIMIT = 56 * 1024 * 1024


def _dot(a, b):
    return jnp.dot(a, b, preferred_element_type=F32)


def _dot_nt(a, b):
    return lax.dot_general(a, b, (((1,), (1,)), ((), ())), preferred_element_type=F32)


def _dot_tn(a, b):
    return lax.dot_general(a, b, (((0,), (0,)), ((), ())), preferred_element_type=F32)


def _split3(x):
    hi = x.astype(BF16)
    r = x - hi.astype(F32)
    mid = r.astype(BF16)
    lo = (r - mid.astype(F32)).astype(BF16)
    return hi, mid, lo


def _cumsum_rows(tri, x):
    hi, mid, lo = _split3(x)
    return _dot(tri, hi) + _dot(tri, mid) + _dot(tri, lo)


def _sigmoid(x):
    return 1.0 / (1.0 + jnp.exp(-x))


def _log_sigmoid(x):
    return jnp.minimum(x, 0.0) - jnp.log(1.0 + jnp.exp(-jnp.abs(x)))


def _layer_norm(x, g, b):
    mu = jnp.mean(x, axis=-1, keepdims=True)
    xc = x - mu
    var = jnp.mean(xc * xc, axis=-1, keepdims=True)
    return xc * lax.rsqrt(var + LN_EPS) * g + b


def _rms_norm(x, g):
    return x * lax.rsqrt(jnp.mean(x * x, axis=-1, keepdims=True) + RMS_EPS) * g


def _params(sem):
    return pltpu.CompilerParams(dimension_semantics=sem, vmem_limit_bytes=VMEM_LIMIT)


def _full_spec(a):
    nd = a.ndim
    return pl.BlockSpec(a.shape, lambda *_: (0,) * nd)


def _row_call(kernel, name, rows, tm, row_ins, full_ins, outs, scratch=()):
    assert rows % tm == 0
    in_specs = [pl.BlockSpec((tm, a.shape[1]), lambda i: (i, 0)) for a in row_ins]
    in_specs += [_full_spec(a) for a in full_ins]
    out_specs = [pl.BlockSpec((tm, c), lambda i: (i, 0)) for c, _ in outs]
    out_shape = [jax.ShapeDtypeStruct((rows, c), dt) for c, dt in outs]
    return pl.pallas_call(
        kernel, name=name, grid=(rows // tm,), in_specs=in_specs, out_specs=out_specs,
        out_shape=out_shape, scratch_shapes=list(scratch),
        compiler_params=_params(("parallel",)))(*row_ins, *full_ins)


def _even_proj_kernel(x_ref, w_ref, wf_ref, lbl_ref, fb_ref,
                      hq_ref, lf_ref, hk_ref, hv_ref, hgate_ref,
                      fq_ref, fk_ref, fv_ref, fk16_ref, fv16_ref, flf_ref, *, layer):
    xb = x_ref[...].astype(BF16)

    def blk(j):
        return _dot(xb, w_ref[:, j * HG_W:(j + 1) * HG_W])

    logits = lbl_ref[...]
    e = jnp.exp(logits - jnp.max(logits, axis=0, keepdims=True))
    lb = jnp.sum(e[:layer + 1], axis=0, keepdims=True) / jnp.sum(e, axis=0, keepdims=True)

    hq_ref[...] = blk(0).astype(BF16)
    zf = blk(1)
    lf_ref[...] = jnp.log(lb + (1.0 - lb) * _sigmoid(zf))
    hk_ref[...] = ((1.0 - lb) * _sigmoid(-zf)).astype(BF16)
    hv_ref[...] = blk(2).astype(BF16)
    hgate_ref[...] = _sigmoid(blk(3)).astype(BF16)
    fq_ref[...] = (blk(4) * (FOX_DH ** -0.5 * LOG2E)).astype(BF16)
    fk = blk(5)
    fk_ref[...] = fk
    fk16_ref[...] = fk.astype(BF16)
    fv = blk(6)
    fv_ref[...] = fv
    fv16_ref[...] = fv.astype(BF16)
    flf_ref[...] = _log_sigmoid(_dot(xb, wf_ref[...]) + fb_ref[...])


def _even_proj(x, w_main, w_f, lb_logits, fb_pad, tm, layer):
    rows = x.shape[0]
    outs = [(HG_W, BF16), (HG_W, F32), (HG_W, BF16), (HG_W, BF16), (HG_W, BF16),
            (FOX_W, BF16), (FOX_W, F32), (FOX_W, F32), (FOX_W, BF16), (FOX_W, BF16), (LANES, F32)]
    return _row_call(functools.partial(_even_proj_kernel, layer=layer), "even_proj", rows, tm,
                     [x], [w_main, w_f, lb_logits, fb_pad], outs)


def _bcast_sub(x, j):
    n, c = x.shape
    x3 = x.reshape(n // HG_SUB, HG_SUB, c)
    return jnp.broadcast_to(x3[:, j:j + 1, :], x3.shape).reshape(n, c)


def _level_ref(b, w):
    n, c = b.shape
    parts = [jnp.broadcast_to(b[m * 2 * w + w - 1:m * 2 * w + w, :], (2 * w, c)) for m in range(n // (2 * w))]
    return parts[0] if len(parts) == 1 else jnp.concatenate(parts, axis=0)


def _hgrn2_kernel(q_ref, lf_ref, k_ref, v_ref, gate_ref, flf_ref, g_ref, e_ref, s0_ref, f0_ref,
                  o_ref, fcum_ref, sout_ref, st_scr, fc_scr, *, n_chunks):
    i = pl.program_id(1)
    C = CHUNK

    @pl.when(i == 0)
    def _():
        for h in range(HG_HEADS):
            st_scr[h] = s0_ref[h].T
        fc_scr[...] = f0_ref[...]

    row = lax.broadcasted_iota(jnp.int32, (C, 1), 0)
    col = lax.broadcasted_iota(jnp.int32, (1, C), 1)
    tri = (col <= row).astype(BF16)
    same = lambda w: (row // w) == (col // w)
    levels = (32, 16, 8)

    for c in range(n_chunks):
        sl = slice(c * C, (c + 1) * C)
        fcum = _cumsum_rows(tri, flf_ref[sl, :]) + fc_scr[...]
        fcum_ref[sl, :] = fcum
        fc_scr[...] = fcum[C - 1:C, :]

        for h0 in range(0, HG_HEADS, HG_GROUP):
            gs = slice(h0 * HG_DK, (h0 + HG_GROUP) * HG_DK)
            b = _cumsum_rows(tri, lf_ref[sl, gs])
            q = q_ref[sl, gs].astype(F32)
            k = k_ref[sl, gs].astype(F32)
            v = v_ref[sl, gs]
            qb = (q * jnp.exp(b)).astype(BF16)
            b_last = b[C - 1:C, :]
            kd = (k * jnp.exp(b_last - b)).astype(BF16)
            e_last = jnp.exp(b_last)

            pjs = [(jnp.exp(jnp.where((row % HG_SUB) >= j, b - _bcast_sub(b, j), NEG)) * q
                    * _bcast_sub(k, j)).astype(BF16) for j in range(HG_SUB)]
            lv = []
            for w in levels:
                upper = (row % (2 * w)) >= w
                ew = jnp.exp(-jnp.abs(b - _level_ref(b, w)))
                lv.append((jnp.where(upper, q * ew, 0.0).astype(BF16), jnp.where(upper, 0.0, k * ew).astype(BF16)))

            for hh in range(HG_GROUP):
                h = h0 + hh
                hs = slice(hh * HG_DK, (hh + 1) * HG_DK)
                ho = slice(h * HG_DK, (h + 1) * HG_DK)
                a = jnp.where(same(HG_SUB), _dot(jnp.concatenate([p[:, hs] for p in pjs], axis=1), e_ref[...]), 0.0)
                for w, (qw, kw) in zip(levels, lv):
                    aw = _dot_nt(qw[:, hs], kw[:, hs])
                    a = a + (aw if 2 * w == C else jnp.where(same(2 * w), aw, 0.0))
                st = st_scr[h]
                vh = v[:, hs]
                o = _dot(a.astype(BF16), vh) + _dot_nt(qb[:, hs], st.astype(BF16))
                st_scr[h] = st * e_last[:, hs] + _dot_tn(vh, kd[:, hs])
                o = _rms_norm(o, g_ref[:, ho])
                o_ref[sl, ho] = (o * gate_ref[sl, ho].astype(F32)).astype(BF16)

    @pl.when(i == pl.num_programs(1) - 1)
    def _():
        for h in range(HG_HEADS):
            sout_ref[h] = st_scr[h].T


def _hgrn2(q, lf, k, v, gate, flf, g, e_mat, s0, f0, n_seq, seq_len, row_off, tb):
    assert seq_len % tb == 0 and tb % CHUNK == 0 and row_off % tb == 0
    nb = seq_len // tb
    off = row_off // tb
    per_seq = s0.shape[0] > 1
    rmap = lambda s, i: (off + s * nb + i, 0)
    omap = lambda s, i: (s * nb + i, 0)
    smap = (lambda s, i: (s, 0, 0, 0)) if per_seq else (lambda s, i: (0, 0, 0, 0))
    fmap = (lambda s, i: (s, 0, 0)) if per_seq else (lambda s, i: (0, 0, 0))
    in_specs = [pl.BlockSpec((tb, HG_W), rmap) for _ in range(5)]
    in_specs += [pl.BlockSpec((tb, LANES), rmap), _full_spec(g), _full_spec(e_mat),
                 pl.BlockSpec((None, HG_HEADS, HG_DK, HG_DV), smap), pl.BlockSpec((None, 1, LANES), fmap)]
    out_specs = [pl.BlockSpec((tb, HG_W), omap), pl.BlockSpec((tb, LANES), omap),
                 pl.BlockSpec((None, HG_HEADS, HG_DK, HG_DV), lambda s, i: (s, 0, 0, 0))]
    out_shape = [jax.ShapeDtypeStruct((n_seq * seq_len, HG_W), BF16),
                 jax.ShapeDtypeStruct((n_seq * seq_len, LANES), F32),
                 jax.ShapeDtypeStruct((n_seq, HG_HEADS, HG_DK, HG_DV), F32)]
    scratch = [pltpu.VMEM((HG_HEADS, HG_DV, HG_DK), F32), pltpu.VMEM((1, LANES), F32)]
    return pl.pallas_call(
        functools.partial(_hgrn2_kernel, n_chunks=tb // CHUNK), name="hgrn2",
        grid=(n_seq, nb), in_specs=in_specs, out_specs=out_specs, out_shape=out_shape,
        scratch_shapes=scratch, compiler_params=_params(("parallel", "arbitrary")))(
            q, lf, k, v, gate, flf, g, e_mat, s0, f0)


def _cumsum_kernel(x_ref, tri_ref, o_ref, carry):
    @pl.when(pl.program_id(1) == 0)
    def _():
        carry[...] = jnp.zeros_like(carry)

    out = _cumsum_rows(tri_ref[...], x_ref[...]) + carry[...]
    o_ref[...] = out
    carry[...] = out[out.shape[0] - 1:, :]


def _cumsum_cols(x, n_seq, seq_len, tb):
    c = x.shape[1]
    nb = seq_len // tb
    tri = (jnp.arange(tb)[None, :] <= jnp.arange(tb)[:, None]).astype(BF16)
    return pl.pallas_call(
        _cumsum_kernel, name="cumsum", grid=(n_seq, nb),
        in_specs=[pl.BlockSpec((tb, c), lambda s, i: (s * nb + i, 0)), _full_spec(tri)],
        out_specs=pl.BlockSpec((tb, c), lambda s, i: (s * nb + i, 0)),
        out_shape=jax.ShapeDtypeStruct(x.shape, F32),
        scratch_shapes=[pltpu.VMEM((1, c), F32)],
        compiler_params=_params(("parallel", "arbitrary")))(x, tri)


def _flash_kernel(*refs, n_past_blk, tkp, tq, has_bias, has_rope, mask_mode, has_past, single_q):
    it = iter(refs)
    q_ref = next(it)
    fq_ref = next(it) if has_bias else None
    if has_past:
        kp_ref, vp_ref = next(it), next(it)
        rp_ref = next(it) if has_rope else None
        fkp_ref = next(it) if has_bias else None
    kn_ref, vn_ref = next(it), next(it)
    rn_ref = next(it) if has_rope else None
    fkn_ref = next(it) if has_bias else None
    o_ref = next(it)
    m_scr, acc_scr, sa_scr, sb_scr = next(it), next(it), next(it), next(it)
    dv = o_ref.shape[1]

    qi = pl.program_id(2)
    q = q_ref[...]
    m_scr[...] = jnp.full(m_scr.shape, NEG, F32)
    acc_scr[...] = jnp.zeros(acc_scr.shape, F32)
    fq_b = jnp.broadcast_to(fq_ref[...] * LOG2E, (tq, LANES)) if has_bias else None

    def scores(k, r, fk):
        if has_rope:
            k = jnp.concatenate([k, r], axis=1)
        s = _dot_nt(q, k.astype(BF16))
        if has_bias:
            s = s + jnp.tile(fq_b, (1, s.shape[1] // LANES)) if s.shape[1] % LANES == 0 else s + fq_b[:, :1]
            s = s - fk * LOG2E
        return s

    def update(s, v, mask):
        if mask is not None:
            s = jnp.where(mask, s, NEG)
        m_prev = m_scr[...]
        m_new = jnp.maximum(m_prev, jnp.max(s, axis=1, keepdims=True))
        alpha = jnp.exp2(m_prev - m_new)
        if s.shape[1] % LANES == 0:
            p = jnp.exp2(s - jnp.tile(m_new, (1, s.shape[1] // LANES)))
        else:
            p = jnp.exp2(s - m_new[:, :1])
        v1 = jnp.concatenate([v.astype(BF16), jnp.ones((v.shape[0], LANES), BF16)], axis=1)
        acc_scr[...] = jnp.tile(alpha, (1, acc_scr.shape[1] // LANES)) * acc_scr[...] + _dot(p.astype(BF16), v1)
        m_scr[...] = m_new

    def past_block(rs):
        return (kp_ref[rs, :], rp_ref[rs, :] if has_rope else None, fkp_ref[:, rs] if has_bias else None)

    def new_block(rs):
        return (kn_ref[rs, :], rn_ref[rs, :] if has_rope else None, fkn_ref[:, rs] if has_bias else None)

    if has_past:
        if n_past_blk == 1:
            update(scores(*past_block(slice(None))), vp_ref[...], None)
        else:
            def past_body(j, carry):
                rs = pl.ds(pl.multiple_of(j * tkp, tkp), tkp)
                update(scores(*past_block(rs)), vp_ref[rs, :], None)
                return carry
            lax.fori_loop(0, n_past_blk, past_body, 0)

    row = lax.broadcasted_iota(jnp.int32, (tq, 1), 0)
    col = lax.broadcasted_iota(jnp.int32, (1, tq), 1)
    if mask_mode == "causal":
        mask = col <= row
    elif mask_mode == "chunk":
        mask = (col // CHUNK) <= (row // CHUNK)
    else:
        mask = None

    if single_q:
        update(scores(*new_block(slice(None))), vn_ref[...], mask)
    else:
        blk = lambda j: pl.ds(pl.multiple_of(j * tq, tq), tq)

        def fill(s_ref, j):
            s_ref[...] = scores(*new_block(blk(j)))

        def drain(s_ref, j, msk):
            update(s_ref[...], vn_ref[blk(j), :], msk)

        fill(sa_scr, 0)
        n_pairs = qi // 2

        def pair_body(i, carry):
            fill(sb_scr, 2 * i + 1)
            drain(sa_scr, 2 * i, None)
            fill(sa_scr, 2 * i + 2)
            drain(sb_scr, 2 * i + 1, None)
            return carry
        lax.fori_loop(0, n_pairs, pair_body, 0)

        @pl.when(qi % 2 == 0)
        def _():
            drain(sa_scr, qi, mask)

        @pl.when(qi % 2 == 1)
        def _():
            fill(sb_scr, qi)
            drain(sa_scr, qi - 1, None)
            drain(sb_scr, qi, mask)

    acc = acc_scr[...]
    o_ref[...] = (acc[:, :dv] / acc[:, dv:]).astype(o_ref.dtype)


def _flash(q, kn, vn, *, n_seq, n_heads, seq_len, tq, dq, dk, dv, q_off, k_off, mask_mode,
           fq=None, fkn=None, rn=None, past=None):
    assert seq_len % tq == 0 and q_off % tq == 0 and k_off % seq_len == 0
    nq = seq_len // tq
    qo = q_off // tq
    ko = k_off // seq_len
    has_bias = fq is not None
    has_rope = rn is not None
    has_past = past is not None
    ins, specs = [q], [pl.BlockSpec((tq, dq), lambda b, h, i: (qo + b * nq + i, h))]
    if has_bias:
        ins.append(fq)
        specs.append(pl.BlockSpec((None, tq, 1), lambda b, h, i: (h, qo + b * nq + i, 0)))
    n_past_blk, tkp = 0, 0
    if has_past:
        tp = past["k"].shape[1]
        tkp = past["tk"]
        assert tp % tkp == 0
        n_past_blk = tp // tkp
        pb = (lambda b: b) if past["k"].shape[0] > 1 else (lambda b: 0)
        ins += [past["k"], past["v"]]
        specs += [pl.BlockSpec((None, tp, dk), lambda b, h, i: (pb(b), 0, h)),
                  pl.BlockSpec((None, tp, dv), lambda b, h, i: (pb(b), 0, h))]
        if has_rope:
            ins.append(past["r"])
            specs.append(pl.BlockSpec((None, tp, LANES), lambda b, h, i: (pb(b), 0, 0)))
        if has_bias:
            ins.append(past["fk"])
            specs.append(pl.BlockSpec((None, None, 1, tp), lambda b, h, i: (pb(b), h, 0, 0)))
    ins += [kn, vn]
    specs += [pl.BlockSpec((seq_len, dk), lambda b, h, i: (ko + b, h)),
              pl.BlockSpec((seq_len, dv), lambda b, h, i: (ko + b, h))]
    if has_rope:
        ins.append(rn)
        specs.append(pl.BlockSpec((seq_len, LANES), lambda b, h, i: (ko + b, 0)))
    if has_bias:
        ins.append(fkn)
        specs.append(pl.BlockSpec((None, None, 1, seq_len), lambda b, h, i: (h, ko + b, 0, 0)))
    kern = functools.partial(_flash_kernel, n_past_blk=n_past_blk, tkp=tkp, tq=tq, has_bias=has_bias,
                             has_rope=has_rope, mask_mode=mask_mode, has_past=has_past, single_q=nq == 1)
    return pl.pallas_call(
        kern, name="flash", grid=(n_seq, n_heads, nq), in_specs=specs,
        out_specs=pl.BlockSpec((tq, dv), lambda b, h, i: (b * nq + i, h)),
        out_shape=jax.ShapeDtypeStruct((n_seq * seq_len, n_heads * dv), BF16),
        scratch_shapes=[pltpu.VMEM((tq, LANES), F32), pltpu.VMEM((tq, dv + LANES), F32),
                        pltpu.VMEM((tq, tq) if nq > 1 else (HG_SUB, LANES), F32),
                        pltpu.VMEM((tq, tq) if nq > 1 else (HG_SUB, LANES), F32)],
        compiler_params=_params(("parallel", "parallel", "arbitrary")))(*ins)


def _route(sc, sb):
    def top2_sum(v):
        a, b, c, d = v
        a, b = jnp.maximum(a, b), jnp.minimum(a, b)
        c, d = jnp.maximum(c, d), jnp.minimum(c, d)
        hi, lo2 = jnp.maximum(a, c), jnp.minimum(a, c)
        return hi + jnp.maximum(lo2, jnp.maximum(b, d))

    gs = [top2_sum(sb[g * EXPERTS_PER_GROUP:(g + 1) * EXPERTS_PER_GROUP]) for g in range(N_GROUPS)]
    best_v, best_g = gs[0], jnp.zeros(gs[0].shape, jnp.int32)
    for g in range(1, N_GROUPS):
        upd = gs[g] > best_v
        best_v = jnp.where(upd, gs[g], best_v)
        best_g = jnp.where(upd, g, best_g)
    masked = [jnp.where(best_g == (e // EXPERTS_PER_GROUP), sb[e], -jnp.inf) for e in range(N_EXPERTS)]

    def argmax_first(vals, exclude=None):
        bv = jnp.full(vals[0].shape, -jnp.inf, F32)
        bi = jnp.full(vals[0].shape, -1, jnp.int32)
        for e, v in enumerate(vals):
            upd = v > bv
            if exclude is not None:
                upd = upd & (exclude != e)
            bv = jnp.where(upd, v, bv)
            bi = jnp.where(upd, e, bi)
        return bi

    i1 = argmax_first(masked)
    i2 = argmax_first(masked, exclude=i1)
    w1 = sum(jnp.where(i1 == e, sc[e], 0.0) for e in range(N_EXPERTS))
    w2 = sum(jnp.where(i2 == e, sc[e], 0.0) for e in range(N_EXPERTS))
    tot = w1 + w2
    w1, w2 = w1 / tot, w2 / tot
    comb = [jnp.where(i1 == e, w1, 0.0) + jnp.where(i2 == e, w2, 0.0) for e in range(N_EXPERTS)]
    return comb + [i1.astype(F32), i2.astype(F32), w1, w2]


def _mix_kernel(*refs, n_act):
    x_ref = refs[0]
    a_refs = refs[1:1 + n_act]
    w_ref, g_ref, b_ref, rw2_ref, rb_ref, x1_ref, x1p_ref, comb_ref, ct_scr = refs[1 + n_act:]
    half = D_MODEL // 2
    ys = []
    for n0 in (0, half):
        y = None
        k0 = 0
        for a_ref in a_refs:
            kw = a_ref.shape[1]
            part = _dot(a_ref[...], w_ref[k0:k0 + kw, n0:n0 + half])
            y = part if y is None else y + part
            k0 += kw
        ys.append(y)
    x1 = _layer_norm(ALPHA * x_ref[...] + jnp.concatenate(ys, axis=1), g_ref[...], b_ref[...])
    x1_ref[...] = x1

    x1p_ref[...] = _pack_pair(x1[:, :half], x1[:, half:])
    x_hi = x1.astype(BF16)
    x_lo = (x1 - x_hi.astype(F32)).astype(BF16)
    l2 = _dot(x_hi, rw2_ref[...])
    logits = l2[:, :LANES] + l2[:, LANES:] + _dot(x_lo, rw2_ref[:, :LANES])
    scores_t = _sigmoid(logits).T
    sc = [scores_t[e:e + 1, :] for e in range(N_EXPERTS)]
    sb = [sc[e] + rb_ref[e:e + 1, :] for e in range(N_EXPERTS)]
    route_rows = _route(sc, sb)
    ct_scr[...] = jnp.zeros(ct_scr.shape, F32)
    for r, val in enumerate(route_rows):
        ct_scr[r:r + 1, :] = val
    comb_ref[...] = ct_scr[...].T


def _mix(x, acts, w_out, ln_g, ln_b, rw2, rb, tm):
    rows = x.shape[0]
    return _row_call(functools.partial(_mix_kernel, n_act=len(acts)), "mix", rows, tm,
                     [x] + list(acts), [w_out, ln_g, ln_b, rw2, rb],
                     [(D_MODEL, F32), (D_MODEL // 2, jnp.uint32), (LANES, F32)],
                     scratch=[pltpu.VMEM((LANES, tm), F32)])


def _moe_kernel(x_ref, comb_ref, wg_ref, wu_ref, wd_ref, g_ref, b_ref, o_ref, xb_scr, acc_scr):
    e = pl.program_id(1)

    @pl.when(e == 0)
    def _():
        xb_scr[...] = x_ref[...].astype(BF16)
        acc_scr[...] = jnp.zeros(acc_scr.shape, F32)

    xb = xb_scr[...]
    lane = lax.broadcasted_iota(jnp.int32, (1, LANES), 1)
    c_e = jnp.sum(jnp.where(lane == e, comb_ref[...], 0.0), axis=1, keepdims=True)
    gate = _dot(xb, wg_ref[...])
    h = gate * _sigmoid(gate) * _dot(xb, wu_ref[...])
    acc_scr[...] += _dot((h * c_e).astype(BF16), wd_ref[...])

    @pl.when(e == N_EXPERTS - 1)
    def _():
        o_ref[...] = _layer_norm(ALPHA * x_ref[...] + acc_scr[...], g_ref[...], b_ref[...])


def _moe(x, comb, wg, wu, wd, ln_g, ln_b, tm):
    rows = x.shape[0]
    assert rows % tm == 0
    return pl.pallas_call(
        _moe_kernel, name="moe", grid=(rows // tm, N_EXPERTS),
        in_specs=[pl.BlockSpec((tm, D_MODEL), lambda i, e: (i, 0)),
                  pl.BlockSpec((tm, LANES), lambda i, e: (i, 0)),
                  pl.BlockSpec((None, D_MODEL, D_EXPERT), lambda i, e: (e, 0, 0)),
                  pl.BlockSpec((None, D_MODEL, D_EXPERT), lambda i, e: (e, 0, 0)),
                  pl.BlockSpec((None, D_EXPERT, D_MODEL), lambda i, e: (e, 0, 0)),
                  _full_spec(ln_g), _full_spec(ln_b)],
        out_specs=pl.BlockSpec((tm, D_MODEL), lambda i, e: (i, 0)),
        out_shape=jax.ShapeDtypeStruct((rows, D_MODEL), F32),
        scratch_shapes=[pltpu.VMEM((tm, D_MODEL), BF16), pltpu.VMEM((tm, D_MODEL), F32)],
        compiler_params=_params(("parallel", "arbitrary")))(x, comb, wg, wu, wd, ln_g, ln_b)


ROUTE_E1, ROUTE_E2, ROUTE_W1, ROUTE_W2 = N_EXPERTS, N_EXPERTS + 1, N_EXPERTS + 2, N_EXPERTS + 3
TE = 512
SC_WINDOW = 128


def _pack_pair(a, b):
    au = lax.bitcast_convert_type(a.astype(BF16).astype(F32), jnp.uint32)
    bu = lax.bitcast_convert_type(b.astype(BF16).astype(F32), jnp.uint32)
    return (au >> 16) | (bu & jnp.uint32(0xFFFF0000))


def _unpack_pair(w):
    a = lax.bitcast_convert_type(w << 16, F32)
    b = lax.bitcast_convert_type(w & jnp.uint32(0xFFFF0000), F32)
    return a, b


def _rank_kernel(route_ref, pos_ref, texp_ref, nused_ref, cnt_scr, carry_scr, seg_scr):
    ph, i = pl.program_id(0), pl.program_id(1)
    T = route_ref.shape[0]
    lane = lax.broadcasted_iota(jnp.int32, (1, LANES), 1)
    lane_f = lane.astype(F32)
    r = route_ref[...]
    e1, e2 = r[:, ROUTE_E1:ROUTE_E1 + 1], r[:, ROUTE_E2:ROUTE_E2 + 1]
    m1, m2 = lane_f == e1, lane_f == e2
    m = jnp.where(m1 | m2, 1.0, 0.0)
    colsum = jnp.sum(m, axis=0, keepdims=True)

    @pl.when((ph == 0) & (i == 0))
    def _():
        cnt_scr[...] = jnp.zeros(cnt_scr.shape, F32)

    @pl.when(ph == 0)
    def _():
        cnt_scr[...] += colsum

    @pl.when((ph == 1) & (i == 0))
    def _():
        cnt = cnt_scr[...].astype(jnp.int32)
        padded = (((cnt + (TE - 1)) // TE) * TE).astype(F32)
        rr = lax.broadcasted_iota(jnp.int32, (LANES, 1), 0)
        upper = (rr < lane).astype(BF16)
        hi, mid, lo = _split3(jnp.broadcast_to(padded, (HG_SUB, LANES)))
        seg = (_dot(hi, upper) + _dot(mid, upper) + _dot(lo, upper))[:1, :]
        seg_scr[...] = seg
        carry_scr[...] = jnp.zeros(carry_scr.shape, F32)
        seg_end = seg + padded
        tile_row = lax.broadcasted_iota(jnp.int32, texp_ref.shape, 1).astype(F32) * float(TE)
        te_acc = jnp.zeros(texp_ref.shape, jnp.int32)
        for e in range(N_EXPERTS):
            te_acc = te_acc + jnp.where(seg_end[:, e:e + 1] <= tile_row, 1, 0)
        texp_ref[...] = jnp.minimum(te_acc, N_EXPERTS - 1)
        nused_ref[...] = jnp.broadcast_to(seg_end[:, N_EXPERTS - 1:N_EXPERTS] / float(TE), nused_ref.shape).astype(jnp.int32)

    @pl.when(ph == 1)
    def _():
        row = lax.broadcasted_iota(jnp.int32, (T, 1), 0)
        col = lax.broadcasted_iota(jnp.int32, (1, T), 1)
        before = (col < row).astype(BF16)
        cum = _dot(before, m.astype(BF16)) + carry_scr[...] + seg_scr[...]
        p1 = jnp.sum(jnp.where(m1, cum, 0.0), axis=1, keepdims=True)
        p2 = jnp.sum(jnp.where(m2, cum, 0.0), axis=1, keepdims=True)
        pos_ref[...] = jnp.where(lane == 0, p1, jnp.where(lane == 1, p2, 0.0)).astype(jnp.int32)
        carry_scr[...] += colsum


def _rank(route, n_tiles, tm):
    rows = route.shape[0]
    nb = rows // tm
    nt_pad = -(-n_tiles // LANES) * LANES
    return pl.pallas_call(
        _rank_kernel, name="rank", grid=(2, nb),
        in_specs=[pl.BlockSpec((tm, LANES), lambda ph, i: (i, 0))],
        out_specs=[pl.BlockSpec((tm, LANES), lambda ph, i: (i * ph, 0)),
                   pl.BlockSpec((1, nt_pad), lambda ph, i: (0, 0)),
                   pl.BlockSpec((1, LANES), lambda ph, i: (0, 0))],
        out_shape=[jax.ShapeDtypeStruct((rows, LANES), jnp.int32),
                   jax.ShapeDtypeStruct((1, nt_pad), jnp.int32),
                   jax.ShapeDtypeStruct((1, LANES), jnp.int32)],
        scratch_shapes=[pltpu.VMEM((1, LANES), F32), pltpu.VMEM((1, LANES), F32), pltpu.VMEM((1, LANES), F32)],
        compiler_params=_params(("arbitrary", "arbitrary")))(route)


def _sc_mesh():
    return plsc.VectorSubcoreMesh(core_axis_name="c", subcore_axis_name="s")


def _sc_scatter_rows(x, idx, n_out):
    rows, d = x.shape
    mesh = _sc_mesh()
    n_workers = mesh.num_cores * mesh.num_subcores
    steps = idx.shape[1] // SC_WINDOW // n_workers
    assert steps * SC_WINDOW * n_workers == idx.shape[1] and rows % SC_WINDOW == 0

    @functools.partial(pl.kernel, out_type=jax.ShapeDtypeStruct((n_out, d), x.dtype), mesh=mesh,
                       scratch_types=[pltpu.VMEM((1, SC_WINDOW), jnp.int32), pltpu.VMEM((SC_WINDOW, d), x.dtype)])
    def scatter(x_hbm, i_hbm, o_hbm, i_vmem, buf):
        first = (lax.axis_index("c") * mesh.num_subcores + lax.axis_index("s")) * steps

        @pl.loop(0, steps)
        def _(t):
            off = (first + t) * SC_WINDOW
            pltpu.sync_copy(i_hbm.at[:, pl.ds(off, SC_WINDOW)], i_vmem)
            pltpu.sync_copy(x_hbm.at[pl.ds(off % rows, SC_WINDOW)], buf)
            pltpu.sync_copy(buf, o_hbm.at[i_vmem.at[0]])

    return scatter(x, idx)


def _sc_gather_rows(x, idx):
    d = x.shape[1]
    n = idx.shape[1]
    mesh = _sc_mesh()
    n_workers = mesh.num_cores * mesh.num_subcores
    steps = n // SC_WINDOW // n_workers
    assert steps * SC_WINDOW * n_workers == n

    @functools.partial(pl.kernel, out_type=jax.ShapeDtypeStruct((n, d), x.dtype), mesh=mesh,
                       scratch_types=[pltpu.VMEM((1, SC_WINDOW), jnp.int32), pltpu.VMEM((SC_WINDOW, d), x.dtype)])
    def gather(x_hbm, i_hbm, o_hbm, i_vmem, buf):
        first = (lax.axis_index("c") * mesh.num_subcores + lax.axis_index("s")) * steps

        @pl.loop(0, steps)
        def _(t):
            off = (first + t) * SC_WINDOW
            pltpu.sync_copy(i_hbm.at[:, pl.ds(off, SC_WINDOW)], i_vmem)
            pltpu.sync_copy(x_hbm.at[i_vmem.at[0]], buf)
            pltpu.sync_copy(buf, o_hbm.at[pl.ds(off, SC_WINDOW)])

    return gather(x, idx)


def _gmm_kernel(texp_ref, nused_ref, x_ref, wg_ref, wu_ref, wd_ref, o_ref):
    @pl.when(pl.program_id(0) < nused_ref[0])
    def _():
        a, b = _unpack_pair(x_ref[...])
        xb = jnp.concatenate([a.astype(BF16), b.astype(BF16)], axis=1)
        gate = _dot(xb, wg_ref[...])
        h = gate * _sigmoid(gate) * _dot(xb, wu_ref[...])
        y = _dot(h.astype(BF16), wd_ref[...])
        o_ref[...] = _pack_pair(y[:, :D_MODEL // 2], y[:, D_MODEL // 2:])


def _gmm(xs, texp, nused, wg, wu, wd):
    rows = xs.shape[0]
    wmap = lambda d, te, nu: (te[d], 0, 0)
    grid_spec = pltpu.PrefetchScalarGridSpec(
        num_scalar_prefetch=2, grid=(rows // TE,),
        in_specs=[pl.BlockSpec((TE, D_MODEL // 2), lambda d, te, nu: (d, 0)),
                  pl.BlockSpec((None, D_MODEL, D_EXPERT), wmap),
                  pl.BlockSpec((None, D_MODEL, D_EXPERT), wmap),
                  pl.BlockSpec((None, D_EXPERT, D_MODEL), wmap)],
        out_specs=pl.BlockSpec((TE, D_MODEL // 2), lambda d, te, nu: (d, 0)))
    return pl.pallas_call(
        _gmm_kernel, name="gmm", grid_spec=grid_spec,
        out_shape=jax.ShapeDtypeStruct((rows, D_MODEL // 2), jnp.uint32),
        compiler_params=_params(("arbitrary",)))(texp, nused, xs, wg, wu, wd)


def _combine_kernel(x_ref, g0_ref, g1_ref, route_ref, g_ref, b_ref, o_ref):
    r = route_ref[...]
    y0 = jnp.concatenate(_unpack_pair(g0_ref[...]), axis=1)
    y1 = jnp.concatenate(_unpack_pair(g1_ref[...]), axis=1)
    f = y0 * r[:, ROUTE_W1:ROUTE_W1 + 1] + y1 * r[:, ROUTE_W2:ROUTE_W2 + 1]
    o_ref[...] = _layer_norm(ALPHA * x_ref[...] + f, g_ref[...], b_ref[...])


def _combine(x, g, route, ln_g, ln_b, tm):
    rows = x.shape[0]
    nb = rows // tm
    return pl.pallas_call(
        _combine_kernel, name="combine", grid=(nb,),
        in_specs=[pl.BlockSpec((tm, D_MODEL), lambda i: (i, 0)),
                  pl.BlockSpec((tm, D_MODEL // 2), lambda i: (i, 0)),
                  pl.BlockSpec((tm, D_MODEL // 2), lambda i: (nb + i, 0)),
                  pl.BlockSpec((tm, LANES), lambda i: (i, 0)), _full_spec(ln_g), _full_spec(ln_b)],
        out_specs=pl.BlockSpec((tm, D_MODEL), lambda i: (i, 0)),
        out_shape=jax.ShapeDtypeStruct((rows, D_MODEL), F32),
        compiler_params=_params(("parallel",)))(x, g, g, route, ln_g, ln_b)


def _moe_routed(x1, x1b, route, wg, wu, wd, ln_g, ln_b, tm):
    rows = x1.shape[0]
    n_rows = 2 * rows + N_EXPERTS * TE
    pos, texp, nused = _rank(route, n_rows // TE, tm)
    idx = jnp.concatenate([pos[:, 0], pos[:, 1]])[None, :]
    xs = _sc_scatter_rows(x1b, idx, n_rows)
    ys = _gmm(xs, texp[0, :n_rows // TE], nused[0, :1], wg, wu, wd)
    g = _sc_gather_rows(ys, idx)
    return _combine(x1, g, route, ln_g, ln_b, tm)


def _rope128(x, cos_t, sin_t):
    lane = lax.broadcasted_iota(jnp.int32, (1, LANES), 1)
    half = MLA_ROPE // 2
    swapped = jnp.where(lane < half, pltpu.roll(x, LANES - half, axis=1), pltpu.roll(x, half, axis=1))
    return x * cos_t + swapped * sin_t


def _odd_proj_kernel(x_ref, cos_ref, sin_ref, w_ref, gq_ref, gkv_ref, wuq_ref,
                     q_ref, ckv_ref, kpe_ref, kpe16_ref):
    z = _dot(x_ref[...].astype(BF16), w_ref[...])
    cq = _rms_norm(z[:, :MLA_Q_LORA], gq_ref[...])
    ckv_ref[...] = _rms_norm(z[:, MLA_Q_LORA:MLA_Q_LORA + MLA_KV_LORA], gkv_ref[...])
    cos_t, sin_t = cos_ref[...], sin_ref[...]
    kpe = _rope128(z[:, MLA_Q_LORA + MLA_KV_LORA:], cos_t, sin_t)
    kpe_ref[...] = kpe[:, :MLA_ROPE]
    kpe16_ref[...] = kpe.astype(BF16)
    qf = _dot(cq.astype(BF16), wuq_ref[...])
    scale = (MLA_NOPE + MLA_ROPE) ** -0.5 * LOG2E
    for h in range(MLA_HEADS):
        c0 = h * MLA_QPAD
        q_ref[:, c0:c0 + MLA_NOPE] = (qf[:, c0:c0 + MLA_NOPE] * scale).astype(BF16)
        qr = _rope128(qf[:, c0 + MLA_NOPE:c0 + MLA_QPAD], cos_t, sin_t)
        q_ref[:, c0 + MLA_NOPE:c0 + MLA_QPAD] = (qr * scale).astype(BF16)


def _odd_proj(x, cos_t, sin_t, w_in, gq, gkv, wuq, tm):
    rows = x.shape[0]
    outs = [(MLA_HEADS * MLA_QPAD, BF16), (MLA_KV_LORA, F32), (MLA_ROPE, F32), (LANES, BF16)]
    return _row_call(_odd_proj_kernel, "odd_proj", rows, tm, [x, cos_t, sin_t], [w_in, gq, gkv, wuq], outs)


def _kv_expand_kernel(c_ref, w_ref, k_ref, v_ref):
    kv = _dot(c_ref[...].astype(BF16), w_ref[...])
    n = MLA_HEADS * MLA_NOPE
    k_ref[...] = kv[:, :n].astype(BF16)
    v_ref[...] = kv[:, n:].astype(BF16)


def _kv_expand(ckv, w_ukv, tm):
    rows = ckv.shape[0]
    return _row_call(_kv_expand_kernel, "kv_expand", rows, tm, [ckv], [w_ukv],
                     [(MLA_HEADS * MLA_NOPE, BF16), (MLA_HEADS * MLA_V, BF16)])


def _rope_tables(pos):
    half = MLA_ROPE // 2
    inv = ROPE_BASE ** (-jnp.arange(half, dtype=F32) / half)
    ang = pos.astype(F32)[:, None] * inv[None, :]
    cos, sin = jnp.cos(ang), jnp.sin(ang)
    z = jnp.zeros((pos.shape[0], LANES - MLA_ROPE), F32)
    return jnp.concatenate([cos, cos, z], axis=1), jnp.concatenate([-sin, sin, z], axis=1)


def _pad_rows(a, n):
    return jnp.pad(a, ((0, n - a.shape[0]),) + ((0, 0),) * (a.ndim - 1))


def kernel(x_prompt, x_sample, state_hgrn2, cache_fox_k, cache_fox_v, cache_fox_logf, cache_mla_ckv, cache_mla_kpe, meta_tokens, even_w_in, hg_lb_logits, hg_norm_g, fox_forget_bias, even_w_out, mla_w_in, mla_q_norm_g, mla_kv_norm_g, mla_w_uq, mla_w_uk, mla_w_uv, mla_w_out, ln_mix_g, ln_mix_b, ln_ffn_g, ln_ffn_b, router_w, router_bias, moe_w_gate, moe_w_up, moe_w_down):
    B, T, _ = x_prompt.shape
    Bs, Ts, _ = x_sample.shape
    P = cache_fox_k.shape[2]
    RM = B * T
    RS = Bs * Ts
    RSM = -(-(RS + N_META) // LANES) * LANES
    ME = slice(RS, RS + N_META)
    TM_MAIN, TM_MOE, TQ = 512, 1024, 512

    xm = x_prompt.reshape(RM, D_MODEL)
    xs = _pad_rows(jnp.concatenate([x_sample.reshape(RS, D_MODEL), meta_tokens.astype(F32)], axis=0), RSM)

    w_in0 = even_w_in[0]
    n_main = 7 * HG_W
    w_even = w_in0[:, :n_main].astype(BF16)
    w_even_f = jnp.pad(w_in0[:, n_main:], ((0, 0), (0, LANES - FOX_HEADS))).astype(BF16)
    fb_pad = jnp.pad(fox_forget_bias[0][None, :], ((0, 0), (0, LANES - FOX_HEADS)))
    g_hg = hg_norm_g[0].reshape(1, HG_W)
    w_out0 = even_w_out[0].astype(BF16)
    e_mat = ((jnp.arange(HG_SUB * HG_DK)[:, None] // HG_DK) == (jnp.arange(CHUNK)[None, :] % HG_SUB)).astype(BF16)

    w_odd = jnp.pad(mla_w_in[0], ((0, 0), (0, LANES - MLA_ROPE))).astype(BF16)
    gq = mla_q_norm_g[0][None, :]
    gkv = mla_kv_norm_g[0][None, :]
    wuq = mla_w_uq[0].reshape(MLA_Q_LORA, MLA_HEADS, MLA_NOPE + MLA_ROPE)
    wuq = jnp.pad(wuq, ((0, 0), (0, 0), (0, MLA_QPAD - MLA_NOPE - MLA_ROPE)))
    wuq = wuq.reshape(MLA_Q_LORA, MLA_HEADS * MLA_QPAD).astype(BF16)
    w_ukv = jnp.concatenate([mla_w_uk[0].reshape(MLA_KV_LORA, -1), mla_w_uv[0].reshape(MLA_KV_LORA, -1)],
                            axis=1).astype(BF16)
    w_out1 = mla_w_out[0].astype(BF16)

    rw = jnp.pad(router_w, ((0, 0), (0, LANES - N_EXPERTS)))
    rw_hi = rw.astype(BF16)
    rw2 = jnp.concatenate([rw_hi, (rw - rw_hi.astype(F32)).astype(BF16)], axis=1)
    rb = jnp.pad(router_bias.astype(F32)[:, None], ((0, LANES - N_EXPERTS), (0, 0)))
    wg = moe_w_gate.astype(BF16)
    wu = moe_w_up.astype(BF16)
    wd = moe_w_down.astype(BF16)
    row2 = lambda a: a[None, :]

    def ffn(x, acts, w_out, l, tm_mix, tm_moe, routed):
        x1, x1b, route = _mix(x, acts, w_out, row2(ln_mix_g[l]), row2(ln_mix_b[l]), rw2, rb, tm_mix)
        ln = (row2(ln_ffn_g[l]), row2(ln_ffn_b[l]))
        if routed:
            return _moe_routed(x1, x1b, route, wg[l], wu[l], wd[l], *ln, tm_mix)
        return _moe(x1, route, wg[l], wu[l], wd[l], *ln, tm_moe)

    pm = _even_proj(xm, w_even, w_even_f, hg_lb_logits, fb_pad, TM_MAIN, 0)
    ps = _even_proj(xs, w_even, w_even_f, hg_lb_logits, fb_pad, RSM, 0)
    names = ("hq", "lf", "hk", "hv", "hgate", "fq", "fk", "fv", "fk16", "fv16", "flf")
    pm = dict(zip(names, pm))
    ps = dict(zip(names, ps))

    hg_keys = ("hq", "lf", "hk", "hv", "hgate", "flf")
    meta_in = [_pad_rows(ps[n][ME], CHUNK) for n in hg_keys]
    zero_s = jnp.zeros((1, HG_HEADS, HG_DK, HG_DV), F32)
    zero_f = jnp.zeros((1, 1, LANES), F32)
    o_hg_meta, fc_meta, s_meta = _hgrn2(*meta_in, g_hg, e_mat, zero_s, zero_f, 1, CHUNK, 0, CHUNK)
    o_hg_meta, fc_meta = o_hg_meta[:N_META], fc_meta[:N_META]
    f_meta_end = fc_meta[N_META - 1:N_META][None]

    o_hg_m, fc_m, s_main = _hgrn2(*[pm[n] for n in hg_keys], g_hg, e_mat, s_meta, f_meta_end, B, T, 0, 256)

    logf_c = jnp.pad(cache_fox_logf[0].reshape(Bs * P, FOX_HEADS), ((0, 0), (0, LANES - FOX_HEADS)))
    fpast = _cumsum_cols(logf_c, Bs, P, 512)[:, :FOX_HEADS].reshape(Bs, P, FOX_HEADS)
    f0_s = jnp.pad(fpast[:, P - 1:P, :], ((0, 0), (0, 0), (0, LANES - FOX_HEADS)))
    o_hg_s, fc_s, s_samp = _hgrn2(*[ps[n] for n in hg_keys], g_hg, e_mat, state_hgrn2[0], f0_s, Bs, Ts, 0, CHUNK)

    def bias_layouts(fc, n_seq, seq_len):
        f4 = fc[:, :FOX_HEADS].T
        return f4[:, :, None], f4.reshape(FOX_HEADS, n_seq, 1, seq_len)

    fq_m, fk_m = bias_layouts(fc_m, B, T)
    fq_s, fk_s = bias_layouts(fc_s, Bs, Ts)
    fq_t, fk_t = bias_layouts(fc_meta, 1, N_META)

    fox_kw = dict(n_heads=FOX_HEADS, dq=FOX_DH, dk=FOX_DH, dv=FOX_DH, mask_mode="causal")
    meta_past = dict(k=ps["fk16"][ME][None], v=ps["fv16"][ME][None],
                     fk=jnp.transpose(fk_t, (1, 0, 2, 3)), tk=N_META)
    o_fox_m = _flash(pm["fq"], pm["fk16"], pm["fv16"], n_seq=B, seq_len=T, tq=TQ, q_off=0, k_off=0,
                     fq=fq_m, fkn=fk_m, past=meta_past, **fox_kw)
    samp_past = dict(k=cache_fox_k[0].reshape(Bs, P, FOX_W), v=cache_fox_v[0].reshape(Bs, P, FOX_W),
                     fk=jnp.transpose(fpast, (0, 2, 1))[:, :, None, :], tk=1024)
    o_fox_s = _flash(ps["fq"], ps["fk16"], ps["fv16"], n_seq=Bs, seq_len=Ts, tq=Ts, q_off=0, k_off=0,
                     fq=fq_s, fkn=fk_s, past=samp_past, **fox_kw)
    o_fox_t = _flash(ps["fq"][ME], ps["fk16"][ME], ps["fv16"][ME], n_seq=1, seq_len=N_META, tq=N_META,
                     q_off=0, k_off=0, fq=fq_t, fkn=fk_t, **fox_kw)

    o_hg_small = _pad_rows(jnp.concatenate([o_hg_s, o_hg_meta], axis=0), RSM)
    o_fox_small = _pad_rows(jnp.concatenate([o_fox_s, o_fox_t], axis=0), RSM)
    xm = ffn(xm, [o_hg_m, o_fox_m], w_out0, 0, TM_MAIN, TM_MOE, True)
    xs = ffn(xs, [o_hg_small, o_fox_small], w_out0, 0, RSM, RSM, False)

    cos_m, sin_m = _rope_tables(N_META + jnp.arange(T, dtype=jnp.int32))
    pos_small = _pad_rows(jnp.concatenate([jnp.tile(P + jnp.arange(Ts, dtype=jnp.int32), Bs),
                                           jnp.arange(N_META, dtype=jnp.int32)]), RSM)
    cos_s, sin_s = _rope_tables(pos_small)
    qm, ckv_m, kpe_m, kpe16_m = _odd_proj(xm, jnp.tile(cos_m, (B, 1)), jnp.tile(sin_m, (B, 1)),
                                          w_odd, gq, gkv, wuq, TM_MAIN)
    qs, ckv_s, kpe_s, kpe16_s = _odd_proj(xs, cos_s, sin_s, w_odd, gq, gkv, wuq, RSM)
    kn_m, vn_m = _kv_expand(ckv_m, w_ukv, 1024)
    kn_s, vn_s = _kv_expand(ckv_s, w_ukv, RSM)
    kp_c, vp_c = _kv_expand(cache_mla_ckv[0].reshape(Bs * P, MLA_KV_LORA), w_ukv, 1024)
    rp_c = jnp.pad(cache_mla_kpe[0], ((0, 0), (0, 0), (0, LANES - MLA_ROPE))).astype(BF16)

    mla_kw = dict(n_heads=MLA_HEADS, dq=MLA_QPAD, dk=MLA_NOPE, dv=MLA_V)
    meta_past = dict(k=kn_s[ME][None], v=vn_s[ME][None], r=kpe16_s[ME][None], tk=N_META)
    o_m = _flash(qm, kn_m, vn_m, n_seq=B, seq_len=T, tq=TQ, q_off=0, k_off=0, rn=kpe16_m,
                 past=meta_past, mask_mode="chunk", **mla_kw)
    samp_past = dict(k=kp_c.reshape(Bs, P, -1), v=vp_c.reshape(Bs, P, -1), r=rp_c, tk=1024)
    o_s = _flash(qs, kn_s, vn_s, n_seq=Bs, seq_len=Ts, tq=Ts, q_off=0, k_off=0, rn=kpe16_s,
                 past=samp_past, mask_mode="full", **mla_kw)
    o_t = _flash(qs[ME], kn_s[ME], vn_s[ME], n_seq=1, seq_len=N_META, tq=N_META, q_off=0, k_off=0,
                 rn=kpe16_s[ME], mask_mode="full", **mla_kw)
    xm = ffn(xm, [o_m], w_out1, 1, TM_MAIN, TM_MOE, True)
    xs = ffn(xs, [_pad_rows(jnp.concatenate([o_s, o_t], axis=0), RSM)], w_out1, 1, RSM, RSM, False)

    def with_meta(main, small, width):
        meta = jnp.broadcast_to(small[ME][None], (B, N_META, width))
        return jnp.concatenate([meta, main.reshape(B, T, width)], axis=1)

    y_prompt = xm.reshape(B, T, D_MODEL)
    y_sample = xs[:RS].reshape(Bs, Ts, D_MODEL)
    hg_p = s_main[None]
    fk_p = with_meta(pm["fk"], ps["fk"], FOX_W).reshape(1, B, N_META + T, FOX_HEADS, FOX_DH)
    fv_p = with_meta(pm["fv"], ps["fv"], FOX_W).reshape(1, B, N_META + T, FOX_HEADS, FOX_DH)
    flf_p = with_meta(pm["flf"][:, :FOX_HEADS], ps["flf"][:, :FOX_HEADS], FOX_HEADS)[None]
    ckv_p = with_meta(ckv_m, ckv_s, MLA_KV_LORA)[None]
    kpe_p = with_meta(kpe_m, kpe_s, MLA_ROPE)[None]
    hg_s = s_samp[None]
    fk_s_out = ps["fk"][:RS].reshape(1, Bs, Ts, FOX_HEADS, FOX_DH)
    fv_s_out = ps["fv"][:RS].reshape(1, Bs, Ts, FOX_HEADS, FOX_DH)
    flf_s = ps["flf"][:RS, :FOX_HEADS].reshape(1, Bs, Ts, FOX_HEADS)
    ckv_so = ckv_s[:RS].reshape(1, Bs, Ts, MLA_KV_LORA)
    kpe_so = kpe_s[:RS].reshape(1, Bs, Ts, MLA_ROPE)
    return (y_prompt, y_sample, hg_p, fk_p, fv_p, flf_p, ckv_p, kpe_p,
            hg_s, fk_s_out, fv_s_out, flf_s, ckv_so, kpe_so)
```

```python
import functools

import jax
import jax.numpy as jnp
from jax import lax
from jax.experimental import pallas as pl
from jax.experimental.pallas import tpu as pltpu
from jax.experimental.pallas import tpu_sc as plsc

D_MODEL = 1024
CHUNK = 64
N_META = 16
HG_HEADS = 4
HG_DK = 128
HG_DV = 128
HG_W = HG_HEADS * HG_DK
FOX_HEADS = 4
FOX_DH = 128
FOX_W = FOX_HEADS * FOX_DH
MLA_HEADS = 8
MLA_Q_LORA = 512
MLA_KV_LORA = 256
MLA_NOPE = 128
MLA_ROPE = 64
MLA_V = 128
MLA_QPAD = 256
ROPE_BASE = 10000.0
N_EXPERTS = 16
N_GROUPS = 4
EXPERTS_PER_GROUP = 4
D_EXPERT = 256
DEPTH = 2
ALPHA = (2 * DEPTH) ** 0.25
LN_EPS = 1e-5
RMS_EPS = 1e-6

LANES = 128
HG_SUB = 8
HG_GROUP = 4
NEG = -1e30
LOG2E = 1.4426950408889634
F32 = jnp.float32
BF16 = jnp.bfloat16
VMEM_LIMIT = 56 * 1024 * 1024


def _dot(a, b):
    return jnp.dot(a, b, preferred_element_type=F32)


def _dot_nt(a, b):
    return lax.dot_general(a, b, (((1,), (1,)), ((), ())), preferred_element_type=F32)


def _dot_tn(a, b):
    return lax.dot_general(a, b, (((0,), (0,)), ((), ())), preferred_element_type=F32)


def _split3(x):
    hi = x.astype(BF16)
    r = x - hi.astype(F32)
    mid = r.astype(BF16)
    lo = (r - mid.astype(F32)).astype(BF16)
    return hi, mid, lo


def _cumsum_rows(tri, x):
    hi, mid, lo = _split3(x)
    return _dot(tri, hi) + _dot(tri, mid) + _dot(tri, lo)


def _sigmoid(x):
    return 1.0 / (1.0 + jnp.exp(-x))


def _log_sigmoid(x):
    return jnp.minimum(x, 0.0) - jnp.log(1.0 + jnp.exp(-jnp.abs(x)))


def _layer_norm(x, g, b):
    mu = jnp.mean(x, axis=-1, keepdims=True)
    xc = x - mu
    var = jnp.mean(xc * xc, axis=-1, keepdims=True)
    return xc * lax.rsqrt(var + LN_EPS) * g + b


def _rms_norm(x, g):
    return x * lax.rsqrt(jnp.mean(x * x, axis=-1, keepdims=True) + RMS_EPS) * g


def _params(sem):
    return pltpu.CompilerParams(dimension_semantics=sem, vmem_limit_bytes=VMEM_LIMIT)


def _full_spec(a):
    nd = a.ndim
    return pl.BlockSpec(a.shape, lambda *_: (0,) * nd)


def _row_call(kernel, name, rows, tm, row_ins, full_ins, outs, scratch=()):
    assert rows % tm == 0
    in_specs = [pl.BlockSpec((tm, a.shape[1]), lambda i: (i, 0)) for a in row_ins]
    in_specs += [_full_spec(a) for a in full_ins]
    trail = [c if isinstance(c, tuple) else (c,) for c, _ in outs]
    out_specs = [pl.BlockSpec((tm,) + t, lambda i, n=len(t): (i,) + (0,) * n) for t in trail]
    out_shape = [jax.ShapeDtypeStruct((rows,) + t, dt) for t, (_, dt) in zip(trail, outs)]
    return pl.pallas_call(
        kernel, name=name, grid=(rows // tm,), in_specs=in_specs, out_specs=out_specs,
        out_shape=out_shape, scratch_shapes=list(scratch),
        compiler_params=_params(("parallel",)))(*row_ins, *full_ins)


def _even_proj_kernel(x_ref, w_ref, wf_ref, lbl_ref, fb_ref,
                      hq_ref, lf_ref, hk_ref, hv_ref, hgate_ref,
                      fq_ref, fk_ref, fv_ref, fk16_ref, fv16_ref, flf_ref, *, layer):
    xb = x_ref[...].astype(BF16)

    def blk(j):
        return _dot(xb, w_ref[:, j * HG_W:(j + 1) * HG_W])

    logits = lbl_ref[...]
    e = jnp.exp(logits - jnp.max(logits, axis=0, keepdims=True))
    lb = jnp.sum(e[:layer + 1], axis=0, keepdims=True) / jnp.sum(e, axis=0, keepdims=True)

    hq_ref[...] = blk(0).astype(BF16)
    zf = blk(1)
    lf_ref[...] = jnp.log(lb + (1.0 - lb) * _sigmoid(zf))
    hk_ref[...] = ((1.0 - lb) * _sigmoid(-zf)).astype(BF16)
    hv_ref[...] = blk(2).astype(BF16)
    hgate_ref[...] = _sigmoid(blk(3)).astype(BF16)
    fq_ref[...] = (blk(4) * (FOX_DH ** -0.5 * LOG2E)).astype(BF16)
    fk = blk(5)
    fk16_ref[...] = fk.astype(BF16)
    fv = blk(6)
    fv16_ref[...] = fv.astype(BF16)
    for h in range(FOX_HEADS):
        fk_ref[:, h, :] = fk[:, h * FOX_DH:(h + 1) * FOX_DH]
        fv_ref[:, h, :] = fv[:, h * FOX_DH:(h + 1) * FOX_DH]
    flf_ref[...] = _log_sigmoid(_dot(xb, wf_ref[...]) + fb_ref[...])


def _even_proj(x, w_main, w_f, lb_logits, fb_pad, tm, layer):
    rows = x.shape[0]
    outs = [(HG_W, BF16), (HG_W, F32), (HG_W, BF16), (HG_W, BF16), (HG_W, BF16),
            (FOX_W, BF16), ((FOX_HEADS, FOX_DH), F32), ((FOX_HEADS, FOX_DH), F32), (FOX_W, BF16), (FOX_W, BF16),
            (LANES, F32)]
    return _row_call(functools.partial(_even_proj_kernel, layer=layer), "even_proj", rows, tm,
                     [x], [w_main, w_f, lb_logits, fb_pad], outs)


def _bcast_sub(x, j):
    n, c = x.shape
    x3 = x.reshape(n // HG_SUB, HG_SUB, c)
    return jnp.broadcast_to(x3[:, j:j + 1, :], x3.shape).reshape(n, c)


def _level_ref(b, w):
    n, c = b.shape
    parts = [jnp.broadcast_to(b[m * 2 * w + w - 1:m * 2 * w + w, :], (2 * w, c)) for m in range(n // (2 * w))]
    return parts[0] if len(parts) == 1 else jnp.concatenate(parts, axis=0)


def _hgrn2_kernel(q_ref, lf_ref, k_ref, v_ref, gate_ref, flf_ref, g_ref, e_ref, s0_ref, f0_ref,
                  o_ref, fcum_ref, sout_ref, st_scr, fc_scr, *, n_chunks):
    i = pl.program_id(1)
    C = CHUNK

    @pl.when(i == 0)
    def _():
        for h in range(HG_HEADS):
            st_scr[h] = s0_ref[h].T
        fc_scr[...] = f0_ref[...]

    row = lax.broadcasted_iota(jnp.int32, (C, 1), 0)
    col = lax.broadcasted_iota(jnp.int32, (1, C), 1)
    tri = (col <= row).astype(BF16)
    same = lambda w: (row // w) == (col // w)
    levels = (32, 16, 8)

    for c in range(n_chunks):
        sl = slice(c * C, (c + 1) * C)
        fcum = _cumsum_rows(tri, flf_ref[sl, :]) + fc_scr[...]
        fcum_ref[sl, :] = fcum
        fc_scr[...] = fcum[C - 1:C, :]

        for h0 in range(0, HG_HEADS, HG_GROUP):
            gs = slice(h0 * HG_DK, (h0 + HG_GROUP) * HG_DK)
            b = _cumsum_rows(tri, lf_ref[sl, gs])
            q = q_ref[sl, gs].astype(F32)
            k = k_ref[sl, gs].astype(F32)
            v = v_ref[sl, gs]
            qb = (q * jnp.exp(b)).astype(BF16)
            b_last = b[C - 1:C, :]
            kd = (k * jnp.exp(b_last - b)).astype(BF16)
            e_last = jnp.exp(b_last)

            pjs = [(jnp.exp(jnp.where((row % HG_SUB) >= j, b - _bcast_sub(b, j), NEG)) * q
                    * _bcast_sub(k, j)).astype(BF16) for j in range(HG_SUB)]
            lv = []
            for w in levels:
                upper = (row % (2 * w)) >= w
                ew = jnp.exp(-jnp.abs(b - _level_ref(b, w)))
                lv.append((jnp.where(upper, q * ew, 0.0).astype(BF16), jnp.where(upper, 0.0, k * ew).astype(BF16)))

            for hh in range(HG_GROUP):
                h = h0 + hh
                hs = slice(hh * HG_DK, (hh + 1) * HG_DK)
                ho = slice(h * HG_DK, (h + 1) * HG_DK)
                a = jnp.where(same(HG_SUB), _dot(jnp.concatenate([p[:, hs] for p in pjs], axis=1), e_ref[...]), 0.0)
                for w, (qw, kw) in zip(levels, lv):
                    aw = _dot_nt(qw[:, hs], kw[:, hs])
                    a = a + (aw if 2 * w == C else jnp.where(same(2 * w), aw, 0.0))
                st = st_scr[h]
                vh = v[:, hs]
                o = _dot(a.astype(BF16), vh) + _dot_nt(qb[:, hs], st.astype(BF16))
                st_scr[h] = st * e_last[:, hs] + _dot_tn(vh, kd[:, hs])
                o = _rms_norm(o, g_ref[:, ho])
                o_ref[sl, ho] = (o * gate_ref[sl, ho].astype(F32)).astype(BF16)

    @pl.when(i == pl.num_programs(1) - 1)
    def _():
        for h in range(HG_HEADS):
            sout_ref[h] = st_scr[h].T


def _hgrn2(q, lf, k, v, gate, flf, g, e_mat, s0, f0, n_seq, seq_len, row_off, tb):
    assert seq_len % tb == 0 and tb % CHUNK == 0 and row_off % tb == 0
    nb = seq_len // tb
    off = row_off // tb
    per_seq = s0.shape[0] > 1
    rmap = lambda s, i: (off + s * nb + i, 0)
    omap = lambda s, i: (s * nb + i, 0)
    smap = (lambda s, i: (s, 0, 0, 0)) if per_seq else (lambda s, i: (0, 0, 0, 0))
    fmap = (lambda s, i: (s, 0, 0)) if per_seq else (lambda s, i: (0, 0, 0))
    in_specs = [pl.BlockSpec((tb, HG_W), rmap) for _ in range(5)]
    in_specs += [pl.BlockSpec((tb, LANES), rmap), _full_spec(g), _full_spec(e_mat),
                 pl.BlockSpec((None, HG_HEADS, HG_DK, HG_DV), smap), pl.BlockSpec((None, 1, LANES), fmap)]
    out_specs = [pl.BlockSpec((tb, HG_W), omap), pl.BlockSpec((tb, LANES), omap),
                 pl.BlockSpec((None, HG_HEADS, HG_DK, HG_DV), lambda s, i: (s, 0, 0, 0))]
    out_shape = [jax.ShapeDtypeStruct((n_seq * seq_len, HG_W), BF16),
                 jax.ShapeDtypeStruct((n_seq * seq_len, LANES), F32),
                 jax.ShapeDtypeStruct((n_seq, HG_HEADS, HG_DK, HG_DV), F32)]
    scratch = [pltpu.VMEM((HG_HEADS, HG_DV, HG_DK), F32), pltpu.VMEM((1, LANES), F32)]
    return pl.pallas_call(
        functools.partial(_hgrn2_kernel, n_chunks=tb // CHUNK), name="hgrn2",
        grid=(n_seq, nb), in_specs=in_specs, out_specs=out_specs, out_shape=out_shape,
        scratch_shapes=scratch, compiler_params=_params(("parallel", "arbitrary")))(
            q, lf, k, v, gate, flf, g, e_mat, s0, f0)


def _cumsum_kernel(x_ref, tri_ref, o_ref, carry):
    @pl.when(pl.program_id(1) == 0)
    def _():
        carry[...] = jnp.zeros_like(carry)

    hi, mid, lo = _split3(x_ref[...])
    tri = tri_ref[...]
    out = _dot(hi, tri) + _dot(mid, tri) + _dot(lo, tri) + carry[...]
    o_ref[...] = out
    carry[...] = out[:, out.shape[1] - 1:]


def _cumsum_lanes(x, tb):
    n_seq, r, seq_len = x.shape
    nb = seq_len // tb
    tri = (jnp.arange(tb)[:, None] <= jnp.arange(tb)[None, :]).astype(BF16)
    return pl.pallas_call(
        _cumsum_kernel, name="cumsum", grid=(n_seq, nb),
        in_specs=[pl.BlockSpec((None, r, tb), lambda s, i: (s, 0, i)), _full_spec(tri)],
        out_specs=pl.BlockSpec((None, r, tb), lambda s, i: (s, 0, i)),
        out_shape=jax.ShapeDtypeStruct(x.shape, F32),
        scratch_shapes=[pltpu.VMEM((r, 1), F32)],
        compiler_params=_params(("parallel", "arbitrary")))(x, tri)


def _flash_kernel(*refs, n_past_blk, tkp, tq, has_bias, has_rope, mask_mode, has_past, past_heads, single_q):
    it = iter(refs)
    q_ref = next(it)
    fq_ref = next(it) if has_bias else None
    if has_past:
        kp_ref, vp_ref = next(it), next(it)
        rp_ref = next(it) if has_rope else None
        fkp_ref = next(it) if has_bias else None
    kn_ref, vn_ref = next(it), next(it)
    rn_ref = next(it) if has_rope else None
    fkn_ref = next(it) if has_bias else None
    o_ref = next(it)
    m_scr, acc_scr, sa_scr, sb_scr = next(it), next(it), next(it), next(it)
    dv = o_ref.shape[1]

    qi = pl.program_id(2)
    q = q_ref[...]
    m_scr[...] = jnp.full(m_scr.shape, NEG, F32)
    acc_scr[...] = jnp.zeros(acc_scr.shape, F32)
    fq_b = jnp.broadcast_to(fq_ref[...] * LOG2E, (tq, LANES)) if has_bias else None

    def scores(k, r, fk):
        if has_rope:
            k = jnp.concatenate([k, r], axis=1)
        s = _dot_nt(q, k.astype(BF16))
        if has_bias:
            s = s + jnp.tile(fq_b, (1, s.shape[1] // LANES)) if s.shape[1] % LANES == 0 else s + fq_b[:, :1]
            s = s - fk * LOG2E
        return s

    def update(s, v, mask):
        if mask is not None:
            s = jnp.where(mask, s, NEG)
        m_prev = m_scr[...]
        m_new = jnp.maximum(m_prev, jnp.max(s, axis=1, keepdims=True))
        alpha = jnp.exp2(m_prev - m_new)
        if s.shape[1] % LANES == 0:
            p = jnp.exp2(s - jnp.tile(m_new, (1, s.shape[1] // LANES)))
        else:
            p = jnp.exp2(s - m_new[:, :1])
        v1 = jnp.concatenate([v.astype(BF16), jnp.ones((v.shape[0], LANES), BF16)], axis=1)
        acc_scr[...] = jnp.tile(alpha, (1, acc_scr.shape[1] // LANES)) * acc_scr[...] + _dot(p.astype(BF16), v1)
        m_scr[...] = m_new

    past_kv = (lambda ref, rs: ref[rs, pl.program_id(1), :]) if past_heads else (lambda ref, rs: ref[rs, :])

    def past_block(rs):
        return (past_kv(kp_ref, rs), rp_ref[rs, :] if has_rope else None, fkp_ref[:, rs] if has_bias else None)

    def new_block(rs):
        return (kn_ref[rs, :], rn_ref[rs, :] if has_rope else None, fkn_ref[:, rs] if has_bias else None)

    if has_past:
        if n_past_blk == 1:
            update(scores(*past_block(slice(None))), past_kv(vp_ref, slice(None)), None)
        else:
            def past_body(j, carry):
                rs = pl.ds(pl.multiple_of(j * tkp, tkp), tkp)
                update(scores(*past_block(rs)), past_kv(vp_ref, rs), None)
                return carry
            lax.fori_loop(0, n_past_blk, past_body, 0)

    row = lax.broadcasted_iota(jnp.int32, (tq, 1), 0)
    col = lax.broadcasted_iota(jnp.int32, (1, tq), 1)
    if mask_mode == "causal":
        mask = col <= row
    elif mask_mode == "chunk":
        mask = (col // CHUNK) <= (row // CHUNK)
    else:
        mask = None

    if single_q:
        update(scores(*new_block(slice(None))), vn_ref[...], mask)
    else:
        blk = lambda j: pl.ds(pl.multiple_of(j * tq, tq), tq)

        def fill(s_ref, j):
            s_ref[...] = scores(*new_block(blk(j)))

        def drain(s_ref, j, msk):
            update(s_ref[...], vn_ref[blk(j), :], msk)

        def pair(j0):
            fill(sb_scr, j0 + 1)
            drain(sa_scr, j0, None)
            fill(sa_scr, j0 + 2)
            drain(sb_scr, j0 + 1, None)

        fill(sa_scr, 0)
        n_quads = qi // 4

        def quad_body(i, carry):
            pair(4 * i)
            pair(4 * i + 2)
            return carry
        lax.fori_loop(0, n_quads, quad_body, 0)

        @pl.when(qi % 4 >= 2)
        def _():
            pair(4 * n_quads)

        @pl.when(qi % 2 == 0)
        def _():
            drain(sa_scr, qi, mask)

        @pl.when(qi % 2 == 1)
        def _():
            fill(sb_scr, qi)
            drain(sa_scr, qi - 1, None)
            drain(sb_scr, qi, mask)

    acc = acc_scr[...]
    o_ref[...] = (acc[:, :dv] / acc[:, dv:]).astype(o_ref.dtype)


def _flash(q, kn, vn, *, n_seq, n_heads, seq_len, tq, dq, dk, dv, q_off, k_off, mask_mode,
           fq=None, fkn=None, rn=None, past=None):
    assert seq_len % tq == 0 and q_off % tq == 0 and k_off % seq_len == 0
    nq = seq_len // tq
    qo = q_off // tq
    ko = k_off // seq_len
    has_bias = fq is not None
    has_rope = rn is not None
    has_past = past is not None
    ins, specs = [q], [pl.BlockSpec((tq, dq), lambda b, h, i: (qo + b * nq + i, h))]
    if has_bias:
        ins.append(fq)
        specs.append(pl.BlockSpec((None, tq, 1), lambda b, h, i: (h, qo + b * nq + i, 0)))
    n_past_blk, tkp, past_heads = 0, 0, False
    if has_past:
        tp = past["k"].shape[1]
        tkp = past["tk"]
        assert tp % tkp == 0
        n_past_blk = tp // tkp
        pb = (lambda b: b) if past["k"].shape[0] > 1 else (lambda b: 0)
        ins += [past["k"], past["v"]]
        past_heads = past["k"].ndim == 4
        if past_heads:
            specs += [pl.BlockSpec((None, tp, n_heads, dk), lambda b, h, i: (pb(b), 0, 0, 0)),
                      pl.BlockSpec((None, tp, n_heads, dv), lambda b, h, i: (pb(b), 0, 0, 0))]
        else:
            specs += [pl.BlockSpec((None, tp, dk), lambda b, h, i: (pb(b), 0, h)),
                      pl.BlockSpec((None, tp, dv), lambda b, h, i: (pb(b), 0, h))]
        if has_rope:
            ins.append(past["r"])
            specs.append(pl.BlockSpec((None, tp, LANES), lambda b, h, i: (pb(b), 0, 0)))
        if has_bias:
            ins.append(past["fk"])
            specs.append(pl.BlockSpec((None, None, 1, tp), lambda b, h, i: (pb(b), h, 0, 0)))
    ins += [kn, vn]
    specs += [pl.BlockSpec((seq_len, dk), lambda b, h, i: (ko + b, h)),
              pl.BlockSpec((seq_len, dv), lambda b, h, i: (ko + b, h))]
    if has_rope:
        ins.append(rn)
        specs.append(pl.BlockSpec((seq_len, LANES), lambda b, h, i: (ko + b, 0)))
    if has_bias:
        ins.append(fkn)
        specs.append(pl.BlockSpec((None, None, 1, seq_len), lambda b, h, i: (h, ko + b, 0, 0)))
    kern = functools.partial(_flash_kernel, n_past_blk=n_past_blk, tkp=tkp, tq=tq, has_bias=has_bias,
                             has_rope=has_rope, mask_mode=mask_mode, has_past=has_past, past_heads=past_heads,
                             single_q=nq == 1)
    return pl.pallas_call(
        kern, name="flash", grid=(n_seq, n_heads, nq), in_specs=specs,
        out_specs=pl.BlockSpec((tq, dv), lambda b, h, i: (b * nq + i, h)),
        out_shape=jax.ShapeDtypeStruct((n_seq * seq_len, n_heads * dv), BF16),
        scratch_shapes=[pltpu.VMEM((tq, LANES), F32), pltpu.VMEM((tq, dv + LANES), F32),
                        pltpu.VMEM((tq, tq) if nq > 1 else (HG_SUB, LANES), F32),
                        pltpu.VMEM((tq, tq) if nq > 1 else (HG_SUB, LANES), F32)],
        compiler_params=_params(("parallel", "parallel", "arbitrary")))(*ins)


def _route(sc, sb):
    def top2_sum(v):
        a, b, c, d = v
        a, b = jnp.maximum(a, b), jnp.minimum(a, b)
        c, d = jnp.maximum(c, d), jnp.minimum(c, d)
        hi, lo2 = jnp.maximum(a, c), jnp.minimum(a, c)
        return hi + jnp.maximum(lo2, jnp.maximum(b, d))

    gs = [top2_sum(sb[g * EXPERTS_PER_GROUP:(g + 1) * EXPERTS_PER_GROUP]) for g in range(N_GROUPS)]
    best_v, best_g = gs[0], jnp.zeros(gs[0].shape, jnp.int32)
    for g in range(1, N_GROUPS):
        upd = gs[g] > best_v
        best_v = jnp.where(upd, gs[g], best_v)
        best_g = jnp.where(upd, g, best_g)
    masked = [jnp.where(best_g == (e // EXPERTS_PER_GROUP), sb[e], -jnp.inf) for e in range(N_EXPERTS)]

    def argmax_first(vals, exclude=None):
        bv = jnp.full(vals[0].shape, -jnp.inf, F32)
        bi = jnp.full(vals[0].shape, -1, jnp.int32)
        for e, v in enumerate(vals):
            upd = v > bv
            if exclude is not None:
                upd = upd & (exclude != e)
            bv = jnp.where(upd, v, bv)
            bi = jnp.where(upd, e, bi)
        return bi

    i1 = argmax_first(masked)
    i2 = argmax_first(masked, exclude=i1)
    w1 = sum(jnp.where(i1 == e, sc[e], 0.0) for e in range(N_EXPERTS))
    w2 = sum(jnp.where(i2 == e, sc[e], 0.0) for e in range(N_EXPERTS))
    tot = w1 + w2
    w1, w2 = w1 / tot, w2 / tot
    comb = [jnp.where(i1 == e, w1, 0.0) + jnp.where(i2 == e, w2, 0.0) for e in range(N_EXPERTS)]
    return comb + [i1.astype(F32), i2.astype(F32), w1, w2]


def _mix_kernel(*refs, n_act):
    x_ref = refs[0]
    a_refs = refs[1:1 + n_act]
    w_ref, g_ref, b_ref, rw2_ref, rb_ref, x1_ref, x1p_ref, comb_ref, ct_scr = refs[1 + n_act:]
    half = D_MODEL // 2
    ys = []
    for n0 in (0, half):
        y = None
        k0 = 0
        for a_ref in a_refs:
            kw = a_ref.shape[1]
            part = _dot(a_ref[...], w_ref[k0:k0 + kw, n0:n0 + half])
            y = part if y is None else y + part
            k0 += kw
        ys.append(y)
    x1 = _layer_norm(ALPHA * x_ref[...] + jnp.concatenate(ys, axis=1), g_ref[...], b_ref[...])
    x1_ref[...] = x1

    x1p_ref[...] = _pack_pair(x1[:, :half], x1[:, half:])
    x_hi = x1.astype(BF16)
    x_lo = (x1 - x_hi.astype(F32)).astype(BF16)
    l2 = _dot(x_hi, rw2_ref[...])
    logits = l2[:, :LANES] + l2[:, LANES:] + _dot(x_lo, rw2_ref[:, :LANES])
    scores_t = _sigmoid(logits).T
    sc = [scores_t[e:e + 1, :] for e in range(N_EXPERTS)]
    sb = [sc[e] + rb_ref[e:e + 1, :] for e in range(N_EXPERTS)]
    route_rows = _route(sc, sb)
    ct_scr[...] = jnp.zeros(ct_scr.shape, F32)
    for r, val in enumerate(route_rows):
        ct_scr[r:r + 1, :] = val
    comb_ref[...] = ct_scr[...].T


def _mix(x, acts, w_out, ln_g, ln_b, rw2, rb, tm):
    rows = x.shape[0]
    return _row_call(functools.partial(_mix_kernel, n_act=len(acts)), "mix", rows, tm,
                     [x] + list(acts), [w_out, ln_g, ln_b, rw2, rb],
                     [(D_MODEL, F32), (D_MODEL // 2, jnp.uint32), (LANES, F32)],
                     scratch=[pltpu.VMEM((LANES, tm), F32)])


def _moe_kernel(x_ref, comb_ref, wg_ref, wu_ref, wd_ref, g_ref, b_ref, o_ref, xb_scr, acc_scr):
    e = pl.program_id(1)

    @pl.when(e == 0)
    def _():
        xb_scr[...] = x_ref[...].astype(BF16)
        acc_scr[...] = jnp.zeros(acc_scr.shape, F32)

    xb = xb_scr[...]
    lane = lax.broadcasted_iota(jnp.int32, (1, LANES), 1)
    c_e = jnp.sum(jnp.where(lane == e, comb_ref[...], 0.0), axis=1, keepdims=True)
    gate = _dot(xb, wg_ref[...])
    h = gate * _sigmoid(gate) * _dot(xb, wu_ref[...])
    acc_scr[...] += _dot((h * c_e).astype(BF16), wd_ref[...])

    @pl.when(e == N_EXPERTS - 1)
    def _():
        o_ref[...] = _layer_norm(ALPHA * x_ref[...] + acc_scr[...], g_ref[...], b_ref[...])


def _moe(x, comb, wg, wu, wd, ln_g, ln_b, tm):
    rows = x.shape[0]
    assert rows % tm == 0
    return pl.pallas_call(
        _moe_kernel, name="moe", grid=(rows // tm, N_EXPERTS),
        in_specs=[pl.BlockSpec((tm, D_MODEL), lambda i, e: (i, 0)),
                  pl.BlockSpec((tm, LANES), lambda i, e: (i, 0)),
                  pl.BlockSpec((None, D_MODEL, D_EXPERT), lambda i, e: (e, 0, 0)),
                  pl.BlockSpec((None, D_MODEL, D_EXPERT), lambda i, e: (e, 0, 0)),
                  pl.BlockSpec((None, D_EXPERT, D_MODEL), lambda i, e: (e, 0, 0)),
                  _full_spec(ln_g), _full_spec(ln_b)],
        out_specs=pl.BlockSpec((tm, D_MODEL), lambda i, e: (i, 0)),
        out_shape=jax.ShapeDtypeStruct((rows, D_MODEL), F32),
        scratch_shapes=[pltpu.VMEM((tm, D_MODEL), BF16), pltpu.VMEM((tm, D_MODEL), F32)],
        compiler_params=_params(("parallel", "arbitrary")))(x, comb, wg, wu, wd, ln_g, ln_b)


ROUTE_E1, ROUTE_E2, ROUTE_W1, ROUTE_W2 = N_EXPERTS, N_EXPERTS + 1, N_EXPERTS + 2, N_EXPERTS + 3
TE = 512
SC_WINDOW = 128


def _pack_pair(a, b):
    au = lax.bitcast_convert_type(a.astype(BF16).astype(F32), jnp.uint32)
    bu = lax.bitcast_convert_type(b.astype(BF16).astype(F32), jnp.uint32)
    return (au >> 16) | (bu & jnp.uint32(0xFFFF0000))


def _unpack_pair(w):
    a = lax.bitcast_convert_type(w << 16, F32)
    b = lax.bitcast_convert_type(w & jnp.uint32(0xFFFF0000), F32)
    return a, b


def _rank_kernel(route_ref, pos_ref, texp_ref, nused_ref, cnt_scr, carry_scr, seg_scr, before_scr):
    ph, i = pl.program_id(0), pl.program_id(1)
    T = route_ref.shape[0]
    lane = lax.broadcasted_iota(jnp.int32, (1, LANES), 1)
    lane_f = lane.astype(F32)
    r = route_ref[...]
    e1, e2 = r[:, ROUTE_E1:ROUTE_E1 + 1], r[:, ROUTE_E2:ROUTE_E2 + 1]
    m1, m2 = lane_f == e1, lane_f == e2
    m = jnp.where(m1 | m2, 1.0, 0.0)
    colsum = jnp.sum(m, axis=0, keepdims=True)

    @pl.when((ph == 0) & (i == 0))
    def _():
        cnt_scr[...] = jnp.zeros(cnt_scr.shape, F32)

    @pl.when(ph == 0)
    def _():
        cnt_scr[...] += colsum

    @pl.when((ph == 1) & (i == 0))
    def _():
        cnt = cnt_scr[...].astype(jnp.int32)
        padded = (((cnt + (TE - 1)) // TE) * TE).astype(F32)
        rr = lax.broadcasted_iota(jnp.int32, (LANES, 1), 0)
        upper = (rr < lane).astype(BF16)
        hi, mid, lo = _split3(jnp.broadcast_to(padded, (HG_SUB, LANES)))
        seg = (_dot(hi, upper) + _dot(mid, upper) + _dot(lo, upper))[:1, :]
        seg_scr[...] = seg
        carry_scr[...] = jnp.zeros(carry_scr.shape, F32)
        seg_end = seg + padded
        tile_row = lax.broadcasted_iota(jnp.int32, texp_ref.shape, 1).astype(F32) * float(TE)
        te_acc = jnp.zeros(texp_ref.shape, jnp.int32)
        for e in range(N_EXPERTS):
            te_acc = te_acc + jnp.where(seg_end[:, e:e + 1] <= tile_row, 1, 0)
        texp_ref[...] = jnp.minimum(te_acc, N_EXPERTS - 1)
        nused_ref[...] = jnp.broadcast_to(seg_end[:, N_EXPERTS - 1:N_EXPERTS] / float(TE), nused_ref.shape).astype(jnp.int32)

    @pl.when((ph == 1) & (i == 0))
    def _():
        row = lax.broadcasted_iota(jnp.int32, (T, 1), 0)
        col = lax.broadcasted_iota(jnp.int32, (1, T), 1)
        before_scr[...] = (col < row).astype(BF16)

    @pl.when(ph == 1)
    def _():
        cum = _dot(before_scr[...], m.astype(BF16)) + carry_scr[...] + seg_scr[...]
        p1 = jnp.sum(jnp.where(m1, cum, 0.0), axis=1, keepdims=True)
        p2 = jnp.sum(jnp.where(m2, cum, 0.0), axis=1, keepdims=True)
        pos_ref[...] = jnp.where(lane == 0, p1, jnp.where(lane == 1, p2, 0.0)).astype(jnp.int32)
        carry_scr[...] += colsum


def _rank(route, n_tiles, tm):
    rows = route.shape[0]
    nb = rows // tm
    nt_pad = -(-n_tiles // LANES) * LANES
    return pl.pallas_call(
        _rank_kernel, name="rank", grid=(2, nb),
        in_specs=[pl.BlockSpec((tm, LANES), lambda ph, i: (i, 0))],
        out_specs=[pl.BlockSpec((tm, LANES), lambda ph, i: (i * ph, 0)),
                   pl.BlockSpec((1, nt_pad), lambda ph, i: (0, 0)),
                   pl.BlockSpec((1, LANES), lambda ph, i: (0, 0))],
        out_shape=[jax.ShapeDtypeStruct((rows, LANES), jnp.int32),
                   jax.ShapeDtypeStruct((1, nt_pad), jnp.int32),
                   jax.ShapeDtypeStruct((1, LANES), jnp.int32)],
        scratch_shapes=[pltpu.VMEM((1, LANES), F32), pltpu.VMEM((1, LANES), F32), pltpu.VMEM((1, LANES), F32),
                        pltpu.VMEM((tm, tm), BF16)],
        compiler_params=_params(("arbitrary", "arbitrary")))(route)


def _sc_mesh():
    return plsc.VectorSubcoreMesh(core_axis_name="c", subcore_axis_name="s")


def _sc_scatter_rows(x, idx, n_out):
    rows, d = x.shape
    mesh = _sc_mesh()
    n_workers = mesh.num_cores * mesh.num_subcores
    steps = idx.shape[1] // SC_WINDOW // n_workers
    assert steps * SC_WINDOW * n_workers == idx.shape[1] and rows % SC_WINDOW == 0

    @functools.partial(pl.kernel, out_type=jax.ShapeDtypeStruct((n_out, d), x.dtype), mesh=mesh,
                       scratch_types=[pltpu.VMEM((1, SC_WINDOW), jnp.int32), pltpu.VMEM((SC_WINDOW, d), x.dtype)])
    def scatter(x_hbm, i_hbm, o_hbm, i_vmem, buf):
        first = (lax.axis_index("c") * mesh.num_subcores + lax.axis_index("s")) * steps

        @pl.loop(0, steps)
        def _(t):
            off = (first + t) * SC_WINDOW
            pltpu.sync_copy(i_hbm.at[:, pl.ds(off, SC_WINDOW)], i_vmem)
            pltpu.sync_copy(x_hbm.at[pl.ds(off % rows, SC_WINDOW)], buf)
            pltpu.sync_copy(buf, o_hbm.at[i_vmem.at[0]])

    return scatter(x, idx)


def _sc_gather_rows(x, idx):
    d = x.shape[1]
    n = idx.shape[1]
    mesh = _sc_mesh()
    n_workers = mesh.num_cores * mesh.num_subcores
    steps = n // SC_WINDOW // n_workers
    assert steps * SC_WINDOW * n_workers == n

    @functools.partial(pl.kernel, out_type=jax.ShapeDtypeStruct((n, d), x.dtype), mesh=mesh,
                       scratch_types=[pltpu.VMEM((1, SC_WINDOW), jnp.int32), pltpu.VMEM((SC_WINDOW, d), x.dtype)])
    def gather(x_hbm, i_hbm, o_hbm, i_vmem, buf):
        first = (lax.axis_index("c") * mesh.num_subcores + lax.axis_index("s")) * steps

        @pl.loop(0, steps)
        def _(t):
            off = (first + t) * SC_WINDOW
            pltpu.sync_copy(i_hbm.at[:, pl.ds(off, SC_WINDOW)], i_vmem)
            pltpu.sync_copy(x_hbm.at[i_vmem.at[0]], buf)
            pltpu.sync_copy(buf, o_hbm.at[pl.ds(off, SC_WINDOW)])

    return gather(x, idx)


def _gmm_kernel(texp_ref, nused_ref, x_ref, wg_ref, wu_ref, wd_ref, o_ref):
    @pl.when(pl.program_id(0) < nused_ref[0])
    def _():
        a, b = _unpack_pair(x_ref[...])
        xb = jnp.concatenate([a.astype(BF16), b.astype(BF16)], axis=1)
        gate = _dot(xb, wg_ref[...])
        h = gate * _sigmoid(gate) * _dot(xb, wu_ref[...])
        y = _dot(h.astype(BF16), wd_ref[...])
        o_ref[...] = _pack_pair(y[:, :D_MODEL // 2], y[:, D_MODEL // 2:])


def _gmm(xs, texp, nused, wg, wu, wd):
    rows = xs.shape[0]
    wmap = lambda d, te, nu: (te[d], 0, 0)
    grid_spec = pltpu.PrefetchScalarGridSpec(
        num_scalar_prefetch=2, grid=(rows // TE,),
        in_specs=[pl.BlockSpec((TE, D_MODEL // 2), lambda d, te, nu: (d, 0)),
                  pl.BlockSpec((None, D_MODEL, D_EXPERT), wmap),
                  pl.BlockSpec((None, D_MODEL, D_EXPERT), wmap),
                  pl.BlockSpec((None, D_EXPERT, D_MODEL), wmap)],
        out_specs=pl.BlockSpec((TE, D_MODEL // 2), lambda d, te, nu: (d, 0)))
    return pl.pallas_call(
        _gmm_kernel, name="gmm", grid_spec=grid_spec,
        out_shape=jax.ShapeDtypeStruct((rows, D_MODEL // 2), jnp.uint32),
        compiler_params=_params(("arbitrary",)))(texp, nused, xs, wg, wu, wd)


def _combine_kernel(x_ref, g0_ref, g1_ref, route_ref, g_ref, b_ref, o_ref):
    r = route_ref[...]
    y0 = jnp.concatenate(_unpack_pair(g0_ref[...]), axis=1)
    y1 = jnp.concatenate(_unpack_pair(g1_ref[...]), axis=1)
    f = y0 * r[:, ROUTE_W1:ROUTE_W1 + 1] + y1 * r[:, ROUTE_W2:ROUTE_W2 + 1]
    o_ref[...] = _layer_norm(ALPHA * x_ref[...] + f, g_ref[...], b_ref[...])


def _combine(x, g, route, ln_g, ln_b, tm):
    rows = x.shape[0]
    nb = rows // tm
    return pl.pallas_call(
        _combine_kernel, name="combine", grid=(nb,),
        in_specs=[pl.BlockSpec((tm, D_MODEL), lambda i: (i, 0)),
                  pl.BlockSpec((tm, D_MODEL // 2), lambda i: (i, 0)),
                  pl.BlockSpec((tm, D_MODEL // 2), lambda i: (nb + i, 0)),
                  pl.BlockSpec((tm, LANES), lambda i: (i, 0)), _full_spec(ln_g), _full_spec(ln_b)],
        out_specs=pl.BlockSpec((tm, D_MODEL), lambda i: (i, 0)),
        out_shape=jax.ShapeDtypeStruct((rows, D_MODEL), F32),
        compiler_params=_params(("parallel",)))(x, g, g, route, ln_g, ln_b)


def _moe_routed(x1, x1b, route, wg, wu, wd, ln_g, ln_b, tm):
    rows = x1.shape[0]
    n_rows = 2 * rows + N_EXPERTS * TE
    pos, texp, nused = _rank(route, n_rows // TE, tm)
    idx = jnp.concatenate([pos[:, 0], pos[:, 1]])[None, :]
    xs = _sc_scatter_rows(x1b, idx, n_rows)
    ys = _gmm(xs, texp[0, :n_rows // TE], nused[0, :1], wg, wu, wd)
    g = _sc_gather_rows(ys, idx)
    return _combine(x1, g, route, ln_g, ln_b, tm)


def _rope128(x, cos_t, sin_t):
    lane = lax.broadcasted_iota(jnp.int32, (1, LANES), 1)
    half = MLA_ROPE // 2
    swapped = jnp.where(lane < half, pltpu.roll(x, LANES - half, axis=1), pltpu.roll(x, half, axis=1))
    return x * cos_t + swapped * sin_t


def _odd_proj_kernel(x_ref, cos_ref, sin_ref, w_ref, gq_ref, gkv_ref, wuq_ref,
                     q_ref, ckv_ref, kpe_ref, kpe16_ref):
    z = _dot(x_ref[...].astype(BF16), w_ref[...])
    cq = _rms_norm(z[:, :MLA_Q_LORA], gq_ref[...])
    ckv_ref[...] = _rms_norm(z[:, MLA_Q_LORA:MLA_Q_LORA + MLA_KV_LORA], gkv_ref[...])
    cos_t, sin_t = cos_ref[...], sin_ref[...]
    kpe = _rope128(z[:, MLA_Q_LORA + MLA_KV_LORA:], cos_t, sin_t)
    kpe_ref[...] = kpe[:, :MLA_ROPE]
    kpe16_ref[...] = kpe.astype(BF16)
    qf = _dot(cq.astype(BF16), wuq_ref[...])
    scale = (MLA_NOPE + MLA_ROPE) ** -0.5 * LOG2E
    for h in range(MLA_HEADS):
        c0 = h * MLA_QPAD
        q_ref[:, c0:c0 + MLA_NOPE] = (qf[:, c0:c0 + MLA_NOPE] * scale).astype(BF16)
        qr = _rope128(qf[:, c0 + MLA_NOPE:c0 + MLA_QPAD], cos_t, sin_t)
        q_ref[:, c0 + MLA_NOPE:c0 + MLA_QPAD] = (qr * scale).astype(BF16)


def _odd_proj(x, cos_t, sin_t, w_in, gq, gkv, wuq, tm):
    rows = x.shape[0]
    outs = [(MLA_HEADS * MLA_QPAD, BF16), (MLA_KV_LORA, F32), (MLA_ROPE, F32), (LANES, BF16)]
    return _row_call(_odd_proj_kernel, "odd_proj", rows, tm, [x, cos_t, sin_t], [w_in, gq, gkv, wuq], outs)


def _kv_expand_kernel(c_ref, w_ref, k_ref, v_ref):
    kv = _dot(c_ref[...].astype(BF16), w_ref[...])
    n = MLA_HEADS * MLA_NOPE
    k_ref[...] = kv[:, :n].astype(BF16)
    v_ref[...] = kv[:, n:].astype(BF16)


def _kv_expand(ckv, w_ukv, tm):
    rows = ckv.shape[0]
    return _row_call(_kv_expand_kernel, "kv_expand", rows, tm, [ckv], [w_ukv],
                     [(MLA_HEADS * MLA_NOPE, BF16), (MLA_HEADS * MLA_V, BF16)])


def _rope_tables(pos):
    half = MLA_ROPE // 2
    inv = ROPE_BASE ** (-jnp.arange(half, dtype=F32) / half)
    ang = pos.astype(F32)[:, None] * inv[None, :]
    cos, sin = jnp.cos(ang), jnp.sin(ang)
    z = jnp.zeros((pos.shape[0], LANES - MLA_ROPE), F32)
    return jnp.concatenate([cos, cos, z], axis=1), jnp.concatenate([-sin, sin, z], axis=1)


def _pad_rows(a, n):
    return jnp.pad(a, ((0, n - a.shape[0]),) + ((0, 0),) * (a.ndim - 1))


def kernel(x_prompt, x_sample, state_hgrn2, cache_fox_k, cache_fox_v, cache_fox_logf, cache_mla_ckv, cache_mla_kpe, meta_tokens, even_w_in, hg_lb_logits, hg_norm_g, fox_forget_bias, even_w_out, mla_w_in, mla_q_norm_g, mla_kv_norm_g, mla_w_uq, mla_w_uk, mla_w_uv, mla_w_out, ln_mix_g, ln_mix_b, ln_ffn_g, ln_ffn_b, router_w, router_bias, moe_w_gate, moe_w_up, moe_w_down):
    B, T, _ = x_prompt.shape
    Bs, Ts, _ = x_sample.shape
    P = cache_fox_k.shape[2]
    RM = B * T
    RS = Bs * Ts
    RSM = -(-(RS + N_META) // LANES) * LANES
    ME = slice(RS, RS + N_META)
    TM_MAIN, TM_MOE, TQ = 512, 1024, 512

    xm = x_prompt.reshape(RM, D_MODEL)
    xs = _pad_rows(jnp.concatenate([x_sample.reshape(RS, D_MODEL), meta_tokens.astype(F32)], axis=0), RSM)

    w_in0 = even_w_in[0]
    n_main = 7 * HG_W
    w_even = w_in0[:, :n_main].astype(BF16)
    w_even_f = jnp.pad(w_in0[:, n_main:], ((0, 0), (0, LANES - FOX_HEADS))).astype(BF16)
    fb_pad = jnp.pad(fox_forget_bias[0][None, :], ((0, 0), (0, LANES - FOX_HEADS)))
    g_hg = hg_norm_g[0].reshape(1, HG_W)
    w_out0 = even_w_out[0].astype(BF16)
    e_mat = ((jnp.arange(HG_SUB * HG_DK)[:, None] // HG_DK) == (jnp.arange(CHUNK)[None, :] % HG_SUB)).astype(BF16)

    w_odd = jnp.pad(mla_w_in[0], ((0, 0), (0, LANES - MLA_ROPE))).astype(BF16)
    gq = mla_q_norm_g[0][None, :]
    gkv = mla_kv_norm_g[0][None, :]
    wuq = mla_w_uq[0].reshape(MLA_Q_LORA, MLA_HEADS, MLA_NOPE + MLA_ROPE)
    wuq = jnp.pad(wuq, ((0, 0), (0, 0), (0, MLA_QPAD - MLA_NOPE - MLA_ROPE)))
    wuq = wuq.reshape(MLA_Q_LORA, MLA_HEADS * MLA_QPAD).astype(BF16)
    w_ukv = jnp.concatenate([mla_w_uk[0].reshape(MLA_KV_LORA, -1), mla_w_uv[0].reshape(MLA_KV_LORA, -1)],
                            axis=1).astype(BF16)
    w_out1 = mla_w_out[0].astype(BF16)

    rw = jnp.pad(router_w, ((0, 0), (0, LANES - N_EXPERTS)))
    rw_hi = rw.astype(BF16)
    rw2 = jnp.concatenate([rw_hi, (rw - rw_hi.astype(F32)).astype(BF16)], axis=1)
    rb = jnp.pad(router_bias.astype(F32)[:, None], ((0, LANES - N_EXPERTS), (0, 0)))
    wg = moe_w_gate.astype(BF16)
    wu = moe_w_up.astype(BF16)
    wd = moe_w_down.astype(BF16)
    row2 = lambda a: a[None, :]

    def ffn(x, acts, w_out, l, tm_mix, tm_moe, routed):
        x1, x1b, route = _mix(x, acts, w_out, row2(ln_mix_g[l]), row2(ln_mix_b[l]), rw2, rb, tm_mix)
        ln = (row2(ln_ffn_g[l]), row2(ln_ffn_b[l]))
        if routed:
            return _moe_routed(x1, x1b, route, wg[l], wu[l], wd[l], *ln, tm_mix)
        return _moe(x1, route, wg[l], wu[l], wd[l], *ln, tm_moe)

    pm = _even_proj(xm, w_even, w_even_f, hg_lb_logits, fb_pad, TM_MAIN, 0)
    ps = _even_proj(xs, w_even, w_even_f, hg_lb_logits, fb_pad, RSM, 0)
    names = ("hq", "lf", "hk", "hv", "hgate", "fq", "fk", "fv", "fk16", "fv16", "flf")
    pm = dict(zip(names, pm))
    ps = dict(zip(names, ps))

    hg_keys = ("hq", "lf", "hk", "hv", "hgate", "flf")
    meta_in = [_pad_rows(ps[n][ME], CHUNK) for n in hg_keys]
    zero_s = jnp.zeros((1, HG_HEADS, HG_DK, HG_DV), F32)
    zero_f = jnp.zeros((1, 1, LANES), F32)
    o_hg_meta, fc_meta, s_meta = _hgrn2(*meta_in, g_hg, e_mat, zero_s, zero_f, 1, CHUNK, 0, CHUNK)
    o_hg_meta, fc_meta = o_hg_meta[:N_META], fc_meta[:N_META]
    f_meta_end = fc_meta[N_META - 1:N_META][None]

    o_hg_m, fc_m, s_main = _hgrn2(*[pm[n] for n in hg_keys], g_hg, e_mat, s_meta, f_meta_end, B, T, 0, 256)

    logf_c = jnp.pad(jnp.transpose(cache_fox_logf[0], (0, 2, 1)), ((0, 0), (0, HG_SUB - FOX_HEADS), (0, 0)))
    fpast = _cumsum_lanes(logf_c, 512)[:, :FOX_HEADS, :]
    f0_s = jnp.pad(fpast[:, :, P - 1][:, None, :], ((0, 0), (0, 0), (0, LANES - FOX_HEADS)))
    o_hg_s, fc_s, s_samp = _hgrn2(*[ps[n] for n in hg_keys], g_hg, e_mat, state_hgrn2[0], f0_s, Bs, Ts, 0, CHUNK)

    def bias_layouts(fc, n_seq, seq_len):
        f4 = fc[:, :FOX_HEADS].T
        return f4[:, :, None], f4.reshape(FOX_HEADS, n_seq, 1, seq_len)

    fq_m, fk_m = bias_layouts(fc_m, B, T)
    fq_s, fk_s = bias_layouts(fc_s, Bs, Ts)
    fq_t, fk_t = bias_layouts(fc_meta, 1, N_META)

    fox_kw = dict(n_heads=FOX_HEADS, dq=FOX_DH, dk=FOX_DH, dv=FOX_DH, mask_mode="causal")
    meta_past = dict(k=ps["fk16"][ME][None], v=ps["fv16"][ME][None],
                     fk=jnp.transpose(fk_t, (1, 0, 2, 3)), tk=N_META)
    o_fox_m = _flash(pm["fq"], pm["fk16"], pm["fv16"], n_seq=B, seq_len=T, tq=TQ, q_off=0, k_off=0,
                     fq=fq_m, fkn=fk_m, past=meta_past, **fox_kw)
    samp_past = dict(k=cache_fox_k[0], v=cache_fox_v[0],
                     fk=fpast[:, :, None, :], tk=1024)
    o_fox_s = _flash(ps["fq"], ps["fk16"], ps["fv16"], n_seq=Bs, seq_len=Ts, tq=Ts, q_off=0, k_off=0,
                     fq=fq_s, fkn=fk_s, past=samp_past, **fox_kw)
    o_fox_t = _flash(ps["fq"][ME], ps["fk16"][ME], ps["fv16"][ME], n_seq=1, seq_len=N_META, tq=N_META,
                     q_off=0, k_off=0, fq=fq_t, fkn=fk_t, **fox_kw)

    o_hg_small = _pad_rows(jnp.concatenate([o_hg_s, o_hg_meta], axis=0), RSM)
    o_fox_small = _pad_rows(jnp.concatenate([o_fox_s, o_fox_t], axis=0), RSM)
    xm = ffn(xm, [o_hg_m, o_fox_m], w_out0, 0, TM_MAIN, TM_MOE, True)
    xs = ffn(xs, [o_hg_small, o_fox_small], w_out0, 0, RSM, RSM, False)

    cos_m, sin_m = _rope_tables(N_META + jnp.arange(T, dtype=jnp.int32))
    pos_small = _pad_rows(jnp.concatenate([jnp.tile(P + jnp.arange(Ts, dtype=jnp.int32), Bs),
                                           jnp.arange(N_META, dtype=jnp.int32)]), RSM)
    cos_s, sin_s = _rope_tables(pos_small)
    qm, ckv_m, kpe_m, kpe16_m = _odd_proj(xm, jnp.tile(cos_m, (B, 1)), jnp.tile(sin_m, (B, 1)),
                                          w_odd, gq, gkv, wuq, TM_MAIN)
    qs, ckv_s, kpe_s, kpe16_s = _odd_proj(xs, cos_s, sin_s, w_odd, gq, gkv, wuq, RSM)
    kn_m, vn_m = _kv_expand(ckv_m, w_ukv, 1024)
    kn_s, vn_s = _kv_expand(ckv_s, w_ukv, RSM)
    kp_c, vp_c = _kv_expand(cache_mla_ckv[0].reshape(Bs * P, MLA_KV_LORA), w_ukv, 1024)
    rp_c = jnp.pad(cache_mla_kpe[0], ((0, 0), (0, 0), (0, LANES - MLA_ROPE))).astype(BF16)

    mla_kw = dict(n_heads=MLA_HEADS, dq=MLA_QPAD, dk=MLA_NOPE, dv=MLA_V)
    meta_past = dict(k=kn_s[ME][None], v=vn_s[ME][None], r=kpe16_s[ME][None], tk=N_META)
    o_m = _flash(qm, kn_m, vn_m, n_seq=B, seq_len=T, tq=TQ, q_off=0, k_off=0, rn=kpe16_m,
                 past=meta_past, mask_mode="chunk", **mla_kw)
    samp_past = dict(k=kp_c.reshape(Bs, P, -1), v=vp_c.reshape(Bs, P, -1), r=rp_c, tk=1024)
    o_s = _flash(qs, kn_s, vn_s, n_seq=Bs, seq_len=Ts, tq=Ts, q_off=0, k_off=0, rn=kpe16_s,
                 past=samp_past, mask_mode="full", **mla_kw)
    o_t = _flash(qs[ME], kn_s[ME], vn_s[ME], n_seq=1, seq_len=N_META, tq=N_META, q_off=0, k_off=0,
                 rn=kpe16_s[ME], mask_mode="full", **mla_kw)
    xm = ffn(xm, [o_m], w_out1, 1, TM_MAIN, TM_MOE, True)
    xs = ffn(xs, [_pad_rows(jnp.concatenate([o_s, o_t], axis=0), RSM)], w_out1, 1, RSM, RSM, False)

    def with_meta(main, small, *width):
        meta = jnp.broadcast_to(small[ME][None], (B, N_META) + width)
        return jnp.concatenate([meta, main.reshape((B, T) + width)], axis=1)

    y_prompt = xm.reshape(B, T, D_MODEL)
    y_sample = xs[:RS].reshape(Bs, Ts, D_MODEL)
    hg_p = s_main[None]
    fk_p = with_meta(pm["fk"], ps["fk"], FOX_HEADS, FOX_DH)[None]
    fv_p = with_meta(pm["fv"], ps["fv"], FOX_HEADS, FOX_DH)[None]
    flf_p = with_meta(pm["flf"][:, :FOX_HEADS], ps["flf"][:, :FOX_HEADS], FOX_HEADS)[None]
    ckv_p = with_meta(ckv_m, ckv_s, MLA_KV_LORA)[None]
    kpe_p = with_meta(kpe_m, kpe_s, MLA_ROPE)[None]
    hg_s = s_samp[None]
    fk_s_out = ps["fk"][:RS].reshape(1, Bs, Ts, FOX_HEADS, FOX_DH)
    fv_s_out = ps["fv"][:RS].reshape(1, Bs, Ts, FOX_HEADS, FOX_DH)
    flf_s = ps["flf"][:RS, :FOX_HEADS].reshape(1, Bs, Ts, FOX_HEADS)
    ckv_so = ckv_s[:RS].reshape(1, Bs, Ts, MLA_KV_LORA)
    kpe_so = kpe_s[:RS].reshape(1, Bs, Ts, MLA_ROPE)
    return (y_prompt, y_sample, hg_p, fk_p, fv_p, flf_p, ckv_p, kpe_p,
            hg_s, fk_s_out, fv_s_out, flf_s, ckv_so, kpe_so)
```

```python
import functools

import jax
import jax.numpy as jnp
from jax import lax
from jax.experimental import pallas as pl
from jax.experimental.pallas import tpu as pltpu
from jax.experimental.pallas import tpu_sc as plsc

D_MODEL = 1024
CHUNK = 64
N_META = 16
HG_HEADS = 4
HG_DK = 128
HG_DV = 128
HG_W = HG_HEADS * HG_DK
FOX_HEADS = 4
FOX_DH = 128
FOX_W = FOX_HEADS * FOX_DH
MLA_HEADS = 8
MLA_Q_LORA = 512
MLA_KV_LORA = 256
MLA_NOPE = 128
MLA_ROPE = 64
MLA_V = 128
MLA_QPAD = 256
ROPE_BASE = 10000.0
N_EXPERTS = 16
N_GROUPS = 4
EXPERTS_PER_GROUP = 4
D_EXPERT = 256
DEPTH = 2
ALPHA = (2 * DEPTH) ** 0.25
LN_EPS = 1e-5
RMS_EPS = 1e-6

LANES = 128
HG_SUB = 8
HG_GROUP = 4
NEG = -1e30
LOG2E = 1.4426950408889634
F32 = jnp.float32
BF16 = jnp.bfloat16
VMEM_LIMIT = 56 * 1024 * 1024


def _dot(a, b):
    return jnp.dot(a, b, preferred_element_type=F32)


def _dot_nt(a, b):
    return lax.dot_general(a, b, (((1,), (1,)), ((), ())), preferred_element_type=F32)


def _dot_tn(a, b):
    return lax.dot_general(a, b, (((0,), (0,)), ((), ())), preferred_element_type=F32)


def _split3(x):
    hi = x.astype(BF16)
    r = x - hi.astype(F32)
    mid = r.astype(BF16)
    lo = (r - mid.astype(F32)).astype(BF16)
    return hi, mid, lo


def _cumsum_rows(tri, x):
    hi, mid, lo = _split3(x)
    return _dot(tri, hi) + _dot(tri, mid) + _dot(tri, lo)


def _sigmoid(x):
    return 1.0 / (1.0 + jnp.exp(-x))


def _log_sigmoid(x):
    return jnp.minimum(x, 0.0) - jnp.log(1.0 + jnp.exp(-jnp.abs(x)))


def _layer_norm(x, g, b):
    mu = jnp.mean(x, axis=-1, keepdims=True)
    xc = x - mu
    var = jnp.mean(xc * xc, axis=-1, keepdims=True)
    return xc * lax.rsqrt(var + LN_EPS) * g + b


def _rms_norm(x, g):
    return x * lax.rsqrt(jnp.mean(x * x, axis=-1, keepdims=True) + RMS_EPS) * g


def _params(sem):
    return pltpu.CompilerParams(dimension_semantics=sem, vmem_limit_bytes=VMEM_LIMIT)


def _full_spec(a):
    nd = a.ndim
    return pl.BlockSpec(a.shape, lambda *_: (0,) * nd)


def _row_call(kernel, name, rows, tm, row_ins, full_ins, outs, scratch=()):
    assert rows % tm == 0
    in_specs = [pl.BlockSpec((tm, a.shape[1]), lambda i: (i, 0)) for a in row_ins]
    in_specs += [_full_spec(a) for a in full_ins]
    trail = [c if isinstance(c, tuple) else (c,) for c, _ in outs]
    out_specs = [pl.BlockSpec((tm,) + t, lambda i, n=len(t): (i,) + (0,) * n) for t in trail]
    out_shape = [jax.ShapeDtypeStruct((rows,) + t, dt) for t, (_, dt) in zip(trail, outs)]
    return pl.pallas_call(
        kernel, name=name, grid=(rows // tm,), in_specs=in_specs, out_specs=out_specs,
        out_shape=out_shape, scratch_shapes=list(scratch),
        compiler_params=_params(("parallel",)))(*row_ins, *full_ins)


def _even_proj_kernel(x_ref, w_ref, wf_ref, lbl_ref, fb_ref,
                      hq_ref, lf_ref, hk_ref, hv_ref, hgate_ref,
                      fq_ref, fk_ref, fv_ref, fk16_ref, fv16_ref, flf_ref, *, layer):
    xb = x_ref[...].astype(BF16)

    def blk(j):
        return _dot(xb, w_ref[:, j * HG_W:(j + 1) * HG_W])

    logits = lbl_ref[...]
    e = jnp.exp(logits - jnp.max(logits, axis=0, keepdims=True))
    lb = jnp.sum(e[:layer + 1], axis=0, keepdims=True) / jnp.sum(e, axis=0, keepdims=True)

    hq_ref[...] = blk(0).astype(BF16)
    zf = blk(1)
    lf_ref[...] = jnp.log(lb + (1.0 - lb) * _sigmoid(zf))
    hk_ref[...] = ((1.0 - lb) * _sigmoid(-zf)).astype(BF16)
    hv_ref[...] = blk(2).astype(BF16)
    hgate_ref[...] = _sigmoid(blk(3)).astype(BF16)
    fq_ref[...] = (blk(4) * (FOX_DH ** -0.5 * LOG2E)).astype(BF16)
    fk = blk(5)
    fk16_ref[...] = fk.astype(BF16)
    fv = blk(6)
    fv16_ref[...] = fv.astype(BF16)
    for h in range(FOX_HEADS):
        fk_ref[:, h, :] = fk[:, h * FOX_DH:(h + 1) * FOX_DH]
        fv_ref[:, h, :] = fv[:, h * FOX_DH:(h + 1) * FOX_DH]
    flf_ref[...] = _log_sigmoid(_dot(xb, wf_ref[...]) + fb_ref[...])


def _even_proj(x, w_main, w_f, lb_logits, fb_pad, tm, layer):
    rows = x.shape[0]
    outs = [(HG_W, BF16), (HG_W, F32), (HG_W, BF16), (HG_W, BF16), (HG_W, BF16),
            (FOX_W, BF16), ((FOX_HEADS, FOX_DH), F32), ((FOX_HEADS, FOX_DH), F32), (FOX_W, BF16), (FOX_W, BF16),
            (LANES, F32)]
    return _row_call(functools.partial(_even_proj_kernel, layer=layer), "even_proj", rows, tm,
                     [x], [w_main, w_f, lb_logits, fb_pad], outs)


def _bcast_sub(x, j):
    n, c = x.shape
    x3 = x.reshape(n // HG_SUB, HG_SUB, c)
    return jnp.broadcast_to(x3[:, j:j + 1, :], x3.shape).reshape(n, c)


def _level_ref(b, w):
    n, c = b.shape
    parts = [jnp.broadcast_to(b[m * 2 * w + w - 1:m * 2 * w + w, :], (2 * w, c)) for m in range(n // (2 * w))]
    return parts[0] if len(parts) == 1 else jnp.concatenate(parts, axis=0)


def _hgrn2_kernel(q_ref, lf_ref, k_ref, v_ref, gate_ref, flf_ref, g_ref, e_ref, s0_ref, f0_ref,
                  o_ref, fcum_ref, sout_ref, st_scr, fc_scr, *, n_chunks):
    i = pl.program_id(1)
    C = CHUNK

    @pl.when(i == 0)
    def _():
        for h in range(HG_HEADS):
            st_scr[h] = s0_ref[h].T
        fc_scr[...] = f0_ref[...]

    row = lax.broadcasted_iota(jnp.int32, (C, 1), 0)
    col = lax.broadcasted_iota(jnp.int32, (1, C), 1)
    tri = (col <= row).astype(BF16)
    same = lambda w: (row // w) == (col // w)
    levels = (32, 16, 8)

    for c in range(n_chunks):
        sl = slice(c * C, (c + 1) * C)
        fcum = _cumsum_rows(tri, flf_ref[sl, :]) + fc_scr[...]
        fcum_ref[sl, :] = fcum
        fc_scr[...] = fcum[C - 1:C, :]

        for h0 in range(0, HG_HEADS, HG_GROUP):
            gs = slice(h0 * HG_DK, (h0 + HG_GROUP) * HG_DK)
            b = _cumsum_rows(tri, lf_ref[sl, gs])
            q = q_ref[sl, gs].astype(F32)
            k = k_ref[sl, gs].astype(F32)
            v = v_ref[sl, gs]
            qb = (q * jnp.exp(b)).astype(BF16)
            b_last = b[C - 1:C, :]
            kd = (k * jnp.exp(b_last - b)).astype(BF16)
            e_last = jnp.exp(b_last)

            pjs = [(jnp.exp(jnp.where((row % HG_SUB) >= j, b - _bcast_sub(b, j), NEG)) * q
                    * _bcast_sub(k, j)).astype(BF16) for j in range(HG_SUB)]
            lv = []
            for w in levels:
                upper = (row % (2 * w)) >= w
                ew = jnp.exp(-jnp.abs(b - _level_ref(b, w)))
                lv.append((jnp.where(upper, q * ew, 0.0).astype(BF16), jnp.where(upper, 0.0, k * ew).astype(BF16)))

            for hh in range(HG_GROUP):
                h = h0 + hh
                hs = slice(hh * HG_DK, (hh + 1) * HG_DK)
                ho = slice(h * HG_DK, (h + 1) * HG_DK)
                a = jnp.where(same(HG_SUB), _dot(jnp.concatenate([p[:, hs] for p in pjs], axis=1), e_ref[...]), 0.0)
                for w, (qw, kw) in zip(levels, lv):
                    aw = _dot_nt(qw[:, hs], kw[:, hs])
                    a = a + (aw if 2 * w == C else jnp.where(same(2 * w), aw, 0.0))
                st = st_scr[h]
                vh = v[:, hs]
                o = _dot(a.astype(BF16), vh) + _dot_nt(qb[:, hs], st.astype(BF16))
                st_scr[h] = st * e_last[:, hs] + _dot_tn(vh, kd[:, hs])
                o = _rms_norm(o, g_ref[:, ho])
                o_ref[sl, ho] = (o * gate_ref[sl, ho].astype(F32)).astype(BF16)

    @pl.when(i == pl.num_programs(1) - 1)
    def _():
        for h in range(HG_HEADS):
            sout_ref[h] = st_scr[h].T


def _hgrn2(q, lf, k, v, gate, flf, g, e_mat, s0, f0, n_seq, seq_len, row_off, tb):
    assert seq_len % tb == 0 and tb % CHUNK == 0 and row_off % tb == 0
    nb = seq_len // tb
    off = row_off // tb
    per_seq = s0.shape[0] > 1
    rmap = lambda s, i: (off + s * nb + i, 0)
    omap = lambda s, i: (s * nb + i, 0)
    smap = (lambda s, i: (s, 0, 0, 0)) if per_seq else (lambda s, i: (0, 0, 0, 0))
    fmap = (lambda s, i: (s, 0, 0)) if per_seq else (lambda s, i: (0, 0, 0))
    in_specs = [pl.BlockSpec((tb, HG_W), rmap) for _ in range(5)]
    in_specs += [pl.BlockSpec((tb, LANES), rmap), _full_spec(g), _full_spec(e_mat),
                 pl.BlockSpec((None, HG_HEADS, HG_DK, HG_DV), smap), pl.BlockSpec((None, 1, LANES), fmap)]
    out_specs = [pl.BlockSpec((tb, HG_W), omap), pl.BlockSpec((tb, LANES), omap),
                 pl.BlockSpec((None, HG_HEADS, HG_DK, HG_DV), lambda s, i: (s, 0, 0, 0))]
    out_shape = [jax.ShapeDtypeStruct((n_seq * seq_len, HG_W), BF16),
                 jax.ShapeDtypeStruct((n_seq * seq_len, LANES), F32),
                 jax.ShapeDtypeStruct((n_seq, HG_HEADS, HG_DK, HG_DV), F32)]
    scratch = [pltpu.VMEM((HG_HEADS, HG_DV, HG_DK), F32), pltpu.VMEM((1, LANES), F32)]
    return pl.pallas_call(
        functools.partial(_hgrn2_kernel, n_chunks=tb // CHUNK), name="hgrn2",
        grid=(n_seq, nb), in_specs=in_specs, out_specs=out_specs, out_shape=out_shape,
        scratch_shapes=scratch, compiler_params=_params(("parallel", "arbitrary")))(
            q, lf, k, v, gate, flf, g, e_mat, s0, f0)


def _cumsum_kernel(x_ref, tri_ref, o_ref, carry):
    @pl.when(pl.program_id(1) == 0)
    def _():
        carry[...] = jnp.zeros_like(carry)

    hi, mid, lo = _split3(x_ref[...])
    tri = tri_ref[...]
    out = _dot(hi, tri) + _dot(mid, tri) + _dot(lo, tri) + carry[...]
    o_ref[...] = out
    carry[...] = out[:, out.shape[1] - 1:]


def _cumsum_lanes(x, tb):
    n_seq, r, seq_len = x.shape
    nb = seq_len // tb
    tri = (jnp.arange(tb)[:, None] <= jnp.arange(tb)[None, :]).astype(BF16)
    return pl.pallas_call(
        _cumsum_kernel, name="cumsum", grid=(n_seq, nb),
        in_specs=[pl.BlockSpec((None, r, tb), lambda s, i: (s, 0, i)), _full_spec(tri)],
        out_specs=pl.BlockSpec((None, r, tb), lambda s, i: (s, 0, i)),
        out_shape=jax.ShapeDtypeStruct(x.shape, F32),
        scratch_shapes=[pltpu.VMEM((r, 1), F32)],
        compiler_params=_params(("parallel", "arbitrary")))(x, tri)


def _flash_kernel(*refs, n_past_blk, tkp, tq, has_bias, has_rope, mask_mode, has_past, past_heads, single_q):
    it = iter(refs)
    q_ref = next(it)
    fq_ref = next(it) if has_bias else None
    if has_past:
        kp_ref, vp_ref = next(it), next(it)
        rp_ref = next(it) if has_rope else None
        fkp_ref = next(it) if has_bias else None
    kn_ref, vn_ref = next(it), next(it)
    rn_ref = next(it) if has_rope else None
    fkn_ref = next(it) if has_bias else None
    o_ref = next(it)
    m_scr, acc_scr, sa_scr, sb_scr = next(it), next(it), next(it), next(it)
    dv = o_ref.shape[1]

    qi = pl.program_id(2)
    q = q_ref[...]
    m_scr[...] = jnp.full(m_scr.shape, NEG, F32)
    acc_scr[...] = jnp.zeros(acc_scr.shape, F32)
    fq_b = jnp.broadcast_to(fq_ref[...] * LOG2E, (tq, LANES)) if has_bias else None

    def scores(k, r, fk):
        if has_rope:
            k = jnp.concatenate([k, r], axis=1)
        s = _dot_nt(q, k.astype(BF16))
        if has_bias:
            s = s + jnp.tile(fq_b, (1, s.shape[1] // LANES)) if s.shape[1] % LANES == 0 else s + fq_b[:, :1]
            s = s - fk * LOG2E
        return s

    def update(s, v, mask):
        if mask is not None:
            s = jnp.where(mask, s, NEG)
        m_prev = m_scr[...]
        m_new = jnp.maximum(m_prev, jnp.max(s, axis=1, keepdims=True))
        alpha = jnp.exp2(m_prev - m_new)
        if s.shape[1] % LANES == 0:
            p = jnp.exp2(s - jnp.tile(m_new, (1, s.shape[1] // LANES)))
        else:
            p = jnp.exp2(s - m_new[:, :1])
        v1 = jnp.concatenate([v.astype(BF16), jnp.ones((v.shape[0], LANES), BF16)], axis=1)
        acc_scr[...] = jnp.tile(alpha, (1, acc_scr.shape[1] // LANES)) * acc_scr[...] + _dot(p.astype(BF16), v1)
        m_scr[...] = m_new

    past_kv = (lambda ref, rs: ref[rs, pl.program_id(1), :]) if past_heads else (lambda ref, rs: ref[rs, :])

    def past_block(rs):
        return (past_kv(kp_ref, rs), rp_ref[rs, :] if has_rope else None, fkp_ref[:, rs] if has_bias else None)

    def new_block(rs):
        return (kn_ref[rs, :], rn_ref[rs, :] if has_rope else None, fkn_ref[:, rs] if has_bias else None)

    if has_past:
        if n_past_blk == 1:
            update(scores(*past_block(slice(None))), past_kv(vp_ref, slice(None)), None)
        else:
            def past_body(j, carry):
                rs = pl.ds(pl.multiple_of(j * tkp, tkp), tkp)
                update(scores(*past_block(rs)), past_kv(vp_ref, rs), None)
                return carry
            lax.fori_loop(0, n_past_blk, past_body, 0)

    row = lax.broadcasted_iota(jnp.int32, (tq, 1), 0)
    col = lax.broadcasted_iota(jnp.int32, (1, tq), 1)
    if mask_mode == "causal":
        mask = col <= row
    elif mask_mode == "chunk":
        mask = (col // CHUNK) <= (row // CHUNK)
    else:
        mask = None

    if single_q:
        update(scores(*new_block(slice(None))), vn_ref[...], mask)
    else:
        blk = lambda j: pl.ds(pl.multiple_of(j * tq, tq), tq)

        def fill(s_ref, j):
            s_ref[...] = scores(*new_block(blk(j)))

        def drain(s_ref, j, msk):
            update(s_ref[...], vn_ref[blk(j), :], msk)

        def pair(j0):
            fill(sb_scr, j0 + 1)
            drain(sa_scr, j0, None)
            fill(sa_scr, j0 + 2)
            drain(sb_scr, j0 + 1, None)

        fill(sa_scr, 0)
        n_quads = qi // 4

        def quad_body(i, carry):
            pair(4 * i)
            pair(4 * i + 2)
            return carry
        lax.fori_loop(0, n_quads, quad_body, 0)

        @pl.when(qi % 4 >= 2)
        def _():
            pair(4 * n_quads)

        @pl.when(qi % 2 == 0)
        def _():
            drain(sa_scr, qi, mask)

        @pl.when(qi % 2 == 1)
        def _():
            fill(sb_scr, qi)
            drain(sa_scr, qi - 1, None)
            drain(sb_scr, qi, mask)

    acc = acc_scr[...]
    o_ref[...] = (acc[:, :dv] / acc[:, dv:]).astype(o_ref.dtype)


def _flash(q, kn, vn, *, n_seq, n_heads, seq_len, tq, dq, dk, dv, q_off, k_off, mask_mode,
           fq=None, fkn=None, rn=None, past=None):
    assert seq_len % tq == 0 and q_off % tq == 0 and k_off % seq_len == 0
    nq = seq_len // tq
    qo = q_off // tq
    ko = k_off // seq_len
    has_bias = fq is not None
    has_rope = rn is not None
    has_past = past is not None
    ins, specs = [q], [pl.BlockSpec((tq, dq), lambda b, h, i: (qo + b * nq + i, h))]
    if has_bias:
        ins.append(fq)
        specs.append(pl.BlockSpec((None, tq, 1), lambda b, h, i: (h, qo + b * nq + i, 0)))
    n_past_blk, tkp, past_heads = 0, 0, False
    if has_past:
        tp = past["k"].shape[1]
        tkp = past["tk"]
        assert tp % tkp == 0
        n_past_blk = tp // tkp
        pb = (lambda b: b) if past["k"].shape[0] > 1 else (lambda b: 0)
        ins += [past["k"], past["v"]]
        past_heads = past["k"].ndim == 4
        if past_heads:
            specs += [pl.BlockSpec((None, tp, n_heads, dk), lambda b, h, i: (pb(b), 0, 0, 0)),
                      pl.BlockSpec((None, tp, n_heads, dv), lambda b, h, i: (pb(b), 0, 0, 0))]
        else:
            specs += [pl.BlockSpec((None, tp, dk), lambda b, h, i: (pb(b), 0, h)),
                      pl.BlockSpec((None, tp, dv), lambda b, h, i: (pb(b), 0, h))]
        if has_rope:
            ins.append(past["r"])
            specs.append(pl.BlockSpec((None, tp, LANES), lambda b, h, i: (pb(b), 0, 0)))
        if has_bias:
            ins.append(past["fk"])
            specs.append(pl.BlockSpec((None, None, 1, tp), lambda b, h, i: (pb(b), h, 0, 0)))
    ins += [kn, vn]
    specs += [pl.BlockSpec((seq_len, dk), lambda b, h, i: (ko + b, h)),
              pl.BlockSpec((seq_len, dv), lambda b, h, i: (ko + b, h))]
    if has_rope:
        ins.append(rn)
        specs.append(pl.BlockSpec((seq_len, LANES), lambda b, h, i: (ko + b, 0)))
    if has_bias:
        ins.append(fkn)
        specs.append(pl.BlockSpec((None, None, 1, seq_len), lambda b, h, i: (h, ko + b, 0, 0)))
    kern = functools.partial(_flash_kernel, n_past_blk=n_past_blk, tkp=tkp, tq=tq, has_bias=has_bias,
                             has_rope=has_rope, mask_mode=mask_mode, has_past=has_past, past_heads=past_heads,
                             single_q=nq == 1)
    return pl.pallas_call(
        kern, name="flash", grid=(n_seq, n_heads, nq), in_specs=specs,
        out_specs=pl.BlockSpec((tq, dv), lambda b, h, i: (b * nq + i, h)),
        out_shape=jax.ShapeDtypeStruct((n_seq * seq_len, n_heads * dv), BF16),
        scratch_shapes=[pltpu.VMEM((tq, LANES), F32), pltpu.VMEM((tq, dv + LANES), F32),
                        pltpu.VMEM((tq, tq) if nq > 1 else (HG_SUB, LANES), F32),
                        pltpu.VMEM((tq, tq) if nq > 1 else (HG_SUB, LANES), F32)],
        compiler_params=_params(("parallel", "parallel", "arbitrary")))(*ins)


FLASH_UNROLL = 4


def _tri_tables(nq):
    pairs = [(qi, kj) for qi in range(nq) for kj in range(qi)] + [(qi, qi) for qi in range(nq)] + [(0, 0)]
    return (jnp.array([p[0] for p in pairs], jnp.int32), jnp.array([p[1] for p in pairs], jnp.int32))


def _flash_tri_kernel(qt_ref, kt_ref, *refs, tq, nq, has_bias, has_rope, mask_mode):
    it = iter(refs)
    q_ref = next(it)
    fq_ref = next(it) if has_bias else None
    kp_ref, vp_ref = next(it), next(it)
    rp_ref = next(it) if has_rope else None
    fkp_ref = next(it) if has_bias else None
    kn_ref, vn_ref = next(it), next(it)
    rn_ref = next(it) if has_rope else None
    fkn_ref = next(it) if has_bias else None
    o_ref = next(it)
    m_scr, acc_scr, sa_scr, sb_scr = next(it), next(it), next(it), next(it)
    fqb_scr = next(it) if has_bias else None
    dv = o_ref.shape[1]
    n_off = nq * (nq - 1) // 2
    assert n_off % FLASH_UNROLL == 0 and nq % FLASH_UNROLL == 0 and FLASH_UNROLL % 2 == 0
    tile = lambda j: pl.ds(pl.multiple_of(j * tq, tq), tq)
    ones = jnp.ones((tq, LANES), BF16)

    kp = kp_ref[...]
    if has_rope:
        kp = jnp.concatenate([kp, rp_ref[...]], axis=1)
    vp1 = jnp.concatenate([vp_ref[...], ones[:vp_ref.shape[0]]], axis=1)
    for i in range(nq):
        rs = slice(i * tq, (i + 1) * tq)
        s = _dot_nt(q_ref[rs, :], kp)
        if has_bias:
            fb = fq_ref[rs, :] * LOG2E
            fqb_scr[rs, :] = jnp.broadcast_to(fb, (tq, LANES))
            s = s + fb - fkp_ref[...] * LOG2E
        m0 = jnp.max(s, axis=1, keepdims=True)
        m_scr[i] = jnp.broadcast_to(m0, (tq, LANES))
        acc_scr[i] = _dot(jnp.exp2(s - m0).astype(BF16), vp1)

    def fill(s_ref, t):
        qs, ks = tile(qt_ref[t]), tile(kt_ref[t])
        k = kn_ref[ks, :]
        if has_rope:
            k = jnp.concatenate([k, rn_ref[ks, :]], axis=1)
        s = _dot_nt(q_ref[qs, :], k)
        if has_bias:
            s = s + jnp.tile(fqb_scr[qs, :], (1, tq // LANES)) - fkn_ref[:, ks] * LOG2E
        s_ref[...] = s

    def drain(s_ref, t, mask):
        qi = qt_ref[t]
        s = s_ref[...]
        if mask is not None:
            s = jnp.where(mask, s, NEG)
        m_prev = m_scr[qi]
        m_new = jnp.maximum(m_prev, jnp.max(s, axis=1, keepdims=True))
        p = jnp.exp2(s - jnp.tile(m_new, (1, tq // LANES)))
        v1 = jnp.concatenate([vn_ref[tile(kt_ref[t]), :], ones], axis=1)
        acc = jnp.tile(jnp.exp2(m_prev - m_new), (1, (dv + LANES) // LANES)) * acc_scr[qi] + _dot(p.astype(BF16), v1)
        return qi, m_new, acc

    def keep(s_ref, t):
        qi, m_new, acc = drain(s_ref, t, None)
        m_scr[qi] = m_new
        acc_scr[qi] = acc

    row = lax.broadcasted_iota(jnp.int32, (tq, 1), 0)
    col = lax.broadcasted_iota(jnp.int32, (1, tq), 1)
    mask = {"causal": col <= row, "chunk": (col // CHUNK) <= (row // CHUNK)}[mask_mode]

    def finish(s_ref, t):
        qi, _, acc = drain(s_ref, t, mask)
        o_ref[tile(qi), :] = (acc[:, :dv] / acc[:, dv:]).astype(o_ref.dtype)

    def pipeline(t0, n, consume):
        def body(i, carry):
            t = t0 + FLASH_UNROLL * i
            for u in range(0, FLASH_UNROLL, 2):
                fill(sb_scr, t + u + 1)
                consume(sa_scr, t + u)
                fill(sa_scr, t + u + 2)
                consume(sb_scr, t + u + 1)
            return carry
        lax.fori_loop(0, n // FLASH_UNROLL, body, 0)

    fill(sa_scr, 0)
    pipeline(0, n_off, keep)
    pipeline(n_off, nq, finish)


def _flash_tri(q, kn, vn, *, n_seq, n_heads, seq_len, tq, dq, dk, dv, mask_mode, past, fq=None, fkn=None, rn=None):
    nq = seq_len // tq
    has_bias = fq is not None
    has_rope = rn is not None
    tp = past["k"].shape[1]
    m3 = lambda f: (lambda b, h, qt, kt: f(b, h))
    ins, specs = [q], [pl.BlockSpec((seq_len, dq), m3(lambda b, h: (b, h)))]
    if has_bias:
        ins.append(fq)
        specs.append(pl.BlockSpec((None, seq_len, 1), m3(lambda b, h: (h, b, 0))))
    ins += [past["k"], past["v"]]
    specs += [pl.BlockSpec((None, tp, dk), m3(lambda b, h: (0, 0, h))),
              pl.BlockSpec((None, tp, dv), m3(lambda b, h: (0, 0, h)))]
    if has_rope:
        ins.append(past["r"])
        specs.append(pl.BlockSpec((None, tp, LANES), m3(lambda b, h: (0, 0, 0))))
    if has_bias:
        ins.append(past["fk"])
        specs.append(pl.BlockSpec((None, None, 1, tp), m3(lambda b, h: (0, h, 0, 0))))
    ins += [kn, vn]
    specs += [pl.BlockSpec((seq_len, dk), m3(lambda b, h: (b, h))),
              pl.BlockSpec((seq_len, dv), m3(lambda b, h: (b, h)))]
    if has_rope:
        ins.append(rn)
        specs.append(pl.BlockSpec((seq_len, LANES), m3(lambda b, h: (b, 0))))
    if has_bias:
        ins.append(fkn)
        specs.append(pl.BlockSpec((None, None, 1, seq_len), m3(lambda b, h: (h, b, 0, 0))))
    scratch = [pltpu.VMEM((nq, tq, LANES), F32), pltpu.VMEM((nq, tq, dv + LANES), F32),
               pltpu.VMEM((tq, tq), F32), pltpu.VMEM((tq, tq), F32)]
    if has_bias:
        scratch.append(pltpu.VMEM((seq_len, LANES), F32))
    grid_spec = pltpu.PrefetchScalarGridSpec(
        num_scalar_prefetch=2, grid=(n_seq, n_heads), in_specs=specs,
        out_specs=pl.BlockSpec((seq_len, dv), m3(lambda b, h: (b, h))), scratch_shapes=scratch)
    kern = functools.partial(_flash_tri_kernel, tq=tq, nq=nq, has_bias=has_bias, has_rope=has_rope,
                             mask_mode=mask_mode)
    return pl.pallas_call(
        kern, name="flash_tri", grid_spec=grid_spec,
        out_shape=jax.ShapeDtypeStruct((n_seq * seq_len, n_heads * dv), BF16),
        compiler_params=_params(("parallel", "arbitrary")))(*_tri_tables(nq), *ins)


def _route(sc, sb):
    def top2_sum(v):
        a, b, c, d = v
        a, b = jnp.maximum(a, b), jnp.minimum(a, b)
        c, d = jnp.maximum(c, d), jnp.minimum(c, d)
        hi, lo2 = jnp.maximum(a, c), jnp.minimum(a, c)
        return hi + jnp.maximum(lo2, jnp.maximum(b, d))

    gs = [top2_sum(sb[g * EXPERTS_PER_GROUP:(g + 1) * EXPERTS_PER_GROUP]) for g in range(N_GROUPS)]
    best_v, best_g = gs[0], jnp.zeros(gs[0].shape, jnp.int32)
    for g in range(1, N_GROUPS):
        upd = gs[g] > best_v
        best_v = jnp.where(upd, gs[g], best_v)
        best_g = jnp.where(upd, g, best_g)
    masked = [jnp.where(best_g == (e // EXPERTS_PER_GROUP), sb[e], -jnp.inf) for e in range(N_EXPERTS)]

    def argmax_first(vals, exclude=None):
        bv = jnp.full(vals[0].shape, -jnp.inf, F32)
        bi = jnp.full(vals[0].shape, -1, jnp.int32)
        for e, v in enumerate(vals):
            upd = v > bv
            if exclude is not None:
                upd = upd & (exclude != e)
            bv = jnp.where(upd, v, bv)
            bi = jnp.where(upd, e, bi)
        return bi

    i1 = argmax_first(masked)
    i2 = argmax_first(masked, exclude=i1)
    w1 = sum(jnp.where(i1 == e, sc[e], 0.0) for e in range(N_EXPERTS))
    w2 = sum(jnp.where(i2 == e, sc[e], 0.0) for e in range(N_EXPERTS))
    tot = w1 + w2
    w1, w2 = w1 / tot, w2 / tot
    comb = [jnp.where(i1 == e, w1, 0.0) + jnp.where(i2 == e, w2, 0.0) for e in range(N_EXPERTS)]
    return comb + [i1.astype(F32), i2.astype(F32), w1, w2]


def _mix_kernel(*refs, n_act):
    x_ref = refs[0]
    a_refs = refs[1:1 + n_act]
    w_ref, g_ref, b_ref, rw2_ref, rb_ref, x1_ref, x1p_ref, comb_ref, ct_scr = refs[1 + n_act:]
    half = D_MODEL // 2
    ys = []
    for n0 in (0, half):
        y = None
        k0 = 0
        for a_ref in a_refs:
            kw = a_ref.shape[1]
            part = _dot(a_ref[...], w_ref[k0:k0 + kw, n0:n0 + half])
            y = part if y is None else y + part
            k0 += kw
        ys.append(y)
    x1 = _layer_norm(ALPHA * x_ref[...] + jnp.concatenate(ys, axis=1), g_ref[...], b_ref[...])
    x1_ref[...] = x1

    x1p_ref[...] = _pack_pair(x1[:, :half], x1[:, half:])
    x_hi = x1.astype(BF16)
    x_lo = (x1 - x_hi.astype(F32)).astype(BF16)
    l2 = _dot(x_hi, rw2_ref[...])
    logits = l2[:, :LANES] + l2[:, LANES:] + _dot(x_lo, rw2_ref[:, :LANES])
    scores_t = _sigmoid(logits).T
    sc = [scores_t[e:e + 1, :] for e in range(N_EXPERTS)]
    sb = [sc[e] + rb_ref[e:e + 1, :] for e in range(N_EXPERTS)]
    route_rows = _route(sc, sb)
    ct_scr[...] = jnp.zeros(ct_scr.shape, F32)
    for r, val in enumerate(route_rows):
        ct_scr[r:r + 1, :] = val
    comb_ref[...] = ct_scr[...].T


def _mix(x, acts, w_out, ln_g, ln_b, rw2, rb, tm):
    rows = x.shape[0]
    return _row_call(functools.partial(_mix_kernel, n_act=len(acts)), "mix", rows, tm,
                     [x] + list(acts), [w_out, ln_g, ln_b, rw2, rb],
                     [(D_MODEL, F32), (D_MODEL // 2, jnp.uint32), (LANES, F32)],
                     scratch=[pltpu.VMEM((LANES, tm), F32)])


def _moe_kernel(x_ref, comb_ref, wg_ref, wu_ref, wd_ref, g_ref, b_ref, o_ref, xb_scr, acc_scr):
    e = pl.program_id(1)

    @pl.when(e == 0)
    def _():
        xb_scr[...] = x_ref[...].astype(BF16)
        acc_scr[...] = jnp.zeros(acc_scr.shape, F32)

    xb = xb_scr[...]
    lane = lax.broadcasted_iota(jnp.int32, (1, LANES), 1)
    c_e = jnp.sum(jnp.where(lane == e, comb_ref[...], 0.0), axis=1, keepdims=True)
    gate = _dot(xb, wg_ref[...])
    h = gate * _sigmoid(gate) * _dot(xb, wu_ref[...])
    acc_scr[...] += _dot((h * c_e).astype(BF16), wd_ref[...])

    @pl.when(e == N_EXPERTS - 1)
    def _():
        o_ref[...] = _layer_norm(ALPHA * x_ref[...] + acc_scr[...], g_ref[...], b_ref[...])


def _moe(x, comb, wg, wu, wd, ln_g, ln_b, tm):
    rows = x.shape[0]
    assert rows % tm == 0
    return pl.pallas_call(
        _moe_kernel, name="moe", grid=(rows // tm, N_EXPERTS),
        in_specs=[pl.BlockSpec((tm, D_MODEL), lambda i, e: (i, 0)),
                  pl.BlockSpec((tm, LANES), lambda i, e: (i, 0)),
                  pl.BlockSpec((None, D_MODEL, D_EXPERT), lambda i, e: (e, 0, 0)),
                  pl.BlockSpec((None, D_MODEL, D_EXPERT), lambda i, e: (e, 0, 0)),
                  pl.BlockSpec((None, D_EXPERT, D_MODEL), lambda i, e: (e, 0, 0)),
                  _full_spec(ln_g), _full_spec(ln_b)],
        out_specs=pl.BlockSpec((tm, D_MODEL), lambda i, e: (i, 0)),
        out_shape=jax.ShapeDtypeStruct((rows, D_MODEL), F32),
        scratch_shapes=[pltpu.VMEM((tm, D_MODEL), BF16), pltpu.VMEM((tm, D_MODEL), F32)],
        compiler_params=_params(("parallel", "arbitrary")))(x, comb, wg, wu, wd, ln_g, ln_b)


ROUTE_E1, ROUTE_E2, ROUTE_W1, ROUTE_W2 = N_EXPERTS, N_EXPERTS + 1, N_EXPERTS + 2, N_EXPERTS + 3
TE = 512
SC_WINDOW = 128


def _pack_pair(a, b):
    au = lax.bitcast_convert_type(a.astype(BF16).astype(F32), jnp.uint32)
    bu = lax.bitcast_convert_type(b.astype(BF16).astype(F32), jnp.uint32)
    return (au >> 16) | (bu & jnp.uint32(0xFFFF0000))


def _unpack_pair(w):
    a = lax.bitcast_convert_type(w << 16, F32)
    b = lax.bitcast_convert_type(w & jnp.uint32(0xFFFF0000), F32)
    return a, b


def _rank_kernel(route_ref, pos_ref, texp_ref, nused_ref, cnt_scr, carry_scr, seg_scr, before_scr):
    ph, i = pl.program_id(0), pl.program_id(1)
    T = route_ref.shape[0]
    lane = lax.broadcasted_iota(jnp.int32, (1, LANES), 1)
    lane_f = lane.astype(F32)
    r = route_ref[...]
    e1, e2 = r[:, ROUTE_E1:ROUTE_E1 + 1], r[:, ROUTE_E2:ROUTE_E2 + 1]
    m1, m2 = lane_f == e1, lane_f == e2
    m = jnp.where(m1 | m2, 1.0, 0.0)
    colsum = jnp.sum(m, axis=0, keepdims=True)

    @pl.when((ph == 0) & (i == 0))
    def _():
        cnt_scr[...] = jnp.zeros(cnt_scr.shape, F32)

    @pl.when(ph == 0)
    def _():
        cnt_scr[...] += colsum

    @pl.when((ph == 1) & (i == 0))
    def _():
        cnt = cnt_scr[...].astype(jnp.int32)
        padded = (((cnt + (TE - 1)) // TE) * TE).astype(F32)
        rr = lax.broadcasted_iota(jnp.int32, (LANES, 1), 0)
        upper = (rr < lane).astype(BF16)
        hi, mid, lo = _split3(jnp.broadcast_to(padded, (HG_SUB, LANES)))
        seg = (_dot(hi, upper) + _dot(mid, upper) + _dot(lo, upper))[:1, :]
        seg_scr[...] = seg
        carry_scr[...] = jnp.zeros(carry_scr.shape, F32)
        seg_end = seg + padded
        tile_row = lax.broadcasted_iota(jnp.int32, texp_ref.shape, 1).astype(F32) * float(TE)
        te_acc = jnp.zeros(texp_ref.shape, jnp.int32)
        for e in range(N_EXPERTS):
            te_acc = te_acc + jnp.where(seg_end[:, e:e + 1] <= tile_row, 1, 0)
        texp_ref[...] = jnp.minimum(te_acc, N_EXPERTS - 1)
        nused_ref[...] = jnp.broadcast_to(seg_end[:, N_EXPERTS - 1:N_EXPERTS] / float(TE), nused_ref.shape).astype(jnp.int32)

    @pl.when((ph == 1) & (i == 0))
    def _():
        row = lax.broadcasted_iota(jnp.int32, (T, 1), 0)
        col = lax.broadcasted_iota(jnp.int32, (1, T), 1)
        before_scr[...] = (col < row).astype(BF16)

    @pl.when(ph == 1)
    def _():
        cum = _dot(before_scr[...], m.astype(BF16)) + carry_scr[...] + seg_scr[...]
        p1 = jnp.sum(jnp.where(m1, cum, 0.0), axis=1, keepdims=True)
        p2 = jnp.sum(jnp.where(m2, cum, 0.0), axis=1, keepdims=True)
        pos_ref[...] = jnp.where(lane == 0, p1, jnp.where(lane == 1, p2, 0.0)).astype(jnp.int32)
        carry_scr[...] += colsum


def _rank(route, n_tiles, tm):
    rows = route.shape[0]
    nb = rows // tm
    nt_pad = -(-n_tiles // LANES) * LANES
    return pl.pallas_call(
        _rank_kernel, name="rank", grid=(2, nb),
        in_specs=[pl.BlockSpec((tm, LANES), lambda ph, i: (i, 0))],
        out_specs=[pl.BlockSpec((tm, LANES), lambda ph, i: (i * ph, 0)),
                   pl.BlockSpec((1, nt_pad), lambda ph, i: (0, 0)),
                   pl.BlockSpec((1, LANES), lambda ph, i: (0, 0))],
        out_shape=[jax.ShapeDtypeStruct((rows, LANES), jnp.int32),
                   jax.ShapeDtypeStruct((1, nt_pad), jnp.int32),
                   jax.ShapeDtypeStruct((1, LANES), jnp.int32)],
        scratch_shapes=[pltpu.VMEM((1, LANES), F32), pltpu.VMEM((1, LANES), F32), pltpu.VMEM((1, LANES), F32),
                        pltpu.VMEM((tm, tm), BF16)],
        compiler_params=_params(("arbitrary", "arbitrary")))(route)


def _sc_mesh():
    return plsc.VectorSubcoreMesh(core_axis_name="c", subcore_axis_name="s")


def _sc_scatter_rows(x, idx, n_out):
    rows, d = x.shape
    mesh = _sc_mesh()
    n_workers = mesh.num_cores * mesh.num_subcores
    steps = idx.shape[1] // SC_WINDOW // n_workers
    assert steps * SC_WINDOW * n_workers == idx.shape[1] and rows % SC_WINDOW == 0

    @functools.partial(pl.kernel, out_type=jax.ShapeDtypeStruct((n_out, d), x.dtype), mesh=mesh,
                       scratch_types=[pltpu.VMEM((1, SC_WINDOW), jnp.int32), pltpu.VMEM((SC_WINDOW, d), x.dtype)])
    def scatter(x_hbm, i_hbm, o_hbm, i_vmem, buf):
        first = (lax.axis_index("c") * mesh.num_subcores + lax.axis_index("s")) * steps

        @pl.loop(0, steps)
        def _(t):
            off = (first + t) * SC_WINDOW
            pltpu.sync_copy(i_hbm.at[:, pl.ds(off, SC_WINDOW)], i_vmem)
            pltpu.sync_copy(x_hbm.at[pl.ds(off % rows, SC_WINDOW)], buf)
            pltpu.sync_copy(buf, o_hbm.at[i_vmem.at[0]])

    return scatter(x, idx)


def _sc_gather_rows(x, idx):
    d = x.shape[1]
    n = idx.shape[1]
    mesh = _sc_mesh()
    n_workers = mesh.num_cores * mesh.num_subcores
    steps = n // SC_WINDOW // n_workers
    assert steps * SC_WINDOW * n_workers == n

    @functools.partial(pl.kernel, out_type=jax.ShapeDtypeStruct((n, d), x.dtype), mesh=mesh,
                       scratch_types=[pltpu.VMEM((1, SC_WINDOW), jnp.int32), pltpu.VMEM((SC_WINDOW, d), x.dtype)])
    def gather(x_hbm, i_hbm, o_hbm, i_vmem, buf):
        first = (lax.axis_index("c") * mesh.num_subcores + lax.axis_index("s")) * steps

        @pl.loop(0, steps)
        def _(t):
            off = (first + t) * SC_WINDOW
            pltpu.sync_copy(i_hbm.at[:, pl.ds(off, SC_WINDOW)], i_vmem)
            pltpu.sync_copy(x_hbm.at[i_vmem.at[0]], buf)
            pltpu.sync_copy(buf, o_hbm.at[pl.ds(off, SC_WINDOW)])

    return gather(x, idx)


def _gmm_kernel(texp_ref, nused_ref, x_ref, wg_ref, wu_ref, wd_ref, o_ref):
    @pl.when(pl.program_id(0) < nused_ref[0])
    def _():
        a, b = _unpack_pair(x_ref[...])
        xb = jnp.concatenate([a.astype(BF16), b.astype(BF16)], axis=1)
        gate = _dot(xb, wg_ref[...])
        h = gate * _sigmoid(gate) * _dot(xb, wu_ref[...])
        y = _dot(h.astype(BF16), wd_ref[...])
        o_ref[...] = _pack_pair(y[:, :D_MODEL // 2], y[:, D_MODEL // 2:])


def _gmm(xs, texp, nused, wg, wu, wd):
    rows = xs.shape[0]
    wmap = lambda d, te, nu: (te[d], 0, 0)
    grid_spec = pltpu.PrefetchScalarGridSpec(
        num_scalar_prefetch=2, grid=(rows // TE,),
        in_specs=[pl.BlockSpec((TE, D_MODEL // 2), lambda d, te, nu: (d, 0)),
                  pl.BlockSpec((None, D_MODEL, D_EXPERT), wmap),
                  pl.BlockSpec((None, D_MODEL, D_EXPERT), wmap),
                  pl.BlockSpec((None, D_EXPERT, D_MODEL), wmap)],
        out_specs=pl.BlockSpec((TE, D_MODEL // 2), lambda d, te, nu: (d, 0)))
    return pl.pallas_call(
        _gmm_kernel, name="gmm", grid_spec=grid_spec,
        out_shape=jax.ShapeDtypeStruct((rows, D_MODEL // 2), jnp.uint32),
        compiler_params=_params(("arbitrary",)))(texp, nused, xs, wg, wu, wd)


def _combine_kernel(x_ref, g0_ref, g1_ref, route_ref, g_ref, b_ref, o_ref):
    r = route_ref[...]
    y0 = jnp.concatenate(_unpack_pair(g0_ref[...]), axis=1)
    y1 = jnp.concatenate(_unpack_pair(g1_ref[...]), axis=1)
    f = y0 * r[:, ROUTE_W1:ROUTE_W1 + 1] + y1 * r[:, ROUTE_W2:ROUTE_W2 + 1]
    o_ref[...] = _layer_norm(ALPHA * x_ref[...] + f, g_ref[...], b_ref[...])


def _combine(x, g, route, ln_g, ln_b, tm):
    rows = x.shape[0]
    nb = rows // tm
    return pl.pallas_call(
        _combine_kernel, name="combine", grid=(nb,),
        in_specs=[pl.BlockSpec((tm, D_MODEL), lambda i: (i, 0)),
                  pl.BlockSpec((tm, D_MODEL // 2), lambda i: (i, 0)),
                  pl.BlockSpec((tm, D_MODEL // 2), lambda i: (nb + i, 0)),
                  pl.BlockSpec((tm, LANES), lambda i: (i, 0)), _full_spec(ln_g), _full_spec(ln_b)],
        out_specs=pl.BlockSpec((tm, D_MODEL), lambda i: (i, 0)),
        out_shape=jax.ShapeDtypeStruct((rows, D_MODEL), F32),
        compiler_params=_params(("parallel",)))(x, g, g, route, ln_g, ln_b)


def _moe_routed(x1, x1b, route, wg, wu, wd, ln_g, ln_b, tm):
    rows = x1.shape[0]
    n_rows = 2 * rows + N_EXPERTS * TE
    pos, texp, nused = _rank(route, n_rows // TE, tm)
    idx = jnp.concatenate([pos[:, 0], pos[:, 1]])[None, :]
    xs = _sc_scatter_rows(x1b, idx, n_rows)
    ys = _gmm(xs, texp[0, :n_rows // TE], nused[0, :1], wg, wu, wd)
    g = _sc_gather_rows(ys, idx)
    return _combine(x1, g, route, ln_g, ln_b, tm)


def _rope128(x, cos_t, sin_t):
    lane = lax.broadcasted_iota(jnp.int32, (1, LANES), 1)
    half = MLA_ROPE // 2
    swapped = jnp.where(lane < half, pltpu.roll(x, LANES - half, axis=1), pltpu.roll(x, half, axis=1))
    return x * cos_t + swapped * sin_t


def _odd_proj_kernel(x_ref, cos_ref, sin_ref, w_ref, gq_ref, gkv_ref, wuq_ref,
                     q_ref, ckv_ref, kpe_ref, kpe16_ref):
    z = _dot(x_ref[...].astype(BF16), w_ref[...])
    cq = _rms_norm(z[:, :MLA_Q_LORA], gq_ref[...])
    ckv_ref[...] = _rms_norm(z[:, MLA_Q_LORA:MLA_Q_LORA + MLA_KV_LORA], gkv_ref[...])
    cos_t, sin_t = cos_ref[...], sin_ref[...]
    kpe = _rope128(z[:, MLA_Q_LORA + MLA_KV_LORA:], cos_t, sin_t)
    kpe_ref[...] = kpe[:, :MLA_ROPE]
    kpe16_ref[...] = kpe.astype(BF16)
    qf = _dot(cq.astype(BF16), wuq_ref[...])
    scale = (MLA_NOPE + MLA_ROPE) ** -0.5 * LOG2E
    for h in range(MLA_HEADS):
        c0 = h * MLA_QPAD
        q_ref[:, c0:c0 + MLA_NOPE] = (qf[:, c0:c0 + MLA_NOPE] * scale).astype(BF16)
        qr = _rope128(qf[:, c0 + MLA_NOPE:c0 + MLA_QPAD], cos_t, sin_t)
        q_ref[:, c0 + MLA_NOPE:c0 + MLA_QPAD] = (qr * scale).astype(BF16)


def _odd_proj(x, cos_t, sin_t, w_in, gq, gkv, wuq, tm):
    rows = x.shape[0]
    outs = [(MLA_HEADS * MLA_QPAD, BF16), (MLA_KV_LORA, F32), (MLA_ROPE, F32), (LANES, BF16)]
    return _row_call(_odd_proj_kernel, "odd_proj", rows, tm, [x, cos_t, sin_t], [w_in, gq, gkv, wuq], outs)


def _kv_expand_kernel(c_ref, w_ref, k_ref, v_ref):
    kv = _dot(c_ref[...].astype(BF16), w_ref[...])
    n = MLA_HEADS * MLA_NOPE
    k_ref[...] = kv[:, :n].astype(BF16)
    v_ref[...] = kv[:, n:].astype(BF16)


def _kv_expand(ckv, w_ukv, tm):
    rows = ckv.shape[0]
    return _row_call(_kv_expand_kernel, "kv_expand", rows, tm, [ckv], [w_ukv],
                     [(MLA_HEADS * MLA_NOPE, BF16), (MLA_HEADS * MLA_V, BF16)])


def _rope_tables(pos):
    half = MLA_ROPE // 2
    inv = ROPE_BASE ** (-jnp.arange(half, dtype=F32) / half)
    ang = pos.astype(F32)[:, None] * inv[None, :]
    cos, sin = jnp.cos(ang), jnp.sin(ang)
    z = jnp.zeros((pos.shape[0], LANES - MLA_ROPE), F32)
    return jnp.concatenate([cos, cos, z], axis=1), jnp.concatenate([-sin, sin, z], axis=1)


def _pad_rows(a, n):
    return jnp.pad(a, ((0, n - a.shape[0]),) + ((0, 0),) * (a.ndim - 1))


def kernel(x_prompt, x_sample, state_hgrn2, cache_fox_k, cache_fox_v, cache_fox_logf, cache_mla_ckv, cache_mla_kpe, meta_tokens, even_w_in, hg_lb_logits, hg_norm_g, fox_forget_bias, even_w_out, mla_w_in, mla_q_norm_g, mla_kv_norm_g, mla_w_uq, mla_w_uk, mla_w_uv, mla_w_out, ln_mix_g, ln_mix_b, ln_ffn_g, ln_ffn_b, router_w, router_bias, moe_w_gate, moe_w_up, moe_w_down):
    B, T, _ = x_prompt.shape
    Bs, Ts, _ = x_sample.shape
    P = cache_fox_k.shape[2]
    RM = B * T
    RS = Bs * Ts
    RSM = -(-(RS + N_META) // LANES) * LANES
    ME = slice(RS, RS + N_META)
    TM_MAIN, TM_MOE, TQ = 512, 1024, 512

    xm = x_prompt.reshape(RM, D_MODEL)
    xs = _pad_rows(jnp.concatenate([x_sample.reshape(RS, D_MODEL), meta_tokens.astype(F32)], axis=0), RSM)

    w_in0 = even_w_in[0]
    n_main = 7 * HG_W
    w_even = w_in0[:, :n_main].astype(BF16)
    w_even_f = jnp.pad(w_in0[:, n_main:], ((0, 0), (0, LANES - FOX_HEADS))).astype(BF16)
    fb_pad = jnp.pad(fox_forget_bias[0][None, :], ((0, 0), (0, LANES - FOX_HEADS)))
    g_hg = hg_norm_g[0].reshape(1, HG_W)
    w_out0 = even_w_out[0].astype(BF16)
    e_mat = ((jnp.arange(HG_SUB * HG_DK)[:, None] // HG_DK) == (jnp.arange(CHUNK)[None, :] % HG_SUB)).astype(BF16)

    w_odd = jnp.pad(mla_w_in[0], ((0, 0), (0, LANES - MLA_ROPE))).astype(BF16)
    gq = mla_q_norm_g[0][None, :]
    gkv = mla_kv_norm_g[0][None, :]
    wuq = mla_w_uq[0].reshape(MLA_Q_LORA, MLA_HEADS, MLA_NOPE + MLA_ROPE)
    wuq = jnp.pad(wuq, ((0, 0), (0, 0), (0, MLA_QPAD - MLA_NOPE - MLA_ROPE)))
    wuq = wuq.reshape(MLA_Q_LORA, MLA_HEADS * MLA_QPAD).astype(BF16)
    w_ukv = jnp.concatenate([mla_w_uk[0].reshape(MLA_KV_LORA, -1), mla_w_uv[0].reshape(MLA_KV_LORA, -1)],
                            axis=1).astype(BF16)
    w_out1 = mla_w_out[0].astype(BF16)

    rw = jnp.pad(router_w, ((0, 0), (0, LANES - N_EXPERTS)))
    rw_hi = rw.astype(BF16)
    rw2 = jnp.concatenate([rw_hi, (rw - rw_hi.astype(F32)).astype(BF16)], axis=1)
    rb = jnp.pad(router_bias.astype(F32)[:, None], ((0, LANES - N_EXPERTS), (0, 0)))
    wg = moe_w_gate.astype(BF16)
    wu = moe_w_up.astype(BF16)
    wd = moe_w_down.astype(BF16)
    row2 = lambda a: a[None, :]

    def ffn(x, acts, w_out, l, tm_mix, tm_moe, routed):
        x1, x1b, route = _mix(x, acts, w_out, row2(ln_mix_g[l]), row2(ln_mix_b[l]), rw2, rb, tm_mix)
        ln = (row2(ln_ffn_g[l]), row2(ln_ffn_b[l]))
        if routed:
            return _moe_routed(x1, x1b, route, wg[l], wu[l], wd[l], *ln, tm_mix)
        return _moe(x1, route, wg[l], wu[l], wd[l], *ln, tm_moe)

    pm = _even_proj(xm, w_even, w_even_f, hg_lb_logits, fb_pad, TM_MAIN, 0)
    ps = _even_proj(xs, w_even, w_even_f, hg_lb_logits, fb_pad, RSM, 0)
    names = ("hq", "lf", "hk", "hv", "hgate", "fq", "fk", "fv", "fk16", "fv16", "flf")
    pm = dict(zip(names, pm))
    ps = dict(zip(names, ps))

    hg_keys = ("hq", "lf", "hk", "hv", "hgate", "flf")
    meta_in = [_pad_rows(ps[n][ME], CHUNK) for n in hg_keys]
    zero_s = jnp.zeros((1, HG_HEADS, HG_DK, HG_DV), F32)
    zero_f = jnp.zeros((1, 1, LANES), F32)
    o_hg_meta, fc_meta, s_meta = _hgrn2(*meta_in, g_hg, e_mat, zero_s, zero_f, 1, CHUNK, 0, CHUNK)
    o_hg_meta, fc_meta = o_hg_meta[:N_META], fc_meta[:N_META]
    f_meta_end = fc_meta[N_META - 1:N_META][None]

    o_hg_m, fc_m, s_main = _hgrn2(*[pm[n] for n in hg_keys], g_hg, e_mat, s_meta, f_meta_end, B, T, 0, 256)

    logf_c = jnp.pad(jnp.transpose(cache_fox_logf[0], (0, 2, 1)), ((0, 0), (0, HG_SUB - FOX_HEADS), (0, 0)))
    fpast = _cumsum_lanes(logf_c, 512)[:, :FOX_HEADS, :]
    f0_s = jnp.pad(fpast[:, :, P - 1][:, None, :], ((0, 0), (0, 0), (0, LANES - FOX_HEADS)))
    o_hg_s, fc_s, s_samp = _hgrn2(*[ps[n] for n in hg_keys], g_hg, e_mat, state_hgrn2[0], f0_s, Bs, Ts, 0, CHUNK)

    def bias_layouts(fc, n_seq, seq_len):
        f4 = fc[:, :FOX_HEADS].T
        return f4[:, :, None], f4.reshape(FOX_HEADS, n_seq, 1, seq_len)

    fq_m, fk_m = bias_layouts(fc_m, B, T)
    fq_s, fk_s = bias_layouts(fc_s, Bs, Ts)
    fq_t, fk_t = bias_layouts(fc_meta, 1, N_META)

    fox_kw = dict(n_heads=FOX_HEADS, dq=FOX_DH, dk=FOX_DH, dv=FOX_DH, mask_mode="causal")
    meta_past = dict(k=ps["fk16"][ME][None], v=ps["fv16"][ME][None],
                     fk=jnp.transpose(fk_t, (1, 0, 2, 3)), tk=N_META)
    o_fox_m = _flash_tri(pm["fq"], pm["fk16"], pm["fv16"], n_seq=B, seq_len=T, tq=TQ,
                         fq=fq_m, fkn=fk_m, past=meta_past, **fox_kw)
    samp_past = dict(k=cache_fox_k[0], v=cache_fox_v[0],
                     fk=fpast[:, :, None, :], tk=1024)
    o_fox_s = _flash(ps["fq"], ps["fk16"], ps["fv16"], n_seq=Bs, seq_len=Ts, tq=Ts, q_off=0, k_off=0,
                     fq=fq_s, fkn=fk_s, past=samp_past, **fox_kw)
    o_fox_t = _flash(ps["fq"][ME], ps["fk16"][ME], ps["fv16"][ME], n_seq=1, seq_len=N_META, tq=N_META,
                     q_off=0, k_off=0, fq=fq_t, fkn=fk_t, **fox_kw)

    o_hg_small = _pad_rows(jnp.concatenate([o_hg_s, o_hg_meta], axis=0), RSM)
    o_fox_small = _pad_rows(jnp.concatenate([o_fox_s, o_fox_t], axis=0), RSM)
    xm = ffn(xm, [o_hg_m, o_fox_m], w_out0, 0, TM_MAIN, TM_MOE, True)
    xs = ffn(xs, [o_hg_small, o_fox_small], w_out0, 0, RSM, RSM, False)

    cos_m, sin_m = _rope_tables(N_META + jnp.arange(T, dtype=jnp.int32))
    pos_small = _pad_rows(jnp.concatenate([jnp.tile(P + jnp.arange(Ts, dtype=jnp.int32), Bs),
                                           jnp.arange(N_META, dtype=jnp.int32)]), RSM)
    cos_s, sin_s = _rope_tables(pos_small)
    qm, ckv_m, kpe_m, kpe16_m = _odd_proj(xm, jnp.tile(cos_m, (B, 1)), jnp.tile(sin_m, (B, 1)),
                                          w_odd, gq, gkv, wuq, TM_MAIN)
    qs, ckv_s, kpe_s, kpe16_s = _odd_proj(xs, cos_s, sin_s, w_odd, gq, gkv, wuq, RSM)
    kn_m, vn_m = _kv_expand(ckv_m, w_ukv, 1024)
    kn_s, vn_s = _kv_expand(ckv_s, w_ukv, RSM)
    kp_c, vp_c = _kv_expand(cache_mla_ckv[0].reshape(Bs * P, MLA_KV_LORA), w_ukv, 1024)
    rp_c = jnp.pad(cache_mla_kpe[0], ((0, 0), (0, 0), (0, LANES - MLA_ROPE))).astype(BF16)

    mla_kw = dict(n_heads=MLA_HEADS, dq=MLA_QPAD, dk=MLA_NOPE, dv=MLA_V)
    meta_past = dict(k=kn_s[ME][None], v=vn_s[ME][None], r=kpe16_s[ME][None], tk=N_META)
    o_m = _flash_tri(qm, kn_m, vn_m, n_seq=B, seq_len=T, tq=TQ, rn=kpe16_m,
                     past=meta_past, mask_mode="chunk", **mla_kw)
    samp_past = dict(k=kp_c.reshape(Bs, P, -1), v=vp_c.reshape(Bs, P, -1), r=rp_c, tk=1024)
    o_s = _flash(qs, kn_s, vn_s, n_seq=Bs, seq_len=Ts, tq=Ts, q_off=0, k_off=0, rn=kpe16_s,
                 past=samp_past, mask_mode="full", **mla_kw)
    o_t = _flash(qs[ME], kn_s[ME], vn_s[ME], n_seq=1, seq_len=N_META, tq=N_META, q_off=0, k_off=0,
                 rn=kpe16_s[ME], mask_mode="full", **mla_kw)
    xm = ffn(xm, [o_m], w_out1, 1, TM_MAIN, TM_MOE, True)
    xs = ffn(xs, [_pad_rows(jnp.concatenate([o_s, o_t], axis=0), RSM)], w_out1, 1, RSM, RSM, False)

    def with_meta(main, small, *width):
        meta = jnp.broadcast_to(small[ME][None], (B, N_META) + width)
        return jnp.concatenate([meta, main.reshape((B, T) + width)], axis=1)

    y_prompt = xm.reshape(B, T, D_MODEL)
    y_sample = xs[:RS].reshape(Bs, Ts, D_MODEL)
    hg_p = s_main[None]
    fk_p = with_meta(pm["fk"], ps["fk"], FOX_HEADS, FOX_DH)[None]
    fv_p = with_meta(pm["fv"], ps["fv"], FOX_HEADS, FOX_DH)[None]
    flf_p = with_meta(pm["flf"][:, :FOX_HEADS], ps["flf"][:, :FOX_HEADS], FOX_HEADS)[None]
    ckv_p = with_meta(ckv_m, ckv_s, MLA_KV_LORA)[None]
    kpe_p = with_meta(kpe_m, kpe_s, MLA_ROPE)[None]
    hg_s = s_samp[None]
    fk_s_out = ps["fk"][:RS].reshape(1, Bs, Ts, FOX_HEADS, FOX_DH)
    fv_s_out = ps["fv"][:RS].reshape(1, Bs, Ts, FOX_HEADS, FOX_DH)
    flf_s = ps["flf"][:RS, :FOX_HEADS].reshape(1, Bs, Ts, FOX_HEADS)
    ckv_so = ckv_s[:RS].reshape(1, Bs, Ts, MLA_KV_LORA)
    kpe_so = kpe_s[:RS].reshape(1, Bs, Ts, MLA_ROPE)
    return (y_prompt, y_sample, hg_p, fk_p, fv_p, flf_p, ckv_p, kpe_p,
            hg_s, fk_s_out, fv_s_out, flf_s, ckv_so, kpe_so)
```

```python
import functools

import jax
import jax.numpy as jnp
from jax import lax
from jax.experimental import pallas as pl
from jax.experimental.pallas import tpu as pltpu
from jax.experimental.pallas import tpu_sc as plsc

D_MODEL = 1024
CHUNK = 64
N_META = 16
HG_HEADS = 4
HG_DK = 128
HG_DV = 128
HG_W = HG_HEADS * HG_DK
FOX_HEADS = 4
FOX_DH = 128
FOX_W = FOX_HEADS * FOX_DH
MLA_HEADS = 8
MLA_Q_LORA = 512
MLA_KV_LORA = 256
MLA_NOPE = 128
MLA_ROPE = 64
MLA_V = 128
MLA_QPAD = 256
ROPE_BASE = 10000.0
N_EXPERTS = 16
N_GROUPS = 4
EXPERTS_PER_GROUP = 4
D_EXPERT = 256
DEPTH = 2
ALPHA = (2 * DEPTH) ** 0.25
LN_EPS = 1e-5
RMS_EPS = 1e-6

LANES = 128
HG_SUB = 8
HG_GROUP = 4
NEG = -1e30
LOG2E = 1.4426950408889634
F32 = jnp.float32
BF16 = jnp.bfloat16
VMEM_LIMIT = 56 * 1024 * 1024


def _dot(a, b):
    return jnp.dot(a, b, preferred_element_type=F32)


def _dot_nt(a, b):
    return lax.dot_general(a, b, (((1,), (1,)), ((), ())), preferred_element_type=F32)


def _dot_tn(a, b):
    return lax.dot_general(a, b, (((0,), (0,)), ((), ())), preferred_element_type=F32)


def _split3(x):
    hi = x.astype(BF16)
    r = x - hi.astype(F32)
    mid = r.astype(BF16)
    lo = (r - mid.astype(F32)).astype(BF16)
    return hi, mid, lo


def _cumsum_rows(tri, x):
    hi, mid, lo = _split3(x)
    return _dot(tri, hi) + _dot(tri, mid) + _dot(tri, lo)


def _sigmoid(x):
    return 1.0 / (1.0 + jnp.exp(-x))


def _log_sigmoid(x):
    return jnp.minimum(x, 0.0) - jnp.log(1.0 + jnp.exp(-jnp.abs(x)))


def _layer_norm(x, g, b):
    mu = jnp.mean(x, axis=-1, keepdims=True)
    xc = x - mu
    var = jnp.mean(xc * xc, axis=-1, keepdims=True)
    return xc * lax.rsqrt(var + LN_EPS) * g + b


def _rms_norm(x, g):
    return x * lax.rsqrt(jnp.mean(x * x, axis=-1, keepdims=True) + RMS_EPS) * g


def _params(sem):
    return pltpu.CompilerParams(dimension_semantics=sem, vmem_limit_bytes=VMEM_LIMIT)


def _full_spec(a):
    nd = a.ndim
    return pl.BlockSpec(a.shape, lambda *_: (0,) * nd)


def _row_call(kernel, name, rows, tm, row_ins, full_ins, outs, scratch=()):
    assert rows % tm == 0
    in_specs = [pl.BlockSpec((tm, a.shape[1]), lambda i: (i, 0)) for a in row_ins]
    in_specs += [_full_spec(a) for a in full_ins]
    trail = [c if isinstance(c, tuple) else (c,) for c, _ in outs]
    out_specs = [pl.BlockSpec((tm,) + t, lambda i, n=len(t): (i,) + (0,) * n) for t in trail]
    out_shape = [jax.ShapeDtypeStruct((rows,) + t, dt) for t, (_, dt) in zip(trail, outs)]
    return pl.pallas_call(
        kernel, name=name, grid=(rows // tm,), in_specs=in_specs, out_specs=out_specs,
        out_shape=out_shape, scratch_shapes=list(scratch),
        compiler_params=_params(("parallel",)))(*row_ins, *full_ins)


def _even_proj_kernel(x_ref, w_ref, wf_ref, lbl_ref, fb_ref,
                      hq_ref, lf_ref, hk_ref, hv_ref, hgate_ref,
                      fq_ref, fk_ref, fv_ref, fk16_ref, fv16_ref, flf_ref, *, layer):
    xb = x_ref[...].astype(BF16)

    def blk(j):
        return _dot(xb, w_ref[:, j * HG_W:(j + 1) * HG_W])

    logits = lbl_ref[...]
    e = jnp.exp(logits - jnp.max(logits, axis=0, keepdims=True))
    lb = jnp.sum(e[:layer + 1], axis=0, keepdims=True) / jnp.sum(e, axis=0, keepdims=True)

    hq_ref[...] = blk(0).astype(BF16)
    zf = blk(1)
    lf_ref[...] = jnp.log(lb + (1.0 - lb) * _sigmoid(zf))
    hk_ref[...] = ((1.0 - lb) * _sigmoid(-zf)).astype(BF16)
    hv_ref[...] = blk(2).astype(BF16)
    hgate_ref[...] = _sigmoid(blk(3)).astype(BF16)
    fq_ref[...] = (blk(4) * (FOX_DH ** -0.5 * LOG2E)).astype(BF16)
    fk = blk(5)
    fk16_ref[...] = fk.astype(BF16)
    fv = blk(6)
    fv16_ref[...] = fv.astype(BF16)
    for h in range(FOX_HEADS):
        fk_ref[:, h, :] = fk[:, h * FOX_DH:(h + 1) * FOX_DH]
        fv_ref[:, h, :] = fv[:, h * FOX_DH:(h + 1) * FOX_DH]
    flf_ref[...] = _log_sigmoid(_dot(xb, wf_ref[...]) + fb_ref[...])


def _even_proj(x, w_main, w_f, lb_logits, fb_pad, tm, layer):
    rows = x.shape[0]
    outs = [(HG_W, BF16), (HG_W, F32), (HG_W, BF16), (HG_W, BF16), (HG_W, BF16),
            (FOX_W, BF16), ((FOX_HEADS, FOX_DH), F32), ((FOX_HEADS, FOX_DH), F32), (FOX_W, BF16), (FOX_W, BF16),
            (LANES, F32)]
    return _row_call(functools.partial(_even_proj_kernel, layer=layer), "even_proj", rows, tm,
                     [x], [w_main, w_f, lb_logits, fb_pad], outs)


def _bcast_sub(x, j):
    n, c = x.shape
    x3 = x.reshape(n // HG_SUB, HG_SUB, c)
    return jnp.broadcast_to(x3[:, j:j + 1, :], x3.shape).reshape(n, c)


def _level_ref(b, w):
    n, c = b.shape
    parts = [jnp.broadcast_to(b[m * 2 * w + w - 1:m * 2 * w + w, :], (2 * w, c)) for m in range(n // (2 * w))]
    return parts[0] if len(parts) == 1 else jnp.concatenate(parts, axis=0)


def _hgrn2_kernel(q_ref, lf_ref, k_ref, v_ref, gate_ref, flf_ref, g_ref, e_ref, s0_ref, f0_ref,
                  o_ref, fcum_ref, sout_ref, st_scr, fc_scr, *, n_chunks):
    i = pl.program_id(1)
    C = CHUNK

    @pl.when(i == 0)
    def _():
        for h in range(HG_HEADS):
            st_scr[h] = s0_ref[h].T
        fc_scr[...] = f0_ref[...]

    row = lax.broadcasted_iota(jnp.int32, (C, 1), 0)
    col = lax.broadcasted_iota(jnp.int32, (1, C), 1)
    tri = (col <= row).astype(BF16)
    same = lambda w: (row // w) == (col // w)
    levels = (32, 16, 8)

    for c in range(n_chunks):
        sl = slice(c * C, (c + 1) * C)
        fcum = _cumsum_rows(tri, flf_ref[sl, :]) + fc_scr[...]
        fcum_ref[sl, :] = fcum
        fc_scr[...] = fcum[C - 1:C, :]

        for h0 in range(0, HG_HEADS, HG_GROUP):
            gs = slice(h0 * HG_DK, (h0 + HG_GROUP) * HG_DK)
            b = _cumsum_rows(tri, lf_ref[sl, gs])
            q = q_ref[sl, gs].astype(F32)
            k = k_ref[sl, gs].astype(F32)
            v = v_ref[sl, gs]
            qb = (q * jnp.exp(b)).astype(BF16)
            b_last = b[C - 1:C, :]
            kd = (k * jnp.exp(b_last - b)).astype(BF16)
            e_last = jnp.exp(b_last)

            pjs = [(jnp.exp(jnp.where((row % HG_SUB) >= j, b - _bcast_sub(b, j), NEG)) * q
                    * _bcast_sub(k, j)).astype(BF16) for j in range(HG_SUB)]
            lv = []
            for w in levels:
                upper = (row % (2 * w)) >= w
                ew = jnp.exp(-jnp.abs(b - _level_ref(b, w)))
                lv.append((jnp.where(upper, q * ew, 0.0).astype(BF16), jnp.where(upper, 0.0, k * ew).astype(BF16)))

            for hh in range(HG_GROUP):
                h = h0 + hh
                hs = slice(hh * HG_DK, (hh + 1) * HG_DK)
                ho = slice(h * HG_DK, (h + 1) * HG_DK)
                a = jnp.where(same(HG_SUB), _dot(jnp.concatenate([p[:, hs] for p in pjs], axis=1), e_ref[...]), 0.0)
                for w, (qw, kw) in zip(levels, lv):
                    aw = _dot_nt(qw[:, hs], kw[:, hs])
                    a = a + (aw if 2 * w == C else jnp.where(same(2 * w), aw, 0.0))
                st = st_scr[h]
                vh = v[:, hs]
                o = _dot(a.astype(BF16), vh) + _dot_nt(qb[:, hs], st.astype(BF16))
                st_scr[h] = st * e_last[:, hs] + _dot_tn(vh, kd[:, hs])
                o = _rms_norm(o, g_ref[:, ho])
                o_ref[sl, ho] = (o * gate_ref[sl, ho].astype(F32)).astype(BF16)

    @pl.when(i == pl.num_programs(1) - 1)
    def _():
        for h in range(HG_HEADS):
            sout_ref[h] = st_scr[h].T


def _hgrn2(q, lf, k, v, gate, flf, g, e_mat, s0, f0, n_seq, seq_len, row_off, tb):
    assert seq_len % tb == 0 and tb % CHUNK == 0 and row_off % tb == 0
    nb = seq_len // tb
    off = row_off // tb
    per_seq = s0.shape[0] > 1
    rmap = lambda s, i: (off + s * nb + i, 0)
    omap = lambda s, i: (s * nb + i, 0)
    smap = (lambda s, i: (s, 0, 0, 0)) if per_seq else (lambda s, i: (0, 0, 0, 0))
    fmap = (lambda s, i: (s, 0, 0)) if per_seq else (lambda s, i: (0, 0, 0))
    in_specs = [pl.BlockSpec((tb, HG_W), rmap) for _ in range(5)]
    in_specs += [pl.BlockSpec((tb, LANES), rmap), _full_spec(g), _full_spec(e_mat),
                 pl.BlockSpec((None, HG_HEADS, HG_DK, HG_DV), smap), pl.BlockSpec((None, 1, LANES), fmap)]
    out_specs = [pl.BlockSpec((tb, HG_W), omap), pl.BlockSpec((tb, LANES), omap),
                 pl.BlockSpec((None, HG_HEADS, HG_DK, HG_DV), lambda s, i: (s, 0, 0, 0))]
    out_shape = [jax.ShapeDtypeStruct((n_seq * seq_len, HG_W), BF16),
                 jax.ShapeDtypeStruct((n_seq * seq_len, LANES), F32),
                 jax.ShapeDtypeStruct((n_seq, HG_HEADS, HG_DK, HG_DV), F32)]
    scratch = [pltpu.VMEM((HG_HEADS, HG_DV, HG_DK), F32), pltpu.VMEM((1, LANES), F32)]
    return pl.pallas_call(
        functools.partial(_hgrn2_kernel, n_chunks=tb // CHUNK), name="hgrn2",
        grid=(n_seq, nb), in_specs=in_specs, out_specs=out_specs, out_shape=out_shape,
        scratch_shapes=scratch, compiler_params=_params(("parallel", "arbitrary")))(
            q, lf, k, v, gate, flf, g, e_mat, s0, f0)


def _cumsum_kernel(x_ref, tri_ref, o_ref, carry):
    @pl.when(pl.program_id(1) == 0)
    def _():
        carry[...] = jnp.zeros_like(carry)

    hi, mid, lo = _split3(x_ref[...])
    tri = tri_ref[...]
    out = _dot(hi, tri) + _dot(mid, tri) + _dot(lo, tri) + carry[...]
    o_ref[...] = out
    carry[...] = out[:, out.shape[1] - 1:]


def _cumsum_lanes(x, tb):
    n_seq, r, seq_len = x.shape
    nb = seq_len // tb
    tri = (jnp.arange(tb)[:, None] <= jnp.arange(tb)[None, :]).astype(BF16)
    return pl.pallas_call(
        _cumsum_kernel, name="cumsum", grid=(n_seq, nb),
        in_specs=[pl.BlockSpec((None, r, tb), lambda s, i: (s, 0, i)), _full_spec(tri)],
        out_specs=pl.BlockSpec((None, r, tb), lambda s, i: (s, 0, i)),
        out_shape=jax.ShapeDtypeStruct(x.shape, F32),
        scratch_shapes=[pltpu.VMEM((r, 1), F32)],
        compiler_params=_params(("parallel", "arbitrary")))(x, tri)


def _flash_kernel(*refs, n_past_blk, tkp, tq, has_bias, has_rope, mask_mode, has_past, past_heads, single_q):
    it = iter(refs)
    q_ref = next(it)
    fq_ref = next(it) if has_bias else None
    if has_past:
        kp_ref, vp_ref = next(it), next(it)
        rp_ref = next(it) if has_rope else None
        fkp_ref = next(it) if has_bias else None
    kn_ref, vn_ref = next(it), next(it)
    rn_ref = next(it) if has_rope else None
    fkn_ref = next(it) if has_bias else None
    o_ref = next(it)
    m_scr, acc_scr, sa_scr, sb_scr = next(it), next(it), next(it), next(it)
    dv = o_ref.shape[1]

    qi = pl.program_id(2)
    q = q_ref[...]
    m_scr[...] = jnp.full(m_scr.shape, NEG, F32)
    acc_scr[...] = jnp.zeros(acc_scr.shape, F32)
    fq_b = jnp.broadcast_to(fq_ref[...] * LOG2E, (tq, LANES)) if has_bias else None

    def scores(k, r, fk):
        if has_rope:
            k = jnp.concatenate([k, r], axis=1)
        s = _dot_nt(q, k.astype(BF16))
        if has_bias:
            s = s + jnp.tile(fq_b, (1, s.shape[1] // LANES)) if s.shape[1] % LANES == 0 else s + fq_b[:, :1]
            s = s - fk * LOG2E
        return s

    def update(s, v, mask):
        if mask is not None:
            s = jnp.where(mask, s, NEG)
        m_prev = m_scr[...]
        m_new = jnp.maximum(m_prev, jnp.max(s, axis=1, keepdims=True))
        alpha = jnp.exp2(m_prev - m_new)
        if s.shape[1] % LANES == 0:
            p = jnp.exp2(s - jnp.tile(m_new, (1, s.shape[1] // LANES)))
        else:
            p = jnp.exp2(s - m_new[:, :1])
        v1 = jnp.concatenate([v.astype(BF16), jnp.ones((v.shape[0], LANES), BF16)], axis=1)
        acc_scr[...] = jnp.tile(alpha, (1, acc_scr.shape[1] // LANES)) * acc_scr[...] + _dot(p.astype(BF16), v1)
        m_scr[...] = m_new

    past_kv = (lambda ref, rs: ref[rs, pl.program_id(1), :]) if past_heads else (lambda ref, rs: ref[rs, :])

    def past_block(rs):
        return (past_kv(kp_ref, rs), rp_ref[rs, :] if has_rope else None, fkp_ref[:, rs] if has_bias else None)

    def new_block(rs):
        return (kn_ref[rs, :], rn_ref[rs, :] if has_rope else None, fkn_ref[:, rs] if has_bias else None)

    if has_past:
        if n_past_blk == 1:
            update(scores(*past_block(slice(None))), past_kv(vp_ref, slice(None)), None)
        else:
            def past_body(j, carry):
                rs = pl.ds(pl.multiple_of(j * tkp, tkp), tkp)
                update(scores(*past_block(rs)), past_kv(vp_ref, rs), None)
                return carry
            lax.fori_loop(0, n_past_blk, past_body, 0)

    row = lax.broadcasted_iota(jnp.int32, (tq, 1), 0)
    col = lax.broadcasted_iota(jnp.int32, (1, tq), 1)
    if mask_mode == "causal":
        mask = col <= row
    elif mask_mode == "chunk":
        mask = (col // CHUNK) <= (row // CHUNK)
    else:
        mask = None

    if single_q:
        update(scores(*new_block(slice(None))), vn_ref[...], mask)
    else:
        blk = lambda j: pl.ds(pl.multiple_of(j * tq, tq), tq)

        def fill(s_ref, j):
            s_ref[...] = scores(*new_block(blk(j)))

        def drain(s_ref, j, msk):
            update(s_ref[...], vn_ref[blk(j), :], msk)

        def pair(j0):
            fill(sb_scr, j0 + 1)
            drain(sa_scr, j0, None)
            fill(sa_scr, j0 + 2)
            drain(sb_scr, j0 + 1, None)

        fill(sa_scr, 0)
        n_quads = qi // 4

        def quad_body(i, carry):
            pair(4 * i)
            pair(4 * i + 2)
            return carry
        lax.fori_loop(0, n_quads, quad_body, 0)

        @pl.when(qi % 4 >= 2)
        def _():
            pair(4 * n_quads)

        @pl.when(qi % 2 == 0)
        def _():
            drain(sa_scr, qi, mask)

        @pl.when(qi % 2 == 1)
        def _():
            fill(sb_scr, qi)
            drain(sa_scr, qi - 1, None)
            drain(sb_scr, qi, mask)

    acc = acc_scr[...]
    o_ref[...] = (acc[:, :dv] / acc[:, dv:]).astype(o_ref.dtype)


def _flash(q, kn, vn, *, n_seq, n_heads, seq_len, tq, dq, dk, dv, q_off, k_off, mask_mode,
           fq=None, fkn=None, rn=None, past=None):
    assert seq_len % tq == 0 and q_off % tq == 0 and k_off % seq_len == 0
    nq = seq_len // tq
    qo = q_off // tq
    ko = k_off // seq_len
    has_bias = fq is not None
    has_rope = rn is not None
    has_past = past is not None
    ins, specs = [q], [pl.BlockSpec((tq, dq), lambda b, h, i: (qo + b * nq + i, h))]
    if has_bias:
        ins.append(fq)
        specs.append(pl.BlockSpec((None, tq, 1), lambda b, h, i: (h, qo + b * nq + i, 0)))
    n_past_blk, tkp, past_heads = 0, 0, False
    if has_past:
        tp = past["k"].shape[1]
        tkp = past["tk"]
        assert tp % tkp == 0
        n_past_blk = tp // tkp
        pb = (lambda b: b) if past["k"].shape[0] > 1 else (lambda b: 0)
        ins += [past["k"], past["v"]]
        past_heads = past["k"].ndim == 4
        if past_heads:
            specs += [pl.BlockSpec((None, tp, n_heads, dk), lambda b, h, i: (pb(b), 0, 0, 0)),
                      pl.BlockSpec((None, tp, n_heads, dv), lambda b, h, i: (pb(b), 0, 0, 0))]
        else:
            specs += [pl.BlockSpec((None, tp, dk), lambda b, h, i: (pb(b), 0, h)),
                      pl.BlockSpec((None, tp, dv), lambda b, h, i: (pb(b), 0, h))]
        if has_rope:
            ins.append(past["r"])
            specs.append(pl.BlockSpec((None, tp, LANES), lambda b, h, i: (pb(b), 0, 0)))
        if has_bias:
            ins.append(past["fk"])
            specs.append(pl.BlockSpec((None, None, 1, tp), lambda b, h, i: (pb(b), h, 0, 0)))
    ins += [kn, vn]
    specs += [pl.BlockSpec((seq_len, dk), lambda b, h, i: (ko + b, h)),
              pl.BlockSpec((seq_len, dv), lambda b, h, i: (ko + b, h))]
    if has_rope:
        ins.append(rn)
        specs.append(pl.BlockSpec((seq_len, LANES), lambda b, h, i: (ko + b, 0)))
    if has_bias:
        ins.append(fkn)
        specs.append(pl.BlockSpec((None, None, 1, seq_len), lambda b, h, i: (h, ko + b, 0, 0)))
    kern = functools.partial(_flash_kernel, n_past_blk=n_past_blk, tkp=tkp, tq=tq, has_bias=has_bias,
                             has_rope=has_rope, mask_mode=mask_mode, has_past=has_past, past_heads=past_heads,
                             single_q=nq == 1)
    return pl.pallas_call(
        kern, name="flash", grid=(n_seq, n_heads, nq), in_specs=specs,
        out_specs=pl.BlockSpec((tq, dv), lambda b, h, i: (b * nq + i, h)),
        out_shape=jax.ShapeDtypeStruct((n_seq * seq_len, n_heads * dv), BF16),
        scratch_shapes=[pltpu.VMEM((tq, LANES), F32), pltpu.VMEM((tq, dv + LANES), F32),
                        pltpu.VMEM((tq, tq) if nq > 1 else (HG_SUB, LANES), F32),
                        pltpu.VMEM((tq, tq) if nq > 1 else (HG_SUB, LANES), F32)],
        compiler_params=_params(("parallel", "parallel", "arbitrary")))(*ins)


FLASH_UNROLL = 4


def _tri_tables(nq):
    pairs = [(qi, kj) for qi in range(nq) for kj in range(qi)] + [(qi, qi) for qi in range(nq)] + [(0, 0)]
    return (jnp.array([p[0] for p in pairs], jnp.int32), jnp.array([p[1] for p in pairs], jnp.int32))


def _flash_tri_kernel(qt_ref, kt_ref, *refs, tq, nq, has_bias, has_rope, mask_mode):
    it = iter(refs)
    q_ref = next(it)
    fq_ref = next(it) if has_bias else None
    kp_ref, vp_ref = next(it), next(it)
    rp_ref = next(it) if has_rope else None
    fkp_ref = next(it) if has_bias else None
    kn_ref, vn_ref = next(it), next(it)
    rn_ref = next(it) if has_rope else None
    fkn_ref = next(it) if has_bias else None
    o_ref = next(it)
    m_scr, acc_scr, sa_scr, sb_scr = next(it), next(it), next(it), next(it)
    fqb_scr = next(it) if has_bias else None
    dv = o_ref.shape[1]
    n_off = nq * (nq - 1) // 2
    assert n_off % FLASH_UNROLL == 0 and nq % FLASH_UNROLL == 0 and FLASH_UNROLL % 2 == 0
    tile = lambda j: pl.ds(pl.multiple_of(j * tq, tq), tq)
    ones = jnp.ones((tq, LANES), BF16)

    kp = kp_ref[...]
    if has_rope:
        kp = jnp.concatenate([kp, rp_ref[...]], axis=1)
    vp1 = jnp.concatenate([vp_ref[...], ones[:vp_ref.shape[0]]], axis=1)
    for i in range(nq):
        rs = slice(i * tq, (i + 1) * tq)
        s = _dot_nt(q_ref[rs, :], kp)
        if has_bias:
            fb = fq_ref[rs, :] * LOG2E
            fqb_scr[rs, :] = jnp.broadcast_to(fb, (tq, LANES))
            s = s + fb - fkp_ref[...] * LOG2E
        m0 = jnp.max(s, axis=1, keepdims=True)
        m_scr[i] = jnp.broadcast_to(m0, (tq, LANES))
        acc_scr[i] = _dot(jnp.exp2(s - m0).astype(BF16), vp1)

    def fill(s_ref, t):
        qs, ks = tile(qt_ref[t]), tile(kt_ref[t])
        k = kn_ref[ks, :]
        if has_rope:
            k = jnp.concatenate([k, rn_ref[ks, :]], axis=1)
        s = _dot_nt(q_ref[qs, :], k)
        if has_bias:
            s = s + jnp.tile(fqb_scr[qs, :], (1, tq // LANES)) - fkn_ref[:, ks] * LOG2E
        s_ref[...] = s

    def drain(s_ref, t, mask):
        qi = qt_ref[t]
        s = s_ref[...]
        if mask is not None:
            s = jnp.where(mask, s, NEG)
        m_prev = m_scr[qi]
        m_new = jnp.maximum(m_prev, jnp.max(s, axis=1, keepdims=True))
        p = jnp.exp2(s - jnp.tile(m_new, (1, tq // LANES)))
        v1 = jnp.concatenate([vn_ref[tile(kt_ref[t]), :], ones], axis=1)
        acc = jnp.tile(jnp.exp2(m_prev - m_new), (1, (dv + LANES) // LANES)) * acc_scr[qi] + _dot(p.astype(BF16), v1)
        return qi, m_new, acc

    def keep(s_ref, t):
        qi, m_new, acc = drain(s_ref, t, None)
        m_scr[qi] = m_new
        acc_scr[qi] = acc

    row = lax.broadcasted_iota(jnp.int32, (tq, 1), 0)
    col = lax.broadcasted_iota(jnp.int32, (1, tq), 1)
    mask = {"causal": col <= row, "chunk": (col // CHUNK) <= (row // CHUNK)}[mask_mode]

    def finish(s_ref, t):
        qi, _, acc = drain(s_ref, t, mask)
        o_ref[tile(qi), :] = (acc[:, :dv] / acc[:, dv:]).astype(o_ref.dtype)

    def pipeline(t0, n, consume):
        def body(i, carry):
            t = t0 + FLASH_UNROLL * i
            for u in range(0, FLASH_UNROLL, 2):
                fill(sb_scr, t + u + 1)
                consume(sa_scr, t + u)
                fill(sa_scr, t + u + 2)
                consume(sb_scr, t + u + 1)
            return carry
        lax.fori_loop(0, n // FLASH_UNROLL, body, 0)

    fill(sa_scr, 0)
    pipeline(0, n_off, keep)
    pipeline(n_off, nq, finish)


def _flash_tri(q, kn, vn, *, n_seq, n_heads, seq_len, tq, dq, dk, dv, mask_mode, past, fq=None, fkn=None, rn=None):
    nq = seq_len // tq
    has_bias = fq is not None
    has_rope = rn is not None
    tp = past["k"].shape[1]
    m3 = lambda f: (lambda b, h, qt, kt: f(b, h))
    ins, specs = [q], [pl.BlockSpec((seq_len, dq), m3(lambda b, h: (b, h)))]
    if has_bias:
        ins.append(fq)
        specs.append(pl.BlockSpec((None, seq_len, 1), m3(lambda b, h: (h, b, 0))))
    ins += [past["k"], past["v"]]
    specs += [pl.BlockSpec((None, tp, dk), m3(lambda b, h: (0, 0, h))),
              pl.BlockSpec((None, tp, dv), m3(lambda b, h: (0, 0, h)))]
    if has_rope:
        ins.append(past["r"])
        specs.append(pl.BlockSpec((None, tp, LANES), m3(lambda b, h: (0, 0, 0))))
    if has_bias:
        ins.append(past["fk"])
        specs.append(pl.BlockSpec((None, None, 1, tp), m3(lambda b, h: (0, h, 0, 0))))
    ins += [kn, vn]
    specs += [pl.BlockSpec((seq_len, dk), m3(lambda b, h: (b, h))),
              pl.BlockSpec((seq_len, dv), m3(lambda b, h: (b, h)))]
    if has_rope:
        ins.append(rn)
        specs.append(pl.BlockSpec((seq_len, LANES), m3(lambda b, h: (b, 0))))
    if has_bias:
        ins.append(fkn)
        specs.append(pl.BlockSpec((None, None, 1, seq_len), m3(lambda b, h: (h, b, 0, 0))))
    scratch = [pltpu.VMEM((nq, tq, LANES), F32), pltpu.VMEM((nq, tq, dv + LANES), F32),
               pltpu.VMEM((tq, tq), F32), pltpu.VMEM((tq, tq), F32)]
    if has_bias:
        scratch.append(pltpu.VMEM((seq_len, LANES), F32))
    grid_spec = pltpu.PrefetchScalarGridSpec(
        num_scalar_prefetch=2, grid=(n_seq, n_heads), in_specs=specs,
        out_specs=pl.BlockSpec((seq_len, dv), m3(lambda b, h: (b, h))), scratch_shapes=scratch)
    kern = functools.partial(_flash_tri_kernel, tq=tq, nq=nq, has_bias=has_bias, has_rope=has_rope,
                             mask_mode=mask_mode)
    return pl.pallas_call(
        kern, name="flash_tri", grid_spec=grid_spec,
        out_shape=jax.ShapeDtypeStruct((n_seq * seq_len, n_heads * dv), BF16),
        compiler_params=_params(("parallel", "arbitrary")))(*_tri_tables(nq), *ins)


def _route(sc, sb):
    def top2_sum(v):
        a, b, c, d = v
        a, b = jnp.maximum(a, b), jnp.minimum(a, b)
        c, d = jnp.maximum(c, d), jnp.minimum(c, d)
        hi, lo2 = jnp.maximum(a, c), jnp.minimum(a, c)
        return hi + jnp.maximum(lo2, jnp.maximum(b, d))

    gs = [top2_sum(sb[g * EXPERTS_PER_GROUP:(g + 1) * EXPERTS_PER_GROUP]) for g in range(N_GROUPS)]
    best_v, best_g = gs[0], jnp.zeros(gs[0].shape, jnp.int32)
    for g in range(1, N_GROUPS):
        upd = gs[g] > best_v
        best_v = jnp.where(upd, gs[g], best_v)
        best_g = jnp.where(upd, g, best_g)
    masked = [jnp.where(best_g == (e // EXPERTS_PER_GROUP), sb[e], -jnp.inf) for e in range(N_EXPERTS)]

    def argmax_first(vals, exclude=None):
        bv = jnp.full(vals[0].shape, -jnp.inf, F32)
        bi = jnp.full(vals[0].shape, -1, jnp.int32)
        for e, v in enumerate(vals):
            upd = v > bv
            if exclude is not None:
                upd = upd & (exclude != e)
            bv = jnp.where(upd, v, bv)
            bi = jnp.where(upd, e, bi)
        return bi

    i1 = argmax_first(masked)
    i2 = argmax_first(masked, exclude=i1)
    w1 = sum(jnp.where(i1 == e, sc[e], 0.0) for e in range(N_EXPERTS))
    w2 = sum(jnp.where(i2 == e, sc[e], 0.0) for e in range(N_EXPERTS))
    tot = w1 + w2
    w1, w2 = w1 / tot, w2 / tot
    comb = [jnp.where(i1 == e, w1, 0.0) + jnp.where(i2 == e, w2, 0.0) for e in range(N_EXPERTS)]
    return comb + [i1.astype(F32), i2.astype(F32), w1, w2]


def _mix_kernel(*refs, n_act):
    x_ref = refs[0]
    a_refs = refs[1:1 + n_act]
    w_ref, g_ref, b_ref, rw_ref, rb_ref, x1_ref, x1p_ref, comb_ref, ct_scr = refs[1 + n_act:]
    half = D_MODEL // 2
    ys = []
    for n0 in (0, half):
        y = None
        k0 = 0
        for a_ref in a_refs:
            kw = a_ref.shape[1]
            part = _dot(a_ref[...], w_ref[k0:k0 + kw, n0:n0 + half])
            y = part if y is None else y + part
            k0 += kw
        ys.append(y)
    x1 = _layer_norm(ALPHA * x_ref[...] + jnp.concatenate(ys, axis=1), g_ref[...], b_ref[...])
    x1_ref[...] = x1

    x1p_ref[...] = _pack_pair(x1[:, :half], x1[:, half:])
    logits = _dot(x1.astype(BF16), rw_ref[...])
    scores_t = _sigmoid(logits).T
    sc = [scores_t[e:e + 1, :] for e in range(N_EXPERTS)]
    sb = [sc[e] + rb_ref[e:e + 1, :] for e in range(N_EXPERTS)]
    route_rows = _route(sc, sb)
    ct_scr[...] = jnp.zeros(ct_scr.shape, F32)
    for r, val in enumerate(route_rows):
        ct_scr[r:r + 1, :] = val
    comb_ref[...] = ct_scr[...].T


def _mix(x, acts, w_out, ln_g, ln_b, rw, rb, tm):
    rows = x.shape[0]
    return _row_call(functools.partial(_mix_kernel, n_act=len(acts)), "mix", rows, tm,
                     [x] + list(acts), [w_out, ln_g, ln_b, rw, rb],
                     [(D_MODEL, F32), (D_MODEL // 2, jnp.uint32), (LANES, F32)],
                     scratch=[pltpu.VMEM((LANES, tm), F32)])


def _moe_kernel(x_ref, comb_ref, wg_ref, wu_ref, wd_ref, g_ref, b_ref, o_ref, xb_scr, acc_scr):
    e = pl.program_id(1)

    @pl.when(e == 0)
    def _():
        xb_scr[...] = x_ref[...].astype(BF16)
        acc_scr[...] = jnp.zeros(acc_scr.shape, F32)

    xb = xb_scr[...]
    lane = lax.broadcasted_iota(jnp.int32, (1, LANES), 1)
    c_e = jnp.sum(jnp.where(lane == e, comb_ref[...], 0.0), axis=1, keepdims=True)
    gate = _dot(xb, wg_ref[...].astype(BF16))
    h = gate * _sigmoid(gate) * _dot(xb, wu_ref[...].astype(BF16))
    acc_scr[...] += _dot((h * c_e).astype(BF16), wd_ref[...].astype(BF16))

    @pl.when(e == N_EXPERTS - 1)
    def _():
        o_ref[...] = _layer_norm(ALPHA * x_ref[...] + acc_scr[...], g_ref[...], b_ref[...])


def _moe(x, comb, wg, wu, wd, layer, ln_g, ln_b, tm):
    rows = x.shape[0]
    assert rows % tm == 0
    return pl.pallas_call(
        _moe_kernel, name="moe", grid=(rows // tm, N_EXPERTS),
        in_specs=[pl.BlockSpec((tm, D_MODEL), lambda i, e: (i, 0)),
                  pl.BlockSpec((tm, LANES), lambda i, e: (i, 0)),
                  pl.BlockSpec((None, None, D_MODEL, D_EXPERT), lambda i, e: (layer, e, 0, 0)),
                  pl.BlockSpec((None, None, D_MODEL, D_EXPERT), lambda i, e: (layer, e, 0, 0)),
                  pl.BlockSpec((None, None, D_EXPERT, D_MODEL), lambda i, e: (layer, e, 0, 0)),
                  _full_spec(ln_g), _full_spec(ln_b)],
        out_specs=pl.BlockSpec((tm, D_MODEL), lambda i, e: (i, 0)),
        out_shape=jax.ShapeDtypeStruct((rows, D_MODEL), F32),
        scratch_shapes=[pltpu.VMEM((tm, D_MODEL), BF16), pltpu.VMEM((tm, D_MODEL), F32)],
        compiler_params=_params(("parallel", "arbitrary")))(x, comb, wg, wu, wd, ln_g, ln_b)


ROUTE_E1, ROUTE_E2, ROUTE_W1, ROUTE_W2 = N_EXPERTS, N_EXPERTS + 1, N_EXPERTS + 2, N_EXPERTS + 3
TE = 512
SC_WINDOW = 128


def _pack_pair(a, b):
    au = lax.bitcast_convert_type(a.astype(BF16).astype(F32), jnp.uint32)
    bu = lax.bitcast_convert_type(b.astype(BF16).astype(F32), jnp.uint32)
    return (au >> 16) | (bu & jnp.uint32(0xFFFF0000))


def _unpack_pair(w):
    a = lax.bitcast_convert_type(w << 16, F32)
    b = lax.bitcast_convert_type(w & jnp.uint32(0xFFFF0000), F32)
    return a, b


def _rank_kernel(route_ref, pos_ref, texp_ref, nused_ref, cnt_scr, carry_scr, seg_scr, before_scr):
    ph, i = pl.program_id(0), pl.program_id(1)
    T = route_ref.shape[0]
    lane = lax.broadcasted_iota(jnp.int32, (1, LANES), 1)
    lane_f = lane.astype(F32)
    r = route_ref[...]
    e1, e2 = r[:, ROUTE_E1:ROUTE_E1 + 1], r[:, ROUTE_E2:ROUTE_E2 + 1]
    m1, m2 = lane_f == e1, lane_f == e2
    m = jnp.where(m1 | m2, 1.0, 0.0)
    colsum = jnp.sum(m, axis=0, keepdims=True)

    @pl.when((ph == 0) & (i == 0))
    def _():
        cnt_scr[...] = jnp.zeros(cnt_scr.shape, F32)

    @pl.when(ph == 0)
    def _():
        cnt_scr[...] += colsum

    @pl.when((ph == 1) & (i == 0))
    def _():
        cnt = cnt_scr[...].astype(jnp.int32)
        padded = (((cnt + (TE - 1)) // TE) * TE).astype(F32)
        rr = lax.broadcasted_iota(jnp.int32, (LANES, 1), 0)
        upper = (rr < lane).astype(BF16)
        hi, mid, lo = _split3(jnp.broadcast_to(padded, (HG_SUB, LANES)))
        seg = (_dot(hi, upper) + _dot(mid, upper) + _dot(lo, upper))[:1, :]
        seg_scr[...] = seg
        carry_scr[...] = jnp.zeros(carry_scr.shape, F32)
        seg_end = seg + padded
        tile_row = lax.broadcasted_iota(jnp.int32, texp_ref.shape, 1).astype(F32) * float(TE)
        te_acc = jnp.zeros(texp_ref.shape, jnp.int32)
        for e in range(N_EXPERTS):
            te_acc = te_acc + jnp.where(seg_end[:, e:e + 1] <= tile_row, 1, 0)
        texp_ref[...] = jnp.minimum(te_acc, N_EXPERTS - 1)
        nused_ref[...] = jnp.broadcast_to(seg_end[:, N_EXPERTS - 1:N_EXPERTS] / float(TE), nused_ref.shape).astype(jnp.int32)

    @pl.when((ph == 1) & (i == 0))
    def _():
        row = lax.broadcasted_iota(jnp.int32, (T, 1), 0)
        col = lax.broadcasted_iota(jnp.int32, (1, T), 1)
        before_scr[...] = (col < row).astype(BF16)

    @pl.when(ph == 1)
    def _():
        cum = _dot(before_scr[...], m.astype(BF16)) + carry_scr[...] + seg_scr[...]
        p1 = jnp.sum(jnp.where(m1, cum, 0.0), axis=1, keepdims=True)
        p2 = jnp.sum(jnp.where(m2, cum, 0.0), axis=1, keepdims=True)
        pos_ref[...] = jnp.where(lane == 0, p1, jnp.where(lane == 1, p2, 0.0)).astype(jnp.int32)
        carry_scr[...] += colsum


def _rank(route, n_tiles, tm):
    rows = route.shape[0]
    nb = rows // tm
    nt_pad = -(-n_tiles // LANES) * LANES
    return pl.pallas_call(
        _rank_kernel, name="rank", grid=(2, nb),
        in_specs=[pl.BlockSpec((tm, LANES), lambda ph, i: (i, 0))],
        out_specs=[pl.BlockSpec((tm, LANES), lambda ph, i: (i * ph, 0)),
                   pl.BlockSpec((1, nt_pad), lambda ph, i: (0, 0)),
                   pl.BlockSpec((1, LANES), lambda ph, i: (0, 0))],
        out_shape=[jax.ShapeDtypeStruct((rows, LANES), jnp.int32),
                   jax.ShapeDtypeStruct((1, nt_pad), jnp.int32),
                   jax.ShapeDtypeStruct((1, LANES), jnp.int32)],
        scratch_shapes=[pltpu.VMEM((1, LANES), F32), pltpu.VMEM((1, LANES), F32), pltpu.VMEM((1, LANES), F32),
                        pltpu.VMEM((tm, tm), BF16)],
        compiler_params=_params(("arbitrary", "arbitrary")))(route)


def _sc_mesh():
    return plsc.VectorSubcoreMesh(core_axis_name="c", subcore_axis_name="s")


def _sc_scatter_rows(x, idx, n_out):
    rows, d = x.shape
    mesh = _sc_mesh()
    n_workers = mesh.num_cores * mesh.num_subcores
    steps = idx.shape[1] // SC_WINDOW // n_workers
    assert steps * SC_WINDOW * n_workers == idx.shape[1] and rows % SC_WINDOW == 0

    @functools.partial(pl.kernel, out_type=jax.ShapeDtypeStruct((n_out, d), x.dtype), mesh=mesh,
                       scratch_types=[pltpu.VMEM((1, SC_WINDOW), jnp.int32), pltpu.VMEM((SC_WINDOW, d), x.dtype)])
    def scatter(x_hbm, i_hbm, o_hbm, i_vmem, buf):
        first = (lax.axis_index("c") * mesh.num_subcores + lax.axis_index("s")) * steps

        @pl.loop(0, steps)
        def _(t):
            off = (first + t) * SC_WINDOW
            pltpu.sync_copy(i_hbm.at[:, pl.ds(off, SC_WINDOW)], i_vmem)
            pltpu.sync_copy(x_hbm.at[pl.ds(off % rows, SC_WINDOW)], buf)
            pltpu.sync_copy(buf, o_hbm.at[i_vmem.at[0]])

    return scatter(x, idx)


def _sc_gather_rows(x, idx):
    d = x.shape[1]
    n = idx.shape[1]
    mesh = _sc_mesh()
    n_workers = mesh.num_cores * mesh.num_subcores
    steps = n // SC_WINDOW // n_workers
    assert steps * SC_WINDOW * n_workers == n

    @functools.partial(pl.kernel, out_type=jax.ShapeDtypeStruct((n, d), x.dtype), mesh=mesh,
                       scratch_types=[pltpu.VMEM((1, SC_WINDOW), jnp.int32), pltpu.VMEM((SC_WINDOW, d), x.dtype)])
    def gather(x_hbm, i_hbm, o_hbm, i_vmem, buf):
        first = (lax.axis_index("c") * mesh.num_subcores + lax.axis_index("s")) * steps

        @pl.loop(0, steps)
        def _(t):
            off = (first + t) * SC_WINDOW
            pltpu.sync_copy(i_hbm.at[:, pl.ds(off, SC_WINDOW)], i_vmem)
            pltpu.sync_copy(x_hbm.at[i_vmem.at[0]], buf)
            pltpu.sync_copy(buf, o_hbm.at[pl.ds(off, SC_WINDOW)])

    return gather(x, idx)


def _gmm_kernel(texp_ref, nused_ref, x_ref, wg_ref, wu_ref, wd_ref, o_ref):
    @pl.when(pl.program_id(0) < nused_ref[0])
    def _():
        a, b = _unpack_pair(x_ref[...])
        xb = jnp.concatenate([a.astype(BF16), b.astype(BF16)], axis=1)
        gate = _dot(xb, wg_ref[...].astype(BF16))
        h = gate * _sigmoid(gate) * _dot(xb, wu_ref[...].astype(BF16))
        y = _dot(h.astype(BF16), wd_ref[...].astype(BF16))
        o_ref[...] = _pack_pair(y[:, :D_MODEL // 2], y[:, D_MODEL // 2:])


def _gmm(xs, texp, nused, wg, wu, wd, layer):
    rows = xs.shape[0]
    wmap = lambda d, te, nu: (layer, te[d], 0, 0)
    grid_spec = pltpu.PrefetchScalarGridSpec(
        num_scalar_prefetch=2, grid=(rows // TE,),
        in_specs=[pl.BlockSpec((TE, D_MODEL // 2), lambda d, te, nu: (d, 0)),
                  pl.BlockSpec((None, None, D_MODEL, D_EXPERT), wmap),
                  pl.BlockSpec((None, None, D_MODEL, D_EXPERT), wmap),
                  pl.BlockSpec((None, None, D_EXPERT, D_MODEL), wmap)],
        out_specs=pl.BlockSpec((TE, D_MODEL // 2), lambda d, te, nu: (d, 0)))
    return pl.pallas_call(
        _gmm_kernel, name="gmm", grid_spec=grid_spec,
        out_shape=jax.ShapeDtypeStruct((rows, D_MODEL // 2), jnp.uint32),
        compiler_params=_params(("arbitrary",)))(texp, nused, xs, wg, wu, wd)


def _combine_kernel(x_ref, g0_ref, g1_ref, route_ref, g_ref, b_ref, o_ref):
    r = route_ref[...]
    y0 = jnp.concatenate(_unpack_pair(g0_ref[...]), axis=1)
    y1 = jnp.concatenate(_unpack_pair(g1_ref[...]), axis=1)
    f = y0 * r[:, ROUTE_W1:ROUTE_W1 + 1] + y1 * r[:, ROUTE_W2:ROUTE_W2 + 1]
    o_ref[...] = _layer_norm(ALPHA * x_ref[...] + f, g_ref[...], b_ref[...])


def _combine(x, g, route, ln_g, ln_b, tm):
    rows = x.shape[0]
    nb = rows // tm
    return pl.pallas_call(
        _combine_kernel, name="combine", grid=(nb,),
        in_specs=[pl.BlockSpec((tm, D_MODEL), lambda i: (i, 0)),
                  pl.BlockSpec((tm, D_MODEL // 2), lambda i: (i, 0)),
                  pl.BlockSpec((tm, D_MODEL // 2), lambda i: (nb + i, 0)),
                  pl.BlockSpec((tm, LANES), lambda i: (i, 0)), _full_spec(ln_g), _full_spec(ln_b)],
        out_specs=pl.BlockSpec((tm, D_MODEL), lambda i: (i, 0)),
        out_shape=jax.ShapeDtypeStruct((rows, D_MODEL), F32),
        compiler_params=_params(("parallel",)))(x, g, g, route, ln_g, ln_b)


def _moe_routed(x1, x1b, route, wg, wu, wd, layer, ln_g, ln_b, tm):
    rows = x1.shape[0]
    n_rows = 2 * rows + N_EXPERTS * TE
    pos, texp, nused = _rank(route, n_rows // TE, tm)
    idx = jnp.concatenate([pos[:, 0], pos[:, 1]])[None, :]
    xs = _sc_scatter_rows(x1b, idx, n_rows)
    ys = _gmm(xs, texp[0, :n_rows // TE], nused[0, :1], wg, wu, wd, layer)
    g = _sc_gather_rows(ys, idx)
    return _combine(x1, g, route, ln_g, ln_b, tm)


def _rope128(x, cos_t, sin_t):
    lane = lax.broadcasted_iota(jnp.int32, (1, LANES), 1)
    half = MLA_ROPE // 2
    swapped = jnp.where(lane < half, pltpu.roll(x, LANES - half, axis=1), pltpu.roll(x, half, axis=1))
    return x * cos_t + swapped * sin_t


def _odd_proj_kernel(x_ref, cos_ref, sin_ref, w_ref, gq_ref, gkv_ref, wuq_ref,
                     q_ref, ckv_ref, kpe_ref, kpe16_ref):
    z = _dot(x_ref[...].astype(BF16), w_ref[...])
    cq = _rms_norm(z[:, :MLA_Q_LORA], gq_ref[...])
    ckv_ref[...] = _rms_norm(z[:, MLA_Q_LORA:MLA_Q_LORA + MLA_KV_LORA], gkv_ref[...])
    cos_t, sin_t = cos_ref[...], sin_ref[...]
    kpe = _rope128(z[:, MLA_Q_LORA + MLA_KV_LORA:], cos_t, sin_t)
    kpe_ref[...] = kpe[:, :MLA_ROPE]
    kpe16_ref[...] = kpe.astype(BF16)
    qf = _dot(cq.astype(BF16), wuq_ref[...])
    scale = (MLA_NOPE + MLA_ROPE) ** -0.5 * LOG2E
    for h in range(MLA_HEADS):
        c0 = h * MLA_QPAD
        q_ref[:, c0:c0 + MLA_NOPE] = (qf[:, c0:c0 + MLA_NOPE] * scale).astype(BF16)
        qr = _rope128(qf[:, c0 + MLA_NOPE:c0 + MLA_QPAD], cos_t, sin_t)
        q_ref[:, c0 + MLA_NOPE:c0 + MLA_QPAD] = (qr * scale).astype(BF16)


def _odd_proj(x, cos_t, sin_t, w_in, gq, gkv, wuq, tm):
    rows = x.shape[0]
    outs = [(MLA_HEADS * MLA_QPAD, BF16), (MLA_KV_LORA, F32), (MLA_ROPE, F32), (LANES, BF16)]
    return _row_call(_odd_proj_kernel, "odd_proj", rows, tm, [x, cos_t, sin_t], [w_in, gq, gkv, wuq], outs)


def _kv_expand_kernel(c_ref, w_ref, k_ref, v_ref):
    kv = _dot(c_ref[...].astype(BF16), w_ref[...])
    n = MLA_HEADS * MLA_NOPE
    k_ref[...] = kv[:, :n].astype(BF16)
    v_ref[...] = kv[:, n:].astype(BF16)


def _kv_expand(ckv, w_ukv, tm):
    rows = ckv.shape[0]
    return _row_call(_kv_expand_kernel, "kv_expand", rows, tm, [ckv], [w_ukv],
                     [(MLA_HEADS * MLA_NOPE, BF16), (MLA_HEADS * MLA_V, BF16)])


def _mla_absorbed_kernel(q_ref, cp_ref, rp_ref, cn_ref, rn_ref, wuk_ref, wuv_ref, o_ref, m_scr, acc_scr, *, tkp):
    tq = q_ref.shape[0]
    rows = MLA_HEADS * tq
    q = q_ref[...]
    qa = []
    for h in range(MLA_HEADS):
        c0 = h * MLA_QPAD
        q_abs = _dot_nt(q[:, c0:c0 + MLA_NOPE], wuk_ref[h])
        qa.append(jnp.concatenate([q_abs.astype(BF16), q[:, c0 + MLA_NOPE:c0 + MLA_NOPE + MLA_ROPE]], axis=1))
    qs = jnp.concatenate(qa, axis=0)
    m_scr[...] = jnp.full(m_scr.shape, NEG, F32)
    acc_scr[...] = jnp.zeros(acc_scr.shape, F32)

    def update(c, r):
        c = c.astype(BF16)
        s = _dot_nt(qs, jnp.concatenate([c, r.astype(BF16)], axis=1))
        m_prev = m_scr[...]
        m_new = jnp.maximum(m_prev, jnp.max(s, axis=1, keepdims=True))
        if s.shape[1] % LANES == 0:
            p = jnp.exp2(s - jnp.tile(m_new, (1, s.shape[1] // LANES)))
        else:
            p = jnp.exp2(s - m_new[:, :1])
        c1 = jnp.concatenate([c, jnp.ones((c.shape[0], LANES), BF16)], axis=1)
        acc_scr[...] = (jnp.tile(jnp.exp2(m_prev - m_new), (1, acc_scr.shape[1] // LANES)) * acc_scr[...]
                        + _dot(p.astype(BF16), c1))
        m_scr[...] = m_new

    def past_body(j, carry):
        rs = pl.ds(pl.multiple_of(j * tkp, tkp), tkp)
        update(cp_ref[rs, :], rp_ref[rs, :])
        return carry
    lax.fori_loop(0, cp_ref.shape[0] // tkp, past_body, 0)
    update(cn_ref[...], rn_ref[:, :MLA_ROPE])

    acc = acc_scr[...]
    lat = (acc[:, :MLA_KV_LORA] / jnp.tile(acc[:, MLA_KV_LORA:], (1, MLA_KV_LORA // LANES))).astype(BF16)
    for h in range(MLA_HEADS):
        o_ref[:, h * MLA_V:(h + 1) * MLA_V] = _dot(lat[h * tq:(h + 1) * tq, :],
                                                   wuv_ref[:, h * MLA_V:(h + 1) * MLA_V]).astype(o_ref.dtype)


def _mla_absorbed(q, ckv_past, kpe_past, ckv_new, kpe_new, wuk_t, wuv, n_seq, tq, tkp):
    p = ckv_past.shape[1]
    assert p % tkp == 0
    rows = MLA_HEADS * tq
    return pl.pallas_call(
        functools.partial(_mla_absorbed_kernel, tkp=tkp), name="mla_absorbed", grid=(n_seq,),
        in_specs=[pl.BlockSpec((tq, MLA_HEADS * MLA_QPAD), lambda b: (b, 0)),
                  pl.BlockSpec((None, p, MLA_KV_LORA), lambda b: (b, 0, 0)),
                  pl.BlockSpec((None, p, MLA_ROPE), lambda b: (b, 0, 0)),
                  pl.BlockSpec((tq, MLA_KV_LORA), lambda b: (b, 0)),
                  pl.BlockSpec((tq, LANES), lambda b: (b, 0)),
                  _full_spec(wuk_t), _full_spec(wuv)],
        out_specs=pl.BlockSpec((tq, MLA_HEADS * MLA_V), lambda b: (b, 0)),
        out_shape=jax.ShapeDtypeStruct((n_seq * tq, MLA_HEADS * MLA_V), BF16),
        scratch_shapes=[pltpu.VMEM((rows, LANES), F32), pltpu.VMEM((rows, MLA_KV_LORA + LANES), F32)],
        compiler_params=_params(("parallel",)))(q, ckv_past, kpe_past, ckv_new, kpe_new, wuk_t, wuv)


def _rope_tables(pos):
    half = MLA_ROPE // 2
    inv = ROPE_BASE ** (-jnp.arange(half, dtype=F32) / half)
    ang = pos.astype(F32)[:, None] * inv[None, :]
    cos, sin = jnp.cos(ang), jnp.sin(ang)
    z = jnp.zeros((pos.shape[0], LANES - MLA_ROPE), F32)
    return jnp.concatenate([cos, cos, z], axis=1), jnp.concatenate([-sin, sin, z], axis=1)


def _pad_rows(a, n):
    return jnp.pad(a, ((0, n - a.shape[0]),) + ((0, 0),) * (a.ndim - 1))


def kernel(x_prompt, x_sample, state_hgrn2, cache_fox_k, cache_fox_v, cache_fox_logf, cache_mla_ckv, cache_mla_kpe, meta_tokens, even_w_in, hg_lb_logits, hg_norm_g, fox_forget_bias, even_w_out, mla_w_in, mla_q_norm_g, mla_kv_norm_g, mla_w_uq, mla_w_uk, mla_w_uv, mla_w_out, ln_mix_g, ln_mix_b, ln_ffn_g, ln_ffn_b, router_w, router_bias, moe_w_gate, moe_w_up, moe_w_down):
    B, T, _ = x_prompt.shape
    Bs, Ts, _ = x_sample.shape
    P = cache_fox_k.shape[2]
    RM = B * T
    RS = Bs * Ts
    RSM = -(-(RS + N_META) // LANES) * LANES
    ME = slice(RS, RS + N_META)
    TM_MAIN, TM_MOE, TQ = 512, 1024, 512

    xm = x_prompt.reshape(RM, D_MODEL)
    xs = _pad_rows(jnp.concatenate([x_sample.reshape(RS, D_MODEL), meta_tokens.astype(F32)], axis=0), RSM)

    w_in0 = even_w_in[0]
    n_main = 7 * HG_W
    w_even = w_in0[:, :n_main].astype(BF16)
    w_even_f = jnp.pad(w_in0[:, n_main:], ((0, 0), (0, LANES - FOX_HEADS))).astype(BF16)
    fb_pad = jnp.pad(fox_forget_bias[0][None, :], ((0, 0), (0, LANES - FOX_HEADS)))
    g_hg = hg_norm_g[0].reshape(1, HG_W)
    w_out0 = even_w_out[0].astype(BF16)
    e_mat = ((jnp.arange(HG_SUB * HG_DK)[:, None] // HG_DK) == (jnp.arange(CHUNK)[None, :] % HG_SUB)).astype(BF16)

    w_odd = jnp.pad(mla_w_in[0], ((0, 0), (0, LANES - MLA_ROPE))).astype(BF16)
    gq = mla_q_norm_g[0][None, :]
    gkv = mla_kv_norm_g[0][None, :]
    wuq = mla_w_uq[0].reshape(MLA_Q_LORA, MLA_HEADS, MLA_NOPE + MLA_ROPE)
    wuq = jnp.pad(wuq, ((0, 0), (0, 0), (0, MLA_QPAD - MLA_NOPE - MLA_ROPE)))
    wuq = wuq.reshape(MLA_Q_LORA, MLA_HEADS * MLA_QPAD).astype(BF16)
    w_ukv = jnp.concatenate([mla_w_uk[0].reshape(MLA_KV_LORA, -1), mla_w_uv[0].reshape(MLA_KV_LORA, -1)],
                            axis=1).astype(BF16)
    w_out1 = mla_w_out[0].astype(BF16)

    rw = jnp.pad(router_w, ((0, 0), (0, LANES - N_EXPERTS))).astype(BF16)
    rb = jnp.pad(router_bias.astype(F32)[:, None], ((0, LANES - N_EXPERTS), (0, 0)))
    experts = (moe_w_gate, moe_w_up, moe_w_down)
    row2 = lambda a: a[None, :]

    def ffn(x, acts, w_out, l, tm_mix, tm_moe, routed):
        x1, x1b, route = _mix(x, acts, w_out, row2(ln_mix_g[l]), row2(ln_mix_b[l]), rw, rb, tm_mix)
        ln = (row2(ln_ffn_g[l]), row2(ln_ffn_b[l]))
        if routed:
            return _moe_routed(x1, x1b, route, *experts, l, *ln, tm_mix)
        return _moe(x1, route, *experts, l, *ln, tm_moe)

    pm = _even_proj(xm, w_even, w_even_f, hg_lb_logits, fb_pad, TM_MAIN, 0)
    ps = _even_proj(xs, w_even, w_even_f, hg_lb_logits, fb_pad, RSM, 0)
    names = ("hq", "lf", "hk", "hv", "hgate", "fq", "fk", "fv", "fk16", "fv16", "flf")
    pm = dict(zip(names, pm))
    ps = dict(zip(names, ps))

    hg_keys = ("hq", "lf", "hk", "hv", "hgate", "flf")
    meta_in = [_pad_rows(ps[n][ME], CHUNK) for n in hg_keys]
    zero_s = jnp.zeros((1, HG_HEADS, HG_DK, HG_DV), F32)
    zero_f = jnp.zeros((1, 1, LANES), F32)
    o_hg_meta, fc_meta, s_meta = _hgrn2(*meta_in, g_hg, e_mat, zero_s, zero_f, 1, CHUNK, 0, CHUNK)
    o_hg_meta, fc_meta = o_hg_meta[:N_META], fc_meta[:N_META]
    f_meta_end = fc_meta[N_META - 1:N_META][None]

    o_hg_m, fc_m, s_main = _hgrn2(*[pm[n] for n in hg_keys], g_hg, e_mat, s_meta, f_meta_end, B, T, 0, 256)

    logf_c = jnp.pad(jnp.transpose(cache_fox_logf[0], (0, 2, 1)), ((0, 0), (0, HG_SUB - FOX_HEADS), (0, 0)))
    fpast = _cumsum_lanes(logf_c, 512)[:, :FOX_HEADS, :]
    f0_s = jnp.pad(fpast[:, :, P - 1][:, None, :], ((0, 0), (0, 0), (0, LANES - FOX_HEADS)))
    o_hg_s, fc_s, s_samp = _hgrn2(*[ps[n] for n in hg_keys], g_hg, e_mat, state_hgrn2[0], f0_s, Bs, Ts, 0, CHUNK)

    def bias_layouts(fc, n_seq, seq_len):
        f4 = fc[:, :FOX_HEADS].T
        return f4[:, :, None], f4.reshape(FOX_HEADS, n_seq, 1, seq_len)

    fq_m, fk_m = bias_layouts(fc_m, B, T)
    fq_s, fk_s = bias_layouts(fc_s, Bs, Ts)
    fq_t, fk_t = bias_layouts(fc_meta, 1, N_META)

    fox_kw = dict(n_heads=FOX_HEADS, dq=FOX_DH, dk=FOX_DH, dv=FOX_DH, mask_mode="causal")
    meta_past = dict(k=ps["fk16"][ME][None], v=ps["fv16"][ME][None],
                     fk=jnp.transpose(fk_t, (1, 0, 2, 3)), tk=N_META)
    o_fox_m = _flash_tri(pm["fq"], pm["fk16"], pm["fv16"], n_seq=B, seq_len=T, tq=TQ,
                         fq=fq_m, fkn=fk_m, past=meta_past, **fox_kw)
    samp_past = dict(k=cache_fox_k[0], v=cache_fox_v[0],
                     fk=fpast[:, :, None, :], tk=1024)
    o_fox_s = _flash(ps["fq"], ps["fk16"], ps["fv16"], n_seq=Bs, seq_len=Ts, tq=Ts, q_off=0, k_off=0,
                     fq=fq_s, fkn=fk_s, past=samp_past, **fox_kw)
    o_fox_t = _flash(ps["fq"][ME], ps["fk16"][ME], ps["fv16"][ME], n_seq=1, seq_len=N_META, tq=N_META,
                     q_off=0, k_off=0, fq=fq_t, fkn=fk_t, **fox_kw)

    o_hg_small = _pad_rows(jnp.concatenate([o_hg_s, o_hg_meta], axis=0), RSM)
    o_fox_small = _pad_rows(jnp.concatenate([o_fox_s, o_fox_t], axis=0), RSM)
    xm = ffn(xm, [o_hg_m, o_fox_m], w_out0, 0, TM_MAIN, TM_MOE, True)
    xs = ffn(xs, [o_hg_small, o_fox_small], w_out0, 0, RSM, RSM, False)

    cos_m, sin_m = _rope_tables(N_META + jnp.arange(T, dtype=jnp.int32))
    pos_small = _pad_rows(jnp.concatenate([jnp.tile(P + jnp.arange(Ts, dtype=jnp.int32), Bs),
                                           jnp.arange(N_META, dtype=jnp.int32)]), RSM)
    cos_s, sin_s = _rope_tables(pos_small)
    qm, ckv_m, kpe_m, kpe16_m = _odd_proj(xm, jnp.tile(cos_m, (B, 1)), jnp.tile(sin_m, (B, 1)),
                                          w_odd, gq, gkv, wuq, TM_MAIN)
    qs, ckv_s, kpe_s, kpe16_s = _odd_proj(xs, cos_s, sin_s, w_odd, gq, gkv, wuq, RSM)
    kn_m, vn_m = _kv_expand(ckv_m, w_ukv, 1024)
    kn_s, vn_s = _kv_expand(ckv_s, w_ukv, RSM)
    wuk_t = jnp.transpose(mla_w_uk[0], (1, 0, 2)).astype(BF16)
    wuv = mla_w_uv[0].reshape(MLA_KV_LORA, MLA_HEADS * MLA_V).astype(BF16)

    mla_kw = dict(n_heads=MLA_HEADS, dq=MLA_QPAD, dk=MLA_NOPE, dv=MLA_V)
    meta_past = dict(k=kn_s[ME][None], v=vn_s[ME][None], r=kpe16_s[ME][None], tk=N_META)
    o_m = _flash_tri(qm, kn_m, vn_m, n_seq=B, seq_len=T, tq=TQ, rn=kpe16_m,
                     past=meta_past, mask_mode="chunk", **mla_kw)
    assert P % CHUNK == 0 and Ts <= CHUNK
    o_s = _mla_absorbed(qs, cache_mla_ckv[0], cache_mla_kpe[0], ckv_s, kpe16_s, wuk_t, wuv, Bs, Ts, 1024)
    o_t = _flash(qs[ME], kn_s[ME], vn_s[ME], n_seq=1, seq_len=N_META, tq=N_META, q_off=0, k_off=0,
                 rn=kpe16_s[ME], mask_mode="full", **mla_kw)
    xm = ffn(xm, [o_m], w_out1, 1, TM_MAIN, TM_MOE, True)
    xs = ffn(xs, [_pad_rows(jnp.concatenate([o_s, o_t], axis=0), RSM)], w_out1, 1, RSM, RSM, False)

    def with_meta(main, small, *width):
        meta = jnp.broadcast_to(small[ME][None], (B, N_META) + width)
        return jnp.concatenate([meta, main.reshape((B, T) + width)], axis=1)

    y_prompt = xm.reshape(B, T, D_MODEL)
    y_sample = xs[:RS].reshape(Bs, Ts, D_MODEL)
    hg_p = s_main[None]
    fk_p = with_meta(pm["fk"], ps["fk"], FOX_HEADS, FOX_DH)[None]
    fv_p = with_meta(pm["fv"], ps["fv"], FOX_HEADS, FOX_DH)[None]
    flf_p = with_meta(pm["flf"][:, :FOX_HEADS], ps["flf"][:, :FOX_HEADS], FOX_HEADS)[None]
    ckv_p = with_meta(ckv_m, ckv_s, MLA_KV_LORA)[None]
    kpe_p = with_meta(kpe_m, kpe_s, MLA_ROPE)[None]
    hg_s = s_samp[None]
    fk_s_out = ps["fk"][:RS].reshape(1, Bs, Ts, FOX_HEADS, FOX_DH)
    fv_s_out = ps["fv"][:RS].reshape(1, Bs, Ts, FOX_HEADS, FOX_DH)
    flf_s = ps["flf"][:RS, :FOX_HEADS].reshape(1, Bs, Ts, FOX_HEADS)
    ckv_so = ckv_s[:RS].reshape(1, Bs, Ts, MLA_KV_LORA)
    kpe_so = kpe_s[:RS].reshape(1, Bs, Ts, MLA_ROPE)
    return (y_prompt, y_sample, hg_p, fk_p, fv_p, flf_p, ckv_p, kpe_p,
            hg_s, fk_s_out, fv_s_out, flf_s, ckv_so, kpe_so)
```

```python
import functools

import jax
import jax.numpy as jnp
from jax import lax
from jax.experimental import pallas as pl
from jax.experimental.pallas import tpu as pltpu
from jax.experimental.pallas import tpu_sc as plsc

D_MODEL = 1024
CHUNK = 64
N_META = 16
HG_HEADS = 4
HG_DK = 128
HG_DV = 128
HG_W = HG_HEADS * HG_DK
FOX_HEADS = 4
FOX_DH = 128
FOX_W = FOX_HEADS * FOX_DH
MLA_HEADS = 8
MLA_Q_LORA = 512
MLA_KV_LORA = 256
MLA_NOPE = 128
MLA_ROPE = 64
MLA_V = 128
MLA_QPAD = 256
ROPE_BASE = 10000.0
N_EXPERTS = 16
N_GROUPS = 4
EXPERTS_PER_GROUP = 4
D_EXPERT = 256
DEPTH = 2
ALPHA = (2 * DEPTH) ** 0.25
LN_EPS = 1e-5
RMS_EPS = 1e-6

LANES = 128
HG_SUB = 8
HG_GROUP = 4
NEG = -1e30
LOG2E = 1.4426950408889634
F32 = jnp.float32
BF16 = jnp.bfloat16
VMEM_LIMIT = 56 * 1024 * 1024


def _dot(a, b):
    return jnp.dot(a, b, preferred_element_type=F32)


def _dot_nt(a, b):
    return lax.dot_general(a, b, (((1,), (1,)), ((), ())), preferred_element_type=F32)


def _dot_tn(a, b):
    return lax.dot_general(a, b, (((0,), (0,)), ((), ())), preferred_element_type=F32)


def _split3(x):
    hi = x.astype(BF16)
    r = x - hi.astype(F32)
    mid = r.astype(BF16)
    lo = (r - mid.astype(F32)).astype(BF16)
    return hi, mid, lo


def _cumsum_rows(tri, x):
    hi, mid, lo = _split3(x)
    return _dot(tri, hi) + _dot(tri, mid) + _dot(tri, lo)


def _sigmoid(x):
    return 1.0 / (1.0 + jnp.exp(-x))


def _log_sigmoid(x):
    return jnp.minimum(x, 0.0) - jnp.log(1.0 + jnp.exp(-jnp.abs(x)))


def _layer_norm(x, g, b):
    mu = jnp.mean(x, axis=-1, keepdims=True)
    xc = x - mu
    var = jnp.mean(xc * xc, axis=-1, keepdims=True)
    return xc * lax.rsqrt(var + LN_EPS) * g + b


def _rms_norm(x, g):
    return x * lax.rsqrt(jnp.mean(x * x, axis=-1, keepdims=True) + RMS_EPS) * g


def _params(sem):
    return pltpu.CompilerParams(dimension_semantics=sem, vmem_limit_bytes=VMEM_LIMIT)


def _full_spec(a):
    nd = a.ndim
    return pl.BlockSpec(a.shape, lambda *_: (0,) * nd)


def _row_call(kernel, name, rows, tm, row_ins, full_ins, outs, scratch=()):
    assert rows % tm == 0
    in_specs = [pl.BlockSpec((tm, a.shape[1]), lambda i: (i, 0)) for a in row_ins]
    in_specs += [_full_spec(a) for a in full_ins]
    trail = [c if isinstance(c, tuple) else (c,) for c, _ in outs]
    out_specs = [pl.BlockSpec((tm,) + t, lambda i, n=len(t): (i,) + (0,) * n) for t in trail]
    out_shape = [jax.ShapeDtypeStruct((rows,) + t, dt) for t, (_, dt) in zip(trail, outs)]
    return pl.pallas_call(
        kernel, name=name, grid=(rows // tm,), in_specs=in_specs, out_specs=out_specs,
        out_shape=out_shape, scratch_shapes=list(scratch),
        compiler_params=_params(("parallel",)))(*row_ins, *full_ins)


def _even_proj_kernel(x_ref, w_ref, wf_ref, lbl_ref, fb_ref,
                      hq_ref, lf_ref, hk_ref, hv_ref, hgate_ref,
                      fq_ref, fk_ref, fv_ref, fk16_ref, fv16_ref, flf_ref, *, layer):
    xb = x_ref[...].astype(BF16)

    def blk(j):
        return _dot(xb, w_ref[:, j * HG_W:(j + 1) * HG_W])

    logits = lbl_ref[...]
    e = jnp.exp(logits - jnp.max(logits, axis=0, keepdims=True))
    lb = jnp.sum(e[:layer + 1], axis=0, keepdims=True) / jnp.sum(e, axis=0, keepdims=True)

    hq_ref[...] = blk(0).astype(BF16)
    zf = blk(1)
    lf_ref[...] = jnp.log(lb + (1.0 - lb) * _sigmoid(zf))
    hk_ref[...] = ((1.0 - lb) * _sigmoid(-zf)).astype(BF16)
    hv_ref[...] = blk(2).astype(BF16)
    hgate_ref[...] = _sigmoid(blk(3)).astype(BF16)
    fq_ref[...] = (blk(4) * (FOX_DH ** -0.5 * LOG2E)).astype(BF16)
    fk = blk(5)
    fk16_ref[...] = fk.astype(BF16)
    fv = blk(6)
    fv16_ref[...] = fv.astype(BF16)
    for h in range(FOX_HEADS):
        fk_ref[:, h, :] = fk[:, h * FOX_DH:(h + 1) * FOX_DH]
        fv_ref[:, h, :] = fv[:, h * FOX_DH:(h + 1) * FOX_DH]
    flf_ref[...] = _log_sigmoid(_dot(xb, wf_ref[...]) + fb_ref[...])


def _even_proj(x, w_main, w_f, lb_logits, fb_pad, tm, layer):
    rows = x.shape[0]
    outs = [(HG_W, BF16), (HG_W, F32), (HG_W, BF16), (HG_W, BF16), (HG_W, BF16),
            (FOX_W, BF16), ((FOX_HEADS, FOX_DH), F32), ((FOX_HEADS, FOX_DH), F32), (FOX_W, BF16), (FOX_W, BF16),
            (LANES, F32)]
    return _row_call(functools.partial(_even_proj_kernel, layer=layer), "even_proj", rows, tm,
                     [x], [w_main, w_f, lb_logits, fb_pad], outs)


def _bcast_sub(x, j):
    n, c = x.shape
    x3 = x.reshape(n // HG_SUB, HG_SUB, c)
    return jnp.broadcast_to(x3[:, j:j + 1, :], x3.shape).reshape(n, c)


def _level_ref(b, w):
    n, c = b.shape
    parts = [jnp.broadcast_to(b[m * 2 * w + w - 1:m * 2 * w + w, :], (2 * w, c)) for m in range(n // (2 * w))]
    return parts[0] if len(parts) == 1 else jnp.concatenate(parts, axis=0)


def _hgrn2_kernel(q_ref, lf_ref, k_ref, v_ref, gate_ref, flf_ref, g_ref, e_ref, s0_ref, f0_ref,
                  o_ref, fcum_ref, sout_ref, st_scr, fc_scr, *, n_chunks):
    i = pl.program_id(1)
    C = CHUNK

    @pl.when(i == 0)
    def _():
        for h in range(HG_HEADS):
            st_scr[h] = s0_ref[h].T
        fc_scr[...] = f0_ref[...]

    row = lax.broadcasted_iota(jnp.int32, (C, 1), 0)
    col = lax.broadcasted_iota(jnp.int32, (1, C), 1)
    tri = (col <= row).astype(BF16)
    same = lambda w: (row // w) == (col // w)
    levels = (32, 16, 8)

    for c in range(n_chunks):
        sl = slice(c * C, (c + 1) * C)
        fcum = _cumsum_rows(tri, flf_ref[sl, :]) + fc_scr[...]
        fcum_ref[sl, :] = fcum
        fc_scr[...] = fcum[C - 1:C, :]

        for h0 in range(0, HG_HEADS, HG_GROUP):
            gs = slice(h0 * HG_DK, (h0 + HG_GROUP) * HG_DK)
            b = _cumsum_rows(tri, lf_ref[sl, gs]) * LOG2E
            q = q_ref[sl, gs].astype(F32)
            k = k_ref[sl, gs].astype(F32)
            v = v_ref[sl, gs]
            qb = (q * jnp.exp2(b)).astype(BF16)
            b_last = b[C - 1:C, :]
            kd = (k * jnp.exp2(b_last - b)).astype(BF16)
            e_last = jnp.exp2(b_last)

            pjs = [(jnp.exp2(jnp.where((row % HG_SUB) >= j, b - _bcast_sub(b, j), NEG)) * q
                    * _bcast_sub(k, j)).astype(BF16) for j in range(HG_SUB)]
            lv = []
            for w in levels:
                upper = (row % (2 * w)) >= w
                ew = jnp.exp2(-jnp.abs(b - _level_ref(b, w)))
                lv.append((jnp.where(upper, q * ew, 0.0).astype(BF16), jnp.where(upper, 0.0, k * ew).astype(BF16)))

            for hh in range(HG_GROUP):
                h = h0 + hh
                hs = slice(hh * HG_DK, (hh + 1) * HG_DK)
                ho = slice(h * HG_DK, (h + 1) * HG_DK)
                a = jnp.where(same(HG_SUB), _dot(jnp.concatenate([p[:, hs] for p in pjs], axis=1), e_ref[...]), 0.0)
                for w, (qw, kw) in zip(levels, lv):
                    aw = _dot_nt(qw[:, hs], kw[:, hs])
                    a = a + (aw if 2 * w == C else jnp.where(same(2 * w), aw, 0.0))
                st = st_scr[h]
                vh = v[:, hs]
                o = _dot(a.astype(BF16), vh) + _dot_nt(qb[:, hs], st.astype(BF16))
                st_scr[h] = st * e_last[:, hs] + _dot_tn(vh, kd[:, hs])
                o = _rms_norm(o, g_ref[:, ho])
                o_ref[sl, ho] = (o * gate_ref[sl, ho].astype(F32)).astype(BF16)

    @pl.when(i == pl.num_programs(1) - 1)
    def _():
        for h in range(HG_HEADS):
            sout_ref[h] = st_scr[h].T


def _hgrn2(q, lf, k, v, gate, flf, g, e_mat, s0, f0, n_seq, seq_len, row_off, tb):
    assert seq_len % tb == 0 and tb % CHUNK == 0 and row_off % tb == 0
    nb = seq_len // tb
    off = row_off // tb
    per_seq = s0.shape[0] > 1
    rmap = lambda s, i: (off + s * nb + i, 0)
    omap = lambda s, i: (s * nb + i, 0)
    smap = (lambda s, i: (s, 0, 0, 0)) if per_seq else (lambda s, i: (0, 0, 0, 0))
    fmap = (lambda s, i: (s, 0, 0)) if per_seq else (lambda s, i: (0, 0, 0))
    in_specs = [pl.BlockSpec((tb, HG_W), rmap) for _ in range(5)]
    in_specs += [pl.BlockSpec((tb, LANES), rmap), _full_spec(g), _full_spec(e_mat),
                 pl.BlockSpec((None, HG_HEADS, HG_DK, HG_DV), smap), pl.BlockSpec((None, 1, LANES), fmap)]
    out_specs = [pl.BlockSpec((tb, HG_W), omap), pl.BlockSpec((tb, LANES), omap),
                 pl.BlockSpec((None, HG_HEADS, HG_DK, HG_DV), lambda s, i: (s, 0, 0, 0))]
    out_shape = [jax.ShapeDtypeStruct((n_seq * seq_len, HG_W), BF16),
                 jax.ShapeDtypeStruct((n_seq * seq_len, LANES), F32),
                 jax.ShapeDtypeStruct((n_seq, HG_HEADS, HG_DK, HG_DV), F32)]
    scratch = [pltpu.VMEM((HG_HEADS, HG_DV, HG_DK), F32), pltpu.VMEM((1, LANES), F32)]
    return pl.pallas_call(
        functools.partial(_hgrn2_kernel, n_chunks=tb // CHUNK), name="hgrn2",
        grid=(n_seq, nb), in_specs=in_specs, out_specs=out_specs, out_shape=out_shape,
        scratch_shapes=scratch, compiler_params=_params(("parallel", "arbitrary")))(
            q, lf, k, v, gate, flf, g, e_mat, s0, f0)


def _cumsum_kernel(x_ref, tri_ref, o_ref, carry):
    @pl.when(pl.program_id(0) == 0)
    def _():
        carry[...] = jnp.zeros_like(carry)

    hi, mid, lo = _split3(x_ref[...])
    tri = tri_ref[...]
    out = _dot(hi, tri) + _dot(mid, tri) + _dot(lo, tri) + carry[...]
    o_ref[...] = out
    carry[...] = out[:, out.shape[1] - 1:]


def _cumsum_lanes(x, tb):
    r, seq_len = x.shape
    tri = (jnp.arange(tb)[:, None] <= jnp.arange(tb)[None, :]).astype(BF16)
    return pl.pallas_call(
        _cumsum_kernel, name="cumsum", grid=(seq_len // tb,),
        in_specs=[pl.BlockSpec((r, tb), lambda i: (0, i)), _full_spec(tri)],
        out_specs=pl.BlockSpec((r, tb), lambda i: (0, i)),
        out_shape=jax.ShapeDtypeStruct(x.shape, F32),
        scratch_shapes=[pltpu.VMEM((r, 1), F32)],
        compiler_params=_params(("arbitrary",)))(x, tri)


def _flash_kernel(*refs, n_past_blk, tkp, tq, has_bias, has_rope, mask_mode, has_past, past_heads):
    it = iter(refs)
    q_ref = next(it)
    fq_ref = next(it) if has_bias else None
    if has_past:
        kp_ref, vp_ref = next(it), next(it)
        rp_ref = next(it) if has_rope else None
        fkp_ref = next(it) if has_bias else None
    kn_ref, vn_ref = next(it), next(it)
    rn_ref = next(it) if has_rope else None
    fkn_ref = next(it) if has_bias else None
    o_ref = next(it)
    m_scr, acc_scr = next(it), next(it)
    dv = o_ref.shape[1]

    q = q_ref[...]
    m_scr[...] = jnp.full(m_scr.shape, NEG, F32)
    acc_scr[...] = jnp.zeros(acc_scr.shape, F32)
    fq_b = jnp.broadcast_to(fq_ref[...] * LOG2E, (tq, LANES)) if has_bias else None

    def scores(k, r, fk):
        if has_rope:
            k = jnp.concatenate([k, r], axis=1)
        s = _dot_nt(q, k.astype(BF16))
        if has_bias:
            s = s + jnp.tile(fq_b, (1, s.shape[1] // LANES)) if s.shape[1] % LANES == 0 else s + fq_b[:, :1]
            s = s - fk * LOG2E
        return s

    def update(s, v, mask):
        if mask is not None:
            s = jnp.where(mask, s, NEG)
        m_prev = m_scr[...]
        m_new = jnp.maximum(m_prev, jnp.max(s, axis=1, keepdims=True))
        alpha = jnp.exp2(m_prev - m_new)
        if s.shape[1] % LANES == 0:
            p = jnp.exp2(s - jnp.tile(m_new, (1, s.shape[1] // LANES)))
        else:
            p = jnp.exp2(s - m_new[:, :1])
        v1 = jnp.concatenate([v.astype(BF16), jnp.ones((v.shape[0], LANES), BF16)], axis=1)
        acc_scr[...] = jnp.tile(alpha, (1, acc_scr.shape[1] // LANES)) * acc_scr[...] + _dot(p.astype(BF16), v1)
        m_scr[...] = m_new

    past_kv = (lambda ref, rs: ref[rs, pl.program_id(1), :]) if past_heads else (lambda ref, rs: ref[rs, :])

    def past_block(rs):
        return (past_kv(kp_ref, rs), rp_ref[rs, :] if has_rope else None, fkp_ref[:, rs] if has_bias else None)

    def new_block(rs):
        return (kn_ref[rs, :], rn_ref[rs, :] if has_rope else None, fkn_ref[:, rs] if has_bias else None)

    if has_past:
        if n_past_blk == 1:
            update(scores(*past_block(slice(None))), past_kv(vp_ref, slice(None)), None)
        else:
            def past_body(j, carry):
                rs = pl.ds(pl.multiple_of(j * tkp, tkp), tkp)
                update(scores(*past_block(rs)), past_kv(vp_ref, rs), None)
                return carry
            lax.fori_loop(0, n_past_blk, past_body, 0)

    row = lax.broadcasted_iota(jnp.int32, (tq, 1), 0)
    col = lax.broadcasted_iota(jnp.int32, (1, tq), 1)
    if mask_mode == "causal":
        mask = col <= row
    elif mask_mode == "chunk":
        mask = (col // CHUNK) <= (row // CHUNK)
    else:
        mask = None

    update(scores(*new_block(slice(None))), vn_ref[...], mask)
    acc = acc_scr[...]
    o_ref[...] = (acc[:, :dv] / acc[:, dv:]).astype(o_ref.dtype)


def _flash(q, kn, vn, *, n_seq, n_heads, seq_len, tq, dq, dk, dv, q_off, k_off, mask_mode,
           fq=None, fkn=None, rn=None, past=None):
    assert seq_len % tq == 0 and q_off % tq == 0 and k_off % seq_len == 0
    nq = seq_len // tq
    qo = q_off // tq
    ko = k_off // seq_len
    has_bias = fq is not None
    has_rope = rn is not None
    has_past = past is not None
    ins, specs = [q], [pl.BlockSpec((tq, dq), lambda b, h, i: (qo + b * nq + i, h))]
    if has_bias:
        ins.append(fq)
        specs.append(pl.BlockSpec((None, tq, 1), lambda b, h, i: (h, qo + b * nq + i, 0)))
    n_past_blk, tkp, past_heads = 0, 0, False
    if has_past:
        tp = past["k"].shape[1]
        tkp = past["tk"]
        assert tp % tkp == 0
        n_past_blk = tp // tkp
        pb = (lambda b: b) if past["k"].shape[0] > 1 else (lambda b: 0)
        ins += [past["k"], past["v"]]
        past_heads = past["k"].ndim == 4
        if past_heads:
            specs += [pl.BlockSpec((None, tp, n_heads, dk), lambda b, h, i: (pb(b), 0, 0, 0)),
                      pl.BlockSpec((None, tp, n_heads, dv), lambda b, h, i: (pb(b), 0, 0, 0))]
        else:
            specs += [pl.BlockSpec((None, tp, dk), lambda b, h, i: (pb(b), 0, h)),
                      pl.BlockSpec((None, tp, dv), lambda b, h, i: (pb(b), 0, h))]
        if has_rope:
            ins.append(past["r"])
            specs.append(pl.BlockSpec((None, tp, LANES), lambda b, h, i: (pb(b), 0, 0)))
        if has_bias:
            ins.append(past["fk"])
            specs.append(pl.BlockSpec((None, None, 1, tp), lambda b, h, i: (pb(b), h, 0, 0)))
    ins += [kn, vn]
    specs += [pl.BlockSpec((seq_len, dk), lambda b, h, i: (ko + b, h)),
              pl.BlockSpec((seq_len, dv), lambda b, h, i: (ko + b, h))]
    if has_rope:
        ins.append(rn)
        specs.append(pl.BlockSpec((seq_len, LANES), lambda b, h, i: (ko + b, 0)))
    if has_bias:
        ins.append(fkn)
        specs.append(pl.BlockSpec((None, None, 1, seq_len), lambda b, h, i: (h, ko + b, 0, 0)))
    assert nq == 1
    kern = functools.partial(_flash_kernel, n_past_blk=n_past_blk, tkp=tkp, tq=tq, has_bias=has_bias,
                             has_rope=has_rope, mask_mode=mask_mode, has_past=has_past, past_heads=past_heads)
    return pl.pallas_call(
        kern, name="flash", grid=(n_seq, n_heads, nq), in_specs=specs,
        out_specs=pl.BlockSpec((tq, dv), lambda b, h, i: (b * nq + i, h)),
        out_shape=jax.ShapeDtypeStruct((n_seq * seq_len, n_heads * dv), BF16),
        scratch_shapes=[pltpu.VMEM((tq, LANES), F32), pltpu.VMEM((tq, dv + LANES), F32)],
        compiler_params=_params(("parallel", "parallel", "arbitrary")))(*ins)


FLASH_UNROLL_OFF = 14
FLASH_UNROLL_DIAG = 8


def _tri_tables(nq):
    pairs = [(qi, kj) for qi in range(nq) for kj in range(qi)] + [(qi, qi) for qi in range(nq)] + [(0, 0)]
    return (jnp.array([p[0] for p in pairs], jnp.int32), jnp.array([p[1] for p in pairs], jnp.int32))


def _flash_tri_kernel(qt_ref, kt_ref, *refs, tq, nq, has_bias, has_rope, mask_mode):
    it = iter(refs)
    q_ref = next(it)
    fq_ref = next(it) if has_bias else None
    kp_ref, vp_ref = next(it), next(it)
    rp_ref = next(it) if has_rope else None
    fkp_ref = next(it) if has_bias else None
    kn_ref, vn_ref = next(it), next(it)
    rn_ref = next(it) if has_rope else None
    fkn_ref = next(it) if has_bias else None
    o_ref = next(it)
    m_scr, acc_scr, sa_scr, sb_scr = next(it), next(it), next(it), next(it)
    fqb_scr = next(it) if has_bias else None
    dv = o_ref.shape[1]
    n_off = nq * (nq - 1) // 2
    tile = lambda j: pl.ds(pl.multiple_of(j * tq, tq), tq)
    ones = jnp.ones((tq, LANES), BF16)

    kp = kp_ref[...]
    if has_rope:
        kp = jnp.concatenate([kp, rp_ref[...]], axis=1)
    vp1 = jnp.concatenate([vp_ref[...], ones[:vp_ref.shape[0]]], axis=1)
    for i in range(nq):
        rs = slice(i * tq, (i + 1) * tq)
        s = _dot_nt(q_ref[rs, :], kp)
        if has_bias:
            fb = fq_ref[rs, :] * LOG2E
            fqb_scr[rs, :] = jnp.broadcast_to(fb, (tq, LANES))
            s = s + fb - fkp_ref[...] * LOG2E
        m0 = jnp.max(s, axis=1, keepdims=True)
        m_scr[i] = jnp.broadcast_to(m0, (tq, LANES))
        acc_scr[i] = _dot(jnp.exp2(s - m0).astype(BF16), vp1)

    def fill(s_ref, t):
        qs, ks = tile(qt_ref[t]), tile(kt_ref[t])
        k = kn_ref[ks, :]
        if has_rope:
            k = jnp.concatenate([k, rn_ref[ks, :]], axis=1)
        s = _dot_nt(q_ref[qs, :], k)
        if has_bias:
            s = s + jnp.tile(fqb_scr[qs, :], (1, tq // LANES)) - fkn_ref[:, ks] * LOG2E
        s_ref[...] = s

    def drain(s_ref, t, mask):
        qi = qt_ref[t]
        s = s_ref[...]
        if mask is not None:
            s = jnp.where(mask, s, NEG)
        m_prev = m_scr[qi]
        m_new = jnp.maximum(m_prev, jnp.max(s, axis=1, keepdims=True))
        p = jnp.exp2(s - jnp.tile(m_new, (1, tq // LANES)))
        v1 = jnp.concatenate([vn_ref[tile(kt_ref[t]), :], ones], axis=1)
        acc = jnp.tile(jnp.exp2(m_prev - m_new), (1, (dv + LANES) // LANES)) * acc_scr[qi] + _dot(p.astype(BF16), v1)
        return qi, m_new, acc

    def keep(s_ref, t):
        qi, m_new, acc = drain(s_ref, t, None)
        m_scr[qi] = m_new
        acc_scr[qi] = acc

    row = lax.broadcasted_iota(jnp.int32, (tq, 1), 0)
    col = lax.broadcasted_iota(jnp.int32, (1, tq), 1)
    mask = {"causal": col <= row, "chunk": (col // CHUNK) <= (row // CHUNK)}[mask_mode]

    def finish(s_ref, t):
        qi, _, acc = drain(s_ref, t, mask)
        o_ref[tile(qi), :] = (acc[:, :dv] / acc[:, dv:]).astype(o_ref.dtype)

    def pipeline(t0, n, unroll, consume):
        assert n % unroll == 0 and unroll % 2 == 0

        def body(i, carry):
            t = t0 + unroll * i
            for u in range(0, unroll, 2):
                fill(sb_scr, t + u + 1)
                consume(sa_scr, t + u)
                fill(sa_scr, t + u + 2)
                consume(sb_scr, t + u + 1)
            return carry
        lax.fori_loop(0, n // unroll, body, 0)

    fill(sa_scr, 0)
    pipeline(0, n_off, FLASH_UNROLL_OFF, keep)
    pipeline(n_off, nq, FLASH_UNROLL_DIAG, finish)


def _flash_tri(q, kn, vn, *, n_seq, n_heads, seq_len, tq, dq, dk, dv, mask_mode, past, fq=None, fkn=None, rn=None):
    nq = seq_len // tq
    has_bias = fq is not None
    has_rope = rn is not None
    tp = past["k"].shape[1]
    m3 = lambda f: (lambda b, h, qt, kt: f(b, h))
    ins, specs = [q], [pl.BlockSpec((seq_len, dq), m3(lambda b, h: (b, h)))]
    if has_bias:
        ins.append(fq)
        specs.append(pl.BlockSpec((None, seq_len, 1), m3(lambda b, h: (h, b, 0))))
    ins += [past["k"], past["v"]]
    specs += [pl.BlockSpec((None, tp, dk), m3(lambda b, h: (0, 0, h))),
              pl.BlockSpec((None, tp, dv), m3(lambda b, h: (0, 0, h)))]
    if has_rope:
        ins.append(past["r"])
        specs.append(pl.BlockSpec((None, tp, LANES), m3(lambda b, h: (0, 0, 0))))
    if has_bias:
        ins.append(past["fk"])
        specs.append(pl.BlockSpec((None, None, 1, tp), m3(lambda b, h: (0, h, 0, 0))))
    ins += [kn, vn]
    specs += [pl.BlockSpec((seq_len, dk), m3(lambda b, h: (b, h))),
              pl.BlockSpec((seq_len, dv), m3(lambda b, h: (b, h)))]
    if has_rope:
        ins.append(rn)
        specs.append(pl.BlockSpec((seq_len, LANES), m3(lambda b, h: (b, 0))))
    if has_bias:
        ins.append(fkn)
        specs.append(pl.BlockSpec((None, None, 1, seq_len), m3(lambda b, h: (h, b, 0, 0))))
    scratch = [pltpu.VMEM((nq, tq, LANES), F32), pltpu.VMEM((nq, tq, dv + LANES), F32),
               pltpu.VMEM((tq, tq), F32), pltpu.VMEM((tq, tq), F32)]
    if has_bias:
        scratch.append(pltpu.VMEM((seq_len, LANES), F32))
    grid_spec = pltpu.PrefetchScalarGridSpec(
        num_scalar_prefetch=2, grid=(n_seq, n_heads), in_specs=specs,
        out_specs=pl.BlockSpec((seq_len, dv), m3(lambda b, h: (b, h))), scratch_shapes=scratch)
    kern = functools.partial(_flash_tri_kernel, tq=tq, nq=nq, has_bias=has_bias, has_rope=has_rope,
                             mask_mode=mask_mode)
    return pl.pallas_call(
        kern, name="flash_tri", grid_spec=grid_spec,
        out_shape=jax.ShapeDtypeStruct((n_seq * seq_len, n_heads * dv), BF16),
        compiler_params=_params(("parallel", "arbitrary")))(*_tri_tables(nq), *ins)


def _route(sc, sb):
    def top2_sum(v):
        a, b, c, d = v
        a, b = jnp.maximum(a, b), jnp.minimum(a, b)
        c, d = jnp.maximum(c, d), jnp.minimum(c, d)
        hi, lo2 = jnp.maximum(a, c), jnp.minimum(a, c)
        return hi + jnp.maximum(lo2, jnp.maximum(b, d))

    gs = [top2_sum(sb[g * EXPERTS_PER_GROUP:(g + 1) * EXPERTS_PER_GROUP]) for g in range(N_GROUPS)]
    best_v, best_g = gs[0], jnp.zeros(gs[0].shape, jnp.int32)
    for g in range(1, N_GROUPS):
        upd = gs[g] > best_v
        best_v = jnp.where(upd, gs[g], best_v)
        best_g = jnp.where(upd, g, best_g)
    masked = [jnp.where(best_g == (e // EXPERTS_PER_GROUP), sb[e], -jnp.inf) for e in range(N_EXPERTS)]

    def argmax_first(vals, exclude=None):
        bv = jnp.full(vals[0].shape, -jnp.inf, F32)
        bi = jnp.full(vals[0].shape, -1, jnp.int32)
        for e, v in enumerate(vals):
            upd = v > bv
            if exclude is not None:
                upd = upd & (exclude != e)
            bv = jnp.where(upd, v, bv)
            bi = jnp.where(upd, e, bi)
        return bi

    i1 = argmax_first(masked)
    i2 = argmax_first(masked, exclude=i1)
    w1 = sum(jnp.where(i1 == e, sc[e], 0.0) for e in range(N_EXPERTS))
    w2 = sum(jnp.where(i2 == e, sc[e], 0.0) for e in range(N_EXPERTS))
    tot = w1 + w2
    w1, w2 = w1 / tot, w2 / tot
    comb = [jnp.where(i1 == e, w1, 0.0) + jnp.where(i2 == e, w2, 0.0) for e in range(N_EXPERTS)]
    return comb + [i1.astype(F32), i2.astype(F32), w1, w2]


def _mix_kernel(*refs, n_act):
    x_ref = refs[0]
    a_refs = refs[1:1 + n_act]
    w_ref, g_ref, b_ref, rw_ref, rb_ref, x1_ref, x1p_ref, comb_ref, ct_scr = refs[1 + n_act:]
    half = D_MODEL // 2
    ys = []
    for n0 in (0, half):
        y = None
        k0 = 0
        for a_ref in a_refs:
            kw = a_ref.shape[1]
            part = _dot(a_ref[...], w_ref[k0:k0 + kw, n0:n0 + half])
            y = part if y is None else y + part
            k0 += kw
        ys.append(y)
    x1 = _layer_norm(ALPHA * x_ref[...] + jnp.concatenate(ys, axis=1), g_ref[...], b_ref[...])
    x1_ref[...] = x1

    x1p_ref[...] = _pack_pair(x1[:, :half], x1[:, half:])
    logits = _dot(x1.astype(BF16), rw_ref[...])
    scores_t = _sigmoid(logits).T
    sc = [scores_t[e:e + 1, :] for e in range(N_EXPERTS)]
    sb = [sc[e] + rb_ref[e:e + 1, :] for e in range(N_EXPERTS)]
    route_rows = _route(sc, sb)
    ct_scr[...] = jnp.zeros(ct_scr.shape, F32)
    for r, val in enumerate(route_rows):
        ct_scr[r:r + 1, :] = val
    comb_ref[...] = ct_scr[...].T


def _mix(x, acts, w_out, ln_g, ln_b, rw, rb, tm):
    rows = x.shape[0]
    return _row_call(functools.partial(_mix_kernel, n_act=len(acts)), "mix", rows, tm,
                     [x] + list(acts), [w_out, ln_g, ln_b, rw, rb],
                     [(D_MODEL, F32), (D_MODEL // 2, jnp.uint32), (LANES, F32)],
                     scratch=[pltpu.VMEM((LANES, tm), F32)])


def _moe_kernel(x_ref, comb_ref, wg_ref, wu_ref, wd_ref, g_ref, b_ref, o_ref, xb_scr, acc_scr):
    e = pl.program_id(1)

    @pl.when(e == 0)
    def _():
        xb_scr[...] = x_ref[...].astype(BF16)
        acc_scr[...] = jnp.zeros(acc_scr.shape, F32)

    xb = xb_scr[...]
    lane = lax.broadcasted_iota(jnp.int32, (1, LANES), 1)
    c_e = jnp.sum(jnp.where(lane == e, comb_ref[...], 0.0), axis=1, keepdims=True)
    gate = _dot(xb, wg_ref[...].astype(BF16))
    h = gate * _sigmoid(gate) * _dot(xb, wu_ref[...].astype(BF16))
    acc_scr[...] += _dot((h * c_e).astype(BF16), wd_ref[...].astype(BF16))

    @pl.when(e == N_EXPERTS - 1)
    def _():
        o_ref[...] = _layer_norm(ALPHA * x_ref[...] + acc_scr[...], g_ref[...], b_ref[...])


def _moe(x, comb, wg, wu, wd, layer, ln_g, ln_b, tm):
    rows = x.shape[0]
    assert rows % tm == 0
    return pl.pallas_call(
        _moe_kernel, name="moe", grid=(rows // tm, N_EXPERTS),
        in_specs=[pl.BlockSpec((tm, D_MODEL), lambda i, e: (i, 0)),
                  pl.BlockSpec((tm, LANES), lambda i, e: (i, 0)),
                  pl.BlockSpec((None, None, D_MODEL, D_EXPERT), lambda i, e: (layer, e, 0, 0)),
                  pl.BlockSpec((None, None, D_MODEL, D_EXPERT), lambda i, e: (layer, e, 0, 0)),
                  pl.BlockSpec((None, None, D_EXPERT, D_MODEL), lambda i, e: (layer, e, 0, 0)),
                  _full_spec(ln_g), _full_spec(ln_b)],
        out_specs=pl.BlockSpec((tm, D_MODEL), lambda i, e: (i, 0)),
        out_shape=jax.ShapeDtypeStruct((rows, D_MODEL), F32),
        scratch_shapes=[pltpu.VMEM((tm, D_MODEL), BF16), pltpu.VMEM((tm, D_MODEL), F32)],
        compiler_params=_params(("parallel", "arbitrary")))(x, comb, wg, wu, wd, ln_g, ln_b)


ROUTE_E1, ROUTE_E2, ROUTE_W1, ROUTE_W2 = N_EXPERTS, N_EXPERTS + 1, N_EXPERTS + 2, N_EXPERTS + 3
TE = 512
SC_WINDOW = 128
RANK_TILE = 1024


def _pack_pair(a, b):
    au = lax.bitcast_convert_type(a.astype(BF16).astype(F32), jnp.uint32)
    bu = lax.bitcast_convert_type(b.astype(BF16).astype(F32), jnp.uint32)
    return (au >> 16) | (bu & jnp.uint32(0xFFFF0000))


def _unpack_pair(w):
    a = lax.bitcast_convert_type(w << 16, F32)
    b = lax.bitcast_convert_type(w & jnp.uint32(0xFFFF0000), F32)
    return a, b


def _rank_kernel(route_ref, pos_ref, texp_ref, nused_ref, cnt_scr, carry_scr, seg_scr, before_scr):
    ph, i = pl.program_id(0), pl.program_id(1)
    T = route_ref.shape[0]
    lane = lax.broadcasted_iota(jnp.int32, (1, LANES), 1)
    lane_f = lane.astype(F32)
    r = route_ref[...]
    e1, e2 = r[:, ROUTE_E1:ROUTE_E1 + 1], r[:, ROUTE_E2:ROUTE_E2 + 1]
    m1, m2 = lane_f == e1, lane_f == e2
    m = jnp.where(m1 | m2, 1.0, 0.0)
    colsum = jnp.sum(m, axis=0, keepdims=True)

    @pl.when((ph == 0) & (i == 0))
    def _():
        cnt_scr[...] = jnp.zeros(cnt_scr.shape, F32)

    @pl.when(ph == 0)
    def _():
        cnt_scr[...] += colsum

    @pl.when((ph == 1) & (i == 0))
    def _():
        cnt = cnt_scr[...].astype(jnp.int32)
        padded = (((cnt + (TE - 1)) // TE) * TE).astype(F32)
        rr = lax.broadcasted_iota(jnp.int32, (LANES, 1), 0)
        upper = (rr < lane).astype(BF16)
        hi, mid, lo = _split3(jnp.broadcast_to(padded, (HG_SUB, LANES)))
        seg = (_dot(hi, upper) + _dot(mid, upper) + _dot(lo, upper))[:1, :]
        seg_scr[...] = seg
        carry_scr[...] = jnp.zeros(carry_scr.shape, F32)
        seg_end = seg + padded
        tile_row = lax.broadcasted_iota(jnp.int32, texp_ref.shape, 1).astype(F32) * float(TE)
        te_acc = jnp.zeros(texp_ref.shape, jnp.int32)
        for e in range(N_EXPERTS):
            te_acc = te_acc + jnp.where(seg_end[:, e:e + 1] <= tile_row, 1, 0)
        texp_ref[...] = jnp.minimum(te_acc, N_EXPERTS - 1)
        nused_ref[...] = jnp.broadcast_to(seg_end[:, N_EXPERTS - 1:N_EXPERTS] / float(TE), nused_ref.shape).astype(jnp.int32)

    @pl.when((ph == 1) & (i == 0))
    def _():
        row = lax.broadcasted_iota(jnp.int32, (T, 1), 0)
        col = lax.broadcasted_iota(jnp.int32, (1, T), 1)
        before_scr[...] = (col < row).astype(BF16)

    @pl.when(ph == 1)
    def _():
        cum = _dot(before_scr[...], m.astype(BF16)) + carry_scr[...] + seg_scr[...]
        p1 = jnp.sum(jnp.where(m1, cum, 0.0), axis=1, keepdims=True)
        p2 = jnp.sum(jnp.where(m2, cum, 0.0), axis=1, keepdims=True)
        pos_ref[...] = jnp.where(lane == 0, p1, jnp.where(lane == 1, p2, 0.0)).astype(jnp.int32)
        carry_scr[...] += colsum


def _rank(route, n_tiles, tm):
    rows = route.shape[0]
    nb = rows // tm
    nt_pad = -(-n_tiles // LANES) * LANES
    return pl.pallas_call(
        _rank_kernel, name="rank", grid=(2, nb),
        in_specs=[pl.BlockSpec((tm, LANES), lambda ph, i: (i, 0))],
        out_specs=[pl.BlockSpec((tm, LANES), lambda ph, i: (i * ph, 0)),
                   pl.BlockSpec((1, nt_pad), lambda ph, i: (0, 0)),
                   pl.BlockSpec((1, LANES), lambda ph, i: (0, 0))],
        out_shape=[jax.ShapeDtypeStruct((rows, LANES), jnp.int32),
                   jax.ShapeDtypeStruct((1, nt_pad), jnp.int32),
                   jax.ShapeDtypeStruct((1, LANES), jnp.int32)],
        scratch_shapes=[pltpu.VMEM((1, LANES), F32), pltpu.VMEM((1, LANES), F32), pltpu.VMEM((1, LANES), F32),
                        pltpu.VMEM((tm, tm), BF16)],
        compiler_params=_params(("arbitrary", "arbitrary")))(route)


def _sc_mesh():
    return plsc.VectorSubcoreMesh(core_axis_name="c", subcore_axis_name="s")


def _sc_scatter_rows(x, idx, n_out):
    rows, d = x.shape
    mesh = _sc_mesh()
    n_workers = mesh.num_cores * mesh.num_subcores
    steps = idx.shape[1] // SC_WINDOW // n_workers
    assert steps * SC_WINDOW * n_workers == idx.shape[1] and rows % SC_WINDOW == 0

    @functools.partial(pl.kernel, out_type=jax.ShapeDtypeStruct((n_out, d), x.dtype), mesh=mesh,
                       scratch_types=[pltpu.VMEM((1, SC_WINDOW), jnp.int32), pltpu.VMEM((SC_WINDOW, d), x.dtype)])
    def scatter(x_hbm, i_hbm, o_hbm, i_vmem, buf):
        first = (lax.axis_index("c") * mesh.num_subcores + lax.axis_index("s")) * steps

        @pl.loop(0, steps)
        def _(t):
            off = (first + t) * SC_WINDOW
            pltpu.sync_copy(i_hbm.at[:, pl.ds(off, SC_WINDOW)], i_vmem)
            pltpu.sync_copy(x_hbm.at[pl.ds(off % rows, SC_WINDOW)], buf)
            pltpu.sync_copy(buf, o_hbm.at[i_vmem.at[0]])

    return scatter(x, idx)


def _sc_gather_rows(x, idx):
    d = x.shape[1]
    n = idx.shape[1]
    mesh = _sc_mesh()
    n_workers = mesh.num_cores * mesh.num_subcores
    steps = n // SC_WINDOW // n_workers
    assert steps * SC_WINDOW * n_workers == n

    @functools.partial(pl.kernel, out_type=jax.ShapeDtypeStruct((n, d), x.dtype), mesh=mesh,
                       scratch_types=[pltpu.VMEM((1, SC_WINDOW), jnp.int32), pltpu.VMEM((SC_WINDOW, d), x.dtype)])
    def gather(x_hbm, i_hbm, o_hbm, i_vmem, buf):
        first = (lax.axis_index("c") * mesh.num_subcores + lax.axis_index("s")) * steps

        @pl.loop(0, steps)
        def _(t):
            off = (first + t) * SC_WINDOW
            pltpu.sync_copy(i_hbm.at[:, pl.ds(off, SC_WINDOW)], i_vmem)
            pltpu.sync_copy(x_hbm.at[i_vmem.at[0]], buf)
            pltpu.sync_copy(buf, o_hbm.at[pl.ds(off, SC_WINDOW)])

    return gather(x, idx)


def _gmm_kernel(texp_ref, nused_ref, x_ref, wg_ref, wu_ref, wd_ref, o_ref):
    @pl.when(pl.program_id(0) < nused_ref[0])
    def _():
        a, b = _unpack_pair(x_ref[...])
        xb = jnp.concatenate([a.astype(BF16), b.astype(BF16)], axis=1)
        gate = _dot(xb, wg_ref[...].astype(BF16))
        h = gate * _sigmoid(gate) * _dot(xb, wu_ref[...].astype(BF16))
        y = _dot(h.astype(BF16), wd_ref[...].astype(BF16))
        o_ref[...] = _pack_pair(y[:, :D_MODEL // 2], y[:, D_MODEL // 2:])


def _gmm(xs, texp, nused, wg, wu, wd, layer):
    rows = xs.shape[0]
    wmap = lambda d, te, nu: (layer, te[d], 0, 0)
    grid_spec = pltpu.PrefetchScalarGridSpec(
        num_scalar_prefetch=2, grid=(rows // TE,),
        in_specs=[pl.BlockSpec((TE, D_MODEL // 2), lambda d, te, nu: (d, 0)),
                  pl.BlockSpec((None, None, D_MODEL, D_EXPERT), wmap),
                  pl.BlockSpec((None, None, D_MODEL, D_EXPERT), wmap),
                  pl.BlockSpec((None, None, D_EXPERT, D_MODEL), wmap)],
        out_specs=pl.BlockSpec((TE, D_MODEL // 2), lambda d, te, nu: (d, 0)))
    return pl.pallas_call(
        _gmm_kernel, name="gmm", grid_spec=grid_spec,
        out_shape=jax.ShapeDtypeStruct((rows, D_MODEL // 2), jnp.uint32),
        compiler_params=_params(("arbitrary",)))(texp, nused, xs, wg, wu, wd)


def _combine_kernel(x_ref, g0_ref, g1_ref, route_ref, g_ref, b_ref, o_ref):
    r = route_ref[...]
    y0 = jnp.concatenate(_unpack_pair(g0_ref[...]), axis=1)
    y1 = jnp.concatenate(_unpack_pair(g1_ref[...]), axis=1)
    f = y0 * r[:, ROUTE_W1:ROUTE_W1 + 1] + y1 * r[:, ROUTE_W2:ROUTE_W2 + 1]
    o_ref[...] = _layer_norm(ALPHA * x_ref[...] + f, g_ref[...], b_ref[...])


def _combine(x, g, route, ln_g, ln_b, tm):
    rows = x.shape[0]
    nb = rows // tm
    return pl.pallas_call(
        _combine_kernel, name="combine", grid=(nb,),
        in_specs=[pl.BlockSpec((tm, D_MODEL), lambda i: (i, 0)),
                  pl.BlockSpec((tm, D_MODEL // 2), lambda i: (i, 0)),
                  pl.BlockSpec((tm, D_MODEL // 2), lambda i: (nb + i, 0)),
                  pl.BlockSpec((tm, LANES), lambda i: (i, 0)), _full_spec(ln_g), _full_spec(ln_b)],
        out_specs=pl.BlockSpec((tm, D_MODEL), lambda i: (i, 0)),
        out_shape=jax.ShapeDtypeStruct((rows, D_MODEL), F32),
        compiler_params=_params(("parallel",)))(x, g, g, route, ln_g, ln_b)


def _moe_routed(x1, x1b, route, wg, wu, wd, layer, ln_g, ln_b, tm):
    rows = x1.shape[0]
    n_rows = 2 * rows + N_EXPERTS * TE
    pos, texp, nused = _rank(route, n_rows // TE, RANK_TILE)
    idx = jnp.concatenate([pos[:, 0], pos[:, 1]])[None, :]
    xs = _sc_scatter_rows(x1b, idx, n_rows)
    ys = _gmm(xs, texp[0, :n_rows // TE], nused[0, :1], wg, wu, wd, layer)
    g = _sc_gather_rows(ys, idx)
    return _combine(x1, g, route, ln_g, ln_b, tm)


def _rope128(x, cos_t, sin_t):
    lane = lax.broadcasted_iota(jnp.int32, (1, LANES), 1)
    half = MLA_ROPE // 2
    swapped = jnp.where(lane < half, pltpu.roll(x, LANES - half, axis=1), pltpu.roll(x, half, axis=1))
    return x * cos_t + swapped * sin_t


def _odd_proj_kernel(x_ref, cos_ref, sin_ref, w_ref, gq_ref, gkv_ref, wuq_ref,
                     q_ref, ckv_ref, kpe_ref, kpe16_ref):
    z = _dot(x_ref[...].astype(BF16), w_ref[...])
    cq = _rms_norm(z[:, :MLA_Q_LORA], gq_ref[...])
    ckv_ref[...] = _rms_norm(z[:, MLA_Q_LORA:MLA_Q_LORA + MLA_KV_LORA], gkv_ref[...])
    cos_t, sin_t = cos_ref[...], sin_ref[...]
    kpe = _rope128(z[:, MLA_Q_LORA + MLA_KV_LORA:], cos_t, sin_t)
    kpe_ref[...] = kpe[:, :MLA_ROPE]
    kpe16_ref[...] = kpe.astype(BF16)
    qf = _dot(cq.astype(BF16), wuq_ref[...])
    scale = (MLA_NOPE + MLA_ROPE) ** -0.5 * LOG2E
    for h in range(MLA_HEADS):
        c0 = h * MLA_QPAD
        q_ref[:, c0:c0 + MLA_NOPE] = (qf[:, c0:c0 + MLA_NOPE] * scale).astype(BF16)
        qr = _rope128(qf[:, c0 + MLA_NOPE:c0 + MLA_QPAD], cos_t, sin_t)
        q_ref[:, c0 + MLA_NOPE:c0 + MLA_QPAD] = (qr * scale).astype(BF16)


def _odd_proj(x, cos_t, sin_t, w_in, gq, gkv, wuq, tm):
    rows = x.shape[0]
    outs = [(MLA_HEADS * MLA_QPAD, BF16), (MLA_KV_LORA, F32), (MLA_ROPE, F32), (LANES, BF16)]
    return _row_call(_odd_proj_kernel, "odd_proj", rows, tm, [x, cos_t, sin_t], [w_in, gq, gkv, wuq], outs)


def _kv_expand_kernel(c_ref, w_ref, k_ref, v_ref):
    kv = _dot(c_ref[...].astype(BF16), w_ref[...])
    n = MLA_HEADS * MLA_NOPE
    k_ref[...] = kv[:, :n].astype(BF16)
    v_ref[...] = kv[:, n:].astype(BF16)


def _kv_expand(ckv, w_ukv, tm):
    rows = ckv.shape[0]
    return _row_call(_kv_expand_kernel, "kv_expand", rows, tm, [ckv], [w_ukv],
                     [(MLA_HEADS * MLA_NOPE, BF16), (MLA_HEADS * MLA_V, BF16)])


def _mla_absorbed_kernel(q_ref, cp_ref, rp_ref, cn_ref, rn_ref, wuk_ref, wuv_ref, o_ref, m_scr, acc_scr, *, tkp):
    tq = q_ref.shape[0]
    rows = MLA_HEADS * tq
    q = q_ref[...]
    qa = []
    for h in range(MLA_HEADS):
        c0 = h * MLA_QPAD
        q_abs = _dot_nt(q[:, c0:c0 + MLA_NOPE], wuk_ref[h])
        qa.append(jnp.concatenate([q_abs.astype(BF16), q[:, c0 + MLA_NOPE:c0 + MLA_NOPE + MLA_ROPE]], axis=1))
    qs = jnp.concatenate(qa, axis=0)
    m_scr[...] = jnp.full(m_scr.shape, NEG, F32)
    acc_scr[...] = jnp.zeros(acc_scr.shape, F32)

    def update(c, r):
        c = c.astype(BF16)
        s = _dot_nt(qs, jnp.concatenate([c, r.astype(BF16)], axis=1))
        m_prev = m_scr[...]
        m_new = jnp.maximum(m_prev, jnp.max(s, axis=1, keepdims=True))
        if s.shape[1] % LANES == 0:
            p = jnp.exp2(s - jnp.tile(m_new, (1, s.shape[1] // LANES)))
        else:
            p = jnp.exp2(s - m_new[:, :1])
        c1 = jnp.concatenate([c, jnp.ones((c.shape[0], LANES), BF16)], axis=1)
        acc_scr[...] = (jnp.tile(jnp.exp2(m_prev - m_new), (1, acc_scr.shape[1] // LANES)) * acc_scr[...]
                        + _dot(p.astype(BF16), c1))
        m_scr[...] = m_new

    def past_body(j, carry):
        rs = pl.ds(pl.multiple_of(j * tkp, tkp), tkp)
        update(cp_ref[rs, :], rp_ref[rs, :])
        return carry
    lax.fori_loop(0, cp_ref.shape[0] // tkp, past_body, 0)
    update(cn_ref[...], rn_ref[:, :MLA_ROPE])

    acc = acc_scr[...]
    lat = (acc[:, :MLA_KV_LORA] / jnp.tile(acc[:, MLA_KV_LORA:], (1, MLA_KV_LORA // LANES))).astype(BF16)
    for h in range(MLA_HEADS):
        o_ref[:, h * MLA_V:(h + 1) * MLA_V] = _dot(lat[h * tq:(h + 1) * tq, :],
                                                   wuv_ref[:, h * MLA_V:(h + 1) * MLA_V]).astype(o_ref.dtype)


def _mla_absorbed(q, ckv_past, kpe_past, ckv_new, kpe_new, wuk_t, wuv, n_seq, tq, tkp):
    p = ckv_past.shape[1]
    assert p % tkp == 0
    rows = MLA_HEADS * tq
    return pl.pallas_call(
        functools.partial(_mla_absorbed_kernel, tkp=tkp), name="mla_absorbed", grid=(n_seq,),
        in_specs=[pl.BlockSpec((tq, MLA_HEADS * MLA_QPAD), lambda b: (b, 0)),
                  pl.BlockSpec((None, p, MLA_KV_LORA), lambda b: (b, 0, 0)),
                  pl.BlockSpec((None, p, MLA_ROPE), lambda b: (b, 0, 0)),
                  pl.BlockSpec((tq, MLA_KV_LORA), lambda b: (b, 0)),
                  pl.BlockSpec((tq, LANES), lambda b: (b, 0)),
                  _full_spec(wuk_t), _full_spec(wuv)],
        out_specs=pl.BlockSpec((tq, MLA_HEADS * MLA_V), lambda b: (b, 0)),
        out_shape=jax.ShapeDtypeStruct((n_seq * tq, MLA_HEADS * MLA_V), BF16),
        scratch_shapes=[pltpu.VMEM((rows, LANES), F32), pltpu.VMEM((rows, MLA_KV_LORA + LANES), F32)],
        compiler_params=_params(("parallel",)))(q, ckv_past, kpe_past, ckv_new, kpe_new, wuk_t, wuv)


def _rope_tables(pos):
    half = MLA_ROPE // 2
    inv = ROPE_BASE ** (-jnp.arange(half, dtype=F32) / half)
    ang = pos.astype(F32)[:, None] * inv[None, :]
    cos, sin = jnp.cos(ang), jnp.sin(ang)
    z = jnp.zeros((pos.shape[0], LANES - MLA_ROPE), F32)
    return jnp.concatenate([cos, cos, z], axis=1), jnp.concatenate([-sin, sin, z], axis=1)


def _pad_rows(a, n):
    return jnp.pad(a, ((0, n - a.shape[0]),) + ((0, 0),) * (a.ndim - 1))


def kernel(x_prompt, x_sample, state_hgrn2, cache_fox_k, cache_fox_v, cache_fox_logf, cache_mla_ckv, cache_mla_kpe, meta_tokens, even_w_in, hg_lb_logits, hg_norm_g, fox_forget_bias, even_w_out, mla_w_in, mla_q_norm_g, mla_kv_norm_g, mla_w_uq, mla_w_uk, mla_w_uv, mla_w_out, ln_mix_g, ln_mix_b, ln_ffn_g, ln_ffn_b, router_w, router_bias, moe_w_gate, moe_w_up, moe_w_down):
    B, T, _ = x_prompt.shape
    Bs, Ts, _ = x_sample.shape
    P = cache_fox_k.shape[2]
    RM = B * T
    RS = Bs * Ts
    RSM = -(-(RS + N_META) // LANES) * LANES
    ME = slice(RS, RS + N_META)
    TM_MAIN, TM_MOE, TQ = 512, 1024, 512

    xm = x_prompt.reshape(RM, D_MODEL)
    xs = _pad_rows(jnp.concatenate([x_sample.reshape(RS, D_MODEL), meta_tokens.astype(F32)], axis=0), RSM)

    w_in0 = even_w_in[0]
    n_main = 7 * HG_W
    w_even = w_in0[:, :n_main].astype(BF16)
    w_even_f = jnp.pad(w_in0[:, n_main:], ((0, 0), (0, LANES - FOX_HEADS))).astype(BF16)
    fb_pad = jnp.pad(fox_forget_bias[0][None, :], ((0, 0), (0, LANES - FOX_HEADS)))
    g_hg = hg_norm_g[0].reshape(1, HG_W)
    w_out0 = even_w_out[0].astype(BF16)
    e_mat = ((jnp.arange(HG_SUB * HG_DK)[:, None] // HG_DK) == (jnp.arange(CHUNK)[None, :] % HG_SUB)).astype(BF16)

    w_odd = jnp.pad(mla_w_in[0], ((0, 0), (0, LANES - MLA_ROPE))).astype(BF16)
    gq = mla_q_norm_g[0][None, :]
    gkv = mla_kv_norm_g[0][None, :]
    wuq = mla_w_uq[0].reshape(MLA_Q_LORA, MLA_HEADS, MLA_NOPE + MLA_ROPE)
    wuq = jnp.pad(wuq, ((0, 0), (0, 0), (0, MLA_QPAD - MLA_NOPE - MLA_ROPE)))
    wuq = wuq.reshape(MLA_Q_LORA, MLA_HEADS * MLA_QPAD).astype(BF16)
    w_ukv = jnp.concatenate([mla_w_uk[0].reshape(MLA_KV_LORA, -1), mla_w_uv[0].reshape(MLA_KV_LORA, -1)],
                            axis=1).astype(BF16)
    w_out1 = mla_w_out[0].astype(BF16)

    rw = jnp.pad(router_w, ((0, 0), (0, LANES - N_EXPERTS))).astype(BF16)
    rb = jnp.pad(router_bias.astype(F32)[:, None], ((0, LANES - N_EXPERTS), (0, 0)))
    experts = (moe_w_gate, moe_w_up, moe_w_down)
    row2 = lambda a: a[None, :]

    def ffn(x, acts, w_out, l, tm_mix, tm_moe, routed):
        x1, x1b, route = _mix(x, acts, w_out, row2(ln_mix_g[l]), row2(ln_mix_b[l]), rw, rb, tm_mix)
        ln = (row2(ln_ffn_g[l]), row2(ln_ffn_b[l]))
        if routed:
            return _moe_routed(x1, x1b, route, *experts, l, *ln, tm_mix)
        return _moe(x1, route, *experts, l, *ln, tm_moe)

    pm = _even_proj(xm, w_even, w_even_f, hg_lb_logits, fb_pad, TM_MAIN, 0)
    ps = _even_proj(xs, w_even, w_even_f, hg_lb_logits, fb_pad, RSM, 0)
    names = ("hq", "lf", "hk", "hv", "hgate", "fq", "fk", "fv", "fk16", "fv16", "flf")
    pm = dict(zip(names, pm))
    ps = dict(zip(names, ps))

    hg_keys = ("hq", "lf", "hk", "hv", "hgate", "flf")
    meta_in = [_pad_rows(ps[n][ME], CHUNK) for n in hg_keys]
    zero_s = jnp.zeros((1, HG_HEADS, HG_DK, HG_DV), F32)
    zero_f = jnp.zeros((1, 1, LANES), F32)
    o_hg_meta, fc_meta, s_meta = _hgrn2(*meta_in, g_hg, e_mat, zero_s, zero_f, 1, CHUNK, 0, CHUNK)
    o_hg_meta, fc_meta = o_hg_meta[:N_META], fc_meta[:N_META]
    f_meta_end = fc_meta[N_META - 1:N_META][None]

    o_hg_m, fc_m, s_main = _hgrn2(*[pm[n] for n in hg_keys], g_hg, e_mat, s_meta, f_meta_end, B, T, 0, 256)

    logf_c = jnp.pad(jnp.transpose(cache_fox_logf[0], (0, 2, 1)), ((0, 0), (0, HG_SUB - FOX_HEADS), (0, 0)))
    fpast = _cumsum_lanes(logf_c.reshape(Bs * HG_SUB, P), 512).reshape(Bs, HG_SUB, P)[:, :FOX_HEADS, :]
    f0_s = jnp.pad(fpast[:, :, P - 1][:, None, :], ((0, 0), (0, 0), (0, LANES - FOX_HEADS)))
    o_hg_s, fc_s, s_samp = _hgrn2(*[ps[n] for n in hg_keys], g_hg, e_mat, state_hgrn2[0], f0_s, Bs, Ts, 0, CHUNK)

    def bias_layouts(fc, n_seq, seq_len):
        f4 = fc[:, :FOX_HEADS].T
        return f4[:, :, None], f4.reshape(FOX_HEADS, n_seq, 1, seq_len)

    fq_m, fk_m = bias_layouts(fc_m, B, T)
    fq_s, fk_s = bias_layouts(fc_s, Bs, Ts)
    fq_t, fk_t = bias_layouts(fc_meta, 1, N_META)

    fox_kw = dict(n_heads=FOX_HEADS, dq=FOX_DH, dk=FOX_DH, dv=FOX_DH, mask_mode="causal")
    meta_past = dict(k=ps["fk16"][ME][None], v=ps["fv16"][ME][None],
                     fk=jnp.transpose(fk_t, (1, 0, 2, 3)), tk=N_META)
    o_fox_m = _flash_tri(pm["fq"], pm["fk16"], pm["fv16"], n_seq=B, seq_len=T, tq=TQ,
                         fq=fq_m, fkn=fk_m, past=meta_past, **fox_kw)
    samp_past = dict(k=cache_fox_k[0], v=cache_fox_v[0],
                     fk=fpast[:, :, None, :], tk=1024)
    o_fox_s = _flash(ps["fq"], ps["fk16"], ps["fv16"], n_seq=Bs, seq_len=Ts, tq=Ts, q_off=0, k_off=0,
                     fq=fq_s, fkn=fk_s, past=samp_past, **fox_kw)
    o_fox_t = _flash(ps["fq"][ME], ps["fk16"][ME], ps["fv16"][ME], n_seq=1, seq_len=N_META, tq=N_META,
                     q_off=0, k_off=0, fq=fq_t, fkn=fk_t, **fox_kw)

    o_hg_small = _pad_rows(jnp.concatenate([o_hg_s, o_hg_meta], axis=0), RSM)
    o_fox_small = _pad_rows(jnp.concatenate([o_fox_s, o_fox_t], axis=0), RSM)
    xm = ffn(xm, [o_hg_m, o_fox_m], w_out0, 0, TM_MAIN, TM_MOE, True)
    xs = ffn(xs, [o_hg_small, o_fox_small], w_out0, 0, RSM, RSM, False)

    cos_m, sin_m = _rope_tables(N_META + jnp.arange(T, dtype=jnp.int32))
    pos_small = _pad_rows(jnp.concatenate([jnp.tile(P + jnp.arange(Ts, dtype=jnp.int32), Bs),
                                           jnp.arange(N_META, dtype=jnp.int32)]), RSM)
    cos_s, sin_s = _rope_tables(pos_small)
    qm, ckv_m, kpe_m, kpe16_m = _odd_proj(xm, jnp.tile(cos_m, (B, 1)), jnp.tile(sin_m, (B, 1)),
                                          w_odd, gq, gkv, wuq, TM_MAIN)
    qs, ckv_s, kpe_s, kpe16_s = _odd_proj(xs, cos_s, sin_s, w_odd, gq, gkv, wuq, RSM)
    kn_m, vn_m = _kv_expand(ckv_m, w_ukv, 1024)
    kn_s, vn_s = _kv_expand(ckv_s, w_ukv, RSM)
    wuk_t = jnp.transpose(mla_w_uk[0], (1, 0, 2)).astype(BF16)
    wuv = mla_w_uv[0].reshape(MLA_KV_LORA, MLA_HEADS * MLA_V).astype(BF16)

    mla_kw = dict(n_heads=MLA_HEADS, dq=MLA_QPAD, dk=MLA_NOPE, dv=MLA_V)
    meta_past = dict(k=kn_s[ME][None], v=vn_s[ME][None], r=kpe16_s[ME][None], tk=N_META)
    o_m = _flash_tri(qm, kn_m, vn_m, n_seq=B, seq_len=T, tq=TQ, rn=kpe16_m,
                     past=meta_past, mask_mode="chunk", **mla_kw)
    assert P % CHUNK == 0 and Ts <= CHUNK
    o_s = _mla_absorbed(qs, cache_mla_ckv[0], cache_mla_kpe[0], ckv_s, kpe16_s, wuk_t, wuv, Bs, Ts, 1024)
    o_t = _flash(qs[ME], kn_s[ME], vn_s[ME], n_seq=1, seq_len=N_META, tq=N_META, q_off=0, k_off=0,
                 rn=kpe16_s[ME], mask_mode="full", **mla_kw)
    xm = ffn(xm, [o_m], w_out1, 1, TM_MAIN, TM_MOE, True)
    xs = ffn(xs, [_pad_rows(jnp.concatenate([o_s, o_t], axis=0), RSM)], w_out1, 1, RSM, RSM, False)

    def with_meta(main, small, *width):
        meta = jnp.broadcast_to(small[ME][None], (B, N_META) + width)
        return jnp.concatenate([meta, main.reshape((B, T) + width)], axis=1)

    y_prompt = xm.reshape(B, T, D_MODEL)
    y_sample = xs[:RS].reshape(Bs, Ts, D_MODEL)
    hg_p = s_main[None]
    fk_p = with_meta(pm["fk"], ps["fk"], FOX_HEADS, FOX_DH)[None]
    fv_p = with_meta(pm["fv"], ps["fv"], FOX_HEADS, FOX_DH)[None]
    flf_p = with_meta(pm["flf"][:, :FOX_HEADS], ps["flf"][:, :FOX_HEADS], FOX_HEADS)[None]
    ckv_p = with_meta(ckv_m, ckv_s, MLA_KV_LORA)[None]
    kpe_p = with_meta(kpe_m, kpe_s, MLA_ROPE)[None]
    hg_s = s_samp[None]
    fk_s_out = ps["fk"][:RS].reshape(1, Bs, Ts, FOX_HEADS, FOX_DH)
    fv_s_out = ps["fv"][:RS].reshape(1, Bs, Ts, FOX_HEADS, FOX_DH)
    flf_s = ps["flf"][:RS, :FOX_HEADS].reshape(1, Bs, Ts, FOX_HEADS)
    ckv_so = ckv_s[:RS].reshape(1, Bs, Ts, MLA_KV_LORA)
    kpe_so = kpe_s[:RS].reshape(1, Bs, Ts, MLA_ROPE)
    return (y_prompt, y_sample, hg_p, fk_p, fv_p, flf_p, ckv_p, kpe_p,
            hg_s, fk_s_out, fv_s_out, flf_s, ckv_so, kpe_so)
```

```python
import functools

import jax
import jax.numpy as jnp
from jax import lax
from jax.experimental import pallas as pl
from jax.experimental.pallas import tpu as pltpu
from jax.experimental.pallas import tpu_sc as plsc

D_MODEL = 1024
CHUNK = 64
N_META = 16
HG_HEADS = 4
HG_DK = 128
HG_DV = 128
HG_W = HG_HEADS * HG_DK
FOX_HEADS = 4
FOX_DH = 128
FOX_W = FOX_HEADS * FOX_DH
MLA_HEADS = 8
MLA_Q_LORA = 512
MLA_KV_LORA = 256
MLA_NOPE = 128
MLA_ROPE = 64
MLA_V = 128
MLA_QPAD = 256
ROPE_BASE = 10000.0
N_EXPERTS = 16
N_GROUPS = 4
EXPERTS_PER_GROUP = 4
D_EXPERT = 256
DEPTH = 2
ALPHA = (2 * DEPTH) ** 0.25
LN_EPS = 1e-5
RMS_EPS = 1e-6

LANES = 128
HG_SUB = 8
HG_GROUP = 4
NEG = -1e30
LOG2E = 1.4426950408889634
F32 = jnp.float32
BF16 = jnp.bfloat16
VMEM_LIMIT = 56 * 1024 * 1024


def _dot(a, b):
    return jnp.dot(a, b, preferred_element_type=F32)


def _dot_nt(a, b):
    return lax.dot_general(a, b, (((1,), (1,)), ((), ())), preferred_element_type=F32)


def _dot_tn(a, b):
    return lax.dot_general(a, b, (((0,), (0,)), ((), ())), preferred_element_type=F32)


def _split3(x):
    hi = x.astype(BF16)
    r = x - hi.astype(F32)
    mid = r.astype(BF16)
    lo = (r - mid.astype(F32)).astype(BF16)
    return hi, mid, lo


def _cumsum_rows(tri, x):
    hi, mid, lo = _split3(x)
    return _dot(tri, hi) + _dot(tri, mid) + _dot(tri, lo)


def _sigmoid(x):
    return 1.0 / (1.0 + jnp.exp(-x))


def _log_sigmoid(x):
    return jnp.minimum(x, 0.0) - jnp.log(1.0 + jnp.exp(-jnp.abs(x)))


def _layer_norm(x, g, b):
    mu = jnp.mean(x, axis=-1, keepdims=True)
    xc = x - mu
    var = jnp.mean(xc * xc, axis=-1, keepdims=True)
    return xc * lax.rsqrt(var + LN_EPS) * g + b


def _rms_norm(x, g):
    return x * lax.rsqrt(jnp.mean(x * x, axis=-1, keepdims=True) + RMS_EPS) * g


def _params(sem):
    return pltpu.CompilerParams(dimension_semantics=sem, vmem_limit_bytes=VMEM_LIMIT)


def _full_spec(a):
    nd = a.ndim
    return pl.BlockSpec(a.shape, lambda *_: (0,) * nd)


def _row_call(kernel, name, rows, tm, row_ins, full_ins, outs, scratch=()):
    assert rows % tm == 0
    in_specs = [pl.BlockSpec((tm, a.shape[1]), lambda i: (i, 0)) for a in row_ins]
    in_specs += [_full_spec(a) for a in full_ins]
    trail = [c if isinstance(c, tuple) else (c,) for c, _ in outs]
    out_specs = [pl.BlockSpec((tm,) + t, lambda i, n=len(t): (i,) + (0,) * n) for t in trail]
    out_shape = [jax.ShapeDtypeStruct((rows,) + t, dt) for t, (_, dt) in zip(trail, outs)]
    return pl.pallas_call(
        kernel, name=name, grid=(rows // tm,), in_specs=in_specs, out_specs=out_specs,
        out_shape=out_shape, scratch_shapes=list(scratch),
        compiler_params=_params(("parallel",)))(*row_ins, *full_ins)


def _even_proj_kernel(x_ref, w_ref, wf_ref, lbl_ref, fb_ref,
                      hq_ref, lf_ref, hk_ref, hv_ref, hgate_ref,
                      fq_ref, fk_ref, fv_ref, fk16_ref, fv16_ref, flf_ref, *, layer):
    xb = x_ref[...].astype(BF16)

    def blk(j):
        return _dot(xb, w_ref[:, j * HG_W:(j + 1) * HG_W])

    logits = lbl_ref[...]
    e = jnp.exp(logits - jnp.max(logits, axis=0, keepdims=True))
    lb = jnp.sum(e[:layer + 1], axis=0, keepdims=True) / jnp.sum(e, axis=0, keepdims=True)

    hq_ref[...] = blk(0).astype(BF16)
    zf = blk(1)
    lf_ref[...] = jnp.log(lb + (1.0 - lb) * _sigmoid(zf))
    hk_ref[...] = ((1.0 - lb) * _sigmoid(-zf)).astype(BF16)
    hv_ref[...] = blk(2).astype(BF16)
    hgate_ref[...] = _sigmoid(blk(3)).astype(BF16)
    fq_ref[...] = (blk(4) * (FOX_DH ** -0.5 * LOG2E)).astype(BF16)
    fk = blk(5)
    fk16_ref[...] = fk.astype(BF16)
    fv = blk(6)
    fv16_ref[...] = fv.astype(BF16)
    for h in range(FOX_HEADS):
        fk_ref[:, h, :] = fk[:, h * FOX_DH:(h + 1) * FOX_DH]
        fv_ref[:, h, :] = fv[:, h * FOX_DH:(h + 1) * FOX_DH]
    flf_ref[...] = _log_sigmoid(_dot(xb, wf_ref[...]) + fb_ref[...])


def _even_proj(x, w_main, w_f, lb_logits, fb_pad, tm, layer):
    rows = x.shape[0]
    outs = [(HG_W, BF16), (HG_W, F32), (HG_W, BF16), (HG_W, BF16), (HG_W, BF16),
            (FOX_W, BF16), ((FOX_HEADS, FOX_DH), F32), ((FOX_HEADS, FOX_DH), F32), (FOX_W, BF16), (FOX_W, BF16),
            (LANES, F32)]
    return _row_call(functools.partial(_even_proj_kernel, layer=layer), "even_proj", rows, tm,
                     [x], [w_main, w_f, lb_logits, fb_pad], outs)


def _bcast_sub(x, j):
    n, c = x.shape
    x3 = x.reshape(n // HG_SUB, HG_SUB, c)
    return jnp.broadcast_to(x3[:, j:j + 1, :], x3.shape).reshape(n, c)


def _level_ref(b, w):
    n, c = b.shape
    parts = [jnp.broadcast_to(b[m * 2 * w + w - 1:m * 2 * w + w, :], (2 * w, c)) for m in range(n // (2 * w))]
    return parts[0] if len(parts) == 1 else jnp.concatenate(parts, axis=0)


def _hgrn2_kernel(q_ref, lf_ref, k_ref, v_ref, gate_ref, flf_ref, g_ref, e_ref, s0_ref, f0_ref,
                  o_ref, fcum_ref, sout_ref, st_scr, fc_scr, *, n_chunks):
    i = pl.program_id(1)
    C = CHUNK

    @pl.when(i == 0)
    def _():
        for h in range(HG_HEADS):
            st_scr[h] = s0_ref[h].T
        fc_scr[...] = f0_ref[...]

    row = lax.broadcasted_iota(jnp.int32, (C, 1), 0)
    col = lax.broadcasted_iota(jnp.int32, (1, C), 1)
    tri = (col <= row).astype(BF16)
    same = lambda w: (row // w) == (col // w)
    levels = (32, 16, 8)

    for c in range(n_chunks):
        sl = slice(c * C, (c + 1) * C)
        fcum = _cumsum_rows(tri, flf_ref[sl, :]) + fc_scr[...]
        fcum_ref[sl, :] = fcum
        fc_scr[...] = fcum[C - 1:C, :]

        for h0 in range(0, HG_HEADS, HG_GROUP):
            gs = slice(h0 * HG_DK, (h0 + HG_GROUP) * HG_DK)
            b = _cumsum_rows(tri, lf_ref[sl, gs]) * LOG2E
            q = q_ref[sl, gs].astype(F32)
            k = k_ref[sl, gs].astype(F32)
            v = v_ref[sl, gs]
            qb = (q * jnp.exp2(b)).astype(BF16)
            b_last = b[C - 1:C, :]
            kd = (k * jnp.exp2(b_last - b)).astype(BF16)
            e_last = jnp.exp2(b_last)

            pjs = [(jnp.exp2(jnp.where((row % HG_SUB) >= j, b - _bcast_sub(b, j), NEG)) * q
                    * _bcast_sub(k, j)).astype(BF16) for j in range(HG_SUB)]
            lv = []
            for w in levels:
                upper = (row % (2 * w)) >= w
                ew = jnp.exp2(-jnp.abs(b - _level_ref(b, w)))
                lv.append((jnp.where(upper, q * ew, 0.0).astype(BF16), jnp.where(upper, 0.0, k * ew).astype(BF16)))

            for hh in range(HG_GROUP):
                h = h0 + hh
                hs = slice(hh * HG_DK, (hh + 1) * HG_DK)
                ho = slice(h * HG_DK, (h + 1) * HG_DK)
                a = jnp.where(same(HG_SUB), _dot(jnp.concatenate([p[:, hs] for p in pjs], axis=1), e_ref[...]), 0.0)
                for w, (qw, kw) in zip(levels, lv):
                    aw = _dot_nt(qw[:, hs], kw[:, hs])
                    a = a + (aw if 2 * w == C else jnp.where(same(2 * w), aw, 0.0))
                st = st_scr[h]
                vh = v[:, hs]
                o = _dot(a.astype(BF16), vh) + _dot_nt(qb[:, hs], st.astype(BF16))
                st_scr[h] = st * e_last[:, hs] + _dot_tn(vh, kd[:, hs])
                o = _rms_norm(o, g_ref[:, ho])
                o_ref[sl, ho] = (o * gate_ref[sl, ho].astype(F32)).astype(BF16)

    @pl.when(i == pl.num_programs(1) - 1)
    def _():
        for h in range(HG_HEADS):
            sout_ref[h] = st_scr[h].T


def _hgrn2(q, lf, k, v, gate, flf, g, e_mat, s0, f0, n_seq, seq_len, row_off, tb):
    assert seq_len % tb == 0 and tb % CHUNK == 0 and row_off % tb == 0
    nb = seq_len // tb
    off = row_off // tb
    per_seq = s0.shape[0] > 1
    rmap = lambda s, i: (off + s * nb + i, 0)
    omap = lambda s, i: (s * nb + i, 0)
    smap = (lambda s, i: (s, 0, 0, 0)) if per_seq else (lambda s, i: (0, 0, 0, 0))
    fmap = (lambda s, i: (s, 0, 0)) if per_seq else (lambda s, i: (0, 0, 0))
    in_specs = [pl.BlockSpec((tb, HG_W), rmap) for _ in range(5)]
    in_specs += [pl.BlockSpec((tb, LANES), rmap), _full_spec(g), _full_spec(e_mat),
                 pl.BlockSpec((None, HG_HEADS, HG_DK, HG_DV), smap), pl.BlockSpec((None, 1, LANES), fmap)]
    out_specs = [pl.BlockSpec((tb, HG_W), omap), pl.BlockSpec((tb, LANES), omap),
                 pl.BlockSpec((None, HG_HEADS, HG_DK, HG_DV), lambda s, i: (s, 0, 0, 0))]
    out_shape = [jax.ShapeDtypeStruct((n_seq * seq_len, HG_W), BF16),
                 jax.ShapeDtypeStruct((n_seq * seq_len, LANES), F32),
                 jax.ShapeDtypeStruct((n_seq, HG_HEADS, HG_DK, HG_DV), F32)]
    scratch = [pltpu.VMEM((HG_HEADS, HG_DV, HG_DK), F32), pltpu.VMEM((1, LANES), F32)]
    return pl.pallas_call(
        functools.partial(_hgrn2_kernel, n_chunks=tb // CHUNK), name="hgrn2",
        grid=(n_seq, nb), in_specs=in_specs, out_specs=out_specs, out_shape=out_shape,
        scratch_shapes=scratch, compiler_params=_params(("parallel", "arbitrary")))(
            q, lf, k, v, gate, flf, g, e_mat, s0, f0)


def _cumsum_kernel(x_ref, tri_ref, o_ref, carry):
    @pl.when(pl.program_id(0) == 0)
    def _():
        carry[...] = jnp.zeros_like(carry)

    hi, mid, lo = _split3(x_ref[...])
    tri = tri_ref[...]
    out = _dot(hi, tri) + _dot(mid, tri) + _dot(lo, tri) + carry[...]
    o_ref[...] = out
    carry[...] = out[:, out.shape[1] - 1:]


def _cumsum_lanes(x, tb):
    r, seq_len = x.shape
    tri = (jnp.arange(tb)[:, None] <= jnp.arange(tb)[None, :]).astype(BF16)
    return pl.pallas_call(
        _cumsum_kernel, name="cumsum", grid=(seq_len // tb,),
        in_specs=[pl.BlockSpec((r, tb), lambda i: (0, i)), _full_spec(tri)],
        out_specs=pl.BlockSpec((r, tb), lambda i: (0, i)),
        out_shape=jax.ShapeDtypeStruct(x.shape, F32),
        scratch_shapes=[pltpu.VMEM((r, 1), F32)],
        compiler_params=_params(("arbitrary",)))(x, tri)


def _flash_kernel(*refs, n_past_blk, tkp, tq, has_bias, has_rope, mask_mode, has_past, past_heads):
    it = iter(refs)
    q_ref = next(it)
    fq_ref = next(it) if has_bias else None
    if has_past:
        kp_ref, vp_ref = next(it), next(it)
        rp_ref = next(it) if has_rope else None
        fkp_ref = next(it) if has_bias else None
    kn_ref, vn_ref = next(it), next(it)
    rn_ref = next(it) if has_rope else None
    fkn_ref = next(it) if has_bias else None
    o_ref = next(it)
    m_scr, acc_scr = next(it), next(it)
    dv = o_ref.shape[1]

    q = q_ref[...]
    m_scr[...] = jnp.full(m_scr.shape, NEG, F32)
    acc_scr[...] = jnp.zeros(acc_scr.shape, F32)
    fq_b = jnp.broadcast_to(fq_ref[...] * LOG2E, (tq, LANES)) if has_bias else None

    def scores(k, r, fk):
        if has_rope:
            k = jnp.concatenate([k, r], axis=1)
        s = _dot_nt(q, k.astype(BF16))
        if has_bias:
            s = s + jnp.tile(fq_b, (1, s.shape[1] // LANES)) if s.shape[1] % LANES == 0 else s + fq_b[:, :1]
            s = s - fk * LOG2E
        return s

    def update(s, v, mask):
        if mask is not None:
            s = jnp.where(mask, s, NEG)
        m_prev = m_scr[...]
        m_new = jnp.maximum(m_prev, jnp.max(s, axis=1, keepdims=True))
        alpha = jnp.exp2(m_prev - m_new)
        if s.shape[1] % LANES == 0:
            p = jnp.exp2(s - jnp.tile(m_new, (1, s.shape[1] // LANES)))
        else:
            p = jnp.exp2(s - m_new[:, :1])
        v1 = jnp.concatenate([v.astype(BF16), jnp.ones((v.shape[0], LANES), BF16)], axis=1)
        acc_scr[...] = jnp.tile(alpha, (1, acc_scr.shape[1] // LANES)) * acc_scr[...] + _dot(p.astype(BF16), v1)
        m_scr[...] = m_new

    past_kv = (lambda ref, rs: ref[rs, pl.program_id(1), :]) if past_heads else (lambda ref, rs: ref[rs, :])

    def past_block(rs):
        return (past_kv(kp_ref, rs), rp_ref[rs, :] if has_rope else None, fkp_ref[:, rs] if has_bias else None)

    def new_block(rs):
        return (kn_ref[rs, :], rn_ref[rs, :] if has_rope else None, fkn_ref[:, rs] if has_bias else None)

    if has_past:
        if n_past_blk == 1:
            update(scores(*past_block(slice(None))), past_kv(vp_ref, slice(None)), None)
        else:
            def past_body(j, carry):
                rs = pl.ds(pl.multiple_of(j * tkp, tkp), tkp)
                update(scores(*past_block(rs)), past_kv(vp_ref, rs), None)
                return carry
            lax.fori_loop(0, n_past_blk, past_body, 0)

    row = lax.broadcasted_iota(jnp.int32, (tq, 1), 0)
    col = lax.broadcasted_iota(jnp.int32, (1, tq), 1)
    if mask_mode == "causal":
        mask = col <= row
    elif mask_mode == "chunk":
        mask = (col // CHUNK) <= (row // CHUNK)
    else:
        mask = None

    update(scores(*new_block(slice(None))), vn_ref[...], mask)
    acc = acc_scr[...]
    o_ref[...] = (acc[:, :dv] / acc[:, dv:]).astype(o_ref.dtype)


def _flash(q, kn, vn, *, n_seq, n_heads, seq_len, tq, dq, dk, dv, q_off, k_off, mask_mode,
           fq=None, fkn=None, rn=None, past=None):
    assert seq_len % tq == 0 and q_off % tq == 0 and k_off % seq_len == 0
    nq = seq_len // tq
    qo = q_off // tq
    ko = k_off // seq_len
    has_bias = fq is not None
    has_rope = rn is not None
    has_past = past is not None
    ins, specs = [q], [pl.BlockSpec((tq, dq), lambda b, h, i: (qo + b * nq + i, h))]
    if has_bias:
        ins.append(fq)
        specs.append(pl.BlockSpec((None, tq, 1), lambda b, h, i: (h, qo + b * nq + i, 0)))
    n_past_blk, tkp, past_heads = 0, 0, False
    if has_past:
        tp = past["k"].shape[1]
        tkp = past["tk"]
        assert tp % tkp == 0
        n_past_blk = tp // tkp
        pb = (lambda b: b) if past["k"].shape[0] > 1 else (lambda b: 0)
        ins += [past["k"], past["v"]]
        past_heads = past["k"].ndim == 4
        if past_heads:
            specs += [pl.BlockSpec((None, tp, n_heads, dk), lambda b, h, i: (pb(b), 0, 0, 0)),
                      pl.BlockSpec((None, tp, n_heads, dv), lambda b, h, i: (pb(b), 0, 0, 0))]
        else:
            specs += [pl.BlockSpec((None, tp, dk), lambda b, h, i: (pb(b), 0, h)),
                      pl.BlockSpec((None, tp, dv), lambda b, h, i: (pb(b), 0, h))]
        if has_rope:
            ins.append(past["r"])
            specs.append(pl.BlockSpec((None, tp, LANES), lambda b, h, i: (pb(b), 0, 0)))
        if has_bias:
            ins.append(past["fk"])
            specs.append(pl.BlockSpec((None, None, 1, tp), lambda b, h, i: (pb(b), h, 0, 0)))
    ins += [kn, vn]
    specs += [pl.BlockSpec((seq_len, dk), lambda b, h, i: (ko + b, h)),
              pl.BlockSpec((seq_len, dv), lambda b, h, i: (ko + b, h))]
    if has_rope:
        ins.append(rn)
        specs.append(pl.BlockSpec((seq_len, LANES), lambda b, h, i: (ko + b, 0)))
    if has_bias:
        ins.append(fkn)
        specs.append(pl.BlockSpec((None, None, 1, seq_len), lambda b, h, i: (h, ko + b, 0, 0)))
    assert nq == 1
    kern = functools.partial(_flash_kernel, n_past_blk=n_past_blk, tkp=tkp, tq=tq, has_bias=has_bias,
                             has_rope=has_rope, mask_mode=mask_mode, has_past=has_past, past_heads=past_heads)
    return pl.pallas_call(
        kern, name="flash", grid=(n_seq, n_heads, nq), in_specs=specs,
        out_specs=pl.BlockSpec((tq, dv), lambda b, h, i: (b * nq + i, h)),
        out_shape=jax.ShapeDtypeStruct((n_seq * seq_len, n_heads * dv), BF16),
        scratch_shapes=[pltpu.VMEM((tq, LANES), F32), pltpu.VMEM((tq, dv + LANES), F32)],
        compiler_params=_params(("parallel", "parallel", "arbitrary")))(*ins)


FLASH_UNROLL_OFF = 14
FLASH_UNROLL_DIAG = 8


def _tri_tables(nq):
    pairs = [(qi, kj) for qi in range(nq) for kj in range(qi)] + [(qi, qi) for qi in range(nq)] + [(0, 0)]
    return (jnp.array([p[0] for p in pairs], jnp.int32), jnp.array([p[1] for p in pairs], jnp.int32))


def _flash_tri_kernel(qt_ref, kt_ref, *refs, tq, nq, has_bias, has_rope, mask_mode):
    it = iter(refs)
    q_ref = next(it)
    fq_ref = next(it) if has_bias else None
    kp_ref, vp_ref = next(it), next(it)
    rp_ref = next(it) if has_rope else None
    fkp_ref = next(it) if has_bias else None
    kn_ref, vn_ref = next(it), next(it)
    rn_ref = next(it) if has_rope else None
    fkn_ref = next(it) if has_bias else None
    o_ref = next(it)
    m_scr, acc_scr, sa_scr, sb_scr = next(it), next(it), next(it), next(it)
    fqb_scr = next(it) if has_bias else None
    dv = o_ref.shape[1]
    n_off = nq * (nq - 1) // 2
    tile = lambda j: pl.ds(pl.multiple_of(j * tq, tq), tq)
    ones = jnp.ones((tq, LANES), BF16)

    kp = kp_ref[...]
    if has_rope:
        kp = jnp.concatenate([kp, rp_ref[...]], axis=1)
    vp1 = jnp.concatenate([vp_ref[...], ones[:vp_ref.shape[0]]], axis=1)
    for i in range(nq):
        rs = slice(i * tq, (i + 1) * tq)
        s = _dot_nt(q_ref[rs, :], kp)
        if has_bias:
            fb = fq_ref[rs, :] * LOG2E
            fqb_scr[rs, :] = jnp.broadcast_to(fb, (tq, LANES))
            s = s + fb - fkp_ref[...] * LOG2E
        m0 = jnp.max(s, axis=1, keepdims=True)
        m_scr[i] = jnp.broadcast_to(m0, (tq, LANES))
        acc_scr[i] = _dot(jnp.exp2(s - m0).astype(BF16), vp1)

    def fill(s_ref, t):
        qs, ks = tile(qt_ref[t]), tile(kt_ref[t])
        k = kn_ref[ks, :]
        if has_rope:
            k = jnp.concatenate([k, rn_ref[ks, :]], axis=1)
        s = _dot_nt(q_ref[qs, :], k)
        if has_bias:
            s = s + jnp.tile(fqb_scr[qs, :], (1, tq // LANES)) - fkn_ref[:, ks] * LOG2E
        s_ref[...] = s

    def drain(s_ref, t, mask):
        qi = qt_ref[t]
        s = s_ref[...]
        if mask is not None:
            s = jnp.where(mask, s, NEG)
        m_prev = m_scr[qi]
        m_new = jnp.maximum(m_prev, jnp.max(s, axis=1, keepdims=True))
        p = jnp.exp2(s - jnp.tile(m_new, (1, tq // LANES)))
        v1 = jnp.concatenate([vn_ref[tile(kt_ref[t]), :], ones], axis=1)
        acc = jnp.tile(jnp.exp2(m_prev - m_new), (1, (dv + LANES) // LANES)) * acc_scr[qi] + _dot(p.astype(BF16), v1)
        return qi, m_new, acc

    def keep(s_ref, t):
        qi, m_new, acc = drain(s_ref, t, None)
        m_scr[qi] = m_new
        acc_scr[qi] = acc

    row = lax.broadcasted_iota(jnp.int32, (tq, 1), 0)
    col = lax.broadcasted_iota(jnp.int32, (1, tq), 1)
    mask = {"causal": col <= row, "chunk": (col // CHUNK) <= (row // CHUNK)}[mask_mode]

    def finish(s_ref, t):
        qi, _, acc = drain(s_ref, t, mask)
        o_ref[tile(qi), :] = (acc[:, :dv] / acc[:, dv:]).astype(o_ref.dtype)

    def pipeline(t0, n, unroll, consume):
        assert n % unroll == 0 and unroll % 2 == 0

        def body(i, carry):
            t = t0 + unroll * i
            for u in range(0, unroll, 2):
                fill(sb_scr, t + u + 1)
                consume(sa_scr, t + u)
                fill(sa_scr, t + u + 2)
                consume(sb_scr, t + u + 1)
            return carry
        lax.fori_loop(0, n // unroll, body, 0)

    fill(sa_scr, 0)
    pipeline(0, n_off, FLASH_UNROLL_OFF, keep)
    pipeline(n_off, nq, FLASH_UNROLL_DIAG, finish)


def _flash_tri(q, kn, vn, *, n_seq, n_heads, seq_len, tq, dq, dk, dv, mask_mode, past, fq=None, fkn=None, rn=None):
    nq = seq_len // tq
    has_bias = fq is not None
    has_rope = rn is not None
    tp = past["k"].shape[1]
    m3 = lambda f: (lambda b, h, qt, kt: f(b, h))
    ins, specs = [q], [pl.BlockSpec((seq_len, dq), m3(lambda b, h: (b, h)))]
    if has_bias:
        ins.append(fq)
        specs.append(pl.BlockSpec((None, seq_len, 1), m3(lambda b, h: (h, b, 0))))
    ins += [past["k"], past["v"]]
    specs += [pl.BlockSpec((None, tp, dk), m3(lambda b, h: (0, 0, h))),
              pl.BlockSpec((None, tp, dv), m3(lambda b, h: (0, 0, h)))]
    if has_rope:
        ins.append(past["r"])
        specs.append(pl.BlockSpec((None, tp, LANES), m3(lambda b, h: (0, 0, 0))))
    if has_bias:
        ins.append(past["fk"])
        specs.append(pl.BlockSpec((None, None, 1, tp), m3(lambda b, h: (0, h, 0, 0))))
    ins += [kn, vn]
    specs += [pl.BlockSpec((seq_len, dk), m3(lambda b, h: (b, h))),
              pl.BlockSpec((seq_len, dv), m3(lambda b, h: (b, h)))]
    if has_rope:
        ins.append(rn)
        specs.append(pl.BlockSpec((seq_len, LANES), m3(lambda b, h: (b, 0))))
    if has_bias:
        ins.append(fkn)
        specs.append(pl.BlockSpec((None, None, 1, seq_len), m3(lambda b, h: (h, b, 0, 0))))
    scratch = [pltpu.VMEM((nq, tq, LANES), F32), pltpu.VMEM((nq, tq, dv + LANES), F32),
               pltpu.VMEM((tq, tq), F32), pltpu.VMEM((tq, tq), F32)]
    if has_bias:
        scratch.append(pltpu.VMEM((seq_len, LANES), F32))
    grid_spec = pltpu.PrefetchScalarGridSpec(
        num_scalar_prefetch=2, grid=(n_seq, n_heads), in_specs=specs,
        out_specs=pl.BlockSpec((seq_len, dv), m3(lambda b, h: (b, h))), scratch_shapes=scratch)
    kern = functools.partial(_flash_tri_kernel, tq=tq, nq=nq, has_bias=has_bias, has_rope=has_rope,
                             mask_mode=mask_mode)
    return pl.pallas_call(
        kern, name="flash_tri", grid_spec=grid_spec,
        out_shape=jax.ShapeDtypeStruct((n_seq * seq_len, n_heads * dv), BF16),
        compiler_params=_params(("parallel", "arbitrary")))(*_tri_tables(nq), *ins)


def _route(sc, sb):
    def top2_sum(v):
        a, b, c, d = v
        a, b = jnp.maximum(a, b), jnp.minimum(a, b)
        c, d = jnp.maximum(c, d), jnp.minimum(c, d)
        hi, lo2 = jnp.maximum(a, c), jnp.minimum(a, c)
        return hi + jnp.maximum(lo2, jnp.maximum(b, d))

    gs = [top2_sum(sb[g * EXPERTS_PER_GROUP:(g + 1) * EXPERTS_PER_GROUP]) for g in range(N_GROUPS)]
    best_v, best_g = gs[0], jnp.zeros(gs[0].shape, jnp.int32)
    for g in range(1, N_GROUPS):
        upd = gs[g] > best_v
        best_v = jnp.where(upd, gs[g], best_v)
        best_g = jnp.where(upd, g, best_g)
    masked = [jnp.where(best_g == (e // EXPERTS_PER_GROUP), sb[e], -jnp.inf) for e in range(N_EXPERTS)]

    def argmax_first(vals, exclude=None):
        bv = jnp.full(vals[0].shape, -jnp.inf, F32)
        bi = jnp.full(vals[0].shape, -1, jnp.int32)
        for e, v in enumerate(vals):
            upd = v > bv
            if exclude is not None:
                upd = upd & (exclude != e)
            bv = jnp.where(upd, v, bv)
            bi = jnp.where(upd, e, bi)
        return bi

    i1 = argmax_first(masked)
    i2 = argmax_first(masked, exclude=i1)
    w1 = sum(jnp.where(i1 == e, sc[e], 0.0) for e in range(N_EXPERTS))
    w2 = sum(jnp.where(i2 == e, sc[e], 0.0) for e in range(N_EXPERTS))
    tot = w1 + w2
    w1, w2 = w1 / tot, w2 / tot
    comb = [jnp.where(i1 == e, w1, 0.0) + jnp.where(i2 == e, w2, 0.0) for e in range(N_EXPERTS)]
    return comb + [i1.astype(F32), i2.astype(F32), w1, w2]


def _mix_kernel(*refs, n_act):
    x_ref = refs[0]
    a_refs = refs[1:1 + n_act]
    w_ref, g_ref, b_ref, rw_ref, rb_ref, x1_ref, x1p_ref, comb_ref, ct_scr = refs[1 + n_act:]
    half = D_MODEL // 2
    tm = x_ref.shape[0]
    group = MIX_GROUP if tm % MIX_GROUP == 0 else tm
    ct_scr[...] = jnp.zeros(ct_scr.shape, F32)
    for r0 in range(0, tm, group):
        rs = slice(r0, r0 + group)
        ys = []
        for n0 in (0, half):
            y = None
            k0 = 0
            for a_ref in a_refs:
                kw = a_ref.shape[1]
                part = _dot(a_ref[rs, :], w_ref[k0:k0 + kw, n0:n0 + half])
                y = part if y is None else y + part
                k0 += kw
            ys.append(y)
        x1 = _layer_norm(ALPHA * x_ref[rs, :] + jnp.concatenate(ys, axis=1), g_ref[...], b_ref[...])
        x1_ref[rs, :] = x1

        x1p_ref[rs, :] = _pack_pair(x1[:, :half], x1[:, half:])
        logits = _dot(x1.astype(BF16), rw_ref[...])
        scores_t = _sigmoid(logits).T
        sc = [scores_t[e:e + 1, :] for e in range(N_EXPERTS)]
        sb = [sc[e] + rb_ref[e:e + 1, :] for e in range(N_EXPERTS)]
        for r, val in enumerate(_route(sc, sb)):
            ct_scr[r:r + 1, rs] = val
        comb_ref[rs, :] = ct_scr[:, rs].T


def _mix(x, acts, w_out, ln_g, ln_b, rw, rb, tm):
    rows = x.shape[0]
    return _row_call(functools.partial(_mix_kernel, n_act=len(acts)), "mix", rows, tm,
                     [x] + list(acts), [w_out, ln_g, ln_b, rw, rb],
                     [(D_MODEL, F32), (D_MODEL // 2, jnp.uint32), (LANES, F32)],
                     scratch=[pltpu.VMEM((LANES, tm), F32)])


def _moe_kernel(x_ref, comb_ref, wg_ref, wu_ref, wd_ref, g_ref, b_ref, o_ref, xb_scr, acc_scr):
    e = pl.program_id(1)

    @pl.when(e == 0)
    def _():
        xb_scr[...] = x_ref[...].astype(BF16)
        acc_scr[...] = jnp.zeros(acc_scr.shape, F32)

    xb = xb_scr[...]
    lane = lax.broadcasted_iota(jnp.int32, (1, LANES), 1)
    c_e = jnp.sum(jnp.where(lane == e, comb_ref[...], 0.0), axis=1, keepdims=True)
    gate = _dot(xb, wg_ref[...].astype(BF16))
    h = gate * _sigmoid(gate) * _dot(xb, wu_ref[...].astype(BF16))
    acc_scr[...] += _dot((h * c_e).astype(BF16), wd_ref[...].astype(BF16))

    @pl.when(e == N_EXPERTS - 1)
    def _():
        o_ref[...] = _layer_norm(ALPHA * x_ref[...] + acc_scr[...], g_ref[...], b_ref[...])


def _moe(x, comb, wg, wu, wd, layer, ln_g, ln_b, tm):
    rows = x.shape[0]
    assert rows % tm == 0
    return pl.pallas_call(
        _moe_kernel, name="moe", grid=(rows // tm, N_EXPERTS),
        in_specs=[pl.BlockSpec((tm, D_MODEL), lambda i, e: (i, 0)),
                  pl.BlockSpec((tm, LANES), lambda i, e: (i, 0)),
                  pl.BlockSpec((None, None, D_MODEL, D_EXPERT), lambda i, e: (layer, e, 0, 0)),
                  pl.BlockSpec((None, None, D_MODEL, D_EXPERT), lambda i, e: (layer, e, 0, 0)),
                  pl.BlockSpec((None, None, D_EXPERT, D_MODEL), lambda i, e: (layer, e, 0, 0)),
                  _full_spec(ln_g), _full_spec(ln_b)],
        out_specs=pl.BlockSpec((tm, D_MODEL), lambda i, e: (i, 0)),
        out_shape=jax.ShapeDtypeStruct((rows, D_MODEL), F32),
        scratch_shapes=[pltpu.VMEM((tm, D_MODEL), BF16), pltpu.VMEM((tm, D_MODEL), F32)],
        compiler_params=_params(("parallel", "arbitrary")))(x, comb, wg, wu, wd, ln_g, ln_b)


ROUTE_E1, ROUTE_E2, ROUTE_W1, ROUTE_W2 = N_EXPERTS, N_EXPERTS + 1, N_EXPERTS + 2, N_EXPERTS + 3
TE = 1024
SC_WINDOW = 128
RANK_TILE = 1024
MIX_GROUP = 256


def _pack_pair(a, b):
    au = lax.bitcast_convert_type(a.astype(BF16).astype(F32), jnp.uint32)
    bu = lax.bitcast_convert_type(b.astype(BF16).astype(F32), jnp.uint32)
    return (au >> 16) | (bu & jnp.uint32(0xFFFF0000))


def _unpack_pair(w):
    a = lax.bitcast_convert_type(w << 16, F32)
    b = lax.bitcast_convert_type(w & jnp.uint32(0xFFFF0000), F32)
    return a, b


def _rank_kernel(route_ref, pos_ref, texp_ref, nused_ref, cnt_scr, carry_scr, seg_scr, before_scr):
    ph, i = pl.program_id(0), pl.program_id(1)
    T = route_ref.shape[0]
    lane = lax.broadcasted_iota(jnp.int32, (1, LANES), 1)
    lane_f = lane.astype(F32)
    r = route_ref[...]
    e1, e2 = r[:, ROUTE_E1:ROUTE_E1 + 1], r[:, ROUTE_E2:ROUTE_E2 + 1]
    m1, m2 = lane_f == e1, lane_f == e2
    m = jnp.where(m1 | m2, 1.0, 0.0)
    colsum = jnp.sum(m, axis=0, keepdims=True)

    @pl.when((ph == 0) & (i == 0))
    def _():
        cnt_scr[...] = jnp.zeros(cnt_scr.shape, F32)

    @pl.when(ph == 0)
    def _():
        cnt_scr[...] += colsum

    @pl.when((ph == 1) & (i == 0))
    def _():
        cnt = cnt_scr[...].astype(jnp.int32)
        padded = (((cnt + (TE - 1)) // TE) * TE).astype(F32)
        rr = lax.broadcasted_iota(jnp.int32, (LANES, 1), 0)
        upper = (rr < lane).astype(BF16)
        hi, mid, lo = _split3(jnp.broadcast_to(padded, (HG_SUB, LANES)))
        seg = (_dot(hi, upper) + _dot(mid, upper) + _dot(lo, upper))[:1, :]
        seg_scr[...] = seg
        carry_scr[...] = jnp.zeros(carry_scr.shape, F32)
        seg_end = seg + padded
        tile_row = lax.broadcasted_iota(jnp.int32, texp_ref.shape, 1).astype(F32) * float(TE)
        te_acc = jnp.zeros(texp_ref.shape, jnp.int32)
        for e in range(N_EXPERTS):
            te_acc = te_acc + jnp.where(seg_end[:, e:e + 1] <= tile_row, 1, 0)
        texp_ref[...] = jnp.minimum(te_acc, N_EXPERTS - 1)
        nused_ref[...] = jnp.broadcast_to(seg_end[:, N_EXPERTS - 1:N_EXPERTS] / float(TE), nused_ref.shape).astype(jnp.int32)

    @pl.when((ph == 1) & (i == 0))
    def _():
        row = lax.broadcasted_iota(jnp.int32, (T, 1), 0)
        col = lax.broadcasted_iota(jnp.int32, (1, T), 1)
        before_scr[...] = (col < row).astype(BF16)

    @pl.when(ph == 1)
    def _():
        cum = _dot(before_scr[...], m.astype(BF16)) + carry_scr[...] + seg_scr[...]
        p1 = jnp.sum(jnp.where(m1, cum, 0.0), axis=1, keepdims=True)
        p2 = jnp.sum(jnp.where(m2, cum, 0.0), axis=1, keepdims=True)
        pos_ref[...] = jnp.where(lane == 0, p1, jnp.where(lane == 1, p2, 0.0)).astype(jnp.int32)
        carry_scr[...] += colsum


def _rank(route, n_tiles, tm):
    rows = route.shape[0]
    nb = rows // tm
    nt_pad = -(-n_tiles // LANES) * LANES
    return pl.pallas_call(
        _rank_kernel, name="rank", grid=(2, nb),
        in_specs=[pl.BlockSpec((tm, LANES), lambda ph, i: (i, 0))],
        out_specs=[pl.BlockSpec((tm, LANES), lambda ph, i: (i * ph, 0)),
                   pl.BlockSpec((1, nt_pad), lambda ph, i: (0, 0)),
                   pl.BlockSpec((1, LANES), lambda ph, i: (0, 0))],
        out_shape=[jax.ShapeDtypeStruct((rows, LANES), jnp.int32),
                   jax.ShapeDtypeStruct((1, nt_pad), jnp.int32),
                   jax.ShapeDtypeStruct((1, LANES), jnp.int32)],
        scratch_shapes=[pltpu.VMEM((1, LANES), F32), pltpu.VMEM((1, LANES), F32), pltpu.VMEM((1, LANES), F32),
                        pltpu.VMEM((tm, tm), BF16)],
        compiler_params=_params(("arbitrary", "arbitrary")))(route)


def _sc_mesh():
    return plsc.VectorSubcoreMesh(core_axis_name="c", subcore_axis_name="s")


def _sc_scatter_rows(x, idx, n_out):
    rows, d = x.shape
    mesh = _sc_mesh()
    n_workers = mesh.num_cores * mesh.num_subcores
    steps = idx.shape[1] // SC_WINDOW // n_workers
    assert steps * SC_WINDOW * n_workers == idx.shape[1] and rows % SC_WINDOW == 0

    @functools.partial(pl.kernel, out_type=jax.ShapeDtypeStruct((n_out, d), x.dtype), mesh=mesh,
                       scratch_types=[pltpu.VMEM((1, SC_WINDOW), jnp.int32), pltpu.VMEM((SC_WINDOW, d), x.dtype)])
    def scatter(x_hbm, i_hbm, o_hbm, i_vmem, buf):
        first = (lax.axis_index("c") * mesh.num_subcores + lax.axis_index("s")) * steps

        @pl.loop(0, steps)
        def _(t):
            off = (first + t) * SC_WINDOW
            pltpu.sync_copy(i_hbm.at[:, pl.ds(off, SC_WINDOW)], i_vmem)
            pltpu.sync_copy(x_hbm.at[pl.ds(off % rows, SC_WINDOW)], buf)
            pltpu.sync_copy(buf, o_hbm.at[i_vmem.at[0]])

    return scatter(x, idx)


def _sc_gather_rows(x, idx):
    d = x.shape[1]
    n = idx.shape[1]
    mesh = _sc_mesh()
    n_workers = mesh.num_cores * mesh.num_subcores
    steps = n // SC_WINDOW // n_workers
    assert steps * SC_WINDOW * n_workers == n

    @functools.partial(pl.kernel, out_type=jax.ShapeDtypeStruct((n, d), x.dtype), mesh=mesh,
                       scratch_types=[pltpu.VMEM((1, SC_WINDOW), jnp.int32), pltpu.VMEM((SC_WINDOW, d), x.dtype)])
    def gather(x_hbm, i_hbm, o_hbm, i_vmem, buf):
        first = (lax.axis_index("c") * mesh.num_subcores + lax.axis_index("s")) * steps

        @pl.loop(0, steps)
        def _(t):
            off = (first + t) * SC_WINDOW
            pltpu.sync_copy(i_hbm.at[:, pl.ds(off, SC_WINDOW)], i_vmem)
            pltpu.sync_copy(x_hbm.at[i_vmem.at[0]], buf)
            pltpu.sync_copy(buf, o_hbm.at[pl.ds(off, SC_WINDOW)])

    return gather(x, idx)


def _gmm_kernel(texp_ref, nused_ref, x_ref, wg_ref, wu_ref, wd_ref, o_ref):
    @pl.when(pl.program_id(0) < nused_ref[0])
    def _():
        a, b = _unpack_pair(x_ref[...])
        xb = jnp.concatenate([a.astype(BF16), b.astype(BF16)], axis=1)
        gate = _dot(xb, wg_ref[...].astype(BF16))
        h = gate * _sigmoid(gate) * _dot(xb, wu_ref[...].astype(BF16))
        y = _dot(h.astype(BF16), wd_ref[...].astype(BF16))
        o_ref[...] = _pack_pair(y[:, :D_MODEL // 2], y[:, D_MODEL // 2:])


def _gmm(xs, texp, nused, wg, wu, wd, layer):
    rows = xs.shape[0]
    wmap = lambda d, te, nu: (layer, te[d], 0, 0)
    grid_spec = pltpu.PrefetchScalarGridSpec(
        num_scalar_prefetch=2, grid=(rows // TE,),
        in_specs=[pl.BlockSpec((TE, D_MODEL // 2), lambda d, te, nu: (d, 0)),
                  pl.BlockSpec((None, None, D_MODEL, D_EXPERT), wmap),
                  pl.BlockSpec((None, None, D_MODEL, D_EXPERT), wmap),
                  pl.BlockSpec((None, None, D_EXPERT, D_MODEL), wmap)],
        out_specs=pl.BlockSpec((TE, D_MODEL // 2), lambda d, te, nu: (d, 0)))
    return pl.pallas_call(
        _gmm_kernel, name="gmm", grid_spec=grid_spec,
        out_shape=jax.ShapeDtypeStruct((rows, D_MODEL // 2), jnp.uint32),
        compiler_params=_params(("arbitrary",)))(texp, nused, xs, wg, wu, wd)


def _combine_kernel(x_ref, g0_ref, g1_ref, route_ref, g_ref, b_ref, o_ref):
    r = route_ref[...]
    y0 = jnp.concatenate(_unpack_pair(g0_ref[...]), axis=1)
    y1 = jnp.concatenate(_unpack_pair(g1_ref[...]), axis=1)
    f = y0 * r[:, ROUTE_W1:ROUTE_W1 + 1] + y1 * r[:, ROUTE_W2:ROUTE_W2 + 1]
    o_ref[...] = _layer_norm(ALPHA * x_ref[...] + f, g_ref[...], b_ref[...])


def _combine(x, g, route, ln_g, ln_b, tm):
    rows = x.shape[0]
    nb = rows // tm
    return pl.pallas_call(
        _combine_kernel, name="combine", grid=(nb,),
        in_specs=[pl.BlockSpec((tm, D_MODEL), lambda i: (i, 0)),
                  pl.BlockSpec((tm, D_MODEL // 2), lambda i: (i, 0)),
                  pl.BlockSpec((tm, D_MODEL // 2), lambda i: (nb + i, 0)),
                  pl.BlockSpec((tm, LANES), lambda i: (i, 0)), _full_spec(ln_g), _full_spec(ln_b)],
        out_specs=pl.BlockSpec((tm, D_MODEL), lambda i: (i, 0)),
        out_shape=jax.ShapeDtypeStruct((rows, D_MODEL), F32),
        compiler_params=_params(("parallel",)))(x, g, g, route, ln_g, ln_b)


def _moe_routed(x1, x1b, route, wg, wu, wd, layer, ln_g, ln_b, tm):
    rows = x1.shape[0]
    n_rows = 2 * rows + N_EXPERTS * TE
    pos, texp, nused = _rank(route, n_rows // TE, RANK_TILE)
    idx = jnp.concatenate([pos[:, 0], pos[:, 1]])[None, :]
    xs = _sc_scatter_rows(x1b, idx, n_rows)
    ys = _gmm(xs, texp[0, :n_rows // TE], nused[0, :1], wg, wu, wd, layer)
    g = _sc_gather_rows(ys, idx)
    return _combine(x1, g, route, ln_g, ln_b, tm)


def _rope128(x, cos_t, sin_t):
    lane = lax.broadcasted_iota(jnp.int32, (1, LANES), 1)
    half = MLA_ROPE // 2
    swapped = jnp.where(lane < half, pltpu.roll(x, LANES - half, axis=1), pltpu.roll(x, half, axis=1))
    return x * cos_t + swapped * sin_t


def _odd_proj_kernel(x_ref, cos_ref, sin_ref, w_ref, gq_ref, gkv_ref, wuq_ref,
                     q_ref, ckv_ref, kpe_ref, kpe16_ref):
    z = _dot(x_ref[...].astype(BF16), w_ref[...])
    cq = _rms_norm(z[:, :MLA_Q_LORA], gq_ref[...])
    ckv_ref[...] = _rms_norm(z[:, MLA_Q_LORA:MLA_Q_LORA + MLA_KV_LORA], gkv_ref[...])
    cos_t, sin_t = cos_ref[...], sin_ref[...]
    kpe = _rope128(z[:, MLA_Q_LORA + MLA_KV_LORA:], cos_t, sin_t)
    kpe_ref[...] = kpe[:, :MLA_ROPE]
    kpe16_ref[...] = kpe.astype(BF16)
    qf = _dot(cq.astype(BF16), wuq_ref[...])
    scale = (MLA_NOPE + MLA_ROPE) ** -0.5 * LOG2E
    for h in range(MLA_HEADS):
        c0 = h * MLA_QPAD
        q_ref[:, c0:c0 + MLA_NOPE] = (qf[:, c0:c0 + MLA_NOPE] * scale).astype(BF16)
        qr = _rope128(qf[:, c0 + MLA_NOPE:c0 + MLA_QPAD], cos_t, sin_t)
        q_ref[:, c0 + MLA_NOPE:c0 + MLA_QPAD] = (qr * scale).astype(BF16)


def _odd_proj(x, cos_t, sin_t, w_in, gq, gkv, wuq, tm):
    rows = x.shape[0]
    outs = [(MLA_HEADS * MLA_QPAD, BF16), (MLA_KV_LORA, F32), (MLA_ROPE, F32), (LANES, BF16)]
    return _row_call(_odd_proj_kernel, "odd_proj", rows, tm, [x, cos_t, sin_t], [w_in, gq, gkv, wuq], outs)


def _kv_expand_kernel(c_ref, w_ref, k_ref, v_ref):
    kv = _dot(c_ref[...].astype(BF16), w_ref[...])
    n = MLA_HEADS * MLA_NOPE
    k_ref[...] = kv[:, :n].astype(BF16)
    v_ref[...] = kv[:, n:].astype(BF16)


def _kv_expand(ckv, w_ukv, tm):
    rows = ckv.shape[0]
    return _row_call(_kv_expand_kernel, "kv_expand", rows, tm, [ckv], [w_ukv],
                     [(MLA_HEADS * MLA_NOPE, BF16), (MLA_HEADS * MLA_V, BF16)])


def _mla_absorbed_kernel(q_ref, cp_ref, rp_ref, cn_ref, rn_ref, wuk_ref, wuv_ref, o_ref, m_scr, acc_scr, *, tkp):
    tq = q_ref.shape[0]
    rows = MLA_HEADS * tq
    q = q_ref[...]
    qa = []
    for h in range(MLA_HEADS):
        c0 = h * MLA_QPAD
        q_abs = _dot_nt(q[:, c0:c0 + MLA_NOPE], wuk_ref[h])
        qa.append(jnp.concatenate([q_abs.astype(BF16), q[:, c0 + MLA_NOPE:c0 + MLA_NOPE + MLA_ROPE]], axis=1))
    qs = jnp.concatenate(qa, axis=0)
    m_scr[...] = jnp.full(m_scr.shape, NEG, F32)
    acc_scr[...] = jnp.zeros(acc_scr.shape, F32)

    def update(c, r):
        c = c.astype(BF16)
        s = _dot_nt(qs, jnp.concatenate([c, r.astype(BF16)], axis=1))
        m_prev = m_scr[...]
        m_new = jnp.maximum(m_prev, jnp.max(s, axis=1, keepdims=True))
        if s.shape[1] % LANES == 0:
            p = jnp.exp2(s - jnp.tile(m_new, (1, s.shape[1] // LANES)))
        else:
            p = jnp.exp2(s - m_new[:, :1])
        c1 = jnp.concatenate([c, jnp.ones((c.shape[0], LANES), BF16)], axis=1)
        acc_scr[...] = (jnp.tile(jnp.exp2(m_prev - m_new), (1, acc_scr.shape[1] // LANES)) * acc_scr[...]
                        + _dot(p.astype(BF16), c1))
        m_scr[...] = m_new

    def past_body(j, carry):
        rs = pl.ds(pl.multiple_of(j * tkp, tkp), tkp)
        update(cp_ref[rs, :], rp_ref[rs, :])
        return carry
    lax.fori_loop(0, cp_ref.shape[0] // tkp, past_body, 0)
    update(cn_ref[...], rn_ref[:, :MLA_ROPE])

    acc = acc_scr[...]
    lat = (acc[:, :MLA_KV_LORA] / jnp.tile(acc[:, MLA_KV_LORA:], (1, MLA_KV_LORA // LANES))).astype(BF16)
    for h in range(MLA_HEADS):
        o_ref[:, h * MLA_V:(h + 1) * MLA_V] = _dot(lat[h * tq:(h + 1) * tq, :],
                                                   wuv_ref[:, h * MLA_V:(h + 1) * MLA_V]).astype(o_ref.dtype)


def _mla_absorbed(q, ckv_past, kpe_past, ckv_new, kpe_new, wuk_t, wuv, n_seq, tq, tkp):
    p = ckv_past.shape[1]
    assert p % tkp == 0
    rows = MLA_HEADS * tq
    return pl.pallas_call(
        functools.partial(_mla_absorbed_kernel, tkp=tkp), name="mla_absorbed", grid=(n_seq,),
        in_specs=[pl.BlockSpec((tq, MLA_HEADS * MLA_QPAD), lambda b: (b, 0)),
                  pl.BlockSpec((None, p, MLA_KV_LORA), lambda b: (b, 0, 0)),
                  pl.BlockSpec((None, p, MLA_ROPE), lambda b: (b, 0, 0)),
                  pl.BlockSpec((tq, MLA_KV_LORA), lambda b: (b, 0)),
                  pl.BlockSpec((tq, LANES), lambda b: (b, 0)),
                  _full_spec(wuk_t), _full_spec(wuv)],
        out_specs=pl.BlockSpec((tq, MLA_HEADS * MLA_V), lambda b: (b, 0)),
        out_shape=jax.ShapeDtypeStruct((n_seq * tq, MLA_HEADS * MLA_V), BF16),
        scratch_shapes=[pltpu.VMEM((rows, LANES), F32), pltpu.VMEM((rows, MLA_KV_LORA + LANES), F32)],
        compiler_params=_params(("parallel",)))(q, ckv_past, kpe_past, ckv_new, kpe_new, wuk_t, wuv)


def _rope_tables(pos):
    half = MLA_ROPE // 2
    inv = ROPE_BASE ** (-jnp.arange(half, dtype=F32) / half)
    ang = pos.astype(F32)[:, None] * inv[None, :]
    cos, sin = jnp.cos(ang), jnp.sin(ang)
    z = jnp.zeros((pos.shape[0], LANES - MLA_ROPE), F32)
    return jnp.concatenate([cos, cos, z], axis=1), jnp.concatenate([-sin, sin, z], axis=1)


def _pad_rows(a, n):
    return jnp.pad(a, ((0, n - a.shape[0]),) + ((0, 0),) * (a.ndim - 1))


def kernel(x_prompt, x_sample, state_hgrn2, cache_fox_k, cache_fox_v, cache_fox_logf, cache_mla_ckv, cache_mla_kpe, meta_tokens, even_w_in, hg_lb_logits, hg_norm_g, fox_forget_bias, even_w_out, mla_w_in, mla_q_norm_g, mla_kv_norm_g, mla_w_uq, mla_w_uk, mla_w_uv, mla_w_out, ln_mix_g, ln_mix_b, ln_ffn_g, ln_ffn_b, router_w, router_bias, moe_w_gate, moe_w_up, moe_w_down):
    B, T, _ = x_prompt.shape
    Bs, Ts, _ = x_sample.shape
    P = cache_fox_k.shape[2]
    RM = B * T
    RS = Bs * Ts
    RSM = -(-(RS + N_META) // LANES) * LANES
    ME = slice(RS, RS + N_META)
    TM_MAIN, TM_MOE, TQ = 512, 1024, 512

    xm = x_prompt.reshape(RM, D_MODEL)
    xs = _pad_rows(jnp.concatenate([x_sample.reshape(RS, D_MODEL), meta_tokens.astype(F32)], axis=0), RSM)

    w_in0 = even_w_in[0]
    n_main = 7 * HG_W
    w_even = w_in0[:, :n_main].astype(BF16)
    w_even_f = jnp.pad(w_in0[:, n_main:], ((0, 0), (0, LANES - FOX_HEADS))).astype(BF16)
    fb_pad = jnp.pad(fox_forget_bias[0][None, :], ((0, 0), (0, LANES - FOX_HEADS)))
    g_hg = hg_norm_g[0].reshape(1, HG_W)
    w_out0 = even_w_out[0].astype(BF16)
    e_mat = ((jnp.arange(HG_SUB * HG_DK)[:, None] // HG_DK) == (jnp.arange(CHUNK)[None, :] % HG_SUB)).astype(BF16)

    w_odd = jnp.pad(mla_w_in[0], ((0, 0), (0, LANES - MLA_ROPE))).astype(BF16)
    gq = mla_q_norm_g[0][None, :]
    gkv = mla_kv_norm_g[0][None, :]
    wuq = mla_w_uq[0].reshape(MLA_Q_LORA, MLA_HEADS, MLA_NOPE + MLA_ROPE)
    wuq = jnp.pad(wuq, ((0, 0), (0, 0), (0, MLA_QPAD - MLA_NOPE - MLA_ROPE)))
    wuq = wuq.reshape(MLA_Q_LORA, MLA_HEADS * MLA_QPAD).astype(BF16)
    w_ukv = jnp.concatenate([mla_w_uk[0].reshape(MLA_KV_LORA, -1), mla_w_uv[0].reshape(MLA_KV_LORA, -1)],
                            axis=1).astype(BF16)
    w_out1 = mla_w_out[0].astype(BF16)

    rw = jnp.pad(router_w, ((0, 0), (0, LANES - N_EXPERTS))).astype(BF16)
    rb = jnp.pad(router_bias.astype(F32)[:, None], ((0, LANES - N_EXPERTS), (0, 0)))
    experts = (moe_w_gate, moe_w_up, moe_w_down)
    row2 = lambda a: a[None, :]

    def ffn(x, acts, w_out, l, tm_mix, tm_moe, routed):
        x1, x1b, route = _mix(x, acts, w_out, row2(ln_mix_g[l]), row2(ln_mix_b[l]), rw, rb, tm_mix)
        ln = (row2(ln_ffn_g[l]), row2(ln_ffn_b[l]))
        if routed:
            return _moe_routed(x1, x1b, route, *experts, l, *ln, tm_mix)
        return _moe(x1, route, *experts, l, *ln, tm_moe)

    pm = _even_proj(xm, w_even, w_even_f, hg_lb_logits, fb_pad, TM_MAIN, 0)
    ps = _even_proj(xs, w_even, w_even_f, hg_lb_logits, fb_pad, RSM, 0)
    names = ("hq", "lf", "hk", "hv", "hgate", "fq", "fk", "fv", "fk16", "fv16", "flf")
    pm = dict(zip(names, pm))
    ps = dict(zip(names, ps))

    hg_keys = ("hq", "lf", "hk", "hv", "hgate", "flf")
    meta_in = [_pad_rows(ps[n][ME], CHUNK) for n in hg_keys]
    zero_s = jnp.zeros((1, HG_HEADS, HG_DK, HG_DV), F32)
    zero_f = jnp.zeros((1, 1, LANES), F32)
    o_hg_meta, fc_meta, s_meta = _hgrn2(*meta_in, g_hg, e_mat, zero_s, zero_f, 1, CHUNK, 0, CHUNK)
    o_hg_meta, fc_meta = o_hg_meta[:N_META], fc_meta[:N_META]
    f_meta_end = fc_meta[N_META - 1:N_META][None]

    o_hg_m, fc_m, s_main = _hgrn2(*[pm[n] for n in hg_keys], g_hg, e_mat, s_meta, f_meta_end, B, T, 0, 256)

    logf_c = jnp.pad(jnp.transpose(cache_fox_logf[0], (0, 2, 1)), ((0, 0), (0, HG_SUB - FOX_HEADS), (0, 0)))
    fpast = _cumsum_lanes(logf_c.reshape(Bs * HG_SUB, P), 512).reshape(Bs, HG_SUB, P)[:, :FOX_HEADS, :]
    f0_s = jnp.pad(fpast[:, :, P - 1][:, None, :], ((0, 0), (0, 0), (0, LANES - FOX_HEADS)))
    o_hg_s, fc_s, s_samp = _hgrn2(*[ps[n] for n in hg_keys], g_hg, e_mat, state_hgrn2[0], f0_s, Bs, Ts, 0, CHUNK)

    def bias_layouts(fc, n_seq, seq_len):
        f4 = fc[:, :FOX_HEADS].T
        return f4[:, :, None], f4.reshape(FOX_HEADS, n_seq, 1, seq_len)

    fq_m, fk_m = bias_layouts(fc_m, B, T)
    fq_s, fk_s = bias_layouts(fc_s, Bs, Ts)
    fq_t, fk_t = bias_layouts(fc_meta, 1, N_META)

    fox_kw = dict(n_heads=FOX_HEADS, dq=FOX_DH, dk=FOX_DH, dv=FOX_DH, mask_mode="causal")
    meta_past = dict(k=ps["fk16"][ME][None], v=ps["fv16"][ME][None],
                     fk=jnp.transpose(fk_t, (1, 0, 2, 3)), tk=N_META)
    o_fox_m = _flash_tri(pm["fq"], pm["fk16"], pm["fv16"], n_seq=B, seq_len=T, tq=TQ,
                         fq=fq_m, fkn=fk_m, past=meta_past, **fox_kw)
    samp_past = dict(k=cache_fox_k[0], v=cache_fox_v[0],
                     fk=fpast[:, :, None, :], tk=1024)
    o_fox_s = _flash(ps["fq"], ps["fk16"], ps["fv16"], n_seq=Bs, seq_len=Ts, tq=Ts, q_off=0, k_off=0,
                     fq=fq_s, fkn=fk_s, past=samp_past, **fox_kw)
    o_fox_t = _flash(ps["fq"][ME], ps["fk16"][ME], ps["fv16"][ME], n_seq=1, seq_len=N_META, tq=N_META,
                     q_off=0, k_off=0, fq=fq_t, fkn=fk_t, **fox_kw)

    o_hg_small = _pad_rows(jnp.concatenate([o_hg_s, o_hg_meta], axis=0), RSM)
    o_fox_small = _pad_rows(jnp.concatenate([o_fox_s, o_fox_t], axis=0), RSM)
    xm = ffn(xm, [o_hg_m, o_fox_m], w_out0, 0, TM_MOE, TM_MOE, True)
    xs = ffn(xs, [o_hg_small, o_fox_small], w_out0, 0, RSM, RSM, False)

    cos_m, sin_m = _rope_tables(N_META + jnp.arange(T, dtype=jnp.int32))
    pos_small = _pad_rows(jnp.concatenate([jnp.tile(P + jnp.arange(Ts, dtype=jnp.int32), Bs),
                                           jnp.arange(N_META, dtype=jnp.int32)]), RSM)
    cos_s, sin_s = _rope_tables(pos_small)
    qm, ckv_m, kpe_m, kpe16_m = _odd_proj(xm, jnp.tile(cos_m, (B, 1)), jnp.tile(sin_m, (B, 1)),
                                          w_odd, gq, gkv, wuq, TM_MAIN)
    qs, ckv_s, kpe_s, kpe16_s = _odd_proj(xs, cos_s, sin_s, w_odd, gq, gkv, wuq, RSM)
    kn_m, vn_m = _kv_expand(ckv_m, w_ukv, 1024)
    kn_s, vn_s = _kv_expand(ckv_s, w_ukv, RSM)
    wuk_t = jnp.transpose(mla_w_uk[0], (1, 0, 2)).astype(BF16)
    wuv = mla_w_uv[0].reshape(MLA_KV_LORA, MLA_HEADS * MLA_V).astype(BF16)

    mla_kw = dict(n_heads=MLA_HEADS, dq=MLA_QPAD, dk=MLA_NOPE, dv=MLA_V)
    meta_past = dict(k=kn_s[ME][None], v=vn_s[ME][None], r=kpe16_s[ME][None], tk=N_META)
    o_m = _flash_tri(qm, kn_m, vn_m, n_seq=B, seq_len=T, tq=TQ, rn=kpe16_m,
                     past=meta_past, mask_mode="chunk", **mla_kw)
    assert P % CHUNK == 0 and Ts <= CHUNK
    o_s = _mla_absorbed(qs, cache_mla_ckv[0], cache_mla_kpe[0], ckv_s, kpe16_s, wuk_t, wuv, Bs, Ts, 1024)
    o_t = _flash(qs[ME], kn_s[ME], vn_s[ME], n_seq=1, seq_len=N_META, tq=N_META, q_off=0, k_off=0,
                 rn=kpe16_s[ME], mask_mode="full", **mla_kw)
    xm = ffn(xm, [o_m], w_out1, 1, TM_MOE, TM_MOE, True)
    xs = ffn(xs, [_pad_rows(jnp.concatenate([o_s, o_t], axis=0), RSM)], w_out1, 1, RSM, RSM, False)

    def with_meta(main, small, *width):
        meta = jnp.broadcast_to(small[ME][None], (B, N_META) + width)
        return jnp.concatenate([meta, main.reshape((B, T) + width)], axis=1)

    y_prompt = xm.reshape(B, T, D_MODEL)
    y_sample = xs[:RS].reshape(Bs, Ts, D_MODEL)
    hg_p = s_main[None]
    fk_p = with_meta(pm["fk"], ps["fk"], FOX_HEADS, FOX_DH)[None]
    fv_p = with_meta(pm["fv"], ps["fv"], FOX_HEADS, FOX_DH)[None]
    flf_p = with_meta(pm["flf"][:, :FOX_HEADS], ps["flf"][:, :FOX_HEADS], FOX_HEADS)[None]
    ckv_p = with_meta(ckv_m, ckv_s, MLA_KV_LORA)[None]
    kpe_p = with_meta(kpe_m, kpe_s, MLA_ROPE)[None]
    hg_s = s_samp[None]
    fk_s_out = ps["fk"][:RS].reshape(1, Bs, Ts, FOX_HEADS, FOX_DH)
    fv_s_out = ps["fv"][:RS].reshape(1, Bs, Ts, FOX_HEADS, FOX_DH)
    flf_s = ps["flf"][:RS, :FOX_HEADS].reshape(1, Bs, Ts, FOX_HEADS)
    ckv_so = ckv_s[:RS].reshape(1, Bs, Ts, MLA_KV_LORA)
    kpe_so = kpe_s[:RS].reshape(1, Bs, Ts, MLA_ROPE)
    return (y_prompt, y_sample, hg_p, fk_p, fv_p, flf_p, ckv_p, kpe_p,
            hg_s, fk_s_out, fv_s_out, flf_s, ckv_so, kpe_so)
```

```python
import functools

import jax
import jax.numpy as jnp
from jax import lax
from jax.experimental import pallas as pl
from jax.experimental.pallas import tpu as pltpu
from jax.experimental.pallas import tpu_sc as plsc

D_MODEL = 1024
CHUNK = 64
N_META = 16
HG_HEADS = 4
HG_DK = 128
HG_DV = 128
HG_W = HG_HEADS * HG_DK
FOX_HEADS = 4
FOX_DH = 128
FOX_W = FOX_HEADS * FOX_DH
MLA_HEADS = 8
MLA_Q_LORA = 512
MLA_KV_LORA = 256
MLA_NOPE = 128
MLA_ROPE = 64
MLA_V = 128
MLA_QPAD = 256
ROPE_BASE = 10000.0
N_EXPERTS = 16
N_GROUPS = 4
EXPERTS_PER_GROUP = 4
D_EXPERT = 256
DEPTH = 2
ALPHA = (2 * DEPTH) ** 0.25
LN_EPS = 1e-5
RMS_EPS = 1e-6

LANES = 128
HG_SUB = 8
HG_GROUP = 4
NEG = -1e30
LOG2E = 1.4426950408889634
F32 = jnp.float32
BF16 = jnp.bfloat16
VMEM_LIMIT = 56 * 1024 * 1024


def _dot(a, b):
    return jnp.dot(a, b, preferred_element_type=F32)


def _dot_nt(a, b):
    return lax.dot_general(a, b, (((1,), (1,)), ((), ())), preferred_element_type=F32)


def _dot_tn(a, b):
    return lax.dot_general(a, b, (((0,), (0,)), ((), ())), preferred_element_type=F32)


def _split3(x):
    hi = x.astype(BF16)
    r = x - hi.astype(F32)
    mid = r.astype(BF16)
    lo = (r - mid.astype(F32)).astype(BF16)
    return hi, mid, lo


def _cumsum_rows(tri, x):
    hi, mid, lo = _split3(x)
    return _dot(tri, hi) + _dot(tri, mid) + _dot(tri, lo)


def _sigmoid(x):
    return 1.0 / (1.0 + jnp.exp(-x))


def _log_sigmoid(x):
    return jnp.minimum(x, 0.0) - jnp.log(1.0 + jnp.exp(-jnp.abs(x)))


def _layer_norm(x, g, b):
    mu = jnp.mean(x, axis=-1, keepdims=True)
    xc = x - mu
    var = jnp.mean(xc * xc, axis=-1, keepdims=True)
    return xc * lax.rsqrt(var + LN_EPS) * g + b


def _rms_norm(x, g):
    return x * lax.rsqrt(jnp.mean(x * x, axis=-1, keepdims=True) + RMS_EPS) * g


def _params(sem):
    return pltpu.CompilerParams(dimension_semantics=sem, vmem_limit_bytes=VMEM_LIMIT)


def _full_spec(a):
    nd = a.ndim
    return pl.BlockSpec(a.shape, lambda *_: (0,) * nd)


def _row_call(kernel, name, rows, tm, row_ins, full_ins, outs, scratch=()):
    assert rows % tm == 0
    in_specs = [pl.BlockSpec((tm, a.shape[1]), lambda i: (i, 0)) for a in row_ins]
    in_specs += [_full_spec(a) for a in full_ins]
    trail = [c if isinstance(c, tuple) else (c,) for c, _ in outs]
    out_specs = [pl.BlockSpec((tm,) + t, lambda i, n=len(t): (i,) + (0,) * n) for t in trail]
    out_shape = [jax.ShapeDtypeStruct((rows,) + t, dt) for t, (_, dt) in zip(trail, outs)]
    return pl.pallas_call(
        kernel, name=name, grid=(rows // tm,), in_specs=in_specs, out_specs=out_specs,
        out_shape=out_shape, scratch_shapes=list(scratch),
        compiler_params=_params(("parallel",)))(*row_ins, *full_ins)


def _even_proj_kernel(x_ref, w_ref, wf_ref, lbl_ref, fb_ref,
                      hq_ref, lf_ref, hk_ref, hv_ref, hgate_ref,
                      fq_ref, fk_ref, fv_ref, fk16_ref, fv16_ref, flf_ref, *, layer):
    xb = x_ref[...].astype(BF16)

    def blk(j):
        return _dot(xb, w_ref[:, j * HG_W:(j + 1) * HG_W])

    logits = lbl_ref[...]
    e = jnp.exp(logits - jnp.max(logits, axis=0, keepdims=True))
    lb = jnp.sum(e[:layer + 1], axis=0, keepdims=True) / jnp.sum(e, axis=0, keepdims=True)

    hq_ref[...] = blk(0).astype(BF16)
    zf = blk(1)
    lf_ref[...] = jnp.log(lb + (1.0 - lb) * _sigmoid(zf))
    hk_ref[...] = ((1.0 - lb) * _sigmoid(-zf)).astype(BF16)
    hv_ref[...] = blk(2).astype(BF16)
    hgate_ref[...] = _sigmoid(blk(3)).astype(BF16)
    fq_ref[...] = (blk(4) * (FOX_DH ** -0.5 * LOG2E)).astype(BF16)
    fk = blk(5)
    fk16_ref[...] = fk.astype(BF16)
    fv = blk(6)
    fv16_ref[...] = fv.astype(BF16)
    for h in range(FOX_HEADS):
        fk_ref[:, h, :] = fk[:, h * FOX_DH:(h + 1) * FOX_DH]
        fv_ref[:, h, :] = fv[:, h * FOX_DH:(h + 1) * FOX_DH]
    flf_ref[...] = _log_sigmoid(_dot(xb, wf_ref[...]) + fb_ref[...])


def _even_proj(x, w_main, w_f, lb_logits, fb_pad, tm, layer):
    rows = x.shape[0]
    outs = [(HG_W, BF16), (HG_W, F32), (HG_W, BF16), (HG_W, BF16), (HG_W, BF16),
            (FOX_W, BF16), ((FOX_HEADS, FOX_DH), F32), ((FOX_HEADS, FOX_DH), F32), (FOX_W, BF16), (FOX_W, BF16),
            (LANES, F32)]
    return _row_call(functools.partial(_even_proj_kernel, layer=layer), "even_proj", rows, tm,
                     [x], [w_main, w_f, lb_logits, fb_pad], outs)


def _bcast_sub(x, j):
    n, c = x.shape
    x3 = x.reshape(n // HG_SUB, HG_SUB, c)
    return jnp.broadcast_to(x3[:, j:j + 1, :], x3.shape).reshape(n, c)


def _level_ref(b, w):
    n, c = b.shape
    parts = [jnp.broadcast_to(b[m * 2 * w + w - 1:m * 2 * w + w, :], (2 * w, c)) for m in range(n // (2 * w))]
    return parts[0] if len(parts) == 1 else jnp.concatenate(parts, axis=0)


def _hgrn2_kernel(q_ref, lf_ref, k_ref, v_ref, gate_ref, flf_ref, g_ref, e_ref, s0_ref, f0_ref,
                  o_ref, fcum_ref, sout_ref, st_scr, fc_scr, *, n_chunks):
    i = pl.program_id(1)
    C = CHUNK

    @pl.when(i == 0)
    def _():
        for h in range(HG_HEADS):
            st_scr[h] = s0_ref[h].T
        fc_scr[...] = f0_ref[...]

    row = lax.broadcasted_iota(jnp.int32, (C, 1), 0)
    col = lax.broadcasted_iota(jnp.int32, (1, C), 1)
    tri = (col <= row).astype(BF16)
    same = lambda w: (row // w) == (col // w)
    levels = (32, 16, 8)

    fc = fc_scr[...]
    for c in range(n_chunks):
        sl = slice(c * C, (c + 1) * C)
        fcum = _cumsum_rows(tri, flf_ref[sl, :]) + fc
        fcum_ref[sl, :] = fcum
        fc = fcum[C - 1:C, :]
    fc_scr[...] = fc

    staged = []
    for c in range(n_chunks):
        sl = slice(c * C, (c + 1) * C)
        per_head = []
        for h0 in range(0, HG_HEADS, HG_GROUP):
            gs = slice(h0 * HG_DK, (h0 + HG_GROUP) * HG_DK)
            b = _cumsum_rows(tri, lf_ref[sl, gs]) * LOG2E
            q = q_ref[sl, gs].astype(F32)
            k = k_ref[sl, gs].astype(F32)
            v = v_ref[sl, gs]
            qb = (q * jnp.exp2(b)).astype(BF16)
            b_last = b[C - 1:C, :]
            kd = (k * jnp.exp2(b_last - b)).astype(BF16)
            e_last = jnp.exp2(b_last)

            pjs = [(jnp.exp2(jnp.where((row % HG_SUB) >= j, b - _bcast_sub(b, j), NEG)) * q
                    * _bcast_sub(k, j)).astype(BF16) for j in range(HG_SUB)]
            lv = []
            for w in levels:
                upper = (row % (2 * w)) >= w
                ew = jnp.exp2(-jnp.abs(b - _level_ref(b, w)))
                lv.append((jnp.where(upper, q * ew, 0.0).astype(BF16), jnp.where(upper, 0.0, k * ew).astype(BF16)))

            for hh in range(HG_GROUP):
                hs = slice(hh * HG_DK, (hh + 1) * HG_DK)
                a = jnp.where(same(HG_SUB), _dot(jnp.concatenate([p[:, hs] for p in pjs], axis=1), e_ref[...]), 0.0)
                for w, (qw, kw) in zip(levels, lv):
                    aw = _dot_nt(qw[:, hs], kw[:, hs])
                    a = a + (aw if 2 * w == C else jnp.where(same(2 * w), aw, 0.0))
                vh = v[:, hs]
                per_head.append((_dot(a.astype(BF16), vh), qb[:, hs], kd[:, hs], e_last[:, hs], vh))
        staged.append(per_head)

    st = [st_scr[h] for h in range(HG_HEADS)]
    for c in range(n_chunks):
        sl = slice(c * C, (c + 1) * C)
        for h, (o_intra, qb_h, kd_h, e_h, vh) in enumerate(staged[c]):
            ho = slice(h * HG_DK, (h + 1) * HG_DK)
            o = o_intra + _dot_nt(qb_h, st[h].astype(BF16))
            st[h] = st[h] * e_h + _dot_tn(vh, kd_h)
            o = _rms_norm(o, g_ref[:, ho])
            o_ref[sl, ho] = (o * gate_ref[sl, ho].astype(F32)).astype(BF16)
    for h in range(HG_HEADS):
        st_scr[h] = st[h]

    @pl.when(i == pl.num_programs(1) - 1)
    def _():
        for h in range(HG_HEADS):
            sout_ref[h] = st_scr[h].T


def _hgrn2(q, lf, k, v, gate, flf, g, e_mat, s0, f0, n_seq, seq_len, row_off, tb):
    assert seq_len % tb == 0 and tb % CHUNK == 0 and row_off % tb == 0
    nb = seq_len // tb
    off = row_off // tb
    per_seq = s0.shape[0] > 1
    rmap = lambda s, i: (off + s * nb + i, 0)
    omap = lambda s, i: (s * nb + i, 0)
    smap = (lambda s, i: (s, 0, 0, 0)) if per_seq else (lambda s, i: (0, 0, 0, 0))
    fmap = (lambda s, i: (s, 0, 0)) if per_seq else (lambda s, i: (0, 0, 0))
    in_specs = [pl.BlockSpec((tb, HG_W), rmap) for _ in range(5)]
    in_specs += [pl.BlockSpec((tb, LANES), rmap), _full_spec(g), _full_spec(e_mat),
                 pl.BlockSpec((None, HG_HEADS, HG_DK, HG_DV), smap), pl.BlockSpec((None, 1, LANES), fmap)]
    out_specs = [pl.BlockSpec((tb, HG_W), omap), pl.BlockSpec((tb, LANES), omap),
                 pl.BlockSpec((None, HG_HEADS, HG_DK, HG_DV), lambda s, i: (s, 0, 0, 0))]
    out_shape = [jax.ShapeDtypeStruct((n_seq * seq_len, HG_W), BF16),
                 jax.ShapeDtypeStruct((n_seq * seq_len, LANES), F32),
                 jax.ShapeDtypeStruct((n_seq, HG_HEADS, HG_DK, HG_DV), F32)]
    scratch = [pltpu.VMEM((HG_HEADS, HG_DV, HG_DK), F32), pltpu.VMEM((1, LANES), F32)]
    return pl.pallas_call(
        functools.partial(_hgrn2_kernel, n_chunks=tb // CHUNK), name="hgrn2",
        grid=(n_seq, nb), in_specs=in_specs, out_specs=out_specs, out_shape=out_shape,
        scratch_shapes=scratch, compiler_params=_params(("parallel", "arbitrary")))(
            q, lf, k, v, gate, flf, g, e_mat, s0, f0)


def _cumsum_kernel(x_ref, tri_ref, o_ref, carry):
    @pl.when(pl.program_id(0) == 0)
    def _():
        carry[...] = jnp.zeros_like(carry)

    hi, mid, lo = _split3(x_ref[...])
    tri = tri_ref[...]
    out = _dot(hi, tri) + _dot(mid, tri) + _dot(lo, tri) + carry[...]
    o_ref[...] = out
    carry[...] = out[:, out.shape[1] - 1:]


def _cumsum_lanes(x, tb):
    r, seq_len = x.shape
    tri = (jnp.arange(tb)[:, None] <= jnp.arange(tb)[None, :]).astype(BF16)
    return pl.pallas_call(
        _cumsum_kernel, name="cumsum", grid=(seq_len // tb,),
        in_specs=[pl.BlockSpec((r, tb), lambda i: (0, i)), _full_spec(tri)],
        out_specs=pl.BlockSpec((r, tb), lambda i: (0, i)),
        out_shape=jax.ShapeDtypeStruct(x.shape, F32),
        scratch_shapes=[pltpu.VMEM((r, 1), F32)],
        compiler_params=_params(("arbitrary",)))(x, tri)


def _flash_kernel(*refs, n_past_blk, tkp, tq, has_bias, has_rope, mask_mode, has_past, past_heads):
    it = iter(refs)
    q_ref = next(it)
    fq_ref = next(it) if has_bias else None
    if has_past:
        kp_ref, vp_ref = next(it), next(it)
        rp_ref = next(it) if has_rope else None
        fkp_ref = next(it) if has_bias else None
    kn_ref, vn_ref = next(it), next(it)
    rn_ref = next(it) if has_rope else None
    fkn_ref = next(it) if has_bias else None
    o_ref = next(it)
    m_scr, acc_scr = next(it), next(it)
    dv = o_ref.shape[1]

    q = q_ref[...]
    m_scr[...] = jnp.full(m_scr.shape, NEG, F32)
    acc_scr[...] = jnp.zeros(acc_scr.shape, F32)
    fq_b = jnp.broadcast_to(fq_ref[...] * LOG2E, (tq, LANES)) if has_bias else None

    def scores(k, r, fk):
        if has_rope:
            k = jnp.concatenate([k, r], axis=1)
        s = _dot_nt(q, k.astype(BF16))
        if has_bias:
            s = s + jnp.tile(fq_b, (1, s.shape[1] // LANES)) if s.shape[1] % LANES == 0 else s + fq_b[:, :1]
            s = s - fk * LOG2E
        return s

    def update(s, v, mask):
        if mask is not None:
            s = jnp.where(mask, s, NEG)
        m_prev = m_scr[...]
        m_new = jnp.maximum(m_prev, jnp.max(s, axis=1, keepdims=True))
        alpha = jnp.exp2(m_prev - m_new)
        if s.shape[1] % LANES == 0:
            p = jnp.exp2(s - jnp.tile(m_new, (1, s.shape[1] // LANES)))
        else:
            p = jnp.exp2(s - m_new[:, :1])
        v1 = jnp.concatenate([v.astype(BF16), jnp.ones((v.shape[0], LANES), BF16)], axis=1)
        acc_scr[...] = jnp.tile(alpha, (1, acc_scr.shape[1] // LANES)) * acc_scr[...] + _dot(p.astype(BF16), v1)
        m_scr[...] = m_new

    past_kv = (lambda ref, rs: ref[rs, pl.program_id(1), :]) if past_heads else (lambda ref, rs: ref[rs, :])

    def past_block(rs):
        return (past_kv(kp_ref, rs), rp_ref[rs, :] if has_rope else None, fkp_ref[:, rs] if has_bias else None)

    def new_block(rs):
        return (kn_ref[rs, :], rn_ref[rs, :] if has_rope else None, fkn_ref[:, rs] if has_bias else None)

    if has_past:
        if n_past_blk == 1:
            update(scores(*past_block(slice(None))), past_kv(vp_ref, slice(None)), None)
        else:
            def past_body(j, carry):
                rs = pl.ds(pl.multiple_of(j * tkp, tkp), tkp)
                update(scores(*past_block(rs)), past_kv(vp_ref, rs), None)
                return carry
            lax.fori_loop(0, n_past_blk, past_body, 0)

    row = lax.broadcasted_iota(jnp.int32, (tq, 1), 0)
    col = lax.broadcasted_iota(jnp.int32, (1, tq), 1)
    if mask_mode == "causal":
        mask = col <= row
    elif mask_mode == "chunk":
        mask = (col // CHUNK) <= (row // CHUNK)
    else:
        mask = None

    update(scores(*new_block(slice(None))), vn_ref[...], mask)
    acc = acc_scr[...]
    o_ref[...] = (acc[:, :dv] / acc[:, dv:]).astype(o_ref.dtype)


def _flash(q, kn, vn, *, n_seq, n_heads, seq_len, tq, dq, dk, dv, q_off, k_off, mask_mode,
           fq=None, fkn=None, rn=None, past=None):
    assert seq_len % tq == 0 and q_off % tq == 0 and k_off % seq_len == 0
    nq = seq_len // tq
    qo = q_off // tq
    ko = k_off // seq_len
    has_bias = fq is not None
    has_rope = rn is not None
    has_past = past is not None
    ins, specs = [q], [pl.BlockSpec((tq, dq), lambda b, h, i: (qo + b * nq + i, h))]
    if has_bias:
        ins.append(fq)
        specs.append(pl.BlockSpec((None, tq, 1), lambda b, h, i: (h, qo + b * nq + i, 0)))
    n_past_blk, tkp, past_heads = 0, 0, False
    if has_past:
        tp = past["k"].shape[1]
        tkp = past["tk"]
        assert tp % tkp == 0
        n_past_blk = tp // tkp
        pb = (lambda b: b) if past["k"].shape[0] > 1 else (lambda b: 0)
        ins += [past["k"], past["v"]]
        past_heads = past["k"].ndim == 4
        if past_heads:
            specs += [pl.BlockSpec((None, tp, n_heads, dk), lambda b, h, i: (pb(b), 0, 0, 0)),
                      pl.BlockSpec((None, tp, n_heads, dv), lambda b, h, i: (pb(b), 0, 0, 0))]
        else:
            specs += [pl.BlockSpec((None, tp, dk), lambda b, h, i: (pb(b), 0, h)),
                      pl.BlockSpec((None, tp, dv), lambda b, h, i: (pb(b), 0, h))]
        if has_rope:
            ins.append(past["r"])
            specs.append(pl.BlockSpec((None, tp, LANES), lambda b, h, i: (pb(b), 0, 0)))
        if has_bias:
            ins.append(past["fk"])
            specs.append(pl.BlockSpec((None, None, 1, tp), lambda b, h, i: (pb(b), h, 0, 0)))
    ins += [kn, vn]
    specs += [pl.BlockSpec((seq_len, dk), lambda b, h, i: (ko + b, h)),
              pl.BlockSpec((seq_len, dv), lambda b, h, i: (ko + b, h))]
    if has_rope:
        ins.append(rn)
        specs.append(pl.BlockSpec((seq_len, LANES), lambda b, h, i: (ko + b, 0)))
    if has_bias:
        ins.append(fkn)
        specs.append(pl.BlockSpec((None, None, 1, seq_len), lambda b, h, i: (h, ko + b, 0, 0)))
    assert nq == 1
    kern = functools.partial(_flash_kernel, n_past_blk=n_past_blk, tkp=tkp, tq=tq, has_bias=has_bias,
                             has_rope=has_rope, mask_mode=mask_mode, has_past=has_past, past_heads=past_heads)
    return pl.pallas_call(
        kern, name="flash", grid=(n_seq, n_heads, nq), in_specs=specs,
        out_specs=pl.BlockSpec((tq, dv), lambda b, h, i: (b * nq + i, h)),
        out_shape=jax.ShapeDtypeStruct((n_seq * seq_len, n_heads * dv), BF16),
        scratch_shapes=[pltpu.VMEM((tq, LANES), F32), pltpu.VMEM((tq, dv + LANES), F32)],
        compiler_params=_params(("parallel", "parallel", "arbitrary")))(*ins)


FLASH_UNROLL_OFF = 14
FLASH_UNROLL_DIAG = 8


def _tri_tables(nq):
    pairs = [(qi, kj) for qi in range(nq) for kj in range(qi)] + [(qi, qi) for qi in range(nq)] + [(0, 0)]
    return (jnp.array([p[0] for p in pairs], jnp.int32), jnp.array([p[1] for p in pairs], jnp.int32))


def _flash_tri_kernel(qt_ref, kt_ref, *refs, tq, nq, has_bias, has_rope, mask_mode):
    it = iter(refs)
    q_ref = next(it)
    fq_ref = next(it) if has_bias else None
    kp_ref, vp_ref = next(it), next(it)
    rp_ref = next(it) if has_rope else None
    fkp_ref = next(it) if has_bias else None
    kn_ref, vn_ref = next(it), next(it)
    rn_ref = next(it) if has_rope else None
    fkn_ref = next(it) if has_bias else None
    o_ref = next(it)
    m_scr, acc_scr, sa_scr, sb_scr = next(it), next(it), next(it), next(it)
    fqb_scr = next(it) if has_bias else None
    dv = o_ref.shape[1]
    n_off = nq * (nq - 1) // 2
    tile = lambda j: pl.ds(pl.multiple_of(j * tq, tq), tq)
    ones = jnp.ones((tq, LANES), BF16)

    kp = kp_ref[...]
    if has_rope:
        kp = jnp.concatenate([kp, rp_ref[...]], axis=1)
    vp1 = jnp.concatenate([vp_ref[...], ones[:vp_ref.shape[0]]], axis=1)
    for i in range(nq):
        rs = slice(i * tq, (i + 1) * tq)
        s = _dot_nt(q_ref[rs, :], kp)
        if has_bias:
            fb = fq_ref[rs, :] * LOG2E
            fqb_scr[rs, :] = jnp.broadcast_to(fb, (tq, LANES))
            s = s + fb - fkp_ref[...] * LOG2E
        m0 = jnp.max(s, axis=1, keepdims=True)
        m_scr[i] = jnp.broadcast_to(m0, (tq, LANES))
        acc_scr[i] = _dot(jnp.exp2(s - m0).astype(BF16), vp1)

    def fill(s_ref, t):
        qs, ks = tile(qt_ref[t]), tile(kt_ref[t])
        k = kn_ref[ks, :]
        if has_rope:
            k = jnp.concatenate([k, rn_ref[ks, :]], axis=1)
        s = _dot_nt(q_ref[qs, :], k)
        if has_bias:
            s = s + jnp.tile(fqb_scr[qs, :], (1, tq // LANES)) - fkn_ref[:, ks] * LOG2E
        s_ref[...] = s

    def drain(s_ref, t, mask):
        qi = qt_ref[t]
        s = s_ref[...]
        if mask is not None:
            s = jnp.where(mask, s, NEG)
        m_prev = m_scr[qi]
        m_new = jnp.maximum(m_prev, jnp.max(s, axis=1, keepdims=True))
        p = jnp.exp2(s - jnp.tile(m_new, (1, tq // LANES)))
        v1 = jnp.concatenate([vn_ref[tile(kt_ref[t]), :], ones], axis=1)
        acc = jnp.tile(jnp.exp2(m_prev - m_new), (1, (dv + LANES) // LANES)) * acc_scr[qi] + _dot(p.astype(BF16), v1)
        return qi, m_new, acc

    def keep(s_ref, t):
        qi, m_new, acc = drain(s_ref, t, None)
        m_scr[qi] = m_new
        acc_scr[qi] = acc

    row = lax.broadcasted_iota(jnp.int32, (tq, 1), 0)
    col = lax.broadcasted_iota(jnp.int32, (1, tq), 1)
    mask = {"causal": col <= row, "chunk": (col // CHUNK) <= (row // CHUNK)}[mask_mode]

    def finish(s_ref, t):
        qi, _, acc = drain(s_ref, t, mask)
        o_ref[tile(qi), :] = (acc[:, :dv] / acc[:, dv:]).astype(o_ref.dtype)

    def pipeline(t0, n, unroll, consume):
        assert n % unroll == 0 and unroll % 2 == 0

        def body(i, carry):
            t = t0 + unroll * i
            for u in range(0, unroll, 2):
                fill(sb_scr, t + u + 1)
                consume(sa_scr, t + u)
                fill(sa_scr, t + u + 2)
                consume(sb_scr, t + u + 1)
            return carry
        lax.fori_loop(0, n // unroll, body, 0)

    fill(sa_scr, 0)
    pipeline(0, n_off, FLASH_UNROLL_OFF, keep)
    pipeline(n_off, nq, FLASH_UNROLL_DIAG, finish)


def _flash_tri(q, kn, vn, *, n_seq, n_heads, seq_len, tq, dq, dk, dv, mask_mode, past, fq=None, fkn=None, rn=None):
    nq = seq_len // tq
    has_bias = fq is not None
    has_rope = rn is not None
    tp = past["k"].shape[1]
    m3 = lambda f: (lambda b, h, qt, kt: f(b, h))
    ins, specs = [q], [pl.BlockSpec((seq_len, dq), m3(lambda b, h: (b, h)))]
    if has_bias:
        ins.append(fq)
        specs.append(pl.BlockSpec((None, seq_len, 1), m3(lambda b, h: (h, b, 0))))
    ins += [past["k"], past["v"]]
    specs += [pl.BlockSpec((None, tp, dk), m3(lambda b, h: (0, 0, h))),
              pl.BlockSpec((None, tp, dv), m3(lambda b, h: (0, 0, h)))]
    if has_rope:
        ins.append(past["r"])
        specs.append(pl.BlockSpec((None, tp, LANES), m3(lambda b, h: (0, 0, 0))))
    if has_bias:
        ins.append(past["fk"])
        specs.append(pl.BlockSpec((None, None, 1, tp), m3(lambda b, h: (0, h, 0, 0))))
    ins += [kn, vn]
    specs += [pl.BlockSpec((seq_len, dk), m3(lambda b, h: (b, h))),
              pl.BlockSpec((seq_len, dv), m3(lambda b, h: (b, h)))]
    if has_rope:
        ins.append(rn)
        specs.append(pl.BlockSpec((seq_len, LANES), m3(lambda b, h: (b, 0))))
    if has_bias:
        ins.append(fkn)
        specs.append(pl.BlockSpec((None, None, 1, seq_len), m3(lambda b, h: (h, b, 0, 0))))
    scratch = [pltpu.VMEM((nq, tq, LANES), F32), pltpu.VMEM((nq, tq, dv + LANES), F32),
               pltpu.VMEM((tq, tq), F32), pltpu.VMEM((tq, tq), F32)]
    if has_bias:
        scratch.append(pltpu.VMEM((seq_len, LANES), F32))
    grid_spec = pltpu.PrefetchScalarGridSpec(
        num_scalar_prefetch=2, grid=(n_seq, n_heads), in_specs=specs,
        out_specs=pl.BlockSpec((seq_len, dv), m3(lambda b, h: (b, h))), scratch_shapes=scratch)
    kern = functools.partial(_flash_tri_kernel, tq=tq, nq=nq, has_bias=has_bias, has_rope=has_rope,
                             mask_mode=mask_mode)
    return pl.pallas_call(
        kern, name="flash_tri", grid_spec=grid_spec,
        out_shape=jax.ShapeDtypeStruct((n_seq * seq_len, n_heads * dv), BF16),
        compiler_params=_params(("parallel", "arbitrary")))(*_tri_tables(nq), *ins)


def _route(sc, sb):
    def top2_sum(v):
        a, b, c, d = v
        a, b = jnp.maximum(a, b), jnp.minimum(a, b)
        c, d = jnp.maximum(c, d), jnp.minimum(c, d)
        hi, lo2 = jnp.maximum(a, c), jnp.minimum(a, c)
        return hi + jnp.maximum(lo2, jnp.maximum(b, d))

    gs = [top2_sum(sb[g * EXPERTS_PER_GROUP:(g + 1) * EXPERTS_PER_GROUP]) for g in range(N_GROUPS)]
    best_v, best_g = gs[0], jnp.zeros(gs[0].shape, jnp.int32)
    for g in range(1, N_GROUPS):
        upd = gs[g] > best_v
        best_v = jnp.where(upd, gs[g], best_v)
        best_g = jnp.where(upd, g, best_g)
    masked = [jnp.where(best_g == (e // EXPERTS_PER_GROUP), sb[e], -jnp.inf) for e in range(N_EXPERTS)]

    def argmax_first(vals, exclude=None):
        bv = jnp.full(vals[0].shape, -jnp.inf, F32)
        bi = jnp.full(vals[0].shape, -1, jnp.int32)
        for e, v in enumerate(vals):
            upd = v > bv
            if exclude is not None:
                upd = upd & (exclude != e)
            bv = jnp.where(upd, v, bv)
            bi = jnp.where(upd, e, bi)
        return bi

    i1 = argmax_first(masked)
    i2 = argmax_first(masked, exclude=i1)
    w1 = sum(jnp.where(i1 == e, sc[e], 0.0) for e in range(N_EXPERTS))
    w2 = sum(jnp.where(i2 == e, sc[e], 0.0) for e in range(N_EXPERTS))
    tot = w1 + w2
    w1, w2 = w1 / tot, w2 / tot
    comb = [jnp.where(i1 == e, w1, 0.0) + jnp.where(i2 == e, w2, 0.0) for e in range(N_EXPERTS)]
    return comb + [i1.astype(F32), i2.astype(F32), w1, w2]


def _mix_kernel(*refs, n_act):
    x_ref = refs[0]
    a_refs = refs[1:1 + n_act]
    w_ref, g_ref, b_ref, rw_ref, rb_ref, x1_ref, x1p_ref, comb_ref, ct_scr = refs[1 + n_act:]
    half = D_MODEL // 2
    tm = x_ref.shape[0]
    group = MIX_GROUP if tm % MIX_GROUP == 0 else tm
    ct_scr[...] = jnp.zeros(ct_scr.shape, F32)
    for r0 in range(0, tm, group):
        rs = slice(r0, r0 + group)
        ys = []
        for n0 in (0, half):
            y = None
            k0 = 0
            for a_ref in a_refs:
                kw = a_ref.shape[1]
                part = _dot(a_ref[rs, :], w_ref[k0:k0 + kw, n0:n0 + half])
                y = part if y is None else y + part
                k0 += kw
            ys.append(y)
        x1 = _layer_norm(ALPHA * x_ref[rs, :] + jnp.concatenate(ys, axis=1), g_ref[...], b_ref[...])
        x1_ref[rs, :] = x1

        x1p_ref[rs, :] = _pack_pair(x1[:, :half], x1[:, half:])
        logits = _dot(x1.astype(BF16), rw_ref[...])
        scores_t = _sigmoid(logits).T
        sc = [scores_t[e:e + 1, :] for e in range(N_EXPERTS)]
        sb = [sc[e] + rb_ref[e:e + 1, :] for e in range(N_EXPERTS)]
        for r, val in enumerate(_route(sc, sb)):
            ct_scr[r:r + 1, rs] = val
        comb_ref[rs, :] = ct_scr[:, rs].T


def _mix(x, acts, w_out, ln_g, ln_b, rw, rb, tm):
    rows = x.shape[0]
    return _row_call(functools.partial(_mix_kernel, n_act=len(acts)), "mix", rows, tm,
                     [x] + list(acts), [w_out, ln_g, ln_b, rw, rb],
                     [(D_MODEL, F32), (D_MODEL // 2, jnp.uint32), (LANES, F32)],
                     scratch=[pltpu.VMEM((LANES, tm), F32)])


def _moe_kernel(x_ref, comb_ref, wg_ref, wu_ref, wd_ref, g_ref, b_ref, o_ref, xb_scr, acc_scr):
    e = pl.program_id(1)

    @pl.when(e == 0)
    def _():
        xb_scr[...] = x_ref[...].astype(BF16)
        acc_scr[...] = jnp.zeros(acc_scr.shape, F32)

    xb = xb_scr[...]
    lane = lax.broadcasted_iota(jnp.int32, (1, LANES), 1)
    c_e = jnp.sum(jnp.where(lane == e, comb_ref[...], 0.0), axis=1, keepdims=True)
    gate = _dot(xb, wg_ref[...].astype(BF16))
    h = gate * _sigmoid(gate) * _dot(xb, wu_ref[...].astype(BF16))
    acc_scr[...] += _dot((h * c_e).astype(BF16), wd_ref[...].astype(BF16))

    @pl.when(e == N_EXPERTS - 1)
    def _():
        o_ref[...] = _layer_norm(ALPHA * x_ref[...] + acc_scr[...], g_ref[...], b_ref[...])


def _moe(x, comb, wg, wu, wd, layer, ln_g, ln_b, tm):
    rows = x.shape[0]
    assert rows % tm == 0
    return pl.pallas_call(
        _moe_kernel, name="moe", grid=(rows // tm, N_EXPERTS),
        in_specs=[pl.BlockSpec((tm, D_MODEL), lambda i, e: (i, 0)),
                  pl.BlockSpec((tm, LANES), lambda i, e: (i, 0)),
                  pl.BlockSpec((None, None, D_MODEL, D_EXPERT), lambda i, e: (layer, e, 0, 0)),
                  pl.BlockSpec((None, None, D_MODEL, D_EXPERT), lambda i, e: (layer, e, 0, 0)),
                  pl.BlockSpec((None, None, D_EXPERT, D_MODEL), lambda i, e: (layer, e, 0, 0)),
                  _full_spec(ln_g), _full_spec(ln_b)],
        out_specs=pl.BlockSpec((tm, D_MODEL), lambda i, e: (i, 0)),
        out_shape=jax.ShapeDtypeStruct((rows, D_MODEL), F32),
        scratch_shapes=[pltpu.VMEM((tm, D_MODEL), BF16), pltpu.VMEM((tm, D_MODEL), F32)],
        compiler_params=_params(("parallel", "arbitrary")))(x, comb, wg, wu, wd, ln_g, ln_b)


ROUTE_E1, ROUTE_E2, ROUTE_W1, ROUTE_W2 = N_EXPERTS, N_EXPERTS + 1, N_EXPERTS + 2, N_EXPERTS + 3
TE = 1024
SC_WINDOW = 128
RANK_TILE = 1024
MIX_GROUP = 256


def _pack_pair(a, b):
    au = lax.bitcast_convert_type(a.astype(BF16).astype(F32), jnp.uint32)
    bu = lax.bitcast_convert_type(b.astype(BF16).astype(F32), jnp.uint32)
    return (au >> 16) | (bu & jnp.uint32(0xFFFF0000))


def _unpack_pair(w):
    a = lax.bitcast_convert_type(w << 16, F32)
    b = lax.bitcast_convert_type(w & jnp.uint32(0xFFFF0000), F32)
    return a, b


def _rank_kernel(route_ref, pos_ref, texp_ref, nused_ref, cnt_scr, carry_scr, seg_scr, before_scr):
    ph, i = pl.program_id(0), pl.program_id(1)
    T = route_ref.shape[0]
    lane = lax.broadcasted_iota(jnp.int32, (1, LANES), 1)
    lane_f = lane.astype(F32)
    r = route_ref[...]
    e1, e2 = r[:, ROUTE_E1:ROUTE_E1 + 1], r[:, ROUTE_E2:ROUTE_E2 + 1]
    m1, m2 = lane_f == e1, lane_f == e2
    m = jnp.where(m1 | m2, 1.0, 0.0)
    colsum = jnp.sum(m, axis=0, keepdims=True)

    @pl.when((ph == 0) & (i == 0))
    def _():
        cnt_scr[...] = jnp.zeros(cnt_scr.shape, F32)

    @pl.when(ph == 0)
    def _():
        cnt_scr[...] += colsum

    @pl.when((ph == 1) & (i == 0))
    def _():
        cnt = cnt_scr[...].astype(jnp.int32)
        padded = (((cnt + (TE - 1)) // TE) * TE).astype(F32)
        rr = lax.broadcasted_iota(jnp.int32, (LANES, 1), 0)
        upper = (rr < lane).astype(BF16)
        hi, mid, lo = _split3(jnp.broadcast_to(padded, (HG_SUB, LANES)))
        seg = (_dot(hi, upper) + _dot(mid, upper) + _dot(lo, upper))[:1, :]
        seg_scr[...] = seg
        carry_scr[...] = jnp.zeros(carry_scr.shape, F32)
        seg_end = seg + padded
        tile_row = lax.broadcasted_iota(jnp.int32, texp_ref.shape, 1).astype(F32) * float(TE)
        te_acc = jnp.zeros(texp_ref.shape, jnp.int32)
        for e in range(N_EXPERTS):
            te_acc = te_acc + jnp.where(seg_end[:, e:e + 1] <= tile_row, 1, 0)
        texp_ref[...] = jnp.minimum(te_acc, N_EXPERTS - 1)
        nused_ref[...] = jnp.broadcast_to(seg_end[:, N_EXPERTS - 1:N_EXPERTS] / float(TE), nused_ref.shape).astype(jnp.int32)

    @pl.when((ph == 1) & (i == 0))
    def _():
        row = lax.broadcasted_iota(jnp.int32, (T, 1), 0)
        col = lax.broadcasted_iota(jnp.int32, (1, T), 1)
        before_scr[...] = (col < row).astype(BF16)

    @pl.when(ph == 1)
    def _():
        cum = _dot(before_scr[...], m.astype(BF16)) + carry_scr[...] + seg_scr[...]
        p1 = jnp.sum(jnp.where(m1, cum, 0.0), axis=1, keepdims=True)
        p2 = jnp.sum(jnp.where(m2, cum, 0.0), axis=1, keepdims=True)
        pos_ref[...] = jnp.where(lane == 0, p1, jnp.where(lane == 1, p2, 0.0)).astype(jnp.int32)
        carry_scr[...] += colsum


def _rank(route, n_tiles, tm):
    rows = route.shape[0]
    nb = rows // tm
    nt_pad = -(-n_tiles // LANES) * LANES
    return pl.pallas_call(
        _rank_kernel, name="rank", grid=(2, nb),
        in_specs=[pl.BlockSpec((tm, LANES), lambda ph, i: (i, 0))],
        out_specs=[pl.BlockSpec((tm, LANES), lambda ph, i: (i * ph, 0)),
                   pl.BlockSpec((1, nt_pad), lambda ph, i: (0, 0)),
                   pl.BlockSpec((1, LANES), lambda ph, i: (0, 0))],
        out_shape=[jax.ShapeDtypeStruct((rows, LANES), jnp.int32),
                   jax.ShapeDtypeStruct((1, nt_pad), jnp.int32),
                   jax.ShapeDtypeStruct((1, LANES), jnp.int32)],
        scratch_shapes=[pltpu.VMEM((1, LANES), F32), pltpu.VMEM((1, LANES), F32), pltpu.VMEM((1, LANES), F32),
                        pltpu.VMEM((tm, tm), BF16)],
        compiler_params=_params(("arbitrary", "arbitrary")))(route)


def _sc_mesh():
    return plsc.VectorSubcoreMesh(core_axis_name="c", subcore_axis_name="s")


def _sc_scatter_rows(x, idx, n_out):
    rows, d = x.shape
    mesh = _sc_mesh()
    n_workers = mesh.num_cores * mesh.num_subcores
    steps = idx.shape[1] // SC_WINDOW // n_workers
    assert steps * SC_WINDOW * n_workers == idx.shape[1] and rows % SC_WINDOW == 0

    @functools.partial(pl.kernel, out_type=jax.ShapeDtypeStruct((n_out, d), x.dtype), mesh=mesh,
                       scratch_types=[pltpu.VMEM((1, SC_WINDOW), jnp.int32), pltpu.VMEM((SC_WINDOW, d), x.dtype)])
    def scatter(x_hbm, i_hbm, o_hbm, i_vmem, buf):
        first = (lax.axis_index("c") * mesh.num_subcores + lax.axis_index("s")) * steps

        @pl.loop(0, steps)
        def _(t):
            off = (first + t) * SC_WINDOW
            pltpu.sync_copy(i_hbm.at[:, pl.ds(off, SC_WINDOW)], i_vmem)
            pltpu.sync_copy(x_hbm.at[pl.ds(off % rows, SC_WINDOW)], buf)
            pltpu.sync_copy(buf, o_hbm.at[i_vmem.at[0]])

    return scatter(x, idx)


def _sc_gather_rows(x, idx):
    d = x.shape[1]
    n = idx.shape[1]
    mesh = _sc_mesh()
    n_workers = mesh.num_cores * mesh.num_subcores
    steps = n // SC_WINDOW // n_workers
    assert steps * SC_WINDOW * n_workers == n

    @functools.partial(pl.kernel, out_type=jax.ShapeDtypeStruct((n, d), x.dtype), mesh=mesh,
                       scratch_types=[pltpu.VMEM((1, SC_WINDOW), jnp.int32), pltpu.VMEM((SC_WINDOW, d), x.dtype)])
    def gather(x_hbm, i_hbm, o_hbm, i_vmem, buf):
        first = (lax.axis_index("c") * mesh.num_subcores + lax.axis_index("s")) * steps

        @pl.loop(0, steps)
        def _(t):
            off = (first + t) * SC_WINDOW
            pltpu.sync_copy(i_hbm.at[:, pl.ds(off, SC_WINDOW)], i_vmem)
            pltpu.sync_copy(x_hbm.at[i_vmem.at[0]], buf)
            pltpu.sync_copy(buf, o_hbm.at[pl.ds(off, SC_WINDOW)])

    return gather(x, idx)


def _gmm_kernel(texp_ref, nused_ref, x_ref, wg_ref, wu_ref, wd_ref, o_ref):
    @pl.when(pl.program_id(0) < nused_ref[0])
    def _():
        a, b = _unpack_pair(x_ref[...])
        xb = jnp.concatenate([a.astype(BF16), b.astype(BF16)], axis=1)
        gate = _dot(xb, wg_ref[...].astype(BF16))
        h = gate * _sigmoid(gate) * _dot(xb, wu_ref[...].astype(BF16))
        y = _dot(h.astype(BF16), wd_ref[...].astype(BF16))
        o_ref[...] = _pack_pair(y[:, :D_MODEL // 2], y[:, D_MODEL // 2:])


def _gmm(xs, texp, nused, wg, wu, wd, layer):
    rows = xs.shape[0]
    wmap = lambda d, te, nu: (layer, te[d], 0, 0)
    grid_spec = pltpu.PrefetchScalarGridSpec(
        num_scalar_prefetch=2, grid=(rows // TE,),
        in_specs=[pl.BlockSpec((TE, D_MODEL // 2), lambda d, te, nu: (d, 0)),
                  pl.BlockSpec((None, None, D_MODEL, D_EXPERT), wmap),
                  pl.BlockSpec((None, None, D_MODEL, D_EXPERT), wmap),
                  pl.BlockSpec((None, None, D_EXPERT, D_MODEL), wmap)],
        out_specs=pl.BlockSpec((TE, D_MODEL // 2), lambda d, te, nu: (d, 0)))
    return pl.pallas_call(
        _gmm_kernel, name="gmm", grid_spec=grid_spec,
        out_shape=jax.ShapeDtypeStruct((rows, D_MODEL // 2), jnp.uint32),
        compiler_params=_params(("arbitrary",)))(texp, nused, xs, wg, wu, wd)


def _combine_kernel(x_ref, g0_ref, g1_ref, route_ref, g_ref, b_ref, o_ref):
    r = route_ref[...]
    y0 = jnp.concatenate(_unpack_pair(g0_ref[...]), axis=1)
    y1 = jnp.concatenate(_unpack_pair(g1_ref[...]), axis=1)
    f = y0 * r[:, ROUTE_W1:ROUTE_W1 + 1] + y1 * r[:, ROUTE_W2:ROUTE_W2 + 1]
    o_ref[...] = _layer_norm(ALPHA * x_ref[...] + f, g_ref[...], b_ref[...])


def _combine(x, g, route, ln_g, ln_b, tm):
    rows = x.shape[0]
    nb = rows // tm
    return pl.pallas_call(
        _combine_kernel, name="combine", grid=(nb,),
        in_specs=[pl.BlockSpec((tm, D_MODEL), lambda i: (i, 0)),
                  pl.BlockSpec((tm, D_MODEL // 2), lambda i: (i, 0)),
                  pl.BlockSpec((tm, D_MODEL // 2), lambda i: (nb + i, 0)),
                  pl.BlockSpec((tm, LANES), lambda i: (i, 0)), _full_spec(ln_g), _full_spec(ln_b)],
        out_specs=pl.BlockSpec((tm, D_MODEL), lambda i: (i, 0)),
        out_shape=jax.ShapeDtypeStruct((rows, D_MODEL), F32),
        compiler_params=_params(("parallel",)))(x, g, g, route, ln_g, ln_b)


def _moe_routed(x1, x1b, route, wg, wu, wd, layer, ln_g, ln_b, tm):
    rows = x1.shape[0]
    n_rows = 2 * rows + N_EXPERTS * TE
    pos, texp, nused = _rank(route, n_rows // TE, RANK_TILE)
    idx = jnp.concatenate([pos[:, 0], pos[:, 1]])[None, :]
    xs = _sc_scatter_rows(x1b, idx, n_rows)
    ys = _gmm(xs, texp[0, :n_rows // TE], nused[0, :1], wg, wu, wd, layer)
    g = _sc_gather_rows(ys, idx)
    return _combine(x1, g, route, ln_g, ln_b, tm)


def _rope128(x, cos_t, sin_t):
    lane = lax.broadcasted_iota(jnp.int32, (1, LANES), 1)
    half = MLA_ROPE // 2
    swapped = jnp.where(lane < half, pltpu.roll(x, LANES - half, axis=1), pltpu.roll(x, half, axis=1))
    return x * cos_t + swapped * sin_t


def _odd_proj_kernel(x_ref, cos_ref, sin_ref, w_ref, gq_ref, gkv_ref, wuq_ref,
                     q_ref, ckv_ref, kpe_ref, kpe16_ref):
    z = _dot(x_ref[...].astype(BF16), w_ref[...])
    cq = _rms_norm(z[:, :MLA_Q_LORA], gq_ref[...])
    ckv_ref[...] = _rms_norm(z[:, MLA_Q_LORA:MLA_Q_LORA + MLA_KV_LORA], gkv_ref[...])
    cos_t, sin_t = cos_ref[...], sin_ref[...]
    kpe = _rope128(z[:, MLA_Q_LORA + MLA_KV_LORA:], cos_t, sin_t)
    kpe_ref[...] = kpe[:, :MLA_ROPE]
    kpe16_ref[...] = kpe.astype(BF16)
    qf = _dot(cq.astype(BF16), wuq_ref[...])
    scale = (MLA_NOPE + MLA_ROPE) ** -0.5 * LOG2E
    for h in range(MLA_HEADS):
        c0 = h * MLA_QPAD
        q_ref[:, c0:c0 + MLA_NOPE] = (qf[:, c0:c0 + MLA_NOPE] * scale).astype(BF16)
        qr = _rope128(qf[:, c0 + MLA_NOPE:c0 + MLA_QPAD], cos_t, sin_t)
        q_ref[:, c0 + MLA_NOPE:c0 + MLA_QPAD] = (qr * scale).astype(BF16)


def _odd_proj(x, cos_t, sin_t, w_in, gq, gkv, wuq, tm):
    rows = x.shape[0]
    outs = [(MLA_HEADS * MLA_QPAD, BF16), (MLA_KV_LORA, F32), (MLA_ROPE, F32), (LANES, BF16)]
    return _row_call(_odd_proj_kernel, "odd_proj", rows, tm, [x, cos_t, sin_t], [w_in, gq, gkv, wuq], outs)


def _kv_expand_kernel(c_ref, w_ref, k_ref, v_ref):
    kv = _dot(c_ref[...].astype(BF16), w_ref[...])
    n = MLA_HEADS * MLA_NOPE
    k_ref[...] = kv[:, :n].astype(BF16)
    v_ref[...] = kv[:, n:].astype(BF16)


def _kv_expand(ckv, w_ukv, tm):
    rows = ckv.shape[0]
    return _row_call(_kv_expand_kernel, "kv_expand", rows, tm, [ckv], [w_ukv],
                     [(MLA_HEADS * MLA_NOPE, BF16), (MLA_HEADS * MLA_V, BF16)])


def _mla_absorbed_kernel(q_ref, cp_ref, rp_ref, cn_ref, rn_ref, wuk_ref, wuv_ref, o_ref, m_scr, acc_scr, *, tkp):
    tq = q_ref.shape[0]
    rows = MLA_HEADS * tq
    q = q_ref[...]
    qa = []
    for h in range(MLA_HEADS):
        c0 = h * MLA_QPAD
        q_abs = _dot_nt(q[:, c0:c0 + MLA_NOPE], wuk_ref[h])
        qa.append(jnp.concatenate([q_abs.astype(BF16), q[:, c0 + MLA_NOPE:c0 + MLA_NOPE + MLA_ROPE]], axis=1))
    qs = jnp.concatenate(qa, axis=0)
    m_scr[...] = jnp.full(m_scr.shape, NEG, F32)
    acc_scr[...] = jnp.zeros(acc_scr.shape, F32)

    def update(c, r):
        c = c.astype(BF16)
        s = _dot_nt(qs, jnp.concatenate([c, r.astype(BF16)], axis=1))
        m_prev = m_scr[...]
        m_new = jnp.maximum(m_prev, jnp.max(s, axis=1, keepdims=True))
        if s.shape[1] % LANES == 0:
            p = jnp.exp2(s - jnp.tile(m_new, (1, s.shape[1] // LANES)))
        else:
            p = jnp.exp2(s - m_new[:, :1])
        c1 = jnp.concatenate([c, jnp.ones((c.shape[0], LANES), BF16)], axis=1)
        acc_scr[...] = (jnp.tile(jnp.exp2(m_prev - m_new), (1, acc_scr.shape[1] // LANES)) * acc_scr[...]
                        + _dot(p.astype(BF16), c1))
        m_scr[...] = m_new

    def past_body(j, carry):
        rs = pl.ds(pl.multiple_of(j * tkp, tkp), tkp)
        update(cp_ref[rs, :], rp_ref[rs, :])
        return carry
    lax.fori_loop(0, cp_ref.shape[0] // tkp, past_body, 0)
    update(cn_ref[...], rn_ref[:, :MLA_ROPE])

    acc = acc_scr[...]
    lat = (acc[:, :MLA_KV_LORA] / jnp.tile(acc[:, MLA_KV_LORA:], (1, MLA_KV_LORA // LANES))).astype(BF16)
    for h in range(MLA_HEADS):
        o_ref[:, h * MLA_V:(h + 1) * MLA_V] = _dot(lat[h * tq:(h + 1) * tq, :],
                                                   wuv_ref[:, h * MLA_V:(h + 1) * MLA_V]).astype(o_ref.dtype)


def _mla_absorbed(q, ckv_past, kpe_past, ckv_new, kpe_new, wuk_t, wuv, n_seq, tq, tkp):
    p = ckv_past.shape[1]
    assert p % tkp == 0
    rows = MLA_HEADS * tq
    return pl.pallas_call(
        functools.partial(_mla_absorbed_kernel, tkp=tkp), name="mla_absorbed", grid=(n_seq,),
        in_specs=[pl.BlockSpec((tq, MLA_HEADS * MLA_QPAD), lambda b: (b, 0)),
                  pl.BlockSpec((None, p, MLA_KV_LORA), lambda b: (b, 0, 0)),
                  pl.BlockSpec((None, p, MLA_ROPE), lambda b: (b, 0, 0)),
                  pl.BlockSpec((tq, MLA_KV_LORA), lambda b: (b, 0)),
                  pl.BlockSpec((tq, LANES), lambda b: (b, 0)),
                  _full_spec(wuk_t), _full_spec(wuv)],
        out_specs=pl.BlockSpec((tq, MLA_HEADS * MLA_V), lambda b: (b, 0)),
        out_shape=jax.ShapeDtypeStruct((n_seq * tq, MLA_HEADS * MLA_V), BF16),
        scratch_shapes=[pltpu.VMEM((rows, LANES), F32), pltpu.VMEM((rows, MLA_KV_LORA + LANES), F32)],
        compiler_params=_params(("parallel",)))(q, ckv_past, kpe_past, ckv_new, kpe_new, wuk_t, wuv)


def _rope_tables(pos):
    half = MLA_ROPE // 2
    inv = ROPE_BASE ** (-jnp.arange(half, dtype=F32) / half)
    ang = pos.astype(F32)[:, None] * inv[None, :]
    cos, sin = jnp.cos(ang), jnp.sin(ang)
    z = jnp.zeros((pos.shape[0], LANES - MLA_ROPE), F32)
    return jnp.concatenate([cos, cos, z], axis=1), jnp.concatenate([-sin, sin, z], axis=1)


def _pad_rows(a, n):
    return jnp.pad(a, ((0, n - a.shape[0]),) + ((0, 0),) * (a.ndim - 1))


def kernel(x_prompt, x_sample, state_hgrn2, cache_fox_k, cache_fox_v, cache_fox_logf, cache_mla_ckv, cache_mla_kpe, meta_tokens, even_w_in, hg_lb_logits, hg_norm_g, fox_forget_bias, even_w_out, mla_w_in, mla_q_norm_g, mla_kv_norm_g, mla_w_uq, mla_w_uk, mla_w_uv, mla_w_out, ln_mix_g, ln_mix_b, ln_ffn_g, ln_ffn_b, router_w, router_bias, moe_w_gate, moe_w_up, moe_w_down):
    B, T, _ = x_prompt.shape
    Bs, Ts, _ = x_sample.shape
    P = cache_fox_k.shape[2]
    RM = B * T
    RS = Bs * Ts
    RSM = -(-(RS + N_META) // LANES) * LANES
    ME = slice(RS, RS + N_META)
    TM_MAIN, TM_MOE, TQ = 512, 1024, 512

    xm = x_prompt.reshape(RM, D_MODEL)
    xs = _pad_rows(jnp.concatenate([x_sample.reshape(RS, D_MODEL), meta_tokens.astype(F32)], axis=0), RSM)

    w_in0 = even_w_in[0]
    n_main = 7 * HG_W
    w_even = w_in0[:, :n_main].astype(BF16)
    w_even_f = jnp.pad(w_in0[:, n_main:], ((0, 0), (0, LANES - FOX_HEADS))).astype(BF16)
    fb_pad = jnp.pad(fox_forget_bias[0][None, :], ((0, 0), (0, LANES - FOX_HEADS)))
    g_hg = hg_norm_g[0].reshape(1, HG_W)
    w_out0 = even_w_out[0].astype(BF16)
    e_mat = ((jnp.arange(HG_SUB * HG_DK)[:, None] // HG_DK) == (jnp.arange(CHUNK)[None, :] % HG_SUB)).astype(BF16)

    w_odd = jnp.pad(mla_w_in[0], ((0, 0), (0, LANES - MLA_ROPE))).astype(BF16)
    gq = mla_q_norm_g[0][None, :]
    gkv = mla_kv_norm_g[0][None, :]
    wuq = mla_w_uq[0].reshape(MLA_Q_LORA, MLA_HEADS, MLA_NOPE + MLA_ROPE)
    wuq = jnp.pad(wuq, ((0, 0), (0, 0), (0, MLA_QPAD - MLA_NOPE - MLA_ROPE)))
    wuq = wuq.reshape(MLA_Q_LORA, MLA_HEADS * MLA_QPAD).astype(BF16)
    w_ukv = jnp.concatenate([mla_w_uk[0].reshape(MLA_KV_LORA, -1), mla_w_uv[0].reshape(MLA_KV_LORA, -1)],
                            axis=1).astype(BF16)
    w_out1 = mla_w_out[0].astype(BF16)

    rw = jnp.pad(router_w, ((0, 0), (0, LANES - N_EXPERTS))).astype(BF16)
    rb = jnp.pad(router_bias.astype(F32)[:, None], ((0, LANES - N_EXPERTS), (0, 0)))
    experts = (moe_w_gate, moe_w_up, moe_w_down)
    row2 = lambda a: a[None, :]

    def ffn(x, acts, w_out, l, tm_mix, tm_moe, routed):
        x1, x1b, route = _mix(x, acts, w_out, row2(ln_mix_g[l]), row2(ln_mix_b[l]), rw, rb, tm_mix)
        ln = (row2(ln_ffn_g[l]), row2(ln_ffn_b[l]))
        if routed:
            return _moe_routed(x1, x1b, route, *experts, l, *ln, tm_mix)
        return _moe(x1, route, *experts, l, *ln, tm_moe)

    pm = _even_proj(xm, w_even, w_even_f, hg_lb_logits, fb_pad, TM_MAIN, 0)
    ps = _even_proj(xs, w_even, w_even_f, hg_lb_logits, fb_pad, RSM, 0)
    names = ("hq", "lf", "hk", "hv", "hgate", "fq", "fk", "fv", "fk16", "fv16", "flf")
    pm = dict(zip(names, pm))
    ps = dict(zip(names, ps))

    hg_keys = ("hq", "lf", "hk", "hv", "hgate", "flf")
    meta_in = [_pad_rows(ps[n][ME], CHUNK) for n in hg_keys]
    zero_s = jnp.zeros((1, HG_HEADS, HG_DK, HG_DV), F32)
    zero_f = jnp.zeros((1, 1, LANES), F32)
    o_hg_meta, fc_meta, s_meta = _hgrn2(*meta_in, g_hg, e_mat, zero_s, zero_f, 1, CHUNK, 0, CHUNK)
    o_hg_meta, fc_meta = o_hg_meta[:N_META], fc_meta[:N_META]
    f_meta_end = fc_meta[N_META - 1:N_META][None]

    o_hg_m, fc_m, s_main = _hgrn2(*[pm[n] for n in hg_keys], g_hg, e_mat, s_meta, f_meta_end, B, T, 0, 512)

    logf_c = jnp.pad(jnp.transpose(cache_fox_logf[0], (0, 2, 1)), ((0, 0), (0, HG_SUB - FOX_HEADS), (0, 0)))
    fpast = _cumsum_lanes(logf_c.reshape(Bs * HG_SUB, P), 512).reshape(Bs, HG_SUB, P)[:, :FOX_HEADS, :]
    f0_s = jnp.pad(fpast[:, :, P - 1][:, None, :], ((0, 0), (0, 0), (0, LANES - FOX_HEADS)))
    o_hg_s, fc_s, s_samp = _hgrn2(*[ps[n] for n in hg_keys], g_hg, e_mat, state_hgrn2[0], f0_s, Bs, Ts, 0, CHUNK)

    def bias_layouts(fc, n_seq, seq_len):
        f4 = fc[:, :FOX_HEADS].T
        return f4[:, :, None], f4.reshape(FOX_HEADS, n_seq, 1, seq_len)

    fq_m, fk_m = bias_layouts(fc_m, B, T)
    fq_s, fk_s = bias_layouts(fc_s, Bs, Ts)
    fq_t, fk_t = bias_layouts(fc_meta, 1, N_META)

    fox_kw = dict(n_heads=FOX_HEADS, dq=FOX_DH, dk=FOX_DH, dv=FOX_DH, mask_mode="causal")
    meta_past = dict(k=ps["fk16"][ME][None], v=ps["fv16"][ME][None],
                     fk=jnp.transpose(fk_t, (1, 0, 2, 3)), tk=N_META)
    o_fox_m = _flash_tri(pm["fq"], pm["fk16"], pm["fv16"], n_seq=B, seq_len=T, tq=TQ,
                         fq=fq_m, fkn=fk_m, past=meta_past, **fox_kw)
    samp_past = dict(k=cache_fox_k[0], v=cache_fox_v[0],
                     fk=fpast[:, :, None, :], tk=1024)
    o_fox_s = _flash(ps["fq"], ps["fk16"], ps["fv16"], n_seq=Bs, seq_len=Ts, tq=Ts, q_off=0, k_off=0,
                     fq=fq_s, fkn=fk_s, past=samp_past, **fox_kw)
    o_fox_t = _flash(ps["fq"][ME], ps["fk16"][ME], ps["fv16"][ME], n_seq=1, seq_len=N_META, tq=N_META,
                     q_off=0, k_off=0, fq=fq_t, fkn=fk_t, **fox_kw)

    o_hg_small = _pad_rows(jnp.concatenate([o_hg_s, o_hg_meta], axis=0), RSM)
    o_fox_small = _pad_rows(jnp.concatenate([o_fox_s, o_fox_t], axis=0), RSM)
    xm = ffn(xm, [o_hg_m, o_fox_m], w_out0, 0, TM_MOE, TM_MOE, True)
    xs = ffn(xs, [o_hg_small, o_fox_small], w_out0, 0, RSM, RSM, False)

    cos_m, sin_m = _rope_tables(N_META + jnp.arange(T, dtype=jnp.int32))
    pos_small = _pad_rows(jnp.concatenate([jnp.tile(P + jnp.arange(Ts, dtype=jnp.int32), Bs),
                                           jnp.arange(N_META, dtype=jnp.int32)]), RSM)
    cos_s, sin_s = _rope_tables(pos_small)
    qm, ckv_m, kpe_m, kpe16_m = _odd_proj(xm, jnp.tile(cos_m, (B, 1)), jnp.tile(sin_m, (B, 1)),
                                          w_odd, gq, gkv, wuq, TM_MAIN)
    qs, ckv_s, kpe_s, kpe16_s = _odd_proj(xs, cos_s, sin_s, w_odd, gq, gkv, wuq, RSM)
    kn_m, vn_m = _kv_expand(ckv_m, w_ukv, 1024)
    kn_s, vn_s = _kv_expand(ckv_s, w_ukv, RSM)
    wuk_t = jnp.transpose(mla_w_uk[0], (1, 0, 2)).astype(BF16)
    wuv = mla_w_uv[0].reshape(MLA_KV_LORA, MLA_HEADS * MLA_V).astype(BF16)

    mla_kw = dict(n_heads=MLA_HEADS, dq=MLA_QPAD, dk=MLA_NOPE, dv=MLA_V)
    meta_past = dict(k=kn_s[ME][None], v=vn_s[ME][None], r=kpe16_s[ME][None], tk=N_META)
    o_m = _flash_tri(qm, kn_m, vn_m, n_seq=B, seq_len=T, tq=TQ, rn=kpe16_m,
                     past=meta_past, mask_mode="chunk", **mla_kw)
    assert P % CHUNK == 0 and Ts <= CHUNK
    o_s = _mla_absorbed(qs, cache_mla_ckv[0], cache_mla_kpe[0], ckv_s, kpe16_s, wuk_t, wuv, Bs, Ts, 1024)
    o_t = _flash(qs[ME], kn_s[ME], vn_s[ME], n_seq=1, seq_len=N_META, tq=N_META, q_off=0, k_off=0,
                 rn=kpe16_s[ME], mask_mode="full", **mla_kw)
    xm = ffn(xm, [o_m], w_out1, 1, TM_MOE, TM_MOE, True)
    xs = ffn(xs, [_pad_rows(jnp.concatenate([o_s, o_t], axis=0), RSM)], w_out1, 1, RSM, RSM, False)

    def with_meta(main, small, *width):
        meta = jnp.broadcast_to(small[ME][None], (B, N_META) + width)
        return jnp.concatenate([meta, main.reshape((B, T) + width)], axis=1)

    y_prompt = xm.reshape(B, T, D_MODEL)
    y_sample = xs[:RS].reshape(Bs, Ts, D_MODEL)
    hg_p = s_main[None]
    fk_p = with_meta(pm["fk"], ps["fk"], FOX_HEADS, FOX_DH)[None]
    fv_p = with_meta(pm["fv"], ps["fv"], FOX_HEADS, FOX_DH)[None]
    flf_p = with_meta(pm["flf"][:, :FOX_HEADS], ps["flf"][:, :FOX_HEADS], FOX_HEADS)[None]
    ckv_p = with_meta(ckv_m, ckv_s, MLA_KV_LORA)[None]
    kpe_p = with_meta(kpe_m, kpe_s, MLA_ROPE)[None]
    hg_s = s_samp[None]
    fk_s_out = ps["fk"][:RS].reshape(1, Bs, Ts, FOX_HEADS, FOX_DH)
    fv_s_out = ps["fv"][:RS].reshape(1, Bs, Ts, FOX_HEADS, FOX_DH)
    flf_s = ps["flf"][:RS, :FOX_HEADS].reshape(1, Bs, Ts, FOX_HEADS)
    ckv_so = ckv_s[:RS].reshape(1, Bs, Ts, MLA_KV_LORA)
    kpe_so = kpe_s[:RS].reshape(1, Bs, Ts, MLA_ROPE)
    return (y_prompt, y_sample, hg_p, fk_p, fv_p, flf_p, ckv_p, kpe_p,
            hg_s, fk_s_out, fv_s_out, flf_s, ckv_so, kpe_so)
```

```python
import functools

import jax
import jax.numpy as jnp
from jax import lax
from jax.experimental import pallas as pl
from jax.experimental.pallas import tpu as pltpu
from jax.experimental.pallas import tpu_sc as plsc

D_MODEL = 1024
CHUNK = 64
N_META = 16
HG_HEADS = 4
HG_DK = 128
HG_DV = 128
HG_W = HG_HEADS * HG_DK
FOX_HEADS = 4
FOX_DH = 128
FOX_W = FOX_HEADS * FOX_DH
MLA_HEADS = 8
MLA_Q_LORA = 512
MLA_KV_LORA = 256
MLA_NOPE = 128
MLA_ROPE = 64
MLA_V = 128
MLA_QPAD = 256
ROPE_BASE = 10000.0
N_EXPERTS = 16
N_GROUPS = 4
EXPERTS_PER_GROUP = 4
D_EXPERT = 256
DEPTH = 2
ALPHA = (2 * DEPTH) ** 0.25
LN_EPS = 1e-5
RMS_EPS = 1e-6

LANES = 128
HG_SUB = 8
HG_GROUP = 4
NEG = -1e30
LOG2E = 1.4426950408889634
F32 = jnp.float32
BF16 = jnp.bfloat16
VMEM_LIMIT = 56 * 1024 * 1024


def _dot(a, b):
    return jnp.dot(a, b, preferred_element_type=F32)


def _dot_nt(a, b):
    return lax.dot_general(a, b, (((1,), (1,)), ((), ())), preferred_element_type=F32)


def _dot_tn(a, b):
    return lax.dot_general(a, b, (((0,), (0,)), ((), ())), preferred_element_type=F32)


def _split3(x):
    hi = x.astype(BF16)
    r = x - hi.astype(F32)
    mid = r.astype(BF16)
    lo = (r - mid.astype(F32)).astype(BF16)
    return hi, mid, lo


def _cumsum_rows(tri, x):
    hi, mid, lo = _split3(x)
    return _dot(tri, hi) + _dot(tri, mid) + _dot(tri, lo)


def _sigmoid(x):
    return 1.0 / (1.0 + jnp.exp(-x))


def _log_sigmoid(x):
    return jnp.minimum(x, 0.0) - jnp.log(1.0 + jnp.exp(-jnp.abs(x)))


def _layer_norm(x, g, b):
    mu = jnp.mean(x, axis=-1, keepdims=True)
    xc = x - mu
    var = jnp.mean(xc * xc, axis=-1, keepdims=True)
    return xc * lax.rsqrt(var + LN_EPS) * g + b


def _rms_norm(x, g):
    return x * lax.rsqrt(jnp.mean(x * x, axis=-1, keepdims=True) + RMS_EPS) * g


def _params(sem):
    return pltpu.CompilerParams(dimension_semantics=sem, vmem_limit_bytes=VMEM_LIMIT)


def _full_spec(a):
    nd = a.ndim
    return pl.BlockSpec(a.shape, lambda *_: (0,) * nd)


def _row_call(kernel, name, rows, tm, row_ins, full_ins, outs, scratch=()):
    assert rows % tm == 0
    in_specs = [pl.BlockSpec((tm, a.shape[1]), lambda i: (i, 0)) for a in row_ins]
    in_specs += [_full_spec(a) for a in full_ins]
    trail = [c if isinstance(c, tuple) else (c,) for c, _ in outs]
    out_specs = [pl.BlockSpec((tm,) + t, lambda i, n=len(t): (i,) + (0,) * n) for t in trail]
    out_shape = [jax.ShapeDtypeStruct((rows,) + t, dt) for t, (_, dt) in zip(trail, outs)]
    return pl.pallas_call(
        kernel, name=name, grid=(rows // tm,), in_specs=in_specs, out_specs=out_specs,
        out_shape=out_shape, scratch_shapes=list(scratch),
        compiler_params=_params(("parallel",)))(*row_ins, *full_ins)


def _even_proj_kernel(x_ref, w_ref, wf_ref, lbl_ref, fb_ref,
                      hq_ref, lf_ref, hk_ref, hv_ref, hgate_ref,
                      fq_ref, fk_ref, fv_ref, fk16_ref, fv16_ref, flf_ref, *, layer):
    xb = x_ref[...].astype(BF16)

    def blk(j):
        return _dot(xb, w_ref[:, j * HG_W:(j + 1) * HG_W])

    logits = lbl_ref[...]
    e = jnp.exp(logits - jnp.max(logits, axis=0, keepdims=True))
    lb = jnp.sum(e[:layer + 1], axis=0, keepdims=True) / jnp.sum(e, axis=0, keepdims=True)

    hq_ref[...] = blk(0).astype(BF16)
    zf = blk(1)
    lf_ref[...] = jnp.log(lb + (1.0 - lb) * _sigmoid(zf))
    hk_ref[...] = ((1.0 - lb) * _sigmoid(-zf)).astype(BF16)
    hv_ref[...] = blk(2).astype(BF16)
    hgate_ref[...] = _sigmoid(blk(3)).astype(BF16)
    fq_ref[...] = (blk(4) * (FOX_DH ** -0.5 * LOG2E)).astype(BF16)
    fk = blk(5)
    fk16_ref[...] = fk.astype(BF16)
    fv = blk(6)
    fv16_ref[...] = fv.astype(BF16)
    for h in range(FOX_HEADS):
        fk_ref[:, h, :] = fk[:, h * FOX_DH:(h + 1) * FOX_DH]
        fv_ref[:, h, :] = fv[:, h * FOX_DH:(h + 1) * FOX_DH]
    flf_ref[...] = _log_sigmoid(_dot(xb, wf_ref[...]) + fb_ref[...])


def _even_proj(x, w_main, w_f, lb_logits, fb_pad, tm, layer):
    rows = x.shape[0]
    outs = [(HG_W, BF16), (HG_W, F32), (HG_W, BF16), (HG_W, BF16), (HG_W, BF16),
            (FOX_W, BF16), ((FOX_HEADS, FOX_DH), F32), ((FOX_HEADS, FOX_DH), F32), (FOX_W, BF16), (FOX_W, BF16),
            (LANES, F32)]
    return _row_call(functools.partial(_even_proj_kernel, layer=layer), "even_proj", rows, tm,
                     [x], [w_main, w_f, lb_logits, fb_pad], outs)


def _bcast_sub(x, j):
    n, c = x.shape
    x3 = x.reshape(n // HG_SUB, HG_SUB, c)
    return jnp.broadcast_to(x3[:, j:j + 1, :], x3.shape).reshape(n, c)


def _level_ref(b, w):
    n, c = b.shape
    parts = [jnp.broadcast_to(b[m * 2 * w + w - 1:m * 2 * w + w, :], (2 * w, c)) for m in range(n // (2 * w))]
    return parts[0] if len(parts) == 1 else jnp.concatenate(parts, axis=0)


def _hgrn2_kernel(q_ref, lf_ref, k_ref, v_ref, gate_ref, flf_ref, g_ref, e_ref, s0_ref, f0_ref,
                  o_ref, fcum_ref, sout_ref, st_scr, fc_scr, *, n_chunks):
    i = pl.program_id(1)
    C = CHUNK

    @pl.when(i == 0)
    def _():
        for h in range(HG_HEADS):
            st_scr[h] = s0_ref[h].T
        fc_scr[...] = f0_ref[...]

    row = lax.broadcasted_iota(jnp.int32, (C, 1), 0)
    col = lax.broadcasted_iota(jnp.int32, (1, C), 1)
    tri = (col <= row).astype(BF16)
    same = lambda w: (row // w) == (col // w)
    levels = (32, 16, 8)

    fc = fc_scr[...]
    for c in range(n_chunks):
        sl = slice(c * C, (c + 1) * C)
        fcum = _cumsum_rows(tri, flf_ref[sl, :]) + fc
        fcum_ref[sl, :] = fcum
        fc = fcum[C - 1:C, :]
    fc_scr[...] = fc

    staged = []
    for c in range(n_chunks):
        sl = slice(c * C, (c + 1) * C)
        per_head = []
        for h0 in range(0, HG_HEADS, HG_GROUP):
            gs = slice(h0 * HG_DK, (h0 + HG_GROUP) * HG_DK)
            b = _cumsum_rows(tri, lf_ref[sl, gs]) * LOG2E
            q = q_ref[sl, gs].astype(F32)
            k = k_ref[sl, gs].astype(F32)
            v = v_ref[sl, gs]
            qb = (q * jnp.exp2(b)).astype(BF16)
            b_last = b[C - 1:C, :]
            kd = (k * jnp.exp2(b_last - b)).astype(BF16)
            e_last = jnp.exp2(b_last)

            pjs = [(jnp.exp2(jnp.where((row % HG_SUB) >= j, b - _bcast_sub(b, j), NEG)) * q
                    * _bcast_sub(k, j)).astype(BF16) for j in range(HG_SUB)]
            lv = []
            for w in levels:
                upper = (row % (2 * w)) >= w
                ew = jnp.exp2(-jnp.abs(b - _level_ref(b, w)))
                lv.append((jnp.where(upper, q * ew, 0.0).astype(BF16), jnp.where(upper, 0.0, k * ew).astype(BF16)))

            for hh in range(HG_GROUP):
                hs = slice(hh * HG_DK, (hh + 1) * HG_DK)
                a = jnp.where(same(HG_SUB), _dot(jnp.concatenate([p[:, hs] for p in pjs], axis=1), e_ref[...]), 0.0)
                for w, (qw, kw) in zip(levels, lv):
                    aw = _dot_nt(qw[:, hs], kw[:, hs])
                    a = a + (aw if 2 * w == C else jnp.where(same(2 * w), aw, 0.0))
                vh = v[:, hs]
                per_head.append((_dot(a.astype(BF16), vh), qb[:, hs], kd[:, hs], e_last[:, hs], vh))
        staged.append(per_head)

    st = [st_scr[h] for h in range(HG_HEADS)]
    for c in range(n_chunks):
        sl = slice(c * C, (c + 1) * C)
        for h, (o_intra, qb_h, kd_h, e_h, vh) in enumerate(staged[c]):
            ho = slice(h * HG_DK, (h + 1) * HG_DK)
            o = o_intra + _dot_nt(qb_h, st[h].astype(BF16))
            st[h] = st[h] * e_h + _dot_tn(vh, kd_h)
            o = _rms_norm(o, g_ref[:, ho])
            o_ref[sl, ho] = (o * gate_ref[sl, ho].astype(F32)).astype(BF16)
    for h in range(HG_HEADS):
        st_scr[h] = st[h]

    @pl.when(i == pl.num_programs(1) - 1)
    def _():
        for h in range(HG_HEADS):
            sout_ref[h] = st_scr[h].T


def _hgrn2(q, lf, k, v, gate, flf, g, e_mat, s0, f0, n_seq, seq_len, row_off, tb):
    assert seq_len % tb == 0 and tb % CHUNK == 0 and row_off % tb == 0
    nb = seq_len // tb
    off = row_off // tb
    per_seq = s0.shape[0] > 1
    rmap = lambda s, i: (off + s * nb + i, 0)
    omap = lambda s, i: (s * nb + i, 0)
    smap = (lambda s, i: (s, 0, 0, 0)) if per_seq else (lambda s, i: (0, 0, 0, 0))
    fmap = (lambda s, i: (s, 0, 0)) if per_seq else (lambda s, i: (0, 0, 0))
    in_specs = [pl.BlockSpec((tb, HG_W), rmap) for _ in range(5)]
    in_specs += [pl.BlockSpec((tb, LANES), rmap), _full_spec(g), _full_spec(e_mat),
                 pl.BlockSpec((None, HG_HEADS, HG_DK, HG_DV), smap), pl.BlockSpec((None, 1, LANES), fmap)]
    out_specs = [pl.BlockSpec((tb, HG_W), omap), pl.BlockSpec((tb, LANES), omap),
                 pl.BlockSpec((None, HG_HEADS, HG_DK, HG_DV), lambda s, i: (s, 0, 0, 0))]
    out_shape = [jax.ShapeDtypeStruct((n_seq * seq_len, HG_W), BF16),
                 jax.ShapeDtypeStruct((n_seq * seq_len, LANES), F32),
                 jax.ShapeDtypeStruct((n_seq, HG_HEADS, HG_DK, HG_DV), F32)]
    scratch = [pltpu.VMEM((HG_HEADS, HG_DV, HG_DK), F32), pltpu.VMEM((1, LANES), F32)]
    return pl.pallas_call(
        functools.partial(_hgrn2_kernel, n_chunks=tb // CHUNK), name="hgrn2",
        grid=(n_seq, nb), in_specs=in_specs, out_specs=out_specs, out_shape=out_shape,
        scratch_shapes=scratch, compiler_params=_params(("parallel", "arbitrary")))(
            q, lf, k, v, gate, flf, g, e_mat, s0, f0)


def _cumsum_kernel(x_ref, tri_ref, o_ref, carry):
    @pl.when(pl.program_id(0) == 0)
    def _():
        carry[...] = jnp.zeros_like(carry)

    hi, mid, lo = _split3(x_ref[...])
    tri = tri_ref[...]
    out = _dot(hi, tri) + _dot(mid, tri) + _dot(lo, tri) + carry[...]
    o_ref[...] = out
    carry[...] = out[:, out.shape[1] - 1:]


def _cumsum_lanes(x, tb):
    r, seq_len = x.shape
    tri = (jnp.arange(tb)[:, None] <= jnp.arange(tb)[None, :]).astype(BF16)
    return pl.pallas_call(
        _cumsum_kernel, name="cumsum", grid=(seq_len // tb,),
        in_specs=[pl.BlockSpec((r, tb), lambda i: (0, i)), _full_spec(tri)],
        out_specs=pl.BlockSpec((r, tb), lambda i: (0, i)),
        out_shape=jax.ShapeDtypeStruct(x.shape, F32),
        scratch_shapes=[pltpu.VMEM((r, 1), F32)],
        compiler_params=_params(("arbitrary",)))(x, tri)


def _flash_kernel(*refs, n_past_blk, tkp, tq, has_bias, has_rope, mask_mode, has_past, past_heads):
    it = iter(refs)
    q_ref = next(it)
    fq_ref = next(it) if has_bias else None
    if has_past:
        kp_ref, vp_ref = next(it), next(it)
        rp_ref = next(it) if has_rope else None
        fkp_ref = next(it) if has_bias else None
    kn_ref, vn_ref = next(it), next(it)
    rn_ref = next(it) if has_rope else None
    fkn_ref = next(it) if has_bias else None
    o_ref = next(it)
    m_scr, acc_scr = next(it), next(it)
    dv = o_ref.shape[1]

    q = q_ref[...]
    m_scr[...] = jnp.full(m_scr.shape, NEG, F32)
    acc_scr[...] = jnp.zeros(acc_scr.shape, F32)
    fq_b = jnp.broadcast_to(fq_ref[...] * LOG2E, (tq, LANES)) if has_bias else None

    def scores(k, r, fk):
        if has_rope:
            k = jnp.concatenate([k, r], axis=1)
        s = _dot_nt(q, k.astype(BF16))
        if has_bias:
            s = s + jnp.tile(fq_b, (1, s.shape[1] // LANES)) if s.shape[1] % LANES == 0 else s + fq_b[:, :1]
            s = s - fk * LOG2E
        return s

    def update(s, v, mask):
        if mask is not None:
            s = jnp.where(mask, s, NEG)
        m_prev = m_scr[...]
        m_new = jnp.maximum(m_prev, jnp.max(s, axis=1, keepdims=True))
        alpha = jnp.exp2(m_prev - m_new)
        if s.shape[1] % LANES == 0:
            p = jnp.exp2(s - jnp.tile(m_new, (1, s.shape[1] // LANES)))
        else:
            p = jnp.exp2(s - m_new[:, :1])
        v1 = jnp.concatenate([v.astype(BF16), jnp.ones((v.shape[0], LANES), BF16)], axis=1)
        acc_scr[...] = jnp.tile(alpha, (1, acc_scr.shape[1] // LANES)) * acc_scr[...] + _dot(p.astype(BF16), v1)
        m_scr[...] = m_new

    past_kv = (lambda ref, rs: ref[rs, pl.program_id(1), :]) if past_heads else (lambda ref, rs: ref[rs, :])

    def past_block(rs):
        return (past_kv(kp_ref, rs), rp_ref[rs, :] if has_rope else None, fkp_ref[:, rs] if has_bias else None)

    def new_block(rs):
        return (kn_ref[rs, :], rn_ref[rs, :] if has_rope else None, fkn_ref[:, rs] if has_bias else None)

    if has_past:
        if n_past_blk == 1:
            update(scores(*past_block(slice(None))), past_kv(vp_ref, slice(None)), None)
        else:
            def past_body(j, carry):
                rs = pl.ds(pl.multiple_of(j * tkp, tkp), tkp)
                update(scores(*past_block(rs)), past_kv(vp_ref, rs), None)
                return carry
            lax.fori_loop(0, n_past_blk, past_body, 0)

    row = lax.broadcasted_iota(jnp.int32, (tq, 1), 0)
    col = lax.broadcasted_iota(jnp.int32, (1, tq), 1)
    if mask_mode == "causal":
        mask = col <= row
    elif mask_mode == "chunk":
        mask = (col // CHUNK) <= (row // CHUNK)
    else:
        mask = None

    update(scores(*new_block(slice(None))), vn_ref[...], mask)
    acc = acc_scr[...]
    o_ref[...] = (acc[:, :dv] / acc[:, dv:]).astype(o_ref.dtype)


def _flash(q, kn, vn, *, n_seq, n_heads, seq_len, tq, dq, dk, dv, q_off, k_off, mask_mode,
           fq=None, fkn=None, rn=None, past=None):
    assert seq_len % tq == 0 and q_off % tq == 0 and k_off % seq_len == 0
    nq = seq_len // tq
    qo = q_off // tq
    ko = k_off // seq_len
    has_bias = fq is not None
    has_rope = rn is not None
    has_past = past is not None
    ins, specs = [q], [pl.BlockSpec((tq, dq), lambda b, h, i: (qo + b * nq + i, h))]
    if has_bias:
        ins.append(fq)
        specs.append(pl.BlockSpec((None, tq, 1), lambda b, h, i: (h, qo + b * nq + i, 0)))
    n_past_blk, tkp, past_heads = 0, 0, False
    if has_past:
        tp = past["k"].shape[1]
        tkp = past["tk"]
        assert tp % tkp == 0
        n_past_blk = tp // tkp
        pb = (lambda b: b) if past["k"].shape[0] > 1 else (lambda b: 0)
        ins += [past["k"], past["v"]]
        past_heads = past["k"].ndim == 4
        if past_heads:
            specs += [pl.BlockSpec((None, tp, n_heads, dk), lambda b, h, i: (pb(b), 0, 0, 0)),
                      pl.BlockSpec((None, tp, n_heads, dv), lambda b, h, i: (pb(b), 0, 0, 0))]
        else:
            specs += [pl.BlockSpec((None, tp, dk), lambda b, h, i: (pb(b), 0, h)),
                      pl.BlockSpec((None, tp, dv), lambda b, h, i: (pb(b), 0, h))]
        if has_rope:
            ins.append(past["r"])
            specs.append(pl.BlockSpec((None, tp, LANES), lambda b, h, i: (pb(b), 0, 0)))
        if has_bias:
            ins.append(past["fk"])
            specs.append(pl.BlockSpec((None, None, 1, tp), lambda b, h, i: (pb(b), h, 0, 0)))
    ins += [kn, vn]
    specs += [pl.BlockSpec((seq_len, dk), lambda b, h, i: (ko + b, h)),
              pl.BlockSpec((seq_len, dv), lambda b, h, i: (ko + b, h))]
    if has_rope:
        ins.append(rn)
        specs.append(pl.BlockSpec((seq_len, LANES), lambda b, h, i: (ko + b, 0)))
    if has_bias:
        ins.append(fkn)
        specs.append(pl.BlockSpec((None, None, 1, seq_len), lambda b, h, i: (h, ko + b, 0, 0)))
    assert nq == 1
    kern = functools.partial(_flash_kernel, n_past_blk=n_past_blk, tkp=tkp, tq=tq, has_bias=has_bias,
                             has_rope=has_rope, mask_mode=mask_mode, has_past=has_past, past_heads=past_heads)
    return pl.pallas_call(
        kern, name="flash", grid=(n_seq, n_heads, nq), in_specs=specs,
        out_specs=pl.BlockSpec((tq, dv), lambda b, h, i: (b * nq + i, h)),
        out_shape=jax.ShapeDtypeStruct((n_seq * seq_len, n_heads * dv), BF16),
        scratch_shapes=[pltpu.VMEM((tq, LANES), F32), pltpu.VMEM((tq, dv + LANES), F32)],
        compiler_params=_params(("parallel", "parallel", "arbitrary")))(*ins)


def _fox_cached_kernel(q_ref, fq_ref, kp_hbm, vp_hbm, fkp_ref, kn_ref, vn_ref, fkn_ref, o_ref,
                       kbuf, vbuf, ksem, vsem, *, tkp, n_blk):
    b = pl.program_id(0)
    tq = q_ref.shape[0]
    row = lax.broadcasted_iota(jnp.int32, (tq, 1), 0)
    col = lax.broadcasted_iota(jnp.int32, (1, tq), 1)
    ones_p = jnp.ones((tkp, LANES), BF16)
    steps = [(h, j) for h in range(FOX_HEADS) for j in range(n_blk)]

    def copies(i):
        h, j = steps[i]
        slot = i % 2
        src = lambda ref: ref.at[b, pl.ds(j * tkp, tkp), h, :]
        return (pltpu.make_async_copy(src(kp_hbm), kbuf.at[slot], ksem.at[slot]),
                pltpu.make_async_copy(src(vp_hbm), vbuf.at[slot], vsem.at[slot]))

    def update(state, s, v1):
        m_prev, acc = state
        m_new = jnp.maximum(m_prev, jnp.max(s, axis=1, keepdims=True))
        p = jnp.exp2(s - m_new)
        return m_new, jnp.exp2(m_prev - m_new) * acc + _dot(p.astype(BF16), v1)

    for c in copies(0):
        c.start()
    state = None
    for i, (h, j) in enumerate(steps):
        hs = slice(h * FOX_DH, (h + 1) * FOX_DH)
        q = q_ref[:, hs]
        fq = fq_ref[h] * LOG2E
        if j == 0:
            state = (jnp.full((tq, 1), NEG, F32), jnp.zeros((tq, FOX_DH + LANES), F32))
        if i + 1 < len(steps):
            for c in copies(i + 1):
                c.start()
        for c in copies(i):
            c.wait()
        slot = i % 2
        s = _dot_nt(q, kbuf[slot].astype(BF16)) + fq - fkp_ref[h, :, j * tkp:(j + 1) * tkp] * LOG2E
        state = update(state, s, jnp.concatenate([vbuf[slot].astype(BF16), ones_p], axis=1))
        if j == n_blk - 1:
            s = _dot_nt(q, kn_ref[:, hs]) + fq - fkn_ref[h] * LOG2E
            s = jnp.where(col <= row, s, NEG)
            _, acc = update(state, s, jnp.concatenate([vn_ref[:, hs], ones_p[:tq]], axis=1))
            o_ref[:, hs] = (acc[:, :FOX_DH] / acc[:, FOX_DH:FOX_DH + 1]).astype(o_ref.dtype)


def _fox_cached(q, fq, kp, vp, fkp, kn, vn, fkn, n_seq, tq, tkp):
    p = kp.shape[1]
    assert p % tkp == 0
    return pl.pallas_call(
        functools.partial(_fox_cached_kernel, tkp=tkp, n_blk=p // tkp), name="fox_cached", grid=(n_seq,),
        in_specs=[pl.BlockSpec((tq, FOX_W), lambda b: (b, 0)),
                  pl.BlockSpec((FOX_HEADS, tq, 1), lambda b: (0, b, 0)),
                  pl.BlockSpec(memory_space=pl.ANY),
                  pl.BlockSpec(memory_space=pl.ANY),
                  pl.BlockSpec((None, FOX_HEADS, 1, p), lambda b: (b, 0, 0, 0)),
                  pl.BlockSpec((tq, FOX_W), lambda b: (b, 0)),
                  pl.BlockSpec((tq, FOX_W), lambda b: (b, 0)),
                  pl.BlockSpec((FOX_HEADS, None, 1, tq), lambda b: (0, b, 0, 0))],
        out_specs=pl.BlockSpec((tq, FOX_W), lambda b: (b, 0)),
        out_shape=jax.ShapeDtypeStruct((n_seq * tq, FOX_W), BF16),
        scratch_shapes=[pltpu.VMEM((2, tkp, FOX_DH), F32), pltpu.VMEM((2, tkp, FOX_DH), F32),
                        pltpu.SemaphoreType.DMA((2,)), pltpu.SemaphoreType.DMA((2,))],
        compiler_params=_params(("arbitrary",)))(q, fq, kp, vp, fkp, kn, vn, fkn)


FLASH_UNROLL_OFF = 14
FLASH_UNROLL_DIAG = 8


def _tri_tables(nq):
    pairs = [(qi, kj) for qi in range(nq) for kj in range(qi)] + [(qi, qi) for qi in range(nq)] + [(0, 0)]
    return (jnp.array([p[0] for p in pairs], jnp.int32), jnp.array([p[1] for p in pairs], jnp.int32))


def _flash_tri_kernel(qt_ref, kt_ref, *refs, tq, nq, has_bias, has_rope, mask_mode):
    it = iter(refs)
    q_ref = next(it)
    fq_ref = next(it) if has_bias else None
    kp_ref, vp_ref = next(it), next(it)
    rp_ref = next(it) if has_rope else None
    fkp_ref = next(it) if has_bias else None
    kn_ref, vn_ref = next(it), next(it)
    rn_ref = next(it) if has_rope else None
    fkn_ref = next(it) if has_bias else None
    o_ref = next(it)
    m_scr, acc_scr, sa_scr, sb_scr = next(it), next(it), next(it), next(it)
    fqb_scr = next(it) if has_bias else None
    dv = o_ref.shape[1]
    n_off = nq * (nq - 1) // 2
    tile = lambda j: pl.ds(pl.multiple_of(j * tq, tq), tq)
    ones = jnp.ones((tq, LANES), BF16)

    kp = kp_ref[...]
    if has_rope:
        kp = jnp.concatenate([kp, rp_ref[...]], axis=1)
    vp1 = jnp.concatenate([vp_ref[...], ones[:vp_ref.shape[0]]], axis=1)
    for i in range(nq):
        rs = slice(i * tq, (i + 1) * tq)
        s = _dot_nt(q_ref[rs, :], kp)
        if has_bias:
            fb = fq_ref[rs, :] * LOG2E
            fqb_scr[rs, :] = jnp.broadcast_to(fb, (tq, LANES))
            s = s + fb - fkp_ref[...] * LOG2E
        m0 = jnp.max(s, axis=1, keepdims=True)
        m_scr[i] = jnp.broadcast_to(m0, (tq, LANES))
        acc_scr[i] = _dot(jnp.exp2(s - m0).astype(BF16), vp1)

    def fill(s_ref, t):
        qs, ks = tile(qt_ref[t]), tile(kt_ref[t])
        k = kn_ref[ks, :]
        if has_rope:
            k = jnp.concatenate([k, rn_ref[ks, :]], axis=1)
        s = _dot_nt(q_ref[qs, :], k)
        if has_bias:
            s = s + jnp.tile(fqb_scr[qs, :], (1, tq // LANES)) - fkn_ref[:, ks] * LOG2E
        s_ref[...] = s

    def drain(s_ref, t, mask):
        qi = qt_ref[t]
        s = s_ref[...]
        if mask is not None:
            s = jnp.where(mask, s, NEG)
        m_prev = m_scr[qi]
        m_new = jnp.maximum(m_prev, jnp.max(s, axis=1, keepdims=True))
        p = jnp.exp2(s - jnp.tile(m_new, (1, tq // LANES)))
        v1 = jnp.concatenate([vn_ref[tile(kt_ref[t]), :], ones], axis=1)
        acc = jnp.tile(jnp.exp2(m_prev - m_new), (1, (dv + LANES) // LANES)) * acc_scr[qi] + _dot(p.astype(BF16), v1)
        return qi, m_new, acc

    def keep(s_ref, t):
        qi, m_new, acc = drain(s_ref, t, None)
        m_scr[qi] = m_new
        acc_scr[qi] = acc

    row = lax.broadcasted_iota(jnp.int32, (tq, 1), 0)
    col = lax.broadcasted_iota(jnp.int32, (1, tq), 1)
    mask = {"causal": col <= row, "chunk": (col // CHUNK) <= (row // CHUNK)}[mask_mode]

    def finish(s_ref, t):
        qi, _, acc = drain(s_ref, t, mask)
        o_ref[tile(qi), :] = (acc[:, :dv] / acc[:, dv:]).astype(o_ref.dtype)

    def pipeline(t0, n, unroll, consume):
        assert n % unroll == 0 and unroll % 2 == 0

        def body(i, carry):
            t = t0 + unroll * i
            for u in range(0, unroll, 2):
                fill(sb_scr, t + u + 1)
                consume(sa_scr, t + u)
                fill(sa_scr, t + u + 2)
                consume(sb_scr, t + u + 1)
            return carry
        lax.fori_loop(0, n // unroll, body, 0)

    fill(sa_scr, 0)
    pipeline(0, n_off, FLASH_UNROLL_OFF, keep)
    pipeline(n_off, nq, FLASH_UNROLL_DIAG, finish)


def _flash_tri(q, kn, vn, *, n_seq, n_heads, seq_len, tq, dq, dk, dv, mask_mode, past, fq=None, fkn=None, rn=None):
    nq = seq_len // tq
    has_bias = fq is not None
    has_rope = rn is not None
    tp = past["k"].shape[1]
    m3 = lambda f: (lambda b, h, qt, kt: f(b, h))
    ins, specs = [q], [pl.BlockSpec((seq_len, dq), m3(lambda b, h: (b, h)))]
    if has_bias:
        ins.append(fq)
        specs.append(pl.BlockSpec((None, seq_len, 1), m3(lambda b, h: (h, b, 0))))
    ins += [past["k"], past["v"]]
    specs += [pl.BlockSpec((None, tp, dk), m3(lambda b, h: (0, 0, h))),
              pl.BlockSpec((None, tp, dv), m3(lambda b, h: (0, 0, h)))]
    if has_rope:
        ins.append(past["r"])
        specs.append(pl.BlockSpec((None, tp, LANES), m3(lambda b, h: (0, 0, 0))))
    if has_bias:
        ins.append(past["fk"])
        specs.append(pl.BlockSpec((None, None, 1, tp), m3(lambda b, h: (0, h, 0, 0))))
    ins += [kn, vn]
    specs += [pl.BlockSpec((seq_len, dk), m3(lambda b, h: (b, h))),
              pl.BlockSpec((seq_len, dv), m3(lambda b, h: (b, h)))]
    if has_rope:
        ins.append(rn)
        specs.append(pl.BlockSpec((seq_len, LANES), m3(lambda b, h: (b, 0))))
    if has_bias:
        ins.append(fkn)
        specs.append(pl.BlockSpec((None, None, 1, seq_len), m3(lambda b, h: (h, b, 0, 0))))
    scratch = [pltpu.VMEM((nq, tq, LANES), F32), pltpu.VMEM((nq, tq, dv + LANES), F32),
               pltpu.VMEM((tq, tq), F32), pltpu.VMEM((tq, tq), F32)]
    if has_bias:
        scratch.append(pltpu.VMEM((seq_len, LANES), F32))
    grid_spec = pltpu.PrefetchScalarGridSpec(
        num_scalar_prefetch=2, grid=(n_seq, n_heads), in_specs=specs,
        out_specs=pl.BlockSpec((seq_len, dv), m3(lambda b, h: (b, h))), scratch_shapes=scratch)
    kern = functools.partial(_flash_tri_kernel, tq=tq, nq=nq, has_bias=has_bias, has_rope=has_rope,
                             mask_mode=mask_mode)
    return pl.pallas_call(
        kern, name="flash_tri", grid_spec=grid_spec,
        out_shape=jax.ShapeDtypeStruct((n_seq * seq_len, n_heads * dv), BF16),
        compiler_params=_params(("parallel", "arbitrary")))(*_tri_tables(nq), *ins)


def _route(sc, sb):
    def top2_sum(v):
        a, b, c, d = v
        a, b = jnp.maximum(a, b), jnp.minimum(a, b)
        c, d = jnp.maximum(c, d), jnp.minimum(c, d)
        hi, lo2 = jnp.maximum(a, c), jnp.minimum(a, c)
        return hi + jnp.maximum(lo2, jnp.maximum(b, d))

    gs = [top2_sum(sb[g * EXPERTS_PER_GROUP:(g + 1) * EXPERTS_PER_GROUP]) for g in range(N_GROUPS)]
    best_v, best_g = gs[0], jnp.zeros(gs[0].shape, jnp.int32)
    for g in range(1, N_GROUPS):
        upd = gs[g] > best_v
        best_v = jnp.where(upd, gs[g], best_v)
        best_g = jnp.where(upd, g, best_g)
    masked = [jnp.where(best_g == (e // EXPERTS_PER_GROUP), sb[e], -jnp.inf) for e in range(N_EXPERTS)]

    def argmax_first(vals, exclude=None):
        bv = jnp.full(vals[0].shape, -jnp.inf, F32)
        bi = jnp.full(vals[0].shape, -1, jnp.int32)
        for e, v in enumerate(vals):
            upd = v > bv
            if exclude is not None:
                upd = upd & (exclude != e)
            bv = jnp.where(upd, v, bv)
            bi = jnp.where(upd, e, bi)
        return bi

    i1 = argmax_first(masked)
    i2 = argmax_first(masked, exclude=i1)
    w1 = sum(jnp.where(i1 == e, sc[e], 0.0) for e in range(N_EXPERTS))
    w2 = sum(jnp.where(i2 == e, sc[e], 0.0) for e in range(N_EXPERTS))
    tot = w1 + w2
    w1, w2 = w1 / tot, w2 / tot
    comb = [jnp.where(i1 == e, w1, 0.0) + jnp.where(i2 == e, w2, 0.0) for e in range(N_EXPERTS)]
    return comb + [i1.astype(F32), i2.astype(F32), w1, w2]


def _mix_kernel(*refs, n_act):
    x_ref = refs[0]
    a_refs = refs[1:1 + n_act]
    w_ref, g_ref, b_ref, rw_ref, rb_ref, x1_ref, x1p_ref, comb_ref, ct_scr = refs[1 + n_act:]
    half = D_MODEL // 2
    tm = x_ref.shape[0]
    group = MIX_GROUP if tm % MIX_GROUP == 0 else tm
    ct_scr[...] = jnp.zeros(ct_scr.shape, F32)
    for r0 in range(0, tm, group):
        rs = slice(r0, r0 + group)
        ys = []
        for n0 in (0, half):
            y = None
            k0 = 0
            for a_ref in a_refs:
                kw = a_ref.shape[1]
                part = _dot(a_ref[rs, :], w_ref[k0:k0 + kw, n0:n0 + half])
                y = part if y is None else y + part
                k0 += kw
            ys.append(y)
        x1 = _layer_norm(ALPHA * x_ref[rs, :] + jnp.concatenate(ys, axis=1), g_ref[...], b_ref[...])
        x1_ref[rs, :] = x1

        x1p_ref[rs, :] = _pack_pair(x1[:, :half], x1[:, half:])
        logits = _dot(x1.astype(BF16), rw_ref[...])
        scores_t = _sigmoid(logits).T
        sc = [scores_t[e:e + 1, :] for e in range(N_EXPERTS)]
        sb = [sc[e] + rb_ref[e:e + 1, :] for e in range(N_EXPERTS)]
        for r, val in enumerate(_route(sc, sb)):
            ct_scr[r:r + 1, rs] = val
        comb_ref[rs, :] = ct_scr[:, rs].T


def _mix(x, acts, w_out, ln_g, ln_b, rw, rb, tm):
    rows = x.shape[0]
    return _row_call(functools.partial(_mix_kernel, n_act=len(acts)), "mix", rows, tm,
                     [x] + list(acts), [w_out, ln_g, ln_b, rw, rb],
                     [(D_MODEL, F32), (D_MODEL // 2, jnp.uint32), (LANES, F32)],
                     scratch=[pltpu.VMEM((LANES, tm), F32)])


def _moe_kernel(x_ref, comb_ref, wg_ref, wu_ref, wd_ref, g_ref, b_ref, o_ref, xb_scr, acc_scr):
    e = pl.program_id(1)

    @pl.when(e == 0)
    def _():
        xb_scr[...] = x_ref[...].astype(BF16)
        acc_scr[...] = jnp.zeros(acc_scr.shape, F32)

    xb = xb_scr[...]
    lane = lax.broadcasted_iota(jnp.int32, (1, LANES), 1)
    c_e = jnp.sum(jnp.where(lane == e, comb_ref[...], 0.0), axis=1, keepdims=True)
    gate = _dot(xb, wg_ref[...].astype(BF16))
    h = gate * _sigmoid(gate) * _dot(xb, wu_ref[...].astype(BF16))
    acc_scr[...] += _dot((h * c_e).astype(BF16), wd_ref[...].astype(BF16))

    @pl.when(e == N_EXPERTS - 1)
    def _():
        o_ref[...] = _layer_norm(ALPHA * x_ref[...] + acc_scr[...], g_ref[...], b_ref[...])


def _moe(x, comb, wg, wu, wd, layer, ln_g, ln_b, tm):
    rows = x.shape[0]
    assert rows % tm == 0
    return pl.pallas_call(
        _moe_kernel, name="moe", grid=(rows // tm, N_EXPERTS),
        in_specs=[pl.BlockSpec((tm, D_MODEL), lambda i, e: (i, 0)),
                  pl.BlockSpec((tm, LANES), lambda i, e: (i, 0)),
                  pl.BlockSpec((None, None, D_MODEL, D_EXPERT), lambda i, e: (layer, e, 0, 0)),
                  pl.BlockSpec((None, None, D_MODEL, D_EXPERT), lambda i, e: (layer, e, 0, 0)),
                  pl.BlockSpec((None, None, D_EXPERT, D_MODEL), lambda i, e: (layer, e, 0, 0)),
                  _full_spec(ln_g), _full_spec(ln_b)],
        out_specs=pl.BlockSpec((tm, D_MODEL), lambda i, e: (i, 0)),
        out_shape=jax.ShapeDtypeStruct((rows, D_MODEL), F32),
        scratch_shapes=[pltpu.VMEM((tm, D_MODEL), BF16), pltpu.VMEM((tm, D_MODEL), F32)],
        compiler_params=_params(("parallel", "arbitrary")))(x, comb, wg, wu, wd, ln_g, ln_b)


ROUTE_E1, ROUTE_E2, ROUTE_W1, ROUTE_W2 = N_EXPERTS, N_EXPERTS + 1, N_EXPERTS + 2, N_EXPERTS + 3
TE = 1024
SC_WINDOW = 128
RANK_TILE = 1024
MIX_GROUP = 256


def _pack_pair(a, b):
    au = lax.bitcast_convert_type(a.astype(BF16).astype(F32), jnp.uint32)
    bu = lax.bitcast_convert_type(b.astype(BF16).astype(F32), jnp.uint32)
    return (au >> 16) | (bu & jnp.uint32(0xFFFF0000))


def _unpack_pair(w):
    a = lax.bitcast_convert_type(w << 16, F32)
    b = lax.bitcast_convert_type(w & jnp.uint32(0xFFFF0000), F32)
    return a, b


def _rank_kernel(route_ref, pos_ref, texp_ref, nused_ref, cnt_scr, carry_scr, seg_scr, before_scr):
    ph, i = pl.program_id(0), pl.program_id(1)
    T = route_ref.shape[0]
    lane = lax.broadcasted_iota(jnp.int32, (1, LANES), 1)
    lane_f = lane.astype(F32)
    r = route_ref[...]
    e1, e2 = r[:, ROUTE_E1:ROUTE_E1 + 1], r[:, ROUTE_E2:ROUTE_E2 + 1]
    m1, m2 = lane_f == e1, lane_f == e2
    m = jnp.where(m1 | m2, 1.0, 0.0)
    colsum = jnp.sum(m, axis=0, keepdims=True)

    @pl.when((ph == 0) & (i == 0))
    def _():
        cnt_scr[...] = jnp.zeros(cnt_scr.shape, F32)

    @pl.when(ph == 0)
    def _():
        cnt_scr[...] += colsum

    @pl.when((ph == 1) & (i == 0))
    def _():
        cnt = cnt_scr[...].astype(jnp.int32)
        padded = (((cnt + (TE - 1)) // TE) * TE).astype(F32)
        rr = lax.broadcasted_iota(jnp.int32, (LANES, 1), 0)
        upper = (rr < lane).astype(BF16)
        hi, mid, lo = _split3(jnp.broadcast_to(padded, (HG_SUB, LANES)))
        seg = (_dot(hi, upper) + _dot(mid, upper) + _dot(lo, upper))[:1, :]
        seg_scr[...] = seg
        carry_scr[...] = jnp.zeros(carry_scr.shape, F32)
        seg_end = seg + padded
        tile_row = lax.broadcasted_iota(jnp.int32, texp_ref.shape, 1).astype(F32) * float(TE)
        te_acc = jnp.zeros(texp_ref.shape, jnp.int32)
        for e in range(N_EXPERTS):
            te_acc = te_acc + jnp.where(seg_end[:, e:e + 1] <= tile_row, 1, 0)
        texp_ref[...] = jnp.minimum(te_acc, N_EXPERTS - 1)
        nused_ref[...] = jnp.broadcast_to(seg_end[:, N_EXPERTS - 1:N_EXPERTS] / float(TE), nused_ref.shape).astype(jnp.int32)

    @pl.when((ph == 1) & (i == 0))
    def _():
        row = lax.broadcasted_iota(jnp.int32, (T, 1), 0)
        col = lax.broadcasted_iota(jnp.int32, (1, T), 1)
        before_scr[...] = (col < row).astype(BF16)

    @pl.when(ph == 1)
    def _():
        cum = _dot(before_scr[...], m.astype(BF16)) + carry_scr[...] + seg_scr[...]
        p1 = jnp.sum(jnp.where(m1, cum, 0.0), axis=1, keepdims=True)
        p2 = jnp.sum(jnp.where(m2, cum, 0.0), axis=1, keepdims=True)
        pos_ref[...] = jnp.where(lane == 0, p1, jnp.where(lane == 1, p2, 0.0)).astype(jnp.int32)
        carry_scr[...] += colsum


def _rank(route, n_tiles, tm):
    rows = route.shape[0]
    nb = rows // tm
    nt_pad = -(-n_tiles // LANES) * LANES
    return pl.pallas_call(
        _rank_kernel, name="rank", grid=(2, nb),
        in_specs=[pl.BlockSpec((tm, LANES), lambda ph, i: (i, 0))],
        out_specs=[pl.BlockSpec((tm, LANES), lambda ph, i: (i * ph, 0)),
                   pl.BlockSpec((1, nt_pad), lambda ph, i: (0, 0)),
                   pl.BlockSpec((1, LANES), lambda ph, i: (0, 0))],
        out_shape=[jax.ShapeDtypeStruct((rows, LANES), jnp.int32),
                   jax.ShapeDtypeStruct((1, nt_pad), jnp.int32),
                   jax.ShapeDtypeStruct((1, LANES), jnp.int32)],
        scratch_shapes=[pltpu.VMEM((1, LANES), F32), pltpu.VMEM((1, LANES), F32), pltpu.VMEM((1, LANES), F32),
                        pltpu.VMEM((tm, tm), BF16)],
        compiler_params=_params(("arbitrary", "arbitrary")))(route)


def _sc_mesh():
    return plsc.VectorSubcoreMesh(core_axis_name="c", subcore_axis_name="s")


def _sc_scatter_rows(x, idx, n_out):
    rows, d = x.shape
    mesh = _sc_mesh()
    n_workers = mesh.num_cores * mesh.num_subcores
    steps = idx.shape[1] // SC_WINDOW // n_workers
    assert steps * SC_WINDOW * n_workers == idx.shape[1] and rows % SC_WINDOW == 0

    @functools.partial(pl.kernel, out_type=jax.ShapeDtypeStruct((n_out, d), x.dtype), mesh=mesh,
                       scratch_types=[pltpu.VMEM((1, SC_WINDOW), jnp.int32), pltpu.VMEM((SC_WINDOW, d), x.dtype)])
    def scatter(x_hbm, i_hbm, o_hbm, i_vmem, buf):
        first = (lax.axis_index("c") * mesh.num_subcores + lax.axis_index("s")) * steps

        @pl.loop(0, steps)
        def _(t):
            off = (first + t) * SC_WINDOW
            pltpu.sync_copy(i_hbm.at[:, pl.ds(off, SC_WINDOW)], i_vmem)
            pltpu.sync_copy(x_hbm.at[pl.ds(off % rows, SC_WINDOW)], buf)
            pltpu.sync_copy(buf, o_hbm.at[i_vmem.at[0]])

    return scatter(x, idx)


def _sc_gather_rows(x, idx):
    d = x.shape[1]
    n = idx.shape[1]
    mesh = _sc_mesh()
    n_workers = mesh.num_cores * mesh.num_subcores
    steps = n // SC_WINDOW // n_workers
    assert steps * SC_WINDOW * n_workers == n

    @functools.partial(pl.kernel, out_type=jax.ShapeDtypeStruct((n, d), x.dtype), mesh=mesh,
                       scratch_types=[pltpu.VMEM((1, SC_WINDOW), jnp.int32), pltpu.VMEM((SC_WINDOW, d), x.dtype)])
    def gather(x_hbm, i_hbm, o_hbm, i_vmem, buf):
        first = (lax.axis_index("c") * mesh.num_subcores + lax.axis_index("s")) * steps

        @pl.loop(0, steps)
        def _(t):
            off = (first + t) * SC_WINDOW
            pltpu.sync_copy(i_hbm.at[:, pl.ds(off, SC_WINDOW)], i_vmem)
            pltpu.sync_copy(x_hbm.at[i_vmem.at[0]], buf)
            pltpu.sync_copy(buf, o_hbm.at[pl.ds(off, SC_WINDOW)])

    return gather(x, idx)


def _gmm_kernel(texp_ref, nused_ref, x_ref, wg_ref, wu_ref, wd_ref, o_ref):
    @pl.when(pl.program_id(0) < nused_ref[0])
    def _():
        a, b = _unpack_pair(x_ref[...])
        xb = jnp.concatenate([a.astype(BF16), b.astype(BF16)], axis=1)
        gate = _dot(xb, wg_ref[...].astype(BF16))
        h = gate * _sigmoid(gate) * _dot(xb, wu_ref[...].astype(BF16))
        y = _dot(h.astype(BF16), wd_ref[...].astype(BF16))
        o_ref[...] = _pack_pair(y[:, :D_MODEL // 2], y[:, D_MODEL // 2:])


def _gmm(xs, texp, nused, wg, wu, wd, layer):
    rows = xs.shape[0]
    wmap = lambda d, te, nu: (layer, te[d], 0, 0)
    grid_spec = pltpu.PrefetchScalarGridSpec(
        num_scalar_prefetch=2, grid=(rows // TE,),
        in_specs=[pl.BlockSpec((TE, D_MODEL // 2), lambda d, te, nu: (d, 0)),
                  pl.BlockSpec((None, None, D_MODEL, D_EXPERT), wmap),
                  pl.BlockSpec((None, None, D_MODEL, D_EXPERT), wmap),
                  pl.BlockSpec((None, None, D_EXPERT, D_MODEL), wmap)],
        out_specs=pl.BlockSpec((TE, D_MODEL // 2), lambda d, te, nu: (d, 0)))
    return pl.pallas_call(
        _gmm_kernel, name="gmm", grid_spec=grid_spec,
        out_shape=jax.ShapeDtypeStruct((rows, D_MODEL // 2), jnp.uint32),
        compiler_params=_params(("arbitrary",)))(texp, nused, xs, wg, wu, wd)


def _combine_kernel(x_ref, g0_ref, g1_ref, route_ref, g_ref, b_ref, o_ref):
    r = route_ref[...]
    y0 = jnp.concatenate(_unpack_pair(g0_ref[...]), axis=1)
    y1 = jnp.concatenate(_unpack_pair(g1_ref[...]), axis=1)
    f = y0 * r[:, ROUTE_W1:ROUTE_W1 + 1] + y1 * r[:, ROUTE_W2:ROUTE_W2 + 1]
    o_ref[...] = _layer_norm(ALPHA * x_ref[...] + f, g_ref[...], b_ref[...])


def _combine(x, g, route, ln_g, ln_b, tm):
    rows = x.shape[0]
    nb = rows // tm
    return pl.pallas_call(
        _combine_kernel, name="combine", grid=(nb,),
        in_specs=[pl.BlockSpec((tm, D_MODEL), lambda i: (i, 0)),
                  pl.BlockSpec((tm, D_MODEL // 2), lambda i: (i, 0)),
                  pl.BlockSpec((tm, D_MODEL // 2), lambda i: (nb + i, 0)),
                  pl.BlockSpec((tm, LANES), lambda i: (i, 0)), _full_spec(ln_g), _full_spec(ln_b)],
        out_specs=pl.BlockSpec((tm, D_MODEL), lambda i: (i, 0)),
        out_shape=jax.ShapeDtypeStruct((rows, D_MODEL), F32),
        compiler_params=_params(("parallel",)))(x, g, g, route, ln_g, ln_b)


def _moe_routed(x1, x1b, route, wg, wu, wd, layer, ln_g, ln_b, tm):
    rows = x1.shape[0]
    n_rows = 2 * rows + N_EXPERTS * TE
    pos, texp, nused = _rank(route, n_rows // TE, RANK_TILE)
    idx = jnp.concatenate([pos[:, 0], pos[:, 1]])[None, :]
    xs = _sc_scatter_rows(x1b, idx, n_rows)
    ys = _gmm(xs, texp[0, :n_rows // TE], nused[0, :1], wg, wu, wd, layer)
    g = _sc_gather_rows(ys, idx)
    return _combine(x1, g, route, ln_g, ln_b, tm)


def _rope128(x, cos_t, sin_t):
    lane = lax.broadcasted_iota(jnp.int32, (1, LANES), 1)
    half = MLA_ROPE // 2
    swapped = jnp.where(lane < half, pltpu.roll(x, LANES - half, axis=1), pltpu.roll(x, half, axis=1))
    return x * cos_t + swapped * sin_t


def _odd_proj_kernel(x_ref, cos_ref, sin_ref, w_ref, gq_ref, gkv_ref, wuq_ref,
                     q_ref, ckv_ref, kpe_ref, kpe16_ref):
    tm = x_ref.shape[0]
    group = MIX_GROUP if tm % MIX_GROUP == 0 else tm
    scale = (MLA_NOPE + MLA_ROPE) ** -0.5 * LOG2E
    for r0 in range(0, tm, group):
        rs = slice(r0, r0 + group)
        z = _dot(x_ref[rs, :].astype(BF16), w_ref[...])
        cq = _rms_norm(z[:, :MLA_Q_LORA], gq_ref[...])
        ckv_ref[rs, :] = _rms_norm(z[:, MLA_Q_LORA:MLA_Q_LORA + MLA_KV_LORA], gkv_ref[...])
        cos_t, sin_t = cos_ref[rs, :], sin_ref[rs, :]
        kpe = _rope128(z[:, MLA_Q_LORA + MLA_KV_LORA:], cos_t, sin_t)
        kpe_ref[rs, :] = kpe[:, :MLA_ROPE]
        kpe16_ref[rs, :] = kpe.astype(BF16)
        qf = _dot(cq.astype(BF16), wuq_ref[...])
        for h in range(MLA_HEADS):
            c0 = h * MLA_QPAD
            q_ref[rs, c0:c0 + MLA_NOPE] = (qf[:, c0:c0 + MLA_NOPE] * scale).astype(BF16)
            qr = _rope128(qf[:, c0 + MLA_NOPE:c0 + MLA_QPAD], cos_t, sin_t)
            q_ref[rs, c0 + MLA_NOPE:c0 + MLA_QPAD] = (qr * scale).astype(BF16)


def _odd_proj(x, cos_t, sin_t, w_in, gq, gkv, wuq, tm):
    rows = x.shape[0]
    outs = [(MLA_HEADS * MLA_QPAD, BF16), (MLA_KV_LORA, F32), (MLA_ROPE, F32), (LANES, BF16)]
    return _row_call(_odd_proj_kernel, "odd_proj", rows, tm, [x, cos_t, sin_t], [w_in, gq, gkv, wuq], outs)


def _kv_expand_kernel(c_ref, w_ref, k_ref, v_ref):
    kv = _dot(c_ref[...].astype(BF16), w_ref[...])
    n = MLA_HEADS * MLA_NOPE
    k_ref[...] = kv[:, :n].astype(BF16)
    v_ref[...] = kv[:, n:].astype(BF16)


def _kv_expand(ckv, w_ukv, tm):
    rows = ckv.shape[0]
    return _row_call(_kv_expand_kernel, "kv_expand", rows, tm, [ckv], [w_ukv],
                     [(MLA_HEADS * MLA_NOPE, BF16), (MLA_HEADS * MLA_V, BF16)])


def _mla_absorbed_kernel(q_ref, cp_ref, rp_ref, cn_ref, rn_ref, wuk_ref, wuv_ref, o_ref, m_scr, acc_scr, *, tkp):
    tq = q_ref.shape[0]
    rows = MLA_HEADS * tq
    q = q_ref[...]
    qa = []
    for h in range(MLA_HEADS):
        c0 = h * MLA_QPAD
        q_abs = _dot_nt(q[:, c0:c0 + MLA_NOPE], wuk_ref[h])
        qa.append(jnp.concatenate([q_abs.astype(BF16), q[:, c0 + MLA_NOPE:c0 + MLA_NOPE + MLA_ROPE]], axis=1))
    qs = jnp.concatenate(qa, axis=0)
    m_scr[...] = jnp.full(m_scr.shape, NEG, F32)
    acc_scr[...] = jnp.zeros(acc_scr.shape, F32)

    def update(c, r):
        c = c.astype(BF16)
        s = _dot_nt(qs, jnp.concatenate([c, r.astype(BF16)], axis=1))
        m_prev = m_scr[...]
        m_new = jnp.maximum(m_prev, jnp.max(s, axis=1, keepdims=True))
        if s.shape[1] % LANES == 0:
            p = jnp.exp2(s - jnp.tile(m_new, (1, s.shape[1] // LANES)))
        else:
            p = jnp.exp2(s - m_new[:, :1])
        c1 = jnp.concatenate([c, jnp.ones((c.shape[0], LANES), BF16)], axis=1)
        acc_scr[...] = (jnp.tile(jnp.exp2(m_prev - m_new), (1, acc_scr.shape[1] // LANES)) * acc_scr[...]
                        + _dot(p.astype(BF16), c1))
        m_scr[...] = m_new

    def past_body(j, carry):
        rs = pl.ds(pl.multiple_of(j * tkp, tkp), tkp)
        update(cp_ref[rs, :], rp_ref[rs, :])
        return carry
    lax.fori_loop(0, cp_ref.shape[0] // tkp, past_body, 0)
    update(cn_ref[...], rn_ref[:, :MLA_ROPE])

    acc = acc_scr[...]
    lat = (acc[:, :MLA_KV_LORA] / jnp.tile(acc[:, MLA_KV_LORA:], (1, MLA_KV_LORA // LANES))).astype(BF16)
    for h in range(MLA_HEADS):
        o_ref[:, h * MLA_V:(h + 1) * MLA_V] = _dot(lat[h * tq:(h + 1) * tq, :],
                                                   wuv_ref[:, h * MLA_V:(h + 1) * MLA_V]).astype(o_ref.dtype)


def _mla_absorbed(q, ckv_past, kpe_past, ckv_new, kpe_new, wuk_t, wuv, n_seq, tq, tkp):
    p = ckv_past.shape[1]
    assert p % tkp == 0
    rows = MLA_HEADS * tq
    return pl.pallas_call(
        functools.partial(_mla_absorbed_kernel, tkp=tkp), name="mla_absorbed", grid=(n_seq,),
        in_specs=[pl.BlockSpec((tq, MLA_HEADS * MLA_QPAD), lambda b: (b, 0)),
                  pl.BlockSpec((None, p, MLA_KV_LORA), lambda b: (b, 0, 0)),
                  pl.BlockSpec((None, p, MLA_ROPE), lambda b: (b, 0, 0)),
                  pl.BlockSpec((tq, MLA_KV_LORA), lambda b: (b, 0)),
                  pl.BlockSpec((tq, LANES), lambda b: (b, 0)),
                  _full_spec(wuk_t), _full_spec(wuv)],
        out_specs=pl.BlockSpec((tq, MLA_HEADS * MLA_V), lambda b: (b, 0)),
        out_shape=jax.ShapeDtypeStruct((n_seq * tq, MLA_HEADS * MLA_V), BF16),
        scratch_shapes=[pltpu.VMEM((rows, LANES), F32), pltpu.VMEM((rows, MLA_KV_LORA + LANES), F32)],
        compiler_params=_params(("parallel",)))(q, ckv_past, kpe_past, ckv_new, kpe_new, wuk_t, wuv)


def _rope_tables(pos):
    half = MLA_ROPE // 2
    inv = ROPE_BASE ** (-jnp.arange(half, dtype=F32) / half)
    ang = pos.astype(F32)[:, None] * inv[None, :]
    cos, sin = jnp.cos(ang), jnp.sin(ang)
    z = jnp.zeros((pos.shape[0], LANES - MLA_ROPE), F32)
    return jnp.concatenate([cos, cos, z], axis=1), jnp.concatenate([-sin, sin, z], axis=1)


def _pad_rows(a, n):
    return jnp.pad(a, ((0, n - a.shape[0]),) + ((0, 0),) * (a.ndim - 1))


def kernel(x_prompt, x_sample, state_hgrn2, cache_fox_k, cache_fox_v, cache_fox_logf, cache_mla_ckv, cache_mla_kpe, meta_tokens, even_w_in, hg_lb_logits, hg_norm_g, fox_forget_bias, even_w_out, mla_w_in, mla_q_norm_g, mla_kv_norm_g, mla_w_uq, mla_w_uk, mla_w_uv, mla_w_out, ln_mix_g, ln_mix_b, ln_ffn_g, ln_ffn_b, router_w, router_bias, moe_w_gate, moe_w_up, moe_w_down):
    B, T, _ = x_prompt.shape
    Bs, Ts, _ = x_sample.shape
    P = cache_fox_k.shape[2]
    RM = B * T
    RS = Bs * Ts
    RSM = -(-(RS + N_META) // LANES) * LANES
    ME = slice(RS, RS + N_META)
    TM_MAIN, TM_MOE, TQ = 512, 1024, 512

    xm = x_prompt.reshape(RM, D_MODEL)
    xs = _pad_rows(jnp.concatenate([x_sample.reshape(RS, D_MODEL), meta_tokens.astype(F32)], axis=0), RSM)

    w_in0 = even_w_in[0]
    n_main = 7 * HG_W
    w_even = w_in0[:, :n_main].astype(BF16)
    w_even_f = jnp.pad(w_in0[:, n_main:], ((0, 0), (0, LANES - FOX_HEADS))).astype(BF16)
    fb_pad = jnp.pad(fox_forget_bias[0][None, :], ((0, 0), (0, LANES - FOX_HEADS)))
    g_hg = hg_norm_g[0].reshape(1, HG_W)
    w_out0 = even_w_out[0].astype(BF16)
    e_mat = ((jnp.arange(HG_SUB * HG_DK)[:, None] // HG_DK) == (jnp.arange(CHUNK)[None, :] % HG_SUB)).astype(BF16)

    w_odd = jnp.pad(mla_w_in[0], ((0, 0), (0, LANES - MLA_ROPE))).astype(BF16)
    gq = mla_q_norm_g[0][None, :]
    gkv = mla_kv_norm_g[0][None, :]
    wuq = mla_w_uq[0].reshape(MLA_Q_LORA, MLA_HEADS, MLA_NOPE + MLA_ROPE)
    wuq = jnp.pad(wuq, ((0, 0), (0, 0), (0, MLA_QPAD - MLA_NOPE - MLA_ROPE)))
    wuq = wuq.reshape(MLA_Q_LORA, MLA_HEADS * MLA_QPAD).astype(BF16)
    w_ukv = jnp.concatenate([mla_w_uk[0].reshape(MLA_KV_LORA, -1), mla_w_uv[0].reshape(MLA_KV_LORA, -1)],
                            axis=1).astype(BF16)
    w_out1 = mla_w_out[0].astype(BF16)

    rw = jnp.pad(router_w, ((0, 0), (0, LANES - N_EXPERTS))).astype(BF16)
    rb = jnp.pad(router_bias.astype(F32)[:, None], ((0, LANES - N_EXPERTS), (0, 0)))
    experts = (moe_w_gate, moe_w_up, moe_w_down)
    row2 = lambda a: a[None, :]

    def ffn(x, acts, w_out, l, tm_mix, tm_moe, routed):
        x1, x1b, route = _mix(x, acts, w_out, row2(ln_mix_g[l]), row2(ln_mix_b[l]), rw, rb, tm_mix)
        ln = (row2(ln_ffn_g[l]), row2(ln_ffn_b[l]))
        if routed:
            return _moe_routed(x1, x1b, route, *experts, l, *ln, tm_mix)
        return _moe(x1, route, *experts, l, *ln, tm_moe)

    pm = _even_proj(xm, w_even, w_even_f, hg_lb_logits, fb_pad, TM_MAIN, 0)
    ps = _even_proj(xs, w_even, w_even_f, hg_lb_logits, fb_pad, RSM, 0)
    names = ("hq", "lf", "hk", "hv", "hgate", "fq", "fk", "fv", "fk16", "fv16", "flf")
    pm = dict(zip(names, pm))
    ps = dict(zip(names, ps))

    hg_keys = ("hq", "lf", "hk", "hv", "hgate", "flf")
    meta_in = [_pad_rows(ps[n][ME], CHUNK) for n in hg_keys]
    zero_s = jnp.zeros((1, HG_HEADS, HG_DK, HG_DV), F32)
    zero_f = jnp.zeros((1, 1, LANES), F32)
    o_hg_meta, fc_meta, s_meta = _hgrn2(*meta_in, g_hg, e_mat, zero_s, zero_f, 1, CHUNK, 0, CHUNK)
    o_hg_meta, fc_meta = o_hg_meta[:N_META], fc_meta[:N_META]
    f_meta_end = fc_meta[N_META - 1:N_META][None]

    o_hg_m, fc_m, s_main = _hgrn2(*[pm[n] for n in hg_keys], g_hg, e_mat, s_meta, f_meta_end, B, T, 0, 512)

    logf_c = jnp.pad(jnp.transpose(cache_fox_logf[0], (0, 2, 1)), ((0, 0), (0, HG_SUB - FOX_HEADS), (0, 0)))
    fpast = _cumsum_lanes(logf_c.reshape(Bs * HG_SUB, P), 512).reshape(Bs, HG_SUB, P)[:, :FOX_HEADS, :]
    f0_s = jnp.pad(fpast[:, :, P - 1][:, None, :], ((0, 0), (0, 0), (0, LANES - FOX_HEADS)))
    o_hg_s, fc_s, s_samp = _hgrn2(*[ps[n] for n in hg_keys], g_hg, e_mat, state_hgrn2[0], f0_s, Bs, Ts, 0, CHUNK)

    def bias_layouts(fc, n_seq, seq_len):
        f4 = fc[:, :FOX_HEADS].T
        return f4[:, :, None], f4.reshape(FOX_HEADS, n_seq, 1, seq_len)

    fq_m, fk_m = bias_layouts(fc_m, B, T)
    fq_s, fk_s = bias_layouts(fc_s, Bs, Ts)
    fq_t, fk_t = bias_layouts(fc_meta, 1, N_META)

    fox_kw = dict(n_heads=FOX_HEADS, dq=FOX_DH, dk=FOX_DH, dv=FOX_DH, mask_mode="causal")
    meta_past = dict(k=ps["fk16"][ME][None], v=ps["fv16"][ME][None],
                     fk=jnp.transpose(fk_t, (1, 0, 2, 3)), tk=N_META)
    o_fox_m = _flash_tri(pm["fq"], pm["fk16"], pm["fv16"], n_seq=B, seq_len=T, tq=TQ,
                         fq=fq_m, fkn=fk_m, past=meta_past, **fox_kw)
    o_fox_s = _fox_cached(ps["fq"], fq_s, cache_fox_k[0], cache_fox_v[0], fpast[:, :, None, :],
                          ps["fk16"], ps["fv16"], fk_s, Bs, Ts, 1024)
    o_fox_t = _flash(ps["fq"][ME], ps["fk16"][ME], ps["fv16"][ME], n_seq=1, seq_len=N_META, tq=N_META,
                     q_off=0, k_off=0, fq=fq_t, fkn=fk_t, **fox_kw)

    o_hg_small = _pad_rows(jnp.concatenate([o_hg_s, o_hg_meta], axis=0), RSM)
    o_fox_small = _pad_rows(jnp.concatenate([o_fox_s, o_fox_t], axis=0), RSM)
    xm = ffn(xm, [o_hg_m, o_fox_m], w_out0, 0, TM_MOE, TM_MOE, True)
    xs = ffn(xs, [o_hg_small, o_fox_small], w_out0, 0, RSM, RSM, False)

    cos_m, sin_m = _rope_tables(N_META + jnp.arange(T, dtype=jnp.int32))
    pos_small = _pad_rows(jnp.concatenate([jnp.tile(P + jnp.arange(Ts, dtype=jnp.int32), Bs),
                                           jnp.arange(N_META, dtype=jnp.int32)]), RSM)
    cos_s, sin_s = _rope_tables(pos_small)
    qm, ckv_m, kpe_m, kpe16_m = _odd_proj(xm, jnp.tile(cos_m, (B, 1)), jnp.tile(sin_m, (B, 1)),
                                          w_odd, gq, gkv, wuq, TM_MOE)
    qs, ckv_s, kpe_s, kpe16_s = _odd_proj(xs, cos_s, sin_s, w_odd, gq, gkv, wuq, RSM)
    kn_m, vn_m = _kv_expand(ckv_m, w_ukv, 1024)
    kn_s, vn_s = _kv_expand(ckv_s, w_ukv, RSM)
    wuk_t = jnp.transpose(mla_w_uk[0], (1, 0, 2)).astype(BF16)
    wuv = mla_w_uv[0].reshape(MLA_KV_LORA, MLA_HEADS * MLA_V).astype(BF16)

    mla_kw = dict(n_heads=MLA_HEADS, dq=MLA_QPAD, dk=MLA_NOPE, dv=MLA_V)
    meta_past = dict(k=kn_s[ME][None], v=vn_s[ME][None], r=kpe16_s[ME][None], tk=N_META)
    o_m = _flash_tri(qm, kn_m, vn_m, n_seq=B, seq_len=T, tq=TQ, rn=kpe16_m,
                     past=meta_past, mask_mode="chunk", **mla_kw)
    assert P % CHUNK == 0 and Ts <= CHUNK
    o_s = _mla_absorbed(qs, cache_mla_ckv[0], cache_mla_kpe[0], ckv_s, kpe16_s, wuk_t, wuv, Bs, Ts, 1024)
    o_t = _flash(qs[ME], kn_s[ME], vn_s[ME], n_seq=1, seq_len=N_META, tq=N_META, q_off=0, k_off=0,
                 rn=kpe16_s[ME], mask_mode="full", **mla_kw)
    xm = ffn(xm, [o_m], w_out1, 1, TM_MOE, TM_MOE, True)
    xs = ffn(xs, [_pad_rows(jnp.concatenate([o_s, o_t], axis=0), RSM)], w_out1, 1, RSM, RSM, False)

    def with_meta(main, small, *width):
        meta = jnp.broadcast_to(small[ME][None], (B, N_META) + width)
        return jnp.concatenate([meta, main.reshape((B, T) + width)], axis=1)

    y_prompt = xm.reshape(B, T, D_MODEL)
    y_sample = xs[:RS].reshape(Bs, Ts, D_MODEL)
    hg_p = s_main[None]
    fk_p = with_meta(pm["fk"], ps["fk"], FOX_HEADS, FOX_DH)[None]
    fv_p = with_meta(pm["fv"], ps["fv"], FOX_HEADS, FOX_DH)[None]
    flf_p = with_meta(pm["flf"][:, :FOX_HEADS], ps["flf"][:, :FOX_HEADS], FOX_HEADS)[None]
    ckv_p = with_meta(ckv_m, ckv_s, MLA_KV_LORA)[None]
    kpe_p = with_meta(kpe_m, kpe_s, MLA_ROPE)[None]
    hg_s = s_samp[None]
    fk_s_out = ps["fk"][:RS].reshape(1, Bs, Ts, FOX_HEADS, FOX_DH)
    fv_s_out = ps["fv"][:RS].reshape(1, Bs, Ts, FOX_HEADS, FOX_DH)
    flf_s = ps["flf"][:RS, :FOX_HEADS].reshape(1, Bs, Ts, FOX_HEADS)
    ckv_so = ckv_s[:RS].reshape(1, Bs, Ts, MLA_KV_LORA)
    kpe_so = kpe_s[:RS].reshape(1, Bs, Ts, MLA_ROPE)
    return (y_prompt, y_sample, hg_p, fk_p, fv_p, flf_p, ckv_p, kpe_p,
            hg_s, fk_s_out, fv_s_out, flf_s, ckv_so, kpe_so)
```

```python
import functools

import jax
import jax.numpy as jnp
from jax import lax
from jax.experimental import pallas as pl
from jax.experimental.pallas import tpu as pltpu
from jax.experimental.pallas import tpu_sc as plsc

D_MODEL = 1024
CHUNK = 64
N_META = 16
HG_HEADS = 4
HG_DK = 128
HG_DV = 128
HG_W = HG_HEADS * HG_DK
FOX_HEADS = 4
FOX_DH = 128
FOX_W = FOX_HEADS * FOX_DH
MLA_HEADS = 8
MLA_Q_LORA = 512
MLA_KV_LORA = 256
MLA_NOPE = 128
MLA_ROPE = 64
MLA_V = 128
MLA_QPAD = 256
ROPE_BASE = 10000.0
N_EXPERTS = 16
N_GROUPS = 4
EXPERTS_PER_GROUP = 4
D_EXPERT = 256
DEPTH = 2
ALPHA = (2 * DEPTH) ** 0.25
LN_EPS = 1e-5
RMS_EPS = 1e-6

LANES = 128
HG_SUB = 8
HG_GROUP = 4
NEG = -1e30
LOG2E = 1.4426950408889634
F32 = jnp.float32
BF16 = jnp.bfloat16
VMEM_LIMIT = 56 * 1024 * 1024


def _dot(a, b):
    return jnp.dot(a, b, preferred_element_type=F32)


def _dot_nt(a, b):
    return lax.dot_general(a, b, (((1,), (1,)), ((), ())), preferred_element_type=F32)


def _dot_tn(a, b):
    return lax.dot_general(a, b, (((0,), (0,)), ((), ())), preferred_element_type=F32)


def _split3(x):
    hi = x.astype(BF16)
    r = x - hi.astype(F32)
    mid = r.astype(BF16)
    lo = (r - mid.astype(F32)).astype(BF16)
    return hi, mid, lo


def _cumsum_rows(tri, x):
    hi, mid, lo = _split3(x)
    return _dot(tri, hi) + _dot(tri, mid) + _dot(tri, lo)


def _sigmoid(x):
    return 1.0 / (1.0 + jnp.exp(-x))


def _log_sigmoid(x):
    return jnp.minimum(x, 0.0) - jnp.log(1.0 + jnp.exp(-jnp.abs(x)))


def _layer_norm(x, g, b):
    mu = jnp.mean(x, axis=-1, keepdims=True)
    xc = x - mu
    var = jnp.mean(xc * xc, axis=-1, keepdims=True)
    return xc * lax.rsqrt(var + LN_EPS) * g + b


def _rms_norm(x, g):
    return x * lax.rsqrt(jnp.mean(x * x, axis=-1, keepdims=True) + RMS_EPS) * g


def _params(sem):
    return pltpu.CompilerParams(dimension_semantics=sem, vmem_limit_bytes=VMEM_LIMIT)


def _full_spec(a):
    nd = a.ndim
    return pl.BlockSpec(a.shape, lambda *_: (0,) * nd)


def _row_call(kernel, name, rows, tm, row_ins, full_ins, outs, scratch=()):
    assert rows % tm == 0
    in_specs = [pl.BlockSpec((tm, a.shape[1]), lambda i: (i, 0)) for a in row_ins]
    in_specs += [_full_spec(a) for a in full_ins]
    trail = [c if isinstance(c, tuple) else (c,) for c, _ in outs]
    out_specs = [pl.BlockSpec((tm,) + t, lambda i, n=len(t): (i,) + (0,) * n) for t in trail]
    out_shape = [jax.ShapeDtypeStruct((rows,) + t, dt) for t, (_, dt) in zip(trail, outs)]
    return pl.pallas_call(
        kernel, name=name, grid=(rows // tm,), in_specs=in_specs, out_specs=out_specs,
        out_shape=out_shape, scratch_shapes=list(scratch),
        compiler_params=_params(("parallel",)))(*row_ins, *full_ins)


def _even_proj_body(x_ref, w_ref, wf_ref, lbl_ref, fb_ref, hq_ref, lf_ref, hk_ref, hv_ref, hgate_ref,
                    fq_ref, fk16_ref, fv16_ref, flf_ref, layer):
    xb = x_ref[...].astype(BF16)

    def blk(j):
        return _dot(xb, w_ref[:, j * HG_W:(j + 1) * HG_W])

    logits = lbl_ref[...]
    e = jnp.exp(logits - jnp.max(logits, axis=0, keepdims=True))
    lb = jnp.sum(e[:layer + 1], axis=0, keepdims=True) / jnp.sum(e, axis=0, keepdims=True)

    hq_ref[...] = blk(0).astype(BF16)
    zf = blk(1)
    lf_ref[...] = jnp.log(lb + (1.0 - lb) * _sigmoid(zf))
    hk_ref[...] = ((1.0 - lb) * _sigmoid(-zf)).astype(BF16)
    hv_ref[...] = blk(2).astype(BF16)
    hgate_ref[...] = _sigmoid(blk(3)).astype(BF16)
    fq_ref[...] = (blk(4) * (FOX_DH ** -0.5 * LOG2E)).astype(BF16)
    fk = blk(5)
    fk16_ref[...] = fk.astype(BF16)
    fv = blk(6)
    fv16_ref[...] = fv.astype(BF16)
    flf_ref[...] = _log_sigmoid(_dot(xb, wf_ref[...]) + fb_ref[...])
    return fk, fv


def _even_proj_kernel(x_ref, w_ref, wf_ref, lbl_ref, fb_ref,
                      hq_ref, lf_ref, hk_ref, hv_ref, hgate_ref,
                      fq_ref, fk_ref, fv_ref, fk16_ref, fv16_ref, flf_ref, *, layer):
    fk, fv = _even_proj_body(x_ref, w_ref, wf_ref, lbl_ref, fb_ref, hq_ref, lf_ref, hk_ref, hv_ref, hgate_ref,
                             fq_ref, fk16_ref, fv16_ref, flf_ref, layer)
    for h in range(FOX_HEADS):
        fk_ref[:, h, :] = fk[:, h * FOX_DH:(h + 1) * FOX_DH]
        fv_ref[:, h, :] = fv[:, h * FOX_DH:(h + 1) * FOX_DH]


_EVEN_OUTS = [(HG_W, BF16), (HG_W, F32), (HG_W, BF16), (HG_W, BF16), (HG_W, BF16), (FOX_W, BF16)]


def _even_proj(x, w_main, w_f, lb_logits, fb_pad, tm, layer):
    rows = x.shape[0]
    outs = _EVEN_OUTS + [((FOX_HEADS, FOX_DH), F32), ((FOX_HEADS, FOX_DH), F32), (FOX_W, BF16), (FOX_W, BF16),
                         (LANES, F32)]
    return _row_call(functools.partial(_even_proj_kernel, layer=layer), "even_proj", rows, tm,
                     [x], [w_main, w_f, lb_logits, fb_pad], outs)


def _even_proj_cache_kernel(x_ref, w_ref, wf_ref, lbl_ref, fb_ref, mk_ref, mv_ref,
                            hq_ref, lf_ref, hk_ref, hv_ref, hgate_ref, fq_ref, fk16_ref, fv16_ref, flf_ref,
                            fk_hbm, fv_hbm, kbuf, vbuf, sem, msem, *, layer, tiles_per_seq):
    i, n = pl.program_id(0), pl.num_programs(0)
    tm = x_ref.shape[0]

    def row_copies(step):
        b = step // tiles_per_seq
        t0 = N_META + (step % tiles_per_seq) * tm
        return [pltpu.make_async_copy(buf.at[:, pl.ds(h * FOX_DH, FOX_DH)], hbm.at[b, pl.ds(t0, tm), h, :], sem.at[a, h])
                for a, (buf, hbm) in enumerate(((kbuf, fk_hbm), (vbuf, fv_hbm))) for h in range(FOX_HEADS)]

    def meta_copies(step):
        b = step // tiles_per_seq
        return [pltpu.make_async_copy(mk_ref, fk_hbm.at[b, pl.ds(0, N_META)], msem.at[0]),
                pltpu.make_async_copy(mv_ref, fv_hbm.at[b, pl.ds(0, N_META)], msem.at[1])]

    fk, fv = _even_proj_body(x_ref, w_ref, wf_ref, lbl_ref, fb_ref, hq_ref, lf_ref, hk_ref, hv_ref, hgate_ref,
                             fq_ref, fk16_ref, fv16_ref, flf_ref, layer)

    @pl.when(i > 0)
    def _():
        for c in row_copies(i - 1):
            c.wait()

    @pl.when((i > 0) & ((i - 1) % tiles_per_seq == 0))
    def _():
        for c in meta_copies(i - 1):
            c.wait()

    kbuf[...] = fk
    vbuf[...] = fv
    for c in row_copies(i):
        c.start()

    @pl.when(i % tiles_per_seq == 0)
    def _():
        for c in meta_copies(i):
            c.start()

    @pl.when(i == n - 1)
    def _():
        for c in row_copies(i):
            c.wait()

    @pl.when((i == n - 1) & (i % tiles_per_seq == 0))
    def _():
        for c in meta_copies(i):
            c.wait()


def _even_proj_cache(x, w_main, w_f, lb_logits, fb_pad, meta_k, meta_v, tm, layer, n_seq, seq_len):
    rows = x.shape[0]
    assert rows == n_seq * seq_len and seq_len % tm == 0
    outs = _EVEN_OUTS + [(FOX_W, BF16), (FOX_W, BF16), (LANES, F32)]
    full = [w_main, w_f, lb_logits, fb_pad, meta_k, meta_v]
    cache = jax.ShapeDtypeStruct((n_seq, N_META + seq_len, FOX_HEADS, FOX_DH), F32)
    return pl.pallas_call(
        functools.partial(_even_proj_cache_kernel, layer=layer, tiles_per_seq=seq_len // tm),
        name="even_proj_cache", grid=(rows // tm,),
        in_specs=[pl.BlockSpec((tm, D_MODEL), lambda i: (i, 0))] + [_full_spec(a) for a in full],
        out_specs=[pl.BlockSpec((tm, c), lambda i: (i, 0)) for c, _ in outs] + [pl.BlockSpec(memory_space=pl.ANY)] * 2,
        out_shape=[jax.ShapeDtypeStruct((rows, c), dt) for c, dt in outs] + [cache, cache],
        scratch_shapes=[pltpu.VMEM((tm, FOX_W), F32), pltpu.VMEM((tm, FOX_W), F32),
                        pltpu.SemaphoreType.DMA((2, FOX_HEADS)), pltpu.SemaphoreType.DMA((2,))],
        compiler_params=_params(("arbitrary",)))(x, *full)


def _bcast_sub(x, j):
    n, c = x.shape
    x3 = x.reshape(n // HG_SUB, HG_SUB, c)
    return jnp.broadcast_to(x3[:, j:j + 1, :], x3.shape).reshape(n, c)


def _level_ref(b, w):
    n, c = b.shape
    parts = [jnp.broadcast_to(b[m * 2 * w + w - 1:m * 2 * w + w, :], (2 * w, c)) for m in range(n // (2 * w))]
    return parts[0] if len(parts) == 1 else jnp.concatenate(parts, axis=0)


def _hgrn2_kernel(q_ref, lf_ref, k_ref, v_ref, gate_ref, flf_ref, g_ref, e_ref, s0_ref, f0_ref,
                  o_ref, fcum_ref, sout_ref, st_scr, fc_scr, *, n_chunks):
    i = pl.program_id(1)
    C = CHUNK

    @pl.when(i == 0)
    def _():
        for h in range(HG_HEADS):
            st_scr[h] = s0_ref[h].T
        fc_scr[...] = f0_ref[...]

    row = lax.broadcasted_iota(jnp.int32, (C, 1), 0)
    col = lax.broadcasted_iota(jnp.int32, (1, C), 1)
    tri = (col <= row).astype(BF16)
    same = lambda w: (row // w) == (col // w)
    levels = (32, 16, 8)

    fc = fc_scr[...]
    for c in range(n_chunks):
        sl = slice(c * C, (c + 1) * C)
        fcum = _cumsum_rows(tri, flf_ref[sl, :]) + fc
        fcum_ref[sl, :] = fcum
        fc = fcum[C - 1:C, :]
    fc_scr[...] = fc

    staged = []
    for c in range(n_chunks):
        sl = slice(c * C, (c + 1) * C)
        per_head = []
        for h0 in range(0, HG_HEADS, HG_GROUP):
            gs = slice(h0 * HG_DK, (h0 + HG_GROUP) * HG_DK)
            b = _cumsum_rows(tri, lf_ref[sl, gs]) * LOG2E
            q = q_ref[sl, gs].astype(F32)
            k = k_ref[sl, gs].astype(F32)
            v = v_ref[sl, gs]
            qb = (q * jnp.exp2(b)).astype(BF16)
            b_last = b[C - 1:C, :]
            kd = (k * jnp.exp2(b_last - b)).astype(BF16)
            e_last = jnp.exp2(b_last)

            pjs = [(jnp.exp2(jnp.where((row % HG_SUB) >= j, b - _bcast_sub(b, j), NEG)) * q
                    * _bcast_sub(k, j)).astype(BF16) for j in range(HG_SUB)]
            lv = []
            for w in levels:
                upper = (row % (2 * w)) >= w
                ew = jnp.exp2(-jnp.abs(b - _level_ref(b, w)))
                lv.append((jnp.where(upper, q * ew, 0.0).astype(BF16), jnp.where(upper, 0.0, k * ew).astype(BF16)))

            for hh in range(HG_GROUP):
                hs = slice(hh * HG_DK, (hh + 1) * HG_DK)
                a = jnp.where(same(HG_SUB), _dot(jnp.concatenate([p[:, hs] for p in pjs], axis=1), e_ref[...]), 0.0)
                for w, (qw, kw) in zip(levels, lv):
                    aw = _dot_nt(qw[:, hs], kw[:, hs])
                    a = a + (aw if 2 * w == C else jnp.where(same(2 * w), aw, 0.0))
                vh = v[:, hs]
                per_head.append((_dot(a.astype(BF16), vh), qb[:, hs], kd[:, hs], e_last[:, hs], vh))
        staged.append(per_head)

    st = [st_scr[h] for h in range(HG_HEADS)]
    for c in range(n_chunks):
        sl = slice(c * C, (c + 1) * C)
        for h, (o_intra, qb_h, kd_h, e_h, vh) in enumerate(staged[c]):
            ho = slice(h * HG_DK, (h + 1) * HG_DK)
            o = o_intra + _dot_nt(qb_h, st[h].astype(BF16))
            st[h] = st[h] * e_h + _dot_tn(vh, kd_h)
            o = _rms_norm(o, g_ref[:, ho])
            o_ref[sl, ho] = (o * gate_ref[sl, ho].astype(F32)).astype(BF16)
    for h in range(HG_HEADS):
        st_scr[h] = st[h]

    @pl.when(i == pl.num_programs(1) - 1)
    def _():
        for h in range(HG_HEADS):
            sout_ref[h] = st_scr[h].T


def _hgrn2(q, lf, k, v, gate, flf, g, e_mat, s0, f0, n_seq, seq_len, row_off, tb):
    assert seq_len % tb == 0 and tb % CHUNK == 0 and row_off % tb == 0
    nb = seq_len // tb
    off = row_off // tb
    per_seq = s0.shape[0] > 1
    rmap = lambda s, i: (off + s * nb + i, 0)
    omap = lambda s, i: (s * nb + i, 0)
    smap = (lambda s, i: (s, 0, 0, 0)) if per_seq else (lambda s, i: (0, 0, 0, 0))
    fmap = (lambda s, i: (s, 0, 0)) if per_seq else (lambda s, i: (0, 0, 0))
    in_specs = [pl.BlockSpec((tb, HG_W), rmap) for _ in range(5)]
    in_specs += [pl.BlockSpec((tb, LANES), rmap), _full_spec(g), _full_spec(e_mat),
                 pl.BlockSpec((None, HG_HEADS, HG_DK, HG_DV), smap), pl.BlockSpec((None, 1, LANES), fmap)]
    out_specs = [pl.BlockSpec((tb, HG_W), omap), pl.BlockSpec((tb, LANES), omap),
                 pl.BlockSpec((None, HG_HEADS, HG_DK, HG_DV), lambda s, i: (s, 0, 0, 0))]
    out_shape = [jax.ShapeDtypeStruct((n_seq * seq_len, HG_W), BF16),
                 jax.ShapeDtypeStruct((n_seq * seq_len, LANES), F32),
                 jax.ShapeDtypeStruct((n_seq, HG_HEADS, HG_DK, HG_DV), F32)]
    scratch = [pltpu.VMEM((HG_HEADS, HG_DV, HG_DK), F32), pltpu.VMEM((1, LANES), F32)]
    return pl.pallas_call(
        functools.partial(_hgrn2_kernel, n_chunks=tb // CHUNK), name="hgrn2",
        grid=(n_seq, nb), in_specs=in_specs, out_specs=out_specs, out_shape=out_shape,
        scratch_shapes=scratch, compiler_params=_params(("parallel", "arbitrary")))(
            q, lf, k, v, gate, flf, g, e_mat, s0, f0)


def _cumsum_kernel(x_ref, tri_ref, o_ref, carry):
    @pl.when(pl.program_id(0) == 0)
    def _():
        carry[...] = jnp.zeros_like(carry)

    hi, mid, lo = _split3(x_ref[...])
    tri = tri_ref[...]
    out = _dot(hi, tri) + _dot(mid, tri) + _dot(lo, tri) + carry[...]
    o_ref[...] = out
    carry[...] = out[:, out.shape[1] - 1:]


def _cumsum_lanes(x, tb):
    r, seq_len = x.shape
    tri = (jnp.arange(tb)[:, None] <= jnp.arange(tb)[None, :]).astype(BF16)
    return pl.pallas_call(
        _cumsum_kernel, name="cumsum", grid=(seq_len // tb,),
        in_specs=[pl.BlockSpec((r, tb), lambda i: (0, i)), _full_spec(tri)],
        out_specs=pl.BlockSpec((r, tb), lambda i: (0, i)),
        out_shape=jax.ShapeDtypeStruct(x.shape, F32),
        scratch_shapes=[pltpu.VMEM((r, 1), F32)],
        compiler_params=_params(("arbitrary",)))(x, tri)


def _flash_kernel(*refs, n_past_blk, tkp, tq, has_bias, has_rope, mask_mode, has_past, past_heads):
    it = iter(refs)
    q_ref = next(it)
    fq_ref = next(it) if has_bias else None
    if has_past:
        kp_ref, vp_ref = next(it), next(it)
        rp_ref = next(it) if has_rope else None
        fkp_ref = next(it) if has_bias else None
    kn_ref, vn_ref = next(it), next(it)
    rn_ref = next(it) if has_rope else None
    fkn_ref = next(it) if has_bias else None
    o_ref = next(it)
    m_scr, acc_scr = next(it), next(it)
    dv = o_ref.shape[1]

    q = q_ref[...]
    m_scr[...] = jnp.full(m_scr.shape, NEG, F32)
    acc_scr[...] = jnp.zeros(acc_scr.shape, F32)
    fq_b = jnp.broadcast_to(fq_ref[...] * LOG2E, (tq, LANES)) if has_bias else None

    def scores(k, r, fk):
        if has_rope:
            k = jnp.concatenate([k, r], axis=1)
        s = _dot_nt(q, k.astype(BF16))
        if has_bias:
            s = s + jnp.tile(fq_b, (1, s.shape[1] // LANES)) if s.shape[1] % LANES == 0 else s + fq_b[:, :1]
            s = s - fk * LOG2E
        return s

    def update(s, v, mask):
        if mask is not None:
            s = jnp.where(mask, s, NEG)
        m_prev = m_scr[...]
        m_new = jnp.maximum(m_prev, jnp.max(s, axis=1, keepdims=True))
        alpha = jnp.exp2(m_prev - m_new)
        if s.shape[1] % LANES == 0:
            p = jnp.exp2(s - jnp.tile(m_new, (1, s.shape[1] // LANES)))
        else:
            p = jnp.exp2(s - m_new[:, :1])
        v1 = jnp.concatenate([v.astype(BF16), jnp.ones((v.shape[0], LANES), BF16)], axis=1)
        acc_scr[...] = jnp.tile(alpha, (1, acc_scr.shape[1] // LANES)) * acc_scr[...] + _dot(p.astype(BF16), v1)
        m_scr[...] = m_new

    past_kv = (lambda ref, rs: ref[rs, pl.program_id(1), :]) if past_heads else (lambda ref, rs: ref[rs, :])

    def past_block(rs):
        return (past_kv(kp_ref, rs), rp_ref[rs, :] if has_rope else None, fkp_ref[:, rs] if has_bias else None)

    def new_block(rs):
        return (kn_ref[rs, :], rn_ref[rs, :] if has_rope else None, fkn_ref[:, rs] if has_bias else None)

    if has_past:
        if n_past_blk == 1:
            update(scores(*past_block(slice(None))), past_kv(vp_ref, slice(None)), None)
        else:
            def past_body(j, carry):
                rs = pl.ds(pl.multiple_of(j * tkp, tkp), tkp)
                update(scores(*past_block(rs)), past_kv(vp_ref, rs), None)
                return carry
            lax.fori_loop(0, n_past_blk, past_body, 0)

    row = lax.broadcasted_iota(jnp.int32, (tq, 1), 0)
    col = lax.broadcasted_iota(jnp.int32, (1, tq), 1)
    if mask_mode == "causal":
        mask = col <= row
    elif mask_mode == "chunk":
        mask = (col // CHUNK) <= (row // CHUNK)
    else:
        mask = None

    update(scores(*new_block(slice(None))), vn_ref[...], mask)
    acc = acc_scr[...]
    o_ref[...] = (acc[:, :dv] / acc[:, dv:]).astype(o_ref.dtype)


def _flash(q, kn, vn, *, n_seq, n_heads, seq_len, tq, dq, dk, dv, q_off, k_off, mask_mode,
           fq=None, fkn=None, rn=None, past=None):
    assert seq_len % tq == 0 and q_off % tq == 0 and k_off % seq_len == 0
    nq = seq_len // tq
    qo = q_off // tq
    ko = k_off // seq_len
    has_bias = fq is not None
    has_rope = rn is not None
    has_past = past is not None
    ins, specs = [q], [pl.BlockSpec((tq, dq), lambda b, h, i: (qo + b * nq + i, h))]
    if has_bias:
        ins.append(fq)
        specs.append(pl.BlockSpec((None, tq, 1), lambda b, h, i: (h, qo + b * nq + i, 0)))
    n_past_blk, tkp, past_heads = 0, 0, False
    if has_past:
        tp = past["k"].shape[1]
        tkp = past["tk"]
        assert tp % tkp == 0
        n_past_blk = tp // tkp
        pb = (lambda b: b) if past["k"].shape[0] > 1 else (lambda b: 0)
        ins += [past["k"], past["v"]]
        past_heads = past["k"].ndim == 4
        if past_heads:
            specs += [pl.BlockSpec((None, tp, n_heads, dk), lambda b, h, i: (pb(b), 0, 0, 0)),
                      pl.BlockSpec((None, tp, n_heads, dv), lambda b, h, i: (pb(b), 0, 0, 0))]
        else:
            specs += [pl.BlockSpec((None, tp, dk), lambda b, h, i: (pb(b), 0, h)),
                      pl.BlockSpec((None, tp, dv), lambda b, h, i: (pb(b), 0, h))]
        if has_rope:
            ins.append(past["r"])
            specs.append(pl.BlockSpec((None, tp, LANES), lambda b, h, i: (pb(b), 0, 0)))
        if has_bias:
            ins.append(past["fk"])
            specs.append(pl.BlockSpec((None, None, 1, tp), lambda b, h, i: (pb(b), h, 0, 0)))
    ins += [kn, vn]
    specs += [pl.BlockSpec((seq_len, dk), lambda b, h, i: (ko + b, h)),
              pl.BlockSpec((seq_len, dv), lambda b, h, i: (ko + b, h))]
    if has_rope:
        ins.append(rn)
        specs.append(pl.BlockSpec((seq_len, LANES), lambda b, h, i: (ko + b, 0)))
    if has_bias:
        ins.append(fkn)
        specs.append(pl.BlockSpec((None, None, 1, seq_len), lambda b, h, i: (h, ko + b, 0, 0)))
    assert nq == 1
    kern = functools.partial(_flash_kernel, n_past_blk=n_past_blk, tkp=tkp, tq=tq, has_bias=has_bias,
                             has_rope=has_rope, mask_mode=mask_mode, has_past=has_past, past_heads=past_heads)
    return pl.pallas_call(
        kern, name="flash", grid=(n_seq, n_heads, nq), in_specs=specs,
        out_specs=pl.BlockSpec((tq, dv), lambda b, h, i: (b * nq + i, h)),
        out_shape=jax.ShapeDtypeStruct((n_seq * seq_len, n_heads * dv), BF16),
        scratch_shapes=[pltpu.VMEM((tq, LANES), F32), pltpu.VMEM((tq, dv + LANES), F32)],
        compiler_params=_params(("parallel", "parallel", "arbitrary")))(*ins)


def _fox_cached_kernel(q_ref, fq_ref, kp_hbm, vp_hbm, fkp_ref, kn_ref, vn_ref, fkn_ref, o_ref,
                       kbuf, vbuf, ksem, vsem, *, tkp, n_blk):
    b = pl.program_id(0)
    tq = q_ref.shape[0]
    row = lax.broadcasted_iota(jnp.int32, (tq, 1), 0)
    col = lax.broadcasted_iota(jnp.int32, (1, tq), 1)
    ones_p = jnp.ones((tkp, LANES), BF16)
    steps = [(h, j) for h in range(FOX_HEADS) for j in range(n_blk)]

    def copies(i):
        h, j = steps[i]
        slot = i % 2
        src = lambda ref: ref.at[b, pl.ds(j * tkp, tkp), h, :]
        return (pltpu.make_async_copy(src(kp_hbm), kbuf.at[slot], ksem.at[slot]),
                pltpu.make_async_copy(src(vp_hbm), vbuf.at[slot], vsem.at[slot]))

    def update(state, s, v1):
        m_prev, acc = state
        m_new = jnp.maximum(m_prev, jnp.max(s, axis=1, keepdims=True))
        p = jnp.exp2(s - m_new)
        return m_new, jnp.exp2(m_prev - m_new) * acc + _dot(p.astype(BF16), v1)

    for c in copies(0):
        c.start()
    state = None
    for i, (h, j) in enumerate(steps):
        hs = slice(h * FOX_DH, (h + 1) * FOX_DH)
        q = q_ref[:, hs]
        fq = fq_ref[h] * LOG2E
        if j == 0:
            state = (jnp.full((tq, 1), NEG, F32), jnp.zeros((tq, FOX_DH + LANES), F32))
        if i + 1 < len(steps):
            for c in copies(i + 1):
                c.start()
        for c in copies(i):
            c.wait()
        slot = i % 2
        s = _dot_nt(q, kbuf[slot].astype(BF16)) + fq - fkp_ref[h, :, j * tkp:(j + 1) * tkp] * LOG2E
        state = update(state, s, jnp.concatenate([vbuf[slot].astype(BF16), ones_p], axis=1))
        if j == n_blk - 1:
            s = _dot_nt(q, kn_ref[:, hs]) + fq - fkn_ref[h] * LOG2E
            s = jnp.where(col <= row, s, NEG)
            _, acc = update(state, s, jnp.concatenate([vn_ref[:, hs], ones_p[:tq]], axis=1))
            o_ref[:, hs] = (acc[:, :FOX_DH] / acc[:, FOX_DH:FOX_DH + 1]).astype(o_ref.dtype)


def _fox_cached(q, fq, kp, vp, fkp, kn, vn, fkn, n_seq, tq, tkp):
    p = kp.shape[1]
    assert p % tkp == 0
    return pl.pallas_call(
        functools.partial(_fox_cached_kernel, tkp=tkp, n_blk=p // tkp), name="fox_cached", grid=(n_seq,),
        in_specs=[pl.BlockSpec((tq, FOX_W), lambda b: (b, 0)),
                  pl.BlockSpec((FOX_HEADS, tq, 1), lambda b: (0, b, 0)),
                  pl.BlockSpec(memory_space=pl.ANY),
                  pl.BlockSpec(memory_space=pl.ANY),
                  pl.BlockSpec((None, FOX_HEADS, 1, p), lambda b: (b, 0, 0, 0)),
                  pl.BlockSpec((tq, FOX_W), lambda b: (b, 0)),
                  pl.BlockSpec((tq, FOX_W), lambda b: (b, 0)),
                  pl.BlockSpec((FOX_HEADS, None, 1, tq), lambda b: (0, b, 0, 0))],
        out_specs=pl.BlockSpec((tq, FOX_W), lambda b: (b, 0)),
        out_shape=jax.ShapeDtypeStruct((n_seq * tq, FOX_W), BF16),
        scratch_shapes=[pltpu.VMEM((2, tkp, FOX_DH), F32), pltpu.VMEM((2, tkp, FOX_DH), F32),
                        pltpu.SemaphoreType.DMA((2,)), pltpu.SemaphoreType.DMA((2,))],
        compiler_params=_params(("arbitrary",)))(q, fq, kp, vp, fkp, kn, vn, fkn)


FLASH_UNROLL_OFF = 14
FLASH_UNROLL_DIAG = 8


def _tri_tables(nq):
    pairs = [(qi, kj) for qi in range(nq) for kj in range(qi)] + [(qi, qi) for qi in range(nq)] + [(0, 0)]
    return (jnp.array([p[0] for p in pairs], jnp.int32), jnp.array([p[1] for p in pairs], jnp.int32))


def _flash_tri_kernel(qt_ref, kt_ref, *refs, tq, nq, has_bias, has_rope, mask_mode):
    it = iter(refs)
    q_ref = next(it)
    fq_ref = next(it) if has_bias else None
    kp_ref, vp_ref = next(it), next(it)
    rp_ref = next(it) if has_rope else None
    fkp_ref = next(it) if has_bias else None
    kn_ref, vn_ref = next(it), next(it)
    rn_ref = next(it) if has_rope else None
    fkn_ref = next(it) if has_bias else None
    o_ref = next(it)
    m_scr, acc_scr, sa_scr, sb_scr = next(it), next(it), next(it), next(it)
    fqb_scr = next(it) if has_bias else None
    dv = o_ref.shape[1]
    n_off = nq * (nq - 1) // 2
    tile = lambda j: pl.ds(pl.multiple_of(j * tq, tq), tq)
    ones = jnp.ones((tq, LANES), BF16)

    kp = kp_ref[...]
    if has_rope:
        kp = jnp.concatenate([kp, rp_ref[...]], axis=1)
    vp1 = jnp.concatenate([vp_ref[...], ones[:vp_ref.shape[0]]], axis=1)
    for i in range(nq):
        rs = slice(i * tq, (i + 1) * tq)
        s = _dot_nt(q_ref[rs, :], kp)
        if has_bias:
            fb = fq_ref[rs, :] * LOG2E
            fqb_scr[rs, :] = jnp.broadcast_to(fb, (tq, LANES))
            s = s + fb - fkp_ref[...] * LOG2E
        m0 = jnp.max(s, axis=1, keepdims=True)
        m_scr[i] = jnp.broadcast_to(m0, (tq, LANES))
        acc_scr[i] = _dot(jnp.exp2(s - m0).astype(BF16), vp1)

    def fill(s_ref, t):
        qs, ks = tile(qt_ref[t]), tile(kt_ref[t])
        k = kn_ref[ks, :]
        if has_rope:
            k = jnp.concatenate([k, rn_ref[ks, :]], axis=1)
        s = _dot_nt(q_ref[qs, :], k)
        if has_bias:
            s = s + jnp.tile(fqb_scr[qs, :], (1, tq // LANES)) - fkn_ref[:, ks] * LOG2E
        s_ref[...] = s

    def drain(s_ref, t, mask):
        qi = qt_ref[t]
        s = s_ref[...]
        if mask is not None:
            s = jnp.where(mask, s, NEG)
        m_prev = m_scr[qi]
        m_new = jnp.maximum(m_prev, jnp.max(s, axis=1, keepdims=True))
        p = jnp.exp2(s - jnp.tile(m_new, (1, tq // LANES)))
        v1 = jnp.concatenate([vn_ref[tile(kt_ref[t]), :], ones], axis=1)
        acc = jnp.tile(jnp.exp2(m_prev - m_new), (1, (dv + LANES) // LANES)) * acc_scr[qi] + _dot(p.astype(BF16), v1)
        return qi, m_new, acc

    def keep(s_ref, t):
        qi, m_new, acc = drain(s_ref, t, None)
        m_scr[qi] = m_new
        acc_scr[qi] = acc

    row = lax.broadcasted_iota(jnp.int32, (tq, 1), 0)
    col = lax.broadcasted_iota(jnp.int32, (1, tq), 1)
    mask = {"causal": col <= row, "chunk": (col // CHUNK) <= (row // CHUNK)}[mask_mode]

    def finish(s_ref, t):
        qi, _, acc = drain(s_ref, t, mask)
        o_ref[tile(qi), :] = (acc[:, :dv] / acc[:, dv:]).astype(o_ref.dtype)

    def pipeline(t0, n, unroll, consume):
        assert n % unroll == 0 and unroll % 2 == 0

        def body(i, carry):
            t = t0 + unroll * i
            for u in range(0, unroll, 2):
                fill(sb_scr, t + u + 1)
                consume(sa_scr, t + u)
                fill(sa_scr, t + u + 2)
                consume(sb_scr, t + u + 1)
            return carry
        lax.fori_loop(0, n // unroll, body, 0)

    fill(sa_scr, 0)
    pipeline(0, n_off, FLASH_UNROLL_OFF, keep)
    pipeline(n_off, nq, FLASH_UNROLL_DIAG, finish)


def _flash_tri(q, kn, vn, *, n_seq, n_heads, seq_len, tq, dq, dk, dv, mask_mode, past, fq=None, fkn=None, rn=None):
    nq = seq_len // tq
    has_bias = fq is not None
    has_rope = rn is not None
    tp = past["k"].shape[1]
    m3 = lambda f: (lambda b, h, qt, kt: f(b, h))
    ins, specs = [q], [pl.BlockSpec((seq_len, dq), m3(lambda b, h: (b, h)))]
    if has_bias:
        ins.append(fq)
        specs.append(pl.BlockSpec((None, seq_len, 1), m3(lambda b, h: (h, b, 0))))
    ins += [past["k"], past["v"]]
    specs += [pl.BlockSpec((None, tp, dk), m3(lambda b, h: (0, 0, h))),
              pl.BlockSpec((None, tp, dv), m3(lambda b, h: (0, 0, h)))]
    if has_rope:
        ins.append(past["r"])
        specs.append(pl.BlockSpec((None, tp, LANES), m3(lambda b, h: (0, 0, 0))))
    if has_bias:
        ins.append(past["fk"])
        specs.append(pl.BlockSpec((None, None, 1, tp), m3(lambda b, h: (0, h, 0, 0))))
    ins += [kn, vn]
    specs += [pl.BlockSpec((seq_len, dk), m3(lambda b, h: (b, h))),
              pl.BlockSpec((seq_len, dv), m3(lambda b, h: (b, h)))]
    if has_rope:
        ins.append(rn)
        specs.append(pl.BlockSpec((seq_len, LANES), m3(lambda b, h: (b, 0))))
    if has_bias:
        ins.append(fkn)
        specs.append(pl.BlockSpec((None, None, 1, seq_len), m3(lambda b, h: (h, b, 0, 0))))
    scratch = [pltpu.VMEM((nq, tq, LANES), F32), pltpu.VMEM((nq, tq, dv + LANES), F32),
               pltpu.VMEM((tq, tq), F32), pltpu.VMEM((tq, tq), F32)]
    if has_bias:
        scratch.append(pltpu.VMEM((seq_len, LANES), F32))
    grid_spec = pltpu.PrefetchScalarGridSpec(
        num_scalar_prefetch=2, grid=(n_seq, n_heads), in_specs=specs,
        out_specs=pl.BlockSpec((seq_len, dv), m3(lambda b, h: (b, h))), scratch_shapes=scratch)
    kern = functools.partial(_flash_tri_kernel, tq=tq, nq=nq, has_bias=has_bias, has_rope=has_rope,
                             mask_mode=mask_mode)
    return pl.pallas_call(
        kern, name="flash_tri", grid_spec=grid_spec,
        out_shape=jax.ShapeDtypeStruct((n_seq * seq_len, n_heads * dv), BF16),
        compiler_params=_params(("parallel", "arbitrary")))(*_tri_tables(nq), *ins)


def _route(sc, sb):
    def top2_sum(v):
        a, b, c, d = v
        a, b = jnp.maximum(a, b), jnp.minimum(a, b)
        c, d = jnp.maximum(c, d), jnp.minimum(c, d)
        hi, lo2 = jnp.maximum(a, c), jnp.minimum(a, c)
        return hi + jnp.maximum(lo2, jnp.maximum(b, d))

    gs = [top2_sum(sb[g * EXPERTS_PER_GROUP:(g + 1) * EXPERTS_PER_GROUP]) for g in range(N_GROUPS)]
    best_v, best_g = gs[0], jnp.zeros(gs[0].shape, jnp.int32)
    for g in range(1, N_GROUPS):
        upd = gs[g] > best_v
        best_v = jnp.where(upd, gs[g], best_v)
        best_g = jnp.where(upd, g, best_g)
    masked = [jnp.where(best_g == (e // EXPERTS_PER_GROUP), sb[e], -jnp.inf) for e in range(N_EXPERTS)]

    def argmax_first(vals, exclude=None):
        bv = jnp.full(vals[0].shape, -jnp.inf, F32)
        bi = jnp.full(vals[0].shape, -1, jnp.int32)
        for e, v in enumerate(vals):
            upd = v > bv
            if exclude is not None:
                upd = upd & (exclude != e)
            bv = jnp.where(upd, v, bv)
            bi = jnp.where(upd, e, bi)
        return bi

    i1 = argmax_first(masked)
    i2 = argmax_first(masked, exclude=i1)
    w1 = sum(jnp.where(i1 == e, sc[e], 0.0) for e in range(N_EXPERTS))
    w2 = sum(jnp.where(i2 == e, sc[e], 0.0) for e in range(N_EXPERTS))
    tot = w1 + w2
    w1, w2 = w1 / tot, w2 / tot
    comb = [jnp.where(i1 == e, w1, 0.0) + jnp.where(i2 == e, w2, 0.0) for e in range(N_EXPERTS)]
    return comb + [i1.astype(F32), i2.astype(F32), w1, w2]


def _mix_kernel(*refs, n_act):
    x_ref = refs[0]
    a_refs = refs[1:1 + n_act]
    w_ref, g_ref, b_ref, rw_ref, rb_ref, x1_ref, x1p_ref, comb_ref, ct_scr = refs[1 + n_act:]
    half = D_MODEL // 2
    tm = x_ref.shape[0]
    group = MIX_GROUP if tm % MIX_GROUP == 0 else tm
    ct_scr[...] = jnp.zeros(ct_scr.shape, F32)
    for r0 in range(0, tm, group):
        rs = slice(r0, r0 + group)
        ys = []
        for n0 in (0, half):
            y = None
            k0 = 0
            for a_ref in a_refs:
                kw = a_ref.shape[1]
                part = _dot(a_ref[rs, :], w_ref[k0:k0 + kw, n0:n0 + half])
                y = part if y is None else y + part
                k0 += kw
            ys.append(y)
        x1 = _layer_norm(ALPHA * x_ref[rs, :] + jnp.concatenate(ys, axis=1), g_ref[...], b_ref[...])
        x1_ref[rs, :] = x1

        x1p_ref[rs, :] = _pack_pair(x1[:, :half], x1[:, half:])
        logits = _dot(x1.astype(BF16), rw_ref[...])
        scores_t = _sigmoid(logits).T
        sc = [scores_t[e:e + 1, :] for e in range(N_EXPERTS)]
        sb = [sc[e] + rb_ref[e:e + 1, :] for e in range(N_EXPERTS)]
        for r, val in enumerate(_route(sc, sb)):
            ct_scr[r:r + 1, rs] = val
        comb_ref[rs, :] = ct_scr[:, rs].T


def _mix(x, acts, w_out, ln_g, ln_b, rw, rb, tm):
    rows = x.shape[0]
    return _row_call(functools.partial(_mix_kernel, n_act=len(acts)), "mix", rows, tm,
                     [x] + list(acts), [w_out, ln_g, ln_b, rw, rb],
                     [(D_MODEL, F32), (D_MODEL // 2, jnp.uint32), (LANES, F32)],
                     scratch=[pltpu.VMEM((LANES, tm), F32)])


def _moe_kernel(x_ref, comb_ref, wg_ref, wu_ref, wd_ref, g_ref, b_ref, o_ref, xb_scr, acc_scr):
    e = pl.program_id(1)

    @pl.when(e == 0)
    def _():
        xb_scr[...] = x_ref[...].astype(BF16)
        acc_scr[...] = jnp.zeros(acc_scr.shape, F32)

    xb = xb_scr[...]
    lane = lax.broadcasted_iota(jnp.int32, (1, LANES), 1)
    c_e = jnp.sum(jnp.where(lane == e, comb_ref[...], 0.0), axis=1, keepdims=True)
    gate = _dot(xb, wg_ref[...].astype(BF16))
    h = gate * _sigmoid(gate) * _dot(xb, wu_ref[...].astype(BF16))
    acc_scr[...] += _dot((h * c_e).astype(BF16), wd_ref[...].astype(BF16))

    @pl.when(e == N_EXPERTS - 1)
    def _():
        o_ref[...] = _layer_norm(ALPHA * x_ref[...] + acc_scr[...], g_ref[...], b_ref[...])


def _moe(x, comb, wg, wu, wd, layer, ln_g, ln_b, tm):
    rows = x.shape[0]
    assert rows % tm == 0
    return pl.pallas_call(
        _moe_kernel, name="moe", grid=(rows // tm, N_EXPERTS),
        in_specs=[pl.BlockSpec((tm, D_MODEL), lambda i, e: (i, 0)),
                  pl.BlockSpec((tm, LANES), lambda i, e: (i, 0)),
                  pl.BlockSpec((None, None, D_MODEL, D_EXPERT), lambda i, e: (layer, e, 0, 0)),
                  pl.BlockSpec((None, None, D_MODEL, D_EXPERT), lambda i, e: (layer, e, 0, 0)),
                  pl.BlockSpec((None, None, D_EXPERT, D_MODEL), lambda i, e: (layer, e, 0, 0)),
                  _full_spec(ln_g), _full_spec(ln_b)],
        out_specs=pl.BlockSpec((tm, D_MODEL), lambda i, e: (i, 0)),
        out_shape=jax.ShapeDtypeStruct((rows, D_MODEL), F32),
        scratch_shapes=[pltpu.VMEM((tm, D_MODEL), BF16), pltpu.VMEM((tm, D_MODEL), F32)],
        compiler_params=_params(("parallel", "arbitrary")))(x, comb, wg, wu, wd, ln_g, ln_b)


ROUTE_E1, ROUTE_E2, ROUTE_W1, ROUTE_W2 = N_EXPERTS, N_EXPERTS + 1, N_EXPERTS + 2, N_EXPERTS + 3
TE = 1024
SC_WINDOW = 128
RANK_TILE = 1024
MIX_GROUP = 256


def _pack_pair(a, b):
    au = lax.bitcast_convert_type(a.astype(BF16).astype(F32), jnp.uint32)
    bu = lax.bitcast_convert_type(b.astype(BF16).astype(F32), jnp.uint32)
    return (au >> 16) | (bu & jnp.uint32(0xFFFF0000))


def _unpack_pair(w):
    a = lax.bitcast_convert_type(w << 16, F32)
    b = lax.bitcast_convert_type(w & jnp.uint32(0xFFFF0000), F32)
    return a, b


def _rank_kernel(route_ref, pos_ref, texp_ref, nused_ref, cnt_scr, carry_scr, seg_scr, before_scr):
    ph, i = pl.program_id(0), pl.program_id(1)
    T = route_ref.shape[0]
    lane = lax.broadcasted_iota(jnp.int32, (1, LANES), 1)
    lane_f = lane.astype(F32)
    r = route_ref[...]
    e1, e2 = r[:, ROUTE_E1:ROUTE_E1 + 1], r[:, ROUTE_E2:ROUTE_E2 + 1]
    m1, m2 = lane_f == e1, lane_f == e2
    m = jnp.where(m1 | m2, 1.0, 0.0)
    colsum = jnp.sum(m, axis=0, keepdims=True)

    @pl.when((ph == 0) & (i == 0))
    def _():
        cnt_scr[...] = jnp.zeros(cnt_scr.shape, F32)

    @pl.when(ph == 0)
    def _():
        cnt_scr[...] += colsum

    @pl.when((ph == 1) & (i == 0))
    def _():
        cnt = cnt_scr[...].astype(jnp.int32)
        padded = (((cnt + (TE - 1)) // TE) * TE).astype(F32)
        rr = lax.broadcasted_iota(jnp.int32, (LANES, 1), 0)
        upper = (rr < lane).astype(BF16)
        hi, mid, lo = _split3(jnp.broadcast_to(padded, (HG_SUB, LANES)))
        seg = (_dot(hi, upper) + _dot(mid, upper) + _dot(lo, upper))[:1, :]
        seg_scr[...] = seg
        carry_scr[...] = jnp.zeros(carry_scr.shape, F32)
        seg_end = seg + padded
        tile_row = lax.broadcasted_iota(jnp.int32, texp_ref.shape, 1).astype(F32) * float(TE)
        te_acc = jnp.zeros(texp_ref.shape, jnp.int32)
        for e in range(N_EXPERTS):
            te_acc = te_acc + jnp.where(seg_end[:, e:e + 1] <= tile_row, 1, 0)
        texp_ref[...] = jnp.minimum(te_acc, N_EXPERTS - 1)
        nused_ref[...] = jnp.broadcast_to(seg_end[:, N_EXPERTS - 1:N_EXPERTS] / float(TE), nused_ref.shape).astype(jnp.int32)

    @pl.when((ph == 1) & (i == 0))
    def _():
        row = lax.broadcasted_iota(jnp.int32, (T, 1), 0)
        col = lax.broadcasted_iota(jnp.int32, (1, T), 1)
        before_scr[...] = (col < row).astype(BF16)

    @pl.when(ph == 1)
    def _():
        cum = _dot(before_scr[...], m.astype(BF16)) + carry_scr[...] + seg_scr[...]
        p1 = jnp.sum(jnp.where(m1, cum, 0.0), axis=1, keepdims=True)
        p2 = jnp.sum(jnp.where(m2, cum, 0.0), axis=1, keepdims=True)
        pos_ref[...] = jnp.where(lane == 0, p1, jnp.where(lane == 1, p2, 0.0)).astype(jnp.int32)
        carry_scr[...] += colsum


def _rank(route, n_tiles, tm):
    rows = route.shape[0]
    nb = rows // tm
    nt_pad = -(-n_tiles // LANES) * LANES
    return pl.pallas_call(
        _rank_kernel, name="rank", grid=(2, nb),
        in_specs=[pl.BlockSpec((tm, LANES), lambda ph, i: (i, 0))],
        out_specs=[pl.BlockSpec((tm, LANES), lambda ph, i: (i * ph, 0)),
                   pl.BlockSpec((1, nt_pad), lambda ph, i: (0, 0)),
                   pl.BlockSpec((1, LANES), lambda ph, i: (0, 0))],
        out_shape=[jax.ShapeDtypeStruct((rows, LANES), jnp.int32),
                   jax.ShapeDtypeStruct((1, nt_pad), jnp.int32),
                   jax.ShapeDtypeStruct((1, LANES), jnp.int32)],
        scratch_shapes=[pltpu.VMEM((1, LANES), F32), pltpu.VMEM((1, LANES), F32), pltpu.VMEM((1, LANES), F32),
                        pltpu.VMEM((tm, tm), BF16)],
        compiler_params=_params(("arbitrary", "arbitrary")))(route)


def _sc_mesh():
    return plsc.VectorSubcoreMesh(core_axis_name="c", subcore_axis_name="s")


def _sc_scatter_rows(x, idx, n_out):
    rows, d = x.shape
    mesh = _sc_mesh()
    n_workers = mesh.num_cores * mesh.num_subcores
    steps = idx.shape[1] // SC_WINDOW // n_workers
    assert steps * SC_WINDOW * n_workers == idx.shape[1] and rows % SC_WINDOW == 0

    @functools.partial(pl.kernel, out_type=jax.ShapeDtypeStruct((n_out, d), x.dtype), mesh=mesh,
                       scratch_types=[pltpu.VMEM((1, SC_WINDOW), jnp.int32), pltpu.VMEM((SC_WINDOW, d), x.dtype)])
    def scatter(x_hbm, i_hbm, o_hbm, i_vmem, buf):
        first = (lax.axis_index("c") * mesh.num_subcores + lax.axis_index("s")) * steps

        @pl.loop(0, steps)
        def _(t):
            off = (first + t) * SC_WINDOW
            pltpu.sync_copy(i_hbm.at[:, pl.ds(off, SC_WINDOW)], i_vmem)
            pltpu.sync_copy(x_hbm.at[pl.ds(off % rows, SC_WINDOW)], buf)
            pltpu.sync_copy(buf, o_hbm.at[i_vmem.at[0]])

    return scatter(x, idx)


def _sc_gather_rows(x, idx):
    d = x.shape[1]
    n = idx.shape[1]
    mesh = _sc_mesh()
    n_workers = mesh.num_cores * mesh.num_subcores
    steps = n // SC_WINDOW // n_workers
    assert steps * SC_WINDOW * n_workers == n

    @functools.partial(pl.kernel, out_type=jax.ShapeDtypeStruct((n, d), x.dtype), mesh=mesh,
                       scratch_types=[pltpu.VMEM((1, SC_WINDOW), jnp.int32), pltpu.VMEM((SC_WINDOW, d), x.dtype)])
    def gather(x_hbm, i_hbm, o_hbm, i_vmem, buf):
        first = (lax.axis_index("c") * mesh.num_subcores + lax.axis_index("s")) * steps

        @pl.loop(0, steps)
        def _(t):
            off = (first + t) * SC_WINDOW
            pltpu.sync_copy(i_hbm.at[:, pl.ds(off, SC_WINDOW)], i_vmem)
            pltpu.sync_copy(x_hbm.at[i_vmem.at[0]], buf)
            pltpu.sync_copy(buf, o_hbm.at[pl.ds(off, SC_WINDOW)])

    return gather(x, idx)


def _gmm_kernel(texp_ref, nused_ref, x_ref, wg_ref, wu_ref, wd_ref, o_ref):
    @pl.when(pl.program_id(0) < nused_ref[0])
    def _():
        a, b = _unpack_pair(x_ref[...])
        xb = jnp.concatenate([a.astype(BF16), b.astype(BF16)], axis=1)
        gate = _dot(xb, wg_ref[...].astype(BF16))
        h = gate * _sigmoid(gate) * _dot(xb, wu_ref[...].astype(BF16))
        y = _dot(h.astype(BF16), wd_ref[...].astype(BF16))
        o_ref[...] = _pack_pair(y[:, :D_MODEL // 2], y[:, D_MODEL // 2:])


def _gmm(xs, texp, nused, wg, wu, wd, layer):
    rows = xs.shape[0]
    wmap = lambda d, te, nu: (layer, te[d], 0, 0)
    grid_spec = pltpu.PrefetchScalarGridSpec(
        num_scalar_prefetch=2, grid=(rows // TE,),
        in_specs=[pl.BlockSpec((TE, D_MODEL // 2), lambda d, te, nu: (d, 0)),
                  pl.BlockSpec((None, None, D_MODEL, D_EXPERT), wmap),
                  pl.BlockSpec((None, None, D_MODEL, D_EXPERT), wmap),
                  pl.BlockSpec((None, None, D_EXPERT, D_MODEL), wmap)],
        out_specs=pl.BlockSpec((TE, D_MODEL // 2), lambda d, te, nu: (d, 0)))
    return pl.pallas_call(
        _gmm_kernel, name="gmm", grid_spec=grid_spec,
        out_shape=jax.ShapeDtypeStruct((rows, D_MODEL // 2), jnp.uint32),
        compiler_params=_params(("arbitrary",)))(texp, nused, xs, wg, wu, wd)


def _combine_kernel(x_ref, g0_ref, g1_ref, route_ref, g_ref, b_ref, o_ref):
    r = route_ref[...]
    y0 = jnp.concatenate(_unpack_pair(g0_ref[...]), axis=1)
    y1 = jnp.concatenate(_unpack_pair(g1_ref[...]), axis=1)
    f = y0 * r[:, ROUTE_W1:ROUTE_W1 + 1] + y1 * r[:, ROUTE_W2:ROUTE_W2 + 1]
    o_ref[...] = _layer_norm(ALPHA * x_ref[...] + f, g_ref[...], b_ref[...])


def _combine(x, g, route, ln_g, ln_b, tm):
    rows = x.shape[0]
    nb = rows // tm
    return pl.pallas_call(
        _combine_kernel, name="combine", grid=(nb,),
        in_specs=[pl.BlockSpec((tm, D_MODEL), lambda i: (i, 0)),
                  pl.BlockSpec((tm, D_MODEL // 2), lambda i: (i, 0)),
                  pl.BlockSpec((tm, D_MODEL // 2), lambda i: (nb + i, 0)),
                  pl.BlockSpec((tm, LANES), lambda i: (i, 0)), _full_spec(ln_g), _full_spec(ln_b)],
        out_specs=pl.BlockSpec((tm, D_MODEL), lambda i: (i, 0)),
        out_shape=jax.ShapeDtypeStruct((rows, D_MODEL), F32),
        compiler_params=_params(("parallel",)))(x, g, g, route, ln_g, ln_b)


def _moe_routed(x1, x1b, route, wg, wu, wd, layer, ln_g, ln_b, tm):
    rows = x1.shape[0]
    n_rows = 2 * rows + N_EXPERTS * TE
    pos, texp, nused = _rank(route, n_rows // TE, RANK_TILE)
    idx = jnp.concatenate([pos[:, 0], pos[:, 1]])[None, :]
    xs = _sc_scatter_rows(x1b, idx, n_rows)
    ys = _gmm(xs, texp[0, :n_rows // TE], nused[0, :1], wg, wu, wd, layer)
    g = _sc_gather_rows(ys, idx)
    return _combine(x1, g, route, ln_g, ln_b, tm)


def _rope128(x, cos_t, sin_t):
    lane = lax.broadcasted_iota(jnp.int32, (1, LANES), 1)
    half = MLA_ROPE // 2
    swapped = jnp.where(lane < half, pltpu.roll(x, LANES - half, axis=1), pltpu.roll(x, half, axis=1))
    return x * cos_t + swapped * sin_t


def _odd_proj_kernel(x_ref, cos_ref, sin_ref, w_ref, gq_ref, gkv_ref, wuq_ref,
                     q_ref, ckv_ref, kpe_ref, kpe16_ref):
    tm = x_ref.shape[0]
    group = MIX_GROUP if tm % MIX_GROUP == 0 else tm
    scale = (MLA_NOPE + MLA_ROPE) ** -0.5 * LOG2E
    for r0 in range(0, tm, group):
        rs = slice(r0, r0 + group)
        z = _dot(x_ref[rs, :].astype(BF16), w_ref[...])
        cq = _rms_norm(z[:, :MLA_Q_LORA], gq_ref[...])
        ckv_ref[rs, :] = _rms_norm(z[:, MLA_Q_LORA:MLA_Q_LORA + MLA_KV_LORA], gkv_ref[...])
        cos_t, sin_t = cos_ref[rs, :], sin_ref[rs, :]
        kpe = _rope128(z[:, MLA_Q_LORA + MLA_KV_LORA:], cos_t, sin_t)
        kpe_ref[rs, :] = kpe[:, :MLA_ROPE]
        kpe16_ref[rs, :] = kpe.astype(BF16)
        qf = _dot(cq.astype(BF16), wuq_ref[...])
        for h in range(MLA_HEADS):
            c0 = h * MLA_QPAD
            q_ref[rs, c0:c0 + MLA_NOPE] = (qf[:, c0:c0 + MLA_NOPE] * scale).astype(BF16)
            qr = _rope128(qf[:, c0 + MLA_NOPE:c0 + MLA_QPAD], cos_t, sin_t)
            q_ref[rs, c0 + MLA_NOPE:c0 + MLA_QPAD] = (qr * scale).astype(BF16)


def _odd_proj(x, cos_t, sin_t, w_in, gq, gkv, wuq, tm):
    rows = x.shape[0]
    outs = [(MLA_HEADS * MLA_QPAD, BF16), (MLA_KV_LORA, F32), (MLA_ROPE, F32), (LANES, BF16)]
    return _row_call(_odd_proj_kernel, "odd_proj", rows, tm, [x, cos_t, sin_t], [w_in, gq, gkv, wuq], outs)


def _kv_expand_kernel(c_ref, w_ref, k_ref, v_ref):
    kv = _dot(c_ref[...].astype(BF16), w_ref[...])
    n = MLA_HEADS * MLA_NOPE
    k_ref[...] = kv[:, :n].astype(BF16)
    v_ref[...] = kv[:, n:].astype(BF16)


def _kv_expand(ckv, w_ukv, tm):
    rows = ckv.shape[0]
    return _row_call(_kv_expand_kernel, "kv_expand", rows, tm, [ckv], [w_ukv],
                     [(MLA_HEADS * MLA_NOPE, BF16), (MLA_HEADS * MLA_V, BF16)])


def _mla_absorbed_kernel(q_ref, cp_ref, rp_ref, cn_ref, rn_ref, wuk_ref, wuv_ref, o_ref, m_scr, acc_scr, *, tkp):
    tq = q_ref.shape[0]
    rows = MLA_HEADS * tq
    q = q_ref[...]
    qa = []
    for h in range(MLA_HEADS):
        c0 = h * MLA_QPAD
        q_abs = _dot_nt(q[:, c0:c0 + MLA_NOPE], wuk_ref[h])
        qa.append(jnp.concatenate([q_abs.astype(BF16), q[:, c0 + MLA_NOPE:c0 + MLA_NOPE + MLA_ROPE]], axis=1))
    qs = jnp.concatenate(qa, axis=0)
    m_scr[...] = jnp.full(m_scr.shape, NEG, F32)
    acc_scr[...] = jnp.zeros(acc_scr.shape, F32)

    def update(c, r):
        c = c.astype(BF16)
        s = _dot_nt(qs, jnp.concatenate([c, r.astype(BF16)], axis=1))
        m_prev = m_scr[...]
        m_new = jnp.maximum(m_prev, jnp.max(s, axis=1, keepdims=True))
        if s.shape[1] % LANES == 0:
            p = jnp.exp2(s - jnp.tile(m_new, (1, s.shape[1] // LANES)))
        else:
            p = jnp.exp2(s - m_new[:, :1])
        c1 = jnp.concatenate([c, jnp.ones((c.shape[0], LANES), BF16)], axis=1)
        acc_scr[...] = (jnp.tile(jnp.exp2(m_prev - m_new), (1, acc_scr.shape[1] // LANES)) * acc_scr[...]
                        + _dot(p.astype(BF16), c1))
        m_scr[...] = m_new

    def past_body(j, carry):
        rs = pl.ds(pl.multiple_of(j * tkp, tkp), tkp)
        update(cp_ref[rs, :], rp_ref[rs, :])
        return carry
    lax.fori_loop(0, cp_ref.shape[0] // tkp, past_body, 0)
    update(cn_ref[...], rn_ref[:, :MLA_ROPE])

    acc = acc_scr[...]
    lat = (acc[:, :MLA_KV_LORA] / jnp.tile(acc[:, MLA_KV_LORA:], (1, MLA_KV_LORA // LANES))).astype(BF16)
    for h in range(MLA_HEADS):
        o_ref[:, h * MLA_V:(h + 1) * MLA_V] = _dot(lat[h * tq:(h + 1) * tq, :],
                                                   wuv_ref[:, h * MLA_V:(h + 1) * MLA_V]).astype(o_ref.dtype)


def _mla_absorbed(q, ckv_past, kpe_past, ckv_new, kpe_new, wuk_t, wuv, n_seq, tq, tkp):
    p = ckv_past.shape[1]
    assert p % tkp == 0
    rows = MLA_HEADS * tq
    return pl.pallas_call(
        functools.partial(_mla_absorbed_kernel, tkp=tkp), name="mla_absorbed", grid=(n_seq,),
        in_specs=[pl.BlockSpec((tq, MLA_HEADS * MLA_QPAD), lambda b: (b, 0)),
                  pl.BlockSpec((None, p, MLA_KV_LORA), lambda b: (b, 0, 0)),
                  pl.BlockSpec((None, p, MLA_ROPE), lambda b: (b, 0, 0)),
                  pl.BlockSpec((tq, MLA_KV_LORA), lambda b: (b, 0)),
                  pl.BlockSpec((tq, LANES), lambda b: (b, 0)),
                  _full_spec(wuk_t), _full_spec(wuv)],
        out_specs=pl.BlockSpec((tq, MLA_HEADS * MLA_V), lambda b: (b, 0)),
        out_shape=jax.ShapeDtypeStruct((n_seq * tq, MLA_HEADS * MLA_V), BF16),
        scratch_shapes=[pltpu.VMEM((rows, LANES), F32), pltpu.VMEM((rows, MLA_KV_LORA + LANES), F32)],
        compiler_params=_params(("parallel",)))(q, ckv_past, kpe_past, ckv_new, kpe_new, wuk_t, wuv)


def _rope_tables(pos):
    half = MLA_ROPE // 2
    inv = ROPE_BASE ** (-jnp.arange(half, dtype=F32) / half)
    ang = pos.astype(F32)[:, None] * inv[None, :]
    cos, sin = jnp.cos(ang), jnp.sin(ang)
    z = jnp.zeros((pos.shape[0], LANES - MLA_ROPE), F32)
    return jnp.concatenate([cos, cos, z], axis=1), jnp.concatenate([-sin, sin, z], axis=1)


def _pad_rows(a, n):
    return jnp.pad(a, ((0, n - a.shape[0]),) + ((0, 0),) * (a.ndim - 1))


def kernel(x_prompt, x_sample, state_hgrn2, cache_fox_k, cache_fox_v, cache_fox_logf, cache_mla_ckv, cache_mla_kpe, meta_tokens, even_w_in, hg_lb_logits, hg_norm_g, fox_forget_bias, even_w_out, mla_w_in, mla_q_norm_g, mla_kv_norm_g, mla_w_uq, mla_w_uk, mla_w_uv, mla_w_out, ln_mix_g, ln_mix_b, ln_ffn_g, ln_ffn_b, router_w, router_bias, moe_w_gate, moe_w_up, moe_w_down):
    B, T, _ = x_prompt.shape
    Bs, Ts, _ = x_sample.shape
    P = cache_fox_k.shape[2]
    RM = B * T
    RS = Bs * Ts
    RSM = -(-(RS + N_META) // LANES) * LANES
    ME = slice(RS, RS + N_META)
    TM_MAIN, TM_MOE, TQ = 512, 1024, 512

    xm = x_prompt.reshape(RM, D_MODEL)
    xs = _pad_rows(jnp.concatenate([x_sample.reshape(RS, D_MODEL), meta_tokens.astype(F32)], axis=0), RSM)

    w_in0 = even_w_in[0]
    n_main = 7 * HG_W
    w_even = w_in0[:, :n_main].astype(BF16)
    w_even_f = jnp.pad(w_in0[:, n_main:], ((0, 0), (0, LANES - FOX_HEADS))).astype(BF16)
    fb_pad = jnp.pad(fox_forget_bias[0][None, :], ((0, 0), (0, LANES - FOX_HEADS)))
    g_hg = hg_norm_g[0].reshape(1, HG_W)
    w_out0 = even_w_out[0].astype(BF16)
    e_mat = ((jnp.arange(HG_SUB * HG_DK)[:, None] // HG_DK) == (jnp.arange(CHUNK)[None, :] % HG_SUB)).astype(BF16)

    w_odd = jnp.pad(mla_w_in[0], ((0, 0), (0, LANES - MLA_ROPE))).astype(BF16)
    gq = mla_q_norm_g[0][None, :]
    gkv = mla_kv_norm_g[0][None, :]
    wuq = mla_w_uq[0].reshape(MLA_Q_LORA, MLA_HEADS, MLA_NOPE + MLA_ROPE)
    wuq = jnp.pad(wuq, ((0, 0), (0, 0), (0, MLA_QPAD - MLA_NOPE - MLA_ROPE)))
    wuq = wuq.reshape(MLA_Q_LORA, MLA_HEADS * MLA_QPAD).astype(BF16)
    w_ukv = jnp.concatenate([mla_w_uk[0].reshape(MLA_KV_LORA, -1), mla_w_uv[0].reshape(MLA_KV_LORA, -1)],
                            axis=1).astype(BF16)
    w_out1 = mla_w_out[0].astype(BF16)

    rw = jnp.pad(router_w, ((0, 0), (0, LANES - N_EXPERTS))).astype(BF16)
    rb = jnp.pad(router_bias.astype(F32)[:, None], ((0, LANES - N_EXPERTS), (0, 0)))
    experts = (moe_w_gate, moe_w_up, moe_w_down)
    row2 = lambda a: a[None, :]

    def ffn(x, acts, w_out, l, tm_mix, tm_moe, routed):
        x1, x1b, route = _mix(x, acts, w_out, row2(ln_mix_g[l]), row2(ln_mix_b[l]), rw, rb, tm_mix)
        ln = (row2(ln_ffn_g[l]), row2(ln_ffn_b[l]))
        if routed:
            return _moe_routed(x1, x1b, route, *experts, l, *ln, tm_mix)
        return _moe(x1, route, *experts, l, *ln, tm_moe)

    ps = _even_proj(xs, w_even, w_even_f, hg_lb_logits, fb_pad, RSM, 0)
    ps = dict(zip(("hq", "lf", "hk", "hv", "hgate", "fq", "fk", "fv", "fk16", "fv16", "flf"), ps))
    pm = _even_proj_cache(xm, w_even, w_even_f, hg_lb_logits, fb_pad, ps["fk"][ME], ps["fv"][ME], TM_MAIN, 0, B, T)
    pm = dict(zip(("hq", "lf", "hk", "hv", "hgate", "fq", "fk16", "fv16", "flf", "fk_cache", "fv_cache"), pm))

    hg_keys = ("hq", "lf", "hk", "hv", "hgate", "flf")
    meta_in = [_pad_rows(ps[n][ME], CHUNK) for n in hg_keys]
    zero_s = jnp.zeros((1, HG_HEADS, HG_DK, HG_DV), F32)
    zero_f = jnp.zeros((1, 1, LANES), F32)
    o_hg_meta, fc_meta, s_meta = _hgrn2(*meta_in, g_hg, e_mat, zero_s, zero_f, 1, CHUNK, 0, CHUNK)
    o_hg_meta, fc_meta = o_hg_meta[:N_META], fc_meta[:N_META]
    f_meta_end = fc_meta[N_META - 1:N_META][None]

    o_hg_m, fc_m, s_main = _hgrn2(*[pm[n] for n in hg_keys], g_hg, e_mat, s_meta, f_meta_end, B, T, 0, 512)

    logf_c = jnp.pad(jnp.transpose(cache_fox_logf[0], (0, 2, 1)), ((0, 0), (0, HG_SUB - FOX_HEADS), (0, 0)))
    fpast = _cumsum_lanes(logf_c.reshape(Bs * HG_SUB, P), 512).reshape(Bs, HG_SUB, P)[:, :FOX_HEADS, :]
    f0_s = jnp.pad(fpast[:, :, P - 1][:, None, :], ((0, 0), (0, 0), (0, LANES - FOX_HEADS)))
    o_hg_s, fc_s, s_samp = _hgrn2(*[ps[n] for n in hg_keys], g_hg, e_mat, state_hgrn2[0], f0_s, Bs, Ts, 0, CHUNK)

    def bias_layouts(fc, n_seq, seq_len):
        f4 = fc[:, :FOX_HEADS].T
        return f4[:, :, None], f4.reshape(FOX_HEADS, n_seq, 1, seq_len)

    fq_m, fk_m = bias_layouts(fc_m, B, T)
    fq_s, fk_s = bias_layouts(fc_s, Bs, Ts)
    fq_t, fk_t = bias_layouts(fc_meta, 1, N_META)

    fox_kw = dict(n_heads=FOX_HEADS, dq=FOX_DH, dk=FOX_DH, dv=FOX_DH, mask_mode="causal")
    meta_past = dict(k=ps["fk16"][ME][None], v=ps["fv16"][ME][None],
                     fk=jnp.transpose(fk_t, (1, 0, 2, 3)), tk=N_META)
    o_fox_m = _flash_tri(pm["fq"], pm["fk16"], pm["fv16"], n_seq=B, seq_len=T, tq=TQ,
                         fq=fq_m, fkn=fk_m, past=meta_past, **fox_kw)
    o_fox_s = _fox_cached(ps["fq"], fq_s, cache_fox_k[0], cache_fox_v[0], fpast[:, :, None, :],
                          ps["fk16"], ps["fv16"], fk_s, Bs, Ts, 1024)
    o_fox_t = _flash(ps["fq"][ME], ps["fk16"][ME], ps["fv16"][ME], n_seq=1, seq_len=N_META, tq=N_META,
                     q_off=0, k_off=0, fq=fq_t, fkn=fk_t, **fox_kw)

    o_hg_small = _pad_rows(jnp.concatenate([o_hg_s, o_hg_meta], axis=0), RSM)
    o_fox_small = _pad_rows(jnp.concatenate([o_fox_s, o_fox_t], axis=0), RSM)
    xm = ffn(xm, [o_hg_m, o_fox_m], w_out0, 0, TM_MOE, TM_MOE, True)
    xs = ffn(xs, [o_hg_small, o_fox_small], w_out0, 0, RSM, RSM, False)

    cos_m, sin_m = _rope_tables(N_META + jnp.arange(T, dtype=jnp.int32))
    pos_small = _pad_rows(jnp.concatenate([jnp.tile(P + jnp.arange(Ts, dtype=jnp.int32), Bs),
                                           jnp.arange(N_META, dtype=jnp.int32)]), RSM)
    cos_s, sin_s = _rope_tables(pos_small)
    qm, ckv_m, kpe_m, kpe16_m = _odd_proj(xm, jnp.tile(cos_m, (B, 1)), jnp.tile(sin_m, (B, 1)),
                                          w_odd, gq, gkv, wuq, TM_MOE)
    qs, ckv_s, kpe_s, kpe16_s = _odd_proj(xs, cos_s, sin_s, w_odd, gq, gkv, wuq, RSM)
    kn_m, vn_m = _kv_expand(ckv_m, w_ukv, 1024)
    kn_s, vn_s = _kv_expand(ckv_s, w_ukv, RSM)
    wuk_t = jnp.transpose(mla_w_uk[0], (1, 0, 2)).astype(BF16)
    wuv = mla_w_uv[0].reshape(MLA_KV_LORA, MLA_HEADS * MLA_V).astype(BF16)

    mla_kw = dict(n_heads=MLA_HEADS, dq=MLA_QPAD, dk=MLA_NOPE, dv=MLA_V)
    meta_past = dict(k=kn_s[ME][None], v=vn_s[ME][None], r=kpe16_s[ME][None], tk=N_META)
    o_m = _flash_tri(qm, kn_m, vn_m, n_seq=B, seq_len=T, tq=TQ, rn=kpe16_m,
                     past=meta_past, mask_mode="chunk", **mla_kw)
    assert P % CHUNK == 0 and Ts <= CHUNK
    o_s = _mla_absorbed(qs, cache_mla_ckv[0], cache_mla_kpe[0], ckv_s, kpe16_s, wuk_t, wuv, Bs, Ts, 1024)
    o_t = _flash(qs[ME], kn_s[ME], vn_s[ME], n_seq=1, seq_len=N_META, tq=N_META, q_off=0, k_off=0,
                 rn=kpe16_s[ME], mask_mode="full", **mla_kw)
    xm = ffn(xm, [o_m], w_out1, 1, TM_MOE, TM_MOE, True)
    xs = ffn(xs, [_pad_rows(jnp.concatenate([o_s, o_t], axis=0), RSM)], w_out1, 1, RSM, RSM, False)

    def with_meta(main, small, *width):
        meta = jnp.broadcast_to(small[ME][None], (B, N_META) + width)
        return jnp.concatenate([meta, main.reshape((B, T) + width)], axis=1)

    y_prompt = xm.reshape(B, T, D_MODEL)
    y_sample = xs[:RS].reshape(Bs, Ts, D_MODEL)
    hg_p = s_main[None]
    fk_p = pm["fk_cache"][None]
    fv_p = pm["fv_cache"][None]
    flf_p = with_meta(pm["flf"][:, :FOX_HEADS], ps["flf"][:, :FOX_HEADS], FOX_HEADS)[None]
    ckv_p = with_meta(ckv_m, ckv_s, MLA_KV_LORA)[None]
    kpe_p = with_meta(kpe_m, kpe_s, MLA_ROPE)[None]
    hg_s = s_samp[None]
    fk_s_out = ps["fk"][:RS].reshape(1, Bs, Ts, FOX_HEADS, FOX_DH)
    fv_s_out = ps["fv"][:RS].reshape(1, Bs, Ts, FOX_HEADS, FOX_DH)
    flf_s = ps["flf"][:RS, :FOX_HEADS].reshape(1, Bs, Ts, FOX_HEADS)
    ckv_so = ckv_s[:RS].reshape(1, Bs, Ts, MLA_KV_LORA)
    kpe_so = kpe_s[:RS].reshape(1, Bs, Ts, MLA_ROPE)
    return (y_prompt, y_sample, hg_p, fk_p, fv_p, flf_p, ckv_p, kpe_p,
            hg_s, fk_s_out, fv_s_out, flf_s, ckv_so, kpe_so)
```

```python
import functools

import jax
import jax.numpy as jnp
from jax import lax
from jax.experimental import pallas as pl
from jax.experimental.pallas import tpu as pltpu
from jax.experimental.pallas import tpu_sc as plsc

D_MODEL = 1024
CHUNK = 64
N_META = 16
HG_HEADS = 4
HG_DK = 128
HG_DV = 128
HG_W = HG_HEADS * HG_DK
FOX_HEADS = 4
FOX_DH = 128
FOX_W = FOX_HEADS * FOX_DH
MLA_HEADS = 8
MLA_Q_LORA = 512
MLA_KV_LORA = 256
MLA_NOPE = 128
MLA_ROPE = 64
MLA_V = 128
MLA_QPAD = 256
ROPE_BASE = 10000.0
N_EXPERTS = 16
N_GROUPS = 4
EXPERTS_PER_GROUP = 4
D_EXPERT = 256
DEPTH = 2
ALPHA = (2 * DEPTH) ** 0.25
LN_EPS = 1e-5
RMS_EPS = 1e-6

LANES = 128
HG_SUB = 8
HG_GROUP = 4
NEG = -1e30
LOG2E = 1.4426950408889634
F32 = jnp.float32
BF16 = jnp.bfloat16
VMEM_LIMIT = 56 * 1024 * 1024
TILE_PROJ = 512
TILE_FFN = 1024
TILE_ATTN = 512
TILE_HGRN2 = 512
TILE_CUMSUM = 512
TILE_CACHE = 1024


def _dot(a, b):
    return jnp.dot(a, b, preferred_element_type=F32)


def _dot_nt(a, b):
    return lax.dot_general(a, b, (((1,), (1,)), ((), ())), preferred_element_type=F32)


def _dot_tn(a, b):
    return lax.dot_general(a, b, (((0,), (0,)), ((), ())), preferred_element_type=F32)


def _split3(x):
    hi = x.astype(BF16)
    r = x - hi.astype(F32)
    mid = r.astype(BF16)
    lo = (r - mid.astype(F32)).astype(BF16)
    return hi, mid, lo


def _cumsum_rows(tri, x):
    hi, mid, lo = _split3(x)
    return _dot(tri, hi) + _dot(tri, mid) + _dot(tri, lo)


def _sigmoid(x):
    return 1.0 / (1.0 + jnp.exp(-x))


def _log_sigmoid(x):
    return jnp.minimum(x, 0.0) - jnp.log(1.0 + jnp.exp(-jnp.abs(x)))


def _layer_norm(x, g, b):
    mu = jnp.mean(x, axis=-1, keepdims=True)
    xc = x - mu
    var = jnp.mean(xc * xc, axis=-1, keepdims=True)
    return xc * lax.rsqrt(var + LN_EPS) * g + b


def _rms_norm(x, g):
    return x * lax.rsqrt(jnp.mean(x * x, axis=-1, keepdims=True) + RMS_EPS) * g


def _params(sem):
    return pltpu.CompilerParams(dimension_semantics=sem, vmem_limit_bytes=VMEM_LIMIT)


def _full_spec(a):
    nd = a.ndim
    return pl.BlockSpec(a.shape, lambda *_: (0,) * nd)


def _row_call(kernel, name, rows, tm, row_ins, full_ins, outs, scratch=()):
    assert rows % tm == 0
    in_specs = [pl.BlockSpec((tm, a.shape[1]), lambda i: (i, 0)) for a in row_ins]
    in_specs += [_full_spec(a) for a in full_ins]
    trail = [c if isinstance(c, tuple) else (c,) for c, _ in outs]
    out_specs = [pl.BlockSpec((tm,) + t, lambda i, n=len(t): (i,) + (0,) * n) for t in trail]
    out_shape = [jax.ShapeDtypeStruct((rows,) + t, dt) for t, (_, dt) in zip(trail, outs)]
    return pl.pallas_call(
        kernel, name=name, grid=(rows // tm,), in_specs=in_specs, out_specs=out_specs,
        out_shape=out_shape, scratch_shapes=list(scratch),
        compiler_params=_params(("parallel",)))(*row_ins, *full_ins)


def _even_proj_body(x_ref, w_ref, wf_ref, lbl_ref, fb_ref, hq_ref, lf_ref, hk_ref, hv_ref, hgate_ref,
                    fq_ref, fk16_ref, fv16_ref, flf_ref, layer):
    xb = x_ref[...].astype(BF16)

    def blk(j):
        return _dot(xb, w_ref[:, j * HG_W:(j + 1) * HG_W])

    logits = lbl_ref[...]
    e = jnp.exp(logits - jnp.max(logits, axis=0, keepdims=True))
    lb = jnp.sum(e[:layer + 1], axis=0, keepdims=True) / jnp.sum(e, axis=0, keepdims=True)

    hq_ref[...] = blk(0).astype(BF16)
    zf = blk(1)
    lf_ref[...] = jnp.log(lb + (1.0 - lb) * _sigmoid(zf))
    hk_ref[...] = ((1.0 - lb) * _sigmoid(-zf)).astype(BF16)
    hv_ref[...] = blk(2).astype(BF16)
    hgate_ref[...] = _sigmoid(blk(3)).astype(BF16)
    fq_ref[...] = (blk(4) * (FOX_DH ** -0.5 * LOG2E)).astype(BF16)
    fk = blk(5)
    fk16_ref[...] = fk.astype(BF16)
    fv = blk(6)
    fv16_ref[...] = fv.astype(BF16)
    flf_ref[...] = _log_sigmoid(_dot(xb, wf_ref[...]) + fb_ref[...])
    return fk, fv


def _even_proj_kernel(x_ref, w_ref, wf_ref, lbl_ref, fb_ref,
                      hq_ref, lf_ref, hk_ref, hv_ref, hgate_ref,
                      fq_ref, fk_ref, fv_ref, fk16_ref, fv16_ref, flf_ref, *, layer):
    fk, fv = _even_proj_body(x_ref, w_ref, wf_ref, lbl_ref, fb_ref, hq_ref, lf_ref, hk_ref, hv_ref, hgate_ref,
                             fq_ref, fk16_ref, fv16_ref, flf_ref, layer)
    for h in range(FOX_HEADS):
        fk_ref[:, h, :] = fk[:, h * FOX_DH:(h + 1) * FOX_DH]
        fv_ref[:, h, :] = fv[:, h * FOX_DH:(h + 1) * FOX_DH]


_EVEN_OUTS = [(HG_W, BF16), (HG_W, F32), (HG_W, BF16), (HG_W, BF16), (HG_W, BF16), (FOX_W, BF16)]


def _even_proj(x, w_main, w_f, lb_logits, fb_pad, tm, layer):
    rows = x.shape[0]
    outs = _EVEN_OUTS + [((FOX_HEADS, FOX_DH), F32), ((FOX_HEADS, FOX_DH), F32), (FOX_W, BF16), (FOX_W, BF16),
                         (LANES, F32)]
    return _row_call(functools.partial(_even_proj_kernel, layer=layer), "even_proj", rows, tm,
                     [x], [w_main, w_f, lb_logits, fb_pad], outs)


def _even_proj_cache_kernel(x_ref, w_ref, wf_ref, lbl_ref, fb_ref, mk_ref, mv_ref,
                            hq_ref, lf_ref, hk_ref, hv_ref, hgate_ref, fq_ref, fk16_ref, fv16_ref, flf_ref,
                            fk_hbm, fv_hbm, kbuf, vbuf, sem, msem, *, layer, tiles_per_seq):
    i, n = pl.program_id(0), pl.num_programs(0)
    tm = x_ref.shape[0]

    def row_copies(step):
        b = step // tiles_per_seq
        t0 = N_META + (step % tiles_per_seq) * tm
        return [pltpu.make_async_copy(buf.at[:, pl.ds(h * FOX_DH, FOX_DH)], hbm.at[b, pl.ds(t0, tm), h, :], sem.at[a, h])
                for a, (buf, hbm) in enumerate(((kbuf, fk_hbm), (vbuf, fv_hbm))) for h in range(FOX_HEADS)]

    def meta_copies(step):
        b = step // tiles_per_seq
        return [pltpu.make_async_copy(mk_ref, fk_hbm.at[b, pl.ds(0, N_META)], msem.at[0]),
                pltpu.make_async_copy(mv_ref, fv_hbm.at[b, pl.ds(0, N_META)], msem.at[1])]

    fk, fv = _even_proj_body(x_ref, w_ref, wf_ref, lbl_ref, fb_ref, hq_ref, lf_ref, hk_ref, hv_ref, hgate_ref,
                             fq_ref, fk16_ref, fv16_ref, flf_ref, layer)

    @pl.when(i > 0)
    def _():
        for c in row_copies(i - 1):
            c.wait()

    @pl.when((i > 0) & ((i - 1) % tiles_per_seq == 0))
    def _():
        for c in meta_copies(i - 1):
            c.wait()

    kbuf[...] = fk
    vbuf[...] = fv
    for c in row_copies(i):
        c.start()

    @pl.when(i % tiles_per_seq == 0)
    def _():
        for c in meta_copies(i):
            c.start()

    @pl.when(i == n - 1)
    def _():
        for c in row_copies(i):
            c.wait()

    @pl.when((i == n - 1) & (i % tiles_per_seq == 0))
    def _():
        for c in meta_copies(i):
            c.wait()


def _even_proj_cache(x, w_main, w_f, lb_logits, fb_pad, meta_k, meta_v, tm, layer, n_seq, seq_len):
    rows = x.shape[0]
    assert rows == n_seq * seq_len and seq_len % tm == 0
    outs = _EVEN_OUTS + [(FOX_W, BF16), (FOX_W, BF16), (LANES, F32)]
    full = [w_main, w_f, lb_logits, fb_pad, meta_k, meta_v]
    cache = jax.ShapeDtypeStruct((n_seq, N_META + seq_len, FOX_HEADS, FOX_DH), F32)
    return pl.pallas_call(
        functools.partial(_even_proj_cache_kernel, layer=layer, tiles_per_seq=seq_len // tm),
        name="even_proj_cache", grid=(rows // tm,),
        in_specs=[pl.BlockSpec((tm, D_MODEL), lambda i: (i, 0))] + [_full_spec(a) for a in full],
        out_specs=[pl.BlockSpec((tm, c), lambda i: (i, 0)) for c, _ in outs] + [pl.BlockSpec(memory_space=pl.ANY)] * 2,
        out_shape=[jax.ShapeDtypeStruct((rows, c), dt) for c, dt in outs] + [cache, cache],
        scratch_shapes=[pltpu.VMEM((tm, FOX_W), F32), pltpu.VMEM((tm, FOX_W), F32),
                        pltpu.SemaphoreType.DMA((2, FOX_HEADS)), pltpu.SemaphoreType.DMA((2,))],
        compiler_params=_params(("arbitrary",)))(x, *full)


def _bcast_sub(x, j):
    n, c = x.shape
    x3 = x.reshape(n // HG_SUB, HG_SUB, c)
    return jnp.broadcast_to(x3[:, j:j + 1, :], x3.shape).reshape(n, c)


def _level_ref(b, w):
    n, c = b.shape
    parts = [jnp.broadcast_to(b[m * 2 * w + w - 1:m * 2 * w + w, :], (2 * w, c)) for m in range(n // (2 * w))]
    return parts[0] if len(parts) == 1 else jnp.concatenate(parts, axis=0)


def _hgrn2_kernel(q_ref, lf_ref, k_ref, v_ref, gate_ref, flf_ref, g_ref, e_ref, s0_ref, f0_ref,
                  o_ref, fcum_ref, sout_ref, st_scr, fc_scr, *, n_chunks):
    i = pl.program_id(1)
    C = CHUNK

    @pl.when(i == 0)
    def _():
        for h in range(HG_HEADS):
            st_scr[h] = s0_ref[h].T
        fc_scr[...] = f0_ref[...]

    row = lax.broadcasted_iota(jnp.int32, (C, 1), 0)
    col = lax.broadcasted_iota(jnp.int32, (1, C), 1)
    tri = (col <= row).astype(BF16)
    same = lambda w: (row // w) == (col // w)
    levels = (32, 16, 8)

    fc = fc_scr[...]
    for c in range(n_chunks):
        sl = slice(c * C, (c + 1) * C)
        fcum = _cumsum_rows(tri, flf_ref[sl, :]) + fc
        fcum_ref[sl, :] = fcum
        fc = fcum[C - 1:C, :]
    fc_scr[...] = fc

    staged = []
    for c in range(n_chunks):
        sl = slice(c * C, (c + 1) * C)
        per_head = []
        for h0 in range(0, HG_HEADS, HG_GROUP):
            gs = slice(h0 * HG_DK, (h0 + HG_GROUP) * HG_DK)
            b = _cumsum_rows(tri, lf_ref[sl, gs]) * LOG2E
            q = q_ref[sl, gs].astype(F32)
            k = k_ref[sl, gs].astype(F32)
            v = v_ref[sl, gs]
            qb = (q * jnp.exp2(b)).astype(BF16)
            b_last = b[C - 1:C, :]
            kd = (k * jnp.exp2(b_last - b)).astype(BF16)
            e_last = jnp.exp2(b_last)

            pjs = [(jnp.exp2(jnp.where((row % HG_SUB) >= j, b - _bcast_sub(b, j), NEG)) * q
                    * _bcast_sub(k, j)).astype(BF16) for j in range(HG_SUB)]
            lv = []
            for w in levels:
                upper = (row % (2 * w)) >= w
                ew = jnp.exp2(-jnp.abs(b - _level_ref(b, w)))
                lv.append((jnp.where(upper, q * ew, 0.0).astype(BF16), jnp.where(upper, 0.0, k * ew).astype(BF16)))

            for hh in range(HG_GROUP):
                hs = slice(hh * HG_DK, (hh + 1) * HG_DK)
                a = jnp.where(same(HG_SUB), _dot(jnp.concatenate([p[:, hs] for p in pjs], axis=1), e_ref[...]), 0.0)
                for w, (qw, kw) in zip(levels, lv):
                    aw = _dot_nt(qw[:, hs], kw[:, hs])
                    a = a + (aw if 2 * w == C else jnp.where(same(2 * w), aw, 0.0))
                vh = v[:, hs]
                per_head.append((_dot(a.astype(BF16), vh), qb[:, hs], kd[:, hs], e_last[:, hs], vh))
        staged.append(per_head)

    st = [st_scr[h] for h in range(HG_HEADS)]
    for c in range(n_chunks):
        sl = slice(c * C, (c + 1) * C)
        for h, (o_intra, qb_h, kd_h, e_h, vh) in enumerate(staged[c]):
            ho = slice(h * HG_DK, (h + 1) * HG_DK)
            o = o_intra + _dot_nt(qb_h, st[h].astype(BF16))
            st[h] = st[h] * e_h + _dot_tn(vh, kd_h)
            o = _rms_norm(o, g_ref[:, ho])
            o_ref[sl, ho] = (o * gate_ref[sl, ho].astype(F32)).astype(BF16)
    for h in range(HG_HEADS):
        st_scr[h] = st[h]

    @pl.when(i == pl.num_programs(1) - 1)
    def _():
        for h in range(HG_HEADS):
            sout_ref[h] = st_scr[h].T


def _hgrn2(q, lf, k, v, gate, flf, g, e_mat, s0, f0, n_seq, seq_len, row_off, tb):
    assert seq_len % tb == 0 and tb % CHUNK == 0 and row_off % tb == 0
    nb = seq_len // tb
    off = row_off // tb
    per_seq = s0.shape[0] > 1
    rmap = lambda s, i: (off + s * nb + i, 0)
    omap = lambda s, i: (s * nb + i, 0)
    smap = (lambda s, i: (s, 0, 0, 0)) if per_seq else (lambda s, i: (0, 0, 0, 0))
    fmap = (lambda s, i: (s, 0, 0)) if per_seq else (lambda s, i: (0, 0, 0))
    in_specs = [pl.BlockSpec((tb, HG_W), rmap) for _ in range(5)]
    in_specs += [pl.BlockSpec((tb, LANES), rmap), _full_spec(g), _full_spec(e_mat),
                 pl.BlockSpec((None, HG_HEADS, HG_DK, HG_DV), smap), pl.BlockSpec((None, 1, LANES), fmap)]
    out_specs = [pl.BlockSpec((tb, HG_W), omap), pl.BlockSpec((tb, LANES), omap),
                 pl.BlockSpec((None, HG_HEADS, HG_DK, HG_DV), lambda s, i: (s, 0, 0, 0))]
    out_shape = [jax.ShapeDtypeStruct((n_seq * seq_len, HG_W), BF16),
                 jax.ShapeDtypeStruct((n_seq * seq_len, LANES), F32),
                 jax.ShapeDtypeStruct((n_seq, HG_HEADS, HG_DK, HG_DV), F32)]
    scratch = [pltpu.VMEM((HG_HEADS, HG_DV, HG_DK), F32), pltpu.VMEM((1, LANES), F32)]
    return pl.pallas_call(
        functools.partial(_hgrn2_kernel, n_chunks=tb // CHUNK), name="hgrn2",
        grid=(n_seq, nb), in_specs=in_specs, out_specs=out_specs, out_shape=out_shape,
        scratch_shapes=scratch, compiler_params=_params(("parallel", "arbitrary")))(
            q, lf, k, v, gate, flf, g, e_mat, s0, f0)


def _cumsum_kernel(x_ref, tri_ref, o_ref, carry):
    @pl.when(pl.program_id(0) == 0)
    def _():
        carry[...] = jnp.zeros_like(carry)

    hi, mid, lo = _split3(x_ref[...])
    tri = tri_ref[...]
    out = _dot(hi, tri) + _dot(mid, tri) + _dot(lo, tri) + carry[...]
    o_ref[...] = out
    carry[...] = out[:, out.shape[1] - 1:]


def _cumsum_lanes(x, tb):
    r, seq_len = x.shape
    tri = (jnp.arange(tb)[:, None] <= jnp.arange(tb)[None, :]).astype(BF16)
    return pl.pallas_call(
        _cumsum_kernel, name="cumsum", grid=(seq_len // tb,),
        in_specs=[pl.BlockSpec((r, tb), lambda i: (0, i)), _full_spec(tri)],
        out_specs=pl.BlockSpec((r, tb), lambda i: (0, i)),
        out_shape=jax.ShapeDtypeStruct(x.shape, F32),
        scratch_shapes=[pltpu.VMEM((r, 1), F32)],
        compiler_params=_params(("arbitrary",)))(x, tri)


def _flash_kernel(*refs, tq, has_bias, has_rope, mask_mode):
    it = iter(refs)
    q_ref = next(it)
    fq_ref = next(it) if has_bias else None
    kn_ref, vn_ref = next(it), next(it)
    rn_ref = next(it) if has_rope else None
    fkn_ref = next(it) if has_bias else None
    o_ref = next(it)
    m_scr, acc_scr = next(it), next(it)
    dv = o_ref.shape[1]

    q = q_ref[...]
    m_scr[...] = jnp.full(m_scr.shape, NEG, F32)
    acc_scr[...] = jnp.zeros(acc_scr.shape, F32)
    fq_b = jnp.broadcast_to(fq_ref[...] * LOG2E, (tq, LANES)) if has_bias else None

    def scores(k, r, fk):
        if has_rope:
            k = jnp.concatenate([k, r], axis=1)
        s = _dot_nt(q, k.astype(BF16))
        if has_bias:
            s = s + jnp.tile(fq_b, (1, s.shape[1] // LANES)) if s.shape[1] % LANES == 0 else s + fq_b[:, :1]
            s = s - fk * LOG2E
        return s

    def update(s, v, mask):
        if mask is not None:
            s = jnp.where(mask, s, NEG)
        m_prev = m_scr[...]
        m_new = jnp.maximum(m_prev, jnp.max(s, axis=1, keepdims=True))
        alpha = jnp.exp2(m_prev - m_new)
        if s.shape[1] % LANES == 0:
            p = jnp.exp2(s - jnp.tile(m_new, (1, s.shape[1] // LANES)))
        else:
            p = jnp.exp2(s - m_new[:, :1])
        v1 = jnp.concatenate([v.astype(BF16), jnp.ones((v.shape[0], LANES), BF16)], axis=1)
        acc_scr[...] = jnp.tile(alpha, (1, acc_scr.shape[1] // LANES)) * acc_scr[...] + _dot(p.astype(BF16), v1)
        m_scr[...] = m_new

    row = lax.broadcasted_iota(jnp.int32, (tq, 1), 0)
    col = lax.broadcasted_iota(jnp.int32, (1, tq), 1)
    if mask_mode == "causal":
        mask = col <= row
    elif mask_mode == "chunk":
        mask = (col // CHUNK) <= (row // CHUNK)
    else:
        mask = None

    update(scores(kn_ref[...], rn_ref[...] if has_rope else None, fkn_ref[...] if has_bias else None),
           vn_ref[...], mask)
    acc = acc_scr[...]
    o_ref[...] = (acc[:, :dv] / acc[:, dv:]).astype(o_ref.dtype)


def _flash(q, kn, vn, *, n_seq, n_heads, seq_len, dq, dk, dv, mask_mode, fq=None, fkn=None, rn=None):
    tq = seq_len
    has_bias = fq is not None
    has_rope = rn is not None
    ins, specs = [q], [pl.BlockSpec((tq, dq), lambda b, h: (b, h))]
    if has_bias:
        ins.append(fq)
        specs.append(pl.BlockSpec((None, tq, 1), lambda b, h: (h, b, 0)))
    ins += [kn, vn]
    specs += [pl.BlockSpec((seq_len, dk), lambda b, h: (b, h)), pl.BlockSpec((seq_len, dv), lambda b, h: (b, h))]
    if has_rope:
        ins.append(rn)
        specs.append(pl.BlockSpec((seq_len, LANES), lambda b, h: (b, 0)))
    if has_bias:
        ins.append(fkn)
        specs.append(pl.BlockSpec((None, None, 1, seq_len), lambda b, h: (h, b, 0, 0)))
    kern = functools.partial(_flash_kernel, tq=tq, has_bias=has_bias, has_rope=has_rope, mask_mode=mask_mode)
    return pl.pallas_call(
        kern, name="flash", grid=(n_seq, n_heads), in_specs=specs,
        out_specs=pl.BlockSpec((tq, dv), lambda b, h: (b, h)),
        out_shape=jax.ShapeDtypeStruct((n_seq * seq_len, n_heads * dv), BF16),
        scratch_shapes=[pltpu.VMEM((tq, LANES), F32), pltpu.VMEM((tq, dv + LANES), F32)],
        compiler_params=_params(("parallel", "parallel")))(*ins)


def _fox_cached_kernel(q_ref, fq_ref, kp_hbm, vp_hbm, fkp_ref, kn_ref, vn_ref, fkn_ref, o_ref,
                       kbuf, vbuf, ksem, vsem, *, tkp, n_blk):
    b = pl.program_id(0)
    tq = q_ref.shape[0]
    row = lax.broadcasted_iota(jnp.int32, (tq, 1), 0)
    col = lax.broadcasted_iota(jnp.int32, (1, tq), 1)
    ones_p = jnp.ones((tkp, LANES), BF16)
    steps = [(h, j) for h in range(FOX_HEADS) for j in range(n_blk)]

    def copies(i):
        h, j = steps[i]
        slot = i % 2
        src = lambda ref: ref.at[b, pl.ds(j * tkp, tkp), h, :]
        return (pltpu.make_async_copy(src(kp_hbm), kbuf.at[slot], ksem.at[slot]),
                pltpu.make_async_copy(src(vp_hbm), vbuf.at[slot], vsem.at[slot]))

    def update(state, s, v1):
        m_prev, acc = state
        m_new = jnp.maximum(m_prev, jnp.max(s, axis=1, keepdims=True))
        p = jnp.exp2(s - m_new)
        return m_new, jnp.exp2(m_prev - m_new) * acc + _dot(p.astype(BF16), v1)

    for c in copies(0):
        c.start()
    state = None
    for i, (h, j) in enumerate(steps):
        hs = slice(h * FOX_DH, (h + 1) * FOX_DH)
        q = q_ref[:, hs]
        fq = fq_ref[h] * LOG2E
        if j == 0:
            state = (jnp.full((tq, 1), NEG, F32), jnp.zeros((tq, FOX_DH + LANES), F32))
        if i + 1 < len(steps):
            for c in copies(i + 1):
                c.start()
        for c in copies(i):
            c.wait()
        slot = i % 2
        s = _dot_nt(q, kbuf[slot].astype(BF16)) + fq - fkp_ref[h, :, j * tkp:(j + 1) * tkp] * LOG2E
        state = update(state, s, jnp.concatenate([vbuf[slot].astype(BF16), ones_p], axis=1))
        if j == n_blk - 1:
            s = _dot_nt(q, kn_ref[:, hs]) + fq - fkn_ref[h] * LOG2E
            s = jnp.where(col <= row, s, NEG)
            _, acc = update(state, s, jnp.concatenate([vn_ref[:, hs], ones_p[:tq]], axis=1))
            o_ref[:, hs] = (acc[:, :FOX_DH] / acc[:, FOX_DH:FOX_DH + 1]).astype(o_ref.dtype)


def _fox_cached(q, fq, kp, vp, fkp, kn, vn, fkn, n_seq, tq, tkp):
    p = kp.shape[1]
    assert p % tkp == 0
    return pl.pallas_call(
        functools.partial(_fox_cached_kernel, tkp=tkp, n_blk=p // tkp), name="fox_cached", grid=(n_seq,),
        in_specs=[pl.BlockSpec((tq, FOX_W), lambda b: (b, 0)),
                  pl.BlockSpec((FOX_HEADS, tq, 1), lambda b: (0, b, 0)),
                  pl.BlockSpec(memory_space=pl.ANY),
                  pl.BlockSpec(memory_space=pl.ANY),
                  pl.BlockSpec((None, FOX_HEADS, 1, p), lambda b: (b, 0, 0, 0)),
                  pl.BlockSpec((tq, FOX_W), lambda b: (b, 0)),
                  pl.BlockSpec((tq, FOX_W), lambda b: (b, 0)),
                  pl.BlockSpec((FOX_HEADS, None, 1, tq), lambda b: (0, b, 0, 0))],
        out_specs=pl.BlockSpec((tq, FOX_W), lambda b: (b, 0)),
        out_shape=jax.ShapeDtypeStruct((n_seq * tq, FOX_W), BF16),
        scratch_shapes=[pltpu.VMEM((2, tkp, FOX_DH), F32), pltpu.VMEM((2, tkp, FOX_DH), F32),
                        pltpu.SemaphoreType.DMA((2,)), pltpu.SemaphoreType.DMA((2,))],
        compiler_params=_params(("arbitrary",)))(q, fq, kp, vp, fkp, kn, vn, fkn)


FLASH_UNROLL_OFF = 14
FLASH_UNROLL_DIAG = 8


def _tri_tables(nq):
    pairs = [(qi, kj) for qi in range(nq) for kj in range(qi)] + [(qi, qi) for qi in range(nq)] + [(0, 0)]
    return (jnp.array([p[0] for p in pairs], jnp.int32), jnp.array([p[1] for p in pairs], jnp.int32))


def _flash_tri_kernel(qt_ref, kt_ref, *refs, tq, nq, has_bias, has_rope, mask_mode):
    it = iter(refs)
    q_ref = next(it)
    fq_ref = next(it) if has_bias else None
    kp_ref, vp_ref = next(it), next(it)
    rp_ref = next(it) if has_rope else None
    fkp_ref = next(it) if has_bias else None
    kn_ref, vn_ref = next(it), next(it)
    rn_ref = next(it) if has_rope else None
    fkn_ref = next(it) if has_bias else None
    o_ref = next(it)
    m_scr, acc_scr, sa_scr, sb_scr = next(it), next(it), next(it), next(it)
    fqb_scr = next(it) if has_bias else None
    dv = o_ref.shape[1]
    n_off = nq * (nq - 1) // 2
    tile = lambda j: pl.ds(pl.multiple_of(j * tq, tq), tq)
    ones = jnp.ones((tq, LANES), BF16)

    kp = kp_ref[...]
    if has_rope:
        kp = jnp.concatenate([kp, rp_ref[...]], axis=1)
    vp1 = jnp.concatenate([vp_ref[...], ones[:vp_ref.shape[0]]], axis=1)
    for i in range(nq):
        rs = slice(i * tq, (i + 1) * tq)
        s = _dot_nt(q_ref[rs, :], kp)
        if has_bias:
            fb = fq_ref[rs, :] * LOG2E
            fqb_scr[rs, :] = jnp.broadcast_to(fb, (tq, LANES))
            s = s + fb - fkp_ref[...] * LOG2E
        m0 = jnp.max(s, axis=1, keepdims=True)
        m_scr[i] = jnp.broadcast_to(m0, (tq, LANES))
        acc_scr[i] = _dot(jnp.exp2(s - m0).astype(BF16), vp1)

    def fill(s_ref, t):
        qs, ks = tile(qt_ref[t]), tile(kt_ref[t])
        k = kn_ref[ks, :]
        if has_rope:
            k = jnp.concatenate([k, rn_ref[ks, :]], axis=1)
        s = _dot_nt(q_ref[qs, :], k)
        if has_bias:
            s = s + jnp.tile(fqb_scr[qs, :], (1, tq // LANES)) - fkn_ref[:, ks] * LOG2E
        s_ref[...] = s

    def drain(s_ref, t, mask):
        qi = qt_ref[t]
        s = s_ref[...]
        if mask is not None:
            s = jnp.where(mask, s, NEG)
        m_prev = m_scr[qi]
        m_new = jnp.maximum(m_prev, jnp.max(s, axis=1, keepdims=True))
        p = jnp.exp2(s - jnp.tile(m_new, (1, tq // LANES)))
        v1 = jnp.concatenate([vn_ref[tile(kt_ref[t]), :], ones], axis=1)
        acc = jnp.tile(jnp.exp2(m_prev - m_new), (1, (dv + LANES) // LANES)) * acc_scr[qi] + _dot(p.astype(BF16), v1)
        return qi, m_new, acc

    def keep(s_ref, t):
        qi, m_new, acc = drain(s_ref, t, None)
        m_scr[qi] = m_new
        acc_scr[qi] = acc

    row = lax.broadcasted_iota(jnp.int32, (tq, 1), 0)
    col = lax.broadcasted_iota(jnp.int32, (1, tq), 1)
    mask = {"causal": col <= row, "chunk": (col // CHUNK) <= (row // CHUNK)}[mask_mode]

    def finish(s_ref, t):
        qi, _, acc = drain(s_ref, t, mask)
        o_ref[tile(qi), :] = (acc[:, :dv] / acc[:, dv:]).astype(o_ref.dtype)

    def pipeline(t0, n, unroll, consume):
        assert n % unroll == 0 and unroll % 2 == 0

        def body(i, carry):
            t = t0 + unroll * i
            for u in range(0, unroll, 2):
                fill(sb_scr, t + u + 1)
                consume(sa_scr, t + u)
                fill(sa_scr, t + u + 2)
                consume(sb_scr, t + u + 1)
            return carry
        lax.fori_loop(0, n // unroll, body, 0)

    fill(sa_scr, 0)
    pipeline(0, n_off, FLASH_UNROLL_OFF, keep)
    pipeline(n_off, nq, FLASH_UNROLL_DIAG, finish)


def _flash_tri(q, kn, vn, *, n_seq, n_heads, seq_len, tq, dq, dk, dv, mask_mode, past, fq=None, fkn=None, rn=None):
    nq = seq_len // tq
    has_bias = fq is not None
    has_rope = rn is not None
    tp = past["k"].shape[1]
    m3 = lambda f: (lambda b, h, qt, kt: f(b, h))
    ins, specs = [q], [pl.BlockSpec((seq_len, dq), m3(lambda b, h: (b, h)))]
    if has_bias:
        ins.append(fq)
        specs.append(pl.BlockSpec((None, seq_len, 1), m3(lambda b, h: (h, b, 0))))
    ins += [past["k"], past["v"]]
    specs += [pl.BlockSpec((None, tp, dk), m3(lambda b, h: (0, 0, h))),
              pl.BlockSpec((None, tp, dv), m3(lambda b, h: (0, 0, h)))]
    if has_rope:
        ins.append(past["r"])
        specs.append(pl.BlockSpec((None, tp, LANES), m3(lambda b, h: (0, 0, 0))))
    if has_bias:
        ins.append(past["fk"])
        specs.append(pl.BlockSpec((None, None, 1, tp), m3(lambda b, h: (0, h, 0, 0))))
    ins += [kn, vn]
    specs += [pl.BlockSpec((seq_len, dk), m3(lambda b, h: (b, h))),
              pl.BlockSpec((seq_len, dv), m3(lambda b, h: (b, h)))]
    if has_rope:
        ins.append(rn)
        specs.append(pl.BlockSpec((seq_len, LANES), m3(lambda b, h: (b, 0))))
    if has_bias:
        ins.append(fkn)
        specs.append(pl.BlockSpec((None, None, 1, seq_len), m3(lambda b, h: (h, b, 0, 0))))
    scratch = [pltpu.VMEM((nq, tq, LANES), F32), pltpu.VMEM((nq, tq, dv + LANES), F32),
               pltpu.VMEM((tq, tq), F32), pltpu.VMEM((tq, tq), F32)]
    if has_bias:
        scratch.append(pltpu.VMEM((seq_len, LANES), F32))
    grid_spec = pltpu.PrefetchScalarGridSpec(
        num_scalar_prefetch=2, grid=(n_seq, n_heads), in_specs=specs,
        out_specs=pl.BlockSpec((seq_len, dv), m3(lambda b, h: (b, h))), scratch_shapes=scratch)
    kern = functools.partial(_flash_tri_kernel, tq=tq, nq=nq, has_bias=has_bias, has_rope=has_rope,
                             mask_mode=mask_mode)
    return pl.pallas_call(
        kern, name="flash_tri", grid_spec=grid_spec,
        out_shape=jax.ShapeDtypeStruct((n_seq * seq_len, n_heads * dv), BF16),
        compiler_params=_params(("parallel", "arbitrary")))(*_tri_tables(nq), *ins)


def _route(sc, sb):
    def top2_sum(v):
        a, b, c, d = v
        a, b = jnp.maximum(a, b), jnp.minimum(a, b)
        c, d = jnp.maximum(c, d), jnp.minimum(c, d)
        hi, lo2 = jnp.maximum(a, c), jnp.minimum(a, c)
        return hi + jnp.maximum(lo2, jnp.maximum(b, d))

    gs = [top2_sum(sb[g * EXPERTS_PER_GROUP:(g + 1) * EXPERTS_PER_GROUP]) for g in range(N_GROUPS)]
    best_v, best_g = gs[0], jnp.zeros(gs[0].shape, jnp.int32)
    for g in range(1, N_GROUPS):
        upd = gs[g] > best_v
        best_v = jnp.where(upd, gs[g], best_v)
        best_g = jnp.where(upd, g, best_g)
    masked = [jnp.where(best_g == (e // EXPERTS_PER_GROUP), sb[e], -jnp.inf) for e in range(N_EXPERTS)]

    def argmax_first(vals, exclude=None):
        bv = jnp.full(vals[0].shape, -jnp.inf, F32)
        bi = jnp.full(vals[0].shape, -1, jnp.int32)
        for e, v in enumerate(vals):
            upd = v > bv
            if exclude is not None:
                upd = upd & (exclude != e)
            bv = jnp.where(upd, v, bv)
            bi = jnp.where(upd, e, bi)
        return bi

    i1 = argmax_first(masked)
    i2 = argmax_first(masked, exclude=i1)
    w1 = sum(jnp.where(i1 == e, sc[e], 0.0) for e in range(N_EXPERTS))
    w2 = sum(jnp.where(i2 == e, sc[e], 0.0) for e in range(N_EXPERTS))
    tot = w1 + w2
    w1, w2 = w1 / tot, w2 / tot
    comb = [jnp.where(i1 == e, w1, 0.0) + jnp.where(i2 == e, w2, 0.0) for e in range(N_EXPERTS)]
    return comb + [i1.astype(F32), i2.astype(F32), w1, w2]


def _mix_kernel(*refs, n_act):
    x_ref = refs[0]
    a_refs = refs[1:1 + n_act]
    w_ref, g_ref, b_ref, rw_ref, rb_ref, x1_ref, x1p_ref, comb_ref, ct_scr = refs[1 + n_act:]
    half = D_MODEL // 2
    tm = x_ref.shape[0]
    group = MIX_GROUP if tm % MIX_GROUP == 0 else tm
    ct_scr[...] = jnp.zeros(ct_scr.shape, F32)
    for r0 in range(0, tm, group):
        rs = slice(r0, r0 + group)
        ys = []
        for n0 in (0, half):
            y = None
            k0 = 0
            for a_ref in a_refs:
                kw = a_ref.shape[1]
                part = _dot(a_ref[rs, :], w_ref[k0:k0 + kw, n0:n0 + half])
                y = part if y is None else y + part
                k0 += kw
            ys.append(y)
        x1 = _layer_norm(ALPHA * x_ref[rs, :] + jnp.concatenate(ys, axis=1), g_ref[...], b_ref[...])
        x1_ref[rs, :] = x1

        x1p_ref[rs, :] = _pack_pair(x1[:, :half], x1[:, half:])
        logits = _dot(x1.astype(BF16), rw_ref[...])
        scores_t = _sigmoid(logits).T
        sc = [scores_t[e:e + 1, :] for e in range(N_EXPERTS)]
        sb = [sc[e] + rb_ref[e:e + 1, :] for e in range(N_EXPERTS)]
        for r, val in enumerate(_route(sc, sb)):
            ct_scr[r:r + 1, rs] = val
        comb_ref[rs, :] = ct_scr[:, rs].T


def _mix(x, acts, w_out, ln_g, ln_b, rw, rb, tm):
    rows = x.shape[0]
    return _row_call(functools.partial(_mix_kernel, n_act=len(acts)), "mix", rows, tm,
                     [x] + list(acts), [w_out, ln_g, ln_b, rw, rb],
                     [(D_MODEL, F32), (D_MODEL // 2, jnp.uint32), (LANES, F32)],
                     scratch=[pltpu.VMEM((LANES, tm), F32)])


def _moe_kernel(x_ref, comb_ref, wg_ref, wu_ref, wd_ref, g_ref, b_ref, o_ref, xb_scr, acc_scr):
    e = pl.program_id(1)

    @pl.when(e == 0)
    def _():
        xb_scr[...] = x_ref[...].astype(BF16)
        acc_scr[...] = jnp.zeros(acc_scr.shape, F32)

    xb = xb_scr[...]
    lane = lax.broadcasted_iota(jnp.int32, (1, LANES), 1)
    c_e = jnp.sum(jnp.where(lane == e, comb_ref[...], 0.0), axis=1, keepdims=True)
    gate = _dot(xb, wg_ref[...].astype(BF16))
    h = gate * _sigmoid(gate) * _dot(xb, wu_ref[...].astype(BF16))
    acc_scr[...] += _dot((h * c_e).astype(BF16), wd_ref[...].astype(BF16))

    @pl.when(e == N_EXPERTS - 1)
    def _():
        o_ref[...] = _layer_norm(ALPHA * x_ref[...] + acc_scr[...], g_ref[...], b_ref[...])


def _moe(x, comb, wg, wu, wd, layer, ln_g, ln_b, tm):
    rows = x.shape[0]
    assert rows % tm == 0
    return pl.pallas_call(
        _moe_kernel, name="moe", grid=(rows // tm, N_EXPERTS),
        in_specs=[pl.BlockSpec((tm, D_MODEL), lambda i, e: (i, 0)),
                  pl.BlockSpec((tm, LANES), lambda i, e: (i, 0)),
                  pl.BlockSpec((None, None, D_MODEL, D_EXPERT), lambda i, e: (layer, e, 0, 0)),
                  pl.BlockSpec((None, None, D_MODEL, D_EXPERT), lambda i, e: (layer, e, 0, 0)),
                  pl.BlockSpec((None, None, D_EXPERT, D_MODEL), lambda i, e: (layer, e, 0, 0)),
                  _full_spec(ln_g), _full_spec(ln_b)],
        out_specs=pl.BlockSpec((tm, D_MODEL), lambda i, e: (i, 0)),
        out_shape=jax.ShapeDtypeStruct((rows, D_MODEL), F32),
        scratch_shapes=[pltpu.VMEM((tm, D_MODEL), BF16), pltpu.VMEM((tm, D_MODEL), F32)],
        compiler_params=_params(("parallel", "arbitrary")))(x, comb, wg, wu, wd, ln_g, ln_b)


ROUTE_E1, ROUTE_E2, ROUTE_W1, ROUTE_W2 = N_EXPERTS, N_EXPERTS + 1, N_EXPERTS + 2, N_EXPERTS + 3
TE = 1024
SC_WINDOW = 128
RANK_TILE = 1024
MIX_GROUP = 256


def _pack_pair(a, b):
    au = lax.bitcast_convert_type(a.astype(BF16).astype(F32), jnp.uint32)
    bu = lax.bitcast_convert_type(b.astype(BF16).astype(F32), jnp.uint32)
    return (au >> 16) | (bu & jnp.uint32(0xFFFF0000))


def _unpack_pair(w):
    a = lax.bitcast_convert_type(w << 16, F32)
    b = lax.bitcast_convert_type(w & jnp.uint32(0xFFFF0000), F32)
    return a, b


def _rank_kernel(route_ref, pos_ref, texp_ref, nused_ref, cnt_scr, carry_scr, seg_scr, before_scr):
    ph, i = pl.program_id(0), pl.program_id(1)
    T = route_ref.shape[0]
    lane = lax.broadcasted_iota(jnp.int32, (1, LANES), 1)
    lane_f = lane.astype(F32)
    r = route_ref[...]
    e1, e2 = r[:, ROUTE_E1:ROUTE_E1 + 1], r[:, ROUTE_E2:ROUTE_E2 + 1]
    m1, m2 = lane_f == e1, lane_f == e2
    m = jnp.where(m1 | m2, 1.0, 0.0)
    colsum = jnp.sum(m, axis=0, keepdims=True)

    @pl.when((ph == 0) & (i == 0))
    def _():
        cnt_scr[...] = jnp.zeros(cnt_scr.shape, F32)

    @pl.when(ph == 0)
    def _():
        cnt_scr[...] += colsum

    @pl.when((ph == 1) & (i == 0))
    def _():
        cnt = cnt_scr[...].astype(jnp.int32)
        padded = (((cnt + (TE - 1)) // TE) * TE).astype(F32)
        rr = lax.broadcasted_iota(jnp.int32, (LANES, 1), 0)
        upper = (rr < lane).astype(BF16)
        hi, mid, lo = _split3(jnp.broadcast_to(padded, (HG_SUB, LANES)))
        seg = (_dot(hi, upper) + _dot(mid, upper) + _dot(lo, upper))[:1, :]
        seg_scr[...] = seg
        carry_scr[...] = jnp.zeros(carry_scr.shape, F32)
        seg_end = seg + padded
        tile_row = lax.broadcasted_iota(jnp.int32, texp_ref.shape, 1).astype(F32) * float(TE)
        te_acc = jnp.zeros(texp_ref.shape, jnp.int32)
        for e in range(N_EXPERTS):
            te_acc = te_acc + jnp.where(seg_end[:, e:e + 1] <= tile_row, 1, 0)
        texp_ref[...] = jnp.minimum(te_acc, N_EXPERTS - 1)
        nused_ref[...] = jnp.broadcast_to(seg_end[:, N_EXPERTS - 1:N_EXPERTS] / float(TE), nused_ref.shape).astype(jnp.int32)

    @pl.when((ph == 1) & (i == 0))
    def _():
        row = lax.broadcasted_iota(jnp.int32, (T, 1), 0)
        col = lax.broadcasted_iota(jnp.int32, (1, T), 1)
        before_scr[...] = (col < row).astype(BF16)

    @pl.when(ph == 1)
    def _():
        cum = _dot(before_scr[...], m.astype(BF16)) + carry_scr[...] + seg_scr[...]
        p1 = jnp.sum(jnp.where(m1, cum, 0.0), axis=1, keepdims=True)
        p2 = jnp.sum(jnp.where(m2, cum, 0.0), axis=1, keepdims=True)
        pos_ref[...] = jnp.where(lane == 0, p1, jnp.where(lane == 1, p2, 0.0)).astype(jnp.int32)
        carry_scr[...] += colsum


def _rank(route, n_tiles, tm):
    rows = route.shape[0]
    nb = rows // tm
    nt_pad = -(-n_tiles // LANES) * LANES
    return pl.pallas_call(
        _rank_kernel, name="rank", grid=(2, nb),
        in_specs=[pl.BlockSpec((tm, LANES), lambda ph, i: (i, 0))],
        out_specs=[pl.BlockSpec((tm, LANES), lambda ph, i: (i * ph, 0)),
                   pl.BlockSpec((1, nt_pad), lambda ph, i: (0, 0)),
                   pl.BlockSpec((1, LANES), lambda ph, i: (0, 0))],
        out_shape=[jax.ShapeDtypeStruct((rows, LANES), jnp.int32),
                   jax.ShapeDtypeStruct((1, nt_pad), jnp.int32),
                   jax.ShapeDtypeStruct((1, LANES), jnp.int32)],
        scratch_shapes=[pltpu.VMEM((1, LANES), F32), pltpu.VMEM((1, LANES), F32), pltpu.VMEM((1, LANES), F32),
                        pltpu.VMEM((tm, tm), BF16)],
        compiler_params=_params(("arbitrary", "arbitrary")))(route)


def _sc_mesh():
    return plsc.VectorSubcoreMesh(core_axis_name="c", subcore_axis_name="s")


def _sc_scatter_rows(x, idx, n_out):
    rows, d = x.shape
    mesh = _sc_mesh()
    n_workers = mesh.num_cores * mesh.num_subcores
    steps = idx.shape[1] // SC_WINDOW // n_workers
    assert steps * SC_WINDOW * n_workers == idx.shape[1] and rows % SC_WINDOW == 0

    @functools.partial(pl.kernel, out_type=jax.ShapeDtypeStruct((n_out, d), x.dtype), mesh=mesh,
                       scratch_types=[pltpu.VMEM((1, SC_WINDOW), jnp.int32), pltpu.VMEM((SC_WINDOW, d), x.dtype)])
    def scatter(x_hbm, i_hbm, o_hbm, i_vmem, buf):
        first = (lax.axis_index("c") * mesh.num_subcores + lax.axis_index("s")) * steps

        @pl.loop(0, steps)
        def _(t):
            off = (first + t) * SC_WINDOW
            pltpu.sync_copy(i_hbm.at[:, pl.ds(off, SC_WINDOW)], i_vmem)
            pltpu.sync_copy(x_hbm.at[pl.ds(off % rows, SC_WINDOW)], buf)
            pltpu.sync_copy(buf, o_hbm.at[i_vmem.at[0]])

    return scatter(x, idx)


def _sc_gather_rows(x, idx):
    d = x.shape[1]
    n = idx.shape[1]
    mesh = _sc_mesh()
    n_workers = mesh.num_cores * mesh.num_subcores
    steps = n // SC_WINDOW // n_workers
    assert steps * SC_WINDOW * n_workers == n

    @functools.partial(pl.kernel, out_type=jax.ShapeDtypeStruct((n, d), x.dtype), mesh=mesh,
                       scratch_types=[pltpu.VMEM((1, SC_WINDOW), jnp.int32), pltpu.VMEM((SC_WINDOW, d), x.dtype)])
    def gather(x_hbm, i_hbm, o_hbm, i_vmem, buf):
        first = (lax.axis_index("c") * mesh.num_subcores + lax.axis_index("s")) * steps

        @pl.loop(0, steps)
        def _(t):
            off = (first + t) * SC_WINDOW
            pltpu.sync_copy(i_hbm.at[:, pl.ds(off, SC_WINDOW)], i_vmem)
            pltpu.sync_copy(x_hbm.at[i_vmem.at[0]], buf)
            pltpu.sync_copy(buf, o_hbm.at[pl.ds(off, SC_WINDOW)])

    return gather(x, idx)


def _gmm_kernel(texp_ref, nused_ref, x_ref, wg_ref, wu_ref, wd_ref, o_ref):
    @pl.when(pl.program_id(0) < nused_ref[0])
    def _():
        a, b = _unpack_pair(x_ref[...])
        xb = jnp.concatenate([a.astype(BF16), b.astype(BF16)], axis=1)
        gate = _dot(xb, wg_ref[...].astype(BF16))
        h = gate * _sigmoid(gate) * _dot(xb, wu_ref[...].astype(BF16))
        y = _dot(h.astype(BF16), wd_ref[...].astype(BF16))
        o_ref[...] = _pack_pair(y[:, :D_MODEL // 2], y[:, D_MODEL // 2:])


def _gmm(xs, texp, nused, wg, wu, wd, layer):
    rows = xs.shape[0]
    wmap = lambda d, te, nu: (layer, te[d], 0, 0)
    grid_spec = pltpu.PrefetchScalarGridSpec(
        num_scalar_prefetch=2, grid=(rows // TE,),
        in_specs=[pl.BlockSpec((TE, D_MODEL // 2), lambda d, te, nu: (d, 0)),
                  pl.BlockSpec((None, None, D_MODEL, D_EXPERT), wmap),
                  pl.BlockSpec((None, None, D_MODEL, D_EXPERT), wmap),
                  pl.BlockSpec((None, None, D_EXPERT, D_MODEL), wmap)],
        out_specs=pl.BlockSpec((TE, D_MODEL // 2), lambda d, te, nu: (d, 0)))
    return pl.pallas_call(
        _gmm_kernel, name="gmm", grid_spec=grid_spec,
        out_shape=jax.ShapeDtypeStruct((rows, D_MODEL // 2), jnp.uint32),
        compiler_params=_params(("arbitrary",)))(texp, nused, xs, wg, wu, wd)


def _combine_kernel(x_ref, g0_ref, g1_ref, route_ref, g_ref, b_ref, o_ref):
    r = route_ref[...]
    y0 = jnp.concatenate(_unpack_pair(g0_ref[...]), axis=1)
    y1 = jnp.concatenate(_unpack_pair(g1_ref[...]), axis=1)
    f = y0 * r[:, ROUTE_W1:ROUTE_W1 + 1] + y1 * r[:, ROUTE_W2:ROUTE_W2 + 1]
    o_ref[...] = _layer_norm(ALPHA * x_ref[...] + f, g_ref[...], b_ref[...])


def _combine(x, g, route, ln_g, ln_b, tm):
    rows = x.shape[0]
    nb = rows // tm
    return pl.pallas_call(
        _combine_kernel, name="combine", grid=(nb,),
        in_specs=[pl.BlockSpec((tm, D_MODEL), lambda i: (i, 0)),
                  pl.BlockSpec((tm, D_MODEL // 2), lambda i: (i, 0)),
                  pl.BlockSpec((tm, D_MODEL // 2), lambda i: (nb + i, 0)),
                  pl.BlockSpec((tm, LANES), lambda i: (i, 0)), _full_spec(ln_g), _full_spec(ln_b)],
        out_specs=pl.BlockSpec((tm, D_MODEL), lambda i: (i, 0)),
        out_shape=jax.ShapeDtypeStruct((rows, D_MODEL), F32),
        compiler_params=_params(("parallel",)))(x, g, g, route, ln_g, ln_b)


def _moe_routed(x1, x1b, route, wg, wu, wd, layer, ln_g, ln_b, tm):
    rows = x1.shape[0]
    n_rows = 2 * rows + N_EXPERTS * TE
    pos, texp, nused = _rank(route, n_rows // TE, RANK_TILE)
    idx = jnp.concatenate([pos[:, 0], pos[:, 1]])[None, :]
    xs = _sc_scatter_rows(x1b, idx, n_rows)
    ys = _gmm(xs, texp[0, :n_rows // TE], nused[0, :1], wg, wu, wd, layer)
    g = _sc_gather_rows(ys, idx)
    return _combine(x1, g, route, ln_g, ln_b, tm)


def _rope128(x, cos_t, sin_t):
    lane = lax.broadcasted_iota(jnp.int32, (1, LANES), 1)
    half = MLA_ROPE // 2
    swapped = jnp.where(lane < half, pltpu.roll(x, LANES - half, axis=1), pltpu.roll(x, half, axis=1))
    return x * cos_t + swapped * sin_t


def _odd_proj_kernel(x_ref, cos_ref, sin_ref, w_ref, gq_ref, gkv_ref, wuq_ref, wukv_ref,
                     q_ref, ckv_ref, kpe_ref, kpe16_ref, kn_ref, vn_ref):
    tm = x_ref.shape[0]
    group = MIX_GROUP if tm % MIX_GROUP == 0 else tm
    scale = (MLA_NOPE + MLA_ROPE) ** -0.5 * LOG2E
    n_k = MLA_HEADS * MLA_NOPE
    for r0 in range(0, tm, group):
        rs = slice(r0, r0 + group)
        z = _dot(x_ref[rs, :].astype(BF16), w_ref[...])
        cq = _rms_norm(z[:, :MLA_Q_LORA], gq_ref[...])
        ckv = _rms_norm(z[:, MLA_Q_LORA:MLA_Q_LORA + MLA_KV_LORA], gkv_ref[...])
        ckv_ref[rs, :] = ckv
        kv = _dot(ckv.astype(BF16), wukv_ref[...])
        kn_ref[rs, :] = kv[:, :n_k].astype(BF16)
        vn_ref[rs, :] = kv[:, n_k:].astype(BF16)
        cos_t, sin_t = cos_ref[rs, :], sin_ref[rs, :]
        kpe = _rope128(z[:, MLA_Q_LORA + MLA_KV_LORA:], cos_t, sin_t)
        kpe_ref[rs, :] = kpe[:, :MLA_ROPE]
        kpe16_ref[rs, :] = kpe.astype(BF16)
        qf = _dot(cq.astype(BF16), wuq_ref[...])
        for h in range(MLA_HEADS):
            c0 = h * MLA_QPAD
            q_ref[rs, c0:c0 + MLA_NOPE] = (qf[:, c0:c0 + MLA_NOPE] * scale).astype(BF16)
            qr = _rope128(qf[:, c0 + MLA_NOPE:c0 + MLA_QPAD], cos_t, sin_t)
            q_ref[rs, c0 + MLA_NOPE:c0 + MLA_QPAD] = (qr * scale).astype(BF16)


def _odd_proj(x, cos_t, sin_t, w_in, gq, gkv, wuq, w_ukv, tm):
    rows = x.shape[0]
    period = cos_t.shape[0] // tm
    assert rows % tm == 0 and cos_t.shape[0] % tm == 0
    outs = [(MLA_HEADS * MLA_QPAD, BF16), (MLA_KV_LORA, F32), (MLA_ROPE, F32), (LANES, BF16),
            (MLA_HEADS * MLA_NOPE, BF16), (MLA_HEADS * MLA_V, BF16)]
    full = [w_in, gq, gkv, wuq, w_ukv]
    table = pl.BlockSpec((tm, LANES), lambda i: (i % period, 0))
    return pl.pallas_call(
        _odd_proj_kernel, name="odd_proj", grid=(rows // tm,),
        in_specs=[pl.BlockSpec((tm, D_MODEL), lambda i: (i, 0)), table, table] + [_full_spec(a) for a in full],
        out_specs=[pl.BlockSpec((tm, c), lambda i: (i, 0)) for c, _ in outs],
        out_shape=[jax.ShapeDtypeStruct((rows, c), dt) for c, dt in outs],
        compiler_params=_params(("parallel",)))(x, cos_t, sin_t, *full)


def _mla_absorbed_kernel(q_ref, cp_ref, rp_ref, cn_ref, rn_ref, wuk_ref, wuv_ref, o_ref, m_scr, acc_scr, *, tkp):
    tq = q_ref.shape[0]
    rows = MLA_HEADS * tq
    q = q_ref[...]
    qa = []
    for h in range(MLA_HEADS):
        c0 = h * MLA_QPAD
        q_abs = _dot_nt(q[:, c0:c0 + MLA_NOPE], wuk_ref[h])
        qa.append(jnp.concatenate([q_abs.astype(BF16), q[:, c0 + MLA_NOPE:c0 + MLA_NOPE + MLA_ROPE]], axis=1))
    qs = jnp.concatenate(qa, axis=0)
    m_scr[...] = jnp.full(m_scr.shape, NEG, F32)
    acc_scr[...] = jnp.zeros(acc_scr.shape, F32)

    def update(c, r):
        c = c.astype(BF16)
        s = _dot_nt(qs, jnp.concatenate([c, r.astype(BF16)], axis=1))
        m_prev = m_scr[...]
        m_new = jnp.maximum(m_prev, jnp.max(s, axis=1, keepdims=True))
        if s.shape[1] % LANES == 0:
            p = jnp.exp2(s - jnp.tile(m_new, (1, s.shape[1] // LANES)))
        else:
            p = jnp.exp2(s - m_new[:, :1])
        c1 = jnp.concatenate([c, jnp.ones((c.shape[0], LANES), BF16)], axis=1)
        acc_scr[...] = (jnp.tile(jnp.exp2(m_prev - m_new), (1, acc_scr.shape[1] // LANES)) * acc_scr[...]
                        + _dot(p.astype(BF16), c1))
        m_scr[...] = m_new

    def past_body(j, carry):
        rs = pl.ds(pl.multiple_of(j * tkp, tkp), tkp)
        update(cp_ref[rs, :], rp_ref[rs, :])
        return carry
    lax.fori_loop(0, cp_ref.shape[0] // tkp, past_body, 0)
    update(cn_ref[...], rn_ref[:, :MLA_ROPE])

    acc = acc_scr[...]
    lat = (acc[:, :MLA_KV_LORA] / jnp.tile(acc[:, MLA_KV_LORA:], (1, MLA_KV_LORA // LANES))).astype(BF16)
    for h in range(MLA_HEADS):
        o_ref[:, h * MLA_V:(h + 1) * MLA_V] = _dot(lat[h * tq:(h + 1) * tq, :],
                                                   wuv_ref[:, h * MLA_V:(h + 1) * MLA_V]).astype(o_ref.dtype)


def _mla_absorbed(q, ckv_past, kpe_past, ckv_new, kpe_new, wuk_t, wuv, n_seq, tq, tkp):
    p = ckv_past.shape[1]
    assert p % tkp == 0
    rows = MLA_HEADS * tq
    return pl.pallas_call(
        functools.partial(_mla_absorbed_kernel, tkp=tkp), name="mla_absorbed", grid=(n_seq,),
        in_specs=[pl.BlockSpec((tq, MLA_HEADS * MLA_QPAD), lambda b: (b, 0)),
                  pl.BlockSpec((None, p, MLA_KV_LORA), lambda b: (b, 0, 0)),
                  pl.BlockSpec((None, p, MLA_ROPE), lambda b: (b, 0, 0)),
                  pl.BlockSpec((tq, MLA_KV_LORA), lambda b: (b, 0)),
                  pl.BlockSpec((tq, LANES), lambda b: (b, 0)),
                  _full_spec(wuk_t), _full_spec(wuv)],
        out_specs=pl.BlockSpec((tq, MLA_HEADS * MLA_V), lambda b: (b, 0)),
        out_shape=jax.ShapeDtypeStruct((n_seq * tq, MLA_HEADS * MLA_V), BF16),
        scratch_shapes=[pltpu.VMEM((rows, LANES), F32), pltpu.VMEM((rows, MLA_KV_LORA + LANES), F32)],
        compiler_params=_params(("parallel",)))(q, ckv_past, kpe_past, ckv_new, kpe_new, wuk_t, wuv)


def _rope_tables(pos):
    half = MLA_ROPE // 2
    inv = ROPE_BASE ** (-jnp.arange(half, dtype=F32) / half)
    ang = pos.astype(F32)[:, None] * inv[None, :]
    cos, sin = jnp.cos(ang), jnp.sin(ang)
    z = jnp.zeros((pos.shape[0], LANES - MLA_ROPE), F32)
    return jnp.concatenate([cos, cos, z], axis=1), jnp.concatenate([-sin, sin, z], axis=1)


def _pad_rows(a, n):
    return jnp.pad(a, ((0, n - a.shape[0]),) + ((0, 0),) * (a.ndim - 1))


def kernel(x_prompt, x_sample, state_hgrn2, cache_fox_k, cache_fox_v, cache_fox_logf, cache_mla_ckv, cache_mla_kpe, meta_tokens, even_w_in, hg_lb_logits, hg_norm_g, fox_forget_bias, even_w_out, mla_w_in, mla_q_norm_g, mla_kv_norm_g, mla_w_uq, mla_w_uk, mla_w_uv, mla_w_out, ln_mix_g, ln_mix_b, ln_ffn_g, ln_ffn_b, router_w, router_bias, moe_w_gate, moe_w_up, moe_w_down):
    B, T, _ = x_prompt.shape
    Bs, Ts, _ = x_sample.shape
    P = cache_fox_k.shape[2]
    RM = B * T
    RS = Bs * Ts
    RSM = -(-(RS + N_META) // LANES) * LANES
    ME = slice(RS, RS + N_META)
    TM_MAIN, TM_MOE, TQ = TILE_PROJ, TILE_FFN, TILE_ATTN

    xm = x_prompt.reshape(RM, D_MODEL)
    xs = _pad_rows(jnp.concatenate([x_sample.reshape(RS, D_MODEL), meta_tokens.astype(F32)], axis=0), RSM)

    w_in0 = even_w_in[0]
    n_main = 7 * HG_W
    w_even = w_in0[:, :n_main].astype(BF16)
    w_even_f = jnp.pad(w_in0[:, n_main:], ((0, 0), (0, LANES - FOX_HEADS))).astype(BF16)
    fb_pad = jnp.pad(fox_forget_bias[0][None, :], ((0, 0), (0, LANES - FOX_HEADS)))
    g_hg = hg_norm_g[0].reshape(1, HG_W)
    w_out0 = even_w_out[0].astype(BF16)
    e_mat = ((jnp.arange(HG_SUB * HG_DK)[:, None] // HG_DK) == (jnp.arange(CHUNK)[None, :] % HG_SUB)).astype(BF16)

    w_odd = jnp.pad(mla_w_in[0], ((0, 0), (0, LANES - MLA_ROPE))).astype(BF16)
    gq = mla_q_norm_g[0][None, :]
    gkv = mla_kv_norm_g[0][None, :]
    wuq = mla_w_uq[0].reshape(MLA_Q_LORA, MLA_HEADS, MLA_NOPE + MLA_ROPE)
    wuq = jnp.pad(wuq, ((0, 0), (0, 0), (0, MLA_QPAD - MLA_NOPE - MLA_ROPE)))
    wuq = wuq.reshape(MLA_Q_LORA, MLA_HEADS * MLA_QPAD).astype(BF16)
    w_ukv = jnp.concatenate([mla_w_uk[0].reshape(MLA_KV_LORA, -1), mla_w_uv[0].reshape(MLA_KV_LORA, -1)],
                            axis=1).astype(BF16)
    w_out1 = mla_w_out[0].astype(BF16)

    rw = jnp.pad(router_w, ((0, 0), (0, LANES - N_EXPERTS))).astype(BF16)
    rb = jnp.pad(router_bias.astype(F32)[:, None], ((0, LANES - N_EXPERTS), (0, 0)))
    experts = (moe_w_gate, moe_w_up, moe_w_down)
    row2 = lambda a: a[None, :]

    def ffn(x, acts, w_out, l, tm_mix, tm_moe, routed):
        x1, x1b, route = _mix(x, acts, w_out, row2(ln_mix_g[l]), row2(ln_mix_b[l]), rw, rb, tm_mix)
        ln = (row2(ln_ffn_g[l]), row2(ln_ffn_b[l]))
        if routed:
            return _moe_routed(x1, x1b, route, *experts, l, *ln, tm_mix)
        return _moe(x1, route, *experts, l, *ln, tm_moe)

    ps = _even_proj(xs, w_even, w_even_f, hg_lb_logits, fb_pad, RSM, 0)
    ps = dict(zip(("hq", "lf", "hk", "hv", "hgate", "fq", "fk", "fv", "fk16", "fv16", "flf"), ps))
    pm = _even_proj_cache(xm, w_even, w_even_f, hg_lb_logits, fb_pad, ps["fk"][ME], ps["fv"][ME], TM_MAIN, 0, B, T)
    pm = dict(zip(("hq", "lf", "hk", "hv", "hgate", "fq", "fk16", "fv16", "flf", "fk_cache", "fv_cache"), pm))

    hg_keys = ("hq", "lf", "hk", "hv", "hgate", "flf")
    meta_in = [_pad_rows(ps[n][ME], CHUNK) for n in hg_keys]
    zero_s = jnp.zeros((1, HG_HEADS, HG_DK, HG_DV), F32)
    zero_f = jnp.zeros((1, 1, LANES), F32)
    o_hg_meta, fc_meta, s_meta = _hgrn2(*meta_in, g_hg, e_mat, zero_s, zero_f, 1, CHUNK, 0, CHUNK)
    o_hg_meta, fc_meta = o_hg_meta[:N_META], fc_meta[:N_META]
    f_meta_end = fc_meta[N_META - 1:N_META][None]

    o_hg_m, fc_m, s_main = _hgrn2(*[pm[n] for n in hg_keys], g_hg, e_mat, s_meta, f_meta_end, B, T, 0, TILE_HGRN2)

    logf_c = jnp.pad(jnp.transpose(cache_fox_logf[0], (0, 2, 1)), ((0, 0), (0, HG_SUB - FOX_HEADS), (0, 0)))
    fpast = _cumsum_lanes(logf_c.reshape(Bs * HG_SUB, P), TILE_CUMSUM).reshape(Bs, HG_SUB, P)[:, :FOX_HEADS, :]
    f0_s = jnp.pad(fpast[:, :, P - 1][:, None, :], ((0, 0), (0, 0), (0, LANES - FOX_HEADS)))
    o_hg_s, fc_s, s_samp = _hgrn2(*[ps[n] for n in hg_keys], g_hg, e_mat, state_hgrn2[0], f0_s, Bs, Ts, 0, CHUNK)

    def bias_layouts(fc, n_seq, seq_len):
        f4 = fc[:, :FOX_HEADS].T
        return f4[:, :, None], f4.reshape(FOX_HEADS, n_seq, 1, seq_len)

    fq_m, fk_m = bias_layouts(fc_m, B, T)
    fq_s, fk_s = bias_layouts(fc_s, Bs, Ts)
    fq_t, fk_t = bias_layouts(fc_meta, 1, N_META)

    fox_kw = dict(n_heads=FOX_HEADS, dq=FOX_DH, dk=FOX_DH, dv=FOX_DH, mask_mode="causal")
    meta_past = dict(k=ps["fk16"][ME][None], v=ps["fv16"][ME][None],
                     fk=jnp.transpose(fk_t, (1, 0, 2, 3)), tk=N_META)
    o_fox_m = _flash_tri(pm["fq"], pm["fk16"], pm["fv16"], n_seq=B, seq_len=T, tq=TQ,
                         fq=fq_m, fkn=fk_m, past=meta_past, **fox_kw)
    o_fox_s = _fox_cached(ps["fq"], fq_s, cache_fox_k[0], cache_fox_v[0], fpast[:, :, None, :],
                          ps["fk16"], ps["fv16"], fk_s, Bs, Ts, TILE_CACHE)
    o_fox_t = _flash(ps["fq"][ME], ps["fk16"][ME], ps["fv16"][ME], n_seq=1, seq_len=N_META,
                     fq=fq_t, fkn=fk_t, **fox_kw)

    o_hg_small = _pad_rows(jnp.concatenate([o_hg_s, o_hg_meta], axis=0), RSM)
    o_fox_small = _pad_rows(jnp.concatenate([o_fox_s, o_fox_t], axis=0), RSM)
    xm = ffn(xm, [o_hg_m, o_fox_m], w_out0, 0, TM_MOE, TM_MOE, True)
    xs = ffn(xs, [o_hg_small, o_fox_small], w_out0, 0, RSM, RSM, False)

    cos_m, sin_m = _rope_tables(N_META + jnp.arange(T, dtype=jnp.int32))
    pos_small = _pad_rows(jnp.concatenate([jnp.tile(P + jnp.arange(Ts, dtype=jnp.int32), Bs),
                                           jnp.arange(N_META, dtype=jnp.int32)]), RSM)
    cos_s, sin_s = _rope_tables(pos_small)
    qm, ckv_m, kpe_m, kpe16_m, kn_m, vn_m = _odd_proj(xm, cos_m, sin_m, w_odd, gq, gkv, wuq, w_ukv, TM_MOE)
    qs, ckv_s, kpe_s, kpe16_s, kn_s, vn_s = _odd_proj(xs, cos_s, sin_s, w_odd, gq, gkv, wuq, w_ukv, RSM)
    wuk_t = jnp.transpose(mla_w_uk[0], (1, 0, 2)).astype(BF16)
    wuv = mla_w_uv[0].reshape(MLA_KV_LORA, MLA_HEADS * MLA_V).astype(BF16)

    mla_kw = dict(n_heads=MLA_HEADS, dq=MLA_QPAD, dk=MLA_NOPE, dv=MLA_V)
    meta_past = dict(k=kn_s[ME][None], v=vn_s[ME][None], r=kpe16_s[ME][None], tk=N_META)
    o_m = _flash_tri(qm, kn_m, vn_m, n_seq=B, seq_len=T, tq=TQ, rn=kpe16_m,
                     past=meta_past, mask_mode="chunk", **mla_kw)
    assert P % CHUNK == 0 and Ts <= CHUNK
    o_s = _mla_absorbed(qs, cache_mla_ckv[0], cache_mla_kpe[0], ckv_s, kpe16_s, wuk_t, wuv, Bs, Ts, TILE_CACHE)
    o_t = _flash(qs[ME], kn_s[ME], vn_s[ME], n_seq=1, seq_len=N_META, rn=kpe16_s[ME], mask_mode="full", **mla_kw)
    xm = ffn(xm, [o_m], w_out1, 1, TM_MOE, TM_MOE, True)
    xs = ffn(xs, [_pad_rows(jnp.concatenate([o_s, o_t], axis=0), RSM)], w_out1, 1, RSM, RSM, False)

    def with_meta(main, small, *width):
        meta = jnp.broadcast_to(small[ME][None], (B, N_META) + width)
        return jnp.concatenate([meta, main.reshape((B, T) + width)], axis=1)

    y_prompt = xm.reshape(B, T, D_MODEL)
    y_sample = xs[:RS].reshape(Bs, Ts, D_MODEL)
    hg_p = s_main[None]
    fk_p = pm["fk_cache"][None]
    fv_p = pm["fv_cache"][None]
    flf_p = with_meta(pm["flf"][:, :FOX_HEADS], ps["flf"][:, :FOX_HEADS], FOX_HEADS)[None]
    ckv_p = with_meta(ckv_m, ckv_s, MLA_KV_LORA)[None]
    kpe_p = with_meta(kpe_m, kpe_s, MLA_ROPE)[None]
    hg_s = s_samp[None]
    fk_s_out = ps["fk"][:RS].reshape(1, Bs, Ts, FOX_HEADS, FOX_DH)
    fv_s_out = ps["fv"][:RS].reshape(1, Bs, Ts, FOX_HEADS, FOX_DH)
    flf_s = ps["flf"][:RS, :FOX_HEADS].reshape(1, Bs, Ts, FOX_HEADS)
    ckv_so = ckv_s[:RS].reshape(1, Bs, Ts, MLA_KV_LORA)
    kpe_so = kpe_s[:RS].reshape(1, Bs, Ts, MLA_ROPE)
    return (y_prompt, y_sample, hg_p, fk_p, fv_p, flf_p, ckv_p, kpe_p,
            hg_s, fk_s_out, fv_s_out, flf_s, ckv_so, kpe_so)
```

```python
import functools

import jax
import jax.numpy as jnp
from jax import lax
from jax.experimental import pallas as pl
from jax.experimental.pallas import tpu as pltpu
from jax.experimental.pallas import tpu_sc as plsc

D_MODEL = 1024
CHUNK = 64
N_META = 16
HG_HEADS = 4
HG_DK = 128
HG_DV = 128
HG_W = HG_HEADS * HG_DK
FOX_HEADS = 4
FOX_DH = 128
FOX_W = FOX_HEADS * FOX_DH
MLA_HEADS = 8
MLA_Q_LORA = 512
MLA_KV_LORA = 256
MLA_NOPE = 128
MLA_ROPE = 64
MLA_V = 128
MLA_QPAD = 256
ROPE_BASE = 10000.0
N_EXPERTS = 16
N_GROUPS = 4
EXPERTS_PER_GROUP = 4
D_EXPERT = 256
DEPTH = 2
ALPHA = (2 * DEPTH) ** 0.25
LN_EPS = 1e-5
RMS_EPS = 1e-6

LANES = 128
HG_SUB = 8
HG_GROUP = 4
NEG = -1e30
LOG2E = 1.4426950408889634
F32 = jnp.float32
BF16 = jnp.bfloat16
VMEM_LIMIT = 56 * 1024 * 1024
TILE_PROJ = 512
TILE_FFN = 1024
TILE_ATTN = 512
TILE_HGRN2 = 512
TILE_CUMSUM = 512
TILE_CACHE = 1024


def _dot(a, b):
    return jnp.dot(a, b, preferred_element_type=F32)


def _dot_nt(a, b):
    return lax.dot_general(a, b, (((1,), (1,)), ((), ())), preferred_element_type=F32)


def _dot_tn(a, b):
    return lax.dot_general(a, b, (((0,), (0,)), ((), ())), preferred_element_type=F32)


def _split3(x):
    hi = x.astype(BF16)
    r = x - hi.astype(F32)
    mid = r.astype(BF16)
    lo = (r - mid.astype(F32)).astype(BF16)
    return hi, mid, lo


def _cumsum_rows(tri, x):
    hi, mid, lo = _split3(x)
    return _dot(tri, hi) + _dot(tri, mid) + _dot(tri, lo)


def _sigmoid(x):
    return 1.0 / (1.0 + jnp.exp(-x))


def _log_sigmoid(x):
    return jnp.minimum(x, 0.0) - jnp.log(1.0 + jnp.exp(-jnp.abs(x)))


def _layer_norm(x, g, b):
    mu = jnp.mean(x, axis=-1, keepdims=True)
    xc = x - mu
    var = jnp.mean(xc * xc, axis=-1, keepdims=True)
    return xc * lax.rsqrt(var + LN_EPS) * g + b


def _rms_norm(x, g):
    return x * lax.rsqrt(jnp.mean(x * x, axis=-1, keepdims=True) + RMS_EPS) * g


def _params(sem):
    return pltpu.CompilerParams(dimension_semantics=sem, vmem_limit_bytes=VMEM_LIMIT)


def _full_spec(a):
    nd = a.ndim
    return pl.BlockSpec(a.shape, lambda *_: (0,) * nd)


def _row_call(kernel, name, rows, tm, row_ins, full_ins, outs, scratch=()):
    assert rows % tm == 0
    in_specs = [pl.BlockSpec((tm, a.shape[1]), lambda i: (i, 0)) for a in row_ins]
    in_specs += [_full_spec(a) for a in full_ins]
    trail = [c if isinstance(c, tuple) else (c,) for c, _ in outs]
    out_specs = [pl.BlockSpec((tm,) + t, lambda i, n=len(t): (i,) + (0,) * n) for t in trail]
    out_shape = [jax.ShapeDtypeStruct((rows,) + t, dt) for t, (_, dt) in zip(trail, outs)]
    return pl.pallas_call(
        kernel, name=name, grid=(rows // tm,), in_specs=in_specs, out_specs=out_specs,
        out_shape=out_shape, scratch_shapes=list(scratch),
        compiler_params=_params(("parallel",)))(*row_ins, *full_ins)


def _even_proj_body(x_ref, w_ref, wf_ref, lbl_ref, fb_ref, hq_ref, lf_ref, hk_ref, hv_ref, hgate_ref,
                    fq_ref, fk16_ref, fv16_ref, flf_ref, layer):
    xb = x_ref[...].astype(BF16)

    def blk(j):
        return _dot(xb, w_ref[:, j * HG_W:(j + 1) * HG_W])

    logits = lbl_ref[...]
    e = jnp.exp(logits - jnp.max(logits, axis=0, keepdims=True))
    lb = jnp.sum(e[:layer + 1], axis=0, keepdims=True) / jnp.sum(e, axis=0, keepdims=True)

    hq_ref[...] = blk(0).astype(BF16)
    zf = blk(1)
    lf_ref[...] = jnp.log(lb + (1.0 - lb) * _sigmoid(zf))
    hk_ref[...] = ((1.0 - lb) * _sigmoid(-zf)).astype(BF16)
    hv_ref[...] = blk(2).astype(BF16)
    hgate_ref[...] = _sigmoid(blk(3)).astype(BF16)
    fq_ref[...] = (blk(4) * (FOX_DH ** -0.5 * LOG2E)).astype(BF16)
    fk = blk(5)
    fk16_ref[...] = fk.astype(BF16)
    fv = blk(6)
    fv16_ref[...] = fv.astype(BF16)
    flf_ref[...] = _log_sigmoid(_dot(xb, wf_ref[...]) + fb_ref[...])
    return fk, fv


def _even_proj_kernel(x_ref, w_ref, wf_ref, lbl_ref, fb_ref,
                      hq_ref, lf_ref, hk_ref, hv_ref, hgate_ref,
                      fq_ref, fk_ref, fv_ref, fk16_ref, fv16_ref, flf_ref, *, layer):
    fk, fv = _even_proj_body(x_ref, w_ref, wf_ref, lbl_ref, fb_ref, hq_ref, lf_ref, hk_ref, hv_ref, hgate_ref,
                             fq_ref, fk16_ref, fv16_ref, flf_ref, layer)
    for h in range(FOX_HEADS):
        fk_ref[:, h, :] = fk[:, h * FOX_DH:(h + 1) * FOX_DH]
        fv_ref[:, h, :] = fv[:, h * FOX_DH:(h + 1) * FOX_DH]


_EVEN_OUTS = [(HG_W, BF16), (HG_W, F32), (HG_W, BF16), (HG_W, BF16), (HG_W, BF16), (FOX_W, BF16)]


def _even_proj(x, w_main, w_f, lb_logits, fb_pad, tm, layer):
    rows = x.shape[0]
    outs = _EVEN_OUTS + [((FOX_HEADS, FOX_DH), F32), ((FOX_HEADS, FOX_DH), F32), (FOX_W, BF16), (FOX_W, BF16),
                         (LANES, F32)]
    return _row_call(functools.partial(_even_proj_kernel, layer=layer), "even_proj", rows, tm,
                     [x], [w_main, w_f, lb_logits, fb_pad], outs)


def _even_proj_cache_kernel(x_ref, w_ref, wf_ref, lbl_ref, fb_ref, mk_ref, mv_ref,
                            hq_ref, lf_ref, hk_ref, hv_ref, hgate_ref, fq_ref, fk16_ref, fv16_ref, flf_ref,
                            fk_hbm, fv_hbm, kbuf, vbuf, sem, msem, *, layer, tiles_per_seq):
    i, n = pl.program_id(0), pl.num_programs(0)
    tm = x_ref.shape[0]

    def row_copies(step):
        b = step // tiles_per_seq
        t0 = N_META + (step % tiles_per_seq) * tm
        return [pltpu.make_async_copy(buf.at[:, pl.ds(h * FOX_DH, FOX_DH)], hbm.at[b, pl.ds(t0, tm), h, :], sem.at[a, h])
                for a, (buf, hbm) in enumerate(((kbuf, fk_hbm), (vbuf, fv_hbm))) for h in range(FOX_HEADS)]

    def meta_copies(step):
        b = step // tiles_per_seq
        return [pltpu.make_async_copy(mk_ref, fk_hbm.at[b, pl.ds(0, N_META)], msem.at[0]),
                pltpu.make_async_copy(mv_ref, fv_hbm.at[b, pl.ds(0, N_META)], msem.at[1])]

    fk, fv = _even_proj_body(x_ref, w_ref, wf_ref, lbl_ref, fb_ref, hq_ref, lf_ref, hk_ref, hv_ref, hgate_ref,
                             fq_ref, fk16_ref, fv16_ref, flf_ref, layer)

    @pl.when(i > 0)
    def _():
        for c in row_copies(i - 1):
            c.wait()

    @pl.when((i > 0) & ((i - 1) % tiles_per_seq == 0))
    def _():
        for c in meta_copies(i - 1):
            c.wait()

    kbuf[...] = fk
    vbuf[...] = fv
    for c in row_copies(i):
        c.start()

    @pl.when(i % tiles_per_seq == 0)
    def _():
        for c in meta_copies(i):
            c.start()

    @pl.when(i == n - 1)
    def _():
        for c in row_copies(i):
            c.wait()

    @pl.when((i == n - 1) & (i % tiles_per_seq == 0))
    def _():
        for c in meta_copies(i):
            c.wait()


def _even_proj_cache(x, w_main, w_f, lb_logits, fb_pad, meta_k, meta_v, tm, layer, n_seq, seq_len):
    rows = x.shape[0]
    assert rows == n_seq * seq_len and seq_len % tm == 0
    outs = _EVEN_OUTS + [(FOX_W, BF16), (FOX_W, BF16), (LANES, F32)]
    full = [w_main, w_f, lb_logits, fb_pad, meta_k, meta_v]
    cache = jax.ShapeDtypeStruct((n_seq, N_META + seq_len, FOX_HEADS, FOX_DH), F32)
    return pl.pallas_call(
        functools.partial(_even_proj_cache_kernel, layer=layer, tiles_per_seq=seq_len // tm),
        name="even_proj_cache", grid=(rows // tm,),
        in_specs=[pl.BlockSpec((tm, D_MODEL), lambda i: (i, 0))] + [_full_spec(a) for a in full],
        out_specs=[pl.BlockSpec((tm, c), lambda i: (i, 0)) for c, _ in outs] + [pl.BlockSpec(memory_space=pl.ANY)] * 2,
        out_shape=[jax.ShapeDtypeStruct((rows, c), dt) for c, dt in outs] + [cache, cache],
        scratch_shapes=[pltpu.VMEM((tm, FOX_W), F32), pltpu.VMEM((tm, FOX_W), F32),
                        pltpu.SemaphoreType.DMA((2, FOX_HEADS)), pltpu.SemaphoreType.DMA((2,))],
        compiler_params=_params(("arbitrary",)))(x, *full)


def _bcast_sub(x, j):
    n, c = x.shape
    x3 = x.reshape(n // HG_SUB, HG_SUB, c)
    return jnp.broadcast_to(x3[:, j:j + 1, :], x3.shape).reshape(n, c)


def _level_ref(b, w):
    n, c = b.shape
    parts = [jnp.broadcast_to(b[m * 2 * w + w - 1:m * 2 * w + w, :], (2 * w, c)) for m in range(n // (2 * w))]
    return parts[0] if len(parts) == 1 else jnp.concatenate(parts, axis=0)


def _hgrn2_kernel(q_ref, lf_ref, k_ref, v_ref, gate_ref, flf_ref, g_ref, e_ref, s0_ref, f0_ref,
                  o_ref, fcum_ref, sout_ref, st_scr, fc_scr, *, n_chunks):
    i = pl.program_id(1)
    C = CHUNK

    @pl.when(i == 0)
    def _():
        for h in range(HG_HEADS):
            st_scr[h] = s0_ref[h].T
        fc_scr[...] = f0_ref[...]

    row = lax.broadcasted_iota(jnp.int32, (C, 1), 0)
    col = lax.broadcasted_iota(jnp.int32, (1, C), 1)
    tri = (col <= row).astype(BF16)
    same = lambda w: (row // w) == (col // w)
    levels = (32, 16, 8)

    fc = fc_scr[...]
    for c in range(n_chunks):
        sl = slice(c * C, (c + 1) * C)
        fcum = _cumsum_rows(tri, flf_ref[sl, :]) + fc
        fcum_ref[sl, :] = fcum
        fc = fcum[C - 1:C, :]
    fc_scr[...] = fc

    staged = []
    for c in range(n_chunks):
        sl = slice(c * C, (c + 1) * C)
        per_head = []
        for h0 in range(0, HG_HEADS, HG_GROUP):
            gs = slice(h0 * HG_DK, (h0 + HG_GROUP) * HG_DK)
            b = _cumsum_rows(tri, lf_ref[sl, gs]) * LOG2E
            q = q_ref[sl, gs].astype(F32)
            k = k_ref[sl, gs].astype(F32)
            v = v_ref[sl, gs]
            qb = (q * jnp.exp2(b)).astype(BF16)
            b_last = b[C - 1:C, :]
            kd = (k * jnp.exp2(b_last - b)).astype(BF16)
            e_last = jnp.exp2(b_last)

            pjs = [(jnp.exp2(jnp.where((row % HG_SUB) >= j, b - _bcast_sub(b, j), NEG)) * q
                    * _bcast_sub(k, j)).astype(BF16) for j in range(HG_SUB)]
            lv = []
            for w in levels:
                upper = (row % (2 * w)) >= w
                ew = jnp.exp2(-jnp.abs(b - _level_ref(b, w)))
                lv.append((jnp.where(upper, q * ew, 0.0).astype(BF16), jnp.where(upper, 0.0, k * ew).astype(BF16)))

            for hh in range(HG_GROUP):
                hs = slice(hh * HG_DK, (hh + 1) * HG_DK)
                a = jnp.where(same(HG_SUB), _dot(jnp.concatenate([p[:, hs] for p in pjs], axis=1), e_ref[...]), 0.0)
                for w, (qw, kw) in zip(levels, lv):
                    aw = _dot_nt(qw[:, hs], kw[:, hs])
                    a = a + (aw if 2 * w == C else jnp.where(same(2 * w), aw, 0.0))
                vh = v[:, hs]
                per_head.append((_dot(a.astype(BF16), vh), qb[:, hs], kd[:, hs], e_last[:, hs], vh))
        staged.append(per_head)

    st = [st_scr[h] for h in range(HG_HEADS)]
    for c in range(n_chunks):
        sl = slice(c * C, (c + 1) * C)
        for h, (o_intra, qb_h, kd_h, e_h, vh) in enumerate(staged[c]):
            ho = slice(h * HG_DK, (h + 1) * HG_DK)
            o = o_intra + _dot_nt(qb_h, st[h].astype(BF16))
            st[h] = st[h] * e_h + _dot_tn(vh, kd_h)
            o = _rms_norm(o, g_ref[:, ho])
            o_ref[sl, ho] = (o * gate_ref[sl, ho].astype(F32)).astype(BF16)
    for h in range(HG_HEADS):
        st_scr[h] = st[h]

    @pl.when(i == pl.num_programs(1) - 1)
    def _():
        for h in range(HG_HEADS):
            sout_ref[h] = st_scr[h].T


def _hgrn2(q, lf, k, v, gate, flf, g, e_mat, s0, f0, n_seq, seq_len, row_off, tb):
    assert seq_len % tb == 0 and tb % CHUNK == 0 and row_off % tb == 0
    nb = seq_len // tb
    off = row_off // tb
    per_seq = s0.shape[0] > 1
    rmap = lambda s, i: (off + s * nb + i, 0)
    omap = lambda s, i: (s * nb + i, 0)
    smap = (lambda s, i: (s, 0, 0, 0)) if per_seq else (lambda s, i: (0, 0, 0, 0))
    fmap = (lambda s, i: (s, 0, 0)) if per_seq else (lambda s, i: (0, 0, 0))
    in_specs = [pl.BlockSpec((tb, HG_W), rmap) for _ in range(5)]
    in_specs += [pl.BlockSpec((tb, LANES), rmap), _full_spec(g), _full_spec(e_mat),
                 pl.BlockSpec((None, HG_HEADS, HG_DK, HG_DV), smap), pl.BlockSpec((None, 1, LANES), fmap)]
    out_specs = [pl.BlockSpec((tb, HG_W), omap), pl.BlockSpec((tb, LANES), omap),
                 pl.BlockSpec((None, HG_HEADS, HG_DK, HG_DV), lambda s, i: (s, 0, 0, 0))]
    out_shape = [jax.ShapeDtypeStruct((n_seq * seq_len, HG_W), BF16),
                 jax.ShapeDtypeStruct((n_seq * seq_len, LANES), F32),
                 jax.ShapeDtypeStruct((n_seq, HG_HEADS, HG_DK, HG_DV), F32)]
    scratch = [pltpu.VMEM((HG_HEADS, HG_DV, HG_DK), F32), pltpu.VMEM((1, LANES), F32)]
    return pl.pallas_call(
        functools.partial(_hgrn2_kernel, n_chunks=tb // CHUNK), name="hgrn2",
        grid=(n_seq, nb), in_specs=in_specs, out_specs=out_specs, out_shape=out_shape,
        scratch_shapes=scratch, compiler_params=_params(("parallel", "arbitrary")))(
            q, lf, k, v, gate, flf, g, e_mat, s0, f0)


def _cumsum_kernel(x_ref, tri_ref, o_ref, carry):
    @pl.when(pl.program_id(0) == 0)
    def _():
        carry[...] = jnp.zeros_like(carry)

    hi, mid, lo = _split3(x_ref[...])
    tri = tri_ref[...]
    out = _dot(hi, tri) + _dot(mid, tri) + _dot(lo, tri) + carry[...]
    o_ref[...] = out
    carry[...] = out[:, out.shape[1] - 1:]


def _cumsum_lanes(x, tb):
    r, seq_len = x.shape
    tri = (jnp.arange(tb)[:, None] <= jnp.arange(tb)[None, :]).astype(BF16)
    return pl.pallas_call(
        _cumsum_kernel, name="cumsum", grid=(seq_len // tb,),
        in_specs=[pl.BlockSpec((r, tb), lambda i: (0, i)), _full_spec(tri)],
        out_specs=pl.BlockSpec((r, tb), lambda i: (0, i)),
        out_shape=jax.ShapeDtypeStruct(x.shape, F32),
        scratch_shapes=[pltpu.VMEM((r, 1), F32)],
        compiler_params=_params(("arbitrary",)))(x, tri)


def _flash_kernel(*refs, tq, has_bias, has_rope, mask_mode):
    it = iter(refs)
    q_ref = next(it)
    fq_ref = next(it) if has_bias else None
    kn_ref, vn_ref = next(it), next(it)
    rn_ref = next(it) if has_rope else None
    fkn_ref = next(it) if has_bias else None
    o_ref = next(it)
    m_scr, acc_scr = next(it), next(it)
    dv = o_ref.shape[1]

    q = q_ref[...]
    m_scr[...] = jnp.full(m_scr.shape, NEG, F32)
    acc_scr[...] = jnp.zeros(acc_scr.shape, F32)
    fq_b = jnp.broadcast_to(fq_ref[...] * LOG2E, (tq, LANES)) if has_bias else None

    def scores(k, r, fk):
        if has_rope:
            k = jnp.concatenate([k, r], axis=1)
        s = _dot_nt(q, k.astype(BF16))
        if has_bias:
            s = s + jnp.tile(fq_b, (1, s.shape[1] // LANES)) if s.shape[1] % LANES == 0 else s + fq_b[:, :1]
            s = s - fk * LOG2E
        return s

    def update(s, v, mask):
        if mask is not None:
            s = jnp.where(mask, s, NEG)
        m_prev = m_scr[...]
        m_new = jnp.maximum(m_prev, jnp.max(s, axis=1, keepdims=True))
        alpha = jnp.exp2(m_prev - m_new)
        if s.shape[1] % LANES == 0:
            p = jnp.exp2(s - jnp.tile(m_new, (1, s.shape[1] // LANES)))
        else:
            p = jnp.exp2(s - m_new[:, :1])
        v1 = jnp.concatenate([v.astype(BF16), jnp.ones((v.shape[0], LANES), BF16)], axis=1)
        acc_scr[...] = jnp.tile(alpha, (1, acc_scr.shape[1] // LANES)) * acc_scr[...] + _dot(p.astype(BF16), v1)
        m_scr[...] = m_new

    row = lax.broadcasted_iota(jnp.int32, (tq, 1), 0)
    col = lax.broadcasted_iota(jnp.int32, (1, tq), 1)
    if mask_mode == "causal":
        mask = col <= row
    elif mask_mode == "chunk":
        mask = (col // CHUNK) <= (row // CHUNK)
    else:
        mask = None

    update(scores(kn_ref[...], rn_ref[...] if has_rope else None, fkn_ref[...] if has_bias else None),
           vn_ref[...], mask)
    acc = acc_scr[...]
    o_ref[...] = (acc[:, :dv] / acc[:, dv:]).astype(o_ref.dtype)


def _flash(q, kn, vn, *, n_seq, n_heads, seq_len, dq, dk, dv, mask_mode, fq=None, fkn=None, rn=None):
    tq = seq_len
    has_bias = fq is not None
    has_rope = rn is not None
    ins, specs = [q], [pl.BlockSpec((tq, dq), lambda b, h: (b, h))]
    if has_bias:
        ins.append(fq)
        specs.append(pl.BlockSpec((None, tq, 1), lambda b, h: (h, b, 0)))
    ins += [kn, vn]
    specs += [pl.BlockSpec((seq_len, dk), lambda b, h: (b, h)), pl.BlockSpec((seq_len, dv), lambda b, h: (b, h))]
    if has_rope:
        ins.append(rn)
        specs.append(pl.BlockSpec((seq_len, LANES), lambda b, h: (b, 0)))
    if has_bias:
        ins.append(fkn)
        specs.append(pl.BlockSpec((None, None, 1, seq_len), lambda b, h: (h, b, 0, 0)))
    kern = functools.partial(_flash_kernel, tq=tq, has_bias=has_bias, has_rope=has_rope, mask_mode=mask_mode)
    return pl.pallas_call(
        kern, name="flash", grid=(n_seq, n_heads), in_specs=specs,
        out_specs=pl.BlockSpec((tq, dv), lambda b, h: (b, h)),
        out_shape=jax.ShapeDtypeStruct((n_seq * seq_len, n_heads * dv), BF16),
        scratch_shapes=[pltpu.VMEM((tq, LANES), F32), pltpu.VMEM((tq, dv + LANES), F32)],
        compiler_params=_params(("parallel", "parallel")))(*ins)


def _fox_cached_kernel(q_ref, fq_ref, kp_hbm, vp_hbm, fkp_ref, kn_ref, vn_ref, fkn_ref, o_ref,
                       kbuf, vbuf, ksem, vsem, *, tkp, n_blk):
    b = pl.program_id(0)
    tq = q_ref.shape[0]
    row = lax.broadcasted_iota(jnp.int32, (tq, 1), 0)
    col = lax.broadcasted_iota(jnp.int32, (1, tq), 1)
    ones_p = jnp.ones((tkp, LANES), BF16)
    steps = [(h, j) for h in range(FOX_HEADS) for j in range(n_blk)]

    def copies(i):
        h, j = steps[i]
        slot = i % 2
        src = lambda ref: ref.at[b, pl.ds(j * tkp, tkp), h, :]
        return (pltpu.make_async_copy(src(kp_hbm), kbuf.at[slot], ksem.at[slot]),
                pltpu.make_async_copy(src(vp_hbm), vbuf.at[slot], vsem.at[slot]))

    def update(state, s, v1):
        m_prev, acc = state
        m_new = jnp.maximum(m_prev, jnp.max(s, axis=1, keepdims=True))
        p = jnp.exp2(s - m_new)
        return m_new, jnp.exp2(m_prev - m_new) * acc + _dot(p.astype(BF16), v1)

    for c in copies(0):
        c.start()
    state = None
    for i, (h, j) in enumerate(steps):
        hs = slice(h * FOX_DH, (h + 1) * FOX_DH)
        q = q_ref[:, hs]
        fq = fq_ref[h] * LOG2E
        if j == 0:
            state = (jnp.full((tq, 1), NEG, F32), jnp.zeros((tq, FOX_DH + LANES), F32))
        if i + 1 < len(steps):
            for c in copies(i + 1):
                c.start()
        for c in copies(i):
            c.wait()
        slot = i % 2
        s = _dot_nt(q, kbuf[slot].astype(BF16)) + fq - fkp_ref[h, :, j * tkp:(j + 1) * tkp] * LOG2E
        state = update(state, s, jnp.concatenate([vbuf[slot].astype(BF16), ones_p], axis=1))
        if j == n_blk - 1:
            s = _dot_nt(q, kn_ref[:, hs]) + fq - fkn_ref[h] * LOG2E
            s = jnp.where(col <= row, s, NEG)
            _, acc = update(state, s, jnp.concatenate([vn_ref[:, hs], ones_p[:tq]], axis=1))
            o_ref[:, hs] = (acc[:, :FOX_DH] / acc[:, FOX_DH:FOX_DH + 1]).astype(o_ref.dtype)


def _fox_cached(q, fq, kp, vp, fkp, kn, vn, fkn, n_seq, tq, tkp):
    p = kp.shape[1]
    assert p % tkp == 0
    return pl.pallas_call(
        functools.partial(_fox_cached_kernel, tkp=tkp, n_blk=p // tkp), name="fox_cached", grid=(n_seq,),
        in_specs=[pl.BlockSpec((tq, FOX_W), lambda b: (b, 0)),
                  pl.BlockSpec((FOX_HEADS, tq, 1), lambda b: (0, b, 0)),
                  pl.BlockSpec(memory_space=pl.ANY),
                  pl.BlockSpec(memory_space=pl.ANY),
                  pl.BlockSpec((None, FOX_HEADS, 1, p), lambda b: (b, 0, 0, 0)),
                  pl.BlockSpec((tq, FOX_W), lambda b: (b, 0)),
                  pl.BlockSpec((tq, FOX_W), lambda b: (b, 0)),
                  pl.BlockSpec((FOX_HEADS, None, 1, tq), lambda b: (0, b, 0, 0))],
        out_specs=pl.BlockSpec((tq, FOX_W), lambda b: (b, 0)),
        out_shape=jax.ShapeDtypeStruct((n_seq * tq, FOX_W), BF16),
        scratch_shapes=[pltpu.VMEM((2, tkp, FOX_DH), F32), pltpu.VMEM((2, tkp, FOX_DH), F32),
                        pltpu.SemaphoreType.DMA((2,)), pltpu.SemaphoreType.DMA((2,))],
        compiler_params=_params(("arbitrary",)))(q, fq, kp, vp, fkp, kn, vn, fkn)


FLASH_UNROLL_OFF = 14
FLASH_UNROLL_DIAG = 8


def _tri_tables(nq):
    pairs = [(qi, kj) for qi in range(nq) for kj in range(qi)] + [(qi, qi) for qi in range(nq)] + [(0, 0)]
    return (jnp.array([p[0] for p in pairs], jnp.int32), jnp.array([p[1] for p in pairs], jnp.int32))


def _flash_tri_kernel(qt_ref, kt_ref, *refs, tq, nq, has_bias, has_rope, mask_mode):
    it = iter(refs)
    q_ref = next(it)
    fq_ref = next(it) if has_bias else None
    kp_ref, vp_ref = next(it), next(it)
    rp_ref = next(it) if has_rope else None
    fkp_ref = next(it) if has_bias else None
    kn_ref, vn_ref = next(it), next(it)
    rn_ref = next(it) if has_rope else None
    fkn_ref = next(it) if has_bias else None
    o_ref = next(it)
    m_scr, acc_scr, sa_scr, sb_scr = next(it), next(it), next(it), next(it)
    fqb_scr = next(it) if has_bias else None
    dv = o_ref.shape[1]
    n_off = nq * (nq - 1) // 2
    tile = lambda j: pl.ds(pl.multiple_of(j * tq, tq), tq)
    ones = jnp.ones((tq, LANES), BF16)

    kp = kp_ref[...]
    if has_rope:
        kp = jnp.concatenate([kp, rp_ref[...]], axis=1)
    vp1 = jnp.concatenate([vp_ref[...], ones[:vp_ref.shape[0]]], axis=1)
    for i in range(nq):
        rs = slice(i * tq, (i + 1) * tq)
        s = _dot_nt(q_ref[rs, :], kp)
        if has_bias:
            fb = fq_ref[rs, :] * LOG2E
            fqb_scr[rs, :] = fb
            s = s + fb[:, :s.shape[1]] - fkp_ref[...] * LOG2E
        m0 = jnp.max(s, axis=1, keepdims=True)
        m_scr[i] = jnp.broadcast_to(m0, (tq, LANES))
        acc_scr[i] = _dot(jnp.exp2(s - m0).astype(BF16), vp1)

    def fill(s_ref, t):
        qs, ks = tile(qt_ref[t]), tile(kt_ref[t])
        k = kn_ref[ks, :]
        if has_rope:
            k = jnp.concatenate([k, rn_ref[ks, :]], axis=1)
        s = _dot_nt(q_ref[qs, :], k)
        if has_bias:
            s = s + jnp.tile(fqb_scr[qs, :], (1, tq // LANES)) - fkn_ref[:, ks] * LOG2E
        s_ref[...] = s

    def drain(s_ref, t, mask):
        qi = qt_ref[t]
        s = s_ref[...]
        if mask is not None:
            s = jnp.where(mask, s, NEG)
        m_prev = m_scr[qi]
        m_new = jnp.maximum(m_prev, jnp.max(s, axis=1, keepdims=True))
        p = jnp.exp2(s - jnp.tile(m_new, (1, tq // LANES)))
        v1 = jnp.concatenate([vn_ref[tile(kt_ref[t]), :], ones], axis=1)
        acc = jnp.tile(jnp.exp2(m_prev - m_new), (1, (dv + LANES) // LANES)) * acc_scr[qi] + _dot(p.astype(BF16), v1)
        return qi, m_new, acc

    def keep(s_ref, t):
        qi, m_new, acc = drain(s_ref, t, None)
        m_scr[qi] = m_new
        acc_scr[qi] = acc

    row = lax.broadcasted_iota(jnp.int32, (tq, 1), 0)
    col = lax.broadcasted_iota(jnp.int32, (1, tq), 1)
    mask = {"causal": col <= row, "chunk": (col // CHUNK) <= (row // CHUNK)}[mask_mode]

    def finish(s_ref, t):
        qi, _, acc = drain(s_ref, t, mask)
        o_ref[tile(qi), :] = (acc[:, :dv] / acc[:, dv:]).astype(o_ref.dtype)

    def pipeline(t0, n, unroll, consume):
        assert n % unroll == 0 and unroll % 2 == 0

        def body(i, carry):
            t = t0 + unroll * i
            for u in range(0, unroll, 2):
                fill(sb_scr, t + u + 1)
                consume(sa_scr, t + u)
                fill(sa_scr, t + u + 2)
                consume(sb_scr, t + u + 1)
            return carry
        lax.fori_loop(0, n // unroll, body, 0)

    fill(sa_scr, 0)
    pipeline(0, n_off, FLASH_UNROLL_OFF, keep)
    pipeline(n_off, nq, FLASH_UNROLL_DIAG, finish)


def _flash_tri(q, kn, vn, *, n_seq, n_heads, seq_len, tq, dq, dk, dv, mask_mode, past, fq=None, fkn=None, rn=None):
    nq = seq_len // tq
    has_bias = fq is not None
    has_rope = rn is not None
    tp = past["k"].shape[1]
    m3 = lambda f: (lambda b, h, qt, kt: f(b, h))
    ins, specs = [q], [pl.BlockSpec((seq_len, dq), m3(lambda b, h: (b, h)))]
    if has_bias:
        ins.append(fq)
        specs.append(pl.BlockSpec((None, seq_len, LANES), m3(lambda b, h: (h, b, 0))))
    ins += [past["k"], past["v"]]
    specs += [pl.BlockSpec((None, tp, dk), m3(lambda b, h: (0, 0, h))),
              pl.BlockSpec((None, tp, dv), m3(lambda b, h: (0, 0, h)))]
    if has_rope:
        ins.append(past["r"])
        specs.append(pl.BlockSpec((None, tp, LANES), m3(lambda b, h: (0, 0, 0))))
    if has_bias:
        ins.append(past["fk"])
        specs.append(pl.BlockSpec((None, None, 1, tp), m3(lambda b, h: (0, h, 0, 0))))
    ins += [kn, vn]
    specs += [pl.BlockSpec((seq_len, dk), m3(lambda b, h: (b, h))),
              pl.BlockSpec((seq_len, dv), m3(lambda b, h: (b, h)))]
    if has_rope:
        ins.append(rn)
        specs.append(pl.BlockSpec((seq_len, LANES), m3(lambda b, h: (b, 0))))
    if has_bias:
        ins.append(fkn)
        specs.append(pl.BlockSpec((None, None, 1, seq_len), m3(lambda b, h: (h, b, 0, 0))))
    scratch = [pltpu.VMEM((nq, tq, LANES), F32), pltpu.VMEM((nq, tq, dv + LANES), F32),
               pltpu.VMEM((tq, tq), F32), pltpu.VMEM((tq, tq), F32)]
    if has_bias:
        scratch.append(pltpu.VMEM((seq_len, LANES), F32))
    grid_spec = pltpu.PrefetchScalarGridSpec(
        num_scalar_prefetch=2, grid=(n_seq, n_heads), in_specs=specs,
        out_specs=pl.BlockSpec((seq_len, dv), m3(lambda b, h: (b, h))), scratch_shapes=scratch)
    kern = functools.partial(_flash_tri_kernel, tq=tq, nq=nq, has_bias=has_bias, has_rope=has_rope,
                             mask_mode=mask_mode)
    return pl.pallas_call(
        kern, name="flash_tri", grid_spec=grid_spec,
        out_shape=jax.ShapeDtypeStruct((n_seq * seq_len, n_heads * dv), BF16),
        compiler_params=_params(("parallel", "arbitrary")))(*_tri_tables(nq), *ins)


def _route(sc, sb):
    def top2_sum(v):
        a, b, c, d = v
        a, b = jnp.maximum(a, b), jnp.minimum(a, b)
        c, d = jnp.maximum(c, d), jnp.minimum(c, d)
        hi, lo2 = jnp.maximum(a, c), jnp.minimum(a, c)
        return hi + jnp.maximum(lo2, jnp.maximum(b, d))

    gs = [top2_sum(sb[g * EXPERTS_PER_GROUP:(g + 1) * EXPERTS_PER_GROUP]) for g in range(N_GROUPS)]
    best_v, best_g = gs[0], jnp.zeros(gs[0].shape, jnp.int32)
    for g in range(1, N_GROUPS):
        upd = gs[g] > best_v
        best_v = jnp.where(upd, gs[g], best_v)
        best_g = jnp.where(upd, g, best_g)
    masked = [jnp.where(best_g == (e // EXPERTS_PER_GROUP), sb[e], -jnp.inf) for e in range(N_EXPERTS)]

    def argmax_first(vals, exclude=None):
        bv = jnp.full(vals[0].shape, -jnp.inf, F32)
        bi = jnp.full(vals[0].shape, -1, jnp.int32)
        for e, v in enumerate(vals):
            upd = v > bv
            if exclude is not None:
                upd = upd & (exclude != e)
            bv = jnp.where(upd, v, bv)
            bi = jnp.where(upd, e, bi)
        return bi

    i1 = argmax_first(masked)
    i2 = argmax_first(masked, exclude=i1)
    w1 = sum(jnp.where(i1 == e, sc[e], 0.0) for e in range(N_EXPERTS))
    w2 = sum(jnp.where(i2 == e, sc[e], 0.0) for e in range(N_EXPERTS))
    tot = w1 + w2
    w1, w2 = w1 / tot, w2 / tot
    comb = [jnp.where(i1 == e, w1, 0.0) + jnp.where(i2 == e, w2, 0.0) for e in range(N_EXPERTS)]
    return comb + [i1.astype(F32), i2.astype(F32), w1, w2]


def _mix_kernel(*refs, n_act):
    x_ref = refs[0]
    a_refs = refs[1:1 + n_act]
    w_ref, g_ref, b_ref, rw_ref, rb_ref, x1_ref, x1p_ref, comb_ref, ct_scr = refs[1 + n_act:]
    half = D_MODEL // 2
    tm = x_ref.shape[0]
    group = MIX_GROUP if tm % MIX_GROUP == 0 else tm
    ct_scr[...] = jnp.zeros(ct_scr.shape, F32)
    for r0 in range(0, tm, group):
        rs = slice(r0, r0 + group)
        ys = []
        for n0 in (0, half):
            y = None
            k0 = 0
            for a_ref in a_refs:
                kw = a_ref.shape[1]
                part = _dot(a_ref[rs, :], w_ref[k0:k0 + kw, n0:n0 + half])
                y = part if y is None else y + part
                k0 += kw
            ys.append(y)
        x1 = _layer_norm(ALPHA * x_ref[rs, :] + jnp.concatenate(ys, axis=1), g_ref[...], b_ref[...])
        x1_ref[rs, :] = x1

        x1p_ref[rs, :] = _pack_pair(x1[:, :half], x1[:, half:])
        logits = _dot(x1.astype(BF16), rw_ref[...])
        scores_t = _sigmoid(logits).T
        sc = [scores_t[e:e + 1, :] for e in range(N_EXPERTS)]
        sb = [sc[e] + rb_ref[e:e + 1, :] for e in range(N_EXPERTS)]
        for r, val in enumerate(_route(sc, sb)):
            ct_scr[r:r + 1, rs] = val
        comb_ref[rs, :] = ct_scr[:, rs].T


def _mix(x, acts, w_out, ln_g, ln_b, rw, rb, tm):
    rows = x.shape[0]
    return _row_call(functools.partial(_mix_kernel, n_act=len(acts)), "mix", rows, tm,
                     [x] + list(acts), [w_out, ln_g, ln_b, rw, rb],
                     [(D_MODEL, F32), (D_MODEL // 2, jnp.uint32), (LANES, F32)],
                     scratch=[pltpu.VMEM((LANES, tm), F32)])


def _moe_kernel(x_ref, comb_ref, wg_ref, wu_ref, wd_ref, g_ref, b_ref, o_ref, xb_scr, acc_scr):
    e = pl.program_id(1)

    @pl.when(e == 0)
    def _():
        xb_scr[...] = x_ref[...].astype(BF16)
        acc_scr[...] = jnp.zeros(acc_scr.shape, F32)

    xb = xb_scr[...]
    lane = lax.broadcasted_iota(jnp.int32, (1, LANES), 1)
    c_e = jnp.sum(jnp.where(lane == e, comb_ref[...], 0.0), axis=1, keepdims=True)
    gate = _dot(xb, wg_ref[...].astype(BF16))
    h = gate * _sigmoid(gate) * _dot(xb, wu_ref[...].astype(BF16))
    acc_scr[...] += _dot((h * c_e).astype(BF16), wd_ref[...].astype(BF16))

    @pl.when(e == N_EXPERTS - 1)
    def _():
        o_ref[...] = _layer_norm(ALPHA * x_ref[...] + acc_scr[...], g_ref[...], b_ref[...])


def _moe(x, comb, wg, wu, wd, layer, ln_g, ln_b, tm):
    rows = x.shape[0]
    assert rows % tm == 0
    return pl.pallas_call(
        _moe_kernel, name="moe", grid=(rows // tm, N_EXPERTS),
        in_specs=[pl.BlockSpec((tm, D_MODEL), lambda i, e: (i, 0)),
                  pl.BlockSpec((tm, LANES), lambda i, e: (i, 0)),
                  pl.BlockSpec((None, None, D_MODEL, D_EXPERT), lambda i, e: (layer, e, 0, 0)),
                  pl.BlockSpec((None, None, D_MODEL, D_EXPERT), lambda i, e: (layer, e, 0, 0)),
                  pl.BlockSpec((None, None, D_EXPERT, D_MODEL), lambda i, e: (layer, e, 0, 0)),
                  _full_spec(ln_g), _full_spec(ln_b)],
        out_specs=pl.BlockSpec((tm, D_MODEL), lambda i, e: (i, 0)),
        out_shape=jax.ShapeDtypeStruct((rows, D_MODEL), F32),
        scratch_shapes=[pltpu.VMEM((tm, D_MODEL), BF16), pltpu.VMEM((tm, D_MODEL), F32)],
        compiler_params=_params(("parallel", "arbitrary")))(x, comb, wg, wu, wd, ln_g, ln_b)


ROUTE_E1, ROUTE_E2, ROUTE_W1, ROUTE_W2 = N_EXPERTS, N_EXPERTS + 1, N_EXPERTS + 2, N_EXPERTS + 3
TE = 1024
SC_WINDOW = 128
RANK_TILE = 1024
MIX_GROUP = 256


def _pack_pair(a, b):
    au = lax.bitcast_convert_type(a.astype(BF16).astype(F32), jnp.uint32)
    bu = lax.bitcast_convert_type(b.astype(BF16).astype(F32), jnp.uint32)
    return (au >> 16) | (bu & jnp.uint32(0xFFFF0000))


def _unpack_pair(w):
    a = lax.bitcast_convert_type(w << 16, F32)
    b = lax.bitcast_convert_type(w & jnp.uint32(0xFFFF0000), F32)
    return a, b


def _rank_kernel(route_ref, pos_ref, texp_ref, nused_ref, cnt_scr, carry_scr, seg_scr, before_scr):
    ph, i = pl.program_id(0), pl.program_id(1)
    T = route_ref.shape[0]
    lane = lax.broadcasted_iota(jnp.int32, (1, LANES), 1)
    lane_f = lane.astype(F32)
    r = route_ref[...]
    e1, e2 = r[:, ROUTE_E1:ROUTE_E1 + 1], r[:, ROUTE_E2:ROUTE_E2 + 1]
    m1, m2 = lane_f == e1, lane_f == e2
    m = jnp.where(m1 | m2, 1.0, 0.0)
    colsum = jnp.sum(m, axis=0, keepdims=True)

    @pl.when((ph == 0) & (i == 0))
    def _():
        cnt_scr[...] = jnp.zeros(cnt_scr.shape, F32)

    @pl.when(ph == 0)
    def _():
        cnt_scr[...] += colsum

    @pl.when((ph == 1) & (i == 0))
    def _():
        cnt = cnt_scr[...].astype(jnp.int32)
        padded = (((cnt + (TE - 1)) // TE) * TE).astype(F32)
        rr = lax.broadcasted_iota(jnp.int32, (LANES, 1), 0)
        upper = (rr < lane).astype(BF16)
        hi, mid, lo = _split3(jnp.broadcast_to(padded, (HG_SUB, LANES)))
        seg = (_dot(hi, upper) + _dot(mid, upper) + _dot(lo, upper))[:1, :]
        seg_scr[...] = seg
        carry_scr[...] = jnp.zeros(carry_scr.shape, F32)
        seg_end = seg + padded
        tile_row = lax.broadcasted_iota(jnp.int32, texp_ref.shape, 1).astype(F32) * float(TE)
        te_acc = jnp.zeros(texp_ref.shape, jnp.int32)
        for e in range(N_EXPERTS):
            te_acc = te_acc + jnp.where(seg_end[:, e:e + 1] <= tile_row, 1, 0)
        texp_ref[...] = jnp.minimum(te_acc, N_EXPERTS - 1)
        nused_ref[...] = jnp.broadcast_to(seg_end[:, N_EXPERTS - 1:N_EXPERTS] / float(TE), nused_ref.shape).astype(jnp.int32)

    @pl.when((ph == 1) & (i == 0))
    def _():
        row = lax.broadcasted_iota(jnp.int32, (T, 1), 0)
        col = lax.broadcasted_iota(jnp.int32, (1, T), 1)
        before_scr[...] = (col < row).astype(BF16)

    @pl.when(ph == 1)
    def _():
        cum = _dot(before_scr[...], m.astype(BF16)) + carry_scr[...] + seg_scr[...]
        p1 = jnp.sum(jnp.where(m1, cum, 0.0), axis=1, keepdims=True)
        p2 = jnp.sum(jnp.where(m2, cum, 0.0), axis=1, keepdims=True)
        pos_ref[...] = jnp.where(lane == 0, p1, jnp.where(lane == 1, p2, 0.0)).astype(jnp.int32)
        carry_scr[...] += colsum


def _rank(route, n_tiles, tm):
    rows = route.shape[0]
    nb = rows // tm
    nt_pad = -(-n_tiles // LANES) * LANES
    return pl.pallas_call(
        _rank_kernel, name="rank", grid=(2, nb),
        in_specs=[pl.BlockSpec((tm, LANES), lambda ph, i: (i, 0))],
        out_specs=[pl.BlockSpec((tm, LANES), lambda ph, i: (i * ph, 0)),
                   pl.BlockSpec((1, nt_pad), lambda ph, i: (0, 0)),
                   pl.BlockSpec((1, LANES), lambda ph, i: (0, 0))],
        out_shape=[jax.ShapeDtypeStruct((rows, LANES), jnp.int32),
                   jax.ShapeDtypeStruct((1, nt_pad), jnp.int32),
                   jax.ShapeDtypeStruct((1, LANES), jnp.int32)],
        scratch_shapes=[pltpu.VMEM((1, LANES), F32), pltpu.VMEM((1, LANES), F32), pltpu.VMEM((1, LANES), F32),
                        pltpu.VMEM((tm, tm), BF16)],
        compiler_params=_params(("arbitrary", "arbitrary")))(route)


def _sc_mesh():
    return plsc.VectorSubcoreMesh(core_axis_name="c", subcore_axis_name="s")


def _sc_scatter_rows(x, idx, n_out):
    rows, d = x.shape
    mesh = _sc_mesh()
    n_workers = mesh.num_cores * mesh.num_subcores
    steps = idx.shape[1] // SC_WINDOW // n_workers
    assert steps * SC_WINDOW * n_workers == idx.shape[1] and rows % SC_WINDOW == 0

    @functools.partial(pl.kernel, out_type=jax.ShapeDtypeStruct((n_out, d), x.dtype), mesh=mesh,
                       scratch_types=[pltpu.VMEM((1, SC_WINDOW), jnp.int32), pltpu.VMEM((SC_WINDOW, d), x.dtype)])
    def scatter(x_hbm, i_hbm, o_hbm, i_vmem, buf):
        first = (lax.axis_index("c") * mesh.num_subcores + lax.axis_index("s")) * steps

        @pl.loop(0, steps)
        def _(t):
            off = (first + t) * SC_WINDOW
            pltpu.sync_copy(i_hbm.at[:, pl.ds(off, SC_WINDOW)], i_vmem)
            pltpu.sync_copy(x_hbm.at[pl.ds(off % rows, SC_WINDOW)], buf)
            pltpu.sync_copy(buf, o_hbm.at[i_vmem.at[0]])

    return scatter(x, idx)


def _sc_gather_rows(x, idx):
    d = x.shape[1]
    n = idx.shape[1]
    mesh = _sc_mesh()
    n_workers = mesh.num_cores * mesh.num_subcores
    steps = n // SC_WINDOW // n_workers
    assert steps * SC_WINDOW * n_workers == n

    @functools.partial(pl.kernel, out_type=jax.ShapeDtypeStruct((n, d), x.dtype), mesh=mesh,
                       scratch_types=[pltpu.VMEM((1, SC_WINDOW), jnp.int32), pltpu.VMEM((SC_WINDOW, d), x.dtype)])
    def gather(x_hbm, i_hbm, o_hbm, i_vmem, buf):
        first = (lax.axis_index("c") * mesh.num_subcores + lax.axis_index("s")) * steps

        @pl.loop(0, steps)
        def _(t):
            off = (first + t) * SC_WINDOW
            pltpu.sync_copy(i_hbm.at[:, pl.ds(off, SC_WINDOW)], i_vmem)
            pltpu.sync_copy(x_hbm.at[i_vmem.at[0]], buf)
            pltpu.sync_copy(buf, o_hbm.at[pl.ds(off, SC_WINDOW)])

    return gather(x, idx)


def _gmm_kernel(texp_ref, nused_ref, x_ref, wg_ref, wu_ref, wd_ref, o_ref):
    @pl.when(pl.program_id(0) < nused_ref[0])
    def _():
        a, b = _unpack_pair(x_ref[...])
        xb = jnp.concatenate([a.astype(BF16), b.astype(BF16)], axis=1)
        gate = _dot(xb, wg_ref[...].astype(BF16))
        h = gate * _sigmoid(gate) * _dot(xb, wu_ref[...].astype(BF16))
        y = _dot(h.astype(BF16), wd_ref[...].astype(BF16))
        o_ref[...] = _pack_pair(y[:, :D_MODEL // 2], y[:, D_MODEL // 2:])


def _gmm(xs, texp, nused, wg, wu, wd, layer):
    rows = xs.shape[0]
    wmap = lambda d, te, nu: (layer, te[d], 0, 0)
    grid_spec = pltpu.PrefetchScalarGridSpec(
        num_scalar_prefetch=2, grid=(rows // TE,),
        in_specs=[pl.BlockSpec((TE, D_MODEL // 2), lambda d, te, nu: (d, 0)),
                  pl.BlockSpec((None, None, D_MODEL, D_EXPERT), wmap),
                  pl.BlockSpec((None, None, D_MODEL, D_EXPERT), wmap),
                  pl.BlockSpec((None, None, D_EXPERT, D_MODEL), wmap)],
        out_specs=pl.BlockSpec((TE, D_MODEL // 2), lambda d, te, nu: (d, 0)))
    return pl.pallas_call(
        _gmm_kernel, name="gmm", grid_spec=grid_spec,
        out_shape=jax.ShapeDtypeStruct((rows, D_MODEL // 2), jnp.uint32),
        compiler_params=_params(("arbitrary",)))(texp, nused, xs, wg, wu, wd)


def _combine_kernel(x_ref, g0_ref, g1_ref, route_ref, g_ref, b_ref, o_ref):
    r = route_ref[...]
    y0 = jnp.concatenate(_unpack_pair(g0_ref[...]), axis=1)
    y1 = jnp.concatenate(_unpack_pair(g1_ref[...]), axis=1)
    f = y0 * r[:, ROUTE_W1:ROUTE_W1 + 1] + y1 * r[:, ROUTE_W2:ROUTE_W2 + 1]
    o_ref[...] = _layer_norm(ALPHA * x_ref[...] + f, g_ref[...], b_ref[...])


def _combine(x, g, route, ln_g, ln_b, tm):
    rows = x.shape[0]
    nb = rows // tm
    return pl.pallas_call(
        _combine_kernel, name="combine", grid=(nb,),
        in_specs=[pl.BlockSpec((tm, D_MODEL), lambda i: (i, 0)),
                  pl.BlockSpec((tm, D_MODEL // 2), lambda i: (i, 0)),
                  pl.BlockSpec((tm, D_MODEL // 2), lambda i: (nb + i, 0)),
                  pl.BlockSpec((tm, LANES), lambda i: (i, 0)), _full_spec(ln_g), _full_spec(ln_b)],
        out_specs=pl.BlockSpec((tm, D_MODEL), lambda i: (i, 0)),
        out_shape=jax.ShapeDtypeStruct((rows, D_MODEL), F32),
        compiler_params=_params(("parallel",)))(x, g, g, route, ln_g, ln_b)


def _moe_routed(x1, x1b, route, wg, wu, wd, layer, ln_g, ln_b, tm):
    rows = x1.shape[0]
    n_rows = 2 * rows + N_EXPERTS * TE
    pos, texp, nused = _rank(route, n_rows // TE, RANK_TILE)
    idx = jnp.concatenate([pos[:, 0], pos[:, 1]])[None, :]
    xs = _sc_scatter_rows(x1b, idx, n_rows)
    ys = _gmm(xs, texp[0, :n_rows // TE], nused[0, :1], wg, wu, wd, layer)
    g = _sc_gather_rows(ys, idx)
    return _combine(x1, g, route, ln_g, ln_b, tm)


def _rope128(x, cos_t, sin_t):
    lane = lax.broadcasted_iota(jnp.int32, (1, LANES), 1)
    half = MLA_ROPE // 2
    swapped = jnp.where(lane < half, pltpu.roll(x, LANES - half, axis=1), pltpu.roll(x, half, axis=1))
    return x * cos_t + swapped * sin_t


def _odd_proj_kernel(x_ref, cos_ref, sin_ref, w_ref, gq_ref, gkv_ref, wuq_ref, wukv_ref,
                     q_ref, ckv_ref, kpe_ref, kpe16_ref, kn_ref, vn_ref):
    tm = x_ref.shape[0]
    group = MIX_GROUP if tm % MIX_GROUP == 0 else tm
    scale = (MLA_NOPE + MLA_ROPE) ** -0.5 * LOG2E
    n_k = MLA_HEADS * MLA_NOPE
    for r0 in range(0, tm, group):
        rs = slice(r0, r0 + group)
        z = _dot(x_ref[rs, :].astype(BF16), w_ref[...])
        cq = _rms_norm(z[:, :MLA_Q_LORA], gq_ref[...])
        ckv = _rms_norm(z[:, MLA_Q_LORA:MLA_Q_LORA + MLA_KV_LORA], gkv_ref[...])
        ckv_ref[rs, :] = ckv
        kv = _dot(ckv.astype(BF16), wukv_ref[...])
        kn_ref[rs, :] = kv[:, :n_k].astype(BF16)
        vn_ref[rs, :] = kv[:, n_k:].astype(BF16)
        cos_t, sin_t = cos_ref[rs, :], sin_ref[rs, :]
        kpe = _rope128(z[:, MLA_Q_LORA + MLA_KV_LORA:], cos_t, sin_t)
        kpe_ref[rs, :] = kpe[:, :MLA_ROPE]
        kpe16_ref[rs, :] = kpe.astype(BF16)
        qf = _dot(cq.astype(BF16), wuq_ref[...])
        for h in range(MLA_HEADS):
            c0 = h * MLA_QPAD
            q_ref[rs, c0:c0 + MLA_NOPE] = (qf[:, c0:c0 + MLA_NOPE] * scale).astype(BF16)
            qr = _rope128(qf[:, c0 + MLA_NOPE:c0 + MLA_QPAD], cos_t, sin_t)
            q_ref[rs, c0 + MLA_NOPE:c0 + MLA_QPAD] = (qr * scale).astype(BF16)


def _odd_proj(x, cos_t, sin_t, w_in, gq, gkv, wuq, w_ukv, tm):
    rows = x.shape[0]
    period = cos_t.shape[0] // tm
    assert rows % tm == 0 and cos_t.shape[0] % tm == 0
    outs = [(MLA_HEADS * MLA_QPAD, BF16), (MLA_KV_LORA, F32), (MLA_ROPE, F32), (LANES, BF16),
            (MLA_HEADS * MLA_NOPE, BF16), (MLA_HEADS * MLA_V, BF16)]
    full = [w_in, gq, gkv, wuq, w_ukv]
    table = pl.BlockSpec((tm, LANES), lambda i: (i % period, 0))
    return pl.pallas_call(
        _odd_proj_kernel, name="odd_proj", grid=(rows // tm,),
        in_specs=[pl.BlockSpec((tm, D_MODEL), lambda i: (i, 0)), table, table] + [_full_spec(a) for a in full],
        out_specs=[pl.BlockSpec((tm, c), lambda i: (i, 0)) for c, _ in outs],
        out_shape=[jax.ShapeDtypeStruct((rows, c), dt) for c, dt in outs],
        compiler_params=_params(("parallel",)))(x, cos_t, sin_t, *full)


def _mla_absorbed_kernel(q_ref, cp_ref, rp_ref, cn_ref, rn_ref, wuk_ref, wuv_ref, o_ref, m_scr, acc_scr, *, tkp):
    tq = q_ref.shape[0]
    rows = MLA_HEADS * tq
    q = q_ref[...]
    qa = []
    for h in range(MLA_HEADS):
        c0 = h * MLA_QPAD
        q_abs = _dot_nt(q[:, c0:c0 + MLA_NOPE], wuk_ref[h])
        qa.append(jnp.concatenate([q_abs.astype(BF16), q[:, c0 + MLA_NOPE:c0 + MLA_NOPE + MLA_ROPE]], axis=1))
    qs = jnp.concatenate(qa, axis=0)
    m_scr[...] = jnp.full(m_scr.shape, NEG, F32)
    acc_scr[...] = jnp.zeros(acc_scr.shape, F32)

    def update(c, r):
        c = c.astype(BF16)
        s = _dot_nt(qs, jnp.concatenate([c, r.astype(BF16)], axis=1))
        m_prev = m_scr[...]
        m_new = jnp.maximum(m_prev, jnp.max(s, axis=1, keepdims=True))
        if s.shape[1] % LANES == 0:
            p = jnp.exp2(s - jnp.tile(m_new, (1, s.shape[1] // LANES)))
        else:
            p = jnp.exp2(s - m_new[:, :1])
        c1 = jnp.concatenate([c, jnp.ones((c.shape[0], LANES), BF16)], axis=1)
        acc_scr[...] = (jnp.tile(jnp.exp2(m_prev - m_new), (1, acc_scr.shape[1] // LANES)) * acc_scr[...]
                        + _dot(p.astype(BF16), c1))
        m_scr[...] = m_new

    def past_body(j, carry):
        rs = pl.ds(pl.multiple_of(j * tkp, tkp), tkp)
        update(cp_ref[rs, :], rp_ref[rs, :])
        return carry
    lax.fori_loop(0, cp_ref.shape[0] // tkp, past_body, 0)
    update(cn_ref[...], rn_ref[:, :MLA_ROPE])

    acc = acc_scr[...]
    lat = (acc[:, :MLA_KV_LORA] / jnp.tile(acc[:, MLA_KV_LORA:], (1, MLA_KV_LORA // LANES))).astype(BF16)
    for h in range(MLA_HEADS):
        o_ref[:, h * MLA_V:(h + 1) * MLA_V] = _dot(lat[h * tq:(h + 1) * tq, :],
                                                   wuv_ref[:, h * MLA_V:(h + 1) * MLA_V]).astype(o_ref.dtype)


def _mla_absorbed(q, ckv_past, kpe_past, ckv_new, kpe_new, wuk_t, wuv, n_seq, tq, tkp):
    p = ckv_past.shape[1]
    assert p % tkp == 0
    rows = MLA_HEADS * tq
    return pl.pallas_call(
        functools.partial(_mla_absorbed_kernel, tkp=tkp), name="mla_absorbed", grid=(n_seq,),
        in_specs=[pl.BlockSpec((tq, MLA_HEADS * MLA_QPAD), lambda b: (b, 0)),
                  pl.BlockSpec((None, p, MLA_KV_LORA), lambda b: (b, 0, 0)),
                  pl.BlockSpec((None, p, MLA_ROPE), lambda b: (b, 0, 0)),
                  pl.BlockSpec((tq, MLA_KV_LORA), lambda b: (b, 0)),
                  pl.BlockSpec((tq, LANES), lambda b: (b, 0)),
                  _full_spec(wuk_t), _full_spec(wuv)],
        out_specs=pl.BlockSpec((tq, MLA_HEADS * MLA_V), lambda b: (b, 0)),
        out_shape=jax.ShapeDtypeStruct((n_seq * tq, MLA_HEADS * MLA_V), BF16),
        scratch_shapes=[pltpu.VMEM((rows, LANES), F32), pltpu.VMEM((rows, MLA_KV_LORA + LANES), F32)],
        compiler_params=_params(("parallel",)))(q, ckv_past, kpe_past, ckv_new, kpe_new, wuk_t, wuv)


def _rope_tables(pos):
    half = MLA_ROPE // 2
    inv = ROPE_BASE ** (-jnp.arange(half, dtype=F32) / half)
    ang = pos.astype(F32)[:, None] * inv[None, :]
    cos, sin = jnp.cos(ang), jnp.sin(ang)
    z = jnp.zeros((pos.shape[0], LANES - MLA_ROPE), F32)
    return jnp.concatenate([cos, cos, z], axis=1), jnp.concatenate([-sin, sin, z], axis=1)


def _pad_rows(a, n):
    return jnp.pad(a, ((0, n - a.shape[0]),) + ((0, 0),) * (a.ndim - 1))


def kernel(x_prompt, x_sample, state_hgrn2, cache_fox_k, cache_fox_v, cache_fox_logf, cache_mla_ckv, cache_mla_kpe, meta_tokens, even_w_in, hg_lb_logits, hg_norm_g, fox_forget_bias, even_w_out, mla_w_in, mla_q_norm_g, mla_kv_norm_g, mla_w_uq, mla_w_uk, mla_w_uv, mla_w_out, ln_mix_g, ln_mix_b, ln_ffn_g, ln_ffn_b, router_w, router_bias, moe_w_gate, moe_w_up, moe_w_down):
    B, T, _ = x_prompt.shape
    Bs, Ts, _ = x_sample.shape
    P = cache_fox_k.shape[2]
    RM = B * T
    RS = Bs * Ts
    RSM = -(-(RS + N_META) // LANES) * LANES
    ME = slice(RS, RS + N_META)
    TM_MAIN, TM_MOE, TQ = TILE_PROJ, TILE_FFN, TILE_ATTN

    xm = x_prompt.reshape(RM, D_MODEL)
    xs = _pad_rows(jnp.concatenate([x_sample.reshape(RS, D_MODEL), meta_tokens.astype(F32)], axis=0), RSM)

    w_in0 = even_w_in[0]
    n_main = 7 * HG_W
    w_even = w_in0[:, :n_main].astype(BF16)
    w_even_f = jnp.pad(w_in0[:, n_main:], ((0, 0), (0, LANES - FOX_HEADS))).astype(BF16)
    fb_pad = jnp.pad(fox_forget_bias[0][None, :], ((0, 0), (0, LANES - FOX_HEADS)))
    g_hg = hg_norm_g[0].reshape(1, HG_W)
    w_out0 = even_w_out[0].astype(BF16)
    e_mat = ((jnp.arange(HG_SUB * HG_DK)[:, None] // HG_DK) == (jnp.arange(CHUNK)[None, :] % HG_SUB)).astype(BF16)

    w_odd = jnp.pad(mla_w_in[0], ((0, 0), (0, LANES - MLA_ROPE))).astype(BF16)
    gq = mla_q_norm_g[0][None, :]
    gkv = mla_kv_norm_g[0][None, :]
    wuq = mla_w_uq[0].reshape(MLA_Q_LORA, MLA_HEADS, MLA_NOPE + MLA_ROPE)
    wuq = jnp.pad(wuq, ((0, 0), (0, 0), (0, MLA_QPAD - MLA_NOPE - MLA_ROPE)))
    wuq = wuq.reshape(MLA_Q_LORA, MLA_HEADS * MLA_QPAD).astype(BF16)
    w_ukv = jnp.concatenate([mla_w_uk[0].reshape(MLA_KV_LORA, -1), mla_w_uv[0].reshape(MLA_KV_LORA, -1)],
                            axis=1).astype(BF16)
    w_out1 = mla_w_out[0].astype(BF16)

    rw = jnp.pad(router_w, ((0, 0), (0, LANES - N_EXPERTS))).astype(BF16)
    rb = jnp.pad(router_bias.astype(F32)[:, None], ((0, LANES - N_EXPERTS), (0, 0)))
    experts = (moe_w_gate, moe_w_up, moe_w_down)
    row2 = lambda a: a[None, :]

    def ffn(x, acts, w_out, l, tm_mix, tm_moe, routed):
        x1, x1b, route = _mix(x, acts, w_out, row2(ln_mix_g[l]), row2(ln_mix_b[l]), rw, rb, tm_mix)
        ln = (row2(ln_ffn_g[l]), row2(ln_ffn_b[l]))
        if routed:
            return _moe_routed(x1, x1b, route, *experts, l, *ln, tm_mix)
        return _moe(x1, route, *experts, l, *ln, tm_moe)

    ps = _even_proj(xs, w_even, w_even_f, hg_lb_logits, fb_pad, RSM, 0)
    ps = dict(zip(("hq", "lf", "hk", "hv", "hgate", "fq", "fk", "fv", "fk16", "fv16", "flf"), ps))
    pm = _even_proj_cache(xm, w_even, w_even_f, hg_lb_logits, fb_pad, ps["fk"][ME], ps["fv"][ME], TM_MAIN, 0, B, T)
    pm = dict(zip(("hq", "lf", "hk", "hv", "hgate", "fq", "fk16", "fv16", "flf", "fk_cache", "fv_cache"), pm))

    hg_keys = ("hq", "lf", "hk", "hv", "hgate", "flf")
    meta_in = [_pad_rows(ps[n][ME], CHUNK) for n in hg_keys]
    zero_s = jnp.zeros((1, HG_HEADS, HG_DK, HG_DV), F32)
    zero_f = jnp.zeros((1, 1, LANES), F32)
    o_hg_meta, fc_meta, s_meta = _hgrn2(*meta_in, g_hg, e_mat, zero_s, zero_f, 1, CHUNK, 0, CHUNK)
    o_hg_meta, fc_meta = o_hg_meta[:N_META], fc_meta[:N_META]
    f_meta_end = fc_meta[N_META - 1:N_META][None]

    o_hg_m, fc_m, s_main = _hgrn2(*[pm[n] for n in hg_keys], g_hg, e_mat, s_meta, f_meta_end, B, T, 0, TILE_HGRN2)

    logf_c = jnp.pad(jnp.transpose(cache_fox_logf[0], (0, 2, 1)), ((0, 0), (0, HG_SUB - FOX_HEADS), (0, 0)))
    fpast = _cumsum_lanes(logf_c.reshape(Bs * HG_SUB, P), TILE_CUMSUM).reshape(Bs, HG_SUB, P)[:, :FOX_HEADS, :]
    f0_s = jnp.pad(fpast[:, :, P - 1][:, None, :], ((0, 0), (0, 0), (0, LANES - FOX_HEADS)))
    o_hg_s, fc_s, s_samp = _hgrn2(*[ps[n] for n in hg_keys], g_hg, e_mat, state_hgrn2[0], f0_s, Bs, Ts, 0, CHUNK)

    def bias_layouts(fc, n_seq, seq_len):
        f4 = fc[:, :FOX_HEADS].T
        return f4[:, :, None], f4.reshape(FOX_HEADS, n_seq, 1, seq_len)

    fq_m, fk_m = bias_layouts(fc_m, B, T)
    fq_m = jnp.broadcast_to(fq_m, fq_m.shape[:2] + (LANES,))
    fq_s, fk_s = bias_layouts(fc_s, Bs, Ts)
    fq_t, fk_t = bias_layouts(fc_meta, 1, N_META)

    fox_kw = dict(n_heads=FOX_HEADS, dq=FOX_DH, dk=FOX_DH, dv=FOX_DH, mask_mode="causal")
    meta_past = dict(k=ps["fk16"][ME][None], v=ps["fv16"][ME][None],
                     fk=jnp.transpose(fk_t, (1, 0, 2, 3)), tk=N_META)
    o_fox_m = _flash_tri(pm["fq"], pm["fk16"], pm["fv16"], n_seq=B, seq_len=T, tq=TQ,
                         fq=fq_m, fkn=fk_m, past=meta_past, **fox_kw)
    o_fox_s = _fox_cached(ps["fq"], fq_s, cache_fox_k[0], cache_fox_v[0], fpast[:, :, None, :],
                          ps["fk16"], ps["fv16"], fk_s, Bs, Ts, TILE_CACHE)
    o_fox_t = _flash(ps["fq"][ME], ps["fk16"][ME], ps["fv16"][ME], n_seq=1, seq_len=N_META,
                     fq=fq_t, fkn=fk_t, **fox_kw)

    o_hg_small = _pad_rows(jnp.concatenate([o_hg_s, o_hg_meta], axis=0), RSM)
    o_fox_small = _pad_rows(jnp.concatenate([o_fox_s, o_fox_t], axis=0), RSM)
    xm = ffn(xm, [o_hg_m, o_fox_m], w_out0, 0, TM_MOE, TM_MOE, True)
    xs = ffn(xs, [o_hg_small, o_fox_small], w_out0, 0, RSM, RSM, False)

    cos_m, sin_m = _rope_tables(N_META + jnp.arange(T, dtype=jnp.int32))
    pos_small = _pad_rows(jnp.concatenate([jnp.tile(P + jnp.arange(Ts, dtype=jnp.int32), Bs),
                                           jnp.arange(N_META, dtype=jnp.int32)]), RSM)
    cos_s, sin_s = _rope_tables(pos_small)
    qm, ckv_m, kpe_m, kpe16_m, kn_m, vn_m = _odd_proj(xm, cos_m, sin_m, w_odd, gq, gkv, wuq, w_ukv, TM_MOE)
    qs, ckv_s, kpe_s, kpe16_s, kn_s, vn_s = _odd_proj(xs, cos_s, sin_s, w_odd, gq, gkv, wuq, w_ukv, RSM)
    wuk_t = jnp.transpose(mla_w_uk[0], (1, 0, 2)).astype(BF16)
    wuv = mla_w_uv[0].reshape(MLA_KV_LORA, MLA_HEADS * MLA_V).astype(BF16)

    mla_kw = dict(n_heads=MLA_HEADS, dq=MLA_QPAD, dk=MLA_NOPE, dv=MLA_V)
    meta_past = dict(k=kn_s[ME][None], v=vn_s[ME][None], r=kpe16_s[ME][None], tk=N_META)
    o_m = _flash_tri(qm, kn_m, vn_m, n_seq=B, seq_len=T, tq=TQ, rn=kpe16_m,
                     past=meta_past, mask_mode="chunk", **mla_kw)
    assert P % CHUNK == 0 and Ts <= CHUNK
    o_s = _mla_absorbed(qs, cache_mla_ckv[0], cache_mla_kpe[0], ckv_s, kpe16_s, wuk_t, wuv, Bs, Ts, TILE_CACHE)
    o_t = _flash(qs[ME], kn_s[ME], vn_s[ME], n_seq=1, seq_len=N_META, rn=kpe16_s[ME], mask_mode="full", **mla_kw)
    xm = ffn(xm, [o_m], w_out1, 1, TM_MOE, TM_MOE, True)
    xs = ffn(xs, [_pad_rows(jnp.concatenate([o_s, o_t], axis=0), RSM)], w_out1, 1, RSM, RSM, False)

    def with_meta(main, small, *width):
        meta = jnp.broadcast_to(small[ME][None], (B, N_META) + width)
        return jnp.concatenate([meta, main.reshape((B, T) + width)], axis=1)

    y_prompt = xm.reshape(B, T, D_MODEL)
    y_sample = xs[:RS].reshape(Bs, Ts, D_MODEL)
    hg_p = s_main[None]
    fk_p = pm["fk_cache"][None]
    fv_p = pm["fv_cache"][None]
    flf_p = with_meta(pm["flf"][:, :FOX_HEADS], ps["flf"][:, :FOX_HEADS], FOX_HEADS)[None]
    ckv_p = with_meta(ckv_m, ckv_s, MLA_KV_LORA)[None]
    kpe_p = with_meta(kpe_m, kpe_s, MLA_ROPE)[None]
    hg_s = s_samp[None]
    fk_s_out = ps["fk"][:RS].reshape(1, Bs, Ts, FOX_HEADS, FOX_DH)
    fv_s_out = ps["fv"][:RS].reshape(1, Bs, Ts, FOX_HEADS, FOX_DH)
    flf_s = ps["flf"][:RS, :FOX_HEADS].reshape(1, Bs, Ts, FOX_HEADS)
    ckv_so = ckv_s[:RS].reshape(1, Bs, Ts, MLA_KV_LORA)
    kpe_so = kpe_s[:RS].reshape(1, Bs, Ts, MLA_ROPE)
    return (y_prompt, y_sample, hg_p, fk_p, fv_p, flf_p, ckv_p, kpe_p,
            hg_s, fk_s_out, fv_s_out, flf_s, ckv_so, kpe_so)
```

```python
import functools

import jax
import jax.numpy as jnp
from jax import lax
from jax.experimental import pallas as pl
from jax.experimental.pallas import tpu as pltpu
from jax.experimental.pallas import tpu_sc as plsc

D_MODEL = 1024
CHUNK = 64
N_META = 16
HG_HEADS = 4
HG_DK = 128
HG_DV = 128
HG_W = HG_HEADS * HG_DK
FOX_HEADS = 4
FOX_DH = 128
FOX_W = FOX_HEADS * FOX_DH
MLA_HEADS = 8
MLA_Q_LORA = 512
MLA_KV_LORA = 256
MLA_NOPE = 128
MLA_ROPE = 64
MLA_V = 128
MLA_QPAD = 256
ROPE_BASE = 10000.0
N_EXPERTS = 16
N_GROUPS = 4
EXPERTS_PER_GROUP = 4
D_EXPERT = 256
DEPTH = 2
ALPHA = (2 * DEPTH) ** 0.25
LN_EPS = 1e-5
RMS_EPS = 1e-6

LANES = 128
HG_SUB = 8
HG_GROUP = 4
NEG = -1e30
LOG2E = 1.4426950408889634
F32 = jnp.float32
BF16 = jnp.bfloat16
VMEM_LIMIT = 56 * 1024 * 1024
TILE_PROJ = 512
TILE_FFN = 1024
TILE_ATTN = 512
TILE_HGRN2 = 512
TILE_CUMSUM = 512
TILE_CACHE = 1024


def _dot(a, b):
    return jnp.dot(a, b, preferred_element_type=F32)


def _dot_nt(a, b):
    return lax.dot_general(a, b, (((1,), (1,)), ((), ())), preferred_element_type=F32)


def _dot_tn(a, b):
    return lax.dot_general(a, b, (((0,), (0,)), ((), ())), preferred_element_type=F32)


def _split3(x):
    hi = x.astype(BF16)
    r = x - hi.astype(F32)
    mid = r.astype(BF16)
    lo = (r - mid.astype(F32)).astype(BF16)
    return hi, mid, lo


def _cumsum_rows(tri, x):
    hi, mid, lo = _split3(x)
    return _dot(tri, hi) + _dot(tri, mid) + _dot(tri, lo)


def _sigmoid(x):
    return 1.0 / (1.0 + jnp.exp(-x))


def _log_sigmoid(x):
    return jnp.minimum(x, 0.0) - jnp.log(1.0 + jnp.exp(-jnp.abs(x)))


def _layer_norm(x, g, b):
    mu = jnp.mean(x, axis=-1, keepdims=True)
    xc = x - mu
    var = jnp.mean(xc * xc, axis=-1, keepdims=True)
    return xc * lax.rsqrt(var + LN_EPS) * g + b


def _rms_norm(x, g):
    return x * lax.rsqrt(jnp.mean(x * x, axis=-1, keepdims=True) + RMS_EPS) * g


def _params(sem):
    return pltpu.CompilerParams(dimension_semantics=sem, vmem_limit_bytes=VMEM_LIMIT)


def _full_spec(a):
    nd = a.ndim
    return pl.BlockSpec(a.shape, lambda *_: (0,) * nd)


def _row_call(kernel, name, rows, tm, row_ins, full_ins, outs, scratch=()):
    assert rows % tm == 0
    in_specs = [pl.BlockSpec((tm, a.shape[1]), lambda i: (i, 0)) for a in row_ins]
    in_specs += [_full_spec(a) for a in full_ins]
    trail = [c if isinstance(c, tuple) else (c,) for c, _ in outs]
    out_specs = [pl.BlockSpec((tm,) + t, lambda i, n=len(t): (i,) + (0,) * n) for t in trail]
    out_shape = [jax.ShapeDtypeStruct((rows,) + t, dt) for t, (_, dt) in zip(trail, outs)]
    return pl.pallas_call(
        kernel, name=name, grid=(rows // tm,), in_specs=in_specs, out_specs=out_specs,
        out_shape=out_shape, scratch_shapes=list(scratch),
        compiler_params=_params(("parallel",)))(*row_ins, *full_ins)


def _even_proj_body(x_ref, w_ref, wf_ref, lbl_ref, fb_ref, hq_ref, lf_ref, hk_ref, hv_ref, hgate_ref,
                    fq_ref, fk16_ref, fv16_ref, flf_ref, layer):
    xb = x_ref[...].astype(BF16)

    def blk(j):
        return _dot(xb, w_ref[:, j * HG_W:(j + 1) * HG_W])

    logits = lbl_ref[...]
    e = jnp.exp(logits - jnp.max(logits, axis=0, keepdims=True))
    lb = jnp.sum(e[:layer + 1], axis=0, keepdims=True) / jnp.sum(e, axis=0, keepdims=True)

    hq_ref[...] = blk(0).astype(BF16)
    zf = blk(1)
    lf_ref[...] = jnp.log(lb + (1.0 - lb) * _sigmoid(zf))
    hk_ref[...] = ((1.0 - lb) * _sigmoid(-zf)).astype(BF16)
    hv_ref[...] = blk(2).astype(BF16)
    hgate_ref[...] = _sigmoid(blk(3)).astype(BF16)
    fq_ref[...] = (blk(4) * (FOX_DH ** -0.5 * LOG2E)).astype(BF16)
    fk = blk(5)
    fk16_ref[...] = fk.astype(BF16)
    fv = blk(6)
    fv16_ref[...] = fv.astype(BF16)
    flf_ref[...] = _log_sigmoid(_dot(xb, wf_ref[...]) + fb_ref[...])
    return fk, fv


def _even_proj_kernel(x_ref, w_ref, wf_ref, lbl_ref, fb_ref,
                      hq_ref, lf_ref, hk_ref, hv_ref, hgate_ref,
                      fq_ref, fk_ref, fv_ref, fk16_ref, fv16_ref, flf_ref, *, layer):
    fk, fv = _even_proj_body(x_ref, w_ref, wf_ref, lbl_ref, fb_ref, hq_ref, lf_ref, hk_ref, hv_ref, hgate_ref,
                             fq_ref, fk16_ref, fv16_ref, flf_ref, layer)
    for h in range(FOX_HEADS):
        fk_ref[:, h, :] = fk[:, h * FOX_DH:(h + 1) * FOX_DH]
        fv_ref[:, h, :] = fv[:, h * FOX_DH:(h + 1) * FOX_DH]


_EVEN_OUTS = [(HG_W, BF16), (HG_W, F32), (HG_W, BF16), (HG_W, BF16), (HG_W, BF16), (FOX_W, BF16)]


def _even_proj(x, w_main, w_f, lb_logits, fb_pad, tm, layer):
    rows = x.shape[0]
    outs = _EVEN_OUTS + [((FOX_HEADS, FOX_DH), F32), ((FOX_HEADS, FOX_DH), F32), (FOX_W, BF16), (FOX_W, BF16),
                         (LANES, F32)]
    return _row_call(functools.partial(_even_proj_kernel, layer=layer), "even_proj", rows, tm,
                     [x], [w_main, w_f, lb_logits, fb_pad], outs)


def _even_proj_cache_kernel(x_ref, w_ref, wf_ref, lbl_ref, fb_ref, mk_ref, mv_ref,
                            hq_ref, lf_ref, hk_ref, hv_ref, hgate_ref, fq_ref, fk16_ref, fv16_ref, flf_ref,
                            fk_hbm, fv_hbm, kbuf, vbuf, sem, msem, *, layer, tiles_per_seq):
    i, n = pl.program_id(0), pl.num_programs(0)
    tm = x_ref.shape[0]

    def row_copies(step):
        b = step // tiles_per_seq
        t0 = N_META + (step % tiles_per_seq) * tm
        return [pltpu.make_async_copy(buf.at[:, pl.ds(h * FOX_DH, FOX_DH)], hbm.at[b, pl.ds(t0, tm), h, :], sem.at[a, h])
                for a, (buf, hbm) in enumerate(((kbuf, fk_hbm), (vbuf, fv_hbm))) for h in range(FOX_HEADS)]

    def meta_copies(step):
        b = step // tiles_per_seq
        return [pltpu.make_async_copy(mk_ref, fk_hbm.at[b, pl.ds(0, N_META)], msem.at[0]),
                pltpu.make_async_copy(mv_ref, fv_hbm.at[b, pl.ds(0, N_META)], msem.at[1])]

    fk, fv = _even_proj_body(x_ref, w_ref, wf_ref, lbl_ref, fb_ref, hq_ref, lf_ref, hk_ref, hv_ref, hgate_ref,
                             fq_ref, fk16_ref, fv16_ref, flf_ref, layer)

    @pl.when(i > 0)
    def _():
        for c in row_copies(i - 1):
            c.wait()

    @pl.when((i > 0) & ((i - 1) % tiles_per_seq == 0))
    def _():
        for c in meta_copies(i - 1):
            c.wait()

    kbuf[...] = fk
    vbuf[...] = fv
    for c in row_copies(i):
        c.start()

    @pl.when(i % tiles_per_seq == 0)
    def _():
        for c in meta_copies(i):
            c.start()

    @pl.when(i == n - 1)
    def _():
        for c in row_copies(i):
            c.wait()

    @pl.when((i == n - 1) & (i % tiles_per_seq == 0))
    def _():
        for c in meta_copies(i):
            c.wait()


def _even_proj_cache(x, w_main, w_f, lb_logits, fb_pad, meta_k, meta_v, tm, layer, n_seq, seq_len):
    rows = x.shape[0]
    assert rows == n_seq * seq_len and seq_len % tm == 0
    outs = _EVEN_OUTS + [(FOX_W, BF16), (FOX_W, BF16), (LANES, F32)]
    full = [w_main, w_f, lb_logits, fb_pad, meta_k, meta_v]
    cache = jax.ShapeDtypeStruct((n_seq, N_META + seq_len, FOX_HEADS, FOX_DH), F32)
    return pl.pallas_call(
        functools.partial(_even_proj_cache_kernel, layer=layer, tiles_per_seq=seq_len // tm),
        name="even_proj_cache", grid=(rows // tm,),
        in_specs=[pl.BlockSpec((tm, D_MODEL), lambda i: (i, 0))] + [_full_spec(a) for a in full],
        out_specs=[pl.BlockSpec((tm, c), lambda i: (i, 0)) for c, _ in outs] + [pl.BlockSpec(memory_space=pl.ANY)] * 2,
        out_shape=[jax.ShapeDtypeStruct((rows, c), dt) for c, dt in outs] + [cache, cache],
        scratch_shapes=[pltpu.VMEM((tm, FOX_W), F32), pltpu.VMEM((tm, FOX_W), F32),
                        pltpu.SemaphoreType.DMA((2, FOX_HEADS)), pltpu.SemaphoreType.DMA((2,))],
        compiler_params=_params(("arbitrary",)))(x, *full)


def _bcast_sub(x, j):
    n, c = x.shape
    x3 = x.reshape(n // HG_SUB, HG_SUB, c)
    return jnp.broadcast_to(x3[:, j:j + 1, :], x3.shape).reshape(n, c)


def _level_ref(b, w):
    n, c = b.shape
    parts = [jnp.broadcast_to(b[m * 2 * w + w - 1:m * 2 * w + w, :], (2 * w, c)) for m in range(n // (2 * w))]
    return parts[0] if len(parts) == 1 else jnp.concatenate(parts, axis=0)


def _hgrn2_kernel(q_ref, lf_ref, k_ref, v_ref, gate_ref, flf_ref, g_ref, e_ref, s0_ref, f0_ref,
                  o_ref, fcum_ref, sout_ref, st_scr, fc_scr, *, n_chunks):
    i = pl.program_id(1)
    C = CHUNK

    @pl.when(i == 0)
    def _():
        for h in range(HG_HEADS):
            st_scr[h] = s0_ref[h].T
        fc_scr[...] = f0_ref[...]

    row = lax.broadcasted_iota(jnp.int32, (C, 1), 0)
    col = lax.broadcasted_iota(jnp.int32, (1, C), 1)
    tri = (col <= row).astype(BF16)
    same = lambda w: (row // w) == (col // w)
    levels = (32, 16, 8)

    fc = fc_scr[...]
    for c in range(n_chunks):
        sl = slice(c * C, (c + 1) * C)
        fcum = _cumsum_rows(tri, flf_ref[sl, :]) + fc
        fcum_ref[sl, :] = fcum
        fc = fcum[C - 1:C, :]
    fc_scr[...] = fc

    staged = []
    for c in range(n_chunks):
        sl = slice(c * C, (c + 1) * C)
        per_head = []
        for h0 in range(0, HG_HEADS, HG_GROUP):
            gs = slice(h0 * HG_DK, (h0 + HG_GROUP) * HG_DK)
            b = _cumsum_rows(tri, lf_ref[sl, gs]) * LOG2E
            q = q_ref[sl, gs].astype(F32)
            k = k_ref[sl, gs].astype(F32)
            v = v_ref[sl, gs]
            qb = (q * jnp.exp2(b)).astype(BF16)
            b_last = b[C - 1:C, :]
            kd = (k * jnp.exp2(b_last - b)).astype(BF16)
            e_last = jnp.exp2(b_last)

            pjs = [(jnp.exp2(jnp.where((row % HG_SUB) >= j, b - _bcast_sub(b, j), NEG)) * q
                    * _bcast_sub(k, j)).astype(BF16) for j in range(HG_SUB)]
            lv = []
            for w in levels:
                upper = (row % (2 * w)) >= w
                ew = jnp.exp2(-jnp.abs(b - _level_ref(b, w)))
                lv.append((jnp.where(upper, q * ew, 0.0).astype(BF16), jnp.where(upper, 0.0, k * ew).astype(BF16)))

            for hh in range(HG_GROUP):
                hs = slice(hh * HG_DK, (hh + 1) * HG_DK)
                a = jnp.where(same(HG_SUB), _dot(jnp.concatenate([p[:, hs] for p in pjs], axis=1), e_ref[...]), 0.0)
                for w, (qw, kw) in zip(levels, lv):
                    aw = _dot_nt(qw[:, hs], kw[:, hs])
                    a = a + (aw if 2 * w == C else jnp.where(same(2 * w), aw, 0.0))
                vh = v[:, hs]
                per_head.append((_dot(a.astype(BF16), vh), qb[:, hs], kd[:, hs], e_last[:, hs], vh))
        staged.append(per_head)

    st = [st_scr[h] for h in range(HG_HEADS)]
    for c in range(n_chunks):
        sl = slice(c * C, (c + 1) * C)
        for h, (o_intra, qb_h, kd_h, e_h, vh) in enumerate(staged[c]):
            ho = slice(h * HG_DK, (h + 1) * HG_DK)
            o = o_intra + _dot_nt(qb_h, st[h].astype(BF16))
            st[h] = st[h] * e_h + _dot_tn(vh, kd_h)
            o = _rms_norm(o, g_ref[:, ho])
            o_ref[sl, ho] = (o * gate_ref[sl, ho].astype(F32)).astype(BF16)
    for h in range(HG_HEADS):
        st_scr[h] = st[h]

    @pl.when(i == pl.num_programs(1) - 1)
    def _():
        for h in range(HG_HEADS):
            sout_ref[h] = st_scr[h].T


def _hgrn2(q, lf, k, v, gate, flf, g, e_mat, s0, f0, n_seq, seq_len, row_off, tb):
    assert seq_len % tb == 0 and tb % CHUNK == 0 and row_off % tb == 0
    nb = seq_len // tb
    off = row_off // tb
    per_seq = s0.shape[0] > 1
    rmap = lambda s, i: (off + s * nb + i, 0)
    omap = lambda s, i: (s * nb + i, 0)
    smap = (lambda s, i: (s, 0, 0, 0)) if per_seq else (lambda s, i: (0, 0, 0, 0))
    fmap = (lambda s, i: (s, 0, 0)) if per_seq else (lambda s, i: (0, 0, 0))
    in_specs = [pl.BlockSpec((tb, HG_W), rmap) for _ in range(5)]
    in_specs += [pl.BlockSpec((tb, LANES), rmap), _full_spec(g), _full_spec(e_mat),
                 pl.BlockSpec((None, HG_HEADS, HG_DK, HG_DV), smap), pl.BlockSpec((None, 1, LANES), fmap)]
    out_specs = [pl.BlockSpec((tb, HG_W), omap), pl.BlockSpec((tb, LANES), omap),
                 pl.BlockSpec((None, HG_HEADS, HG_DK, HG_DV), lambda s, i: (s, 0, 0, 0))]
    out_shape = [jax.ShapeDtypeStruct((n_seq * seq_len, HG_W), BF16),
                 jax.ShapeDtypeStruct((n_seq * seq_len, LANES), F32),
                 jax.ShapeDtypeStruct((n_seq, HG_HEADS, HG_DK, HG_DV), F32)]
    scratch = [pltpu.VMEM((HG_HEADS, HG_DV, HG_DK), F32), pltpu.VMEM((1, LANES), F32)]
    return pl.pallas_call(
        functools.partial(_hgrn2_kernel, n_chunks=tb // CHUNK), name="hgrn2",
        grid=(n_seq, nb), in_specs=in_specs, out_specs=out_specs, out_shape=out_shape,
        scratch_shapes=scratch, compiler_params=_params(("parallel", "arbitrary")))(
            q, lf, k, v, gate, flf, g, e_mat, s0, f0)


def _cumsum_kernel(x_ref, tri_ref, o_ref, carry):
    @pl.when(pl.program_id(0) == 0)
    def _():
        carry[...] = jnp.zeros_like(carry)

    hi, mid, lo = _split3(x_ref[...])
    tri = tri_ref[...]
    out = _dot(hi, tri) + _dot(mid, tri) + _dot(lo, tri) + carry[...]
    o_ref[...] = out
    carry[...] = out[:, out.shape[1] - 1:]


def _cumsum_lanes(x, tb):
    r, seq_len = x.shape
    tri = (jnp.arange(tb)[:, None] <= jnp.arange(tb)[None, :]).astype(BF16)
    return pl.pallas_call(
        _cumsum_kernel, name="cumsum", grid=(seq_len // tb,),
        in_specs=[pl.BlockSpec((r, tb), lambda i: (0, i)), _full_spec(tri)],
        out_specs=pl.BlockSpec((r, tb), lambda i: (0, i)),
        out_shape=jax.ShapeDtypeStruct(x.shape, F32),
        scratch_shapes=[pltpu.VMEM((r, 1), F32)],
        compiler_params=_params(("arbitrary",)))(x, tri)


def _flash_kernel(*refs, tq, has_bias, has_rope, mask_mode):
    it = iter(refs)
    q_ref = next(it)
    fq_ref = next(it) if has_bias else None
    kn_ref, vn_ref = next(it), next(it)
    rn_ref = next(it) if has_rope else None
    fkn_ref = next(it) if has_bias else None
    o_ref = next(it)
    m_scr, acc_scr = next(it), next(it)
    dv = o_ref.shape[1]

    q = q_ref[...]
    m_scr[...] = jnp.full(m_scr.shape, NEG, F32)
    acc_scr[...] = jnp.zeros(acc_scr.shape, F32)
    fq_b = jnp.broadcast_to(fq_ref[...] * LOG2E, (tq, LANES)) if has_bias else None

    def scores(k, r, fk):
        if has_rope:
            k = jnp.concatenate([k, r], axis=1)
        s = _dot_nt(q, k.astype(BF16))
        if has_bias:
            s = s + jnp.tile(fq_b, (1, s.shape[1] // LANES)) if s.shape[1] % LANES == 0 else s + fq_b[:, :1]
            s = s - fk * LOG2E
        return s

    def update(s, v, mask):
        if mask is not None:
            s = jnp.where(mask, s, NEG)
        m_prev = m_scr[...]
        m_new = jnp.maximum(m_prev, jnp.max(s, axis=1, keepdims=True))
        alpha = jnp.exp2(m_prev - m_new)
        if s.shape[1] % LANES == 0:
            p = jnp.exp2(s - jnp.tile(m_new, (1, s.shape[1] // LANES)))
        else:
            p = jnp.exp2(s - m_new[:, :1])
        v1 = jnp.concatenate([v.astype(BF16), jnp.ones((v.shape[0], LANES), BF16)], axis=1)
        acc_scr[...] = jnp.tile(alpha, (1, acc_scr.shape[1] // LANES)) * acc_scr[...] + _dot(p.astype(BF16), v1)
        m_scr[...] = m_new

    row = lax.broadcasted_iota(jnp.int32, (tq, 1), 0)
    col = lax.broadcasted_iota(jnp.int32, (1, tq), 1)
    if mask_mode == "causal":
        mask = col <= row
    elif mask_mode == "chunk":
        mask = (col // CHUNK) <= (row // CHUNK)
    else:
        mask = None

    update(scores(kn_ref[...], rn_ref[...] if has_rope else None, fkn_ref[...] if has_bias else None),
           vn_ref[...], mask)
    acc = acc_scr[...]
    o_ref[...] = (acc[:, :dv] / acc[:, dv:]).astype(o_ref.dtype)


def _flash(q, kn, vn, *, n_seq, n_heads, seq_len, dq, dk, dv, mask_mode, fq=None, fkn=None, rn=None):
    tq = seq_len
    has_bias = fq is not None
    has_rope = rn is not None
    ins, specs = [q], [pl.BlockSpec((tq, dq), lambda b, h: (b, h))]
    if has_bias:
        ins.append(fq)
        specs.append(pl.BlockSpec((None, tq, 1), lambda b, h: (h, b, 0)))
    ins += [kn, vn]
    specs += [pl.BlockSpec((seq_len, dk), lambda b, h: (b, h)), pl.BlockSpec((seq_len, dv), lambda b, h: (b, h))]
    if has_rope:
        ins.append(rn)
        specs.append(pl.BlockSpec((seq_len, LANES), lambda b, h: (b, 0)))
    if has_bias:
        ins.append(fkn)
        specs.append(pl.BlockSpec((None, None, 1, seq_len), lambda b, h: (h, b, 0, 0)))
    kern = functools.partial(_flash_kernel, tq=tq, has_bias=has_bias, has_rope=has_rope, mask_mode=mask_mode)
    return pl.pallas_call(
        kern, name="flash", grid=(n_seq, n_heads), in_specs=specs,
        out_specs=pl.BlockSpec((tq, dv), lambda b, h: (b, h)),
        out_shape=jax.ShapeDtypeStruct((n_seq * seq_len, n_heads * dv), BF16),
        scratch_shapes=[pltpu.VMEM((tq, LANES), F32), pltpu.VMEM((tq, dv + LANES), F32)],
        compiler_params=_params(("parallel", "parallel")))(*ins)


def _fox_cached_kernel(q_ref, fq_ref, kp_hbm, vp_hbm, fkp_ref, kn_ref, vn_ref, fkn_ref, o_ref,
                       kbuf, vbuf, ksem, vsem, *, tkp, n_blk):
    b = pl.program_id(0)
    tq = q_ref.shape[0]
    row = lax.broadcasted_iota(jnp.int32, (tq, 1), 0)
    col = lax.broadcasted_iota(jnp.int32, (1, tq), 1)
    ones_p = jnp.ones((tkp, LANES), BF16)
    steps = [(h, j) for h in range(FOX_HEADS) for j in range(n_blk)]

    def copies(i):
        h, j = steps[i]
        slot = i % 2
        src = lambda ref: ref.at[b, pl.ds(j * tkp, tkp), h, :]
        return (pltpu.make_async_copy(src(kp_hbm), kbuf.at[slot], ksem.at[slot]),
                pltpu.make_async_copy(src(vp_hbm), vbuf.at[slot], vsem.at[slot]))

    def update(state, s, v1):
        m_prev, acc = state
        m_new = jnp.maximum(m_prev, jnp.max(s, axis=1, keepdims=True))
        p = jnp.exp2(s - m_new)
        return m_new, jnp.exp2(m_prev - m_new) * acc + _dot(p.astype(BF16), v1)

    def start(i):
        for priority, c in enumerate(copies(i)):
            c.start(priority=priority)

    start(0)
    state = None
    for i, (h, j) in enumerate(steps):
        hs = slice(h * FOX_DH, (h + 1) * FOX_DH)
        q = q_ref[:, hs]
        fq = fq_ref[h] * LOG2E
        if j == 0:
            state = (jnp.full((tq, 1), NEG, F32), jnp.zeros((tq, FOX_DH + LANES), F32))
        if i + 1 < len(steps):
            start(i + 1)
        for c in copies(i):
            c.wait()
        slot = i % 2
        s = _dot_nt(q, kbuf[slot].astype(BF16)) + fq - fkp_ref[h, :, j * tkp:(j + 1) * tkp] * LOG2E
        state = update(state, s, jnp.concatenate([vbuf[slot].astype(BF16), ones_p], axis=1))
        if j == n_blk - 1:
            s = _dot_nt(q, kn_ref[:, hs]) + fq - fkn_ref[h] * LOG2E
            s = jnp.where(col <= row, s, NEG)
            _, acc = update(state, s, jnp.concatenate([vn_ref[:, hs], ones_p[:tq]], axis=1))
            o_ref[:, hs] = (acc[:, :FOX_DH] / acc[:, FOX_DH:FOX_DH + 1]).astype(o_ref.dtype)


def _fox_cached(q, fq, kp, vp, fkp, kn, vn, fkn, n_seq, tq, tkp):
    p = kp.shape[1]
    assert p % tkp == 0
    return pl.pallas_call(
        functools.partial(_fox_cached_kernel, tkp=tkp, n_blk=p // tkp), name="fox_cached", grid=(n_seq,),
        in_specs=[pl.BlockSpec((tq, FOX_W), lambda b: (b, 0)),
                  pl.BlockSpec((FOX_HEADS, tq, 1), lambda b: (0, b, 0)),
                  pl.BlockSpec(memory_space=pl.ANY),
                  pl.BlockSpec(memory_space=pl.ANY),
                  pl.BlockSpec((None, FOX_HEADS, 1, p), lambda b: (b, 0, 0, 0)),
                  pl.BlockSpec((tq, FOX_W), lambda b: (b, 0)),
                  pl.BlockSpec((tq, FOX_W), lambda b: (b, 0)),
                  pl.BlockSpec((FOX_HEADS, None, 1, tq), lambda b: (0, b, 0, 0))],
        out_specs=pl.BlockSpec((tq, FOX_W), lambda b: (b, 0)),
        out_shape=jax.ShapeDtypeStruct((n_seq * tq, FOX_W), BF16),
        scratch_shapes=[pltpu.VMEM((2, tkp, FOX_DH), F32), pltpu.VMEM((2, tkp, FOX_DH), F32),
                        pltpu.SemaphoreType.DMA((2,)), pltpu.SemaphoreType.DMA((2,))],
        compiler_params=_params(("arbitrary",)))(q, fq, kp, vp, fkp, kn, vn, fkn)


FLASH_UNROLL_OFF = 14
FLASH_UNROLL_DIAG = 8


def _tri_tables(nq):
    pairs = [(qi, kj) for qi in range(nq) for kj in range(qi)] + [(qi, qi) for qi in range(nq)] + [(0, 0)]
    return (jnp.array([p[0] for p in pairs], jnp.int32), jnp.array([p[1] for p in pairs], jnp.int32))


def _flash_tri_kernel(qt_ref, kt_ref, *refs, tq, nq, has_bias, has_rope, mask_mode):
    it = iter(refs)
    q_ref = next(it)
    fq_ref = next(it) if has_bias else None
    kp_ref, vp_ref = next(it), next(it)
    rp_ref = next(it) if has_rope else None
    fkp_ref = next(it) if has_bias else None
    kn_ref, vn_ref = next(it), next(it)
    rn_ref = next(it) if has_rope else None
    fkn_ref = next(it) if has_bias else None
    o_ref = next(it)
    m_scr, acc_scr, sa_scr, sb_scr = next(it), next(it), next(it), next(it)
    fqb_scr = next(it) if has_bias else None
    dv = o_ref.shape[1]
    n_off = nq * (nq - 1) // 2
    tile = lambda j: pl.ds(pl.multiple_of(j * tq, tq), tq)
    ones = jnp.ones((tq, LANES), BF16)

    kp = kp_ref[...]
    if has_rope:
        kp = jnp.concatenate([kp, rp_ref[...]], axis=1)
    vp1 = jnp.concatenate([vp_ref[...], ones[:vp_ref.shape[0]]], axis=1)
    for i in range(nq):
        rs = slice(i * tq, (i + 1) * tq)
        s = _dot_nt(q_ref[rs, :], kp)
        if has_bias:
            fb = fq_ref[rs, :] * LOG2E
            fqb_scr[rs, :] = fb
            s = s + fb[:, :s.shape[1]] - fkp_ref[...] * LOG2E
        m0 = jnp.max(s, axis=1, keepdims=True)
        m_scr[i] = jnp.broadcast_to(m0, (tq, LANES))
        acc_scr[i] = _dot(jnp.exp2(s - m0).astype(BF16), vp1)

    def fill(s_ref, t):
        qs, ks = tile(qt_ref[t]), tile(kt_ref[t])
        k = kn_ref[ks, :]
        if has_rope:
            k = jnp.concatenate([k, rn_ref[ks, :]], axis=1)
        s = _dot_nt(q_ref[qs, :], k)
        if has_bias:
            s = s + jnp.tile(fqb_scr[qs, :], (1, tq // LANES)) - fkn_ref[:, ks] * LOG2E
        s_ref[...] = s

    def drain(s_ref, t, mask):
        qi = qt_ref[t]
        s = s_ref[...]
        if mask is not None:
            s = jnp.where(mask, s, NEG)
        m_prev = m_scr[qi]
        m_new = jnp.maximum(m_prev, jnp.max(s, axis=1, keepdims=True))
        p = jnp.exp2(s - jnp.tile(m_new, (1, tq // LANES)))
        v1 = jnp.concatenate([vn_ref[tile(kt_ref[t]), :], ones], axis=1)
        acc = jnp.tile(jnp.exp2(m_prev - m_new), (1, (dv + LANES) // LANES)) * acc_scr[qi] + _dot(p.astype(BF16), v1)
        return qi, m_new, acc

    def keep(s_ref, t):
        qi, m_new, acc = drain(s_ref, t, None)
        m_scr[qi] = m_new
        acc_scr[qi] = acc

    row = lax.broadcasted_iota(jnp.int32, (tq, 1), 0)
    col = lax.broadcasted_iota(jnp.int32, (1, tq), 1)
    mask = {"causal": col <= row, "chunk": (col // CHUNK) <= (row // CHUNK)}[mask_mode]

    def finish(s_ref, t):
        qi, _, acc = drain(s_ref, t, mask)
        o_ref[tile(qi), :] = (acc[:, :dv] / acc[:, dv:]).astype(o_ref.dtype)

    def pipeline(t0, n, unroll, consume):
        assert n % unroll == 0 and unroll % 2 == 0

        def body(i, carry):
            t = t0 + unroll * i
            for u in range(0, unroll, 2):
                fill(sb_scr, t + u + 1)
                consume(sa_scr, t + u)
                fill(sa_scr, t + u + 2)
                consume(sb_scr, t + u + 1)
            return carry
        lax.fori_loop(0, n // unroll, body, 0)

    fill(sa_scr, 0)
    pipeline(0, n_off, FLASH_UNROLL_OFF, keep)
    pipeline(n_off, nq, FLASH_UNROLL_DIAG, finish)


def _flash_tri(q, kn, vn, *, n_seq, n_heads, seq_len, tq, dq, dk, dv, mask_mode, past, fq=None, fkn=None, rn=None):
    nq = seq_len // tq
    has_bias = fq is not None
    has_rope = rn is not None
    tp = past["k"].shape[1]
    m3 = lambda f: (lambda b, h, qt, kt: f(b, h))
    ins, specs = [q], [pl.BlockSpec((seq_len, dq), m3(lambda b, h: (b, h)))]
    if has_bias:
        ins.append(fq)
        specs.append(pl.BlockSpec((None, seq_len, LANES), m3(lambda b, h: (h, b, 0))))
    ins += [past["k"], past["v"]]
    specs += [pl.BlockSpec((None, tp, dk), m3(lambda b, h: (0, 0, h))),
              pl.BlockSpec((None, tp, dv), m3(lambda b, h: (0, 0, h)))]
    if has_rope:
        ins.append(past["r"])
        specs.append(pl.BlockSpec((None, tp, LANES), m3(lambda b, h: (0, 0, 0))))
    if has_bias:
        ins.append(past["fk"])
        specs.append(pl.BlockSpec((None, None, 1, tp), m3(lambda b, h: (0, h, 0, 0))))
    ins += [kn, vn]
    specs += [pl.BlockSpec((seq_len, dk), m3(lambda b, h: (b, h))),
              pl.BlockSpec((seq_len, dv), m3(lambda b, h: (b, h)))]
    if has_rope:
        ins.append(rn)
        specs.append(pl.BlockSpec((seq_len, LANES), m3(lambda b, h: (b, 0))))
    if has_bias:
        ins.append(fkn)
        specs.append(pl.BlockSpec((None, None, 1, seq_len), m3(lambda b, h: (h, b, 0, 0))))
    scratch = [pltpu.VMEM((nq, tq, LANES), F32), pltpu.VMEM((nq, tq, dv + LANES), F32),
               pltpu.VMEM((tq, tq), F32), pltpu.VMEM((tq, tq), F32)]
    if has_bias:
        scratch.append(pltpu.VMEM((seq_len, LANES), F32))
    grid_spec = pltpu.PrefetchScalarGridSpec(
        num_scalar_prefetch=2, grid=(n_seq, n_heads), in_specs=specs,
        out_specs=pl.BlockSpec((seq_len, dv), m3(lambda b, h: (b, h))), scratch_shapes=scratch)
    kern = functools.partial(_flash_tri_kernel, tq=tq, nq=nq, has_bias=has_bias, has_rope=has_rope,
                             mask_mode=mask_mode)
    return pl.pallas_call(
        kern, name="flash_tri", grid_spec=grid_spec,
        out_shape=jax.ShapeDtypeStruct((n_seq * seq_len, n_heads * dv), BF16),
        compiler_params=_params(("parallel", "arbitrary")))(*_tri_tables(nq), *ins)


def _route(sc, sb):
    def top2_sum(v):
        a, b, c, d = v
        a, b = jnp.maximum(a, b), jnp.minimum(a, b)
        c, d = jnp.maximum(c, d), jnp.minimum(c, d)
        hi, lo2 = jnp.maximum(a, c), jnp.minimum(a, c)
        return hi + jnp.maximum(lo2, jnp.maximum(b, d))

    gs = [top2_sum(sb[g * EXPERTS_PER_GROUP:(g + 1) * EXPERTS_PER_GROUP]) for g in range(N_GROUPS)]
    best_v, best_g = gs[0], jnp.zeros(gs[0].shape, jnp.int32)
    for g in range(1, N_GROUPS):
        upd = gs[g] > best_v
        best_v = jnp.where(upd, gs[g], best_v)
        best_g = jnp.where(upd, g, best_g)
    masked = [jnp.where(best_g == (e // EXPERTS_PER_GROUP), sb[e], -jnp.inf) for e in range(N_EXPERTS)]

    def argmax_first(vals, exclude=None):
        bv = jnp.full(vals[0].shape, -jnp.inf, F32)
        bi = jnp.full(vals[0].shape, -1, jnp.int32)
        for e, v in enumerate(vals):
            upd = v > bv
            if exclude is not None:
                upd = upd & (exclude != e)
            bv = jnp.where(upd, v, bv)
            bi = jnp.where(upd, e, bi)
        return bi

    i1 = argmax_first(masked)
    i2 = argmax_first(masked, exclude=i1)
    w1 = sum(jnp.where(i1 == e, sc[e], 0.0) for e in range(N_EXPERTS))
    w2 = sum(jnp.where(i2 == e, sc[e], 0.0) for e in range(N_EXPERTS))
    tot = w1 + w2
    w1, w2 = w1 / tot, w2 / tot
    comb = [jnp.where(i1 == e, w1, 0.0) + jnp.where(i2 == e, w2, 0.0) for e in range(N_EXPERTS)]
    return comb + [i1.astype(F32), i2.astype(F32), w1, w2]


def _mix_kernel(*refs, n_act):
    x_ref = refs[0]
    a_refs = refs[1:1 + n_act]
    w_ref, g_ref, b_ref, rw_ref, rb_ref, x1_ref, x1p_ref, comb_ref, ct_scr = refs[1 + n_act:]
    half = D_MODEL // 2
    tm = x_ref.shape[0]
    group = MIX_GROUP if tm % MIX_GROUP == 0 else tm
    ct_scr[...] = jnp.zeros(ct_scr.shape, F32)
    for r0 in range(0, tm, group):
        rs = slice(r0, r0 + group)
        ys = []
        for n0 in (0, half):
            y = None
            k0 = 0
            for a_ref in a_refs:
                kw = a_ref.shape[1]
                part = _dot(a_ref[rs, :], w_ref[k0:k0 + kw, n0:n0 + half])
                y = part if y is None else y + part
                k0 += kw
            ys.append(y)
        x1 = _layer_norm(ALPHA * x_ref[rs, :] + jnp.concatenate(ys, axis=1), g_ref[...], b_ref[...])
        x1_ref[rs, :] = x1

        x1p_ref[rs, :] = _pack_pair(x1[:, :half], x1[:, half:])
        logits = _dot(x1.astype(BF16), rw_ref[...])
        scores_t = _sigmoid(logits).T
        sc = [scores_t[e:e + 1, :] for e in range(N_EXPERTS)]
        sb = [sc[e] + rb_ref[e:e + 1, :] for e in range(N_EXPERTS)]
        for r, val in enumerate(_route(sc, sb)):
            ct_scr[r:r + 1, rs] = val
        comb_ref[rs, :] = ct_scr[:, rs].T


def _mix(x, acts, w_out, ln_g, ln_b, rw, rb, tm):
    rows = x.shape[0]
    return _row_call(functools.partial(_mix_kernel, n_act=len(acts)), "mix", rows, tm,
                     [x] + list(acts), [w_out, ln_g, ln_b, rw, rb],
                     [(D_MODEL, F32), (D_MODEL // 2, jnp.uint32), (LANES, F32)],
                     scratch=[pltpu.VMEM((LANES, tm), F32)])


def _moe_kernel(x_ref, comb_ref, wg_ref, wu_ref, wd_ref, g_ref, b_ref, o_ref, xb_scr, acc_scr):
    e = pl.program_id(1)

    @pl.when(e == 0)
    def _():
        xb_scr[...] = x_ref[...].astype(BF16)
        acc_scr[...] = jnp.zeros(acc_scr.shape, F32)

    xb = xb_scr[...]
    lane = lax.broadcasted_iota(jnp.int32, (1, LANES), 1)
    c_e = jnp.sum(jnp.where(lane == e, comb_ref[...], 0.0), axis=1, keepdims=True)
    gate = _dot(xb, wg_ref[...].astype(BF16))
    h = gate * _sigmoid(gate) * _dot(xb, wu_ref[...].astype(BF16))
    acc_scr[...] += _dot((h * c_e).astype(BF16), wd_ref[...].astype(BF16))

    @pl.when(e == N_EXPERTS - 1)
    def _():
        o_ref[...] = _layer_norm(ALPHA * x_ref[...] + acc_scr[...], g_ref[...], b_ref[...])


def _moe(x, comb, wg, wu, wd, layer, ln_g, ln_b, tm):
    rows = x.shape[0]
    assert rows % tm == 0
    return pl.pallas_call(
        _moe_kernel, name="moe", grid=(rows // tm, N_EXPERTS),
        in_specs=[pl.BlockSpec((tm, D_MODEL), lambda i, e: (i, 0)),
                  pl.BlockSpec((tm, LANES), lambda i, e: (i, 0)),
                  pl.BlockSpec((None, None, D_MODEL, D_EXPERT), lambda i, e: (layer, e, 0, 0)),
                  pl.BlockSpec((None, None, D_MODEL, D_EXPERT), lambda i, e: (layer, e, 0, 0)),
                  pl.BlockSpec((None, None, D_EXPERT, D_MODEL), lambda i, e: (layer, e, 0, 0)),
                  _full_spec(ln_g), _full_spec(ln_b)],
        out_specs=pl.BlockSpec((tm, D_MODEL), lambda i, e: (i, 0)),
        out_shape=jax.ShapeDtypeStruct((rows, D_MODEL), F32),
        scratch_shapes=[pltpu.VMEM((tm, D_MODEL), BF16), pltpu.VMEM((tm, D_MODEL), F32)],
        compiler_params=_params(("parallel", "arbitrary")))(x, comb, wg, wu, wd, ln_g, ln_b)


ROUTE_E1, ROUTE_E2, ROUTE_W1, ROUTE_W2 = N_EXPERTS, N_EXPERTS + 1, N_EXPERTS + 2, N_EXPERTS + 3
TE = 1024
SC_WINDOW = 128
RANK_TILE = 1024
MIX_GROUP = 256


def _pack_pair(a, b):
    au = lax.bitcast_convert_type(a.astype(BF16).astype(F32), jnp.uint32)
    bu = lax.bitcast_convert_type(b.astype(BF16).astype(F32), jnp.uint32)
    return (au >> 16) | (bu & jnp.uint32(0xFFFF0000))


def _unpack_pair(w):
    a = lax.bitcast_convert_type(w << 16, F32)
    b = lax.bitcast_convert_type(w & jnp.uint32(0xFFFF0000), F32)
    return a, b


def _rank_kernel(route_ref, pos_ref, texp_ref, nused_ref, cnt_scr, carry_scr, seg_scr, before_scr):
    ph, i = pl.program_id(0), pl.program_id(1)
    T = route_ref.shape[0]
    lane = lax.broadcasted_iota(jnp.int32, (1, LANES), 1)
    lane_f = lane.astype(F32)
    r = route_ref[...]
    e1, e2 = r[:, ROUTE_E1:ROUTE_E1 + 1], r[:, ROUTE_E2:ROUTE_E2 + 1]
    m1, m2 = lane_f == e1, lane_f == e2
    m = jnp.where(m1 | m2, 1.0, 0.0)
    colsum = jnp.sum(m, axis=0, keepdims=True)

    @pl.when((ph == 0) & (i == 0))
    def _():
        cnt_scr[...] = jnp.zeros(cnt_scr.shape, F32)

    @pl.when(ph == 0)
    def _():
        cnt_scr[...] += colsum

    @pl.when((ph == 1) & (i == 0))
    def _():
        cnt = cnt_scr[...].astype(jnp.int32)
        padded = (((cnt + (TE - 1)) // TE) * TE).astype(F32)
        rr = lax.broadcasted_iota(jnp.int32, (LANES, 1), 0)
        upper = (rr < lane).astype(BF16)
        hi, mid, lo = _split3(jnp.broadcast_to(padded, (HG_SUB, LANES)))
        seg = (_dot(hi, upper) + _dot(mid, upper) + _dot(lo, upper))[:1, :]
        seg_scr[...] = seg
        carry_scr[...] = jnp.zeros(carry_scr.shape, F32)
        seg_end = seg + padded
        tile_row = lax.broadcasted_iota(jnp.int32, texp_ref.shape, 1).astype(F32) * float(TE)
        te_acc = jnp.zeros(texp_ref.shape, jnp.int32)
        for e in range(N_EXPERTS):
            te_acc = te_acc + jnp.where(seg_end[:, e:e + 1] <= tile_row, 1, 0)
        texp_ref[...] = jnp.minimum(te_acc, N_EXPERTS - 1)
        nused_ref[...] = jnp.broadcast_to(seg_end[:, N_EXPERTS - 1:N_EXPERTS] / float(TE), nused_ref.shape).astype(jnp.int32)

    @pl.when((ph == 1) & (i == 0))
    def _():
        row = lax.broadcasted_iota(jnp.int32, (T, 1), 0)
        col = lax.broadcasted_iota(jnp.int32, (1, T), 1)
        before_scr[...] = (col < row).astype(BF16)

    @pl.when(ph == 1)
    def _():
        cum = _dot(before_scr[...], m.astype(BF16)) + carry_scr[...] + seg_scr[...]
        p1 = jnp.sum(jnp.where(m1, cum, 0.0), axis=1, keepdims=True)
        p2 = jnp.sum(jnp.where(m2, cum, 0.0), axis=1, keepdims=True)
        pos_ref[...] = jnp.where(lane == 0, p1, jnp.where(lane == 1, p2, 0.0)).astype(jnp.int32)
        carry_scr[...] += colsum


def _rank(route, n_tiles, tm):
    rows = route.shape[0]
    nb = rows // tm
    nt_pad = -(-n_tiles // LANES) * LANES
    return pl.pallas_call(
        _rank_kernel, name="rank", grid=(2, nb),
        in_specs=[pl.BlockSpec((tm, LANES), lambda ph, i: (i, 0))],
        out_specs=[pl.BlockSpec((tm, LANES), lambda ph, i: (i * ph, 0)),
                   pl.BlockSpec((1, nt_pad), lambda ph, i: (0, 0)),
                   pl.BlockSpec((1, LANES), lambda ph, i: (0, 0))],
        out_shape=[jax.ShapeDtypeStruct((rows, LANES), jnp.int32),
                   jax.ShapeDtypeStruct((1, nt_pad), jnp.int32),
                   jax.ShapeDtypeStruct((1, LANES), jnp.int32)],
        scratch_shapes=[pltpu.VMEM((1, LANES), F32), pltpu.VMEM((1, LANES), F32), pltpu.VMEM((1, LANES), F32),
                        pltpu.VMEM((tm, tm), BF16)],
        compiler_params=_params(("arbitrary", "arbitrary")))(route)


def _sc_mesh():
    return plsc.VectorSubcoreMesh(core_axis_name="c", subcore_axis_name="s")


def _sc_scatter_rows(x, idx, n_out):
    rows, d = x.shape
    mesh = _sc_mesh()
    n_workers = mesh.num_cores * mesh.num_subcores
    steps = idx.shape[1] // SC_WINDOW // n_workers
    assert steps * SC_WINDOW * n_workers == idx.shape[1] and rows % SC_WINDOW == 0

    @functools.partial(pl.kernel, out_type=jax.ShapeDtypeStruct((n_out, d), x.dtype), mesh=mesh,
                       scratch_types=[pltpu.VMEM((1, SC_WINDOW), jnp.int32), pltpu.VMEM((SC_WINDOW, d), x.dtype)])
    def scatter(x_hbm, i_hbm, o_hbm, i_vmem, buf):
        first = (lax.axis_index("c") * mesh.num_subcores + lax.axis_index("s")) * steps

        @pl.loop(0, steps)
        def _(t):
            off = (first + t) * SC_WINDOW
            pltpu.sync_copy(i_hbm.at[:, pl.ds(off, SC_WINDOW)], i_vmem)
            pltpu.sync_copy(x_hbm.at[pl.ds(off % rows, SC_WINDOW)], buf)
            pltpu.sync_copy(buf, o_hbm.at[i_vmem.at[0]])

    return scatter(x, idx)


def _sc_gather_rows(x, idx):
    d = x.shape[1]
    n = idx.shape[1]
    mesh = _sc_mesh()
    n_workers = mesh.num_cores * mesh.num_subcores
    steps = n // SC_WINDOW // n_workers
    assert steps * SC_WINDOW * n_workers == n

    @functools.partial(pl.kernel, out_type=jax.ShapeDtypeStruct((n, d), x.dtype), mesh=mesh,
                       scratch_types=[pltpu.VMEM((1, SC_WINDOW), jnp.int32), pltpu.VMEM((SC_WINDOW, d), x.dtype)])
    def gather(x_hbm, i_hbm, o_hbm, i_vmem, buf):
        first = (lax.axis_index("c") * mesh.num_subcores + lax.axis_index("s")) * steps

        @pl.loop(0, steps)
        def _(t):
            off = (first + t) * SC_WINDOW
            pltpu.sync_copy(i_hbm.at[:, pl.ds(off, SC_WINDOW)], i_vmem)
            pltpu.sync_copy(x_hbm.at[i_vmem.at[0]], buf)
            pltpu.sync_copy(buf, o_hbm.at[pl.ds(off, SC_WINDOW)])

    return gather(x, idx)


def _gmm_kernel(texp_ref, nused_ref, x_ref, wg_ref, wu_ref, wd_ref, o_ref):
    @pl.when(pl.program_id(0) < nused_ref[0])
    def _():
        a, b = _unpack_pair(x_ref[...])
        xb = jnp.concatenate([a.astype(BF16), b.astype(BF16)], axis=1)
        gate = _dot(xb, wg_ref[...].astype(BF16))
        h = gate * _sigmoid(gate) * _dot(xb, wu_ref[...].astype(BF16))
        y = _dot(h.astype(BF16), wd_ref[...].astype(BF16))
        o_ref[...] = _pack_pair(y[:, :D_MODEL // 2], y[:, D_MODEL // 2:])


def _gmm(xs, texp, nused, wg, wu, wd, layer):
    rows = xs.shape[0]
    wmap = lambda d, te, nu: (layer, te[d], 0, 0)
    grid_spec = pltpu.PrefetchScalarGridSpec(
        num_scalar_prefetch=2, grid=(rows // TE,),
        in_specs=[pl.BlockSpec((TE, D_MODEL // 2), lambda d, te, nu: (d, 0)),
                  pl.BlockSpec((None, None, D_MODEL, D_EXPERT), wmap),
                  pl.BlockSpec((None, None, D_MODEL, D_EXPERT), wmap),
                  pl.BlockSpec((None, None, D_EXPERT, D_MODEL), wmap)],
        out_specs=pl.BlockSpec((TE, D_MODEL // 2), lambda d, te, nu: (d, 0)))
    return pl.pallas_call(
        _gmm_kernel, name="gmm", grid_spec=grid_spec,
        out_shape=jax.ShapeDtypeStruct((rows, D_MODEL // 2), jnp.uint32),
        compiler_params=_params(("arbitrary",)))(texp, nused, xs, wg, wu, wd)


def _combine_kernel(x_ref, g0_ref, g1_ref, route_ref, g_ref, b_ref, o_ref):
    r = route_ref[...]
    y0 = jnp.concatenate(_unpack_pair(g0_ref[...]), axis=1)
    y1 = jnp.concatenate(_unpack_pair(g1_ref[...]), axis=1)
    f = y0 * r[:, ROUTE_W1:ROUTE_W1 + 1] + y1 * r[:, ROUTE_W2:ROUTE_W2 + 1]
    o_ref[...] = _layer_norm(ALPHA * x_ref[...] + f, g_ref[...], b_ref[...])


def _combine(x, g, route, ln_g, ln_b, tm):
    rows = x.shape[0]
    nb = rows // tm
    return pl.pallas_call(
        _combine_kernel, name="combine", grid=(nb,),
        in_specs=[pl.BlockSpec((tm, D_MODEL), lambda i: (i, 0)),
                  pl.BlockSpec((tm, D_MODEL // 2), lambda i: (i, 0)),
                  pl.BlockSpec((tm, D_MODEL // 2), lambda i: (nb + i, 0)),
                  pl.BlockSpec((tm, LANES), lambda i: (i, 0)), _full_spec(ln_g), _full_spec(ln_b)],
        out_specs=pl.BlockSpec((tm, D_MODEL), lambda i: (i, 0)),
        out_shape=jax.ShapeDtypeStruct((rows, D_MODEL), F32),
        compiler_params=_params(("parallel",)))(x, g, g, route, ln_g, ln_b)


def _moe_routed(x1, x1b, route, wg, wu, wd, layer, ln_g, ln_b, tm):
    rows = x1.shape[0]
    n_rows = 2 * rows + N_EXPERTS * TE
    pos, texp, nused = _rank(route, n_rows // TE, RANK_TILE)
    idx = jnp.concatenate([pos[:, 0], pos[:, 1]])[None, :]
    xs = _sc_scatter_rows(x1b, idx, n_rows)
    ys = _gmm(xs, texp[0, :n_rows // TE], nused[0, :1], wg, wu, wd, layer)
    g = _sc_gather_rows(ys, idx)
    return _combine(x1, g, route, ln_g, ln_b, tm)


def _rope128(x, cos_t, sin_t):
    lane = lax.broadcasted_iota(jnp.int32, (1, LANES), 1)
    half = MLA_ROPE // 2
    swapped = jnp.where(lane < half, pltpu.roll(x, LANES - half, axis=1), pltpu.roll(x, half, axis=1))
    return x * cos_t + swapped * sin_t


def _odd_proj_kernel(x_ref, cos_ref, sin_ref, w_ref, gq_ref, gkv_ref, wuq_ref, wukv_ref,
                     q_ref, ckv_ref, kpe_ref, kpe16_ref, kn_ref, vn_ref):
    tm = x_ref.shape[0]
    group = MIX_GROUP if tm % MIX_GROUP == 0 else tm
    scale = (MLA_NOPE + MLA_ROPE) ** -0.5 * LOG2E
    n_k = MLA_HEADS * MLA_NOPE
    for r0 in range(0, tm, group):
        rs = slice(r0, r0 + group)
        z = _dot(x_ref[rs, :].astype(BF16), w_ref[...])
        cq = _rms_norm(z[:, :MLA_Q_LORA], gq_ref[...])
        ckv = _rms_norm(z[:, MLA_Q_LORA:MLA_Q_LORA + MLA_KV_LORA], gkv_ref[...])
        ckv_ref[rs, :] = ckv
        kv = _dot(ckv.astype(BF16), wukv_ref[...])
        kn_ref[rs, :] = kv[:, :n_k].astype(BF16)
        vn_ref[rs, :] = kv[:, n_k:].astype(BF16)
        cos_t, sin_t = cos_ref[rs, :], sin_ref[rs, :]
        kpe = _rope128(z[:, MLA_Q_LORA + MLA_KV_LORA:], cos_t, sin_t)
        kpe_ref[rs, :] = kpe[:, :MLA_ROPE]
        kpe16_ref[rs, :] = kpe.astype(BF16)
        qf = _dot(cq.astype(BF16), wuq_ref[...])
        for h in range(MLA_HEADS):
            c0 = h * MLA_QPAD
            q_ref[rs, c0:c0 + MLA_NOPE] = (qf[:, c0:c0 + MLA_NOPE] * scale).astype(BF16)
            qr = _rope128(qf[:, c0 + MLA_NOPE:c0 + MLA_QPAD], cos_t, sin_t)
            q_ref[rs, c0 + MLA_NOPE:c0 + MLA_QPAD] = (qr * scale).astype(BF16)


def _odd_proj(x, cos_t, sin_t, w_in, gq, gkv, wuq, w_ukv, tm):
    rows = x.shape[0]
    period = cos_t.shape[0] // tm
    assert rows % tm == 0 and cos_t.shape[0] % tm == 0
    outs = [(MLA_HEADS * MLA_QPAD, BF16), (MLA_KV_LORA, F32), (MLA_ROPE, F32), (LANES, BF16),
            (MLA_HEADS * MLA_NOPE, BF16), (MLA_HEADS * MLA_V, BF16)]
    full = [w_in, gq, gkv, wuq, w_ukv]
    table = pl.BlockSpec((tm, LANES), lambda i: (i % period, 0))
    return pl.pallas_call(
        _odd_proj_kernel, name="odd_proj", grid=(rows // tm,),
        in_specs=[pl.BlockSpec((tm, D_MODEL), lambda i: (i, 0)), table, table] + [_full_spec(a) for a in full],
        out_specs=[pl.BlockSpec((tm, c), lambda i: (i, 0)) for c, _ in outs],
        out_shape=[jax.ShapeDtypeStruct((rows, c), dt) for c, dt in outs],
        compiler_params=_params(("parallel",)))(x, cos_t, sin_t, *full)


def _mla_absorbed_kernel(q_ref, cp_ref, rp_ref, cn_ref, rn_ref, wuk_ref, wuv_ref, o_ref, m_scr, acc_scr, *, tkp):
    tq = q_ref.shape[0]
    rows = MLA_HEADS * tq
    q = q_ref[...]
    qa = []
    for h in range(MLA_HEADS):
        c0 = h * MLA_QPAD
        q_abs = _dot_nt(q[:, c0:c0 + MLA_NOPE], wuk_ref[h])
        qa.append(jnp.concatenate([q_abs.astype(BF16), q[:, c0 + MLA_NOPE:c0 + MLA_NOPE + MLA_ROPE]], axis=1))
    qs = jnp.concatenate(qa, axis=0)
    m_scr[...] = jnp.full(m_scr.shape, NEG, F32)
    acc_scr[...] = jnp.zeros(acc_scr.shape, F32)

    def update(c, r):
        c = c.astype(BF16)
        s = _dot_nt(qs, jnp.concatenate([c, r.astype(BF16)], axis=1))
        m_prev = m_scr[...]
        m_new = jnp.maximum(m_prev, jnp.max(s, axis=1, keepdims=True))
        if s.shape[1] % LANES == 0:
            p = jnp.exp2(s - jnp.tile(m_new, (1, s.shape[1] // LANES)))
        else:
            p = jnp.exp2(s - m_new[:, :1])
        c1 = jnp.concatenate([c, jnp.ones((c.shape[0], LANES), BF16)], axis=1)
        acc_scr[...] = (jnp.tile(jnp.exp2(m_prev - m_new), (1, acc_scr.shape[1] // LANES)) * acc_scr[...]
                        + _dot(p.astype(BF16), c1))
        m_scr[...] = m_new

    def past_body(j, carry):
        rs = pl.ds(pl.multiple_of(j * tkp, tkp), tkp)
        update(cp_ref[rs, :], rp_ref[rs, :])
        return carry
    lax.fori_loop(0, cp_ref.shape[0] // tkp, past_body, 0)
    update(cn_ref[...], rn_ref[:, :MLA_ROPE])

    acc = acc_scr[...]
    lat = (acc[:, :MLA_KV_LORA] / jnp.tile(acc[:, MLA_KV_LORA:], (1, MLA_KV_LORA // LANES))).astype(BF16)
    for h in range(MLA_HEADS):
        o_ref[:, h * MLA_V:(h + 1) * MLA_V] = _dot(lat[h * tq:(h + 1) * tq, :],
                                                   wuv_ref[:, h * MLA_V:(h + 1) * MLA_V]).astype(o_ref.dtype)


def _mla_absorbed(q, ckv_past, kpe_past, ckv_new, kpe_new, wuk_t, wuv, n_seq, tq, tkp):
    p = ckv_past.shape[1]
    assert p % tkp == 0
    rows = MLA_HEADS * tq
    return pl.pallas_call(
        functools.partial(_mla_absorbed_kernel, tkp=tkp), name="mla_absorbed", grid=(n_seq,),
        in_specs=[pl.BlockSpec((tq, MLA_HEADS * MLA_QPAD), lambda b: (b, 0)),
                  pl.BlockSpec((None, p, MLA_KV_LORA), lambda b: (b, 0, 0)),
                  pl.BlockSpec((None, p, MLA_ROPE), lambda b: (b, 0, 0)),
                  pl.BlockSpec((tq, MLA_KV_LORA), lambda b: (b, 0)),
                  pl.BlockSpec((tq, LANES), lambda b: (b, 0)),
                  _full_spec(wuk_t), _full_spec(wuv)],
        out_specs=pl.BlockSpec((tq, MLA_HEADS * MLA_V), lambda b: (b, 0)),
        out_shape=jax.ShapeDtypeStruct((n_seq * tq, MLA_HEADS * MLA_V), BF16),
        scratch_shapes=[pltpu.VMEM((rows, LANES), F32), pltpu.VMEM((rows, MLA_KV_LORA + LANES), F32)],
        compiler_params=_params(("parallel",)))(q, ckv_past, kpe_past, ckv_new, kpe_new, wuk_t, wuv)


def _rope_tables(pos):
    half = MLA_ROPE // 2
    inv = ROPE_BASE ** (-jnp.arange(half, dtype=F32) / half)
    ang = pos.astype(F32)[:, None] * inv[None, :]
    cos, sin = jnp.cos(ang), jnp.sin(ang)
    z = jnp.zeros((pos.shape[0], LANES - MLA_ROPE), F32)
    return jnp.concatenate([cos, cos, z], axis=1), jnp.concatenate([-sin, sin, z], axis=1)


def _pad_rows(a, n):
    return jnp.pad(a, ((0, n - a.shape[0]),) + ((0, 0),) * (a.ndim - 1))


def kernel(x_prompt, x_sample, state_hgrn2, cache_fox_k, cache_fox_v, cache_fox_logf, cache_mla_ckv, cache_mla_kpe, meta_tokens, even_w_in, hg_lb_logits, hg_norm_g, fox_forget_bias, even_w_out, mla_w_in, mla_q_norm_g, mla_kv_norm_g, mla_w_uq, mla_w_uk, mla_w_uv, mla_w_out, ln_mix_g, ln_mix_b, ln_ffn_g, ln_ffn_b, router_w, router_bias, moe_w_gate, moe_w_up, moe_w_down):
    B, T, _ = x_prompt.shape
    Bs, Ts, _ = x_sample.shape
    P = cache_fox_k.shape[2]
    RM = B * T
    RS = Bs * Ts
    RSM = -(-(RS + N_META) // LANES) * LANES
    ME = slice(RS, RS + N_META)
    TM_MAIN, TM_MOE, TQ = TILE_PROJ, TILE_FFN, TILE_ATTN

    xm = x_prompt.reshape(RM, D_MODEL)
    xs = _pad_rows(jnp.concatenate([x_sample.reshape(RS, D_MODEL), meta_tokens.astype(F32)], axis=0), RSM)

    w_in0 = even_w_in[0]
    n_main = 7 * HG_W
    w_even = w_in0[:, :n_main].astype(BF16)
    w_even_f = jnp.pad(w_in0[:, n_main:], ((0, 0), (0, LANES - FOX_HEADS))).astype(BF16)
    fb_pad = jnp.pad(fox_forget_bias[0][None, :], ((0, 0), (0, LANES - FOX_HEADS)))
    g_hg = hg_norm_g[0].reshape(1, HG_W)
    w_out0 = even_w_out[0].astype(BF16)
    e_mat = ((jnp.arange(HG_SUB * HG_DK)[:, None] // HG_DK) == (jnp.arange(CHUNK)[None, :] % HG_SUB)).astype(BF16)

    w_odd = jnp.pad(mla_w_in[0], ((0, 0), (0, LANES - MLA_ROPE))).astype(BF16)
    gq = mla_q_norm_g[0][None, :]
    gkv = mla_kv_norm_g[0][None, :]
    wuq = mla_w_uq[0].reshape(MLA_Q_LORA, MLA_HEADS, MLA_NOPE + MLA_ROPE)
    wuq = jnp.pad(wuq, ((0, 0), (0, 0), (0, MLA_QPAD - MLA_NOPE - MLA_ROPE)))
    wuq = wuq.reshape(MLA_Q_LORA, MLA_HEADS * MLA_QPAD).astype(BF16)
    w_ukv = jnp.concatenate([mla_w_uk[0].reshape(MLA_KV_LORA, -1), mla_w_uv[0].reshape(MLA_KV_LORA, -1)],
                            axis=1).astype(BF16)
    w_out1 = mla_w_out[0].astype(BF16)

    rw = jnp.pad(router_w, ((0, 0), (0, LANES - N_EXPERTS))).astype(BF16)
    rb = jnp.pad(router_bias.astype(F32)[:, None], ((0, LANES - N_EXPERTS), (0, 0)))
    experts = (moe_w_gate, moe_w_up, moe_w_down)
    row2 = lambda a: a[None, :]

    def ffn(x, acts, w_out, l, tm_mix, tm_moe, routed):
        x1, x1b, route = _mix(x, acts, w_out, row2(ln_mix_g[l]), row2(ln_mix_b[l]), rw, rb, tm_mix)
        ln = (row2(ln_ffn_g[l]), row2(ln_ffn_b[l]))
        if routed:
            return _moe_routed(x1, x1b, route, *experts, l, *ln, tm_mix)
        return _moe(x1, route, *experts, l, *ln, tm_moe)

    ps = _even_proj(xs, w_even, w_even_f, hg_lb_logits, fb_pad, RSM, 0)
    ps = dict(zip(("hq", "lf", "hk", "hv", "hgate", "fq", "fk", "fv", "fk16", "fv16", "flf"), ps))
    pm = _even_proj_cache(xm, w_even, w_even_f, hg_lb_logits, fb_pad, ps["fk"][ME], ps["fv"][ME], TM_MAIN, 0, B, T)
    pm = dict(zip(("hq", "lf", "hk", "hv", "hgate", "fq", "fk16", "fv16", "flf", "fk_cache", "fv_cache"), pm))

    hg_keys = ("hq", "lf", "hk", "hv", "hgate", "flf")
    meta_in = [_pad_rows(ps[n][ME], CHUNK) for n in hg_keys]
    zero_s = jnp.zeros((1, HG_HEADS, HG_DK, HG_DV), F32)
    zero_f = jnp.zeros((1, 1, LANES), F32)
    o_hg_meta, fc_meta, s_meta = _hgrn2(*meta_in, g_hg, e_mat, zero_s, zero_f, 1, CHUNK, 0, CHUNK)
    o_hg_meta, fc_meta = o_hg_meta[:N_META], fc_meta[:N_META]
    f_meta_end = fc_meta[N_META - 1:N_META][None]

    o_hg_m, fc_m, s_main = _hgrn2(*[pm[n] for n in hg_keys], g_hg, e_mat, s_meta, f_meta_end, B, T, 0, TILE_HGRN2)

    logf_c = jnp.pad(jnp.transpose(cache_fox_logf[0], (0, 2, 1)), ((0, 0), (0, HG_SUB - FOX_HEADS), (0, 0)))
    fpast = _cumsum_lanes(logf_c.reshape(Bs * HG_SUB, P), TILE_CUMSUM).reshape(Bs, HG_SUB, P)[:, :FOX_HEADS, :]
    f0_s = jnp.pad(fpast[:, :, P - 1][:, None, :], ((0, 0), (0, 0), (0, LANES - FOX_HEADS)))
    o_hg_s, fc_s, s_samp = _hgrn2(*[ps[n] for n in hg_keys], g_hg, e_mat, state_hgrn2[0], f0_s, Bs, Ts, 0, CHUNK)

    def bias_layouts(fc, n_seq, seq_len):
        f4 = fc[:, :FOX_HEADS].T
        return f4[:, :, None], f4.reshape(FOX_HEADS, n_seq, 1, seq_len)

    fq_m, fk_m = bias_layouts(fc_m, B, T)
    fq_m = jnp.broadcast_to(fq_m, fq_m.shape[:2] + (LANES,))
    fq_s, fk_s = bias_layouts(fc_s, Bs, Ts)
    fq_t, fk_t = bias_layouts(fc_meta, 1, N_META)

    fox_kw = dict(n_heads=FOX_HEADS, dq=FOX_DH, dk=FOX_DH, dv=FOX_DH, mask_mode="causal")
    meta_past = dict(k=ps["fk16"][ME][None], v=ps["fv16"][ME][None],
                     fk=jnp.transpose(fk_t, (1, 0, 2, 3)), tk=N_META)
    o_fox_m = _flash_tri(pm["fq"], pm["fk16"], pm["fv16"], n_seq=B, seq_len=T, tq=TQ,
                         fq=fq_m, fkn=fk_m, past=meta_past, **fox_kw)
    o_fox_s = _fox_cached(ps["fq"], fq_s, cache_fox_k[0], cache_fox_v[0], fpast[:, :, None, :],
                          ps["fk16"], ps["fv16"], fk_s, Bs, Ts, TILE_CACHE)
    o_fox_t = _flash(ps["fq"][ME], ps["fk16"][ME], ps["fv16"][ME], n_seq=1, seq_len=N_META,
                     fq=fq_t, fkn=fk_t, **fox_kw)

    o_hg_small = _pad_rows(jnp.concatenate([o_hg_s, o_hg_meta], axis=0), RSM)
    o_fox_small = _pad_rows(jnp.concatenate([o_fox_s, o_fox_t], axis=0), RSM)
    xm = ffn(xm, [o_hg_m, o_fox_m], w_out0, 0, TM_MOE, TM_MOE, True)
    xs = ffn(xs, [o_hg_small, o_fox_small], w_out0, 0, RSM, RSM, False)

    cos_m, sin_m = _rope_tables(N_META + jnp.arange(T, dtype=jnp.int32))
    pos_small = _pad_rows(jnp.concatenate([jnp.tile(P + jnp.arange(Ts, dtype=jnp.int32), Bs),
                                           jnp.arange(N_META, dtype=jnp.int32)]), RSM)
    cos_s, sin_s = _rope_tables(pos_small)
    qm, ckv_m, kpe_m, kpe16_m, kn_m, vn_m = _odd_proj(xm, cos_m, sin_m, w_odd, gq, gkv, wuq, w_ukv, TM_MOE)
    qs, ckv_s, kpe_s, kpe16_s, kn_s, vn_s = _odd_proj(xs, cos_s, sin_s, w_odd, gq, gkv, wuq, w_ukv, RSM)
    wuk_t = jnp.transpose(mla_w_uk[0], (1, 0, 2)).astype(BF16)
    wuv = mla_w_uv[0].reshape(MLA_KV_LORA, MLA_HEADS * MLA_V).astype(BF16)

    mla_kw = dict(n_heads=MLA_HEADS, dq=MLA_QPAD, dk=MLA_NOPE, dv=MLA_V)
    meta_past = dict(k=kn_s[ME][None], v=vn_s[ME][None], r=kpe16_s[ME][None], tk=N_META)
    o_m = _flash_tri(qm, kn_m, vn_m, n_seq=B, seq_len=T, tq=TQ, rn=kpe16_m,
                     past=meta_past, mask_mode="chunk", **mla_kw)
    assert P % CHUNK == 0 and Ts <= CHUNK
    o_s = _mla_absorbed(qs, cache_mla_ckv[0], cache_mla_kpe[0], ckv_s, kpe16_s, wuk_t, wuv, Bs, Ts, TILE_CACHE)
    o_t = _flash(qs[ME], kn_s[ME], vn_s[ME], n_seq=1, seq_len=N_META, rn=kpe16_s[ME], mask_mode="full", **mla_kw)
    xm = ffn(xm, [o_m], w_out1, 1, TM_MOE, TM_MOE, True)
    xs = ffn(xs, [_pad_rows(jnp.concatenate([o_s, o_t], axis=0), RSM)], w_out1, 1, RSM, RSM, False)

    def with_meta(main, small, *width):
        meta = jnp.broadcast_to(small[ME][None], (B, N_META) + width)
        return jnp.concatenate([meta, main.reshape((B, T) + width)], axis=1)

    y_prompt = xm.reshape(B, T, D_MODEL)
    y_sample = xs[:RS].reshape(Bs, Ts, D_MODEL)
    hg_p = s_main[None]
    fk_p = pm["fk_cache"][None]
    fv_p = pm["fv_cache"][None]
    flf_p = with_meta(pm["flf"][:, :FOX_HEADS], ps["flf"][:, :FOX_HEADS], FOX_HEADS)[None]
    ckv_p = with_meta(ckv_m, ckv_s, MLA_KV_LORA)[None]
    kpe_p = with_meta(kpe_m, kpe_s, MLA_ROPE)[None]
    hg_s = s_samp[None]
    fk_s_out = ps["fk"][:RS].reshape(1, Bs, Ts, FOX_HEADS, FOX_DH)
    fv_s_out = ps["fv"][:RS].reshape(1, Bs, Ts, FOX_HEADS, FOX_DH)
    flf_s = ps["flf"][:RS, :FOX_HEADS].reshape(1, Bs, Ts, FOX_HEADS)
    ckv_so = ckv_s[:RS].reshape(1, Bs, Ts, MLA_KV_LORA)
    kpe_so = kpe_s[:RS].reshape(1, Bs, Ts, MLA_ROPE)
    return (y_prompt, y_sample, hg_p, fk_p, fv_p, flf_p, ckv_p, kpe_p,
            hg_s, fk_s_out, fv_s_out, flf_s, ckv_so, kpe_so)
```

```python
import functools

import jax
import jax.numpy as jnp
from jax import lax
from jax.experimental import pallas as pl
from jax.experimental.pallas import tpu as pltpu
from jax.experimental.pallas import tpu_sc as plsc

D_MODEL = 1024
CHUNK = 64
N_META = 16
HG_HEADS = 4
HG_DK = 128
HG_DV = 128
HG_W = HG_HEADS * HG_DK
FOX_HEADS = 4
FOX_DH = 128
FOX_W = FOX_HEADS * FOX_DH
MLA_HEADS = 8
MLA_Q_LORA = 512
MLA_KV_LORA = 256
MLA_NOPE = 128
MLA_ROPE = 64
MLA_V = 128
MLA_QPAD = 256
ROPE_BASE = 10000.0
N_EXPERTS = 16
N_GROUPS = 4
EXPERTS_PER_GROUP = 4
D_EXPERT = 256
DEPTH = 2
ALPHA = (2 * DEPTH) ** 0.25
LN_EPS = 1e-5
RMS_EPS = 1e-6

LANES = 128
HG_SUB = 8
HG_GROUP = 4
NEG = -1e30
LOG2E = 1.4426950408889634
F32 = jnp.float32
BF16 = jnp.bfloat16
VMEM_LIMIT = 56 * 1024 * 1024
TILE_PROJ = 512
TILE_FFN = 1024
TILE_ATTN = 512
TILE_HGRN2 = 1024
TILE_CUMSUM = 512
TILE_CACHE = 1024


def _dot(a, b):
    return jnp.dot(a, b, preferred_element_type=F32)


def _dot_nt(a, b):
    return lax.dot_general(a, b, (((1,), (1,)), ((), ())), preferred_element_type=F32)


def _dot_tn(a, b):
    return lax.dot_general(a, b, (((0,), (0,)), ((), ())), preferred_element_type=F32)


def _split3(x):
    hi = x.astype(BF16)
    r = x - hi.astype(F32)
    mid = r.astype(BF16)
    lo = (r - mid.astype(F32)).astype(BF16)
    return hi, mid, lo


def _cumsum_rows(tri, x):
    hi, mid, lo = _split3(x)
    return _dot(tri, hi) + _dot(tri, mid) + _dot(tri, lo)


def _sigmoid(x):
    return 1.0 / (1.0 + jnp.exp(-x))


def _log_sigmoid(x):
    return jnp.minimum(x, 0.0) - jnp.log(1.0 + jnp.exp(-jnp.abs(x)))


def _layer_norm(x, g, b):
    mu = jnp.mean(x, axis=-1, keepdims=True)
    xc = x - mu
    var = jnp.mean(xc * xc, axis=-1, keepdims=True)
    return xc * lax.rsqrt(var + LN_EPS) * g + b


def _rms_norm(x, g):
    return x * lax.rsqrt(jnp.mean(x * x, axis=-1, keepdims=True) + RMS_EPS) * g


def _params(sem):
    return pltpu.CompilerParams(dimension_semantics=sem, vmem_limit_bytes=VMEM_LIMIT)


def _full_spec(a):
    nd = a.ndim
    return pl.BlockSpec(a.shape, lambda *_: (0,) * nd)


def _row_call(kernel, name, rows, tm, row_ins, full_ins, outs, scratch=()):
    assert rows % tm == 0
    in_specs = [pl.BlockSpec((tm, a.shape[1]), lambda i: (i, 0)) for a in row_ins]
    in_specs += [_full_spec(a) for a in full_ins]
    trail = [c if isinstance(c, tuple) else (c,) for c, _ in outs]
    out_specs = [pl.BlockSpec((tm,) + t, lambda i, n=len(t): (i,) + (0,) * n) for t in trail]
    out_shape = [jax.ShapeDtypeStruct((rows,) + t, dt) for t, (_, dt) in zip(trail, outs)]
    return pl.pallas_call(
        kernel, name=name, grid=(rows // tm,), in_specs=in_specs, out_specs=out_specs,
        out_shape=out_shape, scratch_shapes=list(scratch),
        compiler_params=_params(("parallel",)))(*row_ins, *full_ins)


def _even_proj_body(x_ref, w_ref, wf_ref, lbl_ref, fb_ref, hq_ref, lf_ref, hk_ref, hv_ref, hgate_ref,
                    fq_ref, fk16_ref, fv16_ref, flf_ref, layer):
    xb = x_ref[...].astype(BF16)

    def blk(j):
        return _dot(xb, w_ref[:, j * HG_W:(j + 1) * HG_W])

    logits = lbl_ref[...]
    e = jnp.exp(logits - jnp.max(logits, axis=0, keepdims=True))
    lb = jnp.sum(e[:layer + 1], axis=0, keepdims=True) / jnp.sum(e, axis=0, keepdims=True)

    hq_ref[...] = blk(0).astype(BF16)
    zf = blk(1)
    lf_ref[...] = jnp.log(lb + (1.0 - lb) * _sigmoid(zf))
    hk_ref[...] = ((1.0 - lb) * _sigmoid(-zf)).astype(BF16)
    hv_ref[...] = blk(2).astype(BF16)
    hgate_ref[...] = _sigmoid(blk(3)).astype(BF16)
    fq_ref[...] = (blk(4) * (FOX_DH ** -0.5 * LOG2E)).astype(BF16)
    fk = blk(5)
    fk16_ref[...] = fk.astype(BF16)
    fv = blk(6)
    fv16_ref[...] = fv.astype(BF16)
    flf_ref[...] = _log_sigmoid(_dot(xb, wf_ref[...]) + fb_ref[...])
    return fk, fv


def _even_proj_kernel(x_ref, w_ref, wf_ref, lbl_ref, fb_ref,
                      hq_ref, lf_ref, hk_ref, hv_ref, hgate_ref,
                      fq_ref, fk_ref, fv_ref, fk16_ref, fv16_ref, flf_ref, *, layer):
    fk, fv = _even_proj_body(x_ref, w_ref, wf_ref, lbl_ref, fb_ref, hq_ref, lf_ref, hk_ref, hv_ref, hgate_ref,
                             fq_ref, fk16_ref, fv16_ref, flf_ref, layer)
    for h in range(FOX_HEADS):
        fk_ref[:, h, :] = fk[:, h * FOX_DH:(h + 1) * FOX_DH]
        fv_ref[:, h, :] = fv[:, h * FOX_DH:(h + 1) * FOX_DH]


_EVEN_OUTS = [(HG_W, BF16), (HG_W, F32), (HG_W, BF16), (HG_W, BF16), (HG_W, BF16), (FOX_W, BF16)]


def _even_proj(x, w_main, w_f, lb_logits, fb_pad, tm, layer):
    rows = x.shape[0]
    outs = _EVEN_OUTS + [((FOX_HEADS, FOX_DH), F32), ((FOX_HEADS, FOX_DH), F32), (FOX_W, BF16), (FOX_W, BF16),
                         (LANES, F32)]
    return _row_call(functools.partial(_even_proj_kernel, layer=layer), "even_proj", rows, tm,
                     [x], [w_main, w_f, lb_logits, fb_pad], outs)


def _even_proj_cache_kernel(x_ref, w_ref, wf_ref, lbl_ref, fb_ref, mk_ref, mv_ref,
                            hq_ref, lf_ref, hk_ref, hv_ref, hgate_ref, fq_ref, fk16_ref, fv16_ref, flf_ref,
                            fk_hbm, fv_hbm, kbuf, vbuf, sem, msem, *, layer, tiles_per_seq):
    i, n = pl.program_id(0), pl.num_programs(0)
    tm = x_ref.shape[0]

    def row_copies(step):
        b = step // tiles_per_seq
        t0 = N_META + (step % tiles_per_seq) * tm
        return [pltpu.make_async_copy(buf.at[:, pl.ds(h * FOX_DH, FOX_DH)], hbm.at[b, pl.ds(t0, tm), h, :], sem.at[a, h])
                for a, (buf, hbm) in enumerate(((kbuf, fk_hbm), (vbuf, fv_hbm))) for h in range(FOX_HEADS)]

    def meta_copies(step):
        b = step // tiles_per_seq
        return [pltpu.make_async_copy(mk_ref, fk_hbm.at[b, pl.ds(0, N_META)], msem.at[0]),
                pltpu.make_async_copy(mv_ref, fv_hbm.at[b, pl.ds(0, N_META)], msem.at[1])]

    fk, fv = _even_proj_body(x_ref, w_ref, wf_ref, lbl_ref, fb_ref, hq_ref, lf_ref, hk_ref, hv_ref, hgate_ref,
                             fq_ref, fk16_ref, fv16_ref, flf_ref, layer)

    @pl.when(i > 0)
    def _():
        for c in row_copies(i - 1):
            c.wait()

    @pl.when((i > 0) & ((i - 1) % tiles_per_seq == 0))
    def _():
        for c in meta_copies(i - 1):
            c.wait()

    kbuf[...] = fk
    vbuf[...] = fv
    for c in row_copies(i):
        c.start()

    @pl.when(i % tiles_per_seq == 0)
    def _():
        for c in meta_copies(i):
            c.start()

    @pl.when(i == n - 1)
    def _():
        for c in row_copies(i):
            c.wait()

    @pl.when((i == n - 1) & (i % tiles_per_seq == 0))
    def _():
        for c in meta_copies(i):
            c.wait()


def _even_proj_cache(x, w_main, w_f, lb_logits, fb_pad, meta_k, meta_v, tm, layer, n_seq, seq_len):
    rows = x.shape[0]
    assert rows == n_seq * seq_len and seq_len % tm == 0
    outs = _EVEN_OUTS + [(FOX_W, BF16), (FOX_W, BF16), (LANES, F32)]
    full = [w_main, w_f, lb_logits, fb_pad, meta_k, meta_v]
    cache = jax.ShapeDtypeStruct((n_seq, N_META + seq_len, FOX_HEADS, FOX_DH), F32)
    return pl.pallas_call(
        functools.partial(_even_proj_cache_kernel, layer=layer, tiles_per_seq=seq_len // tm),
        name="even_proj_cache", grid=(rows // tm,),
        in_specs=[pl.BlockSpec((tm, D_MODEL), lambda i: (i, 0))] + [_full_spec(a) for a in full],
        out_specs=[pl.BlockSpec((tm, c), lambda i: (i, 0)) for c, _ in outs] + [pl.BlockSpec(memory_space=pl.ANY)] * 2,
        out_shape=[jax.ShapeDtypeStruct((rows, c), dt) for c, dt in outs] + [cache, cache],
        scratch_shapes=[pltpu.VMEM((tm, FOX_W), F32), pltpu.VMEM((tm, FOX_W), F32),
                        pltpu.SemaphoreType.DMA((2, FOX_HEADS)), pltpu.SemaphoreType.DMA((2,))],
        compiler_params=_params(("arbitrary",)))(x, *full)


def _bcast_sub(x, j):
    n, c = x.shape
    x3 = x.reshape(n // HG_SUB, HG_SUB, c)
    return jnp.broadcast_to(x3[:, j:j + 1, :], x3.shape).reshape(n, c)


def _level_ref(b, w):
    n, c = b.shape
    parts = [jnp.broadcast_to(b[m * 2 * w + w - 1:m * 2 * w + w, :], (2 * w, c)) for m in range(n // (2 * w))]
    return parts[0] if len(parts) == 1 else jnp.concatenate(parts, axis=0)


def _hgrn2_kernel(q_ref, lf_ref, k_ref, v_ref, gate_ref, flf_ref, g_ref, e_ref, s0_ref, f0_ref,
                  o_ref, fcum_ref, sout_ref, st_scr, fc_scr, *, n_chunks):
    i = pl.program_id(1)
    C = CHUNK

    @pl.when(i == 0)
    def _():
        for h in range(HG_HEADS):
            st_scr[h] = s0_ref[h].T
        fc_scr[...] = f0_ref[...]

    row = lax.broadcasted_iota(jnp.int32, (C, 1), 0)
    col = lax.broadcasted_iota(jnp.int32, (1, C), 1)
    tri = (col <= row).astype(BF16)
    same = lambda w: (row // w) == (col // w)
    levels = (32, 16, 8)

    fc = fc_scr[...]
    for c in range(n_chunks):
        sl = slice(c * C, (c + 1) * C)
        fcum = _cumsum_rows(tri, flf_ref[sl, :]) + fc
        fcum_ref[sl, :] = fcum
        fc = fcum[C - 1:C, :]
    fc_scr[...] = fc

    staged = []
    for c in range(n_chunks):
        sl = slice(c * C, (c + 1) * C)
        per_head = []
        for h0 in range(0, HG_HEADS, HG_GROUP):
            gs = slice(h0 * HG_DK, (h0 + HG_GROUP) * HG_DK)
            b = _cumsum_rows(tri, lf_ref[sl, gs]) * LOG2E
            q = q_ref[sl, gs].astype(F32)
            k = k_ref[sl, gs].astype(F32)
            v = v_ref[sl, gs]
            qb = (q * jnp.exp2(b)).astype(BF16)
            b_last = b[C - 1:C, :]
            kd = (k * jnp.exp2(b_last - b)).astype(BF16)
            e_last = jnp.exp2(b_last)

            pjs = [(jnp.exp2(jnp.where((row % HG_SUB) >= j, b - _bcast_sub(b, j), NEG)) * q
                    * _bcast_sub(k, j)).astype(BF16) for j in range(HG_SUB)]
            lv = []
            for w in levels:
                upper = (row % (2 * w)) >= w
                ew = jnp.exp2(-jnp.abs(b - _level_ref(b, w)))
                lv.append((jnp.where(upper, q * ew, 0.0).astype(BF16), jnp.where(upper, 0.0, k * ew).astype(BF16)))

            for hh in range(HG_GROUP):
                hs = slice(hh * HG_DK, (hh + 1) * HG_DK)
                a = jnp.where(same(HG_SUB), _dot(jnp.concatenate([p[:, hs] for p in pjs], axis=1), e_ref[...]), 0.0)
                for w, (qw, kw) in zip(levels, lv):
                    aw = _dot_nt(qw[:, hs], kw[:, hs])
                    a = a + (aw if 2 * w == C else jnp.where(same(2 * w), aw, 0.0))
                vh = v[:, hs]
                per_head.append((_dot(a.astype(BF16), vh), qb[:, hs], kd[:, hs], e_last[:, hs], vh))
        staged.append(per_head)

    st = [st_scr[h] for h in range(HG_HEADS)]
    for c in range(n_chunks):
        sl = slice(c * C, (c + 1) * C)
        for h, (o_intra, qb_h, kd_h, e_h, vh) in enumerate(staged[c]):
            ho = slice(h * HG_DK, (h + 1) * HG_DK)
            o = o_intra + _dot_nt(qb_h, st[h].astype(BF16))
            st[h] = st[h] * e_h + _dot_tn(vh, kd_h)
            o = _rms_norm(o, g_ref[:, ho])
            o_ref[sl, ho] = (o * gate_ref[sl, ho].astype(F32)).astype(BF16)
    for h in range(HG_HEADS):
        st_scr[h] = st[h]

    @pl.when(i == pl.num_programs(1) - 1)
    def _():
        for h in range(HG_HEADS):
            sout_ref[h] = st_scr[h].T


def _hgrn2(q, lf, k, v, gate, flf, g, e_mat, s0, f0, n_seq, seq_len, row_off, tb):
    assert seq_len % tb == 0 and tb % CHUNK == 0 and row_off % tb == 0
    nb = seq_len // tb
    off = row_off // tb
    per_seq = s0.shape[0] > 1
    rmap = lambda s, i: (off + s * nb + i, 0)
    omap = lambda s, i: (s * nb + i, 0)
    smap = (lambda s, i: (s, 0, 0, 0)) if per_seq else (lambda s, i: (0, 0, 0, 0))
    fmap = (lambda s, i: (s, 0, 0)) if per_seq else (lambda s, i: (0, 0, 0))
    in_specs = [pl.BlockSpec((tb, HG_W), rmap) for _ in range(5)]
    in_specs += [pl.BlockSpec((tb, LANES), rmap), _full_spec(g), _full_spec(e_mat),
                 pl.BlockSpec((None, HG_HEADS, HG_DK, HG_DV), smap), pl.BlockSpec((None, 1, LANES), fmap)]
    out_specs = [pl.BlockSpec((tb, HG_W), omap), pl.BlockSpec((tb, LANES), omap),
                 pl.BlockSpec((None, HG_HEADS, HG_DK, HG_DV), lambda s, i: (s, 0, 0, 0))]
    out_shape = [jax.ShapeDtypeStruct((n_seq * seq_len, HG_W), BF16),
                 jax.ShapeDtypeStruct((n_seq * seq_len, LANES), F32),
                 jax.ShapeDtypeStruct((n_seq, HG_HEADS, HG_DK, HG_DV), F32)]
    scratch = [pltpu.VMEM((HG_HEADS, HG_DV, HG_DK), F32), pltpu.VMEM((1, LANES), F32)]
    return pl.pallas_call(
        functools.partial(_hgrn2_kernel, n_chunks=tb // CHUNK), name="hgrn2",
        grid=(n_seq, nb), in_specs=in_specs, out_specs=out_specs, out_shape=out_shape,
        scratch_shapes=scratch, compiler_params=_params(("parallel", "arbitrary")))(
            q, lf, k, v, gate, flf, g, e_mat, s0, f0)


def _cumsum_kernel(x_ref, tri_ref, o_ref, carry):
    @pl.when(pl.program_id(0) == 0)
    def _():
        carry[...] = jnp.zeros_like(carry)

    hi, mid, lo = _split3(x_ref[...])
    tri = tri_ref[...]
    out = _dot(hi, tri) + _dot(mid, tri) + _dot(lo, tri) + carry[...]
    o_ref[...] = out
    carry[...] = out[:, out.shape[1] - 1:]


def _cumsum_lanes(x, tb):
    r, seq_len = x.shape
    tri = (jnp.arange(tb)[:, None] <= jnp.arange(tb)[None, :]).astype(BF16)
    return pl.pallas_call(
        _cumsum_kernel, name="cumsum", grid=(seq_len // tb,),
        in_specs=[pl.BlockSpec((r, tb), lambda i: (0, i)), _full_spec(tri)],
        out_specs=pl.BlockSpec((r, tb), lambda i: (0, i)),
        out_shape=jax.ShapeDtypeStruct(x.shape, F32),
        scratch_shapes=[pltpu.VMEM((r, 1), F32)],
        compiler_params=_params(("arbitrary",)))(x, tri)


def _flash_kernel(*refs, tq, has_bias, has_rope, mask_mode):
    it = iter(refs)
    q_ref = next(it)
    fq_ref = next(it) if has_bias else None
    kn_ref, vn_ref = next(it), next(it)
    rn_ref = next(it) if has_rope else None
    fkn_ref = next(it) if has_bias else None
    o_ref = next(it)
    m_scr, acc_scr = next(it), next(it)
    dv = o_ref.shape[1]

    q = q_ref[...]
    m_scr[...] = jnp.full(m_scr.shape, NEG, F32)
    acc_scr[...] = jnp.zeros(acc_scr.shape, F32)
    fq_b = jnp.broadcast_to(fq_ref[...] * LOG2E, (tq, LANES)) if has_bias else None

    def scores(k, r, fk):
        if has_rope:
            k = jnp.concatenate([k, r], axis=1)
        s = _dot_nt(q, k.astype(BF16))
        if has_bias:
            s = s + jnp.tile(fq_b, (1, s.shape[1] // LANES)) if s.shape[1] % LANES == 0 else s + fq_b[:, :1]
            s = s - fk * LOG2E
        return s

    def update(s, v, mask):
        if mask is not None:
            s = jnp.where(mask, s, NEG)
        m_prev = m_scr[...]
        m_new = jnp.maximum(m_prev, jnp.max(s, axis=1, keepdims=True))
        alpha = jnp.exp2(m_prev - m_new)
        if s.shape[1] % LANES == 0:
            p = jnp.exp2(s - jnp.tile(m_new, (1, s.shape[1] // LANES)))
        else:
            p = jnp.exp2(s - m_new[:, :1])
        v1 = jnp.concatenate([v.astype(BF16), jnp.ones((v.shape[0], LANES), BF16)], axis=1)
        acc_scr[...] = jnp.tile(alpha, (1, acc_scr.shape[1] // LANES)) * acc_scr[...] + _dot(p.astype(BF16), v1)
        m_scr[...] = m_new

    row = lax.broadcasted_iota(jnp.int32, (tq, 1), 0)
    col = lax.broadcasted_iota(jnp.int32, (1, tq), 1)
    if mask_mode == "causal":
        mask = col <= row
    elif mask_mode == "chunk":
        mask = (col // CHUNK) <= (row // CHUNK)
    else:
        mask = None

    update(scores(kn_ref[...], rn_ref[...] if has_rope else None, fkn_ref[...] if has_bias else None),
           vn_ref[...], mask)
    acc = acc_scr[...]
    o_ref[...] = (acc[:, :dv] / acc[:, dv:]).astype(o_ref.dtype)


def _flash(q, kn, vn, *, n_seq, n_heads, seq_len, dq, dk, dv, mask_mode, fq=None, fkn=None, rn=None):
    tq = seq_len
    has_bias = fq is not None
    has_rope = rn is not None
    ins, specs = [q], [pl.BlockSpec((tq, dq), lambda b, h: (b, h))]
    if has_bias:
        ins.append(fq)
        specs.append(pl.BlockSpec((None, tq, 1), lambda b, h: (h, b, 0)))
    ins += [kn, vn]
    specs += [pl.BlockSpec((seq_len, dk), lambda b, h: (b, h)), pl.BlockSpec((seq_len, dv), lambda b, h: (b, h))]
    if has_rope:
        ins.append(rn)
        specs.append(pl.BlockSpec((seq_len, LANES), lambda b, h: (b, 0)))
    if has_bias:
        ins.append(fkn)
        specs.append(pl.BlockSpec((None, None, 1, seq_len), lambda b, h: (h, b, 0, 0)))
    kern = functools.partial(_flash_kernel, tq=tq, has_bias=has_bias, has_rope=has_rope, mask_mode=mask_mode)
    return pl.pallas_call(
        kern, name="flash", grid=(n_seq, n_heads), in_specs=specs,
        out_specs=pl.BlockSpec((tq, dv), lambda b, h: (b, h)),
        out_shape=jax.ShapeDtypeStruct((n_seq * seq_len, n_heads * dv), BF16),
        scratch_shapes=[pltpu.VMEM((tq, LANES), F32), pltpu.VMEM((tq, dv + LANES), F32)],
        compiler_params=_params(("parallel", "parallel")))(*ins)


def _fox_cached_kernel(q_ref, fq_ref, kp_hbm, vp_hbm, fkp_ref, kn_ref, vn_ref, fkn_ref, o_ref,
                       kbuf, vbuf, ksem, vsem, *, tkp, n_blk):
    b = pl.program_id(0)
    tq = q_ref.shape[0]
    row = lax.broadcasted_iota(jnp.int32, (tq, 1), 0)
    col = lax.broadcasted_iota(jnp.int32, (1, tq), 1)
    ones_p = jnp.ones((tkp, LANES), BF16)
    steps = [(h, j) for h in range(FOX_HEADS) for j in range(n_blk)]

    def copies(i):
        h, j = steps[i]
        slot = i % 2
        src = lambda ref: ref.at[b, pl.ds(j * tkp, tkp), h, :]
        return (pltpu.make_async_copy(src(kp_hbm), kbuf.at[slot], ksem.at[slot]),
                pltpu.make_async_copy(src(vp_hbm), vbuf.at[slot], vsem.at[slot]))

    def update(state, s, v1):
        m_prev, acc = state
        m_new = jnp.maximum(m_prev, jnp.max(s, axis=1, keepdims=True))
        p = jnp.exp2(s - m_new)
        return m_new, jnp.exp2(m_prev - m_new) * acc + _dot(p.astype(BF16), v1)

    for c in copies(0):
        c.start()
    state = None
    for i, (h, j) in enumerate(steps):
        hs = slice(h * FOX_DH, (h + 1) * FOX_DH)
        q = q_ref[:, hs]
        fq = fq_ref[h] * LOG2E
        if j == 0:
            state = (jnp.full((tq, 1), NEG, F32), jnp.zeros((tq, FOX_DH + LANES), F32))
        if i + 1 < len(steps):
            for c in copies(i + 1):
                c.start()
        for c in copies(i):
            c.wait()
        slot = i % 2
        s = _dot_nt(q, kbuf[slot].astype(BF16)) + fq - fkp_ref[h, :, j * tkp:(j + 1) * tkp] * LOG2E
        state = update(state, s, jnp.concatenate([vbuf[slot].astype(BF16), ones_p], axis=1))
        if j == n_blk - 1:
            s = _dot_nt(q, kn_ref[:, hs]) + fq - fkn_ref[h] * LOG2E
            s = jnp.where(col <= row, s, NEG)
            _, acc = update(state, s, jnp.concatenate([vn_ref[:, hs], ones_p[:tq]], axis=1))
            o_ref[:, hs] = (acc[:, :FOX_DH] / acc[:, FOX_DH:FOX_DH + 1]).astype(o_ref.dtype)


def _fox_cached(q, fq, kp, vp, fkp, kn, vn, fkn, n_seq, tq, tkp):
    p = kp.shape[1]
    assert p % tkp == 0
    return pl.pallas_call(
        functools.partial(_fox_cached_kernel, tkp=tkp, n_blk=p // tkp), name="fox_cached", grid=(n_seq,),
        in_specs=[pl.BlockSpec((tq, FOX_W), lambda b: (b, 0)),
                  pl.BlockSpec((FOX_HEADS, tq, 1), lambda b: (0, b, 0)),
                  pl.BlockSpec(memory_space=pl.ANY),
                  pl.BlockSpec(memory_space=pl.ANY),
                  pl.BlockSpec((None, FOX_HEADS, 1, p), lambda b: (b, 0, 0, 0)),
                  pl.BlockSpec((tq, FOX_W), lambda b: (b, 0)),
                  pl.BlockSpec((tq, FOX_W), lambda b: (b, 0)),
                  pl.BlockSpec((FOX_HEADS, None, 1, tq), lambda b: (0, b, 0, 0))],
        out_specs=pl.BlockSpec((tq, FOX_W), lambda b: (b, 0)),
        out_shape=jax.ShapeDtypeStruct((n_seq * tq, FOX_W), BF16),
        scratch_shapes=[pltpu.VMEM((2, tkp, FOX_DH), F32), pltpu.VMEM((2, tkp, FOX_DH), F32),
                        pltpu.SemaphoreType.DMA((2,)), pltpu.SemaphoreType.DMA((2,))],
        compiler_params=_params(("arbitrary",)))(q, fq, kp, vp, fkp, kn, vn, fkn)


FLASH_UNROLL_OFF = 14
FLASH_UNROLL_DIAG = 8


def _tri_tables(nq):
    pairs = [(qi, kj) for qi in range(nq) for kj in range(qi)] + [(qi, qi) for qi in range(nq)] + [(0, 0)]
    return (jnp.array([p[0] for p in pairs], jnp.int32), jnp.array([p[1] for p in pairs], jnp.int32))


def _flash_tri_kernel(qt_ref, kt_ref, *refs, tq, nq, has_bias, has_rope, mask_mode):
    it = iter(refs)
    q_ref = next(it)
    fq_ref = next(it) if has_bias else None
    kp_ref, vp_ref = next(it), next(it)
    rp_ref = next(it) if has_rope else None
    fkp_ref = next(it) if has_bias else None
    kn_ref, vn_ref = next(it), next(it)
    rn_ref = next(it) if has_rope else None
    fkn_ref = next(it) if has_bias else None
    o_ref = next(it)
    m_scr, acc_scr, sa_scr, sb_scr = next(it), next(it), next(it), next(it)
    fqb_scr = next(it) if has_bias else None
    dv = o_ref.shape[1]
    n_off = nq * (nq - 1) // 2
    tile = lambda j: pl.ds(pl.multiple_of(j * tq, tq), tq)
    ones = jnp.ones((tq, LANES), BF16)

    kp = kp_ref[...]
    if has_rope:
        kp = jnp.concatenate([kp, rp_ref[...]], axis=1)
    vp1 = jnp.concatenate([vp_ref[...], ones[:vp_ref.shape[0]]], axis=1)
    for i in range(nq):
        rs = slice(i * tq, (i + 1) * tq)
        s = _dot_nt(q_ref[rs, :], kp)
        if has_bias:
            fb = fq_ref[rs, :] * LOG2E
            fqb_scr[rs, :] = fb
            s = s + fb[:, :s.shape[1]] - fkp_ref[...] * LOG2E
        m0 = jnp.max(s, axis=1, keepdims=True)
        m_scr[i] = jnp.broadcast_to(m0, (tq, LANES))
        acc_scr[i] = _dot(jnp.exp2(s - m0).astype(BF16), vp1)

    def fill(s_ref, t):
        qs, ks = tile(qt_ref[t]), tile(kt_ref[t])
        k = kn_ref[ks, :]
        if has_rope:
            k = jnp.concatenate([k, rn_ref[ks, :]], axis=1)
        s = _dot_nt(q_ref[qs, :], k)
        if has_bias:
            s = s + jnp.tile(fqb_scr[qs, :], (1, tq // LANES)) - fkn_ref[:, ks] * LOG2E
        s_ref[...] = s

    def drain(s_ref, t, mask):
        qi = qt_ref[t]
        s = s_ref[...]
        if mask is not None:
            s = jnp.where(mask, s, NEG)
        m_prev = m_scr[qi]
        m_new = jnp.maximum(m_prev, jnp.max(s, axis=1, keepdims=True))
        p = jnp.exp2(s - jnp.tile(m_new, (1, tq // LANES)))
        v1 = jnp.concatenate([vn_ref[tile(kt_ref[t]), :], ones], axis=1)
        acc = jnp.tile(jnp.exp2(m_prev - m_new), (1, (dv + LANES) // LANES)) * acc_scr[qi] + _dot(p.astype(BF16), v1)
        return qi, m_new, acc

    def keep(s_ref, t):
        qi, m_new, acc = drain(s_ref, t, None)
        m_scr[qi] = m_new
        acc_scr[qi] = acc

    row = lax.broadcasted_iota(jnp.int32, (tq, 1), 0)
    col = lax.broadcasted_iota(jnp.int32, (1, tq), 1)
    mask = {"causal": col <= row, "chunk": (col // CHUNK) <= (row // CHUNK)}[mask_mode]

    def finish(s_ref, t):
        qi, _, acc = drain(s_ref, t, mask)
        o_ref[tile(qi), :] = (acc[:, :dv] / acc[:, dv:]).astype(o_ref.dtype)

    def pipeline(t0, n, unroll, consume):
        assert n % unroll == 0 and unroll % 2 == 0

        def body(i, carry):
            t = t0 + unroll * i
            for u in range(0, unroll, 2):
                fill(sb_scr, t + u + 1)
                consume(sa_scr, t + u)
                fill(sa_scr, t + u + 2)
                consume(sb_scr, t + u + 1)
            return carry
        lax.fori_loop(0, n // unroll, body, 0)

    fill(sa_scr, 0)
    pipeline(0, n_off, FLASH_UNROLL_OFF, keep)
    pipeline(n_off, nq, FLASH_UNROLL_DIAG, finish)


def _flash_tri(q, kn, vn, *, n_seq, n_heads, seq_len, tq, dq, dk, dv, mask_mode, past, fq=None, fkn=None, rn=None):
    nq = seq_len // tq
    has_bias = fq is not None
    has_rope = rn is not None
    tp = past["k"].shape[1]
    m3 = lambda f: (lambda b, h, qt, kt: f(b, h))
    ins, specs = [q], [pl.BlockSpec((seq_len, dq), m3(lambda b, h: (b, h)))]
    if has_bias:
        ins.append(fq)
        specs.append(pl.BlockSpec((None, seq_len, LANES), m3(lambda b, h: (h, b, 0))))
    ins += [past["k"], past["v"]]
    specs += [pl.BlockSpec((None, tp, dk), m3(lambda b, h: (0, 0, h))),
              pl.BlockSpec((None, tp, dv), m3(lambda b, h: (0, 0, h)))]
    if has_rope:
        ins.append(past["r"])
        specs.append(pl.BlockSpec((None, tp, LANES), m3(lambda b, h: (0, 0, 0))))
    if has_bias:
        ins.append(past["fk"])
        specs.append(pl.BlockSpec((None, None, 1, tp), m3(lambda b, h: (0, h, 0, 0))))
    ins += [kn, vn]
    specs += [pl.BlockSpec((seq_len, dk), m3(lambda b, h: (b, h))),
              pl.BlockSpec((seq_len, dv), m3(lambda b, h: (b, h)))]
    if has_rope:
        ins.append(rn)
        specs.append(pl.BlockSpec((seq_len, LANES), m3(lambda b, h: (b, 0))))
    if has_bias:
        ins.append(fkn)
        specs.append(pl.BlockSpec((None, None, 1, seq_len), m3(lambda b, h: (h, b, 0, 0))))
    scratch = [pltpu.VMEM((nq, tq, LANES), F32), pltpu.VMEM((nq, tq, dv + LANES), F32),
               pltpu.VMEM((tq, tq), F32), pltpu.VMEM((tq, tq), F32)]
    if has_bias:
        scratch.append(pltpu.VMEM((seq_len, LANES), F32))
    grid_spec = pltpu.PrefetchScalarGridSpec(
        num_scalar_prefetch=2, grid=(n_seq, n_heads), in_specs=specs,
        out_specs=pl.BlockSpec((seq_len, dv), m3(lambda b, h: (b, h))), scratch_shapes=scratch)
    kern = functools.partial(_flash_tri_kernel, tq=tq, nq=nq, has_bias=has_bias, has_rope=has_rope,
                             mask_mode=mask_mode)
    return pl.pallas_call(
        kern, name="flash_tri", grid_spec=grid_spec,
        out_shape=jax.ShapeDtypeStruct((n_seq * seq_len, n_heads * dv), BF16),
        compiler_params=_params(("parallel", "arbitrary")))(*_tri_tables(nq), *ins)


def _route(sc, sb):
    def top2_sum(v):
        a, b, c, d = v
        a, b = jnp.maximum(a, b), jnp.minimum(a, b)
        c, d = jnp.maximum(c, d), jnp.minimum(c, d)
        hi, lo2 = jnp.maximum(a, c), jnp.minimum(a, c)
        return hi + jnp.maximum(lo2, jnp.maximum(b, d))

    gs = [top2_sum(sb[g * EXPERTS_PER_GROUP:(g + 1) * EXPERTS_PER_GROUP]) for g in range(N_GROUPS)]
    best_v, best_g = gs[0], jnp.zeros(gs[0].shape, jnp.int32)
    for g in range(1, N_GROUPS):
        upd = gs[g] > best_v
        best_v = jnp.where(upd, gs[g], best_v)
        best_g = jnp.where(upd, g, best_g)
    masked = [jnp.where(best_g == (e // EXPERTS_PER_GROUP), sb[e], -jnp.inf) for e in range(N_EXPERTS)]

    def argmax_first(vals, exclude=None):
        bv = jnp.full(vals[0].shape, -jnp.inf, F32)
        bi = jnp.full(vals[0].shape, -1, jnp.int32)
        for e, v in enumerate(vals):
            upd = v > bv
            if exclude is not None:
                upd = upd & (exclude != e)
            bv = jnp.where(upd, v, bv)
            bi = jnp.where(upd, e, bi)
        return bi

    i1 = argmax_first(masked)
    i2 = argmax_first(masked, exclude=i1)
    w1 = sum(jnp.where(i1 == e, sc[e], 0.0) for e in range(N_EXPERTS))
    w2 = sum(jnp.where(i2 == e, sc[e], 0.0) for e in range(N_EXPERTS))
    tot = w1 + w2
    w1, w2 = w1 / tot, w2 / tot
    comb = [jnp.where(i1 == e, w1, 0.0) + jnp.where(i2 == e, w2, 0.0) for e in range(N_EXPERTS)]
    return comb + [i1.astype(F32), i2.astype(F32), w1, w2]


def _mix_kernel(*refs, n_act):
    x_ref = refs[0]
    a_refs = refs[1:1 + n_act]
    w_ref, g_ref, b_ref, rw_ref, rb_ref, x1_ref, x1p_ref, comb_ref, ct_scr = refs[1 + n_act:]
    half = D_MODEL // 2
    tm = x_ref.shape[0]
    group = MIX_GROUP if tm % MIX_GROUP == 0 else tm
    ct_scr[...] = jnp.zeros(ct_scr.shape, F32)
    for r0 in range(0, tm, group):
        rs = slice(r0, r0 + group)
        ys = []
        for n0 in (0, half):
            y = None
            k0 = 0
            for a_ref in a_refs:
                kw = a_ref.shape[1]
                part = _dot(a_ref[rs, :], w_ref[k0:k0 + kw, n0:n0 + half])
                y = part if y is None else y + part
                k0 += kw
            ys.append(y)
        x1 = _layer_norm(ALPHA * x_ref[rs, :] + jnp.concatenate(ys, axis=1), g_ref[...], b_ref[...])
        x1_ref[rs, :] = x1

        x1p_ref[rs, :] = _pack_pair(x1[:, :half], x1[:, half:])
        logits = _dot(x1.astype(BF16), rw_ref[...])
        scores_t = _sigmoid(logits).T
        sc = [scores_t[e:e + 1, :] for e in range(N_EXPERTS)]
        sb = [sc[e] + rb_ref[e:e + 1, :] for e in range(N_EXPERTS)]
        for r, val in enumerate(_route(sc, sb)):
            ct_scr[r:r + 1, rs] = val
        comb_ref[rs, :] = ct_scr[:, rs].T


def _mix(x, acts, w_out, ln_g, ln_b, rw, rb, tm):
    rows = x.shape[0]
    return _row_call(functools.partial(_mix_kernel, n_act=len(acts)), "mix", rows, tm,
                     [x] + list(acts), [w_out, ln_g, ln_b, rw, rb],
                     [(D_MODEL, F32), (D_MODEL // 2, jnp.uint32), (LANES, F32)],
                     scratch=[pltpu.VMEM((LANES, tm), F32)])


def _moe_kernel(x_ref, comb_ref, wg_ref, wu_ref, wd_ref, g_ref, b_ref, o_ref, xb_scr, acc_scr):
    e = pl.program_id(1)

    @pl.when(e == 0)
    def _():
        xb_scr[...] = x_ref[...].astype(BF16)
        acc_scr[...] = jnp.zeros(acc_scr.shape, F32)

    xb = xb_scr[...]
    lane = lax.broadcasted_iota(jnp.int32, (1, LANES), 1)
    c_e = jnp.sum(jnp.where(lane == e, comb_ref[...], 0.0), axis=1, keepdims=True)
    gate = _dot(xb, wg_ref[...].astype(BF16))
    h = gate * _sigmoid(gate) * _dot(xb, wu_ref[...].astype(BF16))
    acc_scr[...] += _dot((h * c_e).astype(BF16), wd_ref[...].astype(BF16))

    @pl.when(e == N_EXPERTS - 1)
    def _():
        o_ref[...] = _layer_norm(ALPHA * x_ref[...] + acc_scr[...], g_ref[...], b_ref[...])


def _moe(x, comb, wg, wu, wd, layer, ln_g, ln_b, tm):
    rows = x.shape[0]
    assert rows % tm == 0
    return pl.pallas_call(
        _moe_kernel, name="moe", grid=(rows // tm, N_EXPERTS),
        in_specs=[pl.BlockSpec((tm, D_MODEL), lambda i, e: (i, 0)),
                  pl.BlockSpec((tm, LANES), lambda i, e: (i, 0)),
                  pl.BlockSpec((None, None, D_MODEL, D_EXPERT), lambda i, e: (layer, e, 0, 0)),
                  pl.BlockSpec((None, None, D_MODEL, D_EXPERT), lambda i, e: (layer, e, 0, 0)),
                  pl.BlockSpec((None, None, D_EXPERT, D_MODEL), lambda i, e: (layer, e, 0, 0)),
                  _full_spec(ln_g), _full_spec(ln_b)],
        out_specs=pl.BlockSpec((tm, D_MODEL), lambda i, e: (i, 0)),
        out_shape=jax.ShapeDtypeStruct((rows, D_MODEL), F32),
        scratch_shapes=[pltpu.VMEM((tm, D_MODEL), BF16), pltpu.VMEM((tm, D_MODEL), F32)],
        compiler_params=_params(("parallel", "arbitrary")))(x, comb, wg, wu, wd, ln_g, ln_b)


ROUTE_E1, ROUTE_E2, ROUTE_W1, ROUTE_W2 = N_EXPERTS, N_EXPERTS + 1, N_EXPERTS + 2, N_EXPERTS + 3
TE = 1024
SC_WINDOW = 128
RANK_TILE = 1024
MIX_GROUP = 256


def _pack_pair(a, b):
    au = lax.bitcast_convert_type(a.astype(BF16).astype(F32), jnp.uint32)
    bu = lax.bitcast_convert_type(b.astype(BF16).astype(F32), jnp.uint32)
    return (au >> 16) | (bu & jnp.uint32(0xFFFF0000))


def _unpack_pair(w):
    a = lax.bitcast_convert_type(w << 16, F32)
    b = lax.bitcast_convert_type(w & jnp.uint32(0xFFFF0000), F32)
    return a, b


def _rank_kernel(route_ref, pos_ref, texp_ref, nused_ref, cnt_scr, carry_scr, seg_scr, before_scr):
    ph, i = pl.program_id(0), pl.program_id(1)
    T = route_ref.shape[0]
    lane = lax.broadcasted_iota(jnp.int32, (1, LANES), 1)
    lane_f = lane.astype(F32)
    r = route_ref[...]
    e1, e2 = r[:, ROUTE_E1:ROUTE_E1 + 1], r[:, ROUTE_E2:ROUTE_E2 + 1]
    m1, m2 = lane_f == e1, lane_f == e2
    m = jnp.where(m1 | m2, 1.0, 0.0)
    colsum = jnp.sum(m, axis=0, keepdims=True)

    @pl.when((ph == 0) & (i == 0))
    def _():
        cnt_scr[...] = jnp.zeros(cnt_scr.shape, F32)

    @pl.when(ph == 0)
    def _():
        cnt_scr[...] += colsum

    @pl.when((ph == 1) & (i == 0))
    def _():
        cnt = cnt_scr[...].astype(jnp.int32)
        padded = (((cnt + (TE - 1)) // TE) * TE).astype(F32)
        rr = lax.broadcasted_iota(jnp.int32, (LANES, 1), 0)
        upper = (rr < lane).astype(BF16)
        hi, mid, lo = _split3(jnp.broadcast_to(padded, (HG_SUB, LANES)))
        seg = (_dot(hi, upper) + _dot(mid, upper) + _dot(lo, upper))[:1, :]
        seg_scr[...] = seg
        carry_scr[...] = jnp.zeros(carry_scr.shape, F32)
        seg_end = seg + padded
        tile_row = lax.broadcasted_iota(jnp.int32, texp_ref.shape, 1).astype(F32) * float(TE)
        te_acc = jnp.zeros(texp_ref.shape, jnp.int32)
        for e in range(N_EXPERTS):
            te_acc = te_acc + jnp.where(seg_end[:, e:e + 1] <= tile_row, 1, 0)
        texp_ref[...] = jnp.minimum(te_acc, N_EXPERTS - 1)
        nused_ref[...] = jnp.broadcast_to(seg_end[:, N_EXPERTS - 1:N_EXPERTS] / float(TE), nused_ref.shape).astype(jnp.int32)

    @pl.when((ph == 1) & (i == 0))
    def _():
        row = lax.broadcasted_iota(jnp.int32, (T, 1), 0)
        col = lax.broadcasted_iota(jnp.int32, (1, T), 1)
        before_scr[...] = (col < row).astype(BF16)

    @pl.when(ph == 1)
    def _():
        cum = _dot(before_scr[...], m.astype(BF16)) + carry_scr[...] + seg_scr[...]
        p1 = jnp.sum(jnp.where(m1, cum, 0.0), axis=1, keepdims=True)
        p2 = jnp.sum(jnp.where(m2, cum, 0.0), axis=1, keepdims=True)
        pos_ref[...] = jnp.where(lane == 0, p1, jnp.where(lane == 1, p2, 0.0)).astype(jnp.int32)
        carry_scr[...] += colsum


def _rank(route, n_tiles, tm):
    rows = route.shape[0]
    nb = rows // tm
    nt_pad = -(-n_tiles // LANES) * LANES
    return pl.pallas_call(
        _rank_kernel, name="rank", grid=(2, nb),
        in_specs=[pl.BlockSpec((tm, LANES), lambda ph, i: (i, 0))],
        out_specs=[pl.BlockSpec((tm, LANES), lambda ph, i: (i * ph, 0)),
                   pl.BlockSpec((1, nt_pad), lambda ph, i: (0, 0)),
                   pl.BlockSpec((1, LANES), lambda ph, i: (0, 0))],
        out_shape=[jax.ShapeDtypeStruct((rows, LANES), jnp.int32),
                   jax.ShapeDtypeStruct((1, nt_pad), jnp.int32),
                   jax.ShapeDtypeStruct((1, LANES), jnp.int32)],
        scratch_shapes=[pltpu.VMEM((1, LANES), F32), pltpu.VMEM((1, LANES), F32), pltpu.VMEM((1, LANES), F32),
                        pltpu.VMEM((tm, tm), BF16)],
        compiler_params=_params(("arbitrary", "arbitrary")))(route)


def _sc_mesh():
    return plsc.VectorSubcoreMesh(core_axis_name="c", subcore_axis_name="s")


def _sc_scatter_rows(x, idx, n_out):
    rows, d = x.shape
    mesh = _sc_mesh()
    n_workers = mesh.num_cores * mesh.num_subcores
    steps = idx.shape[1] // SC_WINDOW // n_workers
    assert steps * SC_WINDOW * n_workers == idx.shape[1] and rows % SC_WINDOW == 0

    @functools.partial(pl.kernel, out_type=jax.ShapeDtypeStruct((n_out, d), x.dtype), mesh=mesh,
                       scratch_types=[pltpu.VMEM((1, SC_WINDOW), jnp.int32), pltpu.VMEM((SC_WINDOW, d), x.dtype)])
    def scatter(x_hbm, i_hbm, o_hbm, i_vmem, buf):
        first = (lax.axis_index("c") * mesh.num_subcores + lax.axis_index("s")) * steps

        @pl.loop(0, steps)
        def _(t):
            off = (first + t) * SC_WINDOW
            pltpu.sync_copy(i_hbm.at[:, pl.ds(off, SC_WINDOW)], i_vmem)
            pltpu.sync_copy(x_hbm.at[pl.ds(off % rows, SC_WINDOW)], buf)
            pltpu.sync_copy(buf, o_hbm.at[i_vmem.at[0]])

    return scatter(x, idx)


def _sc_gather_rows(x, idx):
    d = x.shape[1]
    n = idx.shape[1]
    mesh = _sc_mesh()
    n_workers = mesh.num_cores * mesh.num_subcores
    steps = n // SC_WINDOW // n_workers
    assert steps * SC_WINDOW * n_workers == n

    @functools.partial(pl.kernel, out_type=jax.ShapeDtypeStruct((n, d), x.dtype), mesh=mesh,
                       scratch_types=[pltpu.VMEM((1, SC_WINDOW), jnp.int32), pltpu.VMEM((SC_WINDOW, d), x.dtype)])
    def gather(x_hbm, i_hbm, o_hbm, i_vmem, buf):
        first = (lax.axis_index("c") * mesh.num_subcores + lax.axis_index("s")) * steps

        @pl.loop(0, steps)
        def _(t):
            off = (first + t) * SC_WINDOW
            pltpu.sync_copy(i_hbm.at[:, pl.ds(off, SC_WINDOW)], i_vmem)
            pltpu.sync_copy(x_hbm.at[i_vmem.at[0]], buf)
            pltpu.sync_copy(buf, o_hbm.at[pl.ds(off, SC_WINDOW)])

    return gather(x, idx)


def _gmm_kernel(texp_ref, nused_ref, x_ref, wg_ref, wu_ref, wd_ref, o_ref, wg_s, wu_s, wd_s):
    d = pl.program_id(0)

    @pl.when((d == 0) | (texp_ref[d] != texp_ref[jnp.maximum(d - 1, 0)]))
    def _():
        wg_s[...] = wg_ref[...].astype(BF16)
        wu_s[...] = wu_ref[...].astype(BF16)
        wd_s[...] = wd_ref[...].astype(BF16)

    @pl.when(d < nused_ref[0])
    def _():
        a, b = _unpack_pair(x_ref[...])
        xb = jnp.concatenate([a.astype(BF16), b.astype(BF16)], axis=1)
        gate = _dot(xb, wg_s[...])
        h = gate * _sigmoid(gate) * _dot(xb, wu_s[...])
        y = _dot(h.astype(BF16), wd_s[...])
        o_ref[...] = _pack_pair(y[:, :D_MODEL // 2], y[:, D_MODEL // 2:])


def _gmm(xs, texp, nused, wg, wu, wd, layer):
    rows = xs.shape[0]
    wmap = lambda d, te, nu: (layer, te[d], 0, 0)
    grid_spec = pltpu.PrefetchScalarGridSpec(
        num_scalar_prefetch=2, grid=(rows // TE,),
        in_specs=[pl.BlockSpec((TE, D_MODEL // 2), lambda d, te, nu: (d, 0)),
                  pl.BlockSpec((None, None, D_MODEL, D_EXPERT), wmap),
                  pl.BlockSpec((None, None, D_MODEL, D_EXPERT), wmap),
                  pl.BlockSpec((None, None, D_EXPERT, D_MODEL), wmap)],
        out_specs=pl.BlockSpec((TE, D_MODEL // 2), lambda d, te, nu: (d, 0)),
        scratch_shapes=[pltpu.VMEM((D_MODEL, D_EXPERT), BF16), pltpu.VMEM((D_MODEL, D_EXPERT), BF16),
                        pltpu.VMEM((D_EXPERT, D_MODEL), BF16)])
    return pl.pallas_call(
        _gmm_kernel, name="gmm", grid_spec=grid_spec,
        out_shape=jax.ShapeDtypeStruct((rows, D_MODEL // 2), jnp.uint32),
        compiler_params=_params(("arbitrary",)))(texp, nused, xs, wg, wu, wd)


def _combine_kernel(x_ref, g0_ref, g1_ref, route_ref, g_ref, b_ref, o_ref):
    r = route_ref[...]
    y0 = jnp.concatenate(_unpack_pair(g0_ref[...]), axis=1)
    y1 = jnp.concatenate(_unpack_pair(g1_ref[...]), axis=1)
    f = y0 * r[:, ROUTE_W1:ROUTE_W1 + 1] + y1 * r[:, ROUTE_W2:ROUTE_W2 + 1]
    o_ref[...] = _layer_norm(ALPHA * x_ref[...] + f, g_ref[...], b_ref[...])


def _combine(x, g, route, ln_g, ln_b, tm):
    rows = x.shape[0]
    nb = rows // tm
    return pl.pallas_call(
        _combine_kernel, name="combine", grid=(nb,),
        in_specs=[pl.BlockSpec((tm, D_MODEL), lambda i: (i, 0)),
                  pl.BlockSpec((tm, D_MODEL // 2), lambda i: (i, 0)),
                  pl.BlockSpec((tm, D_MODEL // 2), lambda i: (nb + i, 0)),
                  pl.BlockSpec((tm, LANES), lambda i: (i, 0)), _full_spec(ln_g), _full_spec(ln_b)],
        out_specs=pl.BlockSpec((tm, D_MODEL), lambda i: (i, 0)),
        out_shape=jax.ShapeDtypeStruct((rows, D_MODEL), F32),
        compiler_params=_params(("parallel",)))(x, g, g, route, ln_g, ln_b)


def _moe_routed(x1, x1b, route, wg, wu, wd, layer, ln_g, ln_b, tm):
    rows = x1.shape[0]
    n_rows = 2 * rows + N_EXPERTS * TE
    pos, texp, nused = _rank(route, n_rows // TE, RANK_TILE)
    idx = jnp.concatenate([pos[:, 0], pos[:, 1]])[None, :]
    xs = _sc_scatter_rows(x1b, idx, n_rows)
    ys = _gmm(xs, texp[0, :n_rows // TE], nused[0, :1], wg, wu, wd, layer)
    g = _sc_gather_rows(ys, idx)
    return _combine(x1, g, route, ln_g, ln_b, tm)


def _rope128(x, cos_t, sin_t):
    lane = lax.broadcasted_iota(jnp.int32, (1, LANES), 1)
    half = MLA_ROPE // 2
    swapped = jnp.where(lane < half, pltpu.roll(x, LANES - half, axis=1), pltpu.roll(x, half, axis=1))
    return x * cos_t + swapped * sin_t


def _odd_proj_kernel(x_ref, cos_ref, sin_ref, w_ref, gq_ref, gkv_ref, wuq_ref, wukv_ref,
                     q_ref, ckv_ref, kpe_ref, kpe16_ref, kn_ref, vn_ref):
    tm = x_ref.shape[0]
    group = MIX_GROUP if tm % MIX_GROUP == 0 else tm
    scale = (MLA_NOPE + MLA_ROPE) ** -0.5 * LOG2E
    n_k = MLA_HEADS * MLA_NOPE
    for r0 in range(0, tm, group):
        rs = slice(r0, r0 + group)
        z = _dot(x_ref[rs, :].astype(BF16), w_ref[...])
        cq = _rms_norm(z[:, :MLA_Q_LORA], gq_ref[...])
        ckv = _rms_norm(z[:, MLA_Q_LORA:MLA_Q_LORA + MLA_KV_LORA], gkv_ref[...])
        ckv_ref[rs, :] = ckv
        kv = _dot(ckv.astype(BF16), wukv_ref[...])
        kn_ref[rs, :] = kv[:, :n_k].astype(BF16)
        vn_ref[rs, :] = kv[:, n_k:].astype(BF16)
        cos_t, sin_t = cos_ref[rs, :], sin_ref[rs, :]
        kpe = _rope128(z[:, MLA_Q_LORA + MLA_KV_LORA:], cos_t, sin_t)
        kpe_ref[rs, :] = kpe[:, :MLA_ROPE]
        kpe16_ref[rs, :] = kpe.astype(BF16)
        qf = _dot(cq.astype(BF16), wuq_ref[...])
        for h in range(MLA_HEADS):
            c0 = h * MLA_QPAD
            q_ref[rs, c0:c0 + MLA_NOPE] = (qf[:, c0:c0 + MLA_NOPE] * scale).astype(BF16)
            qr = _rope128(qf[:, c0 + MLA_NOPE:c0 + MLA_QPAD], cos_t, sin_t)
            q_ref[rs, c0 + MLA_NOPE:c0 + MLA_QPAD] = (qr * scale).astype(BF16)


def _odd_proj(x, cos_t, sin_t, w_in, gq, gkv, wuq, w_ukv, tm):
    rows = x.shape[0]
    period = cos_t.shape[0] // tm
    assert rows % tm == 0 and cos_t.shape[0] % tm == 0
    outs = [(MLA_HEADS * MLA_QPAD, BF16), (MLA_KV_LORA, F32), (MLA_ROPE, F32), (LANES, BF16),
            (MLA_HEADS * MLA_NOPE, BF16), (MLA_HEADS * MLA_V, BF16)]
    full = [w_in, gq, gkv, wuq, w_ukv]
    table = pl.BlockSpec((tm, LANES), lambda i: (i % period, 0))
    return pl.pallas_call(
        _odd_proj_kernel, name="odd_proj", grid=(rows // tm,),
        in_specs=[pl.BlockSpec((tm, D_MODEL), lambda i: (i, 0)), table, table] + [_full_spec(a) for a in full],
        out_specs=[pl.BlockSpec((tm, c), lambda i: (i, 0)) for c, _ in outs],
        out_shape=[jax.ShapeDtypeStruct((rows, c), dt) for c, dt in outs],
        compiler_params=_params(("parallel",)))(x, cos_t, sin_t, *full)


def _mla_absorbed_kernel(q_ref, cp_ref, rp_ref, cn_ref, rn_ref, wuk_ref, wuv_ref, o_ref, m_scr, acc_scr, *, tkp):
    tq = q_ref.shape[0]
    rows = MLA_HEADS * tq
    q = q_ref[...]
    qa = []
    for h in range(MLA_HEADS):
        c0 = h * MLA_QPAD
        q_abs = _dot_nt(q[:, c0:c0 + MLA_NOPE], wuk_ref[h])
        qa.append(jnp.concatenate([q_abs.astype(BF16), q[:, c0 + MLA_NOPE:c0 + MLA_NOPE + MLA_ROPE]], axis=1))
    qs = jnp.concatenate(qa, axis=0)
    m_scr[...] = jnp.full(m_scr.shape, NEG, F32)
    acc_scr[...] = jnp.zeros(acc_scr.shape, F32)

    def update(c, r):
        c = c.astype(BF16)
        s = _dot_nt(qs, jnp.concatenate([c, r.astype(BF16)], axis=1))
        m_prev = m_scr[...]
        m_new = jnp.maximum(m_prev, jnp.max(s, axis=1, keepdims=True))
        if s.shape[1] % LANES == 0:
            p = jnp.exp2(s - jnp.tile(m_new, (1, s.shape[1] // LANES)))
        else:
            p = jnp.exp2(s - m_new[:, :1])
        c1 = jnp.concatenate([c, jnp.ones((c.shape[0], LANES), BF16)], axis=1)
        acc_scr[...] = (jnp.tile(jnp.exp2(m_prev - m_new), (1, acc_scr.shape[1] // LANES)) * acc_scr[...]
                        + _dot(p.astype(BF16), c1))
        m_scr[...] = m_new

    def past_body(j, carry):
        rs = pl.ds(pl.multiple_of(j * tkp, tkp), tkp)
        update(cp_ref[rs, :], rp_ref[rs, :])
        return carry
    lax.fori_loop(0, cp_ref.shape[0] // tkp, past_body, 0)
    update(cn_ref[...], rn_ref[:, :MLA_ROPE])

    acc = acc_scr[...]
    lat = (acc[:, :MLA_KV_LORA] / jnp.tile(acc[:, MLA_KV_LORA:], (1, MLA_KV_LORA // LANES))).astype(BF16)
    for h in range(MLA_HEADS):
        o_ref[:, h * MLA_V:(h + 1) * MLA_V] = _dot(lat[h * tq:(h + 1) * tq, :],
                                                   wuv_ref[:, h * MLA_V:(h + 1) * MLA_V]).astype(o_ref.dtype)


def _mla_absorbed(q, ckv_past, kpe_past, ckv_new, kpe_new, wuk_t, wuv, n_seq, tq, tkp):
    p = ckv_past.shape[1]
    assert p % tkp == 0
    rows = MLA_HEADS * tq
    return pl.pallas_call(
        functools.partial(_mla_absorbed_kernel, tkp=tkp), name="mla_absorbed", grid=(n_seq,),
        in_specs=[pl.BlockSpec((tq, MLA_HEADS * MLA_QPAD), lambda b: (b, 0)),
                  pl.BlockSpec((None, p, MLA_KV_LORA), lambda b: (b, 0, 0)),
                  pl.BlockSpec((None, p, MLA_ROPE), lambda b: (b, 0, 0)),
                  pl.BlockSpec((tq, MLA_KV_LORA), lambda b: (b, 0)),
                  pl.BlockSpec((tq, LANES), lambda b: (b, 0)),
                  _full_spec(wuk_t), _full_spec(wuv)],
        out_specs=pl.BlockSpec((tq, MLA_HEADS * MLA_V), lambda b: (b, 0)),
        out_shape=jax.ShapeDtypeStruct((n_seq * tq, MLA_HEADS * MLA_V), BF16),
        scratch_shapes=[pltpu.VMEM((rows, LANES), F32), pltpu.VMEM((rows, MLA_KV_LORA + LANES), F32)],
        compiler_params=_params(("parallel",)))(q, ckv_past, kpe_past, ckv_new, kpe_new, wuk_t, wuv)


def _rope_tables(pos):
    half = MLA_ROPE // 2
    inv = ROPE_BASE ** (-jnp.arange(half, dtype=F32) / half)
    ang = pos.astype(F32)[:, None] * inv[None, :]
    cos, sin = jnp.cos(ang), jnp.sin(ang)
    z = jnp.zeros((pos.shape[0], LANES - MLA_ROPE), F32)
    return jnp.concatenate([cos, cos, z], axis=1), jnp.concatenate([-sin, sin, z], axis=1)


def _pad_rows(a, n):
    return jnp.pad(a, ((0, n - a.shape[0]),) + ((0, 0),) * (a.ndim - 1))


def kernel(x_prompt, x_sample, state_hgrn2, cache_fox_k, cache_fox_v, cache_fox_logf, cache_mla_ckv, cache_mla_kpe, meta_tokens, even_w_in, hg_lb_logits, hg_norm_g, fox_forget_bias, even_w_out, mla_w_in, mla_q_norm_g, mla_kv_norm_g, mla_w_uq, mla_w_uk, mla_w_uv, mla_w_out, ln_mix_g, ln_mix_b, ln_ffn_g, ln_ffn_b, router_w, router_bias, moe_w_gate, moe_w_up, moe_w_down):
    B, T, _ = x_prompt.shape
    Bs, Ts, _ = x_sample.shape
    P = cache_fox_k.shape[2]
    RM = B * T
    RS = Bs * Ts
    RSM = -(-(RS + N_META) // LANES) * LANES
    ME = slice(RS, RS + N_META)
    TM_MAIN, TM_MOE, TQ = TILE_PROJ, TILE_FFN, TILE_ATTN

    xm = x_prompt.reshape(RM, D_MODEL)
    xs = _pad_rows(jnp.concatenate([x_sample.reshape(RS, D_MODEL), meta_tokens.astype(F32)], axis=0), RSM)

    w_in0 = even_w_in[0]
    n_main = 7 * HG_W
    w_even = w_in0[:, :n_main].astype(BF16)
    w_even_f = jnp.pad(w_in0[:, n_main:], ((0, 0), (0, LANES - FOX_HEADS))).astype(BF16)
    fb_pad = jnp.pad(fox_forget_bias[0][None, :], ((0, 0), (0, LANES - FOX_HEADS)))
    g_hg = hg_norm_g[0].reshape(1, HG_W)
    w_out0 = even_w_out[0].astype(BF16)
    e_mat = ((jnp.arange(HG_SUB * HG_DK)[:, None] // HG_DK) == (jnp.arange(CHUNK)[None, :] % HG_SUB)).astype(BF16)

    w_odd = jnp.pad(mla_w_in[0], ((0, 0), (0, LANES - MLA_ROPE))).astype(BF16)
    gq = mla_q_norm_g[0][None, :]
    gkv = mla_kv_norm_g[0][None, :]
    wuq = mla_w_uq[0].reshape(MLA_Q_LORA, MLA_HEADS, MLA_NOPE + MLA_ROPE)
    wuq = jnp.pad(wuq, ((0, 0), (0, 0), (0, MLA_QPAD - MLA_NOPE - MLA_ROPE)))
    wuq = wuq.reshape(MLA_Q_LORA, MLA_HEADS * MLA_QPAD).astype(BF16)
    w_ukv = jnp.concatenate([mla_w_uk[0].reshape(MLA_KV_LORA, -1), mla_w_uv[0].reshape(MLA_KV_LORA, -1)],
                            axis=1).astype(BF16)
    w_out1 = mla_w_out[0].astype(BF16)

    rw = jnp.pad(router_w, ((0, 0), (0, LANES - N_EXPERTS))).astype(BF16)
    rb = jnp.pad(router_bias.astype(F32)[:, None], ((0, LANES - N_EXPERTS), (0, 0)))
    experts = (moe_w_gate, moe_w_up, moe_w_down)
    row2 = lambda a: a[None, :]

    def ffn(x, acts, w_out, l, tm_mix, tm_moe, routed):
        x1, x1b, route = _mix(x, acts, w_out, row2(ln_mix_g[l]), row2(ln_mix_b[l]), rw, rb, tm_mix)
        ln = (row2(ln_ffn_g[l]), row2(ln_ffn_b[l]))
        if routed:
            return _moe_routed(x1, x1b, route, *experts, l, *ln, tm_mix)
        return _moe(x1, route, *experts, l, *ln, tm_moe)

    ps = _even_proj(xs, w_even, w_even_f, hg_lb_logits, fb_pad, RSM, 0)
    ps = dict(zip(("hq", "lf", "hk", "hv", "hgate", "fq", "fk", "fv", "fk16", "fv16", "flf"), ps))
    pm = _even_proj_cache(xm, w_even, w_even_f, hg_lb_logits, fb_pad, ps["fk"][ME], ps["fv"][ME], TM_MAIN, 0, B, T)
    pm = dict(zip(("hq", "lf", "hk", "hv", "hgate", "fq", "fk16", "fv16", "flf", "fk_cache", "fv_cache"), pm))

    hg_keys = ("hq", "lf", "hk", "hv", "hgate", "flf")
    meta_in = [_pad_rows(ps[n][ME], CHUNK) for n in hg_keys]
    zero_s = jnp.zeros((1, HG_HEADS, HG_DK, HG_DV), F32)
    zero_f = jnp.zeros((1, 1, LANES), F32)
    o_hg_meta, fc_meta, s_meta = _hgrn2(*meta_in, g_hg, e_mat, zero_s, zero_f, 1, CHUNK, 0, CHUNK)
    o_hg_meta, fc_meta = o_hg_meta[:N_META], fc_meta[:N_META]
    f_meta_end = fc_meta[N_META - 1:N_META][None]

    o_hg_m, fc_m, s_main = _hgrn2(*[pm[n] for n in hg_keys], g_hg, e_mat, s_meta, f_meta_end, B, T, 0, TILE_HGRN2)

    logf_c = jnp.pad(jnp.transpose(cache_fox_logf[0], (0, 2, 1)), ((0, 0), (0, HG_SUB - FOX_HEADS), (0, 0)))
    fpast = _cumsum_lanes(logf_c.reshape(Bs * HG_SUB, P), TILE_CUMSUM).reshape(Bs, HG_SUB, P)[:, :FOX_HEADS, :]
    f0_s = jnp.pad(fpast[:, :, P - 1][:, None, :], ((0, 0), (0, 0), (0, LANES - FOX_HEADS)))
    o_hg_s, fc_s, s_samp = _hgrn2(*[ps[n] for n in hg_keys], g_hg, e_mat, state_hgrn2[0], f0_s, Bs, Ts, 0, CHUNK)

    def bias_layouts(fc, n_seq, seq_len):
        f4 = fc[:, :FOX_HEADS].T
        return f4[:, :, None], f4.reshape(FOX_HEADS, n_seq, 1, seq_len)

    fq_m, fk_m = bias_layouts(fc_m, B, T)
    fq_m = jnp.broadcast_to(fq_m, fq_m.shape[:2] + (LANES,))
    fq_s, fk_s = bias_layouts(fc_s, Bs, Ts)
    fq_t, fk_t = bias_layouts(fc_meta, 1, N_META)

    fox_kw = dict(n_heads=FOX_HEADS, dq=FOX_DH, dk=FOX_DH, dv=FOX_DH, mask_mode="causal")
    meta_past = dict(k=ps["fk16"][ME][None], v=ps["fv16"][ME][None],
                     fk=jnp.transpose(fk_t, (1, 0, 2, 3)), tk=N_META)
    o_fox_m = _flash_tri(pm["fq"], pm["fk16"], pm["fv16"], n_seq=B, seq_len=T, tq=TQ,
                         fq=fq_m, fkn=fk_m, past=meta_past, **fox_kw)
    o_fox_s = _fox_cached(ps["fq"], fq_s, cache_fox_k[0], cache_fox_v[0], fpast[:, :, None, :],
                          ps["fk16"], ps["fv16"], fk_s, Bs, Ts, TILE_CACHE)
    o_fox_t = _flash(ps["fq"][ME], ps["fk16"][ME], ps["fv16"][ME], n_seq=1, seq_len=N_META,
                     fq=fq_t, fkn=fk_t, **fox_kw)

    o_hg_small = _pad_rows(jnp.concatenate([o_hg_s, o_hg_meta], axis=0), RSM)
    o_fox_small = _pad_rows(jnp.concatenate([o_fox_s, o_fox_t], axis=0), RSM)
    xm = ffn(xm, [o_hg_m, o_fox_m], w_out0, 0, TM_MOE, TM_MOE, True)
    xs = ffn(xs, [o_hg_small, o_fox_small], w_out0, 0, RSM, RSM, False)

    cos_m, sin_m = _rope_tables(N_META + jnp.arange(T, dtype=jnp.int32))
    pos_small = _pad_rows(jnp.concatenate([jnp.tile(P + jnp.arange(Ts, dtype=jnp.int32), Bs),
                                           jnp.arange(N_META, dtype=jnp.int32)]), RSM)
    cos_s, sin_s = _rope_tables(pos_small)
    qm, ckv_m, kpe_m, kpe16_m, kn_m, vn_m = _odd_proj(xm, cos_m, sin_m, w_odd, gq, gkv, wuq, w_ukv, TM_MOE)
    qs, ckv_s, kpe_s, kpe16_s, kn_s, vn_s = _odd_proj(xs, cos_s, sin_s, w_odd, gq, gkv, wuq, w_ukv, RSM)
    wuk_t = jnp.transpose(mla_w_uk[0], (1, 0, 2)).astype(BF16)
    wuv = mla_w_uv[0].reshape(MLA_KV_LORA, MLA_HEADS * MLA_V).astype(BF16)

    mla_kw = dict(n_heads=MLA_HEADS, dq=MLA_QPAD, dk=MLA_NOPE, dv=MLA_V)
    meta_past = dict(k=kn_s[ME][None], v=vn_s[ME][None], r=kpe16_s[ME][None], tk=N_META)
    o_m = _flash_tri(qm, kn_m, vn_m, n_seq=B, seq_len=T, tq=TQ, rn=kpe16_m,
                     past=meta_past, mask_mode="chunk", **mla_kw)
    assert P % CHUNK == 0 and Ts <= CHUNK
    o_s = _mla_absorbed(qs, cache_mla_ckv[0], cache_mla_kpe[0], ckv_s, kpe16_s, wuk_t, wuv, Bs, Ts, TILE_CACHE)
    o_t = _flash(qs[ME], kn_s[ME], vn_s[ME], n_seq=1, seq_len=N_META, rn=kpe16_s[ME], mask_mode="full", **mla_kw)
    xm = ffn(xm, [o_m], w_out1, 1, TM_MOE, TM_MOE, True)
    xs = ffn(xs, [_pad_rows(jnp.concatenate([o_s, o_t], axis=0), RSM)], w_out1, 1, RSM, RSM, False)

    def with_meta(main, small, *width):
        meta = jnp.broadcast_to(small[ME][None], (B, N_META) + width)
        return jnp.concatenate([meta, main.reshape((B, T) + width)], axis=1)

    y_prompt = xm.reshape(B, T, D_MODEL)
    y_sample = xs[:RS].reshape(Bs, Ts, D_MODEL)
    hg_p = s_main[None]
    fk_p = pm["fk_cache"][None]
    fv_p = pm["fv_cache"][None]
    flf_p = with_meta(pm["flf"][:, :FOX_HEADS], ps["flf"][:, :FOX_HEADS], FOX_HEADS)[None]
    ckv_p = with_meta(ckv_m, ckv_s, MLA_KV_LORA)[None]
    kpe_p = with_meta(kpe_m, kpe_s, MLA_ROPE)[None]
    hg_s = s_samp[None]
    fk_s_out = ps["fk"][:RS].reshape(1, Bs, Ts, FOX_HEADS, FOX_DH)
    fv_s_out = ps["fv"][:RS].reshape(1, Bs, Ts, FOX_HEADS, FOX_DH)
    flf_s = ps["flf"][:RS, :FOX_HEADS].reshape(1, Bs, Ts, FOX_HEADS)
    ckv_so = ckv_s[:RS].reshape(1, Bs, Ts, MLA_KV_LORA)
    kpe_so = kpe_s[:RS].reshape(1, Bs, Ts, MLA_ROPE)
    return (y_prompt, y_sample, hg_p, fk_p, fv_p, flf_p, ckv_p, kpe_p,
            hg_s, fk_s_out, fv_s_out, flf_s, ckv_so, kpe_so)
```

```python
import functools

import jax
import jax.numpy as jnp
from jax import lax
from jax.experimental import pallas as pl
from jax.experimental.pallas import tpu as pltpu
from jax.experimental.pallas import tpu_sc as plsc

D_MODEL = 1024
CHUNK = 64
N_META = 16
HG_HEADS = 4
HG_DK = 128
HG_DV = 128
HG_W = HG_HEADS * HG_DK
FOX_HEADS = 4
FOX_DH = 128
FOX_W = FOX_HEADS * FOX_DH
MLA_HEADS = 8
MLA_Q_LORA = 512
MLA_KV_LORA = 256
MLA_NOPE = 128
MLA_ROPE = 64
MLA_V = 128
MLA_QPAD = 256
ROPE_BASE = 10000.0
N_EXPERTS = 16
N_GROUPS = 4
EXPERTS_PER_GROUP = 4
D_EXPERT = 256
DEPTH = 2
ALPHA = (2 * DEPTH) ** 0.25
LN_EPS = 1e-5
RMS_EPS = 1e-6

LANES = 128
HG_SUB = 8
HG_GROUP = 4
NEG = -1e30
LOG2E = 1.4426950408889634
F32 = jnp.float32
BF16 = jnp.bfloat16
VMEM_LIMIT = 56 * 1024 * 1024
TILE_PROJ = 512
TILE_FFN = 1024
TILE_ATTN = 512
TILE_HGRN2 = 1024
TILE_CUMSUM = 512
TILE_CACHE = 1024


def _dot(a, b):
    return jnp.dot(a, b, preferred_element_type=F32)


def _dot_nt(a, b):
    return lax.dot_general(a, b, (((1,), (1,)), ((), ())), preferred_element_type=F32)


def _dot_tn(a, b):
    return lax.dot_general(a, b, (((0,), (0,)), ((), ())), preferred_element_type=F32)


def _split3(x):
    hi = x.astype(BF16)
    r = x - hi.astype(F32)
    mid = r.astype(BF16)
    lo = (r - mid.astype(F32)).astype(BF16)
    return hi, mid, lo


def _cumsum_rows(tri, x):
    hi, mid, lo = _split3(x)
    return _dot(tri, hi) + _dot(tri, mid) + _dot(tri, lo)


def _sigmoid(x):
    return 1.0 / (1.0 + jnp.exp(-x))


def _log_sigmoid(x):
    return jnp.minimum(x, 0.0) - jnp.log(1.0 + jnp.exp(-jnp.abs(x)))


def _layer_norm(x, g, b):
    mu = jnp.mean(x, axis=-1, keepdims=True)
    xc = x - mu
    var = jnp.mean(xc * xc, axis=-1, keepdims=True)
    return xc * lax.rsqrt(var + LN_EPS) * g + b


def _rms_norm(x, g):
    return x * lax.rsqrt(jnp.mean(x * x, axis=-1, keepdims=True) + RMS_EPS) * g


def _params(sem):
    return pltpu.CompilerParams(dimension_semantics=sem, vmem_limit_bytes=VMEM_LIMIT)


def _full_spec(a):
    nd = a.ndim
    return pl.BlockSpec(a.shape, lambda *_: (0,) * nd)


def _row_call(kernel, name, rows, tm, row_ins, full_ins, outs, scratch=()):
    assert rows % tm == 0
    in_specs = [pl.BlockSpec((tm, a.shape[1]), lambda i: (i, 0)) for a in row_ins]
    in_specs += [_full_spec(a) for a in full_ins]
    trail = [c if isinstance(c, tuple) else (c,) for c, _ in outs]
    out_specs = [pl.BlockSpec((tm,) + t, lambda i, n=len(t): (i,) + (0,) * n) for t in trail]
    out_shape = [jax.ShapeDtypeStruct((rows,) + t, dt) for t, (_, dt) in zip(trail, outs)]
    return pl.pallas_call(
        kernel, name=name, grid=(rows // tm,), in_specs=in_specs, out_specs=out_specs,
        out_shape=out_shape, scratch_shapes=list(scratch),
        compiler_params=_params(("parallel",)))(*row_ins, *full_ins)


def _even_proj_body(x_ref, w_ref, wf_ref, lbl_ref, fb_ref, hq_ref, lf_ref, hk_ref, hv_ref, hgate_ref,
                    fq_ref, fk16_ref, fv16_ref, flf_ref, layer):
    xb = x_ref[...].astype(BF16)

    def blk(j):
        return _dot(xb, w_ref[:, j * HG_W:(j + 1) * HG_W])

    logits = lbl_ref[...]
    e = jnp.exp(logits - jnp.max(logits, axis=0, keepdims=True))
    lb = jnp.sum(e[:layer + 1], axis=0, keepdims=True) / jnp.sum(e, axis=0, keepdims=True)

    hq_ref[...] = blk(0).astype(BF16)
    zf = blk(1)
    lf_ref[...] = jnp.log(lb + (1.0 - lb) * _sigmoid(zf))
    hk_ref[...] = ((1.0 - lb) * _sigmoid(-zf)).astype(BF16)
    hv_ref[...] = blk(2).astype(BF16)
    hgate_ref[...] = _sigmoid(blk(3)).astype(BF16)
    fq_ref[...] = (blk(4) * (FOX_DH ** -0.5 * LOG2E)).astype(BF16)
    fk = blk(5)
    fk16_ref[...] = fk.astype(BF16)
    fv = blk(6)
    fv16_ref[...] = fv.astype(BF16)
    flf_ref[...] = _log_sigmoid(_dot(xb, wf_ref[...]) + fb_ref[...])
    return fk, fv


def _even_proj_kernel(x_ref, w_ref, wf_ref, lbl_ref, fb_ref,
                      hq_ref, lf_ref, hk_ref, hv_ref, hgate_ref,
                      fq_ref, fk_ref, fv_ref, fk16_ref, fv16_ref, flf_ref, *, layer):
    fk, fv = _even_proj_body(x_ref, w_ref, wf_ref, lbl_ref, fb_ref, hq_ref, lf_ref, hk_ref, hv_ref, hgate_ref,
                             fq_ref, fk16_ref, fv16_ref, flf_ref, layer)
    for h in range(FOX_HEADS):
        fk_ref[:, h, :] = fk[:, h * FOX_DH:(h + 1) * FOX_DH]
        fv_ref[:, h, :] = fv[:, h * FOX_DH:(h + 1) * FOX_DH]


_EVEN_OUTS = [(HG_W, BF16), (HG_W, F32), (HG_W, BF16), (HG_W, BF16), (HG_W, BF16), (FOX_W, BF16)]


def _even_proj(x, w_main, w_f, lb_logits, fb_pad, tm, layer):
    rows = x.shape[0]
    outs = _EVEN_OUTS + [((FOX_HEADS, FOX_DH), F32), ((FOX_HEADS, FOX_DH), F32), (FOX_W, BF16), (FOX_W, BF16),
                         (LANES, F32)]
    return _row_call(functools.partial(_even_proj_kernel, layer=layer), "even_proj", rows, tm,
                     [x], [w_main, w_f, lb_logits, fb_pad], outs)


def _even_proj_cache_kernel(x_ref, w_ref, wf_ref, lbl_ref, fb_ref, mk_ref, mv_ref,
                            hq_ref, lf_ref, hk_ref, hv_ref, hgate_ref, fq_ref, fk16_ref, fv16_ref, flf_ref,
                            fk_hbm, fv_hbm, kbuf, vbuf, sem, msem, *, layer, tiles_per_seq):
    i, n = pl.program_id(0), pl.num_programs(0)
    tm = x_ref.shape[0]

    def row_copies(step):
        b = step // tiles_per_seq
        t0 = N_META + (step % tiles_per_seq) * tm
        return [pltpu.make_async_copy(buf.at[:, pl.ds(h * FOX_DH, FOX_DH)], hbm.at[b, pl.ds(t0, tm), h, :], sem.at[a, h])
                for a, (buf, hbm) in enumerate(((kbuf, fk_hbm), (vbuf, fv_hbm))) for h in range(FOX_HEADS)]

    def meta_copies(step):
        b = step // tiles_per_seq
        return [pltpu.make_async_copy(mk_ref, fk_hbm.at[b, pl.ds(0, N_META)], msem.at[0]),
                pltpu.make_async_copy(mv_ref, fv_hbm.at[b, pl.ds(0, N_META)], msem.at[1])]

    fk, fv = _even_proj_body(x_ref, w_ref, wf_ref, lbl_ref, fb_ref, hq_ref, lf_ref, hk_ref, hv_ref, hgate_ref,
                             fq_ref, fk16_ref, fv16_ref, flf_ref, layer)

    @pl.when(i > 0)
    def _():
        for c in row_copies(i - 1):
            c.wait()

    @pl.when((i > 0) & ((i - 1) % tiles_per_seq == 0))
    def _():
        for c in meta_copies(i - 1):
            c.wait()

    kbuf[...] = fk
    vbuf[...] = fv
    for c in row_copies(i):
        c.start()

    @pl.when(i % tiles_per_seq == 0)
    def _():
        for c in meta_copies(i):
            c.start()

    @pl.when(i == n - 1)
    def _():
        for c in row_copies(i):
            c.wait()

    @pl.when((i == n - 1) & (i % tiles_per_seq == 0))
    def _():
        for c in meta_copies(i):
            c.wait()


def _even_proj_cache(x, w_main, w_f, lb_logits, fb_pad, meta_k, meta_v, tm, layer, n_seq, seq_len):
    rows = x.shape[0]
    assert rows == n_seq * seq_len and seq_len % tm == 0
    outs = _EVEN_OUTS + [(FOX_W, BF16), (FOX_W, BF16), (LANES, F32)]
    full = [w_main, w_f, lb_logits, fb_pad, meta_k, meta_v]
    cache = jax.ShapeDtypeStruct((n_seq, N_META + seq_len, FOX_HEADS, FOX_DH), F32)
    return pl.pallas_call(
        functools.partial(_even_proj_cache_kernel, layer=layer, tiles_per_seq=seq_len // tm),
        name="even_proj_cache", grid=(rows // tm,),
        in_specs=[pl.BlockSpec((tm, D_MODEL), lambda i: (i, 0))] + [_full_spec(a) for a in full],
        out_specs=[pl.BlockSpec((tm, c), lambda i: (i, 0)) for c, _ in outs] + [pl.BlockSpec(memory_space=pl.ANY)] * 2,
        out_shape=[jax.ShapeDtypeStruct((rows, c), dt) for c, dt in outs] + [cache, cache],
        scratch_shapes=[pltpu.VMEM((tm, FOX_W), F32), pltpu.VMEM((tm, FOX_W), F32),
                        pltpu.SemaphoreType.DMA((2, FOX_HEADS)), pltpu.SemaphoreType.DMA((2,))],
        compiler_params=_params(("arbitrary",)))(x, *full)


def _bcast_sub(x, j):
    n, c = x.shape
    x3 = x.reshape(n // HG_SUB, HG_SUB, c)
    return jnp.broadcast_to(x3[:, j:j + 1, :], x3.shape).reshape(n, c)


def _level_ref(b, w):
    n, c = b.shape
    parts = [jnp.broadcast_to(b[m * 2 * w + w - 1:m * 2 * w + w, :], (2 * w, c)) for m in range(n // (2 * w))]
    return parts[0] if len(parts) == 1 else jnp.concatenate(parts, axis=0)


def _hgrn2_kernel(q_ref, lf_ref, k_ref, v_ref, gate_ref, flf_ref, g_ref, e_ref, s0_ref, f0_ref,
                  o_ref, fcum_ref, sout_ref, st_scr, fc_scr, *, n_chunks):
    i = pl.program_id(1)
    C = CHUNK

    @pl.when(i == 0)
    def _():
        for h in range(HG_HEADS):
            st_scr[h] = s0_ref[h].T
        fc_scr[...] = f0_ref[...]

    row = lax.broadcasted_iota(jnp.int32, (C, 1), 0)
    col = lax.broadcasted_iota(jnp.int32, (1, C), 1)
    tri = (col <= row).astype(BF16)
    same = lambda w: (row // w) == (col // w)
    levels = (32, 16, 8)

    fc = fc_scr[...]
    for c in range(n_chunks):
        sl = slice(c * C, (c + 1) * C)
        fcum = _cumsum_rows(tri, flf_ref[sl, :]) + fc
        fcum_ref[sl, :] = fcum
        fc = fcum[C - 1:C, :]
    fc_scr[...] = fc

    staged = []
    for c in range(n_chunks):
        sl = slice(c * C, (c + 1) * C)
        per_head = []
        for h0 in range(0, HG_HEADS, HG_GROUP):
            gs = slice(h0 * HG_DK, (h0 + HG_GROUP) * HG_DK)
            b = _cumsum_rows(tri, lf_ref[sl, gs]) * LOG2E
            q = q_ref[sl, gs].astype(F32)
            k = k_ref[sl, gs].astype(F32)
            v = v_ref[sl, gs]
            qb = (q * jnp.exp2(b)).astype(BF16)
            b_last = b[C - 1:C, :]
            kd = (k * jnp.exp2(b_last - b)).astype(BF16)
            e_last = jnp.exp2(b_last)

            pjs = [(jnp.exp2(jnp.where((row % HG_SUB) >= j, b - _bcast_sub(b, j), NEG)) * q
                    * _bcast_sub(k, j)).astype(BF16) for j in range(HG_SUB)]
            lv = []
            for w in levels:
                upper = (row % (2 * w)) >= w
                ew = jnp.exp2(-jnp.abs(b - _level_ref(b, w)))
                lv.append((jnp.where(upper, q * ew, 0.0).astype(BF16), jnp.where(upper, 0.0, k * ew).astype(BF16)))

            for hh in range(HG_GROUP):
                hs = slice(hh * HG_DK, (hh + 1) * HG_DK)
                a = jnp.where(same(HG_SUB), _dot(jnp.concatenate([p[:, hs] for p in pjs], axis=1), e_ref[...]), 0.0)
                for w, (qw, kw) in zip(levels, lv):
                    aw = _dot_nt(qw[:, hs], kw[:, hs])
                    a = a + (aw if 2 * w == C else jnp.where(same(2 * w), aw, 0.0))
                vh = v[:, hs]
                per_head.append((_dot(a.astype(BF16), vh), qb[:, hs], kd[:, hs], e_last[:, hs], vh))
        staged.append(per_head)

    st = [st_scr[h] for h in range(HG_HEADS)]
    for c in range(n_chunks):
        sl = slice(c * C, (c + 1) * C)
        for h, (o_intra, qb_h, kd_h, e_h, vh) in enumerate(staged[c]):
            ho = slice(h * HG_DK, (h + 1) * HG_DK)
            o = o_intra + _dot_nt(qb_h, st[h].astype(BF16))
            st[h] = st[h] * e_h + _dot_tn(vh, kd_h)
            o = _rms_norm(o, g_ref[:, ho])
            o_ref[sl, ho] = (o * gate_ref[sl, ho].astype(F32)).astype(BF16)
    for h in range(HG_HEADS):
        st_scr[h] = st[h]

    @pl.when(i == pl.num_programs(1) - 1)
    def _():
        for h in range(HG_HEADS):
            sout_ref[h] = st_scr[h].T


def _hgrn2(q, lf, k, v, gate, flf, g, e_mat, s0, f0, n_seq, seq_len, row_off, tb):
    assert seq_len % tb == 0 and tb % CHUNK == 0 and row_off % tb == 0
    nb = seq_len // tb
    off = row_off // tb
    per_seq = s0.shape[0] > 1
    rmap = lambda s, i: (off + s * nb + i, 0)
    omap = lambda s, i: (s * nb + i, 0)
    smap = (lambda s, i: (s, 0, 0, 0)) if per_seq else (lambda s, i: (0, 0, 0, 0))
    fmap = (lambda s, i: (s, 0, 0)) if per_seq else (lambda s, i: (0, 0, 0))
    in_specs = [pl.BlockSpec((tb, HG_W), rmap) for _ in range(5)]
    in_specs += [pl.BlockSpec((tb, LANES), rmap), _full_spec(g), _full_spec(e_mat),
                 pl.BlockSpec((None, HG_HEADS, HG_DK, HG_DV), smap), pl.BlockSpec((None, 1, LANES), fmap)]
    out_specs = [pl.BlockSpec((tb, HG_W), omap), pl.BlockSpec((tb, LANES), omap),
                 pl.BlockSpec((None, HG_HEADS, HG_DK, HG_DV), lambda s, i: (s, 0, 0, 0))]
    out_shape = [jax.ShapeDtypeStruct((n_seq * seq_len, HG_W), BF16),
                 jax.ShapeDtypeStruct((n_seq * seq_len, LANES), F32),
                 jax.ShapeDtypeStruct((n_seq, HG_HEADS, HG_DK, HG_DV), F32)]
    scratch = [pltpu.VMEM((HG_HEADS, HG_DV, HG_DK), F32), pltpu.VMEM((1, LANES), F32)]
    return pl.pallas_call(
        functools.partial(_hgrn2_kernel, n_chunks=tb // CHUNK), name="hgrn2",
        grid=(n_seq, nb), in_specs=in_specs, out_specs=out_specs, out_shape=out_shape,
        scratch_shapes=scratch, compiler_params=_params(("parallel", "arbitrary")))(
            q, lf, k, v, gate, flf, g, e_mat, s0, f0)


def _cumsum_kernel(x_ref, tri_ref, o_ref, carry):
    @pl.when(pl.program_id(0) == 0)
    def _():
        carry[...] = jnp.zeros_like(carry)

    hi, mid, lo = _split3(x_ref[...])
    tri = tri_ref[...]
    out = _dot(hi, tri) + _dot(mid, tri) + _dot(lo, tri) + carry[...]
    o_ref[...] = out
    carry[...] = out[:, out.shape[1] - 1:]


def _cumsum_lanes(x, tb):
    r, seq_len = x.shape
    tri = (jnp.arange(tb)[:, None] <= jnp.arange(tb)[None, :]).astype(BF16)
    return pl.pallas_call(
        _cumsum_kernel, name="cumsum", grid=(seq_len // tb,),
        in_specs=[pl.BlockSpec((r, tb), lambda i: (0, i)), _full_spec(tri)],
        out_specs=pl.BlockSpec((r, tb), lambda i: (0, i)),
        out_shape=jax.ShapeDtypeStruct(x.shape, F32),
        scratch_shapes=[pltpu.VMEM((r, 1), F32)],
        compiler_params=_params(("arbitrary",)))(x, tri)


def _flash_kernel(*refs, tq, has_bias, has_rope, mask_mode):
    it = iter(refs)
    q_ref = next(it)
    fq_ref = next(it) if has_bias else None
    kn_ref, vn_ref = next(it), next(it)
    rn_ref = next(it) if has_rope else None
    fkn_ref = next(it) if has_bias else None
    o_ref = next(it)
    m_scr, acc_scr = next(it), next(it)
    dv = o_ref.shape[1]

    q = q_ref[...]
    m_scr[...] = jnp.full(m_scr.shape, NEG, F32)
    acc_scr[...] = jnp.zeros(acc_scr.shape, F32)
    fq_b = jnp.broadcast_to(fq_ref[...] * LOG2E, (tq, LANES)) if has_bias else None

    def scores(k, r, fk):
        if has_rope:
            k = jnp.concatenate([k, r], axis=1)
        s = _dot_nt(q, k.astype(BF16))
        if has_bias:
            s = s + jnp.tile(fq_b, (1, s.shape[1] // LANES)) if s.shape[1] % LANES == 0 else s + fq_b[:, :1]
            s = s - fk * LOG2E
        return s

    def update(s, v, mask):
        if mask is not None:
            s = jnp.where(mask, s, NEG)
        m_prev = m_scr[...]
        m_new = jnp.maximum(m_prev, jnp.max(s, axis=1, keepdims=True))
        alpha = jnp.exp2(m_prev - m_new)
        if s.shape[1] % LANES == 0:
            p = jnp.exp2(s - jnp.tile(m_new, (1, s.shape[1] // LANES)))
        else:
            p = jnp.exp2(s - m_new[:, :1])
        v1 = jnp.concatenate([v.astype(BF16), jnp.ones((v.shape[0], LANES), BF16)], axis=1)
        acc_scr[...] = jnp.tile(alpha, (1, acc_scr.shape[1] // LANES)) * acc_scr[...] + _dot(p.astype(BF16), v1)
        m_scr[...] = m_new

    row = lax.broadcasted_iota(jnp.int32, (tq, 1), 0)
    col = lax.broadcasted_iota(jnp.int32, (1, tq), 1)
    if mask_mode == "causal":
        mask = col <= row
    elif mask_mode == "chunk":
        mask = (col // CHUNK) <= (row // CHUNK)
    else:
        mask = None

    update(scores(kn_ref[...], rn_ref[...] if has_rope else None, fkn_ref[...] if has_bias else None),
           vn_ref[...], mask)
    acc = acc_scr[...]
    o_ref[...] = (acc[:, :dv] / acc[:, dv:]).astype(o_ref.dtype)


def _flash(q, kn, vn, *, n_seq, n_heads, seq_len, dq, dk, dv, mask_mode, fq=None, fkn=None, rn=None):
    tq = seq_len
    has_bias = fq is not None
    has_rope = rn is not None
    ins, specs = [q], [pl.BlockSpec((tq, dq), lambda b, h: (b, h))]
    if has_bias:
        ins.append(fq)
        specs.append(pl.BlockSpec((None, tq, 1), lambda b, h: (h, b, 0)))
    ins += [kn, vn]
    specs += [pl.BlockSpec((seq_len, dk), lambda b, h: (b, h)), pl.BlockSpec((seq_len, dv), lambda b, h: (b, h))]
    if has_rope:
        ins.append(rn)
        specs.append(pl.BlockSpec((seq_len, LANES), lambda b, h: (b, 0)))
    if has_bias:
        ins.append(fkn)
        specs.append(pl.BlockSpec((None, None, 1, seq_len), lambda b, h: (h, b, 0, 0)))
    kern = functools.partial(_flash_kernel, tq=tq, has_bias=has_bias, has_rope=has_rope, mask_mode=mask_mode)
    return pl.pallas_call(
        kern, name="flash", grid=(n_seq, n_heads), in_specs=specs,
        out_specs=pl.BlockSpec((tq, dv), lambda b, h: (b, h)),
        out_shape=jax.ShapeDtypeStruct((n_seq * seq_len, n_heads * dv), BF16),
        scratch_shapes=[pltpu.VMEM((tq, LANES), F32), pltpu.VMEM((tq, dv + LANES), F32)],
        compiler_params=_params(("parallel", "parallel")))(*ins)


def _fox_cached_kernel(q_ref, fq_ref, kp_hbm, vp_hbm, fkp_ref, kn_ref, vn_ref, fkn_ref, o_ref,
                       kbuf, vbuf, ksem, vsem, *, tkp, n_blk):
    b = pl.program_id(0)
    tq = q_ref.shape[0]
    row = lax.broadcasted_iota(jnp.int32, (tq, 1), 0)
    col = lax.broadcasted_iota(jnp.int32, (1, tq), 1)
    ones_p = jnp.ones((tkp, LANES), BF16)
    steps = [(h, j) for h in range(FOX_HEADS) for j in range(n_blk)]

    def copies(i):
        h, j = steps[i]
        slot = i % 2
        src = lambda ref: ref.at[b, pl.ds(j * tkp, tkp), h, :]
        return (pltpu.make_async_copy(src(kp_hbm), kbuf.at[slot], ksem.at[slot]),
                pltpu.make_async_copy(src(vp_hbm), vbuf.at[slot], vsem.at[slot]))

    def update(state, s, v1):
        m_prev, acc = state
        m_new = jnp.maximum(m_prev, jnp.max(s, axis=1, keepdims=True))
        p = jnp.exp2(s - m_new)
        return m_new, jnp.exp2(m_prev - m_new) * acc + _dot(p.astype(BF16), v1)

    for c in copies(0):
        c.start()
    state = None
    for i, (h, j) in enumerate(steps):
        hs = slice(h * FOX_DH, (h + 1) * FOX_DH)
        q = q_ref[:, hs]
        fq = fq_ref[h] * LOG2E
        if j == 0:
            state = (jnp.full((tq, 1), NEG, F32), jnp.zeros((tq, FOX_DH + LANES), F32))
        if i + 1 < len(steps):
            for c in copies(i + 1):
                c.start()
        for c in copies(i):
            c.wait()
        slot = i % 2
        s = _dot_nt(q, kbuf[slot].astype(BF16)) + fq - fkp_ref[h, :, j * tkp:(j + 1) * tkp] * LOG2E
        state = update(state, s, jnp.concatenate([vbuf[slot].astype(BF16), ones_p], axis=1))
        if j == n_blk - 1:
            s = _dot_nt(q, kn_ref[:, hs]) + fq - fkn_ref[h] * LOG2E
            s = jnp.where(col <= row, s, NEG)
            _, acc = update(state, s, jnp.concatenate([vn_ref[:, hs], ones_p[:tq]], axis=1))
            o_ref[:, hs] = (acc[:, :FOX_DH] / acc[:, FOX_DH:FOX_DH + 1]).astype(o_ref.dtype)


def _fox_cached(q, fq, kp, vp, fkp, kn, vn, fkn, n_seq, tq, tkp):
    p = kp.shape[1]
    assert p % tkp == 0
    return pl.pallas_call(
        functools.partial(_fox_cached_kernel, tkp=tkp, n_blk=p // tkp), name="fox_cached", grid=(n_seq,),
        in_specs=[pl.BlockSpec((tq, FOX_W), lambda b: (b, 0)),
                  pl.BlockSpec((FOX_HEADS, tq, 1), lambda b: (0, b, 0)),
                  pl.BlockSpec(memory_space=pl.ANY),
                  pl.BlockSpec(memory_space=pl.ANY),
                  pl.BlockSpec((None, FOX_HEADS, 1, p), lambda b: (b, 0, 0, 0)),
                  pl.BlockSpec((tq, FOX_W), lambda b: (b, 0)),
                  pl.BlockSpec((tq, FOX_W), lambda b: (b, 0)),
                  pl.BlockSpec((FOX_HEADS, None, 1, tq), lambda b: (0, b, 0, 0))],
        out_specs=pl.BlockSpec((tq, FOX_W), lambda b: (b, 0)),
        out_shape=jax.ShapeDtypeStruct((n_seq * tq, FOX_W), BF16),
        scratch_shapes=[pltpu.VMEM((2, tkp, FOX_DH), F32), pltpu.VMEM((2, tkp, FOX_DH), F32),
                        pltpu.SemaphoreType.DMA((2,)), pltpu.SemaphoreType.DMA((2,))],
        compiler_params=_params(("arbitrary",)))(q, fq, kp, vp, fkp, kn, vn, fkn)


FLASH_UNROLL_OFF = 14
FLASH_UNROLL_DIAG = 8


def _tri_tables(nq):
    pairs = [(qi, kj) for qi in range(nq) for kj in range(qi)] + [(qi, qi) for qi in range(nq)] + [(0, 0)]
    return (jnp.array([p[0] for p in pairs], jnp.int32), jnp.array([p[1] for p in pairs], jnp.int32))


def _flash_tri_kernel(qt_ref, kt_ref, *refs, tq, nq, has_bias, has_rope, mask_mode):
    it = iter(refs)
    q_ref = next(it)
    fq_ref = next(it) if has_bias else None
    kp_ref, vp_ref = next(it), next(it)
    rp_ref = next(it) if has_rope else None
    fkp_ref = next(it) if has_bias else None
    kn_ref, vn_ref = next(it), next(it)
    rn_ref = next(it) if has_rope else None
    fkn_ref = next(it) if has_bias else None
    o_ref = next(it)
    m_scr, acc_scr, sa_scr, sb_scr = next(it), next(it), next(it), next(it)
    fqb_scr = next(it) if has_bias else None
    dv = o_ref.shape[1]
    n_off = nq * (nq - 1) // 2
    tile = lambda j: pl.ds(pl.multiple_of(j * tq, tq), tq)
    ones = jnp.ones((tq, LANES), BF16)

    kp = kp_ref[...]
    if has_rope:
        kp = jnp.concatenate([kp, rp_ref[...]], axis=1)
    vp1 = jnp.concatenate([vp_ref[...], ones[:vp_ref.shape[0]]], axis=1)
    for i in range(nq):
        rs = slice(i * tq, (i + 1) * tq)
        s = _dot_nt(q_ref[rs, :], kp)
        if has_bias:
            fb = fq_ref[rs, :] * LOG2E
            fqb_scr[rs, :] = fb
            s = s + fb[:, :s.shape[1]] - fkp_ref[...] * LOG2E
        m0 = jnp.max(s, axis=1, keepdims=True)
        m_scr[i] = jnp.broadcast_to(m0, (tq, LANES))
        acc_scr[i] = _dot(jnp.exp2(s - m0).astype(BF16), vp1)

    def fill(s_ref, t):
        qs, ks = tile(qt_ref[t]), tile(kt_ref[t])
        k = kn_ref[ks, :]
        if has_rope:
            k = jnp.concatenate([k, rn_ref[ks, :]], axis=1)
        s = _dot_nt(q_ref[qs, :], k)
        if has_bias:
            s = s + jnp.tile(fqb_scr[qs, :], (1, tq // LANES)) - fkn_ref[:, ks] * LOG2E
        s_ref[...] = s

    def drain(s_ref, t, mask):
        qi = qt_ref[t]
        s = s_ref[...]
        if mask is not None:
            s = jnp.where(mask, s, NEG)
        m_prev = m_scr[qi]
        m_new = jnp.maximum(m_prev, jnp.max(s, axis=1, keepdims=True))
        p = jnp.exp2(s - jnp.tile(m_new, (1, tq // LANES)))
        v1 = jnp.concatenate([vn_ref[tile(kt_ref[t]), :], ones], axis=1)
        acc = jnp.tile(jnp.exp2(m_prev - m_new), (1, (dv + LANES) // LANES)) * acc_scr[qi] + _dot(p.astype(BF16), v1)
        return qi, m_new, acc

    def keep(s_ref, t):
        qi, m_new, acc = drain(s_ref, t, None)
        m_scr[qi] = m_new
        acc_scr[qi] = acc

    row = lax.broadcasted_iota(jnp.int32, (tq, 1), 0)
    col = lax.broadcasted_iota(jnp.int32, (1, tq), 1)
    mask = {"causal": col <= row, "chunk": (col // CHUNK) <= (row // CHUNK)}[mask_mode]

    def finish(s_ref, t):
        qi, _, acc = drain(s_ref, t, mask)
        o_ref[tile(qi), :] = (acc[:, :dv] / acc[:, dv:]).astype(o_ref.dtype)

    def pipeline(t0, n, unroll, consume):
        assert n % unroll == 0 and unroll % 2 == 0

        def body(i, carry):
            t = t0 + unroll * i
            for u in range(0, unroll, 2):
                fill(sb_scr, t + u + 1)
                consume(sa_scr, t + u)
                fill(sa_scr, t + u + 2)
                consume(sb_scr, t + u + 1)
            return carry
        lax.fori_loop(0, n // unroll, body, 0)

    fill(sa_scr, 0)
    pipeline(0, n_off, FLASH_UNROLL_OFF, keep)
    pipeline(n_off, nq, FLASH_UNROLL_DIAG, finish)


def _flash_tri(q, kn, vn, *, n_seq, n_heads, seq_len, tq, dq, dk, dv, mask_mode, past, fq=None, fkn=None, rn=None):
    nq = seq_len // tq
    has_bias = fq is not None
    has_rope = rn is not None
    tp = past["k"].shape[1]
    m3 = lambda f: (lambda b, h, qt, kt: f(b, h))
    ins, specs = [q], [pl.BlockSpec((seq_len, dq), m3(lambda b, h: (b, h)))]
    if has_bias:
        ins.append(fq)
        specs.append(pl.BlockSpec((None, seq_len, LANES), m3(lambda b, h: (h, b, 0))))
    ins += [past["k"], past["v"]]
    specs += [pl.BlockSpec((None, tp, dk), m3(lambda b, h: (0, 0, h))),
              pl.BlockSpec((None, tp, dv), m3(lambda b, h: (0, 0, h)))]
    if has_rope:
        ins.append(past["r"])
        specs.append(pl.BlockSpec((None, tp, LANES), m3(lambda b, h: (0, 0, 0))))
    if has_bias:
        ins.append(past["fk"])
        specs.append(pl.BlockSpec((None, None, 1, tp), m3(lambda b, h: (0, h, 0, 0))))
    ins += [kn, vn]
    specs += [pl.BlockSpec((seq_len, dk), m3(lambda b, h: (b, h))),
              pl.BlockSpec((seq_len, dv), m3(lambda b, h: (b, h)))]
    if has_rope:
        ins.append(rn)
        specs.append(pl.BlockSpec((seq_len, LANES), m3(lambda b, h: (b, 0))))
    if has_bias:
        ins.append(fkn)
        specs.append(pl.BlockSpec((None, None, 1, seq_len), m3(lambda b, h: (h, b, 0, 0))))
    scratch = [pltpu.VMEM((nq, tq, LANES), F32), pltpu.VMEM((nq, tq, dv + LANES), F32),
               pltpu.VMEM((tq, tq), F32), pltpu.VMEM((tq, tq), F32)]
    if has_bias:
        scratch.append(pltpu.VMEM((seq_len, LANES), F32))
    grid_spec = pltpu.PrefetchScalarGridSpec(
        num_scalar_prefetch=2, grid=(n_seq, n_heads), in_specs=specs,
        out_specs=pl.BlockSpec((seq_len, dv), m3(lambda b, h: (b, h))), scratch_shapes=scratch)
    kern = functools.partial(_flash_tri_kernel, tq=tq, nq=nq, has_bias=has_bias, has_rope=has_rope,
                             mask_mode=mask_mode)
    return pl.pallas_call(
        kern, name="flash_tri", grid_spec=grid_spec,
        out_shape=jax.ShapeDtypeStruct((n_seq * seq_len, n_heads * dv), BF16),
        compiler_params=_params(("parallel", "arbitrary")))(*_tri_tables(nq), *ins)


def _route(sc, sb):
    def top2_sum(v):
        a, b, c, d = v
        a, b = jnp.maximum(a, b), jnp.minimum(a, b)
        c, d = jnp.maximum(c, d), jnp.minimum(c, d)
        hi, lo2 = jnp.maximum(a, c), jnp.minimum(a, c)
        return hi + jnp.maximum(lo2, jnp.maximum(b, d))

    gs = [top2_sum(sb[g * EXPERTS_PER_GROUP:(g + 1) * EXPERTS_PER_GROUP]) for g in range(N_GROUPS)]
    best_v, best_g = gs[0], jnp.zeros(gs[0].shape, jnp.int32)
    for g in range(1, N_GROUPS):
        upd = gs[g] > best_v
        best_v = jnp.where(upd, gs[g], best_v)
        best_g = jnp.where(upd, g, best_g)
    masked = [jnp.where(best_g == (e // EXPERTS_PER_GROUP), sb[e], -jnp.inf) for e in range(N_EXPERTS)]

    def argmax_first(vals, exclude=None):
        bv = jnp.full(vals[0].shape, -jnp.inf, F32)
        bi = jnp.full(vals[0].shape, -1, jnp.int32)
        for e, v in enumerate(vals):
            upd = v > bv
            if exclude is not None:
                upd = upd & (exclude != e)
            bv = jnp.where(upd, v, bv)
            bi = jnp.where(upd, e, bi)
        return bi

    i1 = argmax_first(masked)
    i2 = argmax_first(masked, exclude=i1)
    w1 = sum(jnp.where(i1 == e, sc[e], 0.0) for e in range(N_EXPERTS))
    w2 = sum(jnp.where(i2 == e, sc[e], 0.0) for e in range(N_EXPERTS))
    tot = w1 + w2
    w1, w2 = w1 / tot, w2 / tot
    comb = [jnp.where(i1 == e, w1, 0.0) + jnp.where(i2 == e, w2, 0.0) for e in range(N_EXPERTS)]
    return comb + [i1.astype(F32), i2.astype(F32), w1, w2]


def _mix_kernel(*refs, n_act):
    x_ref = refs[0]
    a_refs = refs[1:1 + n_act]
    w_ref, g_ref, b_ref, rw_ref, rb_ref, x1_ref, x1p_ref, comb_ref, ct_scr = refs[1 + n_act:]
    half = D_MODEL // 2
    tm = x_ref.shape[0]
    group = MIX_GROUP if tm % MIX_GROUP == 0 else tm
    ct_scr[...] = jnp.zeros(ct_scr.shape, F32)
    for r0 in range(0, tm, group):
        rs = slice(r0, r0 + group)
        ys = []
        for n0 in (0, half):
            y = None
            k0 = 0
            for a_ref in a_refs:
                kw = a_ref.shape[1]
                part = _dot(a_ref[rs, :], w_ref[k0:k0 + kw, n0:n0 + half])
                y = part if y is None else y + part
                k0 += kw
            ys.append(y)
        x1 = _layer_norm(ALPHA * x_ref[rs, :] + jnp.concatenate(ys, axis=1), g_ref[...], b_ref[...])
        x1_ref[rs, :] = x1

        x1p_ref[rs, :] = _pack_pair(x1[:, :half], x1[:, half:])
        logits = _dot(x1.astype(BF16), rw_ref[...])
        scores_t = _sigmoid(logits).T
        sc = [scores_t[e:e + 1, :] for e in range(N_EXPERTS)]
        sb = [sc[e] + rb_ref[e:e + 1, :] for e in range(N_EXPERTS)]
        for r, val in enumerate(_route(sc, sb)):
            ct_scr[r:r + 1, rs] = val
        comb_ref[rs, :] = ct_scr[:, rs].T


def _mix(x, acts, w_out, ln_g, ln_b, rw, rb, tm):
    rows = x.shape[0]
    return _row_call(functools.partial(_mix_kernel, n_act=len(acts)), "mix", rows, tm,
                     [x] + list(acts), [w_out, ln_g, ln_b, rw, rb],
                     [(D_MODEL, F32), (D_MODEL // 2, jnp.uint32), (LANES, F32)],
                     scratch=[pltpu.VMEM((LANES, tm), F32)])


def _moe_kernel(x_ref, comb_ref, wg_ref, wu_ref, wd_ref, g_ref, b_ref, o_ref, xb_scr, acc_scr):
    e = pl.program_id(1)

    @pl.when(e == 0)
    def _():
        xb_scr[...] = x_ref[...].astype(BF16)
        acc_scr[...] = jnp.zeros(acc_scr.shape, F32)

    xb = xb_scr[...]
    lane = lax.broadcasted_iota(jnp.int32, (1, LANES), 1)
    c_e = jnp.sum(jnp.where(lane == e, comb_ref[...], 0.0), axis=1, keepdims=True)
    gate = _dot(xb, wg_ref[...].astype(BF16))
    h = gate * _sigmoid(gate) * _dot(xb, wu_ref[...].astype(BF16))
    acc_scr[...] += _dot((h * c_e).astype(BF16), wd_ref[...].astype(BF16))

    @pl.when(e == N_EXPERTS - 1)
    def _():
        o_ref[...] = _layer_norm(ALPHA * x_ref[...] + acc_scr[...], g_ref[...], b_ref[...])


def _moe(x, comb, wg, wu, wd, layer, ln_g, ln_b, tm):
    rows = x.shape[0]
    assert rows % tm == 0
    return pl.pallas_call(
        _moe_kernel, name="moe", grid=(rows // tm, N_EXPERTS),
        in_specs=[pl.BlockSpec((tm, D_MODEL), lambda i, e: (i, 0)),
                  pl.BlockSpec((tm, LANES), lambda i, e: (i, 0)),
                  pl.BlockSpec((None, None, D_MODEL, D_EXPERT), lambda i, e: (layer, e, 0, 0)),
                  pl.BlockSpec((None, None, D_MODEL, D_EXPERT), lambda i, e: (layer, e, 0, 0)),
                  pl.BlockSpec((None, None, D_EXPERT, D_MODEL), lambda i, e: (layer, e, 0, 0)),
                  _full_spec(ln_g), _full_spec(ln_b)],
        out_specs=pl.BlockSpec((tm, D_MODEL), lambda i, e: (i, 0)),
        out_shape=jax.ShapeDtypeStruct((rows, D_MODEL), F32),
        scratch_shapes=[pltpu.VMEM((tm, D_MODEL), BF16), pltpu.VMEM((tm, D_MODEL), F32)],
        compiler_params=_params(("parallel", "arbitrary")))(x, comb, wg, wu, wd, ln_g, ln_b)


ROUTE_E1, ROUTE_E2, ROUTE_W1, ROUTE_W2 = N_EXPERTS, N_EXPERTS + 1, N_EXPERTS + 2, N_EXPERTS + 3
TE = 1024
SC_WINDOW = 128
RANK_TILE = 1024
MIX_GROUP = 256


def _pack_pair(a, b):
    au = lax.bitcast_convert_type(a.astype(BF16).astype(F32), jnp.uint32)
    bu = lax.bitcast_convert_type(b.astype(BF16).astype(F32), jnp.uint32)
    return (au >> 16) | (bu & jnp.uint32(0xFFFF0000))


def _unpack_pair(w):
    a = lax.bitcast_convert_type(w << 16, F32)
    b = lax.bitcast_convert_type(w & jnp.uint32(0xFFFF0000), F32)
    return a, b


def _rank_kernel(route_ref, pos_ref, texp_ref, nused_ref, cnt_scr, carry_scr, seg_scr, before_scr):
    ph, i = pl.program_id(0), pl.program_id(1)
    T = route_ref.shape[0]
    lane = lax.broadcasted_iota(jnp.int32, (1, LANES), 1)
    lane_f = lane.astype(F32)
    r = route_ref[...]
    e1, e2 = r[:, ROUTE_E1:ROUTE_E1 + 1], r[:, ROUTE_E2:ROUTE_E2 + 1]
    m1, m2 = lane_f == e1, lane_f == e2
    m = jnp.where(m1 | m2, 1.0, 0.0)
    colsum = jnp.sum(m, axis=0, keepdims=True)

    @pl.when((ph == 0) & (i == 0))
    def _():
        cnt_scr[...] = jnp.zeros(cnt_scr.shape, F32)

    @pl.when(ph == 0)
    def _():
        cnt_scr[...] += colsum

    @pl.when((ph == 1) & (i == 0))
    def _():
        cnt = cnt_scr[...].astype(jnp.int32)
        padded = (((cnt + (TE - 1)) // TE) * TE).astype(F32)
        rr = lax.broadcasted_iota(jnp.int32, (LANES, 1), 0)
        upper = (rr < lane).astype(BF16)
        hi, mid, lo = _split3(jnp.broadcast_to(padded, (HG_SUB, LANES)))
        seg = (_dot(hi, upper) + _dot(mid, upper) + _dot(lo, upper))[:1, :]
        seg_scr[...] = seg
        carry_scr[...] = jnp.zeros(carry_scr.shape, F32)
        seg_end = seg + padded
        tile_row = lax.broadcasted_iota(jnp.int32, texp_ref.shape, 1).astype(F32) * float(TE)
        te_acc = jnp.zeros(texp_ref.shape, jnp.int32)
        for e in range(N_EXPERTS):
            te_acc = te_acc + jnp.where(seg_end[:, e:e + 1] <= tile_row, 1, 0)
        texp_ref[...] = jnp.minimum(te_acc, N_EXPERTS - 1)
        nused_ref[...] = jnp.broadcast_to(seg_end[:, N_EXPERTS - 1:N_EXPERTS] / float(TE), nused_ref.shape).astype(jnp.int32)

    @pl.when((ph == 1) & (i == 0))
    def _():
        row = lax.broadcasted_iota(jnp.int32, (T, 1), 0)
        col = lax.broadcasted_iota(jnp.int32, (1, T), 1)
        before_scr[...] = (col < row).astype(BF16)

    @pl.when(ph == 1)
    def _():
        cum = _dot(before_scr[...], m.astype(BF16)) + carry_scr[...] + seg_scr[...]
        p1 = jnp.sum(jnp.where(m1, cum, 0.0), axis=1, keepdims=True)
        p2 = jnp.sum(jnp.where(m2, cum, 0.0), axis=1, keepdims=True)
        pos_ref[...] = jnp.where(lane == 0, p1, jnp.where(lane == 1, p2, 0.0)).astype(jnp.int32)
        carry_scr[...] += colsum


def _rank(route, n_tiles, tm):
    rows = route.shape[0]
    nb = rows // tm
    nt_pad = -(-n_tiles // LANES) * LANES
    return pl.pallas_call(
        _rank_kernel, name="rank", grid=(2, nb),
        in_specs=[pl.BlockSpec((tm, LANES), lambda ph, i: (i, 0))],
        out_specs=[pl.BlockSpec((tm, LANES), lambda ph, i: (i * ph, 0)),
                   pl.BlockSpec((1, nt_pad), lambda ph, i: (0, 0)),
                   pl.BlockSpec((1, LANES), lambda ph, i: (0, 0))],
        out_shape=[jax.ShapeDtypeStruct((rows, LANES), jnp.int32),
                   jax.ShapeDtypeStruct((1, nt_pad), jnp.int32),
                   jax.ShapeDtypeStruct((1, LANES), jnp.int32)],
        scratch_shapes=[pltpu.VMEM((1, LANES), F32), pltpu.VMEM((1, LANES), F32), pltpu.VMEM((1, LANES), F32),
                        pltpu.VMEM((tm, tm), BF16)],
        compiler_params=_params(("arbitrary", "arbitrary")))(route)


def _sc_mesh():
    return plsc.VectorSubcoreMesh(core_axis_name="c", subcore_axis_name="s")


def _sc_scatter_rows(x, idx, n_out):
    rows, d = x.shape
    mesh = _sc_mesh()
    n_workers = mesh.num_cores * mesh.num_subcores
    steps = idx.shape[1] // SC_WINDOW // n_workers
    assert steps * SC_WINDOW * n_workers == idx.shape[1] and rows % SC_WINDOW == 0

    @functools.partial(pl.kernel, out_type=jax.ShapeDtypeStruct((n_out, d), x.dtype), mesh=mesh,
                       scratch_types=[pltpu.VMEM((1, SC_WINDOW), jnp.int32), pltpu.VMEM((SC_WINDOW, d), x.dtype)])
    def scatter(x_hbm, i_hbm, o_hbm, i_vmem, buf):
        first = (lax.axis_index("c") * mesh.num_subcores + lax.axis_index("s")) * steps

        @pl.loop(0, steps)
        def _(t):
            off = (first + t) * SC_WINDOW
            pltpu.sync_copy(i_hbm.at[:, pl.ds(off, SC_WINDOW)], i_vmem)
            pltpu.sync_copy(x_hbm.at[pl.ds(off % rows, SC_WINDOW)], buf)
            pltpu.sync_copy(buf, o_hbm.at[i_vmem.at[0]])

    return scatter(x, idx)


def _sc_gather_rows(x, idx):
    d = x.shape[1]
    n = idx.shape[1]
    mesh = _sc_mesh()
    n_workers = mesh.num_cores * mesh.num_subcores
    steps = n // SC_WINDOW // n_workers
    assert steps * SC_WINDOW * n_workers == n

    @functools.partial(pl.kernel, out_type=jax.ShapeDtypeStruct((n, d), x.dtype), mesh=mesh,
                       scratch_types=[pltpu.VMEM((1, SC_WINDOW), jnp.int32), pltpu.VMEM((SC_WINDOW, d), x.dtype)])
    def gather(x_hbm, i_hbm, o_hbm, i_vmem, buf):
        first = (lax.axis_index("c") * mesh.num_subcores + lax.axis_index("s")) * steps

        @pl.loop(0, steps)
        def _(t):
            off = (first + t) * SC_WINDOW
            pltpu.sync_copy(i_hbm.at[:, pl.ds(off, SC_WINDOW)], i_vmem)
            pltpu.sync_copy(x_hbm.at[i_vmem.at[0]], buf)
            pltpu.sync_copy(buf, o_hbm.at[pl.ds(off, SC_WINDOW)])

    return gather(x, idx)


def _gmm_kernel(texp_ref, nused_ref, x_ref, wg_ref, wu_ref, wd_ref, o_ref):
    @pl.when(pl.program_id(0) < nused_ref[0])
    def _():
        a, b = _unpack_pair(x_ref[...])
        xb = jnp.concatenate([a.astype(BF16), b.astype(BF16)], axis=1)
        gate = _dot(xb, wg_ref[...].astype(BF16))
        h = gate * _sigmoid(gate) * _dot(xb, wu_ref[...].astype(BF16))
        y = _dot(h.astype(BF16), wd_ref[...].astype(BF16))
        o_ref[...] = _pack_pair(y[:, :D_MODEL // 2], y[:, D_MODEL // 2:])


def _gmm(xs, texp, nused, wg, wu, wd, layer):
    rows = xs.shape[0]
    wmap = lambda d, te, nu: (layer, te[d], 0, 0)
    grid_spec = pltpu.PrefetchScalarGridSpec(
        num_scalar_prefetch=2, grid=(rows // TE,),
        in_specs=[pl.BlockSpec((TE, D_MODEL // 2), lambda d, te, nu: (d, 0)),
                  pl.BlockSpec((None, None, D_MODEL, D_EXPERT), wmap),
                  pl.BlockSpec((None, None, D_MODEL, D_EXPERT), wmap),
                  pl.BlockSpec((None, None, D_EXPERT, D_MODEL), wmap)],
        out_specs=pl.BlockSpec((TE, D_MODEL // 2), lambda d, te, nu: (d, 0)))
    return pl.pallas_call(
        _gmm_kernel, name="gmm", grid_spec=grid_spec,
        out_shape=jax.ShapeDtypeStruct((rows, D_MODEL // 2), jnp.uint32),
        compiler_params=_params(("arbitrary",)))(texp, nused, xs, wg, wu, wd)


def _combine_kernel(x_ref, g0_ref, g1_ref, route_ref, g_ref, b_ref, o_ref):
    r = route_ref[...]
    y0 = jnp.concatenate(_unpack_pair(g0_ref[...]), axis=1)
    y1 = jnp.concatenate(_unpack_pair(g1_ref[...]), axis=1)
    f = y0 * r[:, ROUTE_W1:ROUTE_W1 + 1] + y1 * r[:, ROUTE_W2:ROUTE_W2 + 1]
    o_ref[...] = _layer_norm(ALPHA * x_ref[...] + f, g_ref[...], b_ref[...])


def _combine(x, g, route, ln_g, ln_b, tm):
    rows = x.shape[0]
    nb = rows // tm
    return pl.pallas_call(
        _combine_kernel, name="combine", grid=(nb,),
        in_specs=[pl.BlockSpec((tm, D_MODEL), lambda i: (i, 0)),
                  pl.BlockSpec((tm, D_MODEL // 2), lambda i: (i, 0)),
                  pl.BlockSpec((tm, D_MODEL // 2), lambda i: (nb + i, 0)),
                  pl.BlockSpec((tm, LANES), lambda i: (i, 0)), _full_spec(ln_g), _full_spec(ln_b)],
        out_specs=pl.BlockSpec((tm, D_MODEL), lambda i: (i, 0)),
        out_shape=jax.ShapeDtypeStruct((rows, D_MODEL), F32),
        compiler_params=_params(("parallel",)))(x, g, g, route, ln_g, ln_b)


def _moe_routed(x1, x1b, route, wg, wu, wd, layer, ln_g, ln_b, tm):
    rows = x1.shape[0]
    n_rows = 2 * rows + N_EXPERTS * TE
    pos, texp, nused = _rank(route, n_rows // TE, RANK_TILE)
    idx = jnp.concatenate([pos[:, 0], pos[:, 1]])[None, :]
    xs = _sc_scatter_rows(x1b, idx, n_rows)
    ys = _gmm(xs, texp[0, :n_rows // TE], nused[0, :1], wg, wu, wd, layer)
    g = _sc_gather_rows(ys, idx)
    return _combine(x1, g, route, ln_g, ln_b, tm)


def _rope128(x, cos_t, sin_t):
    lane = lax.broadcasted_iota(jnp.int32, (1, LANES), 1)
    half = MLA_ROPE // 2
    swapped = jnp.where(lane < half, pltpu.roll(x, LANES - half, axis=1), pltpu.roll(x, half, axis=1))
    return x * cos_t + swapped * sin_t


def _odd_proj_kernel(x_ref, cos_ref, sin_ref, w_ref, gq_ref, gkv_ref, wuq_ref, wukv_ref,
                     q_ref, ckv_ref, kpe_ref, kpe16_ref, kn_ref, vn_ref):
    tm = x_ref.shape[0]
    group = MIX_GROUP if tm % MIX_GROUP == 0 else tm
    scale = (MLA_NOPE + MLA_ROPE) ** -0.5 * LOG2E
    n_k = MLA_HEADS * MLA_NOPE
    for r0 in range(0, tm, group):
        rs = slice(r0, r0 + group)
        z = _dot(x_ref[rs, :].astype(BF16), w_ref[...])
        cq = _rms_norm(z[:, :MLA_Q_LORA], gq_ref[...])
        ckv = _rms_norm(z[:, MLA_Q_LORA:MLA_Q_LORA + MLA_KV_LORA], gkv_ref[...])
        ckv_ref[rs, :] = ckv
        kv = _dot(ckv.astype(BF16), wukv_ref[...])
        kn_ref[rs, :] = kv[:, :n_k].astype(BF16)
        vn_ref[rs, :] = kv[:, n_k:].astype(BF16)
        cos_t, sin_t = cos_ref[rs, :], sin_ref[rs, :]
        kpe = _rope128(z[:, MLA_Q_LORA + MLA_KV_LORA:], cos_t, sin_t)
        kpe_ref[rs, :] = kpe[:, :MLA_ROPE]
        kpe16_ref[rs, :] = kpe.astype(BF16)
        qf = _dot(cq.astype(BF16), wuq_ref[...])
        for h in range(MLA_HEADS):
            c0 = h * MLA_QPAD
            q_ref[rs, c0:c0 + MLA_NOPE] = (qf[:, c0:c0 + MLA_NOPE] * scale).astype(BF16)
            qr = _rope128(qf[:, c0 + MLA_NOPE:c0 + MLA_QPAD], cos_t, sin_t)
            q_ref[rs, c0 + MLA_NOPE:c0 + MLA_QPAD] = (qr * scale).astype(BF16)


def _odd_proj(x, cos_t, sin_t, w_in, gq, gkv, wuq, w_ukv, tm):
    rows = x.shape[0]
    period = cos_t.shape[0] // tm
    assert rows % tm == 0 and cos_t.shape[0] % tm == 0
    outs = [(MLA_HEADS * MLA_QPAD, BF16), (MLA_KV_LORA, F32), (MLA_ROPE, F32), (LANES, BF16),
            (MLA_HEADS * MLA_NOPE, BF16), (MLA_HEADS * MLA_V, BF16)]
    full = [w_in, gq, gkv, wuq, w_ukv]
    table = pl.BlockSpec((tm, LANES), lambda i: (i % period, 0))
    return pl.pallas_call(
        _odd_proj_kernel, name="odd_proj", grid=(rows // tm,),
        in_specs=[pl.BlockSpec((tm, D_MODEL), lambda i: (i, 0)), table, table] + [_full_spec(a) for a in full],
        out_specs=[pl.BlockSpec((tm, c), lambda i: (i, 0)) for c, _ in outs],
        out_shape=[jax.ShapeDtypeStruct((rows, c), dt) for c, dt in outs],
        compiler_params=_params(("parallel",)))(x, cos_t, sin_t, *full)


def _mla_absorbed_kernel(q_ref, cp_ref, rp_ref, cn_ref, rn_ref, wuk_ref, wuv_ref, o_ref, m_scr, acc_scr, *, tkp):
    tq = q_ref.shape[0]
    rows = MLA_HEADS * tq
    q = q_ref[...]
    qa = []
    for h in range(MLA_HEADS):
        c0 = h * MLA_QPAD
        q_abs = _dot_nt(q[:, c0:c0 + MLA_NOPE], wuk_ref[h])
        qa.append(jnp.concatenate([q_abs.astype(BF16), q[:, c0 + MLA_NOPE:c0 + MLA_NOPE + MLA_ROPE]], axis=1))
    qs = jnp.concatenate(qa, axis=0)
    m_scr[...] = jnp.full(m_scr.shape, NEG, F32)
    acc_scr[...] = jnp.zeros(acc_scr.shape, F32)

    def update(c, r):
        c = c.astype(BF16)
        s = _dot_nt(qs, jnp.concatenate([c, r.astype(BF16)], axis=1))
        m_prev = m_scr[...]
        m_new = jnp.maximum(m_prev, jnp.max(s, axis=1, keepdims=True))
        if s.shape[1] % LANES == 0:
            p = jnp.exp2(s - jnp.tile(m_new, (1, s.shape[1] // LANES)))
        else:
            p = jnp.exp2(s - m_new[:, :1])
        c1 = jnp.concatenate([c, jnp.ones((c.shape[0], LANES), BF16)], axis=1)
        acc_scr[...] = (jnp.tile(jnp.exp2(m_prev - m_new), (1, acc_scr.shape[1] // LANES)) * acc_scr[...]
                        + _dot(p.astype(BF16), c1))
        m_scr[...] = m_new

    def past_body(j, carry):
        rs = pl.ds(pl.multiple_of(j * tkp, tkp), tkp)
        update(cp_ref[rs, :], rp_ref[rs, :])
        return carry
    lax.fori_loop(0, cp_ref.shape[0] // tkp, past_body, 0)
    update(cn_ref[...], rn_ref[:, :MLA_ROPE])

    acc = acc_scr[...]
    lat = (acc[:, :MLA_KV_LORA] / jnp.tile(acc[:, MLA_KV_LORA:], (1, MLA_KV_LORA // LANES))).astype(BF16)
    for h in range(MLA_HEADS):
        o_ref[:, h * MLA_V:(h + 1) * MLA_V] = _dot(lat[h * tq:(h + 1) * tq, :],
                                                   wuv_ref[:, h * MLA_V:(h + 1) * MLA_V]).astype(o_ref.dtype)


def _mla_absorbed(q, ckv_past, kpe_past, ckv_new, kpe_new, wuk_t, wuv, n_seq, tq, tkp):
    p = ckv_past.shape[1]
    assert p % tkp == 0
    rows = MLA_HEADS * tq
    return pl.pallas_call(
        functools.partial(_mla_absorbed_kernel, tkp=tkp), name="mla_absorbed", grid=(n_seq,),
        in_specs=[pl.BlockSpec((tq, MLA_HEADS * MLA_QPAD), lambda b: (b, 0)),
                  pl.BlockSpec((None, p, MLA_KV_LORA), lambda b: (b, 0, 0)),
                  pl.BlockSpec((None, p, MLA_ROPE), lambda b: (b, 0, 0)),
                  pl.BlockSpec((tq, MLA_KV_LORA), lambda b: (b, 0)),
                  pl.BlockSpec((tq, LANES), lambda b: (b, 0)),
                  _full_spec(wuk_t), _full_spec(wuv)],
        out_specs=pl.BlockSpec((tq, MLA_HEADS * MLA_V), lambda b: (b, 0)),
        out_shape=jax.ShapeDtypeStruct((n_seq * tq, MLA_HEADS * MLA_V), BF16),
        scratch_shapes=[pltpu.VMEM((rows, LANES), F32), pltpu.VMEM((rows, MLA_KV_LORA + LANES), F32)],
        compiler_params=_params(("parallel",)))(q, ckv_past, kpe_past, ckv_new, kpe_new, wuk_t, wuv)


def _rope_tables(pos):
    half = MLA_ROPE // 2
    inv = ROPE_BASE ** (-jnp.arange(half, dtype=F32) / half)
    ang = pos.astype(F32)[:, None] * inv[None, :]
    cos, sin = jnp.cos(ang), jnp.sin(ang)
    z = jnp.zeros((pos.shape[0], LANES - MLA_ROPE), F32)
    return jnp.concatenate([cos, cos, z], axis=1), jnp.concatenate([-sin, sin, z], axis=1)


def _pad_rows(a, n):
    return jnp.pad(a, ((0, n - a.shape[0]),) + ((0, 0),) * (a.ndim - 1))


def kernel(x_prompt, x_sample, state_hgrn2, cache_fox_k, cache_fox_v, cache_fox_logf, cache_mla_ckv, cache_mla_kpe, meta_tokens, even_w_in, hg_lb_logits, hg_norm_g, fox_forget_bias, even_w_out, mla_w_in, mla_q_norm_g, mla_kv_norm_g, mla_w_uq, mla_w_uk, mla_w_uv, mla_w_out, ln_mix_g, ln_mix_b, ln_ffn_g, ln_ffn_b, router_w, router_bias, moe_w_gate, moe_w_up, moe_w_down):
    B, T, _ = x_prompt.shape
    Bs, Ts, _ = x_sample.shape
    P = cache_fox_k.shape[2]
    RM = B * T
    RS = Bs * Ts
    RSM = -(-(RS + N_META) // LANES) * LANES
    ME = slice(RS, RS + N_META)
    TM_MAIN, TM_MOE, TQ = TILE_PROJ, TILE_FFN, TILE_ATTN

    xm = x_prompt.reshape(RM, D_MODEL)
    xs = _pad_rows(jnp.concatenate([x_sample.reshape(RS, D_MODEL), meta_tokens.astype(F32)], axis=0), RSM)

    w_in0 = even_w_in[0]
    n_main = 7 * HG_W
    w_even = w_in0[:, :n_main].astype(BF16)
    w_even_f = jnp.pad(w_in0[:, n_main:], ((0, 0), (0, LANES - FOX_HEADS))).astype(BF16)
    fb_pad = jnp.pad(fox_forget_bias[0][None, :], ((0, 0), (0, LANES - FOX_HEADS)))
    g_hg = hg_norm_g[0].reshape(1, HG_W)
    w_out0 = even_w_out[0].astype(BF16)
    e_mat = ((jnp.arange(HG_SUB * HG_DK)[:, None] // HG_DK) == (jnp.arange(CHUNK)[None, :] % HG_SUB)).astype(BF16)

    w_odd = jnp.pad(mla_w_in[0], ((0, 0), (0, LANES - MLA_ROPE))).astype(BF16)
    gq = mla_q_norm_g[0][None, :]
    gkv = mla_kv_norm_g[0][None, :]
    wuq = mla_w_uq[0].reshape(MLA_Q_LORA, MLA_HEADS, MLA_NOPE + MLA_ROPE)
    wuq = jnp.pad(wuq, ((0, 0), (0, 0), (0, MLA_QPAD - MLA_NOPE - MLA_ROPE)))
    wuq = wuq.reshape(MLA_Q_LORA, MLA_HEADS * MLA_QPAD).astype(BF16)
    w_ukv = jnp.concatenate([mla_w_uk[0].reshape(MLA_KV_LORA, -1), mla_w_uv[0].reshape(MLA_KV_LORA, -1)],
                            axis=1).astype(BF16)
    w_out1 = mla_w_out[0].astype(BF16)

    rw = jnp.pad(router_w, ((0, 0), (0, LANES - N_EXPERTS))).astype(BF16)
    rb = jnp.pad(router_bias.astype(F32)[:, None], ((0, LANES - N_EXPERTS), (0, 0)))
    experts = (moe_w_gate, moe_w_up, moe_w_down)
    row2 = lambda a: a[None, :]

    def ffn(x, acts, w_out, l, tm_mix, tm_moe, routed):
        x1, x1b, route = _mix(x, acts, w_out, row2(ln_mix_g[l]), row2(ln_mix_b[l]), rw, rb, tm_mix)
        ln = (row2(ln_ffn_g[l]), row2(ln_ffn_b[l]))
        if routed:
            return _moe_routed(x1, x1b, route, *experts, l, *ln, tm_mix)
        return _moe(x1, route, *experts, l, *ln, tm_moe)

    ps = _even_proj(xs, w_even, w_even_f, hg_lb_logits, fb_pad, RSM, 0)
    ps = dict(zip(("hq", "lf", "hk", "hv", "hgate", "fq", "fk", "fv", "fk16", "fv16", "flf"), ps))
    pm = _even_proj_cache(xm, w_even, w_even_f, hg_lb_logits, fb_pad, ps["fk"][ME], ps["fv"][ME], TM_MAIN, 0, B, T)
    pm = dict(zip(("hq", "lf", "hk", "hv", "hgate", "fq", "fk16", "fv16", "flf", "fk_cache", "fv_cache"), pm))

    hg_keys = ("hq", "lf", "hk", "hv", "hgate", "flf")
    meta_in = [_pad_rows(ps[n][ME], CHUNK) for n in hg_keys]
    zero_s = jnp.zeros((1, HG_HEADS, HG_DK, HG_DV), F32)
    zero_f = jnp.zeros((1, 1, LANES), F32)
    o_hg_meta, fc_meta, s_meta = _hgrn2(*meta_in, g_hg, e_mat, zero_s, zero_f, 1, CHUNK, 0, CHUNK)
    o_hg_meta, fc_meta = o_hg_meta[:N_META], fc_meta[:N_META]
    f_meta_end = fc_meta[N_META - 1:N_META][None]

    o_hg_m, fc_m, s_main = _hgrn2(*[pm[n] for n in hg_keys], g_hg, e_mat, s_meta, f_meta_end, B, T, 0, TILE_HGRN2)

    logf_c = jnp.pad(jnp.transpose(cache_fox_logf[0], (0, 2, 1)), ((0, 0), (0, HG_SUB - FOX_HEADS), (0, 0)))
    fpast = _cumsum_lanes(logf_c.reshape(Bs * HG_SUB, P), TILE_CUMSUM).reshape(Bs, HG_SUB, P)[:, :FOX_HEADS, :]
    f0_s = jnp.pad(fpast[:, :, P - 1][:, None, :], ((0, 0), (0, 0), (0, LANES - FOX_HEADS)))
    o_hg_s, fc_s, s_samp = _hgrn2(*[ps[n] for n in hg_keys], g_hg, e_mat, state_hgrn2[0], f0_s, Bs, Ts, 0, CHUNK)

    def bias_layouts(fc, n_seq, seq_len):
        f4 = fc[:, :FOX_HEADS].T
        return f4[:, :, None], f4.reshape(FOX_HEADS, n_seq, 1, seq_len)

    fq_m, fk_m = bias_layouts(fc_m, B, T)
    fq_m = jnp.broadcast_to(fq_m, fq_m.shape[:2] + (LANES,))
    fq_s, fk_s = bias_layouts(fc_s, Bs, Ts)
    fq_t, fk_t = bias_layouts(fc_meta, 1, N_META)

    fox_kw = dict(n_heads=FOX_HEADS, dq=FOX_DH, dk=FOX_DH, dv=FOX_DH, mask_mode="causal")
    meta_past = dict(k=ps["fk16"][ME][None], v=ps["fv16"][ME][None],
                     fk=jnp.transpose(fk_t, (1, 0, 2, 3)), tk=N_META)
    o_fox_m = _flash_tri(pm["fq"], pm["fk16"], pm["fv16"], n_seq=B, seq_len=T, tq=TQ,
                         fq=fq_m, fkn=fk_m, past=meta_past, **fox_kw)
    o_fox_s = _fox_cached(ps["fq"], fq_s, cache_fox_k[0], cache_fox_v[0], fpast[:, :, None, :],
                          ps["fk16"], ps["fv16"], fk_s, Bs, Ts, TILE_CACHE)
    o_fox_t = _flash(ps["fq"][ME], ps["fk16"][ME], ps["fv16"][ME], n_seq=1, seq_len=N_META,
                     fq=fq_t, fkn=fk_t, **fox_kw)

    o_hg_small = _pad_rows(jnp.concatenate([o_hg_s, o_hg_meta], axis=0), RSM)
    o_fox_small = _pad_rows(jnp.concatenate([o_fox_s, o_fox_t], axis=0), RSM)
    xm = ffn(xm, [o_hg_m, o_fox_m], w_out0, 0, TM_MOE, TM_MOE, True)
    xs = ffn(xs, [o_hg_small, o_fox_small], w_out0, 0, RSM, RSM, False)

    cos_m, sin_m = _rope_tables(N_META + jnp.arange(T, dtype=jnp.int32))
    pos_small = _pad_rows(jnp.concatenate([jnp.tile(P + jnp.arange(Ts, dtype=jnp.int32), Bs),
                                           jnp.arange(N_META, dtype=jnp.int32)]), RSM)
    cos_s, sin_s = _rope_tables(pos_small)
    qm, ckv_m, kpe_m, kpe16_m, kn_m, vn_m = _odd_proj(xm, cos_m, sin_m, w_odd, gq, gkv, wuq, w_ukv, TM_MOE)
    qs, ckv_s, kpe_s, kpe16_s, kn_s, vn_s = _odd_proj(xs, cos_s, sin_s, w_odd, gq, gkv, wuq, w_ukv, RSM)
    wuk_t = jnp.transpose(mla_w_uk[0], (1, 0, 2)).astype(BF16)
    wuv = mla_w_uv[0].reshape(MLA_KV_LORA, MLA_HEADS * MLA_V).astype(BF16)

    mla_kw = dict(n_heads=MLA_HEADS, dq=MLA_QPAD, dk=MLA_NOPE, dv=MLA_V)
    meta_past = dict(k=kn_s[ME][None], v=vn_s[ME][None], r=kpe16_s[ME][None], tk=N_META)
    o_m = _flash_tri(qm, kn_m, vn_m, n_seq=B, seq_len=T, tq=TQ, rn=kpe16_m,
                     past=meta_past, mask_mode="chunk", **mla_kw)
    assert P % CHUNK == 0 and Ts <= CHUNK
    o_s = _mla_absorbed(qs, cache_mla_ckv[0], cache_mla_kpe[0], ckv_s, kpe16_s, wuk_t, wuv, Bs, Ts, TILE_CACHE)
    o_t = _flash(qs[ME], kn_s[ME], vn_s[ME], n_seq=1, seq_len=N_META, rn=kpe16_s[ME], mask_mode="full", **mla_kw)
    xm = ffn(xm, [o_m], w_out1, 1, TM_MOE, TM_MOE, True)
    xs = ffn(xs, [_pad_rows(jnp.concatenate([o_s, o_t], axis=0), RSM)], w_out1, 1, RSM, RSM, False)

    def with_meta(main, small, *width):
        meta = jnp.broadcast_to(small[ME][None], (B, N_META) + width)
        return jnp.concatenate([meta, main.reshape((B, T) + width)], axis=1)

    y_prompt = xm.reshape(B, T, D_MODEL)
    y_sample = xs[:RS].reshape(Bs, Ts, D_MODEL)
    hg_p = s_main[None]
    fk_p = pm["fk_cache"][None]
    fv_p = pm["fv_cache"][None]
    flf_p = with_meta(pm["flf"][:, :FOX_HEADS], ps["flf"][:, :FOX_HEADS], FOX_HEADS)[None]
    ckv_p = with_meta(ckv_m, ckv_s, MLA_KV_LORA)[None]
    kpe_p = with_meta(kpe_m, kpe_s, MLA_ROPE)[None]
    hg_s = s_samp[None]
    fk_s_out = ps["fk"][:RS].reshape(1, Bs, Ts, FOX_HEADS, FOX_DH)
    fv_s_out = ps["fv"][:RS].reshape(1, Bs, Ts, FOX_HEADS, FOX_DH)
    flf_s = ps["flf"][:RS, :FOX_HEADS].reshape(1, Bs, Ts, FOX_HEADS)
    ckv_so = ckv_s[:RS].reshape(1, Bs, Ts, MLA_KV_LORA)
    kpe_so = kpe_s[:RS].reshape(1, Bs, Ts, MLA_ROPE)
    return (y_prompt, y_sample, hg_p, fk_p, fv_p, flf_p, ckv_p, kpe_p,
            hg_s, fk_s_out, fv_s_out, flf_s, ckv_so, kpe_so)
```
